```python
import math, functools
import jax, jax.numpy as jnp
from jax import lax
import numpy as np


D_MODEL = 1024
BATCH = 16
SEQ = 2048
DEPTH = 4

GRID_W = 64
CTX_LEN = 256
MIX_W = D_MODEL
N_MIXERS = 4
BR_W = MIX_W // N_MIXERS
HEAD_DIM = 64
RET_HEADS = BR_W // HEAD_DIM
SG_HEADS = BR_W // HEAD_DIM
GDN_HEADS = BR_W // HEAD_DIM
RET_CHUNK = 128
SG_CHUNK = 128
GDN_CHUNK = 64
CONV_W = 3
ROPE_BASE = 10000.0
EPS = 1e-6
SEGMENTS = (BR_W,) * 15 + (GDN_HEADS,) * 4
IN_W = 15 * BR_W + 4 * GDN_HEADS

kernel_name = 'hybrid_parallel_mixer_dit_block'


def rms_norm(x, g):
    xf = x.astype(jnp.float32)
    y = xf * lax.rsqrt(jnp.mean(xf * xf, axis=-1, keepdims=True) + EPS)
    return (y * g.astype(jnp.float32)).astype(x.dtype)


def layer_norm_plain(x):
    xf = x.astype(jnp.float32)
    mu = jnp.mean(xf, axis=-1, keepdims=True)
    var = jnp.mean(jnp.square(xf - mu), axis=-1, keepdims=True)
    return ((xf - mu) * lax.rsqrt(var + EPS)).astype(x.dtype)


def l2_normalize(x):
    return x * lax.rsqrt(jnp.sum(x * x, axis=-1, keepdims=True) + EPS)


def dwconv_centred(x, w):
    k_w = w.shape[0]
    t = x.shape[1]
    xp = jnp.pad(x, ((0, 0), (k_w // 2, k_w // 2), (0, 0)))
    out = xp[:, 0:t] * w[0]
    for i in range(1, k_w):
        out = out + xp[:, i:i + t] * w[i]
    return out


def grid_rotary(x, row, col):
    half = x.shape[-1] // 2
    nf = half // 2
    inv = ROPE_BASE ** (-jnp.arange(nf, dtype=jnp.float32) / nf)

    def rot(xs, pos):
        ang = pos.astype(jnp.float32)[:, None] * inv
        cos, sin = jnp.cos(ang)[None, :, None, :], jnp.sin(ang)[None, :, None, :]
        x1, x2 = xs[..., :nf], xs[..., nf:]
        return jnp.concatenate([x1 * cos - x2 * sin, x1 * sin + x2 * cos], axis=-1)

    return jnp.concatenate([rot(x[..., :half], row), rot(x[..., half:], col)], axis=-1)


def split_proj(p):
    cuts = [int(i) for i in np.cumsum(SEGMENTS)[:-1]]
    return jnp.split(p, cuts, axis=-1)


def identity(a):
    return a


flip_time = functools.partial(jnp.flip, axis=1)


def retention_scan(q, k, v, log_gamma, s0, with_output):
    bsz, t, h, dk = k.shape
    n = t // RET_CHUNK

    def chunks(a):
        return a.reshape(bsz, n, RET_CHUNK, h, a.shape[-1]).transpose(1, 0, 3, 2, 4)

    pos = jnp.arange(RET_CHUNK, dtype=jnp.float32)
    lg = log_gamma[:, None]
    k_dec = jnp.exp((RET_CHUNK - 1.0 - pos) * lg)[..., None]
    c_dec = jnp.exp(RET_CHUNK * log_gamma)[:, None, None]
    kc, vc = chunks(k * dk ** -0.5), chunks(v)

    def update(s, ki, vi):
        return s * c_dec + jnp.einsum('bhcd,bhce->bhde', ki * k_dec, vi)

    if not with_output:
        s_fin, _ = lax.scan(lambda s, kv: (update(s, kv[0], kv[1]), None), s0, (kc, vc))
        return None, s_fin
    diff = pos[:, None] - pos[None, :]
    intra = jnp.exp(jnp.where(diff >= 0, diff * lg[..., None], -jnp.inf))
    q_dec = jnp.exp((pos + 1.0) * lg)[..., None]

    def step(s, inp):
        qi, ki, vi = inp
        scores = jnp.einsum('bhid,bhjd->bhij', qi, ki) * intra
        o = jnp.einsum('bhij,bhje->bhie', scores, vi) + jnp.einsum('bhid,bhde->bhie', qi, s) * q_dec
        return update(s, ki, vi), o

    s_fin, o = lax.scan(step, s0, (chunks(q), kc, vc))
    return o.transpose(1, 0, 3, 2, 4).reshape(bsz, t, h, -1), s_fin


def gated_delta_scan(q, k, v, g, beta, s0, with_output):
    bsz, t, h, dk = k.shape
    dv = v.shape[-1]
    n = t // GDN_CHUNK

    def chunks(a):
        return a.reshape(bsz, n, GDN_CHUNK, h, -1).transpose(1, 0, 3, 2, 4)

    kc, vc = chunks(k), chunks(v)
    bc = chunks(beta[..., None])
    gc = jnp.cumsum(chunks(g[..., None])[..., 0], axis=-1)
    idx = jnp.arange(GDN_CHUNK)
    diff = gc[..., :, None] - gc[..., None, :]
    decay = jnp.exp(jnp.where(idx[:, None] >= idx[None, :], diff, -jnp.inf))
    kb = kc * bc
    lower = jnp.where(idx[:, None] > idx[None, :],
                      jnp.einsum('nbhid,nbhjd->nbhij', kb, kc) * decay, 0.0)
    rhs = jnp.concatenate([vc * bc, kb * jnp.exp(gc)[..., None]], axis=-1)
    sol = lax.linalg.triangular_solve(lower, rhs, left_side=True, lower=True, unit_diagonal=True)
    uc, wc = sol[..., :dv], sol[..., dv:]
    g_last = gc[..., -1:]
    k_tail = kc * jnp.exp(g_last - gc)[..., None]
    c_dec = jnp.exp(g_last)[..., None]

    def update(s, ui, wi, kti, cdi):
        v_new = ui - jnp.einsum('bhck,bhkv->bhcv', wi, s)
        return s * cdi + jnp.einsum('bhck,bhcv->bhkv', kti, v_new), v_new

    if not with_output:
        s_fin, _ = lax.scan(lambda s, xs: (update(s, *xs)[0], None), s0, (uc, wc, k_tail, c_dec))
        return None, s_fin
    qc = chunks(q * dk ** -0.5)
    qd = qc * jnp.exp(gc)[..., None]

    def step(s, inp):
        qi, qdi, ki, di, ui, wi, kti, cdi = inp
        s_new, v_new = update(s, ui, wi, kti, cdi)
        intra = jnp.einsum('bhik,bhjk->bhij', qi, ki) * di
        o = jnp.einsum('bhck,bhkv->bhcv', qdi, s) + jnp.einsum('bhij,bhjv->bhiv', intra, v_new)
        return s_new, o

    s_fin, o = lax.scan(step, s0, (qc, qd, kc, decay, uc, wc, k_tail, c_dec))
    return o.transpose(1, 0, 3, 2, 4).reshape(bsz, t, h, dv), s_fin


def retention_mixer(lat, ctx, row, col, norm_g, ctx_out):
    def heads(a):
        return a.astype(jnp.float32).reshape(a.shape[0], a.shape[1], RET_HEADS, HEAD_DIM)

    ql, kl, vl = grid_rotary(heads(lat[0]), row, col), grid_rotary(heads(lat[1]), row, col), heads(lat[2])
    qc, kc, vc = heads(ctx[0]), heads(ctx[1]), heads(ctx[2])
    log_gamma = jnp.log(1.0 - 2.0 ** (-5.0 - jnp.arange(RET_HEADS, dtype=jnp.float32)))
    s0 = jnp.zeros((ql.shape[0], RET_HEADS, HEAD_DIM, HEAD_DIM), jnp.float32)
    o_lat, o_ctx = 0.0, 0.0
    for d in (identity, flip_time):
        oc, sc = retention_scan(d(qc), d(kc), d(vc), log_gamma, s0, ctx_out)
        ol, _ = retention_scan(d(ql), d(kl), d(vl), log_gamma, sc, True)
        o_lat = o_lat + d(ol)
        if ctx_out:
            o_ctx = o_ctx + d(oc)

    def finish(o, z):
        mu = jnp.mean(o, axis=-1, keepdims=True)
        var = jnp.mean(jnp.square(o - mu), axis=-1, keepdims=True)
        y = ((o - mu) * lax.rsqrt(var + EPS)).reshape(o.shape[0], o.shape[1], BR_W) * norm_g
        return (y * jax.nn.silu(z.astype(jnp.float32))).astype(z.dtype)

    return finish(o_lat, lat[3]), (finish(o_ctx, ctx[3]) if ctx_out else None)


def spatial_gating(u, v, z, w_s, b_s):
    bsz, t, _ = u.shape
    n = t // SG_CHUNK
    u = jax.nn.gelu(u)
    v = layer_norm_plain(jax.nn.gelu(v)).reshape(bsz, n, SG_CHUNK, SG_HEADS, HEAD_DIM)
    s = jnp.einsum('hij,bnjhd->bnihd', w_s, v) + b_s.T[:, :, None]
    return u * s.reshape(bsz, t, BR_W) * jax.nn.silu(z)


def short_conv(b, c, h, z, w):
    return b * dwconv_centred(c * h, w) * jax.nn.silu(z)


def gdn_mixer(lat, ctx, conv_w, a_log, dt_bias, norm_g, ctx_out):
    def prep(parts):
        q, k, v, _, a_f, a_b, b_f, b_b = parts
        qkv = jax.nn.silu(dwconv_centred(jnp.concatenate([q, k, v], axis=-1), conv_w)).astype(jnp.float32)
        bsz, t, _ = qkv.shape
        q, k, v = [a.reshape(bsz, t, GDN_HEADS, HEAD_DIM) for a in jnp.split(qkv, 3, axis=-1)]
        g = [-jnp.exp(a_log[i]) * jax.nn.softplus(a.astype(jnp.float32) + dt_bias[i])
             for i, a in enumerate((a_f, a_b))]
        beta = [jax.nn.sigmoid(b.astype(jnp.float32)) for b in (b_f, b_b)]
        return l2_normalize(q), l2_normalize(k), v, g, beta

    ql, kl, vl, gl, bl = prep(lat)
    qc, kc, vc, gcx, bcx = prep(ctx)
    s0 = jnp.zeros((ql.shape[0], GDN_HEADS, HEAD_DIM, HEAD_DIM), jnp.float32)
    o_lat, o_ctx = 0.0, 0.0
    for i, d in enumerate((identity, flip_time)):
        oc, sc = gated_delta_scan(d(qc), d(kc), d(vc), d(gcx[i]), d(bcx[i]), s0, ctx_out)
        ol, _ = gated_delta_scan(d(ql), d(kl), d(vl), d(gl[i]), d(bl[i]), sc, True)
        o_lat = o_lat + d(ol)
        if ctx_out:
            o_ctx = o_ctx + d(oc)

    def finish(o, z):
        y = o * lax.rsqrt(jnp.mean(o * o, axis=-1, keepdims=True) + EPS) * norm_g
        y = y.reshape(o.shape[0], o.shape[1], BR_W)
        return (y * jax.nn.silu(z.astype(jnp.float32))).astype(z.dtype)

    return finish(o_lat, lat[3]), (finish(o_ctx, ctx[3]) if ctx_out else None)


def token_mixers(p_lat, p_ctx, row, col, ret_norm_g, sg_w, sg_b, sc_conv_w,
                 gdn_conv_w, gdn_a_log, gdn_dt_bias, gdn_norm_g, ctx_out):
    lat, cx = split_proj(p_lat), split_proj(p_ctx)
    a_lat, a_ctx = retention_mixer(lat[0:4], cx[0:4], row, col, ret_norm_g, ctx_out)
    d_lat, d_ctx = gdn_mixer(lat[11:19], cx[11:19], gdn_conv_w, gdn_a_log, gdn_dt_bias, gdn_norm_g, ctx_out)
    y_lat = jnp.concatenate([a_lat, spatial_gating(*lat[4:7], sg_w, sg_b),
                             short_conv(*lat[7:11], sc_conv_w), d_lat], axis=-1).astype(p_lat.dtype)
    if not ctx_out:
        return y_lat, None
    y_ctx = jnp.concatenate([a_ctx, spatial_gating(*cx[4:7], sg_w, sg_b),
                             short_conv(*cx[7:11], sc_conv_w), d_ctx], axis=-1).astype(p_ctx.dtype)
    return y_lat, y_ctx


def _fwd_setup_inputs(seed: int = 0) -> dict:
    key = jax.random.key(seed)
    ks = jax.random.split(key, 18)

    def nrm(k, shape, s):
        return jax.random.normal(k, shape, jnp.float32) * s

    dt = jnp.exp(jax.random.uniform(ks[16], (DEPTH, 2, GDN_HEADS), jnp.float32,
                                    minval=math.log(1e-3), maxval=math.log(1e-1)))
    return {
        'x': nrm(ks[0], (BATCH, SEQ, D_MODEL), 1.0),
        'c': nrm(ks[1], (BATCH, D_MODEL), 1.0),
        'ctx': nrm(ks[2], (BATCH, CTX_LEN, D_MODEL), 1.0),
        'c_ctx': nrm(ks[3], (D_MODEL,), 1.0),
        'w_mod': nrm(ks[4], (DEPTH, D_MODEL, 3 * D_MODEL), 0.5 * D_MODEL ** -0.5),
        'b_mod': nrm(ks[5], (DEPTH, 3 * D_MODEL), 0.02),
        'g_pre': 1.0 + nrm(ks[6], (DEPTH, D_MODEL), 0.02),
        'g_post': 1.0 + nrm(ks[7], (DEPTH, D_MODEL), 0.02),
        'w_in': nrm(ks[8], (DEPTH, D_MODEL, IN_W), D_MODEL ** -0.5),
        'w_out': nrm(ks[9], (DEPTH, MIX_W, D_MODEL), MIX_W ** -0.5),
        'ret_norm_g': 1.0 + nrm(ks[10], (DEPTH, BR_W), 0.02),
        'sg_w': nrm(ks[11], (DEPTH, SG_HEADS, SG_CHUNK, SG_CHUNK), SG_CHUNK ** -0.5),
        'sg_b': 1.0 + nrm(ks[12], (DEPTH, SG_HEADS, SG_CHUNK), 0.1),
        'sc_conv_w': nrm(ks[13], (DEPTH, CONV_W, BR_W), CONV_W ** -0.5),
        'gdn_conv_w': nrm(ks[14], (DEPTH, CONV_W, 3 * BR_W), CONV_W ** -0.5),
        'gdn_a_log': jnp.log(jax.random.uniform(ks[15], (DEPTH, 2, GDN_HEADS), jnp.float32, minval=1.0, maxval=16.0)),
        'gdn_dt_bias': dt + jnp.log(-jnp.expm1(-dt)),
        'gdn_norm_g': 1.0 + nrm(ks[17], (DEPTH, HEAD_DIM), 0.02),
    }


def _fwd_reference(x, c, ctx, c_ctx, w_mod, b_mod, g_pre, g_post, w_in, w_out, ret_norm_g, sg_w, sg_b,
              sc_conv_w, gdn_conv_w, gdn_a_log, gdn_dt_bias, gdn_norm_g):
    rows = x.shape[1] // GRID_W
    row = jnp.repeat(jnp.arange(rows), GRID_W)
    col = jnp.tile(jnp.arange(GRID_W), rows)
    silu_c = jax.nn.silu(c)
    silu_cc = jax.nn.silu(c_ctx)
    for l in range(DEPTH):
        ctx_out = l < DEPTH - 1
        shift, scale, gate = jnp.split(silu_c @ w_mod[l] + b_mod[l], 3, axis=-1)
        shift_c, scale_c, gate_c = jnp.split(silu_cc @ w_mod[l] + b_mod[l], 3, axis=-1)
        h = rms_norm(x, g_pre[l]) * (1.0 + scale[:, None]) + shift[:, None]
        hc = rms_norm(ctx, g_pre[l]) * (1.0 + scale_c) + shift_c
        y, yc = token_mixers(h @ w_in[l], hc @ w_in[l], row, col, ret_norm_g[l], sg_w[l], sg_b[l],
                             sc_conv_w[l], gdn_conv_w[l], gdn_a_log[l], gdn_dt_bias[l], gdn_norm_g[l], ctx_out)
        x = x + gate[:, None] * rms_norm(y @ w_out[l], g_post[l])
        if ctx_out:
            ctx = ctx + gate_c * rms_norm(yc @ w_out[l], g_post[l])
    return x


import jax as _jax
import jax.numpy as _jnp

TWIN_FORMAT = 'train_step'
FWD_PARAMS = ['x', 'c', 'ctx', 'c_ctx', 'w_mod', 'b_mod', 'g_pre', 'g_post', 'w_in', 'w_out', 'ret_norm_g', 'sg_w', 'sg_b', 'sc_conv_w', 'gdn_conv_w', 'gdn_a_log', 'gdn_dt_bias', 'gdn_norm_g']
TWIN_WEIGHTS = ['c_ctx', 'w_mod', 'b_mod', 'g_pre', 'g_post', 'w_in', 'w_out', 'ret_norm_g', 'sg_w', 'sg_b', 'sc_conv_w', 'gdn_conv_w', 'gdn_a_log', 'gdn_dt_bias', 'gdn_norm_g']
TWIN_DIFF_INPUT = 'x'
TWIN_INPUTS = ['x', 'c', 'ctx', 'c_ctx', 'w_mod', 'b_mod', 'g_pre', 'g_post', 'w_in', 'w_out', 'ret_norm_g', 'sg_w', 'sg_b', 'sc_conv_w', 'gdn_conv_w', 'gdn_a_log', 'gdn_dt_bias', 'gdn_norm_g', 'loss_target', 'm_c_ctx', 'm_w_mod', 'm_b_mod', 'm_g_pre', 'm_g_post', 'm_w_in', 'm_w_out', 'm_ret_norm_g', 'm_sg_w', 'm_sg_b', 'm_sc_conv_w', 'm_gdn_conv_w', 'm_gdn_a_log', 'm_gdn_dt_bias', 'm_gdn_norm_g', 'v_c_ctx', 'v_w_mod', 'v_b_mod', 'v_g_pre', 'v_g_post', 'v_w_in', 'v_w_out', 'v_ret_norm_g', 'v_sg_w', 'v_sg_b', 'v_sc_conv_w', 'v_gdn_conv_w', 'v_gdn_a_log', 'v_gdn_dt_bias', 'v_gdn_norm_g']
TWIN_OUTPUTS = ['loss', 'grad_x', 'grad_c_ctx', 'grad_w_mod', 'grad_b_mod', 'grad_g_pre', 'grad_g_post', 'grad_w_in', 'grad_w_out', 'grad_ret_norm_g', 'grad_sg_w', 'grad_sg_b', 'grad_sc_conv_w', 'grad_gdn_conv_w', 'grad_gdn_a_log', 'grad_gdn_dt_bias', 'grad_gdn_norm_g', 'delta_c_ctx', 'delta_w_mod', 'delta_b_mod', 'delta_g_pre', 'delta_g_post', 'delta_w_in', 'delta_w_out', 'delta_ret_norm_g', 'delta_sg_w', 'delta_sg_b', 'delta_sc_conv_w', 'delta_gdn_conv_w', 'delta_gdn_a_log', 'delta_gdn_dt_bias', 'delta_gdn_norm_g', 'new_m_c_ctx', 'new_m_w_mod', 'new_m_b_mod', 'new_m_g_pre', 'new_m_g_post', 'new_m_w_in', 'new_m_w_out', 'new_m_ret_norm_g', 'new_m_sg_w', 'new_m_sg_b', 'new_m_sc_conv_w', 'new_m_gdn_conv_w', 'new_m_gdn_a_log', 'new_m_gdn_dt_bias', 'new_m_gdn_norm_g', 'new_v_c_ctx', 'new_v_w_mod', 'new_v_b_mod', 'new_v_g_pre', 'new_v_g_post', 'new_v_w_in', 'new_v_w_out', 'new_v_ret_norm_g', 'new_v_sg_w', 'new_v_sg_b', 'new_v_sc_conv_w', 'new_v_gdn_conv_w', 'new_v_gdn_a_log', 'new_v_gdn_dt_bias', 'new_v_gdn_norm_g']
TWIN_LEAF_KINDS = {'loss': 'loss', 'grad_x': 'grad_x', 'grad_c_ctx': 'grad_w', 'grad_w_mod': 'grad_w', 'grad_b_mod': 'grad_w', 'grad_g_pre': 'grad_w', 'grad_g_post': 'grad_w', 'grad_w_in': 'grad_w', 'grad_w_out': 'grad_w', 'grad_ret_norm_g': 'grad_w', 'grad_sg_w': 'grad_w', 'grad_sg_b': 'grad_w', 'grad_sc_conv_w': 'grad_w', 'grad_gdn_conv_w': 'grad_w', 'grad_gdn_a_log': 'grad_w', 'grad_gdn_dt_bias': 'grad_w', 'grad_gdn_norm_g': 'grad_w', 'delta_c_ctx': 'delta_w', 'delta_w_mod': 'delta_w', 'delta_b_mod': 'delta_w', 'delta_g_pre': 'delta_w', 'delta_g_post': 'delta_w', 'delta_w_in': 'delta_w', 'delta_w_out': 'delta_w', 'delta_ret_norm_g': 'delta_w', 'delta_sg_w': 'delta_w', 'delta_sg_b': 'delta_w', 'delta_sc_conv_w': 'delta_w', 'delta_gdn_conv_w': 'delta_w', 'delta_gdn_a_log': 'delta_w', 'delta_gdn_dt_bias': 'delta_w', 'delta_gdn_norm_g': 'delta_w', 'new_m_c_ctx': 'new_m', 'new_m_w_mod': 'new_m', 'new_m_b_mod': 'new_m', 'new_m_g_pre': 'new_m', 'new_m_g_post': 'new_m', 'new_m_w_in': 'new_m', 'new_m_w_out': 'new_m', 'new_m_ret_norm_g': 'new_m', 'new_m_sg_w': 'new_m', 'new_m_sg_b': 'new_m', 'new_m_sc_conv_w': 'new_m', 'new_m_gdn_conv_w': 'new_m', 'new_m_gdn_a_log': 'new_m', 'new_m_gdn_dt_bias': 'new_m', 'new_m_gdn_norm_g': 'new_m', 'new_v_c_ctx': 'new_v', 'new_v_w_mod': 'new_v', 'new_v_b_mod': 'new_v', 'new_v_g_pre': 'new_v', 'new_v_g_post': 'new_v', 'new_v_w_in': 'new_v', 'new_v_w_out': 'new_v', 'new_v_ret_norm_g': 'new_v', 'new_v_sg_w': 'new_v', 'new_v_sg_b': 'new_v', 'new_v_sc_conv_w': 'new_v', 'new_v_gdn_conv_w': 'new_v', 'new_v_gdn_a_log': 'new_v', 'new_v_gdn_dt_bias': 'new_v', 'new_v_gdn_norm_g': 'new_v'}


def _forward(args):
    return _fwd_reference(*[args[k] for k in FWD_PARAMS])


def _output_shape():
    out = _jax.eval_shape(lambda: _forward(_fwd_setup_inputs(0)))
    return out.shape, out.dtype

N_MICROBATCH = 1
ADAM_LR = 0.001
ADAM_B1 = 0.9
ADAM_B2 = 0.999
ADAM_EPS = 1e-08
ADAM_WD = 0.01
ADAM_STEP = 10
PER_EXAMPLE_BATCH_AXIS = {'x': 0, 'c': 0, 'ctx': 0, 'loss_target': 0}
SHARED_INPUTS = []
_WEIGHT_DTYPES = {'c_ctx': _jnp.float32, 'w_mod': _jnp.float32, 'b_mod': _jnp.float32, 'g_pre': _jnp.float32, 'g_post': _jnp.float32, 'w_in': _jnp.float32, 'w_out': _jnp.float32, 'ret_norm_g': _jnp.float32, 'sg_w': _jnp.float32, 'sg_b': _jnp.float32, 'sc_conv_w': _jnp.float32, 'gdn_conv_w': _jnp.float32, 'gdn_a_log': _jnp.float32, 'gdn_dt_bias': _jnp.float32, 'gdn_norm_g': _jnp.float32}
MOMENT_SCALE = {'c_ctx': 6.099523e-02, 'w_mod': 1.670417e+00, 'b_mod': 3.155659e+00, 'g_pre': 2.047268e-01, 'g_post': 3.734847e+00, 'w_in': 1.135718e-01, 'w_out': 1.310111e-01, 'ret_norm_g': 1.243157e-01, 'sg_w': 4.820657e-02, 'sg_b': 4.863113e-02, 'sc_conv_w': 1.220990e-01, 'gdn_conv_w': 1.035188e-01, 'gdn_a_log': 1.931139e-01, 'gdn_dt_bias': 1.928374e-01, 'gdn_norm_g': 3.095813e-01}


def _to_microbatches(a, axis):
    t = _jnp.moveaxis(a, axis, 0)
    t = t.reshape((N_MICROBATCH, t.shape[0] // N_MICROBATCH) + t.shape[1:])
    return _jnp.moveaxis(t, 1, axis + 1)


def setup_inputs(seed: int = 0) -> dict:
    inp = _fwd_setup_inputs(seed)
    key = _jax.random.fold_in(_jax.random.key(seed), 7919)
    shape, _ = _output_shape()
    out = dict(inp)
    out["loss_target"] = _jax.random.normal(_jax.random.fold_in(key, 0), shape, _jnp.float32)
    for i, name in enumerate(TWIN_WEIGHTS):
        w = inp[name].astype(_jnp.float32)
        if MOMENT_SCALE is None:
            s = _jnp.sqrt(_jnp.mean(_jnp.square(w)) + 1e-30)
        else:
            s = MOMENT_SCALE[name]
        km, kv = _jax.random.split(_jax.random.fold_in(key, i + 1))
        out[name] = w
        out["m_" + name] = s * _jax.random.normal(km, w.shape, _jnp.float32)
        out["v_" + name] = (s * s) * _jax.random.uniform(kv, w.shape, _jnp.float32, 0.5, 1.5)
    if N_MICROBATCH > 1:
        for name, axis in PER_EXAMPLE_BATCH_AXIS.items():
            out[name] = _to_microbatches(out[name], axis)
    return {'x': out['x'], 'c': out['c'], 'ctx': out['ctx'], 'c_ctx': out['c_ctx'], 'w_mod': out['w_mod'], 'b_mod': out['b_mod'], 'g_pre': out['g_pre'], 'g_post': out['g_post'], 'w_in': out['w_in'], 'w_out': out['w_out'], 'ret_norm_g': out['ret_norm_g'], 'sg_w': out['sg_w'], 'sg_b': out['sg_b'], 'sc_conv_w': out['sc_conv_w'], 'gdn_conv_w': out['gdn_conv_w'], 'gdn_a_log': out['gdn_a_log'], 'gdn_dt_bias': out['gdn_dt_bias'], 'gdn_norm_g': out['gdn_norm_g'], 'loss_target': out['loss_target'], 'm_c_ctx': out['m_c_ctx'], 'm_w_mod': out['m_w_mod'], 'm_b_mod': out['m_b_mod'], 'm_g_pre': out['m_g_pre'], 'm_g_post': out['m_g_post'], 'm_w_in': out['m_w_in'], 'm_w_out': out['m_w_out'], 'm_ret_norm_g': out['m_ret_norm_g'], 'm_sg_w': out['m_sg_w'], 'm_sg_b': out['m_sg_b'], 'm_sc_conv_w': out['m_sc_conv_w'], 'm_gdn_conv_w': out['m_gdn_conv_w'], 'm_gdn_a_log': out['m_gdn_a_log'], 'm_gdn_dt_bias': out['m_gdn_dt_bias'], 'm_gdn_norm_g': out['m_gdn_norm_g'], 'v_c_ctx': out['v_c_ctx'], 'v_w_mod': out['v_w_mod'], 'v_b_mod': out['v_b_mod'], 'v_g_pre': out['v_g_pre'], 'v_g_post': out['v_g_post'], 'v_w_in': out['v_w_in'], 'v_w_out': out['v_w_out'], 'v_ret_norm_g': out['v_ret_norm_g'], 'v_sg_w': out['v_sg_w'], 'v_sg_b': out['v_sg_b'], 'v_sc_conv_w': out['v_sc_conv_w'], 'v_gdn_conv_w': out['v_gdn_conv_w'], 'v_gdn_a_log': out['v_gdn_a_log'], 'v_gdn_dt_bias': out['v_gdn_dt_bias'], 'v_gdn_norm_g': out['v_gdn_norm_g']}


def _loss(weights, diff, rest, loss_target):
    with _jax.named_scope("forward"):
        args = {**rest, TWIN_DIFF_INPUT: diff, **{k: w.astype(_WEIGHT_DTYPES[k]) for k, w in weights.items()}}
        y = _forward(args)
    with _jax.named_scope("loss_head"):
        err = _jnp.square(y.astype(_jnp.float32) - loss_target)
        return 0.5 * _jnp.sum(_jnp.mean(err, axis=-1)) if err.ndim else 0.5 * err


def _adamw(w, g, m, v):
    m = ADAM_B1 * m + (1.0 - ADAM_B1) * g
    v = ADAM_B2 * v + (1.0 - ADAM_B2) * _jnp.square(g)
    m_hat = m / (1.0 - ADAM_B1 ** ADAM_STEP)
    v_hat = v / (1.0 - ADAM_B2 ** ADAM_STEP)
    delta = -ADAM_LR * (m_hat / (_jnp.sqrt(v_hat) + ADAM_EPS) + ADAM_WD * w)
    return delta, m, v


def reference(x, c, ctx, c_ctx, w_mod, b_mod, g_pre, g_post, w_in, w_out, ret_norm_g, sg_w, sg_b, sc_conv_w, gdn_conv_w, gdn_a_log, gdn_dt_bias, gdn_norm_g, loss_target, m_c_ctx, m_w_mod, m_b_mod, m_g_pre, m_g_post, m_w_in, m_w_out, m_ret_norm_g, m_sg_w, m_sg_b, m_sc_conv_w, m_gdn_conv_w, m_gdn_a_log, m_gdn_dt_bias, m_gdn_norm_g, v_c_ctx, v_w_mod, v_b_mod, v_g_pre, v_g_post, v_w_in, v_w_out, v_ret_norm_g, v_sg_w, v_sg_b, v_sc_conv_w, v_gdn_conv_w, v_gdn_a_log, v_gdn_dt_bias, v_gdn_norm_g):
    given = dict(x=x, c=c, ctx=ctx, c_ctx=c_ctx, w_mod=w_mod, b_mod=b_mod, g_pre=g_pre, g_post=g_post, w_in=w_in, w_out=w_out, ret_norm_g=ret_norm_g, sg_w=sg_w, sg_b=sg_b, sc_conv_w=sc_conv_w, gdn_conv_w=gdn_conv_w, gdn_a_log=gdn_a_log, gdn_dt_bias=gdn_dt_bias, gdn_norm_g=gdn_norm_g, loss_target=loss_target, m_c_ctx=m_c_ctx, m_w_mod=m_w_mod, m_b_mod=m_b_mod, m_g_pre=m_g_pre, m_g_post=m_g_post, m_w_in=m_w_in, m_w_out=m_w_out, m_ret_norm_g=m_ret_norm_g, m_sg_w=m_sg_w, m_sg_b=m_sg_b, m_sc_conv_w=m_sc_conv_w, m_gdn_conv_w=m_gdn_conv_w, m_gdn_a_log=m_gdn_a_log, m_gdn_dt_bias=m_gdn_dt_bias, m_gdn_norm_g=m_gdn_norm_g, v_c_ctx=v_c_ctx, v_w_mod=v_w_mod, v_b_mod=v_b_mod, v_g_pre=v_g_pre, v_g_post=v_g_post, v_w_in=v_w_in, v_w_out=v_w_out, v_ret_norm_g=v_ret_norm_g, v_sg_w=v_sg_w, v_sg_b=v_sg_b, v_sc_conv_w=v_sc_conv_w, v_gdn_conv_w=v_gdn_conv_w, v_gdn_a_log=v_gdn_a_log, v_gdn_dt_bias=v_gdn_dt_bias, v_gdn_norm_g=v_gdn_norm_g)
    weights = {n: given[n] for n in TWIN_WEIGHTS}
    shared = {n: given[n] for n in SHARED_INPUTS}
    per_example = {n: given[n] for n in ['x', 'c', 'ctx']}
    grad_fn = _jax.value_and_grad(_loss, argnums=(0, 1))

    def one_microbatch(ex, loss_target):
        ex = dict(ex)
        diff = ex.pop(TWIN_DIFF_INPUT)
        return grad_fn(weights, diff, {**shared, **ex}, loss_target)

    if N_MICROBATCH == 1:
        loss, (grad_w, grad_x) = one_microbatch(per_example, given["loss_target"])
    else:
        def body(carry, xs):
            loss_sum, grad_sum = carry
            l_k, (gw_k, gx_k) = one_microbatch(xs[0], xs[1])
            with _jax.named_scope("update"):
                return (loss_sum + l_k, _jax.tree.map(_jnp.add, grad_sum, gw_k)), gx_k

        init = (_jnp.zeros((), _jnp.float32), _jax.tree.map(_jnp.zeros_like, weights))
        (loss, grad_w), grad_x = _jax.lax.scan(body, init, (per_example, given["loss_target"]))
    with _jax.named_scope("update"):
        delta_w, new_m, new_v = {}, {}, {}
        for n in TWIN_WEIGHTS:
            delta_w[n], new_m[n], new_v[n] = _adamw(weights[n], grad_w[n], given["m_" + n], given["v_" + n])
    return (loss, grad_x, *[grad_w[n] for n in TWIN_WEIGHTS], *[delta_w[n] for n in TWIN_WEIGHTS],
            *[new_m[n] for n in TWIN_WEIGHTS], *[new_v[n] for n in TWIN_WEIGHTS])
```

```python
import functools
import math

import jax
import jax.numpy as jnp
import numpy as np
from jax import lax
from jax.experimental import pallas as pl
from jax.experimental.pallas import tpu as pltpu

F32, BF16 = jnp.float32, jnp.bfloat16
HI = lax.Precision.HIGHEST

N_DEV = 8
D = 1024
DEPTH = 4
BRW = 256
HD = 64
NH = 4
LANES = 128
GRID_W = 64
ROPE_BASE = 10000.0
W_IN = 15 * BRW + 4 * NH
W_PAD = 31 * LANES
RC = 128
GC = 64
EPS = 1e-6
LOG_GAMMA = tuple(math.log(1.0 - 2.0 ** (-5.0 - h)) for h in range(NH))
ADAM_LR, ADAM_B1, ADAM_B2, ADAM_EPS, ADAM_WD, ADAM_STEP = 0.001, 0.9, 0.999, 1e-08, 0.01, 10
VMEM_LIMIT = 56 * 1024 * 1024

COL_RET, COL_SG, COL_SC, COL_GDN = 0, 4, 7, 11
COL_A128 = 30


def _params(sem):
    return pltpu.CompilerParams(dimension_semantics=sem, vmem_limit_bytes=VMEM_LIMIT)


def _bdot(a, b, ca, cb):
    return lax.dot_general(a.astype(BF16), b.astype(BF16), (((ca,), (cb,)), ((), ())),
                           preferred_element_type=F32)


@jax.custom_vjp
def mm(a, b):
    return _bdot(a, b, 1, 0)


mm.defvjp(lambda a, b: (_bdot(a, b, 1, 0), (a, b)),
          lambda r, g: (_bdot(g, r[1], 1, 1), _bdot(r[0], g, 0, 0)))


@jax.custom_vjp
def mm_nt(a, b):
    return _bdot(a, b, 1, 1)


mm_nt.defvjp(lambda a, b: (_bdot(a, b, 1, 1), (a, b)),
             lambda r, g: (_bdot(g, r[1], 1, 0), _bdot(g, r[0], 0, 0)))


@jax.custom_vjp
def mm_tn(a, b):
    return _bdot(a, b, 0, 0)


mm_tn.defvjp(lambda a, b: (_bdot(a, b, 0, 0), (a, b)),
             lambda r, g: (_bdot(r[1], g, 1, 1), _bdot(r[0], g, 1, 0)))


def _dotf(a, b):
    return jnp.dot(a, b, precision=HI, preferred_element_type=F32)


def _split(a):
    hi = a.astype(BF16)
    return hi, (a - hi.astype(F32)).astype(BF16)


def _dot3(a, b, ca=1, cb=0):
    ah, al = _split(a)
    bh, bl = _split(b)
    dn = (((ca,), (cb,)), ((), ()))
    dg = functools.partial(lax.dot_general, dimension_numbers=dn, preferred_element_type=F32)
    return dg(ah, bh) + (dg(ah, bl) + dg(al, bh))


def _iota(shape, dim):
    return lax.broadcasted_iota(jnp.int32, shape, dim)


def _head_mask(h, width=BRW):
    return (_iota((1, width), 1) // HD == h).astype(F32)


def _lane_by_head(vals, width=BRW, lane0=0):
    head = (_iota((1, width), 1) + lane0) // HD
    out = jnp.full((1, width), vals[NH - 1], F32)
    for h in range(NH - 2, -1, -1):
        out = jnp.where(head == h, vals[h], out)
    return out


def _block_diag(n, width):
    return (_iota((n, width), 0) // HD == _iota((n, width), 1) // HD).astype(F32)


def _head_sum(x):
    w = x.shape[1]
    return _dotf(x, _block_diag(w, w))


def _silu(x):
    return x * jax.nn.sigmoid(x)


def _stack_heads(x):
    return jnp.concatenate([x * _head_mask(h) for h in range(NH)], axis=0)


@jax.custom_vjp
def _rot_half(x):
    n = x.shape[1]
    first = (_iota(x.shape, 1) % 32) < 16
    return jnp.where(first, -pltpu.roll(x, n - 16, 1), pltpu.roll(x, 16, 1))


_rot_half.defvjp(lambda x: (_rot_half(x), None), lambda _, g: (-_rot_half(g),))


def _rotary(x, cos, sin):
    return x * cos + _rot_half(x) * sin


def _make_shifts(seq, t_ctx):
    def dn_raw(x):
        r = _iota(x.shape, 0)
        return jnp.where((r == 0) | (r == t_ctx), 0.0, pltpu.roll(x, 1, 0))

    def up_raw(x):
        r = _iota(x.shape, 0)
        return jnp.where((r == t_ctx - 1) | (r == seq - 1), 0.0, pltpu.roll(x, seq - 1, 0))

    @jax.custom_vjp
    def dn(x):
        return dn_raw(x)

    @jax.custom_vjp
    def up(x):
        return up_raw(x)

    dn.defvjp(lambda x: (dn_raw(x), None), lambda _, g: (up_raw(g),))
    up.defvjp(lambda x: (up_raw(x), None), lambda _, g: (dn_raw(g),))
    return dn, up


def _conv3(t, w0, w1, w2, dn, up):
    return dn(t) * w0 + t * w1 + up(t) * w2


def _acc(ref, val, first, at=()):
    idx = at + (Ellipsis,)

    @pl.when(first)
    def _():
        ref[idx] = val

    @pl.when(jnp.logical_not(first))
    def _():
        ref[idx] += val


def _mod_fwd(cvec8, wmod, bmod):
    depth = wmod.shape[0]

    def body(c_ref, w_ref, b_ref, o_ref):
        sc = _silu(c_ref[...])
        o_ref[0] = jnp.dot(sc.astype(BF16), w_ref[0], preferred_element_type=F32) + b_ref[0]

    return pl.pallas_call(
        body, grid=(depth, 3),
        in_specs=[pl.BlockSpec((8, D), lambda l, j: (0, 0)),
                  pl.BlockSpec((1, D, D), lambda l, j: (l, 0, j)),
                  pl.BlockSpec((1, 1, D), lambda l, j: (l, 0, j))],
        out_specs=pl.BlockSpec((1, 8, D), lambda l, j: (l, 0, j)),
        out_shape=jax.ShapeDtypeStruct((depth, 8, 3 * D), F32),
        compiler_params=_params(("arbitrary", "arbitrary")), name="mod_fwd")(cvec8, wmod, bmod)


def _mod_bwd(dmod, wmod, cvec8):
    depth = wmod.shape[0]

    def body(dm_ref, w_ref, c_ref, dc_ref, db_ref):
        l, j = pl.program_id(0), pl.program_id(1)
        dm = dm_ref[0]
        db_ref[0] = jnp.sum(dm, axis=0, keepdims=True)
        part = _bdot(dm, w_ref[0], 1, 1)
        _acc(dc_ref, part, (l == 0) & (j == 0))

        @pl.when((l == depth - 1) & (j == 2))
        def _():
            c = c_ref[...]
            s = jax.nn.sigmoid(c)
            dc_ref[...] = dc_ref[...] * (s * (1.0 + c * (1.0 - s)))

    return pl.pallas_call(
        body, grid=(depth, 3),
        in_specs=[pl.BlockSpec((1, 8, D), lambda l, j: (l, 0, j)),
                  pl.BlockSpec((1, D, D), lambda l, j: (l, 0, j)),
                  pl.BlockSpec((8, D), lambda l, j: (0, 0))],
        out_specs=[pl.BlockSpec((8, D), lambda l, j: (0, 0)),
                   pl.BlockSpec((1, 1, D), lambda l, j: (l, 0, j))],
        out_shape=[jax.ShapeDtypeStruct((8, D), F32), jax.ShapeDtypeStruct((depth, 1, 3 * D), F32)],
        compiler_params=_params(("arbitrary", "arbitrary")), name="mod_bwd")(dmod, wmod, cvec8)


def _wmod_grad(c_rows, dmod_cols):
    depth, rows, n = dmod_cols.shape

    def body(c_ref, dm_ref, o_ref):
        sc = _silu(c_ref[...])
        o_ref[0] = lax.dot_general(sc, dm_ref[0], (((0,), (0,)), ((), ())), precision=HI,
                                   preferred_element_type=F32)

    return pl.pallas_call(
        body, grid=(depth,),
        in_specs=[pl.BlockSpec((rows, D), lambda l: (0, 0)), pl.BlockSpec((1, rows, n), lambda l: (l, 0, 0))],
        out_specs=pl.BlockSpec((1, D, n), lambda l: (l, 0, 0)),
        out_shape=jax.ShapeDtypeStruct((depth, D, n), F32),
        compiler_params=_params(("arbitrary",)), name="wmod_grad")(c_rows, dmod_cols)


class _Lay:
    def __init__(self, batch, t_ctx, t_lat):
        self.b, self.t_ctx, self.t_lat = batch, t_ctx, t_lat
        self.s = t_ctx + t_lat
        self.tm = min(256, t_ctx)
        self.tpb = self.s // self.tm
        self.nct = t_ctx // self.tm
        self.ntiles = batch * self.tpb
        self.rows = batch * self.s

    def mod_row(self, i):
        return jnp.where(i % self.tpb < self.nct, 2, i // self.tpb)

    def group(self, i):
        return 2 * (i // self.tpb) + jnp.where(i % self.tpb < self.nct, 0, 1)

    def group_first(self, i):
        return (i % self.tpb == 0) | (i % self.tpb == self.nct)


def _norm_mod(x, g, shift, scale):
    r = lax.rsqrt(jnp.mean(x * x, axis=-1, keepdims=True) + EPS)
    return (x * r * g) * (1.0 + scale) + shift


def _inproj_fwd(lay, xc, mod3, gpre, w):
    tm = lay.tm

    def body(x_ref, sh_ref, sc_ref, g_ref, w_ref, p_ref, h_ref):
        h = _norm_mod(x_ref[...], g_ref[...], sh_ref[0, 0], sc_ref[0, 0]).astype(BF16)
        h_ref[...] = h
        p_ref[...] = jnp.dot(h, w_ref[...], preferred_element_type=F32)

    return pl.pallas_call(
        body, grid=(lay.ntiles,),
        in_specs=[pl.BlockSpec((tm, D), lambda i: (i, 0)),
                  pl.BlockSpec((1, 1, 1, D), lambda i: (0, lay.mod_row(i), 0, 0)),
                  pl.BlockSpec((1, 1, 1, D), lambda i: (1, lay.mod_row(i), 0, 0)),
                  pl.BlockSpec((1, D), lambda i: (0, 0)),
                  pl.BlockSpec((D, W_PAD), lambda i: (0, 0))],
        out_specs=[pl.BlockSpec((tm, W_PAD), lambda i: (i, 0)), pl.BlockSpec((tm, D), lambda i: (i, 0))],
        out_shape=[jax.ShapeDtypeStruct((lay.rows, W_PAD), F32), jax.ShapeDtypeStruct((lay.rows, D), BF16)],
        compiler_params=_params(("arbitrary",)), name="inproj_fwd")(xc, mod3, mod3, gpre, w)


def _inproj_bwd(lay, xc, mod3, gpre, wt, dxc, pieces):
    tm = lay.tm
    npc = len(pieces)
    offs = [off for _, off in pieces]

    def body(*refs):
        x_ref, sh_ref, sc_ref, g_ref, wt_ref, dx_in = refs[:6]
        dps = refs[6:6 + npc]
        dx_ref, dg_ref, dsh_ref, dsc_ref = refs[6 + npc:]
        i = pl.program_id(0)
        dh = None
        for dp_ref, off in zip(dps, offs):
            wd = dp_ref.shape[1]
            part = jnp.dot(dp_ref[...], wt_ref[off:off + wd, :], preferred_element_type=F32)
            dh = part if dh is None else dh + part
        _, vjp = jax.vjp(_norm_mod, x_ref[...], g_ref[...], sh_ref[0, 0], sc_ref[0, 0])
        dx, dg, dsh, dsc = vjp(dh)
        dx_ref[...] = dx_in[...] + dx
        _acc(dg_ref, dg, i == 0)
        first = lay.group_first(i)
        _acc(dsh_ref, dsh, first, at=(0,))
        _acc(dsc_ref, dsc, first, at=(0,))

    return pl.pallas_call(
        body, grid=(lay.ntiles,),
        in_specs=[pl.BlockSpec((tm, D), lambda i: (i, 0)),
                  pl.BlockSpec((1, 1, 1, D), lambda i: (0, lay.mod_row(i), 0, 0)),
                  pl.BlockSpec((1, 1, 1, D), lambda i: (1, lay.mod_row(i), 0, 0)),
                  pl.BlockSpec((1, D), lambda i: (0, 0)),
                  pl.BlockSpec((W_PAD, D), lambda i: (0, 0)),
                  pl.BlockSpec((tm, D), lambda i: (i, 0))]
        + [pl.BlockSpec((tm, dp.shape[1]), lambda i: (i, 0)) for dp, _ in pieces],
        out_specs=[pl.BlockSpec((tm, D), lambda i: (i, 0)),
                   pl.BlockSpec((1, D), lambda i: (0, 0)),
                   pl.BlockSpec((1, 1, D), lambda i: (lay.group(i), 0, 0)),
                   pl.BlockSpec((1, 1, D), lambda i: (lay.group(i), 0, 0))],
        out_shape=[jax.ShapeDtypeStruct((lay.rows, D), F32), jax.ShapeDtypeStruct((1, D), F32),
                   jax.ShapeDtypeStruct((2 * lay.b, 1, D), F32), jax.ShapeDtypeStruct((2 * lay.b, 1, D), F32)],
        compiler_params=_params(("arbitrary",)), name="inproj_bwd",
    )(xc, mod3, mod3, gpre, wt, dxc, *[dp for dp, _ in pieces])


def _weight_grad(lay, h, dp, name):
    wd = dp.shape[1]
    tn = 256 if wd % 256 == 0 else LANES
    tr = lay.tm

    def body(h_ref, dp_ref, o_ref):
        part = lax.dot_general(h_ref[...], dp_ref[...], (((0,), (0,)), ((), ())), preferred_element_type=F32)
        _acc(o_ref, part, pl.program_id(1) == 0)

    return pl.pallas_call(
        body, grid=(wd // tn, lay.rows // tr),
        in_specs=[pl.BlockSpec((tr, D), lambda j, i: (i, 0)), pl.BlockSpec((tr, tn), lambda j, i: (i, j))],
        out_specs=pl.BlockSpec((D, tn), lambda j, i: (0, j)),
        out_shape=jax.ShapeDtypeStruct((D, wd), F32),
        compiler_params=_params(("arbitrary", "arbitrary")), name=name)(h, dp)


def _outproj_fn(y0, y1, y2, y3, x, w0, w1, w2, w3, gpost, gate):
    o = mm(y0, w0) + mm(y1, w1) + mm(y2, w2) + mm(y3, w3)
    r = lax.rsqrt(jnp.mean(o * o, axis=-1, keepdims=True) + EPS)
    return x + gate * (o * r * gpost)


def _outproj_specs(lay):
    tm = lay.tm
    return ([pl.BlockSpec((tm, BRW), lambda i: (i, 0))] * 4
            + [pl.BlockSpec((tm, D), lambda i: (i, 0))]
            + [pl.BlockSpec((D, D), lambda i: (0, 0))]
            + [pl.BlockSpec((1, D), lambda i: (0, 0))]
            + [pl.BlockSpec((1, 1, 1, D), lambda i: (2, lay.mod_row(i), 0, 0))])


def _outproj_args(x_ref_or_none, ys, w_ref, g_ref, gt_ref):
    ws = [w_ref[BRW * k:BRW * (k + 1), :].astype(F32) for k in range(4)]
    return [y[...].astype(F32) for y in ys] + [x_ref_or_none[...]] + ws + [g_ref[...], gt_ref[0, 0]]


def _outproj_fwd(lay, ys, xc, wout, gpost, mod3):
    tm = lay.tm

    def body(y0, y1, y2, y3, x_ref, w_ref, g_ref, gt_ref, o_ref):
        o_ref[...] = _outproj_fn(*_outproj_args(x_ref, (y0, y1, y2, y3), w_ref, g_ref, gt_ref))

    return pl.pallas_call(
        body, grid=(lay.ntiles,), in_specs=_outproj_specs(lay),
        out_specs=pl.BlockSpec((tm, D), lambda i: (i, 0)),
        out_shape=jax.ShapeDtypeStruct((lay.rows, D), F32),
        compiler_params=_params(("arbitrary",)), name="outproj_fwd")(*ys, xc, wout, gpost, mod3)


def _outproj_bwd(lay, ys, xc, wout, gpost, mod3, dxc):
    tm = lay.tm

    def body(y0, y1, y2, y3, x_ref, w_ref, g_ref, gt_ref, dx_ref, d0, d1, d2, d3, dw_ref, dg_ref, dgt_ref):
        i = pl.program_id(0)
        args = _outproj_args(x_ref, (y0, y1, y2, y3), w_ref, g_ref, gt_ref)
        _, vjp = jax.vjp(_outproj_fn, *args)
        g = vjp(dx_ref[...])
        for k, d in enumerate((d0, d1, d2, d3)):
            d[...] = g[k]

        @pl.when(i == 0)
        def _():
            for k in range(4):
                dw_ref[BRW * k:BRW * (k + 1), :] = g[5 + k]

        @pl.when(i != 0)
        def _():
            for k in range(4):
                dw_ref[BRW * k:BRW * (k + 1), :] += g[5 + k]

        _acc(dg_ref, g[9], i == 0)
        _acc(dgt_ref, g[10], lay.group_first(i), at=(0,))

    return pl.pallas_call(
        body, grid=(lay.ntiles,),
        in_specs=_outproj_specs(lay) + [pl.BlockSpec((tm, D), lambda i: (i, 0))],
        out_specs=[pl.BlockSpec((tm, BRW), lambda i: (i, 0))] * 4
        + [pl.BlockSpec((D, D), lambda i: (0, 0)), pl.BlockSpec((1, D), lambda i: (0, 0)),
           pl.BlockSpec((1, 1, D), lambda i: (lay.group(i), 0, 0))],
        out_shape=[jax.ShapeDtypeStruct((lay.rows, BRW), F32)] * 4
        + [jax.ShapeDtypeStruct((D, D), F32), jax.ShapeDtypeStruct((1, D), F32),
           jax.ShapeDtypeStruct((2 * lay.b, 1, D), F32)],
        compiler_params=_params(("arbitrary",)), name="outproj_bwd")(*ys, xc, wout, gpost, mod3, dxc)


def _loss_kernel(lay, xc3, target):
    tm, nct = lay.tm, lay.nct

    def body(x_ref, t_ref, loss_ref, dx_ref):
        b, i = pl.program_id(0), pl.program_id(1)
        lat = i >= nct
        err = x_ref[0] - t_ref[0]
        dx_ref[0] = jnp.where(lat, err * (1.0 / D), 0.0)
        part = jnp.sum(jnp.sum(err * err, axis=1, keepdims=True), axis=0, keepdims=True) * (0.5 / D)
        part = jnp.broadcast_to(jnp.where(lat, part, 0.0), (8, LANES))
        _acc(loss_ref, part, (b == 0) & (i == 0))

    return pl.pallas_call(
        body, grid=(lay.b, lay.tpb),
        in_specs=[pl.BlockSpec((1, tm, D), lambda b, i: (b, i, 0)),
                  pl.BlockSpec((1, tm, D), lambda b, i: (b, jnp.maximum(i - nct, 0), 0))],
        out_specs=[pl.BlockSpec((8, LANES), lambda b, i: (0, 0)), pl.BlockSpec((1, tm, D), lambda b, i: (b, i, 0))],
        out_shape=[jax.ShapeDtypeStruct((8, LANES), F32), jax.ShapeDtypeStruct(xc3.shape, F32)],
        compiler_params=_params(("arbitrary", "arbitrary")), name="loss")(xc3, target)


def _chunk_orders(n_ctx, n_all):
    fwd = list(range(n_all))
    rev = list(range(n_ctx - 1, -1, -1)) + list(range(n_all - 1, n_ctx - 1, -1))
    return fwd, rev


def _ret_state_fn(k, v, cos, sin):
    kt = _rotary(k, cos, sin) * (HD ** -0.5)
    lg = _lane_by_head(LOG_GAMMA)
    j = _iota((RC, 1), 0).astype(F32)
    bd = _block_diag(BRW, BRW)
    af = mm_tn(kt * jnp.exp((RC - 1.0 - j) * lg), v) * bd
    ar = mm_tn(kt * jnp.exp(j * lg), v) * bd
    return af, ar


def _ret_out_fn(q, k, v, z, cos, sin, sf, sr, ng):
    qt = _rotary(q, cos, sin)
    kt = _rotary(k, cos, sin) * (HD ** -0.5)
    diff = (_iota((RC, RC), 0) - _iota((RC, RC), 1)).astype(F32)
    o = None
    for h in range(NH):
        m = _head_mask(h)
        sc = mm_nt(qt * m, kt)
        wgt = sc * jnp.exp(-jnp.abs(diff) * (-LOG_GAMMA[h])) * jnp.where(diff == 0, 2.0, 1.0)
        part = mm(wgt, v * m)
        o = part if o is None else o + part
    lg = _lane_by_head(LOG_GAMMA)
    i = _iota((RC, 1), 0).astype(F32)
    o = o + mm(qt, sf) * jnp.exp((i + 1.0) * lg) + mm(qt, sr) * jnp.exp((RC - i) * lg)
    mu = _head_sum(o) * (1.0 / HD)
    cen = o - mu
    var = _head_sum(cen * cen) * (1.0 / HD)
    return cen * lax.rsqrt(var + EPS) * ng * _silu(z)


def _ret_specs(lay, cols):
    return [pl.BlockSpec((1, RC, BRW), functools.partial(lambda b, i, c: (b, i, c), c=COL_RET + c)) for c in cols]


def _ret_state(lay, p3, cos, sin):
    nc = lay.s // RC

    def body(k_ref, v_ref, c_ref, s_ref, a_ref):
        af, ar = _ret_state_fn(k_ref[0], v_ref[0], c_ref[...], s_ref[...])
        a_ref[0, 0, 0] = af
        a_ref[0, 0, 1] = ar

    tab = pl.BlockSpec((RC, BRW), lambda b, i: (i, 0))
    return pl.pallas_call(
        body, grid=(lay.b, nc), in_specs=_ret_specs(lay, (1, 2)) + [tab, tab],
        out_specs=pl.BlockSpec((1, 1, 2, BRW, BRW), lambda b, i: (b, i, 0, 0, 0)),
        out_shape=jax.ShapeDtypeStruct((lay.b, nc, 2, BRW, BRW), F32),
        compiler_params=_params(("arbitrary", "arbitrary")), name="ret_state")(p3, p3, cos, sin)


def _ret_state_bwd(lay, p3, cos, sin, d_a, dpr):
    nc = lay.s // RC

    def body(k_ref, v_ref, c_ref, s_ref, da_ref, dpr_ref, o_ref):
        _, vjp = jax.vjp(lambda k, v: _ret_state_fn(k, v, c_ref[...], s_ref[...]), k_ref[0], v_ref[0])
        dk, dv = vjp((da_ref[0, 0, 0], da_ref[0, 0, 1]))
        o_ref[0, :, 0:BRW] = dpr_ref[0, :, 0:BRW].astype(BF16)
        o_ref[0, :, BRW:2 * BRW] = (dpr_ref[0, :, BRW:2 * BRW] + dk).astype(BF16)
        o_ref[0, :, 2 * BRW:3 * BRW] = (dpr_ref[0, :, 2 * BRW:3 * BRW] + dv).astype(BF16)
        o_ref[0, :, 3 * BRW:] = dpr_ref[0, :, 3 * BRW:].astype(BF16)

    tab = pl.BlockSpec((RC, BRW), lambda b, i: (i, 0))
    return pl.pallas_call(
        body, grid=(lay.b, nc),
        in_specs=_ret_specs(lay, (1, 2)) + [tab, tab,
                                            pl.BlockSpec((1, 1, 2, BRW, BRW), lambda b, i: (b, i, 0, 0, 0)),
                                            pl.BlockSpec((1, RC, 4 * BRW), lambda b, i: (b, i, 0))],
        out_specs=pl.BlockSpec((1, RC, 4 * BRW), lambda b, i: (b, i, 0)),
        out_shape=jax.ShapeDtypeStruct((lay.b, lay.s, 4 * BRW), BF16),
        compiler_params=_params(("arbitrary", "arbitrary")), name="ret_state_bwd")(p3, p3, cos, sin, d_a, dpr)


def _state_scan(lay, a, nc_ctx, transpose, name):
    b, nc = a.shape[0], a.shape[1]
    orders = _chunk_orders(nc_ctx, nc)

    def body(a_ref, o_ref):
        d, jh = pl.program_id(1), pl.program_id(2)
        head = (_iota((1, LANES), 1) + jh * LANES) // HD
        lg = jnp.full((1, LANES), LOG_GAMMA[NH - 1], F32)
        for h in range(NH - 2, -1, -1):
            lg = jnp.where(head == h, LOG_GAMMA[h], lg)
        dec = jnp.exp(RC * lg)
        for dd in (0, 1):
            @pl.when(d == dd)
            def _(order=orders[dd]):
                acc = jnp.zeros((BRW, LANES), F32)
                if not transpose:
                    for c in order:
                        o_ref[0, c, 0] = acc
                        acc = acc * dec + a_ref[0, c, 0]
                else:
                    for c in reversed(order):
                        o_ref[0, c, 0] = acc
                        acc = a_ref[0, c, 0] + acc * dec

    spec = pl.BlockSpec((1, nc, 1, BRW, LANES), lambda bb, d, jh: (bb, 0, d, 0, jh))
    return pl.pallas_call(
        body, grid=(b, 2, BRW // LANES), in_specs=[spec], out_specs=spec,
        out_shape=jax.ShapeDtypeStruct(a.shape, F32),
        compiler_params=_params(("arbitrary",) * 3), name=name)(a)


def _ret_out(lay, p3, cos, sin, states, ng):
    nc = lay.s // RC

    def body(q_ref, k_ref, v_ref, z_ref, c_ref, s_ref, st_ref, ng_ref, y_ref):
        y = _ret_out_fn(q_ref[0], k_ref[0], v_ref[0], z_ref[0], c_ref[...], s_ref[...],
                        st_ref[0, 0, 0], st_ref[0, 0, 1], ng_ref[...])
        y_ref[0] = y.astype(BF16)

    tab = pl.BlockSpec((RC, BRW), lambda b, i: (i, 0))
    return pl.pallas_call(
        body, grid=(lay.b, nc),
        in_specs=_ret_specs(lay, (0, 1, 2, 3)) + [tab, tab,
                                                  pl.BlockSpec((1, 1, 2, BRW, BRW), lambda b, i: (b, i, 0, 0, 0)),
                                                  pl.BlockSpec((1, BRW), lambda b, i: (0, 0))],
        out_specs=pl.BlockSpec((1, RC, BRW), lambda b, i: (b, i, 0)),
        out_shape=jax.ShapeDtypeStruct((lay.b, lay.s, BRW), BF16),
        compiler_params=_params(("arbitrary", "arbitrary")), name="ret_out")(p3, p3, p3, p3, cos, sin, states, ng)


def _ret_out_bwd(lay, p3, cos, sin, states, ng, dy):
    nc = lay.s // RC

    def body(q_ref, k_ref, v_ref, z_ref, c_ref, s_ref, st_ref, ng_ref, dy_ref, dp_ref, dst_ref, dng_ref):
        b, i = pl.program_id(0), pl.program_id(1)
        fn = lambda q, k, v, z, sf, sr, ng: _ret_out_fn(q, k, v, z, c_ref[...], s_ref[...], sf, sr, ng)
        _, vjp = jax.vjp(fn, q_ref[0], k_ref[0], v_ref[0], z_ref[0], st_ref[0, 0, 0], st_ref[0, 0, 1], ng_ref[...])
        dq, dk, dv, dz, dsf, dsr, dng = vjp(dy_ref[0])
        for n, g in enumerate((dq, dk, dv, dz)):
            dp_ref[0, :, BRW * n:BRW * (n + 1)] = g
        dst_ref[0, 0, 0] = dsf
        dst_ref[0, 0, 1] = dsr
        _acc(dng_ref, dng, (b == 0) & (i == 0))

    tab = pl.BlockSpec((RC, BRW), lambda b, i: (i, 0))
    st = pl.BlockSpec((1, 1, 2, BRW, BRW), lambda b, i: (b, i, 0, 0, 0))
    return pl.pallas_call(
        body, grid=(lay.b, nc),
        in_specs=_ret_specs(lay, (0, 1, 2, 3)) + [tab, tab, st, pl.BlockSpec((1, BRW), lambda b, i: (0, 0)),
                                                  pl.BlockSpec((1, RC, BRW), lambda b, i: (b, i, 0))],
        out_specs=[pl.BlockSpec((1, RC, 4 * BRW), lambda b, i: (b, i, 0)), st,
                   pl.BlockSpec((1, BRW), lambda b, i: (0, 0))],
        out_shape=[jax.ShapeDtypeStruct((lay.b, lay.s, 4 * BRW), F32),
                   jax.ShapeDtypeStruct(states.shape, F32), jax.ShapeDtypeStruct((1, BRW), F32)],
        compiler_params=_params(("arbitrary", "arbitrary")), name="ret_out_bwd",
    )(p3, p3, p3, p3, cos, sin, states, ng, dy)


def _sg_fn(u, v, z, w0, w1, w2, w3, b8):
    ug = jax.nn.gelu(u)
    vg = jax.nn.gelu(v)
    mu = jnp.mean(vg, axis=-1, keepdims=True)
    cen = vg - mu
    vn = cen * lax.rsqrt(jnp.mean(cen * cen, axis=-1, keepdims=True) + EPS)
    s = None
    for h, w in enumerate((w0, w1, w2, w3)):
        part = mm(w, vn * _head_mask(h))
        s = part if s is None else s + part
    expand = (_iota((8, BRW), 1) // HD == _iota((8, BRW), 0)).astype(F32)
    bias = lax.dot_general(b8, expand, (((0,), (0,)), ((), ())), precision=HI, preferred_element_type=F32)
    return ug * (s + bias) * _silu(z)


def _sg_specs():
    return ([pl.BlockSpec((1, RC, BRW), functools.partial(lambda b, i, c: (b, i, c), c=COL_SG + c)) for c in range(3)]
            + [pl.BlockSpec((NH, RC, RC), lambda b, i: (0, 0, 0)), pl.BlockSpec((8, RC), lambda b, i: (0, 0))])


def _sg_fwd(lay, p3, sgw, sgb8):
    nc = lay.s // RC

    def body(u_ref, v_ref, z_ref, w_ref, b_ref, y_ref):
        y = _sg_fn(u_ref[0], v_ref[0], z_ref[0], w_ref[0], w_ref[1], w_ref[2], w_ref[3], b_ref[...])
        y_ref[0] = y.astype(BF16)

    return pl.pallas_call(
        body, grid=(lay.b, nc), in_specs=_sg_specs(),
        out_specs=pl.BlockSpec((1, RC, BRW), lambda b, i: (b, i, 0)),
        out_shape=jax.ShapeDtypeStruct((lay.b, lay.s, BRW), BF16),
        compiler_params=_params(("arbitrary", "arbitrary")), name="sg_fwd")(p3, p3, p3, sgw, sgb8)


def _sg_bwd(lay, p3, sgw, sgb8, dy):
    nc = lay.s // RC

    def body(u_ref, v_ref, z_ref, w_ref, b_ref, dy_ref, dp_ref, dw_ref, db_ref):
        first = (pl.program_id(0) == 0) & (pl.program_id(1) == 0)
        _, vjp = jax.vjp(_sg_fn, u_ref[0], v_ref[0], z_ref[0], w_ref[0], w_ref[1], w_ref[2], w_ref[3], b_ref[...])
        g = vjp(dy_ref[0])
        for n in range(3):
            dp_ref[0, :, BRW * n:BRW * (n + 1)] = g[n].astype(BF16)
        for h in range(NH):
            _acc(dw_ref, g[3 + h], first, at=(h,))
        _acc(db_ref, g[7], first)

    return pl.pallas_call(
        body, grid=(lay.b, nc),
        in_specs=_sg_specs() + [pl.BlockSpec((1, RC, BRW), lambda b, i: (b, i, 0))],
        out_specs=[pl.BlockSpec((1, RC, 3 * BRW), lambda b, i: (b, i, 0)),
                   pl.BlockSpec((NH, RC, RC), lambda b, i: (0, 0, 0)), pl.BlockSpec((8, RC), lambda b, i: (0, 0))],
        out_shape=[jax.ShapeDtypeStruct((lay.b, lay.s, 3 * BRW), BF16),
                   jax.ShapeDtypeStruct((NH, RC, RC), F32), jax.ShapeDtypeStruct((8, RC), F32)],
        compiler_params=_params(("arbitrary", "arbitrary")), name="sg_bwd")(p3, p3, p3, sgw, sgb8, dy)


def _sc_specs(lay):
    first = COL_SC * BRW // LANES
    blk = [pl.BlockSpec((1, lay.s, LANES), functools.partial(lambda j, b, c: (b, 0, c + j), c=first + 2 * n))
           for n in range(4)]
    return blk + [pl.BlockSpec((8, LANES), lambda j, b: (0, j))]


def _sc_fwd(lay, p3, w8):
    dn, up = _make_shifts(lay.s, lay.t_ctx)

    def fn(b_, c_, h_, z_, w0, w1, w2):
        return b_ * _conv3(c_ * h_, w0, w1, w2, dn, up) * _silu(z_)

    def body(b_ref, c_ref, h_ref, z_ref, w_ref, y_ref):
        y = fn(b_ref[0], c_ref[0], h_ref[0], z_ref[0], w_ref[0:1, :], w_ref[1:2, :], w_ref[2:3, :])
        y_ref[0] = y.astype(BF16)

    return pl.pallas_call(
        body, grid=(BRW // LANES, lay.b), in_specs=_sc_specs(lay),
        out_specs=pl.BlockSpec((1, lay.s, LANES), lambda j, b: (b, 0, j)),
        out_shape=jax.ShapeDtypeStruct((lay.b, lay.s, BRW), BF16),
        compiler_params=_params(("arbitrary", "arbitrary")), name="sc_fwd")(p3, p3, p3, p3, w8)


def _sc_bwd(lay, p3, w8, dy):
    dn, up = _make_shifts(lay.s, lay.t_ctx)

    def fn(b_, c_, h_, z_, w0, w1, w2):
        return b_ * _conv3(c_ * h_, w0, w1, w2, dn, up) * _silu(z_)

    def body(b_ref, c_ref, h_ref, z_ref, w_ref, dy_ref, db_ref, dc_ref, dh_ref, dz_ref, dw_ref):
        _, vjp = jax.vjp(fn, b_ref[0], c_ref[0], h_ref[0], z_ref[0], w_ref[0:1, :], w_ref[1:2, :], w_ref[2:3, :])
        g = vjp(dy_ref[0])
        for ref, val in zip((db_ref, dc_ref, dh_ref, dz_ref), g[:4]):
            ref[0] = val.astype(BF16)
        dw = jnp.concatenate([g[4], g[5], g[6], jnp.zeros((5, LANES), F32)], axis=0)
        _acc(dw_ref, dw, pl.program_id(1) == 0)

    out = pl.BlockSpec((1, lay.s, LANES), lambda j, b: (b, 0, j))
    return pl.pallas_call(
        body, grid=(BRW // LANES, lay.b), in_specs=_sc_specs(lay) + [out],
        out_specs=[out] * 4 + [pl.BlockSpec((8, LANES), lambda j, b: (0, j))],
        out_shape=[jax.ShapeDtypeStruct((lay.b, lay.s, BRW), BF16)] * 4 + [jax.ShapeDtypeStruct((8, BRW), F32)],
        compiler_params=_params(("arbitrary", "arbitrary")), name="sc_bwd")(p3, p3, p3, p3, w8, dy)


def _gdn_conv_fn(x, w0, w1, w2, normed, dn, up):
    a = _silu(_conv3(x, w0, w1, w2, dn, up))
    nrm = a * lax.rsqrt(_head_sum(a * a) + EPS)
    return jnp.where(normed, nrm, a)


def _gdn_conv(lay, p3, w8):
    dn, up = _make_shifts(lay.s, lay.t_ctx)
    first = COL_GDN * BRW // LANES

    def body(x_ref, w_ref, o_ref):
        normed = pl.program_id(0) < 2 * BRW // LANES
        o_ref[0] = _gdn_conv_fn(x_ref[0], w_ref[0:1, :], w_ref[1:2, :], w_ref[2:3, :], normed, dn, up)

    return pl.pallas_call(
        body, grid=(3 * BRW // LANES, lay.b),
        in_specs=[pl.BlockSpec((1, lay.s, LANES), lambda j, b: (b, 0, first + j)),
                  pl.BlockSpec((8, LANES), lambda j, b: (0, j))],
        out_specs=pl.BlockSpec((1, lay.s, LANES), lambda j, b: (b, 0, j)),
        out_shape=jax.ShapeDtypeStruct((lay.b, lay.s, 3 * BRW), F32),
        compiler_params=_params(("arbitrary", "arbitrary")), name="gdn_conv")(p3, w8)


def _gdn_conv_bwd(lay, p3, w8, dqkv):
    dn, up = _make_shifts(lay.s, lay.t_ctx)
    first = COL_GDN * BRW // LANES

    def body(x_ref, w_ref, g_ref, dx_ref, dw_ref):
        normed = pl.program_id(0) < 2 * BRW // LANES
        fn = lambda x, w0, w1, w2: _gdn_conv_fn(x, w0, w1, w2, normed, dn, up)
        _, vjp = jax.vjp(fn, x_ref[0], w_ref[0:1, :], w_ref[1:2, :], w_ref[2:3, :])
        g = vjp(g_ref[0])
        dx_ref[0] = g[0].astype(BF16)
        dw = jnp.concatenate([g[1], g[2], g[3], jnp.zeros((5, LANES), F32)], axis=0)
        _acc(dw_ref, dw, pl.program_id(1) == 0)

    blk = pl.BlockSpec((1, lay.s, LANES), lambda j, b: (b, 0, j))
    return pl.pallas_call(
        body, grid=(3 * BRW // LANES, lay.b),
        in_specs=[pl.BlockSpec((1, lay.s, LANES), lambda j, b: (b, 0, first + j)),
                  pl.BlockSpec((8, LANES), lambda j, b: (0, j)), blk],
        out_specs=[blk, pl.BlockSpec((8, LANES), lambda j, b: (0, j))],
        out_shape=[jax.ShapeDtypeStruct((lay.b, lay.s, 3 * BRW), BF16), jax.ShapeDtypeStruct((8, 3 * BRW), F32)],
        compiler_params=_params(("arbitrary", "arbitrary")), name="gdn_conv_bwd")(p3, w8, dqkv)


def _neumann_inverse(low):
    n = low.shape[0]
    eye = (_iota((n, n), 0) == _iota((n, n), 1)).astype(F32)
    pw = -low
    t = eye + pw
    for _ in range(5):
        pw = _dot3(pw, pw)
        t = t + _dot3(t, pw)
    return t


@jax.custom_vjp
def _tri_solve(low, r1, r2):
    t = _neumann_inverse(low)
    return _dot3(t, r1), _dot3(t, r2)


def _tri_solve_fwd(low, r1, r2):
    t = _neumann_inverse(low)
    x1, x2 = _dot3(t, r1), _dot3(t, r2)
    return (x1, x2), (t, x1, x2)


def _tri_solve_bwd(res, g):
    t, x1, x2 = res
    d1 = _dot3(t, g[0], 0, 0)
    d2 = _dot3(t, g[1], 0, 0)
    dlow = -(_dot3(d1, x1, 1, 1) + _dot3(d2, x2, 1, 1))
    return dlow, d1, d2


_tri_solve.defvjp(_tri_solve_fwd, _tri_solve_bwd)

N_PACK = 5


def _gdn_prep_fn(qn, kn, vv, a, alog, dtb):
    col = _iota((1, LANES), 1)
    xx = a + dtb
    softplus = jnp.maximum(xx, 0.0) + jnp.log(1.0 + jnp.exp(-jnp.abs(xx)))
    g_small = jnp.where(col < 8, -jnp.exp(alog) * softplus, 0.0)
    beta_small = jax.nn.sigmoid(a)
    n = NH * GC
    r, c = _iota((n, n), 0), _iota((n, n), 1)
    same = (r // GC) == (c // GC)
    eye = r == c
    ri, ci = _iota((GC, GC), 0), _iota((GC, GC), 1)
    kst, qst, vst = _stack_heads(kn), _stack_heads(qn), _stack_heads(vv)
    gtot_small = jnp.sum(g_small, axis=0, keepdims=True)

    def column(mat, cc):
        return jnp.sum(jnp.where(col == cc, mat, 0.0), axis=1, keepdims=True)

    packs, cdecs = [], []
    for d in (0, 1):
        tri = (ri >= ci) if d == 0 else (ri <= ci)
        gc_small = _dotf(tri.astype(F32), g_small)
        gcol = jnp.concatenate([column(gc_small, 4 * d + h) for h in range(NH)], axis=0)
        bcol = jnp.concatenate([column(beta_small, 8 + 4 * d + h) for h in range(NH)], axis=0)
        gtot = [column(gtot_small, 4 * d + h) for h in range(NH)]
        gtot_col = jnp.concatenate([jnp.broadcast_to(t, (GC, 1)) for t in gtot], axis=0)
        grow = jnp.sum(jnp.where(eye, gcol, 0.0), axis=0, keepdims=True)
        incl = same & ((r >= c) if d == 0 else (r <= c))
        strict = same & ((r > c) if d == 0 else (r < c))
        decay = jnp.where(incl, jnp.exp(jnp.where(incl, gcol - grow, 0.0)), 0.0)
        kb = kst * bcol
        low = jnp.where(strict, mm_nt(kb, kst) * decay, 0.0)
        eg = jnp.exp(gcol)
        u, w = _tri_solve(low, vst * bcol, kb * eg)
        k_tail = kst * jnp.exp(gtot_col - gcol)
        qs = qst * (HD ** -0.5)
        intra = mm_nt(qs, kst) * decay
        head = _iota((1, BRW), 1) // HD
        cdec = jnp.zeros((1, BRW), F32)
        for h in range(NH):
            cdec = jnp.where(head == h, jnp.exp(gtot[h]), cdec)
        packs += [u, w, k_tail, qs * eg, intra]
        cdecs.append(cdec)
    return tuple(packs), tuple(cdecs)


def _gdn_prep_specs(lay):
    return ([pl.BlockSpec((1, GC, BRW), functools.partial(lambda b, i, c: (b, i, c), c=c)) for c in range(3)]
            + [pl.BlockSpec((1, GC, LANES), lambda b, i: (b, i, COL_A128)),
               pl.BlockSpec((8, LANES), lambda b, i: (0, 0))])


def _gdn_prep(lay, qkv, p3, prm):
    nc = lay.s // GC

    def body(q_ref, k_ref, v_ref, a_ref, prm_ref, pack_ref, cd_ref):
        pack, cd = _gdn_prep_fn(q_ref[0], k_ref[0], v_ref[0], a_ref[0], prm_ref[0:1, :], prm_ref[1:2, :])
        for d in (0, 1):
            for n in range(N_PACK):
                pack_ref[0, 0, d, n] = pack[N_PACK * d + n]
            cd_ref[0, 0, d] = cd[d]

    return pl.pallas_call(
        body, grid=(lay.b, nc), in_specs=_gdn_prep_specs(lay),
        out_specs=[pl.BlockSpec((1, 1, 2, N_PACK, BRW, BRW), lambda b, i: (b, i, 0, 0, 0, 0)),
                   pl.BlockSpec((1, 1, 2, 1, BRW), lambda b, i: (b, i, 0, 0, 0))],
        out_shape=[jax.ShapeDtypeStruct((lay.b, nc, 2, N_PACK, BRW, BRW), F32),
                   jax.ShapeDtypeStruct((lay.b, nc, 2, 1, BRW), F32)],
        compiler_params=_params(("arbitrary", "arbitrary")), name="gdn_prep")(qkv, qkv, qkv, p3, prm)


def _gdn_prep_bwd(lay, qkv, p3, prm, dpack, dcd):
    nc = lay.s // GC

    def body(q_ref, k_ref, v_ref, a_ref, prm_ref, dpack_ref, dcd_ref, dqkv_ref, da_ref, dprm_ref):
        first = (pl.program_id(0) == 0) & (pl.program_id(1) == 0)
        _, vjp = jax.vjp(_gdn_prep_fn, q_ref[0], k_ref[0], v_ref[0], a_ref[0], prm_ref[0:1, :], prm_ref[1:2, :])
        dpack = tuple(dpack_ref[0, 0, d, n] for d in (0, 1) for n in range(N_PACK))
        dq, dk, dv, da, dalog, ddtb = vjp((dpack, (dcd_ref[0, 0, 0], dcd_ref[0, 0, 1])))
        dqkv_ref[0, :, 0:BRW] = dq
        dqkv_ref[0, :, BRW:2 * BRW] = dk
        dqkv_ref[0, :, 2 * BRW:] = dv
        da_ref[0] = da.astype(BF16)
        _acc(dprm_ref, jnp.concatenate([dalog, ddtb, jnp.zeros((6, LANES), F32)], axis=0), first)

    return pl.pallas_call(
        body, grid=(lay.b, nc),
        in_specs=_gdn_prep_specs(lay)
        + [pl.BlockSpec((1, 1, 2, N_PACK, BRW, BRW), lambda b, i: (b, i, 0, 0, 0, 0)),
           pl.BlockSpec((1, 1, 2, 1, BRW), lambda b, i: (b, i, 0, 0, 0))],
        out_specs=[pl.BlockSpec((1, GC, 3 * BRW), lambda b, i: (b, i, 0)),
                   pl.BlockSpec((1, GC, LANES), lambda b, i: (b, i, 0)),
                   pl.BlockSpec((8, LANES), lambda b, i: (0, 0))],
        out_shape=[jax.ShapeDtypeStruct((lay.b, lay.s, 3 * BRW), F32),
                   jax.ShapeDtypeStruct((lay.b, lay.s, LANES), BF16), jax.ShapeDtypeStruct((8, LANES), F32)],
        compiler_params=_params(("arbitrary", "arbitrary")), name="gdn_prep_bwd",
    )(qkv, qkv, qkv, p3, prm, dpack, dcd)


def _gdn_step_fn(s, u, w, k_tail, qd, intra, cdec):
    v_new = u - mm(w, s)
    o = mm(qd, s) + mm(intra, v_new)
    return s * cdec + mm_tn(k_tail, v_new), o


def _order_index(nc_ctx, nc, d, step):
    rev = jnp.where(step < nc_ctx, nc_ctx - 1 - step, nc + nc_ctx - 1 - step)
    return jnp.where(d == 0, step, rev)


def _gdn_scan(lay, pack, cd):
    nc, nc_ctx = lay.s // GC, lay.t_ctx // GC
    chunk = functools.partial(_order_index, nc_ctx, nc)

    def body(pack_ref, cd_ref, o_ref, st_ref, s_scr):
        @pl.when(pl.program_id(2) == 0)
        def _():
            s_scr[...] = jnp.zeros_like(s_scr)

        s = s_scr[...]
        st_ref[0, 0, 0] = s
        s_new, o = _gdn_step_fn(s, *[pack_ref[0, 0, 0, n] for n in range(N_PACK)], cd_ref[0, 0, 0])
        o_ref[0, 0, 0] = o
        s_scr[...] = s_new

    blk = pl.BlockSpec((1, 1, 1, BRW, BRW), lambda b, d, t: (b, chunk(d, t), d, 0, 0))
    return pl.pallas_call(
        body, grid=(lay.b, 2, nc),
        in_specs=[pl.BlockSpec((1, 1, 1, N_PACK, BRW, BRW), lambda b, d, t: (b, chunk(d, t), d, 0, 0, 0)),
                  pl.BlockSpec((1, 1, 1, 1, BRW), lambda b, d, t: (b, chunk(d, t), d, 0, 0))],
        out_specs=[blk, blk],
        out_shape=[jax.ShapeDtypeStruct((lay.b, nc, 2, BRW, BRW), F32)] * 2,
        scratch_shapes=[pltpu.VMEM((BRW, BRW), F32)],
        compiler_params=_params(("arbitrary",) * 3), name="gdn_scan")(pack, cd)


def _gdn_scan_bwd(lay, pack, cd, states, do):
    nc, nc_ctx = lay.s // GC, lay.t_ctx // GC

    def chunk(d, t):
        return _order_index(nc_ctx, nc, d, nc - 1 - t)

    def body(pack_ref, cd_ref, st_ref, do_ref, dpack_ref, dcd_ref, ds_scr):
        @pl.when(pl.program_id(2) == 0)
        def _():
            ds_scr[...] = jnp.zeros_like(ds_scr)

        args = [st_ref[0, 0, 0]] + [pack_ref[0, 0, 0, n] for n in range(N_PACK)] + [cd_ref[0, 0, 0]]
        _, vjp = jax.vjp(_gdn_step_fn, *args)
        g = vjp((ds_scr[...], _stack_heads(do_ref[0])))
        ds_scr[...] = g[0]
        for n in range(N_PACK):
            dpack_ref[0, 0, 0, n] = g[1 + n]
        dcd_ref[0, 0, 0] = g[1 + N_PACK]

    pk = pl.BlockSpec((1, 1, 1, N_PACK, BRW, BRW), lambda b, d, t: (b, chunk(d, t), d, 0, 0, 0))
    cdb = pl.BlockSpec((1, 1, 1, 1, BRW), lambda b, d, t: (b, chunk(d, t), d, 0, 0))
    return pl.pallas_call(
        body, grid=(lay.b, 2, nc),
        in_specs=[pk, cdb, pl.BlockSpec((1, 1, 1, BRW, BRW), lambda b, d, t: (b, chunk(d, t), d, 0, 0)),
                  pl.BlockSpec((1, GC, BRW), lambda b, d, t: (b, chunk(d, t), 0))],
        out_specs=[pk, cdb],
        out_shape=[jax.ShapeDtypeStruct(pack.shape, F32), jax.ShapeDtypeStruct(cd.shape, F32)],
        scratch_shapes=[pltpu.VMEM((BRW, BRW), F32)],
        compiler_params=_params(("arbitrary",) * 3), name="gdn_scan_bwd")(pack, cd, states, do)


def _gdn_finish_fn(o, z, ng):
    return o * lax.rsqrt(_head_sum(o * o) * (1.0 / HD) + EPS) * ng * _silu(z)


def _gdn_unstack(o_ref):
    o = None
    for d in (0, 1):
        for h in range(NH):
            part = o_ref[0, 0, d, GC * h:GC * (h + 1), :]
            o = part if o is None else o + part
    return o


def _gdn_finish_specs():
    return [pl.BlockSpec((1, 1, 2, BRW, BRW), lambda b, i: (b, i, 0, 0, 0)),
            pl.BlockSpec((1, GC, BRW), lambda b, i: (b, i, COL_GDN + 3)),
            pl.BlockSpec((1, BRW), lambda b, i: (0, 0))]


def _gdn_finish(lay, o_st, p3, ng):
    nc = lay.s // GC

    def body(o_ref, z_ref, ng_ref, y_ref):
        y_ref[0] = _gdn_finish_fn(_gdn_unstack(o_ref), z_ref[0], ng_ref[...]).astype(BF16)

    return pl.pallas_call(
        body, grid=(lay.b, nc), in_specs=_gdn_finish_specs(),
        out_specs=pl.BlockSpec((1, GC, BRW), lambda b, i: (b, i, 0)),
        out_shape=jax.ShapeDtypeStruct((lay.b, lay.s, BRW), BF16),
        compiler_params=_params(("arbitrary", "arbitrary")), name="gdn_finish")(o_st, p3, ng)


def _gdn_finish_bwd(lay, o_st, p3, ng, dy):
    nc = lay.s // GC

    def body(o_ref, z_ref, ng_ref, dy_ref, do_ref, dz_ref, dng_ref):
        first = (pl.program_id(0) == 0) & (pl.program_id(1) == 0)
        _, vjp = jax.vjp(_gdn_finish_fn, _gdn_unstack(o_ref), z_ref[0], ng_ref[...])
        do, dz, dng = vjp(dy_ref[0])
        do_ref[0] = do
        dz_ref[0] = dz.astype(BF16)
        _acc(dng_ref, dng, first)

    blk = pl.BlockSpec((1, GC, BRW), lambda b, i: (b, i, 0))
    return pl.pallas_call(
        body, grid=(lay.b, nc), in_specs=_gdn_finish_specs() + [blk],
        out_specs=[blk, blk, pl.BlockSpec((1, BRW), lambda b, i: (0, 0))],
        out_shape=[jax.ShapeDtypeStruct((lay.b, lay.s, BRW), F32), jax.ShapeDtypeStruct((lay.b, lay.s, BRW), BF16),
                   jax.ShapeDtypeStruct((1, BRW), F32)],
        compiler_params=_params(("arbitrary", "arbitrary")), name="gdn_finish_bwd")(o_st, p3, ng, dy)


def _rope_tables(lay):
    t = jnp.arange(lay.t_lat)
    lane = np.arange(BRW)
    dim = lane % HD
    inv = jnp.asarray(ROPE_BASE ** (-(dim % 16).astype(np.float32) / 16.0), F32)
    pos = jnp.where((dim // 32 == 0)[None, :], (t // GRID_W)[:, None], (t % GRID_W)[:, None]).astype(F32)
    ang = pos * inv[None, :]
    cos = jnp.concatenate([jnp.ones((lay.t_ctx, BRW), F32), jnp.cos(ang)], axis=0)
    sin = jnp.concatenate([jnp.zeros((lay.t_ctx, BRW), F32), jnp.sin(ang)], axis=0)
    return cos, sin


def _pad_rows(a, rows):
    return jnp.concatenate([a, jnp.zeros((rows - a.shape[0],) + a.shape[1:], a.dtype)], axis=0)


def _layer_fwd(lay, xc, wl, cos, sin):
    p, h = _inproj_fwd(lay, xc, wl["mod3"], wl["gpre"], wl["win"])
    p3 = p.reshape(lay.b, lay.s, W_PAD)
    nctx = lay.t_ctx // RC
    states = _state_scan(lay, _ret_state(lay, p3, cos, sin), nctx, False, "ret_scan")
    y_ret = _ret_out(lay, p3, cos, sin, states, wl["ret_ng"])
    y_sg = _sg_fwd(lay, p3, wl["sgw"], wl["sgb8"])
    y_sc = _sc_fwd(lay, p3, wl["scw8"])
    qkv = _gdn_conv(lay, p3, wl["gdnw8"])
    pack, cd = _gdn_prep(lay, qkv, p3, wl["prm"])
    o_st, gstates = _gdn_scan(lay, pack, cd)
    y_gdn = _gdn_finish(lay, o_st, p3, wl["gdn_ng"])
    ys = [y.reshape(lay.rows, BRW) for y in (y_ret, y_sg, y_sc, y_gdn)]
    xc_new = _outproj_fwd(lay, ys, xc, wl["wout"], wl["gpost"], wl["mod3"])
    saved = dict(xc=xc, p3=p3, h=h, states=states, qkv=qkv, pack=pack, cd=cd, o_st=o_st, gstates=gstates, ys=ys)
    return xc_new, saved


def _layer_bwd(lay, sv, wl, cos, sin, dxc):
    p3 = sv["p3"]
    as3 = lambda a: a.reshape(lay.b, lay.s, a.shape[-1])
    as2 = lambda a: a.reshape(lay.rows, a.shape[-1])
    dy_ret, dy_sg, dy_sc, dy_gdn, dwout, dgpost, dgate = _outproj_bwd(
        lay, sv["ys"], sv["xc"], wl["wout"], wl["gpost"], wl["mod3"], dxc)
    nctx = lay.t_ctx // RC
    dpr, dstates, dret_ng = _ret_out_bwd(lay, p3, cos, sin, sv["states"], wl["ret_ng"], as3(dy_ret))
    d_a = _state_scan(lay, dstates, nctx, True, "ret_scan_bwd")
    dp_ret = _ret_state_bwd(lay, p3, cos, sin, d_a, dpr)
    dp_sg, dsgw, dsgb8 = _sg_bwd(lay, p3, wl["sgw"], wl["sgb8"], as3(dy_sg))
    dsb, dsc_, dsh_, dsz, dscw8 = _sc_bwd(lay, p3, wl["scw8"], as3(dy_sc))
    do, dgz, dgdn_ng = _gdn_finish_bwd(lay, sv["o_st"], p3, wl["gdn_ng"], as3(dy_gdn))
    dpack, dcd = _gdn_scan_bwd(lay, sv["pack"], sv["cd"], sv["gstates"], do)
    dqkv, da, dprm = _gdn_prep_bwd(lay, sv["qkv"], p3, wl["prm"], dpack, dcd)
    dp_gqkv, dgdnw8 = _gdn_conv_bwd(lay, p3, wl["gdnw8"], dqkv)
    pieces = [(as2(dp_ret), 0), (as2(dp_sg), COL_SG * BRW), (as2(dsb), COL_SC * BRW), (as2(dsc_), (COL_SC + 1) * BRW),
              (as2(dsh_), (COL_SC + 2) * BRW), (as2(dsz), (COL_SC + 3) * BRW), (as2(dp_gqkv), COL_GDN * BRW),
              (as2(dgz), (COL_GDN + 3) * BRW), (as2(da), COL_A128 * LANES)]
    dxc_prev, dgpre, dshift, dscale = _inproj_bwd(lay, sv["xc"], wl["mod3"], wl["gpre"], wl["wint"], dxc, pieces)
    dwin = jnp.concatenate([_weight_grad(lay, sv["h"], dp, "win_grad_%d" % off) for dp, off in pieces],
                           axis=1)[:, :W_IN]

    def rows3(g):
        return jnp.concatenate([g[1], g[3], g[0] + g[2]], axis=0)

    dmod = _pad_rows(jnp.concatenate([rows3(dshift), rows3(dscale), rows3(dgate)], axis=1), 8)
    grads = dict(win=dwin, wout=dwout, gpre=dgpre[0], gpost=dgpost[0], ret_ng=dret_ng[0], sgw=dsgw, sgb=dsgb8[:NH],
                 scw=dscw8[:3], gdnw=dgdnw8[:3], alog=dprm[0, :2 * NH].reshape(2, NH),
                 dtb=dprm[1, :2 * NH].reshape(2, NH), gdn_ng=dgdn_ng.reshape(NH, HD).sum(axis=0), dmod=dmod)
    return dxc_prev, grads


def _local_step(x, c, ctx, c_ctx, wmod, bmod, gpre, gpost, win, wout, ret_ng, sgw, sgb, scw, gdnw, alog, dtb,
                gdn_ng, target):
    depth = wmod.shape[0]
    lay = _Lay(x.shape[0], ctx.shape[1], x.shape[1])
    assert lay.b == 2 and lay.t_ctx % RC == 0 and lay.t_lat % RC == 0
    cos, sin = _rope_tables(lay)
    cvec8 = _pad_rows(jnp.concatenate([c, c_ctx[None]], axis=0), 8)
    mod = _mod_fwd(cvec8, wmod, bmod[:, None, :])
    wint = jnp.swapaxes(win, 1, 2)
    xc = jnp.concatenate([ctx, x], axis=1).reshape(lay.rows, D)
    layers, saved = [], []
    for l in range(depth):
        wl = dict(mod3=mod[l].reshape(8, 3, D).transpose(1, 0, 2)[:, :, None, :], gpre=gpre[l][None], gpost=gpost[l][None],
                  win=win[l], wint=wint[l], wout=wout[l], ret_ng=ret_ng[l][None], sgw=sgw[l],
                  sgb8=_pad_rows(sgb[l], 8), scw8=_pad_rows(scw[l], 8), gdnw8=_pad_rows(gdnw[l], 8),
                  prm=_pad_rows(jnp.pad(jnp.stack([alog[l].reshape(-1), dtb[l].reshape(-1)]),
                                        ((0, 0), (0, LANES - 2 * NH))), 8),
                  gdn_ng=jnp.tile(gdn_ng[l], NH)[None])
        xc, sv = _layer_fwd(lay, xc, wl, cos, sin)
        layers.append(wl)
        saved.append(sv)
    loss, dxc3 = _loss_kernel(lay, xc.reshape(lay.b, lay.s, D), target)
    dxc = dxc3.reshape(lay.rows, D)
    grads = [None] * depth
    for l in reversed(range(depth)):
        dxc, grads[l] = _layer_bwd(lay, saved[l], layers[l], cos, sin, dxc)
    stacked = {k: jnp.stack([g[k] for g in grads]) for k in grads[0]}
    dcvec8, dbmod = _mod_bwd(stacked["dmod"], wmod, cvec8)
    stacked["bmod"] = dbmod[:, 0, :]
    stacked["c_ctx"] = dcvec8[2]
    dx = dxc.reshape(lay.b, lay.s, D)[:, lay.t_ctx:, :]
    return loss, dx, stacked, cvec8


def _exchange(src, scatter, name):
    blk = src.shape[-2:]

    def body(src_ref, out_ref, send_sems, recv_sems, loc_sem):
        x, y, c = lax.axis_index("x"), lax.axis_index("y"), lax.axis_index("c")
        me = 4 * x + 2 * y + c

        def block(j):
            return src_ref.at[j] if scatter else src_ref

        def remote(k, src_blk, dst_blk, peer_xyz):
            return pltpu.make_async_remote_copy(
                src_ref=block(src_blk), dst_ref=out_ref.at[dst_blk], send_sem=send_sems.at[k], recv_sem=recv_sems.at[k],
                device_id=peer_xyz, device_id_type=pl.DeviceIdType.MESH)

        local = pltpu.make_async_copy(block(me), out_ref.at[me], loc_sem)
        local.start()
        peers = []
        for k in range(1, N_DEV):
            px = 1 - x if k & 4 else x
            py = 1 - y if k & 2 else y
            pc = 1 - c if k & 1 else c
            peers.append((4 * px + 2 * py + pc, (px, py, pc)))
        sends = [remote(k, peer, me, xyz) for k, (peer, xyz) in enumerate(peers)]
        for cp in sends:
            cp.start()
        for k, (peer, xyz) in enumerate(peers):
            remote(k, peer, peer, xyz).wait_recv()
        for cp in sends:
            cp.wait_send()
        local.wait()

    return pl.pallas_call(
        body, in_specs=[pl.BlockSpec(memory_space=pl.ANY)], out_specs=pl.BlockSpec(memory_space=pl.ANY),
        out_shape=jax.ShapeDtypeStruct((N_DEV,) + blk, src.dtype),
        scratch_shapes=[pltpu.SemaphoreType.DMA((N_DEV - 1,)), pltpu.SemaphoreType.DMA((N_DEV - 1,)),
                        pltpu.SemaphoreType.DMA(())],
        name=name)(src)


def _row_tile(rows, cap):
    best = 8
    for t in range(8, min(rows, cap) + 1, 8):
        if rows % t == 0:
            best = t
    return best


def _sum_devices(x):
    _, rows, cols = x.shape
    tr = _row_tile(rows, 2048)

    def body(x_ref, o_ref):
        acc = x_ref[0]
        for j in range(1, N_DEV):
            acc = acc + x_ref[j]
        o_ref[...] = acc

    return pl.pallas_call(
        body, grid=(rows // tr,), in_specs=[pl.BlockSpec((N_DEV, tr, cols), lambda i: (0, i, 0))],
        out_specs=pl.BlockSpec((tr, cols), lambda i: (i, 0)), out_shape=jax.ShapeDtypeStruct((rows, cols), F32),
        compiler_params=_params(("arbitrary",)), name="sum_devices")(x)


def _adamw(w, g, m, v, name):
    rows, cols = w.shape
    tr = _row_tile(rows, 512)
    bc1 = 1.0 - ADAM_B1 ** ADAM_STEP
    bc2 = 1.0 - ADAM_B2 ** ADAM_STEP

    def body(w_ref, g_ref, m_ref, v_ref, d_ref, nm_ref, nv_ref):
        g_ = g_ref[...]
        m_ = ADAM_B1 * m_ref[...] + (1.0 - ADAM_B1) * g_
        v_ = ADAM_B2 * v_ref[...] + (1.0 - ADAM_B2) * (g_ * g_)
        d_ref[...] = -ADAM_LR * ((m_ / bc1) / (jnp.sqrt(v_ / bc2) + ADAM_EPS) + ADAM_WD * w_ref[...])
        nm_ref[...] = m_
        nv_ref[...] = v_

    blk = pl.BlockSpec((tr, cols), lambda i: (i, 0))
    return pl.pallas_call(
        body, grid=(rows // tr,), in_specs=[blk] * 4, out_specs=[blk] * 3,
        out_shape=[jax.ShapeDtypeStruct((rows, cols), F32)] * 3,
        compiler_params=_params(("arbitrary",)), name=name)(w, g, m, v)


def _pack(arrs, dtype=F32):
    flat = jnp.concatenate([a.reshape(-1).astype(dtype) for a in arrs])
    rows = -(-flat.shape[0] // (16 * LANES)) * 16
    flat = jnp.concatenate([flat, jnp.zeros((rows * LANES - flat.shape[0],), dtype)])
    return flat.reshape(rows, LANES)


def _unpack(flat, shapes):
    flat = flat.reshape(-1)
    out, off = [], 0
    for s in shapes:
        n = int(np.prod(s))
        out.append(flat[off:off + n].reshape(s))
        off += n
    return out


SMALL = ("c_ctx", "b_mod", "g_pre", "g_post", "ret_norm_g", "sg_w", "sg_b", "sc_conv_w", "gdn_conv_w", "gdn_a_log",
         "gdn_dt_bias", "gdn_norm_g")
ORDER = ("c_ctx", "w_mod", "b_mod", "g_pre", "g_post", "w_in", "w_out", "ret_norm_g", "sg_w", "sg_b", "sc_conv_w",
         "gdn_conv_w", "gdn_a_log", "gdn_dt_bias", "gdn_norm_g")


def kernel(x, c, ctx, c_ctx, w_mod, b_mod, g_pre, g_post, w_in, w_out, ret_norm_g, sg_w, sg_b, sc_conv_w, gdn_conv_w, gdn_a_log, gdn_dt_bias, gdn_norm_g, loss_target, m_c_ctx, m_w_mod, m_b_mod, m_g_pre, m_g_post, m_w_in, m_w_out, m_ret_norm_g, m_sg_w, m_sg_b, m_sc_conv_w, m_gdn_conv_w, m_gdn_a_log, m_gdn_dt_bias, m_gdn_norm_g, v_c_ctx, v_w_mod, v_b_mod, v_g_pre, v_g_post, v_w_in, v_w_out, v_ret_norm_g, v_sg_w, v_sg_b, v_sc_conv_w, v_gdn_conv_w, v_gdn_a_log, v_gdn_dt_bias, v_gdn_norm_g):
    wts = dict(c_ctx=c_ctx, w_mod=w_mod, b_mod=b_mod, g_pre=g_pre, g_post=g_post, w_in=w_in, w_out=w_out,
               ret_norm_g=ret_norm_g, sg_w=sg_w, sg_b=sg_b, sc_conv_w=sc_conv_w, gdn_conv_w=gdn_conv_w,
               gdn_a_log=gdn_a_log, gdn_dt_bias=gdn_dt_bias, gdn_norm_g=gdn_norm_g)
    mom = dict(c_ctx=m_c_ctx, w_mod=m_w_mod, b_mod=m_b_mod, g_pre=m_g_pre, g_post=m_g_post, w_in=m_w_in, w_out=m_w_out,
               ret_norm_g=m_ret_norm_g, sg_w=m_sg_w, sg_b=m_sg_b, sc_conv_w=m_sc_conv_w, gdn_conv_w=m_gdn_conv_w,
               gdn_a_log=m_gdn_a_log, gdn_dt_bias=m_gdn_dt_bias, gdn_norm_g=m_gdn_norm_g)
    var = dict(c_ctx=v_c_ctx, w_mod=v_w_mod, b_mod=v_b_mod, g_pre=v_g_pre, g_post=v_g_post, w_in=v_w_in, w_out=v_w_out,
               ret_norm_g=v_ret_norm_g, sg_w=v_sg_w, sg_b=v_sg_b, sc_conv_w=v_sc_conv_w, gdn_conv_w=v_gdn_conv_w,
               gdn_a_log=v_gdn_a_log, gdn_dt_bias=v_gdn_dt_bias, gdn_norm_g=v_gdn_norm_g)
    depth = w_mod.shape[0]
    n_mod, n_in, n_out = w_mod.shape[2], w_in.shape[2], w_out.shape[1]
    n_sc, n_gdn = sc_conv_w.shape[2], gdn_conv_w.shape[2]
    me = 4 * lax.axis_index("x") + 2 * lax.axis_index("y") + lax.axis_index("c")

    conv_bits = lax.bitcast_convert_type(jnp.concatenate([sc_conv_w.reshape(-1), gdn_conv_w.reshape(-1)]), BF16)
    shard = _pack([w_mod.astype(BF16), w_in.astype(BF16), w_out.astype(BF16), conv_bits], BF16)
    gathered = _exchange(shard, False, "gather_weights").reshape(N_DEV, -1)
    sizes = [depth * D * n_mod, depth * D * n_in, depth * n_out * D, 2 * depth * 3 * (n_sc + n_gdn)]
    offs = np.cumsum([0] + sizes)
    part = [gathered[:, offs[i]:offs[i + 1]] for i in range(4)]
    wmod_f = part[0].reshape(N_DEV, depth, D, n_mod).transpose(1, 2, 0, 3).reshape(depth, D, N_DEV * n_mod)
    win_f = part[1].reshape(N_DEV, depth, D, n_in).transpose(1, 2, 0, 3).reshape(depth, D, N_DEV * n_in)
    win_f = jnp.pad(win_f, ((0, 0), (0, 0), (0, W_PAD - N_DEV * n_in)))
    wout_f = part[2].reshape(N_DEV, depth, n_out, D).transpose(1, 0, 2, 3).reshape(depth, N_DEV * n_out, D)
    conv_f = lax.bitcast_convert_type(part[3].reshape(N_DEV, -1, 2), F32)
    scw_f = conv_f[:, :depth * 3 * n_sc].reshape(N_DEV, depth, 3, n_sc).transpose(1, 2, 0, 3).reshape(depth, 3, -1)
    gdnw_f = conv_f[:, depth * 3 * n_sc:].reshape(N_DEV, depth, 3, n_gdn).transpose(1, 2, 0, 3).reshape(depth, 3, -1)

    loss8, dx, g, cvec8 = _local_step(x, c, ctx, c_ctx, wmod_f, b_mod, g_pre, g_post, win_f, wout_f, ret_norm_g, sg_w,
                                      sg_b, scw_f, gdnw_f, gdn_a_log, gdn_dt_bias, gdn_norm_g, loss_target)

    slab_in = g["win"].reshape(depth, D, N_DEV, n_in).transpose(2, 0, 1, 3).reshape(N_DEV, -1, LANES)
    slab_out = g["wout"].reshape(depth, N_DEV, n_out, D).transpose(1, 0, 2, 3).reshape(N_DEV, -1, LANES)
    rows_in = slab_in.shape[1]
    summed = _sum_devices(_exchange(jnp.concatenate([slab_in, slab_out], axis=1), True, "scatter_grads"))
    grad = dict(w_in=summed[:rows_in].reshape(depth, D, n_in), w_out=summed[rows_in:].reshape(depth, n_out, D))

    local_small = dict(c_ctx=g["c_ctx"], b_mod=g["bmod"], g_pre=g["gpre"], g_post=g["gpost"], ret_norm_g=g["ret_ng"],
                       sg_w=g["sgw"], sg_b=g["sgb"], sc_conv_w=g["scw"], gdn_conv_w=g["gdnw"], gdn_a_log=g["alog"],
                       gdn_dt_bias=g["dtb"], gdn_norm_g=g["gdn_ng"])
    to_sum = _pack([loss8[0, :1]] + [local_small[k] for k in SMALL])
    rows_sum = to_sum.shape[0]
    as_is = _pack([cvec8[:3], g["dmod"][:, :3, :]])
    everyone = _exchange(jnp.concatenate([to_sum, as_is], axis=0), False, "gather_small")
    small_sum = _unpack(_sum_devices(everyone[:, :rows_sum]), [(1,)] + [local_small[k].shape for k in SMALL])
    loss = small_sum[0][0]
    for k, val in zip(SMALL, small_sum[1:]):
        grad[k] = val
    grad["sc_conv_w"] = lax.dynamic_slice_in_dim(grad["sc_conv_w"], me * n_sc, n_sc, axis=2)
    grad["gdn_conv_w"] = lax.dynamic_slice_in_dim(grad["gdn_conv_w"], me * n_gdn, n_gdn, axis=2)
    rest =everyone[:, rows_sum:].reshape(N_DEV, -1)
    c_all = rest[:, :3 * D].reshape(N_DEV * 3, D)
    dmod_all = rest[:, 3 * D:3 * D + depth * 9 * D].reshape(N_DEV, depth, 3, 3 * D).transpose(1, 0, 2, 3)
    dmod_mine = lax.dynamic_slice_in_dim(dmod_all.reshape(depth, N_DEV * 3, 3 * D), me * n_mod, n_mod, axis=2)
    grad["w_mod"] = _wmod_grad(_pad_rows(c_all, 32), jnp.pad(dmod_mine, ((0, 0), (0, 32 - N_DEV * 3), (0, 0))))

    delta, new_m, new_v = {}, {}, {}
    for k in ("w_mod", "w_in", "w_out"):
        shp = wts[k].shape
        two = lambda a: a.reshape(-1, shp[-1])
        res = _adamw(two(wts[k]), two(grad[k]), two(mom[k]), two(var[k]), "adamw_" + k)
        delta[k], new_m[k], new_v[k] = [r.reshape(shp) for r in res]
    res = _adamw(*[_pack([d[k] for k in SMALL]) for d in (wts, grad, mom, var)], "adamw_small")
    for dst, flat in zip((delta, new_m, new_v), res):
        for k, val in zip(SMALL, _unpack(flat, [wts[k].shape for k in SMALL])):
            dst[k] = val
    return (loss, dx, *[grad[k] for k in ORDER], *[delta[k] for k in ORDER], *[new_m[k] for k in ORDER],
            *[new_v[k] for k in ORDER])
```

```python
import functools
import math

import jax
import jax.numpy as jnp
import numpy as np
from jax import lax
from jax.experimental import pallas as pl
from jax.experimental.pallas import tpu as pltpu

F32, BF16 = jnp.float32, jnp.bfloat16
HI = lax.Precision.HIGHEST

N_DEV = 8
D = 1024
DEPTH = 4
BRW = 256
HD = 64
NH = 4
LANES = 128
GRID_W = 64
ROPE_BASE = 10000.0
W_IN = 15 * BRW + 4 * NH
W_PAD = 31 * LANES
RC = 128
GC = 64
EPS = 1e-6
LOG_GAMMA = tuple(math.log(1.0 - 2.0 ** (-5.0 - h)) for h in range(NH))
ADAM_LR, ADAM_B1, ADAM_B2, ADAM_EPS, ADAM_WD, ADAM_STEP = 0.001, 0.9, 0.999, 1e-08, 0.01, 10
VMEM_LIMIT = 56 * 1024 * 1024

COL_RET, COL_SG, COL_SC, COL_GDN = 0, 4, 7, 11
COL_A128 = 30


def _params(sem):
    return pltpu.CompilerParams(dimension_semantics=sem, vmem_limit_bytes=VMEM_LIMIT)


def _bdot(a, b, ca, cb):
    return lax.dot_general(a.astype(BF16), b.astype(BF16), (((ca,), (cb,)), ((), ())),
                           preferred_element_type=F32)


@jax.custom_vjp
def mm(a, b):
    return _bdot(a, b, 1, 0)


mm.defvjp(lambda a, b: (_bdot(a, b, 1, 0), (a, b)),
          lambda r, g: (_bdot(g, r[1], 1, 1), _bdot(r[0], g, 0, 0)))


@jax.custom_vjp
def mm_nt(a, b):
    return _bdot(a, b, 1, 1)


mm_nt.defvjp(lambda a, b: (_bdot(a, b, 1, 1), (a, b)),
             lambda r, g: (_bdot(g, r[1], 1, 0), _bdot(g, r[0], 0, 0)))


@jax.custom_vjp
def mm_tn(a, b):
    return _bdot(a, b, 0, 0)


mm_tn.defvjp(lambda a, b: (_bdot(a, b, 0, 0), (a, b)),
             lambda r, g: (_bdot(r[1], g, 1, 1), _bdot(r[0], g, 1, 0)))


def _dotf(a, b):
    return jnp.dot(a, b, precision=HI, preferred_element_type=F32)


def _iota(shape, dim):
    return lax.broadcasted_iota(jnp.int32, shape, dim)


def _head_mask(h, width=BRW):
    return (_iota((1, width), 1) // HD == h).astype(F32)


def _lane_by_head(vals, width=BRW, lane0=0):
    head = (_iota((1, width), 1) + lane0) // HD
    out = jnp.full((1, width), vals[NH - 1], F32)
    for h in range(NH - 2, -1, -1):
        out = jnp.where(head == h, vals[h], out)
    return out


def _block_diag(n, width):
    return (_iota((n, width), 0) // HD == _iota((n, width), 1) // HD).astype(F32)


def _head_sum(x):
    w = x.shape[1]
    return _dotf(x, _block_diag(w, w))


def _silu(x):
    return x * jax.nn.sigmoid(x)


def _stack_heads(x):
    return jnp.concatenate([x * _head_mask(h) for h in range(NH)], axis=0)


@jax.custom_vjp
def _unstack_heads(x):
    n = x.shape[0] // NH
    return (x[0:n] + x[n:2 * n]) + (x[2 * n:3 * n] + x[3 * n:4 * n])


_unstack_heads.defvjp(lambda x: (_unstack_heads(x), None), lambda _, g: (_stack_heads(g),))


@jax.custom_vjp
def _rot_half(x):
    n = x.shape[1]
    first = (_iota(x.shape, 1) % 32) < 16
    return jnp.where(first, -pltpu.roll(x, n - 16, 1), pltpu.roll(x, 16, 1))


_rot_half.defvjp(lambda x: (_rot_half(x), None), lambda _, g: (-_rot_half(g),))


def _rotary(x, cos, sin):
    return x * cos + _rot_half(x) * sin


def _make_shifts(seq, t_ctx):
    def dn_raw(x):
        r = _iota(x.shape, 0)
        return jnp.where((r == 0) | (r == t_ctx), 0.0, pltpu.roll(x, 1, 0))

    def up_raw(x):
        r = _iota(x.shape, 0)
        return jnp.where((r == t_ctx - 1) | (r == seq - 1), 0.0, pltpu.roll(x, seq - 1, 0))

    @jax.custom_vjp
    def dn(x):
        return dn_raw(x)

    @jax.custom_vjp
    def up(x):
        return up_raw(x)

    dn.defvjp(lambda x: (dn_raw(x), None), lambda _, g: (up_raw(g),))
    up.defvjp(lambda x: (up_raw(x), None), lambda _, g: (dn_raw(g),))
    return dn, up


def _conv3(t, w0, w1, w2, dn, up):
    return dn(t) * w0 + t * w1 + up(t) * w2


def _acc(ref, val, first, at=()):
    idx = at + (Ellipsis,)

    @pl.when(first)
    def _():
        ref[idx] = val

    @pl.when(jnp.logical_not(first))
    def _():
        ref[idx] += val


def _mod_fwd(cvec8, wmod, bmod):
    depth = wmod.shape[0]

    def body(c_ref, w_ref, b_ref, o_ref):
        sc = _silu(c_ref[...])
        o_ref[0] = jnp.dot(sc.astype(BF16), w_ref[0], preferred_element_type=F32) + b_ref[0]

    return pl.pallas_call(
        body, grid=(depth, 3),
        in_specs=[pl.BlockSpec((8, D), lambda l, j: (0, 0)),
                  pl.BlockSpec((1, D, D), lambda l, j: (l, 0, j)),
                  pl.BlockSpec((1, 1, D), lambda l, j: (l, 0, j))],
        out_specs=pl.BlockSpec((1, 8, D), lambda l, j: (l, 0, j)),
        out_shape=jax.ShapeDtypeStruct((depth, 8, 3 * D), F32),
        compiler_params=_params(("arbitrary", "arbitrary")), name="mod_fwd")(cvec8, wmod, bmod)


def _mod_bwd(dmod, wmod, cvec8):
    depth = wmod.shape[0]

    def body(dm_ref, w_ref, c_ref, dc_ref, db_ref):
        l, j = pl.program_id(0), pl.program_id(1)
        dm = dm_ref[0]
        db_ref[0] = jnp.sum(dm, axis=0, keepdims=True)
        part = _bdot(dm, w_ref[0], 1, 1)
        _acc(dc_ref, part, (l == 0) & (j == 0))

        @pl.when((l == depth - 1) & (j == 2))
        def _():
            c = c_ref[...]
            s = jax.nn.sigmoid(c)
            dc_ref[...] = dc_ref[...] * (s * (1.0 + c * (1.0 - s)))

    return pl.pallas_call(
        body, grid=(depth, 3),
        in_specs=[pl.BlockSpec((1, 8, D), lambda l, j: (l, 0, j)),
                  pl.BlockSpec((1, D, D), lambda l, j: (l, 0, j)),
                  pl.BlockSpec((8, D), lambda l, j: (0, 0))],
        out_specs=[pl.BlockSpec((8, D), lambda l, j: (0, 0)),
                   pl.BlockSpec((1, 1, D), lambda l, j: (l, 0, j))],
        out_shape=[jax.ShapeDtypeStruct((8, D), F32), jax.ShapeDtypeStruct((depth, 1, 3 * D), F32)],
        compiler_params=_params(("arbitrary", "arbitrary")), name="mod_bwd")(dmod, wmod, cvec8)


def _wmod_grad(c_rows, dmod_cols):
    depth, rows, n = dmod_cols.shape

    def body(c_ref, dm_ref, o_ref):
        sc = _silu(c_ref[...])
        o_ref[0] = lax.dot_general(sc, dm_ref[0], (((0,), (0,)), ((), ())), precision=HI,
                                   preferred_element_type=F32)

    return pl.pallas_call(
        body, grid=(depth,),
        in_specs=[pl.BlockSpec((rows, D), lambda l: (0, 0)), pl.BlockSpec((1, rows, n), lambda l: (l, 0, 0))],
        out_specs=pl.BlockSpec((1, D, n), lambda l: (l, 0, 0)),
        out_shape=jax.ShapeDtypeStruct((depth, D, n), F32),
        compiler_params=_params(("arbitrary",)), name="wmod_grad")(c_rows, dmod_cols)


class _Lay:
    def __init__(self, batch, t_ctx, t_lat):
        self.b, self.t_ctx, self.t_lat = batch, t_ctx, t_lat
        self.s = t_ctx + t_lat
        self.tm = min(256, t_ctx)
        self.tpb = self.s // self.tm
        self.nct = t_ctx // self.tm
        self.ntiles = batch * self.tpb
        self.rows = batch * self.s

    def mod_row(self, i):
        return jnp.where(i % self.tpb < self.nct, 2, i // self.tpb)

    def group(self, i):
        return 2 * (i // self.tpb) + jnp.where(i % self.tpb < self.nct, 0, 1)

    def group_first(self, i):
        return (i % self.tpb == 0) | (i % self.tpb == self.nct)


def _norm_mod(x, g, shift, scale):
    r = lax.rsqrt(jnp.mean(x * x, axis=-1, keepdims=True) + EPS)
    return (x * r * g) * (1.0 + scale) + shift


def _inproj_fwd(lay, xc, mod3, gpre, w):
    tm = lay.tm

    def body(x_ref, sh_ref, sc_ref, g_ref, w_ref, p_ref, h_ref):
        h = _norm_mod(x_ref[...], g_ref[...], sh_ref[0, 0], sc_ref[0, 0]).astype(BF16)
        h_ref[...] = h
        p_ref[...] = jnp.dot(h, w_ref[...], preferred_element_type=F32)

    return pl.pallas_call(
        body, grid=(lay.ntiles,),
        in_specs=[pl.BlockSpec((tm, D), lambda i: (i, 0)),
                  pl.BlockSpec((1, 1, 1, D), lambda i: (0, lay.mod_row(i), 0, 0)),
                  pl.BlockSpec((1, 1, 1, D), lambda i: (1, lay.mod_row(i), 0, 0)),
                  pl.BlockSpec((1, D), lambda i: (0, 0)),
                  pl.BlockSpec((D, W_PAD), lambda i: (0, 0))],
        out_specs=[pl.BlockSpec((tm, W_PAD), lambda i: (i, 0)), pl.BlockSpec((tm, D), lambda i: (i, 0))],
        out_shape=[jax.ShapeDtypeStruct((lay.rows, W_PAD), F32), jax.ShapeDtypeStruct((lay.rows, D), BF16)],
        compiler_params=_params(("arbitrary",)), name="inproj_fwd")(xc, mod3, mod3, gpre, w)


def _inproj_bwd(lay, xc, mod3, gpre, wt, dxc, pieces):
    tm = lay.tm
    npc = len(pieces)
    offs = [off for _, off in pieces]

    def body(*refs):
        x_ref, sh_ref, sc_ref, g_ref, wt_ref, dx_in = refs[:6]
        dps = refs[6:6 + npc]
        dx_ref, dg_ref, dsh_ref, dsc_ref = refs[6 + npc:]
        i = pl.program_id(0)
        dh = None
        for dp_ref, off in zip(dps, offs):
            wd = dp_ref.shape[1]
            part = jnp.dot(dp_ref[...], wt_ref[off:off + wd, :], preferred_element_type=F32)
            dh = part if dh is None else dh + part
        _, vjp = jax.vjp(_norm_mod, x_ref[...], g_ref[...], sh_ref[0, 0], sc_ref[0, 0])
        dx, dg, dsh, dsc = vjp(dh)
        dx_ref[...] = dx_in[...] + dx
        _acc(dg_ref, dg, i == 0)
        first = lay.group_first(i)
        _acc(dsh_ref, dsh, first, at=(0,))
        _acc(dsc_ref, dsc, first, at=(0,))

    return pl.pallas_call(
        body, grid=(lay.ntiles,),
        in_specs=[pl.BlockSpec((tm, D), lambda i: (i, 0)),
                  pl.BlockSpec((1, 1, 1, D), lambda i: (0, lay.mod_row(i), 0, 0)),
                  pl.BlockSpec((1, 1, 1, D), lambda i: (1, lay.mod_row(i), 0, 0)),
                  pl.BlockSpec((1, D), lambda i: (0, 0)),
                  pl.BlockSpec((W_PAD, D), lambda i: (0, 0)),
                  pl.BlockSpec((tm, D), lambda i: (i, 0))]
        + [pl.BlockSpec((tm, dp.shape[1]), lambda i: (i, 0)) for dp, _ in pieces],
        out_specs=[pl.BlockSpec((tm, D), lambda i: (i, 0)),
                   pl.BlockSpec((1, D), lambda i: (0, 0)),
                   pl.BlockSpec((1, 1, D), lambda i: (lay.group(i), 0, 0)),
                   pl.BlockSpec((1, 1, D), lambda i: (lay.group(i), 0, 0))],
        out_shape=[jax.ShapeDtypeStruct((lay.rows, D), F32), jax.ShapeDtypeStruct((1, D), F32),
                   jax.ShapeDtypeStruct((2 * lay.b, 1, D), F32), jax.ShapeDtypeStruct((2 * lay.b, 1, D), F32)],
        compiler_params=_params(("arbitrary",)), name="inproj_bwd",
    )(xc, mod3, mod3, gpre, wt, dxc, *[dp for dp, _ in pieces])


def _weight_grad(lay, h, dp, name):
    wd = dp.shape[1]
    tn = 512 if wd % 512 == 0 else (256 if wd % 256 == 0 else LANES)
    tr = lay.rows // 3 if lay.rows % (3 * 256) == 0 else lay.tm

    def body(h_ref, dp_ref, o_ref):
        part = lax.dot_general(h_ref[...], dp_ref[...], (((0,), (0,)), ((), ())), preferred_element_type=F32)
        _acc(o_ref, part, pl.program_id(1) == 0)

    return pl.pallas_call(
        body, grid=(wd // tn, lay.rows // tr),
        in_specs=[pl.BlockSpec((tr, D), lambda j, i: (i, 0)), pl.BlockSpec((tr, tn), lambda j, i: (i, j))],
        out_specs=pl.BlockSpec((D, tn), lambda j, i: (0, j)),
        out_shape=jax.ShapeDtypeStruct((D, wd), F32),
        compiler_params=_params(("arbitrary", "arbitrary")), name=name)(h, dp)


def _outproj_fn(y0, y1, y2, y3, x, w0, w1, w2, w3, gpost, gate):
    o = mm(y0, w0) + mm(y1, w1) + mm(y2, w2) + mm(y3, w3)
    r = lax.rsqrt(jnp.mean(o * o, axis=-1, keepdims=True) + EPS)
    return x + gate * (o * r * gpost)


def _outproj_specs(lay):
    tm = lay.tm
    return ([pl.BlockSpec((tm, BRW), lambda i: (i, 0))] * 4
            + [pl.BlockSpec((tm, D), lambda i: (i, 0))]
            + [pl.BlockSpec((D, D), lambda i: (0, 0))]
            + [pl.BlockSpec((1, D), lambda i: (0, 0))]
            + [pl.BlockSpec((1, 1, 1, D), lambda i: (2, lay.mod_row(i), 0, 0))])


def _outproj_args(x_ref_or_none, ys, w_ref, g_ref, gt_ref):
    ws = [w_ref[BRW * k:BRW * (k + 1), :].astype(F32) for k in range(4)]
    return [y[...].astype(F32) for y in ys] + [x_ref_or_none[...]] + ws + [g_ref[...], gt_ref[0, 0]]


def _outproj_fwd(lay, ys, xc, wout, gpost, mod3):
    tm = lay.tm

    def body(y0, y1, y2, y3, x_ref, w_ref, g_ref, gt_ref, o_ref):
        o_ref[...] = _outproj_fn(*_outproj_args(x_ref, (y0, y1, y2, y3), w_ref, g_ref, gt_ref))

    return pl.pallas_call(
        body, grid=(lay.ntiles,), in_specs=_outproj_specs(lay),
        out_specs=pl.BlockSpec((tm, D), lambda i: (i, 0)),
        out_shape=jax.ShapeDtypeStruct((lay.rows, D), F32),
        compiler_params=_params(("arbitrary",)), name="outproj_fwd")(*ys, xc, wout, gpost, mod3)


def _outproj_bwd(lay, ys, xc, wout, gpost, mod3, dxc):
    tm = lay.tm

    def body(y0, y1, y2, y3, x_ref, w_ref, g_ref, gt_ref, dx_ref, d0, d1, d2, d3, dw_ref, dg_ref, dgt_ref):
        i = pl.program_id(0)
        args = _outproj_args(x_ref, (y0, y1, y2, y3), w_ref, g_ref, gt_ref)
        _, vjp = jax.vjp(_outproj_fn, *args)
        g = vjp(dx_ref[...])
        for k, d in enumerate((d0, d1, d2, d3)):
            d[...] = g[k]

        @pl.when(i == 0)
        def _():
            for k in range(4):
                dw_ref[BRW * k:BRW * (k + 1), :] = g[5 + k]

        @pl.when(i != 0)
        def _():
            for k in range(4):
                dw_ref[BRW * k:BRW * (k + 1), :] += g[5 + k]

        _acc(dg_ref, g[9], i == 0)
        _acc(dgt_ref, g[10], lay.group_first(i), at=(0,))

    return pl.pallas_call(
        body, grid=(lay.ntiles,),
        in_specs=_outproj_specs(lay) + [pl.BlockSpec((tm, D), lambda i: (i, 0))],
        out_specs=[pl.BlockSpec((tm, BRW), lambda i: (i, 0))] * 4
        + [pl.BlockSpec((D, D), lambda i: (0, 0)), pl.BlockSpec((1, D), lambda i: (0, 0)),
           pl.BlockSpec((1, 1, D), lambda i: (lay.group(i), 0, 0))],
        out_shape=[jax.ShapeDtypeStruct((lay.rows, BRW), F32)] * 4
        + [jax.ShapeDtypeStruct((D, D), F32), jax.ShapeDtypeStruct((1, D), F32),
           jax.ShapeDtypeStruct((2 * lay.b, 1, D), F32)],
        compiler_params=_params(("arbitrary",)), name="outproj_bwd")(*ys, xc, wout, gpost, mod3, dxc)


def _loss_kernel(lay, xc3, target):
    tm, nct = lay.tm, lay.nct

    def body(x_ref, t_ref, loss_ref, dx_ref):
        b, i = pl.program_id(0), pl.program_id(1)
        lat = i >= nct
        err = x_ref[0] - t_ref[0]
        dx_ref[0] = jnp.where(lat, err * (1.0 / D), 0.0)
        part = jnp.sum(jnp.sum(err * err, axis=1, keepdims=True), axis=0, keepdims=True) * (0.5 / D)
        part = jnp.broadcast_to(jnp.where(lat, part, 0.0), (8, LANES))
        _acc(loss_ref, part, (b == 0) & (i == 0))

    return pl.pallas_call(
        body, grid=(lay.b, lay.tpb),
        in_specs=[pl.BlockSpec((1, tm, D), lambda b, i: (b, i, 0)),
                  pl.BlockSpec((1, tm, D), lambda b, i: (b, jnp.maximum(i - nct, 0), 0))],
        out_specs=[pl.BlockSpec((8, LANES), lambda b, i: (0, 0)), pl.BlockSpec((1, tm, D), lambda b, i: (b, i, 0))],
        out_shape=[jax.ShapeDtypeStruct((8, LANES), F32), jax.ShapeDtypeStruct(xc3.shape, F32)],
        compiler_params=_params(("arbitrary", "arbitrary")), name="loss")(xc3, target)


def _chunk_orders(n_ctx, n_all):
    fwd = list(range(n_all))
    rev = list(range(n_ctx - 1, -1, -1)) + list(range(n_all - 1, n_ctx - 1, -1))
    return fwd, rev


def _ret_state_fn(k, v, cos, sin):
    kt = _rotary(k, cos, sin) * (HD ** -0.5)
    lg = _lane_by_head(LOG_GAMMA)
    j = _iota((RC, 1), 0).astype(F32)
    bd = _block_diag(BRW, BRW)
    af = mm_tn(kt * jnp.exp((RC - 1.0 - j) * lg), v) * bd
    ar = mm_tn(kt * jnp.exp(j * lg), v) * bd
    return af, ar


def _ret_out_fn(q, k, v, z, cos, sin, sf, sr, ng):
    qt = _rotary(q, cos, sin)
    kt = _rotary(k, cos, sin) * (HD ** -0.5)
    diff = (_iota((RC, RC), 0) - _iota((RC, RC), 1)).astype(F32)
    o = None
    for h in range(NH):
        m = _head_mask(h)
        sc = mm_nt(qt * m, kt)
        wgt = sc * jnp.exp(-jnp.abs(diff) * (-LOG_GAMMA[h])) * jnp.where(diff == 0, 2.0, 1.0)
        part = mm(wgt, v * m)
        o = part if o is None else o + part
    lg = _lane_by_head(LOG_GAMMA)
    i = _iota((RC, 1), 0).astype(F32)
    o = o + mm(qt, sf) * jnp.exp((i + 1.0) * lg) + mm(qt, sr) * jnp.exp((RC - i) * lg)
    mu = _head_sum(o) * (1.0 / HD)
    cen = o - mu
    var = _head_sum(cen * cen) * (1.0 / HD)
    return cen * lax.rsqrt(var + EPS) * ng * _silu(z)


def _ret_specs(lay, cols):
    return [pl.BlockSpec((1, RC, BRW), functools.partial(lambda b, i, c: (b, i, c), c=COL_RET + c)) for c in cols]


def _ret_state(lay, p3, cos, sin):
    nc = lay.s // RC

    def body(k_ref, v_ref, c_ref, s_ref, a_ref):
        af, ar = _ret_state_fn(k_ref[0], v_ref[0], c_ref[...], s_ref[...])
        a_ref[0, 0, 0] = af
        a_ref[0, 0, 1] = ar

    tab = pl.BlockSpec((RC, BRW), lambda b, i: (i, 0))
    return pl.pallas_call(
        body, grid=(lay.b, nc), in_specs=_ret_specs(lay, (1, 2)) + [tab, tab],
        out_specs=pl.BlockSpec((1, 1, 2, BRW, BRW), lambda b, i: (b, i, 0, 0, 0)),
        out_shape=jax.ShapeDtypeStruct((lay.b, nc, 2, BRW, BRW), F32),
        compiler_params=_params(("arbitrary", "arbitrary")), name="ret_state")(p3, p3, cos, sin)


def _ret_state_bwd(lay, p3, cos, sin, d_a, dpr):
    nc = lay.s // RC

    def body(k_ref, v_ref, c_ref, s_ref, da_ref, dpr_ref, o_ref):
        _, vjp = jax.vjp(lambda k, v: _ret_state_fn(k, v, c_ref[...], s_ref[...]), k_ref[0], v_ref[0])
        dk, dv = vjp((da_ref[0, 0, 0], da_ref[0, 0, 1]))
        o_ref[0, :, 0:BRW] = dpr_ref[0, :, 0:BRW].astype(BF16)
        o_ref[0, :, BRW:2 * BRW] = (dpr_ref[0, :, BRW:2 * BRW] + dk).astype(BF16)
        o_ref[0, :, 2 * BRW:3 * BRW] = (dpr_ref[0, :, 2 * BRW:3 * BRW] + dv).astype(BF16)
        o_ref[0, :, 3 * BRW:] = dpr_ref[0, :, 3 * BRW:].astype(BF16)

    tab = pl.BlockSpec((RC, BRW), lambda b, i: (i, 0))
    return pl.pallas_call(
        body, grid=(lay.b, nc),
        in_specs=_ret_specs(lay, (1, 2)) + [tab, tab,
                                            pl.BlockSpec((1, 1, 2, BRW, BRW), lambda b, i: (b, i, 0, 0, 0)),
                                            pl.BlockSpec((1, RC, 4 * BRW), lambda b, i: (b, i, 0))],
        out_specs=pl.BlockSpec((1, RC, 4 * BRW), lambda b, i: (b, i, 0)),
        out_shape=jax.ShapeDtypeStruct((lay.b, lay.s, 4 * BRW), BF16),
        compiler_params=_params(("arbitrary", "arbitrary")), name="ret_state_bwd")(p3, p3, cos, sin, d_a, dpr)


def _state_scan(lay, a, nc_ctx, transpose, name):
    b, nc = a.shape[0], a.shape[1]
    orders = _chunk_orders(nc_ctx, nc)

    def body(a_ref, o_ref):
        d, jh = pl.program_id(1), pl.program_id(2)
        head = (_iota((1, LANES), 1) + jh * LANES) // HD
        lg = jnp.full((1, LANES), LOG_GAMMA[NH - 1], F32)
        for h in range(NH - 2, -1, -1):
            lg = jnp.where(head == h, LOG_GAMMA[h], lg)
        dec = jnp.exp(RC * lg)
        for dd in (0, 1):
            @pl.when(d == dd)
            def _(order=orders[dd]):
                acc = jnp.zeros((BRW, LANES), F32)
                if not transpose:
                    for c in order:
                        o_ref[0, c, 0] = acc
                        acc = acc * dec + a_ref[0, c, 0]
                else:
                    for c in reversed(order):
                        o_ref[0, c, 0] = acc
                        acc = a_ref[0, c, 0] + acc * dec

    spec = pl.BlockSpec((1, nc, 1, BRW, LANES), lambda bb, d, jh: (bb, 0, d, 0, jh))
    return pl.pallas_call(
        body, grid=(b, 2, BRW // LANES), in_specs=[spec], out_specs=spec,
        out_shape=jax.ShapeDtypeStruct(a.shape, F32),
        compiler_params=_params(("arbitrary",) * 3), name=name)(a)


def _ret_out(lay, p3, cos, sin, states, ng):
    nc = lay.s // RC

    def body(q_ref, k_ref, v_ref, z_ref, c_ref, s_ref, st_ref, ng_ref, y_ref):
        y = _ret_out_fn(q_ref[0], k_ref[0], v_ref[0], z_ref[0], c_ref[...], s_ref[...],
                        st_ref[0, 0, 0], st_ref[0, 0, 1], ng_ref[...])
        y_ref[0] = y.astype(BF16)

    tab = pl.BlockSpec((RC, BRW), lambda b, i: (i, 0))
    return pl.pallas_call(
        body, grid=(lay.b, nc),
        in_specs=_ret_specs(lay, (0, 1, 2, 3)) + [tab, tab,
                                                  pl.BlockSpec((1, 1, 2, BRW, BRW), lambda b, i: (b, i, 0, 0, 0)),
                                                  pl.BlockSpec((1, BRW), lambda b, i: (0, 0))],
        out_specs=pl.BlockSpec((1, RC, BRW), lambda b, i: (b, i, 0)),
        out_shape=jax.ShapeDtypeStruct((lay.b, lay.s, BRW), BF16),
        compiler_params=_params(("arbitrary", "arbitrary")), name="ret_out")(p3, p3, p3, p3, cos, sin, states, ng)


def _ret_out_bwd(lay, p3, cos, sin, states, ng, dy):
    nc = lay.s // RC

    def body(q_ref, k_ref, v_ref, z_ref, c_ref, s_ref, st_ref, ng_ref, dy_ref, dp_ref, dst_ref, dng_ref):
        b, i = pl.program_id(0), pl.program_id(1)
        fn = lambda q, k, v, z, sf, sr, ng: _ret_out_fn(q, k, v, z, c_ref[...], s_ref[...], sf, sr, ng)
        _, vjp = jax.vjp(fn, q_ref[0], k_ref[0], v_ref[0], z_ref[0], st_ref[0, 0, 0], st_ref[0, 0, 1], ng_ref[...])
        dq, dk, dv, dz, dsf, dsr, dng = vjp(dy_ref[0])
        for n, g in enumerate((dq, dk, dv, dz)):
            dp_ref[0, :, BRW * n:BRW * (n + 1)] = g
        dst_ref[0, 0, 0] = dsf
        dst_ref[0, 0, 1] = dsr
        _acc(dng_ref, dng, (b == 0) & (i == 0))

    tab = pl.BlockSpec((RC, BRW), lambda b, i: (i, 0))
    st = pl.BlockSpec((1, 1, 2, BRW, BRW), lambda b, i: (b, i, 0, 0, 0))
    return pl.pallas_call(
        body, grid=(lay.b, nc),
        in_specs=_ret_specs(lay, (0, 1, 2, 3)) + [tab, tab, st, pl.BlockSpec((1, BRW), lambda b, i: (0, 0)),
                                                  pl.BlockSpec((1, RC, BRW), lambda b, i: (b, i, 0))],
        out_specs=[pl.BlockSpec((1, RC, 4 * BRW), lambda b, i: (b, i, 0)), st,
                   pl.BlockSpec((1, BRW), lambda b, i: (0, 0))],
        out_shape=[jax.ShapeDtypeStruct((lay.b, lay.s, 4 * BRW), F32),
                   jax.ShapeDtypeStruct(states.shape, F32), jax.ShapeDtypeStruct((1, BRW), F32)],
        compiler_params=_params(("arbitrary", "arbitrary")), name="ret_out_bwd",
    )(p3, p3, p3, p3, cos, sin, states, ng, dy)


def _sg_fn(u, v, z, w0, w1, w2, w3, b8):
    ug = jax.nn.gelu(u)
    vg = jax.nn.gelu(v)
    mu = jnp.mean(vg, axis=-1, keepdims=True)
    cen = vg - mu
    vn = cen * lax.rsqrt(jnp.mean(cen * cen, axis=-1, keepdims=True) + EPS)
    s = None
    for h, w in enumerate((w0, w1, w2, w3)):
        part = mm(w, vn * _head_mask(h))
        s = part if s is None else s + part
    expand = (_iota((8, BRW), 1) // HD == _iota((8, BRW), 0)).astype(F32)
    bias = lax.dot_general(b8, expand, (((0,), (0,)), ((), ())), precision=HI, preferred_element_type=F32)
    return ug * (s + bias) * _silu(z)


def _sg_specs():
    return ([pl.BlockSpec((1, RC, BRW), functools.partial(lambda b, i, c: (b, i, c), c=COL_SG + c)) for c in range(3)]
            + [pl.BlockSpec((NH, RC, RC), lambda b, i: (0, 0, 0)), pl.BlockSpec((8, RC), lambda b, i: (0, 0))])


def _sg_fwd(lay, p3, sgw, sgb8):
    nc = lay.s // RC

    def body(u_ref, v_ref, z_ref, w_ref, b_ref, y_ref):
        y = _sg_fn(u_ref[0], v_ref[0], z_ref[0], w_ref[0], w_ref[1], w_ref[2], w_ref[3], b_ref[...])
        y_ref[0] = y.astype(BF16)

    return pl.pallas_call(
        body, grid=(lay.b, nc), in_specs=_sg_specs(),
        out_specs=pl.BlockSpec((1, RC, BRW), lambda b, i: (b, i, 0)),
        out_shape=jax.ShapeDtypeStruct((lay.b, lay.s, BRW), BF16),
        compiler_params=_params(("arbitrary", "arbitrary")), name="sg_fwd")(p3, p3, p3, sgw, sgb8)


def _sg_bwd(lay, p3, sgw, sgb8, dy):
    nc = lay.s // RC

    def body(u_ref, v_ref, z_ref, w_ref, b_ref, dy_ref, dp_ref, dw_ref, db_ref):
        first = (pl.program_id(0) == 0) & (pl.program_id(1) == 0)
        _, vjp = jax.vjp(_sg_fn, u_ref[0], v_ref[0], z_ref[0], w_ref[0], w_ref[1], w_ref[2], w_ref[3], b_ref[...])
        g = vjp(dy_ref[0])
        for n in range(3):
            dp_ref[0, :, BRW * n:BRW * (n + 1)] = g[n].astype(BF16)
        for h in range(NH):
            _acc(dw_ref, g[3 + h], first, at=(h,))
        _acc(db_ref, g[7], first)

    return pl.pallas_call(
        body, grid=(lay.b, nc),
        in_specs=_sg_specs() + [pl.BlockSpec((1, RC, BRW), lambda b, i: (b, i, 0))],
        out_specs=[pl.BlockSpec((1, RC, 3 * BRW), lambda b, i: (b, i, 0)),
                   pl.BlockSpec((NH, RC, RC), lambda b, i: (0, 0, 0)), pl.BlockSpec((8, RC), lambda b, i: (0, 0))],
        out_shape=[jax.ShapeDtypeStruct((lay.b, lay.s, 3 * BRW), BF16),
                   jax.ShapeDtypeStruct((NH, RC, RC), F32), jax.ShapeDtypeStruct((8, RC), F32)],
        compiler_params=_params(("arbitrary", "arbitrary")), name="sg_bwd")(p3, p3, p3, sgw, sgb8, dy)


def _sc_specs(lay):
    first = COL_SC * BRW // LANES
    blk = [pl.BlockSpec((1, lay.s, LANES), functools.partial(lambda j, b, c: (b, 0, c + j), c=first + 2 * n))
           for n in range(4)]
    return blk + [pl.BlockSpec((8, LANES), lambda j, b: (0, j))]


def _sc_fwd(lay, p3, w8):
    dn, up = _make_shifts(lay.s, lay.t_ctx)

    def fn(b_, c_, h_, z_, w0, w1, w2):
        return b_ * _conv3(c_ * h_, w0, w1, w2, dn, up) * _silu(z_)

    def body(b_ref, c_ref, h_ref, z_ref, w_ref, y_ref):
        y = fn(b_ref[0], c_ref[0], h_ref[0], z_ref[0], w_ref[0:1, :], w_ref[1:2, :], w_ref[2:3, :])
        y_ref[0] = y.astype(BF16)

    return pl.pallas_call(
        body, grid=(BRW // LANES, lay.b), in_specs=_sc_specs(lay),
        out_specs=pl.BlockSpec((1, lay.s, LANES), lambda j, b: (b, 0, j)),
        out_shape=jax.ShapeDtypeStruct((lay.b, lay.s, BRW), BF16),
        compiler_params=_params(("arbitrary", "arbitrary")), name="sc_fwd")(p3, p3, p3, p3, w8)


def _sc_bwd(lay, p3, w8, dy):
    dn, up = _make_shifts(lay.s, lay.t_ctx)

    def fn(b_, c_, h_, z_, w0, w1, w2):
        return b_ * _conv3(c_ * h_, w0, w1, w2, dn, up) * _silu(z_)

    def body(b_ref, c_ref, h_ref, z_ref, w_ref, dy_ref, db_ref, dc_ref, dh_ref, dz_ref, dw_ref):
        _, vjp = jax.vjp(fn, b_ref[0], c_ref[0], h_ref[0], z_ref[0], w_ref[0:1, :], w_ref[1:2, :], w_ref[2:3, :])
        g = vjp(dy_ref[0])
        for ref, val in zip((db_ref, dc_ref, dh_ref, dz_ref), g[:4]):
            ref[0] = val.astype(BF16)
        dw = jnp.concatenate([g[4], g[5], g[6], jnp.zeros((5, LANES), F32)], axis=0)
        _acc(dw_ref, dw, pl.program_id(1) == 0)

    out = pl.BlockSpec((1, lay.s, LANES), lambda j, b: (b, 0, j))
    return pl.pallas_call(
        body, grid=(BRW // LANES, lay.b), in_specs=_sc_specs(lay) + [out],
        out_specs=[out] * 4 + [pl.BlockSpec((8, LANES), lambda j, b: (0, j))],
        out_shape=[jax.ShapeDtypeStruct((lay.b, lay.s, BRW), BF16)] * 4 + [jax.ShapeDtypeStruct((8, BRW), F32)],
        compiler_params=_params(("arbitrary", "arbitrary")), name="sc_bwd")(p3, p3, p3, p3, w8, dy)


def _gdn_conv_fn(x, w0, w1, w2, normed, dn, up):
    a = _silu(_conv3(x, w0, w1, w2, dn, up))
    nrm = a * lax.rsqrt(_head_sum(a * a) + EPS)
    return jnp.where(normed, nrm, a)


def _gdn_conv(lay, p3, w8):
    dn, up = _make_shifts(lay.s, lay.t_ctx)
    first = COL_GDN * BRW // LANES

    def body(x_ref, w_ref, o_ref):
        normed = pl.program_id(0) < 2 * BRW // LANES
        o_ref[0] = _gdn_conv_fn(x_ref[0], w_ref[0:1, :], w_ref[1:2, :], w_ref[2:3, :], normed, dn, up)

    return pl.pallas_call(
        body, grid=(3 * BRW // LANES, lay.b),
        in_specs=[pl.BlockSpec((1, lay.s, LANES), lambda j, b: (b, 0, first + j)),
                  pl.BlockSpec((8, LANES), lambda j, b: (0, j))],
        out_specs=pl.BlockSpec((1, lay.s, LANES), lambda j, b: (b, 0, j)),
        out_shape=jax.ShapeDtypeStruct((lay.b, lay.s, 3 * BRW), F32),
        compiler_params=_params(("arbitrary", "arbitrary")), name="gdn_conv")(p3, w8)


def _gdn_conv_bwd(lay, p3, w8, dqkv):
    dn, up = _make_shifts(lay.s, lay.t_ctx)
    first = COL_GDN * BRW // LANES

    def body(x_ref, w_ref, g_ref, dx_ref, dw_ref):
        normed = pl.program_id(0) < 2 * BRW // LANES
        fn = lambda x, w0, w1, w2: _gdn_conv_fn(x, w0, w1, w2, normed, dn, up)
        _, vjp = jax.vjp(fn, x_ref[0], w_ref[0:1, :], w_ref[1:2, :], w_ref[2:3, :])
        g = vjp(g_ref[0])
        dx_ref[0] = g[0].astype(BF16)
        dw = jnp.concatenate([g[1], g[2], g[3], jnp.zeros((5, LANES), F32)], axis=0)
        _acc(dw_ref, dw, pl.program_id(1) == 0)

    blk = pl.BlockSpec((1, lay.s, LANES), lambda j, b: (b, 0, j))
    return pl.pallas_call(
        body, grid=(3 * BRW // LANES, lay.b),
        in_specs=[pl.BlockSpec((1, lay.s, LANES), lambda j, b: (b, 0, first + j)),
                  pl.BlockSpec((8, LANES), lambda j, b: (0, j)), blk],
        out_specs=[blk, pl.BlockSpec((8, LANES), lambda j, b: (0, j))],
        out_shape=[jax.ShapeDtypeStruct((lay.b, lay.s, 3 * BRW), BF16), jax.ShapeDtypeStruct((8, 3 * BRW), F32)],
        compiler_params=_params(("arbitrary", "arbitrary")), name="gdn_conv_bwd")(p3, w8, dqkv)


def _tri_inverse(low):
    n = low.shape[0]
    r, c = _iota((n, n), 0), _iota((n, n), 1)
    t = (r == c).astype(F32)
    s = 1
    while s < GC:
        pair = (r // (2 * s)) == (c // (2 * s))
        off = pair & (((r // s) % 2) != ((c // s) % 2))
        cb = jnp.where(off, low, 0.0)
        t = t - (cb if s == 1 else _bdot(t, _bdot(cb, t, 1, 0), 1, 0))
        s *= 2
    return t


@jax.custom_vjp
def _tri_solve(low, r1, r2):
    t = _tri_inverse(low)
    return _bdot(t, r1, 1, 0), _bdot(t, r2, 1, 0)


def _tri_solve_fwd(low, r1, r2):
    t = _tri_inverse(low)
    x1, x2 = _bdot(t, r1, 1, 0), _bdot(t, r2, 1, 0)
    return (x1, x2), (t, x1, x2)


def _tri_solve_bwd(res, g):
    t, x1, x2 = res
    d1 = _bdot(t, g[0], 0, 0)
    d2 = _bdot(t, g[1], 0, 0)
    dlow = -(_bdot(d1, x1, 1, 1) + _bdot(d2, x2, 1, 1))
    return dlow, d1, d2


_tri_solve.defvjp(_tri_solve_fwd, _tri_solve_bwd)

N_PACK = 5


def _gdn_prep_fn(qn, kn, vv, a, alog, dtb):
    col = _iota((1, LANES), 1)
    xx = a + dtb
    softplus = jnp.maximum(xx, 0.0) + jnp.log(1.0 + jnp.exp(-jnp.abs(xx)))
    g_small = jnp.where(col < 8, -jnp.exp(alog) * softplus, 0.0)
    beta_small = jax.nn.sigmoid(a)
    n = NH * GC
    r, c = _iota((n, n), 0), _iota((n, n), 1)
    same = (r // GC) == (c // GC)
    eye = r == c
    ri, ci = _iota((GC, GC), 0), _iota((GC, GC), 1)
    kst, qst, vst = _stack_heads(kn), _stack_heads(qn), _stack_heads(vv)
    gtot_small = jnp.sum(g_small, axis=0, keepdims=True)

    def column(mat, cc):
        return jnp.sum(jnp.where(col == cc, mat, 0.0), axis=1, keepdims=True)

    packs, cdecs = [], []
    for d in (0, 1):
        tri = (ri >= ci) if d == 0 else (ri <= ci)
        gc_small = _dotf(tri.astype(F32), g_small)
        gcol = jnp.concatenate([column(gc_small, 4 * d + h) for h in range(NH)], axis=0)
        bcol = jnp.concatenate([column(beta_small, 8 + 4 * d + h) for h in range(NH)], axis=0)
        gtot = [column(gtot_small, 4 * d + h) for h in range(NH)]
        gtot_col = jnp.concatenate([jnp.broadcast_to(t, (GC, 1)) for t in gtot], axis=0)
        grow = jnp.sum(jnp.where(eye, gcol, 0.0), axis=0, keepdims=True)
        incl = same & ((r >= c) if d == 0 else (r <= c))
        strict = same & ((r > c) if d == 0 else (r < c))
        decay = jnp.where(incl, jnp.exp(jnp.where(incl, gcol - grow, 0.0)), 0.0)
        kb = kst * bcol
        low = jnp.where(strict, mm_nt(kb, kst) * decay, 0.0)
        eg = jnp.exp(gcol)
        u, w = _tri_solve(low, vst * bcol, kb * eg)
        k_tail = kst * jnp.exp(gtot_col - gcol)
        qs = qst * (HD ** -0.5)
        intra = mm_nt(qs, kst) * decay
        head = _iota((1, BRW), 1) // HD
        cdec = jnp.zeros((1, BRW), F32)
        for h in range(NH):
            cdec = jnp.where(head == h, jnp.exp(gtot[h]), cdec)
        packs += [_unstack_heads(t) for t in (u, w, k_tail, qs * eg, intra)]
        cdecs.append(cdec)
    return tuple(packs), tuple(cdecs)


def _gdn_prep_specs(lay):
    return ([pl.BlockSpec((1, GC, BRW), functools.partial(lambda b, i, c: (b, i, c), c=c)) for c in range(3)]
            + [pl.BlockSpec((1, GC, LANES), lambda b, i: (b, i, COL_A128)),
               pl.BlockSpec((8, LANES), lambda b, i: (0, 0))])


def _gdn_prep(lay, qkv, p3, prm):
    nc = lay.s // GC

    def body(q_ref, k_ref, v_ref, a_ref, prm_ref, pack_ref, cd_ref):
        pack, cd = _gdn_prep_fn(q_ref[0], k_ref[0], v_ref[0], a_ref[0], prm_ref[0:1, :], prm_ref[1:2, :])
        for d in (0, 1):
            for n in range(N_PACK):
                pack_ref[0, 0, d, n] = pack[N_PACK * d + n]
            cd_ref[0, 0, d] = cd[d]

    return pl.pallas_call(
        body, grid=(lay.b, nc), in_specs=_gdn_prep_specs(lay),
        out_specs=[pl.BlockSpec((1, 1, 2, N_PACK, GC, BRW), lambda b, i: (b, i, 0, 0, 0, 0)),
                   pl.BlockSpec((1, 1, 2, 1, BRW), lambda b, i: (b, i, 0, 0, 0))],
        out_shape=[jax.ShapeDtypeStruct((lay.b, nc, 2, N_PACK, GC, BRW), F32),
                   jax.ShapeDtypeStruct((lay.b, nc, 2, 1, BRW), F32)],
        compiler_params=_params(("arbitrary", "arbitrary")), name="gdn_prep")(qkv, qkv, qkv, p3, prm)


def _gdn_prep_bwd(lay, qkv, p3, prm, dpacks, dcds):
    nc = lay.s // GC

    def body(q_ref, k_ref, v_ref, a_ref, prm_ref, dpf_ref, dpr_ref, dcf_ref, dcr_ref, dqkv_ref, da_ref, dprm_ref):
        first = (pl.program_id(0) == 0) & (pl.program_id(1) == 0)
        _, vjp = jax.vjp(_gdn_prep_fn, q_ref[0], k_ref[0], v_ref[0], a_ref[0], prm_ref[0:1, :], prm_ref[1:2, :])
        dpack = tuple(ref[0, 0, n] for ref in (dpf_ref, dpr_ref) for n in range(N_PACK))
        dq, dk, dv, da, dalog, ddtb = vjp((dpack, (dcf_ref[0, 0], dcr_ref[0, 0])))
        dqkv_ref[0, :, 0:BRW] = dq
        dqkv_ref[0, :, BRW:2 * BRW] = dk
        dqkv_ref[0, :, 2 * BRW:] = dv
        da_ref[0] = da.astype(BF16)
        _acc(dprm_ref, jnp.concatenate([dalog, ddtb, jnp.zeros((6, LANES), F32)], axis=0), first)

    return pl.pallas_call(
        body, grid=(lay.b, nc),
        in_specs=_gdn_prep_specs(lay)
        + [pl.BlockSpec((1, 1, N_PACK, GC, BRW), lambda b, i: (b, i, 0, 0, 0))] * 2
        + [pl.BlockSpec((1, 1, 1, BRW), lambda b, i: (b, i, 0, 0))] * 2,
        out_specs=[pl.BlockSpec((1, GC, 3 * BRW), lambda b, i: (b, i, 0)),
                   pl.BlockSpec((1, GC, LANES), lambda b, i: (b, i, 0)),
                   pl.BlockSpec((8, LANES), lambda b, i: (0, 0))],
        out_shape=[jax.ShapeDtypeStruct((lay.b, lay.s, 3 * BRW), F32),
                   jax.ShapeDtypeStruct((lay.b, lay.s, LANES), BF16), jax.ShapeDtypeStruct((8, LANES), F32)],
        compiler_params=_params(("arbitrary", "arbitrary")), name="gdn_prep_bwd",
    )(qkv, qkv, qkv, p3, prm, *dpacks, *dcds)


def _gdn_step_fn(s, u, w, k_tail, qd, intra, cdec):
    u, w, k_tail, qd, intra = [_stack_heads(t) for t in (u, w, k_tail, qd, intra)]
    v_new = u - mm(w, s)
    o = mm(qd, s) + mm(intra, v_new)
    return s * cdec + mm_tn(k_tail, v_new), _unstack_heads(o)


def _order_index(nc_ctx, nc, d, step):
    rev = jnp.where(step < nc_ctx, nc_ctx - 1 - step, nc + nc_ctx - 1 - step)
    return jnp.where(d == 0, step, rev)


def _gdn_scan(lay, pack, cd):
    nc, nc_ctx = lay.s // GC, lay.t_ctx // GC
    chunk = functools.partial(_order_index, nc_ctx, nc)

    def body(pf_ref, pr_ref, cf_ref, cr_ref, of_ref, or_ref, sf_ref, sr_ref, s_scr):
        @pl.when(pl.program_id(0) == 0)
        def _():
            s_scr[...] = jnp.zeros_like(s_scr)

        for b in range(lay.b):
            for d, (p_ref, c_ref, o_ref, st_ref) in enumerate(((pf_ref, cf_ref, of_ref, sf_ref),
                                                               (pr_ref, cr_ref, or_ref, sr_ref))):
                s = s_scr[2 * b + d]
                st_ref[b, 0] = _unstack_heads(s)
                s_new, o = _gdn_step_fn(s, *[p_ref[b, 0, 0, n] for n in range(N_PACK)], c_ref[b, 0, 0])
                o_ref[b, 0] = o
                s_scr[2 * b + d] = s_new

    def pk(d):
        return pl.BlockSpec((lay.b, 1, 1, N_PACK, GC, BRW), lambda t: (0, chunk(d, t), d, 0, 0, 0))

    def cdb(d):
        return pl.BlockSpec((lay.b, 1, 1, 1, BRW), lambda t: (0, chunk(d, t), d, 0, 0))

    def out(d):
        return pl.BlockSpec((lay.b, 1, GC, BRW), lambda t: (0, chunk(d, t), 0, 0))

    return pl.pallas_call(
        body, grid=(nc,), in_specs=[pk(0), pk(1), cdb(0), cdb(1)],
        out_specs=[out(0), out(1), out(0), out(1)],
        out_shape=[jax.ShapeDtypeStruct((lay.b, nc, GC, BRW), F32)] * 4,
        scratch_shapes=[pltpu.VMEM((2 * lay.b, BRW, BRW), F32)],
        compiler_params=_params(("arbitrary",)), name="gdn_scan")(pack, pack, cd, cd)


def _gdn_scan_bwd(lay, pack, cd, states, do):
    nc, nc_ctx = lay.s // GC, lay.t_ctx // GC

    def chunk(d, t):
        return _order_index(nc_ctx, nc, d, nc - 1 - t)

    def body(pf_ref, pr_ref, cf_ref, cr_ref, sf_ref, sr_ref, dof_ref, dor_ref, dpf_ref, dpr_ref, dcf_ref, dcr_ref,
             ds_scr):
        @pl.when(pl.program_id(0) == 0)
        def _():
            ds_scr[...] = jnp.zeros_like(ds_scr)

        for b in range(lay.b):
            for d, (p_ref, c_ref, st_ref, do_ref, dp_ref, dc_ref) in enumerate(
                    ((pf_ref, cf_ref, sf_ref, dof_ref, dpf_ref, dcf_ref),
                     (pr_ref, cr_ref, sr_ref, dor_ref, dpr_ref, dcr_ref))):
                args = [_stack_heads(st_ref[b, 0])] + [p_ref[b, 0, 0, n] for n in range(N_PACK)] + [c_ref[b, 0, 0]]
                _, vjp = jax.vjp(_gdn_step_fn, *args)
                g = vjp((ds_scr[2 * b + d], do_ref[b]))
                ds_scr[2 * b + d] = g[0]
                for n in range(N_PACK):
                    dp_ref[b, 0, n] = g[1 + n]
                dc_ref[b, 0] = g[1 + N_PACK]

    def pk(d):
        return pl.BlockSpec((lay.b, 1, 1, N_PACK, GC, BRW), lambda t: (0, chunk(d, t), d, 0, 0, 0))

    def cdb(d):
        return pl.BlockSpec((lay.b, 1, 1, 1, BRW), lambda t: (0, chunk(d, t), d, 0, 0))

    def st(d):
        return pl.BlockSpec((lay.b, 1, GC, BRW), lambda t: (0, chunk(d, t), 0, 0))

    def dob(d):
        return pl.BlockSpec((lay.b, GC, BRW), lambda t: (0, chunk(d, t), 0))

    def dpk(d):
        return pl.BlockSpec((lay.b, 1, N_PACK, GC, BRW), lambda t: (0, chunk(d, t), 0, 0, 0))

    def dcb(d):
        return pl.BlockSpec((lay.b, 1, 1, BRW), lambda t: (0, chunk(d, t), 0, 0))

    return pl.pallas_call(
        body, grid=(nc,),
        in_specs=[pk(0), pk(1), cdb(0), cdb(1), st(0), st(1), dob(0), dob(1)],
        out_specs=[dpk(0), dpk(1), dcb(0), dcb(1)],
        out_shape=[jax.ShapeDtypeStruct((lay.b, nc, N_PACK, GC, BRW), F32)] * 2
        + [jax.ShapeDtypeStruct((lay.b, nc, 1, BRW), F32)] * 2,
        scratch_shapes=[pltpu.VMEM((2 * lay.b, BRW, BRW), F32)],
        compiler_params=_params(("arbitrary",)), name="gdn_scan_bwd")(pack, pack, cd, cd, *states, do, do)


def _gdn_finish_fn(o, z, ng):
    return o * lax.rsqrt(_head_sum(o * o) * (1.0 / HD) + EPS) * ng * _silu(z)


GF_CHUNKS = 2


def _gdn_o(of_ref, or_ref):
    return jnp.concatenate([of_ref[0, k] + or_ref[0, k] for k in range(GF_CHUNKS)], axis=0)


def _gdn_finish_specs():
    rows = GF_CHUNKS * GC
    ob = pl.BlockSpec((1, GF_CHUNKS, GC, BRW), lambda b, i: (b, i, 0, 0))
    return [ob, ob, pl.BlockSpec((1, rows, BRW), lambda b, i: (b, i, COL_GDN + 3)),
            pl.BlockSpec((1, BRW), lambda b, i: (0, 0))]


def _gdn_finish(lay, o_f, o_r, p3, ng):
    rows = GF_CHUNKS * GC

    def body(of_ref, or_ref, z_ref, ng_ref, y_ref):
        y_ref[0] = _gdn_finish_fn(_gdn_o(of_ref, or_ref), z_ref[0], ng_ref[...]).astype(BF16)

    return pl.pallas_call(
        body, grid=(lay.b, lay.s // rows), in_specs=_gdn_finish_specs(),
        out_specs=pl.BlockSpec((1, rows, BRW), lambda b, i: (b, i, 0)),
        out_shape=jax.ShapeDtypeStruct((lay.b, lay.s, BRW), BF16),
        compiler_params=_params(("arbitrary", "arbitrary")), name="gdn_finish")(o_f, o_r, p3, ng)


def _gdn_finish_bwd(lay, o_f, o_r, p3, ng, dy):
    rows = GF_CHUNKS * GC

    def body(of_ref, or_ref, z_ref, ng_ref, dy_ref, do_ref, dz_ref, dng_ref):
        first = (pl.program_id(0) == 0) & (pl.program_id(1) == 0)
        _, vjp = jax.vjp(_gdn_finish_fn, _gdn_o(of_ref, or_ref), z_ref[0], ng_ref[...])
        do, dz, dng = vjp(dy_ref[0])
        do_ref[0] = do
        dz_ref[0] = dz.astype(BF16)
        _acc(dng_ref, dng, first)

    blk = pl.BlockSpec((1, rows, BRW), lambda b, i: (b, i, 0))
    return pl.pallas_call(
        body, grid=(lay.b, lay.s // rows), in_specs=_gdn_finish_specs() + [blk],
        out_specs=[blk, blk, pl.BlockSpec((1, BRW), lambda b, i: (0, 0))],
        out_shape=[jax.ShapeDtypeStruct((lay.b, lay.s, BRW), F32), jax.ShapeDtypeStruct((lay.b, lay.s, BRW), BF16),
                   jax.ShapeDtypeStruct((1, BRW), F32)],
        compiler_params=_params(("arbitrary", "arbitrary")), name="gdn_finish_bwd")(o_f, o_r, p3, ng, dy)


def _rope_tables(lay):
    t = jnp.arange(lay.t_lat)
    lane = np.arange(BRW)
    dim = lane % HD
    inv = jnp.asarray(ROPE_BASE ** (-(dim % 16).astype(np.float32) / 16.0), F32)
    pos = jnp.where((dim // 32 == 0)[None, :], (t // GRID_W)[:, None], (t % GRID_W)[:, None]).astype(F32)
    ang = pos * inv[None, :]
    cos = jnp.concatenate([jnp.ones((lay.t_ctx, BRW), F32), jnp.cos(ang)], axis=0)
    sin = jnp.concatenate([jnp.zeros((lay.t_ctx, BRW), F32), jnp.sin(ang)], axis=0)
    return cos, sin


def _pad_rows(a, rows):
    return jnp.concatenate([a, jnp.zeros((rows - a.shape[0],) + a.shape[1:], a.dtype)], axis=0)


def _layer_fwd(lay, xc, wl, cos, sin):
    p, h = _inproj_fwd(lay, xc, wl["mod3"], wl["gpre"], wl["win"])
    p3 = p.reshape(lay.b, lay.s, W_PAD)
    nctx = lay.t_ctx // RC
    states = _state_scan(lay, _ret_state(lay, p3, cos, sin), nctx, False, "ret_scan")
    y_ret = _ret_out(lay, p3, cos, sin, states, wl["ret_ng"])
    y_sg = _sg_fwd(lay, p3, wl["sgw"], wl["sgb8"])
    y_sc = _sc_fwd(lay, p3, wl["scw8"])
    qkv = _gdn_conv(lay, p3, wl["gdnw8"])
    pack, cd = _gdn_prep(lay, qkv, p3, wl["prm"])
    o_f, o_r, st_f, st_r = _gdn_scan(lay, pack, cd)
    y_gdn = _gdn_finish(lay, o_f, o_r, p3, wl["gdn_ng"])
    ys = [y.reshape(lay.rows, BRW) for y in (y_ret, y_sg, y_sc, y_gdn)]
    xc_new = _outproj_fwd(lay, ys, xc, wl["wout"], wl["gpost"], wl["mod3"])
    saved = dict(xc=xc, p3=p3, h=h, states=states, qkv=qkv, pack=pack, cd=cd, o_f=o_f, o_r=o_r, gstates=(st_f, st_r),
                 ys=ys)
    return xc_new, saved


def _layer_bwd(lay, sv, wl, cos, sin, dxc):
    p3 = sv["p3"]
    as3 = lambda a: a.reshape(lay.b, lay.s, a.shape[-1])
    as2 = lambda a: a.reshape(lay.rows, a.shape[-1])
    dy_ret, dy_sg, dy_sc, dy_gdn, dwout, dgpost, dgate = _outproj_bwd(
        lay, sv["ys"], sv["xc"], wl["wout"], wl["gpost"], wl["mod3"], dxc)
    nctx = lay.t_ctx // RC
    dpr, dstates, dret_ng = _ret_out_bwd(lay, p3, cos, sin, sv["states"], wl["ret_ng"], as3(dy_ret))
    d_a = _state_scan(lay, dstates, nctx, True, "ret_scan_bwd")
    dp_ret = _ret_state_bwd(lay, p3, cos, sin, d_a, dpr)
    dp_sg, dsgw, dsgb8 = _sg_bwd(lay, p3, wl["sgw"], wl["sgb8"], as3(dy_sg))
    dsb, dsc_, dsh_, dsz, dscw8 = _sc_bwd(lay, p3, wl["scw8"], as3(dy_sc))
    do, dgz, dgdn_ng = _gdn_finish_bwd(lay, sv["o_f"], sv["o_r"], p3, wl["gdn_ng"], as3(dy_gdn))
    dpf, dpr_, dcf, dcr = _gdn_scan_bwd(lay, sv["pack"], sv["cd"], sv["gstates"], do)
    dqkv, da, dprm = _gdn_prep_bwd(lay, sv["qkv"], p3, wl["prm"], (dpf, dpr_), (dcf, dcr))
    dp_gqkv, dgdnw8 = _gdn_conv_bwd(lay, p3, wl["gdnw8"], dqkv)
    pieces = [(as2(dp_ret), 0), (as2(dp_sg), COL_SG * BRW), (as2(dsb), COL_SC * BRW), (as2(dsc_), (COL_SC + 1) * BRW),
              (as2(dsh_), (COL_SC + 2) * BRW), (as2(dsz), (COL_SC + 3) * BRW), (as2(dp_gqkv), COL_GDN * BRW),
              (as2(dgz), (COL_GDN + 3) * BRW), (as2(da), COL_A128 * LANES)]
    dxc_prev, dgpre, dshift, dscale = _inproj_bwd(lay, sv["xc"], wl["mod3"], wl["gpre"], wl["wint"], dxc, pieces)
    dwin = jnp.concatenate([_weight_grad(lay, sv["h"], dp, "win_grad_%d" % off) for dp, off in pieces],
                           axis=1)[:, :W_IN]

    def rows3(g):
        return jnp.concatenate([g[1], g[3], g[0] + g[2]], axis=0)

    dmod = _pad_rows(jnp.concatenate([rows3(dshift), rows3(dscale), rows3(dgate)], axis=1), 8)
    grads = dict(win=dwin, wout=dwout, gpre=dgpre[0], gpost=dgpost[0], ret_ng=dret_ng[0], sgw=dsgw, sgb=dsgb8[:NH],
                 scw=dscw8[:3], gdnw=dgdnw8[:3], alog=dprm[0, :2 * NH].reshape(2, NH),
                 dtb=dprm[1, :2 * NH].reshape(2, NH), gdn_ng=dgdn_ng.reshape(NH, HD).sum(axis=0), dmod=dmod)
    return dxc_prev, grads


def _local_step(x, c, ctx, c_ctx, wmod, bmod, gpre, gpost, win, wout, ret_ng, sgw, sgb, scw, gdnw, alog, dtb,
                gdn_ng, target):
    depth = wmod.shape[0]
    lay = _Lay(x.shape[0], ctx.shape[1], x.shape[1])
    assert lay.b == 2 and lay.t_ctx % RC == 0 and lay.t_lat % RC == 0
    cos, sin = _rope_tables(lay)
    cvec8 = _pad_rows(jnp.concatenate([c, c_ctx[None]], axis=0), 8)
    mod = _mod_fwd(cvec8, wmod, bmod[:, None, :])
    wint = jnp.swapaxes(win, 1, 2)
    xc = jnp.concatenate([ctx, x], axis=1).reshape(lay.rows, D)
    layers, saved = [], []
    for l in range(depth):
        wl = dict(mod3=mod[l].reshape(8, 3, D).transpose(1, 0, 2)[:, :, None, :], gpre=gpre[l][None], gpost=gpost[l][None],
                  win=win[l], wint=wint[l], wout=wout[l], ret_ng=ret_ng[l][None], sgw=sgw[l],
                  sgb8=_pad_rows(sgb[l], 8), scw8=_pad_rows(scw[l], 8), gdnw8=_pad_rows(gdnw[l], 8),
                  prm=_pad_rows(jnp.pad(jnp.stack([alog[l].reshape(-1), dtb[l].reshape(-1)]),
                                        ((0, 0), (0, LANES - 2 * NH))), 8),
                  gdn_ng=jnp.tile(gdn_ng[l], NH)[None])
        xc, sv = _layer_fwd(lay, xc, wl, cos, sin)
        layers.append(wl)
        saved.append(sv)
    loss, dxc3 = _loss_kernel(lay, xc.reshape(lay.b, lay.s, D), target)
    dxc = dxc3.reshape(lay.rows, D)
    grads = [None] * depth
    for l in reversed(range(depth)):
        dxc, grads[l] = _layer_bwd(lay, saved[l], layers[l], cos, sin, dxc)
    stacked = {k: jnp.stack([g[k] for g in grads]) for k in grads[0]}
    dcvec8, dbmod = _mod_bwd(stacked["dmod"], wmod, cvec8)
    stacked["bmod"] = dbmod[:, 0, :]
    stacked["c_ctx"] = dcvec8[2]
    dx = dxc.reshape(lay.b, lay.s, D)[:, lay.t_ctx:, :]
    return loss, dx, stacked, cvec8


MESH = pl.DeviceIdType.MESH
ANY = pl.BlockSpec(memory_space=pl.ANY)


def _me():
    return lax.axis_index("x"), lax.axis_index("y"), lax.axis_index("c")


def _gather_weights(shards, fulls, blocks):
    n = len(shards)

    def body(*refs):
        ins, outs = refs[:n], refs[n:2 * n]
        send_sems, recv_sems, loc_sems = refs[2 * n:]
        x, y, c = _me()
        me, sibling = (x, y, c), (x, y, 1 - c)
        chips = [(1 - x, y), (x, 1 - y), (1 - x, 1 - y)]

        def blk(a, dev):
            return blocks[a](outs[a], 4 * dev[0] + 2 * dev[1] + dev[2])

        def copy(a, k, block, to, src=None):
            return pltpu.make_async_remote_copy(
                src_ref=blk(a, block) if src is None else src, dst_ref=blk(a, block), send_sem=send_sems.at[a, k],
                recv_sem=recv_sems.at[a, k], device_id=to, device_id_type=MESH)

        mine = [pltpu.make_async_copy(ins[a], blk(a, me), loc_sems.at[a]) for a in range(n)]
        for cp in mine:
            cp.start()
        first = []
        for a in range(n):
            first.append(copy(a, 0, me, sibling, src=ins[a]))
            first += [copy(a, 1 + j, me, (*chip, c), src=ins[a]) for j, chip in enumerate(chips)]
        for cp in first:
            cp.start()
        passed = []
        for j, chip in enumerate(chips):
            for a in range(n):
                copy(a, 1 + j, (*chip, c), me).wait_recv()
                fwd = copy(a, 4 + j, (*chip, c), sibling)
                fwd.start()
                passed.append(fwd)
        for a in range(n):
            copy(a, 0, sibling, me).wait_recv()
            for j, chip in enumerate(chips):
                copy(a, 4 + j, (*chip, 1 - c), me).wait_recv()
        for cp in first + passed:
            cp.wait_send()
        for cp in mine:
            cp.wait()

    return pl.pallas_call(
        body, in_specs=[ANY] * n, out_specs=[ANY] * n,
        out_shape=[jax.ShapeDtypeStruct(f, s.dtype) for f, s in zip(fulls, shards)],
        scratch_shapes=[pltpu.SemaphoreType.DMA((n, 7)), pltpu.SemaphoreType.DMA((n, 7)),
                        pltpu.SemaphoreType.DMA((n,))],
        name="gather_weights")(*shards)


def _scatter_pair(srcs, slabs, slab_shapes):
    n = len(srcs)

    def body(*refs):
        ins, outs = refs[:n], refs[n:2 * n]
        send_sems, recv_sems = refs[2 * n:]
        x, y, c = _me()
        cps = []
        for a in range(n):
            for q in range(4):
                j = 2 * q + (1 - c)
                cps.append(pltpu.make_async_remote_copy(
                    src_ref=slabs[a](ins[a], j), dst_ref=outs[a].at[q], send_sem=send_sems.at[a, q],
                    recv_sem=recv_sems.at[a, q], device_id=(x, y, 1 - c), device_id_type=MESH))
        for cp in cps:
            cp.start()
        for cp in cps:
            cp.wait_recv()
        for cp in cps:
            cp.wait_send()

    return pl.pallas_call(
        body, in_specs=[ANY] * n, out_specs=[ANY] * n,
        out_shape=[jax.ShapeDtypeStruct((4,) + tuple(shp), s.dtype) for shp, s in zip(slab_shapes, srcs)],
        scratch_shapes=[pltpu.SemaphoreType.DMA((n, 4)), pltpu.SemaphoreType.DMA((n, 4))],
        name="scatter_pair")(*srcs)


def _scatter_chips(parts):
    n = len(parts)

    def body(*refs):
        ins, outs = refs[:n], refs[n:2 * n]
        send_sems, recv_sems = refs[2 * n:]
        x, y, c = _me()
        chips = [(1 - x, y), (x, 1 - y), (1 - x, 1 - y)]
        cps = []
        for a in range(n):
            for k, (px, py) in enumerate(chips):
                cps.append(pltpu.make_async_remote_copy(
                    src_ref=ins[a].at[2 * px + py], dst_ref=outs[a].at[k], send_sem=send_sems.at[a, k],
                    recv_sem=recv_sems.at[a, k], device_id=(px, py, c), device_id_type=MESH))
        for cp in cps:
            cp.start()
        for cp in cps:
            cp.wait_recv()
        for cp in cps:
            cp.wait_send()

    return pl.pallas_call(
        body, in_specs=[ANY] * n, out_specs=[ANY] * n,
        out_shape=[jax.ShapeDtypeStruct((3,) + p.shape[1:], p.dtype) for p in parts],
        scratch_shapes=[pltpu.SemaphoreType.DMA((n, 3)), pltpu.SemaphoreType.DMA((n, 3))],
        name="scatter_chips")(*parts)


def _add_rows(arrs, out_dtype, name):
    shp = arrs[0].shape
    two = [a.reshape(-1, shp[-1]) for a in arrs]
    rows, cols = two[0].shape
    tr = _row_tile(rows, 1024)

    def body(*refs):
        acc = refs[0][...].astype(F32)
        for r in refs[1:-1]:
            acc = acc + r[...].astype(F32)
        refs[-1][...] = acc.astype(out_dtype)

    blk = pl.BlockSpec((tr, cols), lambda i: (i, 0))
    return pl.pallas_call(
        body, grid=(rows // tr,), in_specs=[blk] * len(two), out_specs=blk,
        out_shape=jax.ShapeDtypeStruct((rows, cols), out_dtype),
        compiler_params=_params(("arbitrary",)), name=name)(*two).reshape(shp)


def _exchange(src, name):
    blk = src.shape[-2:]

    def body(src_ref, out_ref, send_sems, recv_sems, loc_sem):
        x, y, c = lax.axis_index("x"), lax.axis_index("y"), lax.axis_index("c")
        me = 4 * x + 2 * y + c

        def block(j):
            return src_ref

        def remote(k, src_blk, dst_blk, peer_xyz):
            return pltpu.make_async_remote_copy(
                src_ref=block(src_blk), dst_ref=out_ref.at[dst_blk], send_sem=send_sems.at[k], recv_sem=recv_sems.at[k],
                device_id=peer_xyz, device_id_type=pl.DeviceIdType.MESH)

        local = pltpu.make_async_copy(block(me), out_ref.at[me], loc_sem)
        local.start()
        peers = []
        for k in range(1, N_DEV):
            px = 1 - x if k & 4 else x
            py = 1 - y if k & 2 else y
            pc = 1 - c if k & 1 else c
            peers.append((4 * px + 2 * py + pc, (px, py, pc)))
        sends = [remote(k, peer, me, xyz) for k, (peer, xyz) in enumerate(peers)]
        for cp in sends:
            cp.start()
        for k, (peer, xyz) in enumerate(peers):
            remote(k, peer, peer, xyz).wait_recv()
        for cp in sends:
            cp.wait_send()
        local.wait()

    return pl.pallas_call(
        body, in_specs=[pl.BlockSpec(memory_space=pl.ANY)], out_specs=pl.BlockSpec(memory_space=pl.ANY),
        out_shape=jax.ShapeDtypeStruct((N_DEV,) + blk, src.dtype),
        scratch_shapes=[pltpu.SemaphoreType.DMA((N_DEV - 1,)), pltpu.SemaphoreType.DMA((N_DEV - 1,)),
                        pltpu.SemaphoreType.DMA(())],
        name=name)(src)


def _row_tile(rows, cap):
    best = 8
    for t in range(8, min(rows, cap) + 1, 8):
        if rows % t == 0:
            best = t
    return best


def _sum_devices(x):
    _, rows, cols = x.shape
    tr = _row_tile(rows, 2048)

    def body(x_ref, o_ref):
        acc = x_ref[0]
        for j in range(1, N_DEV):
            acc = acc + x_ref[j]
        o_ref[...] = acc

    return pl.pallas_call(
        body, grid=(rows // tr,), in_specs=[pl.BlockSpec((N_DEV, tr, cols), lambda i: (0, i, 0))],
        out_specs=pl.BlockSpec((tr, cols), lambda i: (i, 0)), out_shape=jax.ShapeDtypeStruct((rows, cols), F32),
        compiler_params=_params(("arbitrary",)), name="sum_devices")(x)


def _adamw(w, g, m, v, name):
    rows, cols = w.shape
    tr = _row_tile(rows, 512)
    bc1 = 1.0 - ADAM_B1 ** ADAM_STEP
    bc2 = 1.0 - ADAM_B2 ** ADAM_STEP

    def body(w_ref, g_ref, m_ref, v_ref, d_ref, nm_ref, nv_ref):
        g_ = g_ref[...]
        m_ = ADAM_B1 * m_ref[...] + (1.0 - ADAM_B1) * g_
        v_ = ADAM_B2 * v_ref[...] + (1.0 - ADAM_B2) * (g_ * g_)
        d_ref[...] = -ADAM_LR * ((m_ / bc1) / (jnp.sqrt(v_ / bc2) + ADAM_EPS) + ADAM_WD * w_ref[...])
        nm_ref[...] = m_
        nv_ref[...] = v_

    blk = pl.BlockSpec((tr, cols), lambda i: (i, 0))
    return pl.pallas_call(
        body, grid=(rows // tr,), in_specs=[blk] * 4, out_specs=[blk] * 3,
        out_shape=[jax.ShapeDtypeStruct((rows, cols), F32)] * 3,
        compiler_params=_params(("arbitrary",)), name=name)(w, g, m, v)


def _pack(arrs, dtype=F32):
    flat = jnp.concatenate([a.reshape(-1).astype(dtype) for a in arrs])
    rows = -(-flat.shape[0] // (16 * LANES)) * 16
    flat = jnp.concatenate([flat, jnp.zeros((rows * LANES - flat.shape[0],), dtype)])
    return flat.reshape(rows, LANES)


def _unpack(flat, shapes):
    flat = flat.reshape(-1)
    out, off = [], 0
    for s in shapes:
        n = int(np.prod(s))
        out.append(flat[off:off + n].reshape(s))
        off += n
    return out


SMALL = ("c_ctx", "b_mod", "g_pre", "g_post", "ret_norm_g", "sg_w", "sg_b", "sc_conv_w", "gdn_conv_w", "gdn_a_log",
         "gdn_dt_bias", "gdn_norm_g")
ORDER = ("c_ctx", "w_mod", "b_mod", "g_pre", "g_post", "w_in", "w_out", "ret_norm_g", "sg_w", "sg_b", "sc_conv_w",
         "gdn_conv_w", "gdn_a_log", "gdn_dt_bias", "gdn_norm_g")


def kernel(x, c, ctx, c_ctx, w_mod, b_mod, g_pre, g_post, w_in, w_out, ret_norm_g, sg_w, sg_b, sc_conv_w, gdn_conv_w, gdn_a_log, gdn_dt_bias, gdn_norm_g, loss_target, m_c_ctx, m_w_mod, m_b_mod, m_g_pre, m_g_post, m_w_in, m_w_out, m_ret_norm_g, m_sg_w, m_sg_b, m_sc_conv_w, m_gdn_conv_w, m_gdn_a_log, m_gdn_dt_bias, m_gdn_norm_g, v_c_ctx, v_w_mod, v_b_mod, v_g_pre, v_g_post, v_w_in, v_w_out, v_ret_norm_g, v_sg_w, v_sg_b, v_sc_conv_w, v_gdn_conv_w, v_gdn_a_log, v_gdn_dt_bias, v_gdn_norm_g):
    wts = dict(c_ctx=c_ctx, w_mod=w_mod, b_mod=b_mod, g_pre=g_pre, g_post=g_post, w_in=w_in, w_out=w_out,
               ret_norm_g=ret_norm_g, sg_w=sg_w, sg_b=sg_b, sc_conv_w=sc_conv_w, gdn_conv_w=gdn_conv_w,
               gdn_a_log=gdn_a_log, gdn_dt_bias=gdn_dt_bias, gdn_norm_g=gdn_norm_g)
    mom = dict(c_ctx=m_c_ctx, w_mod=m_w_mod, b_mod=m_b_mod, g_pre=m_g_pre, g_post=m_g_post, w_in=m_w_in, w_out=m_w_out,
               ret_norm_g=m_ret_norm_g, sg_w=m_sg_w, sg_b=m_sg_b, sc_conv_w=m_sc_conv_w, gdn_conv_w=m_gdn_conv_w,
               gdn_a_log=m_gdn_a_log, gdn_dt_bias=m_gdn_dt_bias, gdn_norm_g=m_gdn_norm_g)
    var = dict(c_ctx=v_c_ctx, w_mod=v_w_mod, b_mod=v_b_mod, g_pre=v_g_pre, g_post=v_g_post, w_in=v_w_in, w_out=v_w_out,
               ret_norm_g=v_ret_norm_g, sg_w=v_sg_w, sg_b=v_sg_b, sc_conv_w=v_sc_conv_w, gdn_conv_w=v_gdn_conv_w,
               gdn_a_log=v_gdn_a_log, gdn_dt_bias=v_gdn_dt_bias, gdn_norm_g=v_gdn_norm_g)
    depth = w_mod.shape[0]
    n_mod, n_in, n_out = w_mod.shape[2], w_in.shape[2], w_out.shape[1]
    n_sc, n_gdn = sc_conv_w.shape[2], gdn_conv_w.shape[2]
    xi, yi, ci = _me()
    me = 4 * xi + 2 * yi + ci

    conv = _pack([sc_conv_w, gdn_conv_w])
    n_conv = depth * 3 * n_sc
    shards = [w_mod.astype(BF16), w_in.astype(BF16), w_out.astype(BF16), conv]
    fulls = [(depth, D, N_DEV * n_mod), (N_DEV, depth, D, n_in), (depth, N_DEV * n_out, D), (N_DEV,) + conv.shape]
    blocks = [lambda r, j: r.at[:, :, pl.ds(pl.multiple_of(j * n_mod, LANES), n_mod)],
              lambda r, j: r.at[j],
              lambda r, j: r.at[:, pl.ds(pl.multiple_of(j * n_out, 16), n_out), :],
              lambda r, j: r.at[j]]
    wmod_f, win_g, wout_f, conv_g = _gather_weights(shards, fulls, blocks)
    win_f = jnp.pad(win_g.transpose(1, 2, 0, 3).reshape(depth, D, N_DEV * n_in),
                    ((0, 0), (0, 0), (0, W_PAD - N_DEV * n_in)))
    conv_g = conv_g.reshape(N_DEV, -1)
    scw_f = conv_g[:, :n_conv].reshape(N_DEV, depth, 3, n_sc).transpose(1, 2, 0, 3).reshape(depth, 3, -1)
    gdnw_f = conv_g[:, n_conv:n_conv + depth * 3 * n_gdn].reshape(N_DEV, depth, 3, n_gdn).transpose(1, 2, 0, 3)
    gdnw_f = gdnw_f.reshape(depth, 3, -1)

    loss8, dx, g, cvec8 = _local_step(x, c, ctx, c_ctx, wmod_f, b_mod, g_pre, g_post, win_f, wout_f, ret_norm_g, sg_w,
                                      sg_b, scw_f, gdnw_f, gdn_a_log, gdn_dt_bias, gdn_norm_g, loss_target)

    gin = g["win"].astype(BF16).reshape(depth, D, N_DEV, n_in).transpose(2, 0, 1, 3)
    gout = g["wout"].astype(BF16)
    slabs = [lambda r, j: r.at[j], lambda r, j: r.at[:, pl.ds(pl.multiple_of(j * n_out, 16), n_out), :]]
    got_in, got_out = _scatter_pair([gin, gout], slabs, [(depth, D, n_in), (depth, n_out, D)])
    mine_in = lax.dynamic_index_in_dim(gin.reshape(4, 2, depth, D, n_in), ci, axis=1, keepdims=False)
    mine_out = lax.dynamic_index_in_dim(gout.reshape(depth, 4, 2, n_out, D), ci, axis=2, keepdims=False)
    mine_out = mine_out.transpose(1, 0, 2, 3)
    far_in, far_out = _scatter_chips([_add_rows([mine_in, got_in], BF16, "pair_sum_in"),
                                      _add_rows([mine_out, got_out], BF16, "pair_sum_out")])
    chip = 2 * xi + yi
    own = lambda a: lax.dynamic_index_in_dim(a, chip, axis=0, keepdims=False)
    grad = dict(w_in=_add_rows([own(mine_in), own(got_in), far_in[0], far_in[1], far_in[2]], F32, "grad_sum_in"),
                w_out=_add_rows([own(mine_out), own(got_out), far_out[0], far_out[1], far_out[2]], F32,
                                "grad_sum_out"))

    local_small = dict(c_ctx=g["c_ctx"], b_mod=g["bmod"], g_pre=g["gpre"], g_post=g["gpost"], ret_norm_g=g["ret_ng"],
                       sg_w=g["sgw"], sg_b=g["sgb"], sc_conv_w=g["scw"], gdn_conv_w=g["gdnw"], gdn_a_log=g["alog"],
                       gdn_dt_bias=g["dtb"], gdn_norm_g=g["gdn_ng"])
    to_sum = _pack([loss8[0, :1]] + [local_small[k] for k in SMALL])
    rows_sum = to_sum.shape[0]
    as_is = _pack([cvec8[:3], g["dmod"][:, :3, :]])
    everyone = _exchange(jnp.concatenate([to_sum, as_is], axis=0), "gather_small")
    small_sum = _unpack(_sum_devices(everyone[:, :rows_sum]), [(1,)] + [local_small[k].shape for k in SMALL])
    loss = small_sum[0][0]
    for k, val in zip(SMALL, small_sum[1:]):
        grad[k] = val
    grad["sc_conv_w"] = lax.dynamic_slice_in_dim(grad["sc_conv_w"], me * n_sc, n_sc, axis=2)
    grad["gdn_conv_w"] = lax.dynamic_slice_in_dim(grad["gdn_conv_w"], me * n_gdn, n_gdn, axis=2)
    rest =everyone[:, rows_sum:].reshape(N_DEV, -1)
    c_all = rest[:, :3 * D].reshape(N_DEV * 3, D)
    dmod_all = rest[:, 3 * D:3 * D + depth * 9 * D].reshape(N_DEV, depth, 3, 3 * D).transpose(1, 0, 2, 3)
    dmod_mine = lax.dynamic_slice_in_dim(dmod_all.reshape(depth, N_DEV * 3, 3 * D), me * n_mod, n_mod, axis=2)
    grad["w_mod"] = _wmod_grad(_pad_rows(c_all, 32), jnp.pad(dmod_mine, ((0, 0), (0, 32 - N_DEV * 3), (0, 0))))

    delta, new_m, new_v = {}, {}, {}
    for k in ("w_mod", "w_in", "w_out"):
        shp = wts[k].shape
        two = lambda a: a.reshape(-1, shp[-1])
        res = _adamw(two(wts[k]), two(grad[k]), two(mom[k]), two(var[k]), "adamw_" + k)
        delta[k], new_m[k], new_v[k] = [r.reshape(shp) for r in res]
    res = _adamw(*[_pack([d[k] for k in SMALL]) for d in (wts, grad, mom, var)], "adamw_small")
    for dst, flat in zip((delta, new_m, new_v), res):
        for k, val in zip(SMALL, _unpack(flat, [wts[k].shape for k in SMALL])):
            dst[k] = val
    return (loss, dx, *[grad[k] for k in ORDER], *[delta[k] for k in ORDER], *[new_m[k] for k in ORDER],
            *[new_v[k] for k in ORDER])
```

```python
import functools
import math

import jax
import jax.numpy as jnp
import numpy as np
from jax import lax
from jax.experimental import pallas as pl
from jax.experimental.pallas import tpu as pltpu

F32, BF16 = jnp.float32, jnp.bfloat16
HI = lax.Precision.HIGHEST

N_DEV = 8
D = 1024
DEPTH = 4
BRW = 256
HD = 64
NH = 4
LANES = 128
GRID_W = 64
ROPE_BASE = 10000.0
W_IN = 15 * BRW + 4 * NH
W_PAD = 31 * LANES
RC = 128
GC = 64
EPS = 1e-6
LOG_GAMMA = tuple(math.log(1.0 - 2.0 ** (-5.0 - h)) for h in range(NH))
ADAM_LR, ADAM_B1, ADAM_B2, ADAM_EPS, ADAM_WD, ADAM_STEP = 0.001, 0.9, 0.999, 1e-08, 0.01, 10
VMEM_LIMIT = 56 * 1024 * 1024

COL_RET, COL_SG, COL_SC, COL_GDN = 0, 4, 7, 11
COL_A128 = 30


def _params(sem):
    return pltpu.CompilerParams(dimension_semantics=sem, vmem_limit_bytes=VMEM_LIMIT)


def _bdot(a, b, ca, cb):
    if a.ndim == 3:
        dn = (((ca + 1,), (cb + 1,)), ((0,), (0,)))
    else:
        dn = (((ca,), (cb,)), ((), ()))
    return lax.dot_general(a.astype(BF16), b.astype(BF16), dn, preferred_element_type=F32)


@jax.custom_vjp
def mm(a, b):
    return _bdot(a, b, 1, 0)


mm.defvjp(lambda a, b: (_bdot(a, b, 1, 0), (a, b)),
          lambda r, g: (_bdot(g, r[1], 1, 1), _bdot(r[0], g, 0, 0)))


@jax.custom_vjp
def mm_nt(a, b):
    return _bdot(a, b, 1, 1)


mm_nt.defvjp(lambda a, b: (_bdot(a, b, 1, 1), (a, b)),
             lambda r, g: (_bdot(g, r[1], 1, 0), _bdot(g, r[0], 0, 0)))


@jax.custom_vjp
def mm_tn(a, b):
    return _bdot(a, b, 0, 0)


mm_tn.defvjp(lambda a, b: (_bdot(a, b, 0, 0), (a, b)),
             lambda r, g: (_bdot(r[1], g, 1, 1), _bdot(r[0], g, 1, 0)))


def _dotf(a, b):
    return jnp.dot(a, b, precision=HI, preferred_element_type=F32)


def _iota(shape, dim):
    return lax.broadcasted_iota(jnp.int32, shape, dim)


def _head_mask(h, width=BRW):
    return (_iota((1, width), 1) // HD == h).astype(F32)


def _lane_by_head(vals, width=BRW, lane0=0):
    head = (_iota((1, width), 1) + lane0) // HD
    out = jnp.full((1, width), vals[NH - 1], F32)
    for h in range(NH - 2, -1, -1):
        out = jnp.where(head == h, vals[h], out)
    return out


def _block_diag(n, width):
    return (_iota((n, width), 0) // HD == _iota((n, width), 1) // HD).astype(F32)


@jax.custom_vjp
def _head_sum(x):
    w = x.shape[1]
    ones = _block_diag(w, w).astype(BF16)
    hi = x.astype(BF16)
    lo = (x - hi.astype(F32)).astype(BF16)
    return jnp.dot(hi, ones, preferred_element_type=F32) + jnp.dot(lo, ones, preferred_element_type=F32)


_head_sum.defvjp(lambda x: (_head_sum(x), None), lambda _, g: (_head_sum(g),))


def _silu(x):
    return x * jax.nn.sigmoid(x)


def _stack_heads(x):
    return jnp.concatenate([x * _head_mask(h) for h in range(NH)], axis=-2)


@jax.custom_vjp
def _unstack_heads(x):
    n = x.shape[-2] // NH
    return (x[..., 0:n, :] + x[..., n:2 * n, :]) + (x[..., 2 * n:3 * n, :] + x[..., 3 * n:4 * n, :])


_unstack_heads.defvjp(lambda x: (_unstack_heads(x), None), lambda _, g: (_stack_heads(g),))


@jax.custom_vjp
def _rot_half(x):
    n = x.shape[1]
    first = (_iota(x.shape, 1) % 32) < 16
    return jnp.where(first, -pltpu.roll(x, n - 16, 1), pltpu.roll(x, 16, 1))


_rot_half.defvjp(lambda x: (_rot_half(x), None), lambda _, g: (-_rot_half(g),))


def _rotary(x, cos, sin):
    return x * cos + _rot_half(x) * sin


def _make_shifts(seq, t_ctx):
    def dn_raw(x):
        r = _iota(x.shape, 0)
        return jnp.where((r == 0) | (r == t_ctx), 0.0, pltpu.roll(x, 1, 0))

    def up_raw(x):
        r = _iota(x.shape, 0)
        return jnp.where((r == t_ctx - 1) | (r == seq - 1), 0.0, pltpu.roll(x, seq - 1, 0))

    @jax.custom_vjp
    def dn(x):
        return dn_raw(x)

    @jax.custom_vjp
    def up(x):
        return up_raw(x)

    dn.defvjp(lambda x: (dn_raw(x), None), lambda _, g: (up_raw(g),))
    up.defvjp(lambda x: (up_raw(x), None), lambda _, g: (dn_raw(g),))
    return dn, up


def _conv3(t, w0, w1, w2, dn, up):
    return dn(t) * w0 + t * w1 + up(t) * w2


def _acc(ref, val, first, at=()):
    idx = at + (Ellipsis,)

    @pl.when(first)
    def _():
        ref[idx] = val

    @pl.when(jnp.logical_not(first))
    def _():
        ref[idx] += val


def _mod_fwd(cvec8, wmod, bmod):
    depth = wmod.shape[0]

    def body(c_ref, w_ref, b_ref, o_ref):
        sc = _silu(c_ref[...])
        o_ref[0] = jnp.dot(sc.astype(BF16), w_ref[0], preferred_element_type=F32) + b_ref[0]

    return pl.pallas_call(
        body, grid=(depth, 3),
        in_specs=[pl.BlockSpec((8, D), lambda l, j: (0, 0)),
                  pl.BlockSpec((1, D, D), lambda l, j: (l, 0, j)),
                  pl.BlockSpec((1, 1, D), lambda l, j: (l, 0, j))],
        out_specs=pl.BlockSpec((1, 8, D), lambda l, j: (l, 0, j)),
        out_shape=jax.ShapeDtypeStruct((depth, 8, 3 * D), F32),
        compiler_params=_params(("arbitrary", "arbitrary")), name="mod_fwd")(cvec8, wmod, bmod)


def _mod_bwd(dmod, wmod, cvec8):
    depth = wmod.shape[0]

    def body(dm_ref, w_ref, c_ref, dc_ref, db_ref):
        l, j = pl.program_id(0), pl.program_id(1)
        dm = dm_ref[0]
        db_ref[0] = jnp.sum(dm, axis=0, keepdims=True)
        part = _bdot(dm, w_ref[0], 1, 1)
        _acc(dc_ref, part, (l == 0) & (j == 0))

        @pl.when((l == depth - 1) & (j == 2))
        def _():
            c = c_ref[...]
            s = jax.nn.sigmoid(c)
            dc_ref[...] = dc_ref[...] * (s * (1.0 + c * (1.0 - s)))

    return pl.pallas_call(
        body, grid=(depth, 3),
        in_specs=[pl.BlockSpec((1, 8, D), lambda l, j: (l, 0, j)),
                  pl.BlockSpec((1, D, D), lambda l, j: (l, 0, j)),
                  pl.BlockSpec((8, D), lambda l, j: (0, 0))],
        out_specs=[pl.BlockSpec((8, D), lambda l, j: (0, 0)),
                   pl.BlockSpec((1, 1, D), lambda l, j: (l, 0, j))],
        out_shape=[jax.ShapeDtypeStruct((8, D), F32), jax.ShapeDtypeStruct((depth, 1, 3 * D), F32)],
        compiler_params=_params(("arbitrary", "arbitrary")), name="mod_bwd")(dmod, wmod, cvec8)


def _wmod_grad(c_rows, dmod_cols):
    depth, rows, n = dmod_cols.shape

    def body(c_ref, dm_ref, o_ref):
        sc = _silu(c_ref[...])
        o_ref[0] = lax.dot_general(sc, dm_ref[0], (((0,), (0,)), ((), ())), precision=HI,
                                   preferred_element_type=F32)

    return pl.pallas_call(
        body, grid=(depth,),
        in_specs=[pl.BlockSpec((rows, D), lambda l: (0, 0)), pl.BlockSpec((1, rows, n), lambda l: (l, 0, 0))],
        out_specs=pl.BlockSpec((1, D, n), lambda l: (l, 0, 0)),
        out_shape=jax.ShapeDtypeStruct((depth, D, n), F32),
        compiler_params=_params(("arbitrary",)), name="wmod_grad")(c_rows, dmod_cols)


class _Lay:
    def __init__(self, batch, t_ctx, t_lat):
        self.b, self.t_ctx, self.t_lat = batch, t_ctx, t_lat
        self.s = t_ctx + t_lat
        self.tm = min(256, t_ctx)
        self.tpb = self.s // self.tm
        self.nct = t_ctx // self.tm
        self.ntiles = batch * self.tpb
        self.rows = batch * self.s

    def mod_row(self, i):
        return jnp.where(i % self.tpb < self.nct, 2, i // self.tpb)

    def group(self, i):
        return 2 * (i // self.tpb) + jnp.where(i % self.tpb < self.nct, 0, 1)

    def group_first(self, i):
        return (i % self.tpb == 0) | (i % self.tpb == self.nct)


def _norm_mod(x, g, shift, scale):
    r = lax.rsqrt(jnp.mean(x * x, axis=-1, keepdims=True) + EPS)
    return (x * r * g) * (1.0 + scale) + shift


def _inproj_fwd(lay, xc, mod3, gpre, w):
    tm = lay.tm

    def body(x_ref, sh_ref, sc_ref, g_ref, w_ref, p_ref, h_ref):
        h = _norm_mod(x_ref[...], g_ref[...], sh_ref[0, 0], sc_ref[0, 0]).astype(BF16)
        h_ref[...] = h
        p_ref[...] = jnp.dot(h, w_ref[...], preferred_element_type=F32)

    return pl.pallas_call(
        body, grid=(lay.ntiles,),
        in_specs=[pl.BlockSpec((tm, D), lambda i: (i, 0)),
                  pl.BlockSpec((1, 1, 1, D), lambda i: (0, lay.mod_row(i), 0, 0)),
                  pl.BlockSpec((1, 1, 1, D), lambda i: (1, lay.mod_row(i), 0, 0)),
                  pl.BlockSpec((1, D), lambda i: (0, 0)),
                  pl.BlockSpec((D, W_PAD), lambda i: (0, 0))],
        out_specs=[pl.BlockSpec((tm, W_PAD), lambda i: (i, 0)), pl.BlockSpec((tm, D), lambda i: (i, 0))],
        out_shape=[jax.ShapeDtypeStruct((lay.rows, W_PAD), F32), jax.ShapeDtypeStruct((lay.rows, D), BF16)],
        compiler_params=_params(("arbitrary",)), name="inproj_fwd")(xc, mod3, mod3, gpre, w)


def _inproj_bwd(lay, xc, mod3, gpre, wt, dxc, pieces):
    tm = lay.tm
    npc = len(pieces)
    offs = [off for _, off in pieces]

    def body(*refs):
        x_ref, sh_ref, sc_ref, g_ref, wt_ref, dx_in = refs[:6]
        dps = refs[6:6 + npc]
        dx_ref, dg_ref, dsh_ref, dsc_ref = refs[6 + npc:]
        i = pl.program_id(0)
        dh = None
        for dp_ref, off in zip(dps, offs):
            wd = dp_ref.shape[1]
            part = jnp.dot(dp_ref[...], wt_ref[off:off + wd, :], preferred_element_type=F32)
            dh = part if dh is None else dh + part
        _, vjp = jax.vjp(_norm_mod, x_ref[...], g_ref[...], sh_ref[0, 0], sc_ref[0, 0])
        dx, dg, dsh, dsc = vjp(dh)
        dx_ref[...] = dx_in[...] + dx
        _acc(dg_ref, dg, i == 0)
        first = lay.group_first(i)
        _acc(dsh_ref, dsh, first, at=(0,))
        _acc(dsc_ref, dsc, first, at=(0,))

    return pl.pallas_call(
        body, grid=(lay.ntiles,),
        in_specs=[pl.BlockSpec((tm, D), lambda i: (i, 0)),
                  pl.BlockSpec((1, 1, 1, D), lambda i: (0, lay.mod_row(i), 0, 0)),
                  pl.BlockSpec((1, 1, 1, D), lambda i: (1, lay.mod_row(i), 0, 0)),
                  pl.BlockSpec((1, D), lambda i: (0, 0)),
                  pl.BlockSpec((W_PAD, D), lambda i: (0, 0)),
                  pl.BlockSpec((tm, D), lambda i: (i, 0))]
        + [pl.BlockSpec((tm, dp.shape[1]), lambda i: (i, 0)) for dp, _ in pieces],
        out_specs=[pl.BlockSpec((tm, D), lambda i: (i, 0)),
                   pl.BlockSpec((1, D), lambda i: (0, 0)),
                   pl.BlockSpec((1, 1, D), lambda i: (lay.group(i), 0, 0)),
                   pl.BlockSpec((1, 1, D), lambda i: (lay.group(i), 0, 0))],
        out_shape=[jax.ShapeDtypeStruct((lay.rows, D), F32), jax.ShapeDtypeStruct((1, D), F32),
                   jax.ShapeDtypeStruct((2 * lay.b, 1, D), F32), jax.ShapeDtypeStruct((2 * lay.b, 1, D), F32)],
        compiler_params=_params(("arbitrary",)), name="inproj_bwd",
    )(xc, mod3, mod3, gpre, wt, dxc, *[dp for dp, _ in pieces])


def _weight_grad(lay, h, dp, name):
    wd = dp.shape[1]
    tn = 512 if wd % 512 == 0 else (256 if wd % 256 == 0 else LANES)
    tr = lay.rows // 3 if lay.rows % (3 * 256) == 0 else lay.tm

    def body(h_ref, dp_ref, o_ref):
        part = lax.dot_general(h_ref[...], dp_ref[...], (((0,), (0,)), ((), ())), preferred_element_type=F32)
        _acc(o_ref, part, pl.program_id(1) == 0)

    return pl.pallas_call(
        body, grid=(wd // tn, lay.rows // tr),
        in_specs=[pl.BlockSpec((tr, D), lambda j, i: (i, 0)), pl.BlockSpec((tr, tn), lambda j, i: (i, j))],
        out_specs=pl.BlockSpec((D, tn), lambda j, i: (0, j)),
        out_shape=jax.ShapeDtypeStruct((D, wd), F32),
        compiler_params=_params(("arbitrary", "arbitrary")), name=name)(h, dp)


def _outproj_fn(y0, y1, y2, y3, x, w0, w1, w2, w3, gpost, gate):
    o = mm(y0, w0) + mm(y1, w1) + mm(y2, w2) + mm(y3, w3)
    r = lax.rsqrt(jnp.mean(o * o, axis=-1, keepdims=True) + EPS)
    return x + gate * (o * r * gpost)


def _outproj_specs(lay):
    tm = lay.tm
    return ([pl.BlockSpec((tm, BRW), lambda i: (i, 0))] * 4
            + [pl.BlockSpec((tm, D), lambda i: (i, 0))]
            + [pl.BlockSpec((D, D), lambda i: (0, 0))]
            + [pl.BlockSpec((1, D), lambda i: (0, 0))]
            + [pl.BlockSpec((1, 1, 1, D), lambda i: (2, lay.mod_row(i), 0, 0))])


def _outproj_args(x_ref_or_none, ys, w_ref, g_ref, gt_ref):
    ws = [w_ref[BRW * k:BRW * (k + 1), :].astype(F32) for k in range(4)]
    return [y[...].astype(F32) for y in ys] + [x_ref_or_none[...]] + ws + [g_ref[...], gt_ref[0, 0]]


def _outproj_fwd(lay, ys, xc, wout, gpost, mod3):
    tm = lay.tm

    def body(y0, y1, y2, y3, x_ref, w_ref, g_ref, gt_ref, o_ref):
        o_ref[...] = _outproj_fn(*_outproj_args(x_ref, (y0, y1, y2, y3), w_ref, g_ref, gt_ref))

    return pl.pallas_call(
        body, grid=(lay.ntiles,), in_specs=_outproj_specs(lay),
        out_specs=pl.BlockSpec((tm, D), lambda i: (i, 0)),
        out_shape=jax.ShapeDtypeStruct((lay.rows, D), F32),
        compiler_params=_params(("arbitrary",)), name="outproj_fwd")(*ys, xc, wout, gpost, mod3)


def _outproj_bwd(lay, ys, xc, wout, gpost, mod3, dxc):
    tm = lay.tm

    def body(y0, y1, y2, y3, x_ref, w_ref, g_ref, gt_ref, dx_ref, d0, d1, d2, d3, dw_ref, dg_ref, dgt_ref):
        i = pl.program_id(0)
        args = _outproj_args(x_ref, (y0, y1, y2, y3), w_ref, g_ref, gt_ref)
        _, vjp = jax.vjp(_outproj_fn, *args)
        g = vjp(dx_ref[...])
        for k, d in enumerate((d0, d1, d2, d3)):
            d[...] = g[k]

        @pl.when(i == 0)
        def _():
            for k in range(4):
                dw_ref[BRW * k:BRW * (k + 1), :] = g[5 + k]

        @pl.when(i != 0)
        def _():
            for k in range(4):
                dw_ref[BRW * k:BRW * (k + 1), :] += g[5 + k]

        _acc(dg_ref, g[9], i == 0)
        _acc(dgt_ref, g[10], lay.group_first(i), at=(0,))

    return pl.pallas_call(
        body, grid=(lay.ntiles,),
        in_specs=_outproj_specs(lay) + [pl.BlockSpec((tm, D), lambda i: (i, 0))],
        out_specs=[pl.BlockSpec((tm, BRW), lambda i: (i, 0))] * 4
        + [pl.BlockSpec((D, D), lambda i: (0, 0)), pl.BlockSpec((1, D), lambda i: (0, 0)),
           pl.BlockSpec((1, 1, D), lambda i: (lay.group(i), 0, 0))],
        out_shape=[jax.ShapeDtypeStruct((lay.rows, BRW), F32)] * 4
        + [jax.ShapeDtypeStruct((D, D), F32), jax.ShapeDtypeStruct((1, D), F32),
           jax.ShapeDtypeStruct((2 * lay.b, 1, D), F32)],
        compiler_params=_params(("arbitrary",)), name="outproj_bwd")(*ys, xc, wout, gpost, mod3, dxc)


def _loss_kernel(lay, xc3, target):
    tm, nct = lay.tm, lay.nct

    def body(x_ref, t_ref, loss_ref, dx_ref):
        b, i = pl.program_id(0), pl.program_id(1)
        lat = i >= nct
        err = x_ref[0] - t_ref[0]
        dx_ref[0] = jnp.where(lat, err * (1.0 / D), 0.0)
        part = jnp.sum(jnp.sum(err * err, axis=1, keepdims=True), axis=0, keepdims=True) * (0.5 / D)
        part = jnp.broadcast_to(jnp.where(lat, part, 0.0), (8, LANES))
        _acc(loss_ref, part, (b == 0) & (i == 0))

    return pl.pallas_call(
        body, grid=(lay.b, lay.tpb),
        in_specs=[pl.BlockSpec((1, tm, D), lambda b, i: (b, i, 0)),
                  pl.BlockSpec((1, tm, D), lambda b, i: (b, jnp.maximum(i - nct, 0), 0))],
        out_specs=[pl.BlockSpec((8, LANES), lambda b, i: (0, 0)), pl.BlockSpec((1, tm, D), lambda b, i: (b, i, 0))],
        out_shape=[jax.ShapeDtypeStruct((8, LANES), F32), jax.ShapeDtypeStruct(xc3.shape, F32)],
        compiler_params=_params(("arbitrary", "arbitrary")), name="loss")(xc3, target)


def _chunk_orders(n_ctx, n_all):
    fwd = list(range(n_all))
    rev = list(range(n_ctx - 1, -1, -1)) + list(range(n_all - 1, n_ctx - 1, -1))
    return fwd, rev


def _ret_state_fn(k, v, cos, sin):
    kt = _rotary(k, cos, sin) * (HD ** -0.5)
    lg = _lane_by_head(LOG_GAMMA)
    j = _iota((RC, 1), 0).astype(F32)
    bd = _block_diag(BRW, BRW)
    af = mm_tn(kt * jnp.exp((RC - 1.0 - j) * lg), v) * bd
    ar = mm_tn(kt * jnp.exp(j * lg), v) * bd
    return af, ar


def _ret_out_fn(q, k, v, z, cos, sin, sf, sr, ng):
    qt = _rotary(q, cos, sin)
    kt = _rotary(k, cos, sin) * (HD ** -0.5)
    diff = (_iota((RC, RC), 0) - _iota((RC, RC), 1)).astype(F32)
    o = None
    for h in range(NH):
        m = _head_mask(h)
        sc = mm_nt(qt * m, kt)
        wgt = sc * jnp.exp(-jnp.abs(diff) * (-LOG_GAMMA[h])) * jnp.where(diff == 0, 2.0, 1.0)
        part = mm(wgt, v * m)
        o = part if o is None else o + part
    lg = _lane_by_head(LOG_GAMMA)
    i = _iota((RC, 1), 0).astype(F32)
    o = o + mm(qt, sf) * jnp.exp((i + 1.0) * lg) + mm(qt, sr) * jnp.exp((RC - i) * lg)
    mu = _head_sum(o) * (1.0 / HD)
    cen = o - mu
    var = _head_sum(cen * cen) * (1.0 / HD)
    return cen * lax.rsqrt(var + EPS) * ng * _silu(z)


def _ret_specs(lay, cols):
    return [pl.BlockSpec((1, RC, BRW), functools.partial(lambda b, i, c: (b, i, c), c=COL_RET + c)) for c in cols]


def _ret_state(lay, p3, cos, sin):
    nc = lay.s // RC

    def body(k_ref, v_ref, c_ref, s_ref, a_ref):
        af, ar = _ret_state_fn(k_ref[0], v_ref[0], c_ref[...], s_ref[...])
        a_ref[0, 0, 0] = af
        a_ref[0, 0, 1] = ar

    tab = pl.BlockSpec((RC, BRW), lambda b, i: (i, 0))
    return pl.pallas_call(
        body, grid=(lay.b, nc), in_specs=_ret_specs(lay, (1, 2)) + [tab, tab],
        out_specs=pl.BlockSpec((1, 1, 2, BRW, BRW), lambda b, i: (b, i, 0, 0, 0)),
        out_shape=jax.ShapeDtypeStruct((lay.b, nc, 2, BRW, BRW), F32),
        compiler_params=_params(("arbitrary", "arbitrary")), name="ret_state")(p3, p3, cos, sin)


def _ret_state_bwd(lay, p3, cos, sin, d_a, dpr):
    nc = lay.s // RC

    def body(k_ref, v_ref, c_ref, s_ref, da_ref, dpr_ref, o_ref):
        _, vjp = jax.vjp(lambda k, v: _ret_state_fn(k, v, c_ref[...], s_ref[...]), k_ref[0], v_ref[0])
        dk, dv = vjp((da_ref[0, 0, 0], da_ref[0, 0, 1]))
        o_ref[0, :, 0:BRW] = dpr_ref[0, :, 0:BRW].astype(BF16)
        o_ref[0, :, BRW:2 * BRW] = (dpr_ref[0, :, BRW:2 * BRW] + dk).astype(BF16)
        o_ref[0, :, 2 * BRW:3 * BRW] = (dpr_ref[0, :, 2 * BRW:3 * BRW] + dv).astype(BF16)
        o_ref[0, :, 3 * BRW:] = dpr_ref[0, :, 3 * BRW:].astype(BF16)

    tab = pl.BlockSpec((RC, BRW), lambda b, i: (i, 0))
    return pl.pallas_call(
        body, grid=(lay.b, nc),
        in_specs=_ret_specs(lay, (1, 2)) + [tab, tab,
                                            pl.BlockSpec((1, 1, 2, BRW, BRW), lambda b, i: (b, i, 0, 0, 0)),
                                            pl.BlockSpec((1, RC, 4 * BRW), lambda b, i: (b, i, 0))],
        out_specs=pl.BlockSpec((1, RC, 4 * BRW), lambda b, i: (b, i, 0)),
        out_shape=jax.ShapeDtypeStruct((lay.b, lay.s, 4 * BRW), BF16),
        compiler_params=_params(("arbitrary", "arbitrary")), name="ret_state_bwd")(p3, p3, cos, sin, d_a, dpr)


def _state_scan(lay, a, nc_ctx, transpose, name):
    b, nc = a.shape[0], a.shape[1]
    orders = _chunk_orders(nc_ctx, nc)

    def body(a_ref, o_ref):
        d, jh = pl.program_id(1), pl.program_id(2)
        head = (_iota((1, LANES), 1) + jh * LANES) // HD
        lg = jnp.full((1, LANES), LOG_GAMMA[NH - 1], F32)
        for h in range(NH - 2, -1, -1):
            lg = jnp.where(head == h, LOG_GAMMA[h], lg)
        dec = jnp.exp(RC * lg)
        for dd in (0, 1):
            @pl.when(d == dd)
            def _(order=orders[dd]):
                acc = jnp.zeros((BRW, LANES), F32)
                if not transpose:
                    for c in order:
                        o_ref[0, c, 0] = acc
                        acc = acc * dec + a_ref[0, c, 0]
                else:
                    for c in reversed(order):
                        o_ref[0, c, 0] = acc
                        acc = a_ref[0, c, 0] + acc * dec

    spec = pl.BlockSpec((1, nc, 1, BRW, LANES), lambda bb, d, jh: (bb, 0, d, 0, jh))
    return pl.pallas_call(
        body, grid=(b, 2, BRW // LANES), in_specs=[spec], out_specs=spec,
        out_shape=jax.ShapeDtypeStruct(a.shape, F32),
        compiler_params=_params(("arbitrary",) * 3), name=name)(a)


def _ret_out(lay, p3, cos, sin, states, ng):
    nc = lay.s // RC

    def body(q_ref, k_ref, v_ref, z_ref, c_ref, s_ref, st_ref, ng_ref, y_ref):
        y = _ret_out_fn(q_ref[0], k_ref[0], v_ref[0], z_ref[0], c_ref[...], s_ref[...],
                        st_ref[0, 0, 0], st_ref[0, 0, 1], ng_ref[...])
        y_ref[0] = y.astype(BF16)

    tab = pl.BlockSpec((RC, BRW), lambda b, i: (i, 0))
    return pl.pallas_call(
        body, grid=(lay.b, nc),
        in_specs=_ret_specs(lay, (0, 1, 2, 3)) + [tab, tab,
                                                  pl.BlockSpec((1, 1, 2, BRW, BRW), lambda b, i: (b, i, 0, 0, 0)),
                                                  pl.BlockSpec((1, BRW), lambda b, i: (0, 0))],
        out_specs=pl.BlockSpec((1, RC, BRW), lambda b, i: (b, i, 0)),
        out_shape=jax.ShapeDtypeStruct((lay.b, lay.s, BRW), BF16),
        compiler_params=_params(("arbitrary", "arbitrary")), name="ret_out")(p3, p3, p3, p3, cos, sin, states, ng)


def _ret_out_bwd(lay, p3, cos, sin, states, ng, dy):
    nc = lay.s // RC

    def body(q_ref, k_ref, v_ref, z_ref, c_ref, s_ref, st_ref, ng_ref, dy_ref, dp_ref, dst_ref, dng_ref):
        b, i = pl.program_id(0), pl.program_id(1)
        fn = lambda q, k, v, z, sf, sr, ng: _ret_out_fn(q, k, v, z, c_ref[...], s_ref[...], sf, sr, ng)
        _, vjp = jax.vjp(fn, q_ref[0], k_ref[0], v_ref[0], z_ref[0], st_ref[0, 0, 0], st_ref[0, 0, 1], ng_ref[...])
        dq, dk, dv, dz, dsf, dsr, dng = vjp(dy_ref[0])
        for n, g in enumerate((dq, dk, dv, dz)):
            dp_ref[0, :, BRW * n:BRW * (n + 1)] = g
        dst_ref[0, 0, 0] = dsf
        dst_ref[0, 0, 1] = dsr
        _acc(dng_ref, dng, (b == 0) & (i == 0))

    tab = pl.BlockSpec((RC, BRW), lambda b, i: (i, 0))
    st = pl.BlockSpec((1, 1, 2, BRW, BRW), lambda b, i: (b, i, 0, 0, 0))
    return pl.pallas_call(
        body, grid=(lay.b, nc),
        in_specs=_ret_specs(lay, (0, 1, 2, 3)) + [tab, tab, st, pl.BlockSpec((1, BRW), lambda b, i: (0, 0)),
                                                  pl.BlockSpec((1, RC, BRW), lambda b, i: (b, i, 0))],
        out_specs=[pl.BlockSpec((1, RC, 4 * BRW), lambda b, i: (b, i, 0)), st,
                   pl.BlockSpec((1, BRW), lambda b, i: (0, 0))],
        out_shape=[jax.ShapeDtypeStruct((lay.b, lay.s, 4 * BRW), F32),
                   jax.ShapeDtypeStruct(states.shape, F32), jax.ShapeDtypeStruct((1, BRW), F32)],
        compiler_params=_params(("arbitrary", "arbitrary")), name="ret_out_bwd",
    )(p3, p3, p3, p3, cos, sin, states, ng, dy)


def _sg_fn(u, v, z, w0, w1, w2, w3, b8):
    ug = jax.nn.gelu(u)
    vg = jax.nn.gelu(v)
    mu = jnp.mean(vg, axis=-1, keepdims=True)
    cen = vg - mu
    vn = cen * lax.rsqrt(jnp.mean(cen * cen, axis=-1, keepdims=True) + EPS)
    s = None
    for h, w in enumerate((w0, w1, w2, w3)):
        part = mm(w, vn * _head_mask(h))
        s = part if s is None else s + part
    expand = (_iota((8, BRW), 1) // HD == _iota((8, BRW), 0)).astype(F32)
    bias = lax.dot_general(b8, expand, (((0,), (0,)), ((), ())), precision=HI, preferred_element_type=F32)
    return ug * (s + bias) * _silu(z)


def _sg_specs():
    return ([pl.BlockSpec((1, RC, BRW), functools.partial(lambda b, i, c: (b, i, c), c=COL_SG + c)) for c in range(3)]
            + [pl.BlockSpec((NH, RC, RC), lambda b, i: (0, 0, 0)), pl.BlockSpec((8, RC), lambda b, i: (0, 0))])


def _sg_fwd(lay, p3, sgw, sgb8):
    nc = lay.s // RC

    def body(u_ref, v_ref, z_ref, w_ref, b_ref, y_ref):
        y = _sg_fn(u_ref[0], v_ref[0], z_ref[0], w_ref[0], w_ref[1], w_ref[2], w_ref[3], b_ref[...])
        y_ref[0] = y.astype(BF16)

    return pl.pallas_call(
        body, grid=(lay.b, nc), in_specs=_sg_specs(),
        out_specs=pl.BlockSpec((1, RC, BRW), lambda b, i: (b, i, 0)),
        out_shape=jax.ShapeDtypeStruct((lay.b, lay.s, BRW), BF16),
        compiler_params=_params(("arbitrary", "arbitrary")), name="sg_fwd")(p3, p3, p3, sgw, sgb8)


def _sg_bwd(lay, p3, sgw, sgb8, dy):
    nc = lay.s // RC

    def body(u_ref, v_ref, z_ref, w_ref, b_ref, dy_ref, dp_ref, dw_ref, db_ref):
        first = (pl.program_id(0) == 0) & (pl.program_id(1) == 0)
        _, vjp = jax.vjp(_sg_fn, u_ref[0], v_ref[0], z_ref[0], w_ref[0], w_ref[1], w_ref[2], w_ref[3], b_ref[...])
        g = vjp(dy_ref[0])
        for n in range(3):
            dp_ref[0, :, BRW * n:BRW * (n + 1)] = g[n].astype(BF16)
        for h in range(NH):
            _acc(dw_ref, g[3 + h], first, at=(h,))
        _acc(db_ref, g[7], first)

    return pl.pallas_call(
        body, grid=(lay.b, nc),
        in_specs=_sg_specs() + [pl.BlockSpec((1, RC, BRW), lambda b, i: (b, i, 0))],
        out_specs=[pl.BlockSpec((1, RC, 3 * BRW), lambda b, i: (b, i, 0)),
                   pl.BlockSpec((NH, RC, RC), lambda b, i: (0, 0, 0)), pl.BlockSpec((8, RC), lambda b, i: (0, 0))],
        out_shape=[jax.ShapeDtypeStruct((lay.b, lay.s, 3 * BRW), BF16),
                   jax.ShapeDtypeStruct((NH, RC, RC), F32), jax.ShapeDtypeStruct((8, RC), F32)],
        compiler_params=_params(("arbitrary", "arbitrary")), name="sg_bwd")(p3, p3, p3, sgw, sgb8, dy)


def _sc_specs(lay):
    first = COL_SC * BRW // LANES
    blk = [pl.BlockSpec((1, lay.s, LANES), functools.partial(lambda j, b, c: (b, 0, c + j), c=first + 2 * n))
           for n in range(4)]
    return blk + [pl.BlockSpec((8, LANES), lambda j, b: (0, j))]


def _sc_fwd(lay, p3, w8):
    dn, up = _make_shifts(lay.s, lay.t_ctx)

    def fn(b_, c_, h_, z_, w0, w1, w2):
        return b_ * _conv3(c_ * h_, w0, w1, w2, dn, up) * _silu(z_)

    def body(b_ref, c_ref, h_ref, z_ref, w_ref, y_ref):
        y = fn(b_ref[0], c_ref[0], h_ref[0], z_ref[0], w_ref[0:1, :], w_ref[1:2, :], w_ref[2:3, :])
        y_ref[0] = y.astype(BF16)

    return pl.pallas_call(
        body, grid=(BRW // LANES, lay.b), in_specs=_sc_specs(lay),
        out_specs=pl.BlockSpec((1, lay.s, LANES), lambda j, b: (b, 0, j)),
        out_shape=jax.ShapeDtypeStruct((lay.b, lay.s, BRW), BF16),
        compiler_params=_params(("arbitrary", "arbitrary")), name="sc_fwd")(p3, p3, p3, p3, w8)


def _sc_bwd(lay, p3, w8, dy):
    dn, up = _make_shifts(lay.s, lay.t_ctx)

    def fn(b_, c_, h_, z_, w0, w1, w2):
        return b_ * _conv3(c_ * h_, w0, w1, w2, dn, up) * _silu(z_)

    def body(b_ref, c_ref, h_ref, z_ref, w_ref, dy_ref, db_ref, dc_ref, dh_ref, dz_ref, dw_ref):
        _, vjp = jax.vjp(fn, b_ref[0], c_ref[0], h_ref[0], z_ref[0], w_ref[0:1, :], w_ref[1:2, :], w_ref[2:3, :])
        g = vjp(dy_ref[0])
        for ref, val in zip((db_ref, dc_ref, dh_ref, dz_ref), g[:4]):
            ref[0] = val.astype(BF16)
        dw = jnp.concatenate([g[4], g[5], g[6], jnp.zeros((5, LANES), F32)], axis=0)
        _acc(dw_ref, dw, pl.program_id(1) == 0)

    out = pl.BlockSpec((1, lay.s, LANES), lambda j, b: (b, 0, j))
    return pl.pallas_call(
        body, grid=(BRW // LANES, lay.b), in_specs=_sc_specs(lay) + [out],
        out_specs=[out] * 4 + [pl.BlockSpec((8, LANES), lambda j, b: (0, j))],
        out_shape=[jax.ShapeDtypeStruct((lay.b, lay.s, BRW), BF16)] * 4 + [jax.ShapeDtypeStruct((8, BRW), F32)],
        compiler_params=_params(("arbitrary", "arbitrary")), name="sc_bwd")(p3, p3, p3, p3, w8, dy)


def _gdn_conv_fn(x, w0, w1, w2, normed, dn, up):
    a = _silu(_conv3(x, w0, w1, w2, dn, up))
    nrm = a * lax.rsqrt(_head_sum(a * a) + EPS)
    return jnp.where(normed, nrm, a)


def _gdn_conv(lay, p3, w8):
    dn, up = _make_shifts(lay.s, lay.t_ctx)
    first = COL_GDN * BRW // LANES

    def body(x_ref, w_ref, o_ref):
        normed = pl.program_id(0) < 2 * BRW // LANES
        o_ref[0] = _gdn_conv_fn(x_ref[0], w_ref[0:1, :], w_ref[1:2, :], w_ref[2:3, :], normed, dn, up)

    return pl.pallas_call(
        body, grid=(3 * BRW // LANES, lay.b),
        in_specs=[pl.BlockSpec((1, lay.s, LANES), lambda j, b: (b, 0, first + j)),
                  pl.BlockSpec((8, LANES), lambda j, b: (0, j))],
        out_specs=pl.BlockSpec((1, lay.s, LANES), lambda j, b: (b, 0, j)),
        out_shape=jax.ShapeDtypeStruct((lay.b, lay.s, 3 * BRW), F32),
        compiler_params=_params(("arbitrary", "arbitrary")), name="gdn_conv")(p3, w8)


def _gdn_conv_bwd(lay, p3, w8, dqkv):
    dn, up = _make_shifts(lay.s, lay.t_ctx)
    first = COL_GDN * BRW // LANES

    def body(x_ref, w_ref, g_ref, dx_ref, dw_ref):
        normed = pl.program_id(0) < 2 * BRW // LANES
        fn = lambda x, w0, w1, w2: _gdn_conv_fn(x, w0, w1, w2, normed, dn, up)
        _, vjp = jax.vjp(fn, x_ref[0], w_ref[0:1, :], w_ref[1:2, :], w_ref[2:3, :])
        g = vjp(g_ref[0])
        dx_ref[0] = g[0].astype(BF16)
        dw = jnp.concatenate([g[1], g[2], g[3], jnp.zeros((5, LANES), F32)], axis=0)
        _acc(dw_ref, dw, pl.program_id(1) == 0)

    blk = pl.BlockSpec((1, lay.s, LANES), lambda j, b: (b, 0, j))
    return pl.pallas_call(
        body, grid=(3 * BRW // LANES, lay.b),
        in_specs=[pl.BlockSpec((1, lay.s, LANES), lambda j, b: (b, 0, first + j)),
                  pl.BlockSpec((8, LANES), lambda j, b: (0, j)), blk],
        out_specs=[blk, pl.BlockSpec((8, LANES), lambda j, b: (0, j))],
        out_shape=[jax.ShapeDtypeStruct((lay.b, lay.s, 3 * BRW), BF16), jax.ShapeDtypeStruct((8, 3 * BRW), F32)],
        compiler_params=_params(("arbitrary", "arbitrary")), name="gdn_conv_bwd")(p3, w8, dqkv)


def _tri_inverse(low):
    i, j = _iota(low.shape, low.ndim - 2), _iota(low.shape, low.ndim - 1) % GC
    t = (i == j).astype(F32)
    s = 1
    while s < GC:
        pair = (i // (2 * s)) == (j // (2 * s))
        off = pair & (((i // s) % 2) != ((j // s) % 2))
        cb = jnp.where(off, low, 0.0)
        t = t - (cb if s == 1 else _bdot(t, _stack_heads(_bdot(cb, _stack_heads(t), 1, 0)), 1, 0))
        s *= 2
    return t


@jax.custom_vjp
def _tri_solve(low, r1, r2):
    t = _tri_inverse(low)
    return _bdot(t, _stack_heads(r1), 1, 0), _bdot(t, _stack_heads(r2), 1, 0)


def _tri_solve_fwd(low, r1, r2):
    t = _tri_inverse(low)
    x1, x2 = _bdot(t, _stack_heads(r1), 1, 0), _bdot(t, _stack_heads(r2), 1, 0)
    return (x1, x2), (t, x1, x2)


def _tri_solve_bwd(res, g):
    t, x1, x2 = res
    bd = _block_diag(BRW, BRW)
    d1 = _unstack_heads(_bdot(t, g[0], 0, 0) * bd)
    d2 = _unstack_heads(_bdot(t, g[1], 0, 0) * bd)
    dlow = -(_bdot(d1, _stack_heads(x1), 1, 1) + _bdot(d2, _stack_heads(x2), 1, 1))
    return dlow, d1, d2


_tri_solve.defvjp(_tri_solve_fwd, _tri_solve_bwd)

N_PACK = 5


def _gdn_prep_fn(qn, kn, vv, a, alog, dtb):
    n = qn.shape[0]
    col = _iota((1, 1, LANES), 2)
    xx = a + dtb
    softplus = jnp.maximum(xx, 0.0) + jnp.log(1.0 + jnp.exp(-jnp.abs(xx)))
    g_small = jnp.where(col < 8, -jnp.exp(alog) * softplus, 0.0).reshape(n * GC, LANES)
    beta_small = jax.nn.sigmoid(a).reshape(n * GC, LANES)
    sel_col, sel_head = _iota((LANES, BRW), 0), _iota((LANES, BRW), 1) // HD
    g_l, b_l = [], []
    for d in (0, 1):
        g_l.append(_dotf(g_small, (sel_col == 4 * d + sel_head).astype(F32)))
        b_l.append(_dotf(beta_small, (sel_col == 8 + 4 * d + sel_head).astype(F32)))
    g_l = jnp.concatenate(g_l, axis=0).reshape(2 * n, GC, BRW)
    b_l = jnp.concatenate(b_l, axis=0).reshape(2 * n, GC, BRW)
    rev = _iota((2 * n, 1, 1), 0) >= n
    fwd = jnp.logical_not(rev)
    ri, ci = _iota((1, GC, GC), 1), _iota((1, GC, GC), 2)
    tri = ((fwd & (ri >= ci)) | (rev & (ri <= ci))).astype(F32)
    gc_l = lax.dot_general(tri, g_l, (((2,), (1,)), ((0,), (0,))), precision=HI,
                           preferred_element_type=F32)
    gtot_l = jnp.sum(g_l, axis=1, keepdims=True)
    i, j = _iota((1, GC, BRW), 1), _iota((1, GC, BRW), 2) % GC
    gc_t = jnp.sum(jnp.where(i == j, gc_l, 0.0), axis=1, keepdims=True)
    incl = (fwd & (i >= j)) | (rev & (i <= j))
    strict = (fwd & (i > j)) | (rev & (i < j))
    decay = jnp.where(incl, jnp.exp(jnp.where(incl, gc_l - gc_t, 0.0)), 0.0)
    kn2 = jnp.concatenate([kn, kn], axis=0)
    vv2 = jnp.concatenate([vv, vv], axis=0)
    qs = jnp.concatenate([qn, qn], axis=0) * (HD ** -0.5)
    kst = _stack_heads(kn2)
    kb = kn2 * b_l
    low = jnp.where(strict, mm_nt(kb, kst) * decay, 0.0)
    eg = jnp.exp(gc_l)
    u, w = _tri_solve(low, vv2 * b_l, kb * eg)
    k_tail = kn2 * jnp.exp(gtot_l - gc_l)
    intra = mm_nt(qs, kst) * decay
    return (u, w, k_tail, qs * eg, intra), jnp.exp(gtot_l)


def _prep_chunks(lay):
    return 4 if (lay.s // GC) % 4 == 0 else 2


def _gdn_prep_specs(lay):
    rows = _prep_chunks(lay) * GC
    return ([pl.BlockSpec((1, rows, BRW), functools.partial(lambda b, i, c: (b, i, c), c=c)) for c in range(3)]
            + [pl.BlockSpec((1, rows, LANES), lambda b, i: (b, i, COL_A128)),
               pl.BlockSpec((8, LANES), lambda b, i: (0, 0))])


def _gdn_prep(lay, qkv, p3, prm):
    nc, per = lay.s // GC, _prep_chunks(lay)

    def body(q_ref, k_ref, v_ref, a_ref, prm_ref, pack_ref, cd_ref):
        chunks = lambda ref: ref[0].reshape(per, GC, ref.shape[-1])
        pack, cd = _gdn_prep_fn(chunks(q_ref), chunks(k_ref), chunks(v_ref), chunks(a_ref),
                                prm_ref[0:1, :], prm_ref[1:2, :])
        for d in (0, 1):
            for n in range(N_PACK):
                pack_ref[0, :, d, n] = pack[n][per * d:per * (d + 1)]
            cd_ref[0, :, d] = cd[per * d:per * (d + 1)]

    return pl.pallas_call(
        body, grid=(lay.b, nc // per), in_specs=_gdn_prep_specs(lay),
        out_specs=[pl.BlockSpec((1, per, 2, N_PACK, GC, BRW), lambda b, i: (b, i, 0, 0, 0, 0)),
                   pl.BlockSpec((1, per, 2, 1, BRW), lambda b, i: (b, i, 0, 0, 0))],
        out_shape=[jax.ShapeDtypeStruct((lay.b, nc, 2, N_PACK, GC, BRW), F32),
                   jax.ShapeDtypeStruct((lay.b, nc, 2, 1, BRW), F32)],
        compiler_params=_params(("arbitrary", "arbitrary")), name="gdn_prep")(qkv, qkv, qkv, p3, prm)


def _gdn_prep_bwd(lay, qkv, p3, prm, dpacks, dcds):
    nc, per = lay.s // GC, _prep_chunks(lay)

    def body(q_ref, k_ref, v_ref, a_ref, prm_ref, dpf_ref, dpr_ref, dcf_ref, dcr_ref, dqkv_ref, da_ref, dprm_ref):
        first = (pl.program_id(0) == 0) & (pl.program_id(1) == 0)
        chunks = lambda ref: ref[0].reshape(per, GC, ref.shape[-1])
        _, vjp = jax.vjp(_gdn_prep_fn, chunks(q_ref), chunks(k_ref), chunks(v_ref), chunks(a_ref),
                         prm_ref[0:1, :], prm_ref[1:2, :])
        dpack = tuple(jnp.concatenate([dpf_ref[0, :, n], dpr_ref[0, :, n]], axis=0) for n in range(N_PACK))
        dq, dk, dv, da, dalog, ddtb = vjp((dpack, jnp.concatenate([dcf_ref[0], dcr_ref[0]], axis=0)))
        dqkv_ref[0, :, 0:BRW] = dq.reshape(per * GC, BRW)
        dqkv_ref[0, :, BRW:2 * BRW] = dk.reshape(per * GC, BRW)
        dqkv_ref[0, :, 2 * BRW:] = dv.reshape(per * GC, BRW)
        da_ref[0] = da.reshape(per * GC, LANES).astype(BF16)
        _acc(dprm_ref, jnp.concatenate([dalog, ddtb, jnp.zeros((6, LANES), F32)], axis=0), first)

    rows_blk = per * GC
    return pl.pallas_call(
        body, grid=(lay.b, nc // per),
        in_specs=_gdn_prep_specs(lay)
        + [pl.BlockSpec((1, per, N_PACK, GC, BRW), lambda b, i: (b, i, 0, 0, 0))] * 2
        + [pl.BlockSpec((1, per, 1, BRW), lambda b, i: (b, i, 0, 0))] * 2,
        out_specs=[pl.BlockSpec((1, rows_blk, 3 * BRW), lambda b, i: (b, i, 0)),
                   pl.BlockSpec((1, rows_blk, LANES), lambda b, i: (b, i, 0)),
                   pl.BlockSpec((8, LANES), lambda b, i: (0, 0))],
        out_shape=[jax.ShapeDtypeStruct((lay.b, lay.s, 3 * BRW), F32),
                   jax.ShapeDtypeStruct((lay.b, lay.s, LANES), BF16), jax.ShapeDtypeStruct((8, LANES), F32)],
        compiler_params=_params(("arbitrary", "arbitrary")), name="gdn_prep_bwd",
    )(qkv, qkv, qkv, p3, prm, *dpacks, *dcds)


def _gdn_step_fn(s, u, w, k_tail, qd, intra, cdec):
    v_new = u - mm(w, s)
    o = mm(qd, s) + mm(intra, _stack_heads(v_new))
    return s * cdec + mm_tn(k_tail, v_new) * _block_diag(BRW, BRW), o


def _order_index(nc_ctx, nc, d, step):
    rev = jnp.where(step < nc_ctx, nc_ctx - 1 - step, nc + nc_ctx - 1 - step)
    return jnp.where(d == 0, step, rev)


def _gdn_scan(lay, pack, cd):
    nc, nc_ctx = lay.s // GC, lay.t_ctx // GC
    chunk = functools.partial(_order_index, nc_ctx, nc)

    def body(pf_ref, pr_ref, cf_ref, cr_ref, of_ref, or_ref, sf_ref, sr_ref, s_scr):
        @pl.when(pl.program_id(0) == 0)
        def _():
            s_scr[...] = jnp.zeros_like(s_scr)

        nb = lay.b
        s = s_scr[...]
        st = _unstack_heads(s)
        sf_ref[:, 0] = st[:nb]
        sr_ref[:, 0] = st[nb:]
        args = [jnp.concatenate([pf_ref[:, 0, 0, n], pr_ref[:, 0, 0, n]], axis=0) for n in range(N_PACK)]
        s_new, o = _gdn_step_fn(s, *args, jnp.concatenate([cf_ref[:, 0, 0], cr_ref[:, 0, 0]], axis=0))
        of_ref[:, 0] = o[:nb]
        or_ref[:, 0] = o[nb:]
        s_scr[...] = s_new

    def pk(d):
        return pl.BlockSpec((lay.b, 1, 1, N_PACK, GC, BRW), lambda t: (0, chunk(d, t), d, 0, 0, 0))

    def cdb(d):
        return pl.BlockSpec((lay.b, 1, 1, 1, BRW), lambda t: (0, chunk(d, t), d, 0, 0))

    def out(d):
        return pl.BlockSpec((lay.b, 1, GC, BRW), lambda t: (0, chunk(d, t), 0, 0))

    return pl.pallas_call(
        body, grid=(nc,), in_specs=[pk(0), pk(1), cdb(0), cdb(1)],
        out_specs=[out(0), out(1), out(0), out(1)],
        out_shape=[jax.ShapeDtypeStruct((lay.b, nc, GC, BRW), F32)] * 4,
        scratch_shapes=[pltpu.VMEM((2 * lay.b, BRW, BRW), F32)],
        compiler_params=_params(("arbitrary",)), name="gdn_scan")(pack, pack, cd, cd)


def _gdn_scan_bwd(lay, pack, cd, states, do):
    nc, nc_ctx = lay.s // GC, lay.t_ctx // GC

    def chunk(d, t):
        return _order_index(nc_ctx, nc, d, nc - 1 - t)

    def body(pf_ref, pr_ref, cf_ref, cr_ref, sf_ref, sr_ref, dof_ref, dor_ref, dpf_ref, dpr_ref, dcf_ref, dcr_ref,
             ds_scr):
        @pl.when(pl.program_id(0) == 0)
        def _():
            ds_scr[...] = jnp.zeros_like(ds_scr)

        nb = lay.b
        both = lambda f, r: jnp.concatenate([f, r], axis=0)
        args = ([_stack_heads(both(sf_ref[:, 0], sr_ref[:, 0]))]
                + [both(pf_ref[:, 0, 0, n], pr_ref[:, 0, 0, n]) for n in range(N_PACK)]
                + [both(cf_ref[:, 0, 0], cr_ref[:, 0, 0])])
        _, vjp = jax.vjp(_gdn_step_fn, *args)
        g = vjp((ds_scr[...], both(dof_ref[...], dor_ref[...])))
        ds_scr[...] = g[0]
        for n in range(N_PACK):
            dpf_ref[:, 0, n] = g[1 + n][:nb]
            dpr_ref[:, 0, n] = g[1 + n][nb:]
        dcf_ref[:, 0] = g[1 + N_PACK][:nb]
        dcr_ref[:, 0] = g[1 + N_PACK][nb:]

    def pk(d):
        return pl.BlockSpec((lay.b, 1, 1, N_PACK, GC, BRW), lambda t: (0, chunk(d, t), d, 0, 0, 0))

    def cdb(d):
        return pl.BlockSpec((lay.b, 1, 1, 1, BRW), lambda t: (0, chunk(d, t), d, 0, 0))

    def st(d):
        return pl.BlockSpec((lay.b, 1, GC, BRW), lambda t: (0, chunk(d, t), 0, 0))

    def dob(d):
        return pl.BlockSpec((lay.b, GC, BRW), lambda t: (0, chunk(d, t), 0))

    def dpk(d):
        return pl.BlockSpec((lay.b, 1, N_PACK, GC, BRW), lambda t: (0, chunk(d, t), 0, 0, 0))

    def dcb(d):
        return pl.BlockSpec((lay.b, 1, 1, BRW), lambda t: (0, chunk(d, t), 0, 0))

    return pl.pallas_call(
        body, grid=(nc,),
        in_specs=[pk(0), pk(1), cdb(0), cdb(1), st(0), st(1), dob(0), dob(1)],
        out_specs=[dpk(0), dpk(1), dcb(0), dcb(1)],
        out_shape=[jax.ShapeDtypeStruct((lay.b, nc, N_PACK, GC, BRW), F32)] * 2
        + [jax.ShapeDtypeStruct((lay.b, nc, 1, BRW), F32)] * 2,
        scratch_shapes=[pltpu.VMEM((2 * lay.b, BRW, BRW), F32)],
        compiler_params=_params(("arbitrary",)), name="gdn_scan_bwd")(pack, pack, cd, cd, *states, do, do)


def _gdn_finish_fn(o, z, ng):
    return o * lax.rsqrt(_head_sum(o * o) * (1.0 / HD) + EPS) * ng * _silu(z)


GF_CHUNKS = 2


def _gdn_o(of_ref, or_ref):
    return jnp.concatenate([of_ref[0, k] + or_ref[0, k] for k in range(GF_CHUNKS)], axis=0)


def _gdn_finish_specs():
    rows = GF_CHUNKS * GC
    ob = pl.BlockSpec((1, GF_CHUNKS, GC, BRW), lambda b, i: (b, i, 0, 0))
    return [ob, ob, pl.BlockSpec((1, rows, BRW), lambda b, i: (b, i, COL_GDN + 3)),
            pl.BlockSpec((1, BRW), lambda b, i: (0, 0))]


def _gdn_finish(lay, o_f, o_r, p3, ng):
    rows = GF_CHUNKS * GC

    def body(of_ref, or_ref, z_ref, ng_ref, y_ref):
        y_ref[0] = _gdn_finish_fn(_gdn_o(of_ref, or_ref), z_ref[0], ng_ref[...]).astype(BF16)

    return pl.pallas_call(
        body, grid=(lay.b, lay.s // rows), in_specs=_gdn_finish_specs(),
        out_specs=pl.BlockSpec((1, rows, BRW), lambda b, i: (b, i, 0)),
        out_shape=jax.ShapeDtypeStruct((lay.b, lay.s, BRW), BF16),
        compiler_params=_params(("arbitrary", "arbitrary")), name="gdn_finish")(o_f, o_r, p3, ng)


def _gdn_finish_bwd(lay, o_f, o_r, p3, ng, dy):
    rows = GF_CHUNKS * GC

    def body(of_ref, or_ref, z_ref, ng_ref, dy_ref, do_ref, dz_ref, dng_ref):
        first = (pl.program_id(0) == 0) & (pl.program_id(1) == 0)
        _, vjp = jax.vjp(_gdn_finish_fn, _gdn_o(of_ref, or_ref), z_ref[0], ng_ref[...])
        do, dz, dng = vjp(dy_ref[0])
        do_ref[0] = do
        dz_ref[0] = dz.astype(BF16)
        _acc(dng_ref, dng, first)

    blk = pl.BlockSpec((1, rows, BRW), lambda b, i: (b, i, 0))
    return pl.pallas_call(
        body, grid=(lay.b, lay.s // rows), in_specs=_gdn_finish_specs() + [blk],
        out_specs=[blk, blk, pl.BlockSpec((1, BRW), lambda b, i: (0, 0))],
        out_shape=[jax.ShapeDtypeStruct((lay.b, lay.s, BRW), F32), jax.ShapeDtypeStruct((lay.b, lay.s, BRW), BF16),
                   jax.ShapeDtypeStruct((1, BRW), F32)],
        compiler_params=_params(("arbitrary", "arbitrary")), name="gdn_finish_bwd")(o_f, o_r, p3, ng, dy)


def _rope_tables(lay):
    t = jnp.arange(lay.t_lat)
    lane = np.arange(BRW)
    dim = lane % HD
    inv = jnp.asarray(ROPE_BASE ** (-(dim % 16).astype(np.float32) / 16.0), F32)
    pos = jnp.where((dim // 32 == 0)[None, :], (t // GRID_W)[:, None], (t % GRID_W)[:, None]).astype(F32)
    ang = pos * inv[None, :]
    cos = jnp.concatenate([jnp.ones((lay.t_ctx, BRW), F32), jnp.cos(ang)], axis=0)
    sin = jnp.concatenate([jnp.zeros((lay.t_ctx, BRW), F32), jnp.sin(ang)], axis=0)
    return cos, sin


def _pad_rows(a, rows):
    return jnp.concatenate([a, jnp.zeros((rows - a.shape[0],) + a.shape[1:], a.dtype)], axis=0)


def _layer_fwd(lay, xc, wl, cos, sin):
    p, h = _inproj_fwd(lay, xc, wl["mod3"], wl["gpre"], wl["win"])
    p3 = p.reshape(lay.b, lay.s, W_PAD)
    nctx = lay.t_ctx // RC
    states = _state_scan(lay, _ret_state(lay, p3, cos, sin), nctx, False, "ret_scan")
    y_ret = _ret_out(lay, p3, cos, sin, states, wl["ret_ng"])
    y_sg = _sg_fwd(lay, p3, wl["sgw"], wl["sgb8"])
    y_sc = _sc_fwd(lay, p3, wl["scw8"])
    qkv = _gdn_conv(lay, p3, wl["gdnw8"])
    pack, cd = _gdn_prep(lay, qkv, p3, wl["prm"])
    o_f, o_r, st_f, st_r = _gdn_scan(lay, pack, cd)
    y_gdn = _gdn_finish(lay, o_f, o_r, p3, wl["gdn_ng"])
    ys = [y.reshape(lay.rows, BRW) for y in (y_ret, y_sg, y_sc, y_gdn)]
    xc_new = _outproj_fwd(lay, ys, xc, wl["wout"], wl["gpost"], wl["mod3"])
    saved = dict(xc=xc, p3=p3, h=h, states=states, qkv=qkv, pack=pack, cd=cd, o_f=o_f, o_r=o_r, gstates=(st_f, st_r),
                 ys=ys)
    return xc_new, saved


def _layer_bwd(lay, sv, wl, cos, sin, dxc):
    p3 = sv["p3"]
    as3 = lambda a: a.reshape(lay.b, lay.s, a.shape[-1])
    as2 = lambda a: a.reshape(lay.rows, a.shape[-1])
    dy_ret, dy_sg, dy_sc, dy_gdn, dwout, dgpost, dgate = _outproj_bwd(
        lay, sv["ys"], sv["xc"], wl["wout"], wl["gpost"], wl["mod3"], dxc)
    nctx = lay.t_ctx // RC
    dpr, dstates, dret_ng = _ret_out_bwd(lay, p3, cos, sin, sv["states"], wl["ret_ng"], as3(dy_ret))
    d_a = _state_scan(lay, dstates, nctx, True, "ret_scan_bwd")
    dp_ret = _ret_state_bwd(lay, p3, cos, sin, d_a, dpr)
    dp_sg, dsgw, dsgb8 = _sg_bwd(lay, p3, wl["sgw"], wl["sgb8"], as3(dy_sg))
    dsb, dsc_, dsh_, dsz, dscw8 = _sc_bwd(lay, p3, wl["scw8"], as3(dy_sc))
    do, dgz, dgdn_ng = _gdn_finish_bwd(lay, sv["o_f"], sv["o_r"], p3, wl["gdn_ng"], as3(dy_gdn))
    dpf, dpr_, dcf, dcr = _gdn_scan_bwd(lay, sv["pack"], sv["cd"], sv["gstates"], do)
    dqkv, da, dprm = _gdn_prep_bwd(lay, sv["qkv"], p3, wl["prm"], (dpf, dpr_), (dcf, dcr))
    dp_gqkv, dgdnw8 = _gdn_conv_bwd(lay, p3, wl["gdnw8"], dqkv)
    pieces = [(as2(dp_ret), 0), (as2(dp_sg), COL_SG * BRW), (as2(dsb), COL_SC * BRW), (as2(dsc_), (COL_SC + 1) * BRW),
              (as2(dsh_), (COL_SC + 2) * BRW), (as2(dsz), (COL_SC + 3) * BRW), (as2(dp_gqkv), COL_GDN * BRW),
              (as2(dgz), (COL_GDN + 3) * BRW), (as2(da), COL_A128 * LANES)]
    dxc_prev, dgpre, dshift, dscale = _inproj_bwd(lay, sv["xc"], wl["mod3"], wl["gpre"], wl["wint"], dxc, pieces)
    dws = [_weight_grad(lay, sv["h"], dp, "win_grad_%d" % off) for dp, off in pieces]
    dwin = jnp.concatenate(dws[:-1] + [dws[-1][:, :W_IN - COL_A128 * LANES]], axis=1)

    def rows3(g):
        return jnp.concatenate([g[1], g[3], g[0] + g[2]], axis=0)

    dmod = _pad_rows(jnp.concatenate([rows3(dshift), rows3(dscale), rows3(dgate)], axis=1), 8)
    grads = dict(win=dwin, wout=dwout, gpre=dgpre[0], gpost=dgpost[0], ret_ng=dret_ng[0], sgw=dsgw, sgb=dsgb8[:NH],
                 scw=dscw8[:3], gdnw=dgdnw8[:3], alog=dprm[0, :2 * NH].reshape(2, NH),
                 dtb=dprm[1, :2 * NH].reshape(2, NH), gdn_ng=dgdn_ng.reshape(NH, HD).sum(axis=0), dmod=dmod)
    return dxc_prev, grads


def _local_step(x, c, ctx, c_ctx, wmod, bmod, gpre, gpost, win, wout, ret_ng, sgw, sgb, scw, gdnw, alog, dtb,
                gdn_ng, target):
    depth = wmod.shape[0]
    lay = _Lay(x.shape[0], ctx.shape[1], x.shape[1])
    assert lay.b == 2 and lay.t_ctx % RC == 0 and lay.t_lat % RC == 0
    cos, sin = _rope_tables(lay)
    cvec8 = _pad_rows(jnp.concatenate([c, c_ctx[None]], axis=0), 8)
    mod = _mod_fwd(cvec8, wmod, bmod[:, None, :])
    wint = jnp.swapaxes(win, 1, 2)
    xc = jnp.concatenate([ctx, x], axis=1).reshape(lay.rows, D)
    layers, saved = [], []
    for l in range(depth):
        wl = dict(mod3=mod[l].reshape(8, 3, D).transpose(1, 0, 2)[:, :, None, :], gpre=gpre[l][None], gpost=gpost[l][None],
                  win=win[l], wint=wint[l], wout=wout[l], ret_ng=ret_ng[l][None], sgw=sgw[l],
                  sgb8=_pad_rows(sgb[l], 8), scw8=_pad_rows(scw[l], 8), gdnw8=_pad_rows(gdnw[l], 8),
                  prm=_pad_rows(jnp.pad(jnp.stack([alog[l].reshape(-1), dtb[l].reshape(-1)]),
                                        ((0, 0), (0, LANES - 2 * NH))), 8),
                  gdn_ng=jnp.tile(gdn_ng[l], NH)[None])
        xc, sv = _layer_fwd(lay, xc, wl, cos, sin)
        layers.append(wl)
        saved.append(sv)
    loss, dxc3 = _loss_kernel(lay, xc.reshape(lay.b, lay.s, D), target)
    dxc = dxc3.reshape(lay.rows, D)
    grads = [None] * depth
    for l in reversed(range(depth)):
        dxc, grads[l] = _layer_bwd(lay, saved[l], layers[l], cos, sin, dxc)
    stacked = {k: jnp.stack([g[k] for g in grads]) for k in grads[0]}
    dcvec8, dbmod = _mod_bwd(stacked["dmod"], wmod, cvec8)
    stacked["bmod"] = dbmod[:, 0, :]
    stacked["c_ctx"] = dcvec8[2]
    dx = dxc.reshape(lay.b, lay.s, D)[:, lay.t_ctx:, :]
    return loss, dx, stacked, cvec8


MESH = pl.DeviceIdType.MESH
ANY = pl.BlockSpec(memory_space=pl.ANY)


def _me():
    return lax.axis_index("x"), lax.axis_index("y"), lax.axis_index("c")


def _gather_weights(shards, fulls, blocks):
    n = len(shards)

    def body(*refs):
        ins, outs = refs[:n], refs[n:2 * n]
        send_sems, recv_sems, loc_sems = refs[2 * n:]
        x, y, c = _me()
        me, sibling = (x, y, c), (x, y, 1 - c)
        chips = [(1 - x, y), (x, 1 - y), (1 - x, 1 - y)]

        def blk(a, dev):
            return blocks[a](outs[a], 4 * dev[0] + 2 * dev[1] + dev[2])

        def copy(a, k, block, to, src=None):
            return pltpu.make_async_remote_copy(
                src_ref=blk(a, block) if src is None else src, dst_ref=blk(a, block), send_sem=send_sems.at[a, k],
                recv_sem=recv_sems.at[a, k], device_id=to, device_id_type=MESH)

        mine = [pltpu.make_async_copy(ins[a], blk(a, me), loc_sems.at[a]) for a in range(n)]
        for cp in mine:
            cp.start()
        first = []
        for a in range(n):
            first.append(copy(a, 0, me, sibling, src=ins[a]))
            first += [copy(a, 1 + j, me, (*chip, c), src=ins[a]) for j, chip in enumerate(chips)]
        for cp in first:
            cp.start()
        passed = []
        for j, chip in enumerate(chips):
            for a in range(n):
                copy(a, 1 + j, (*chip, c), me).wait_recv()
                fwd = copy(a, 4 + j, (*chip, c), sibling)
                fwd.start()
                passed.append(fwd)
        for a in range(n):
            copy(a, 0, sibling, me).wait_recv()
            for j, chip in enumerate(chips):
                copy(a, 4 + j, (*chip, 1 - c), me).wait_recv()
        for cp in first + passed:
            cp.wait_send()
        for cp in mine:
            cp.wait()

    return pl.pallas_call(
        body, in_specs=[ANY] * n, out_specs=[ANY] * n,
        out_shape=[jax.ShapeDtypeStruct(f, s.dtype) for f, s in zip(fulls, shards)],
        scratch_shapes=[pltpu.SemaphoreType.DMA((n, 7)), pltpu.SemaphoreType.DMA((n, 7)),
                        pltpu.SemaphoreType.DMA((n,))],
        name="gather_weights")(*shards)


def _scatter_pair(srcs, slabs, slab_shapes):
    n = len(srcs)

    def body(*refs):
        ins, outs = refs[:n], refs[n:2 * n]
        send_sems, recv_sems = refs[2 * n:]
        x, y, c = _me()
        cps = []
        for a in range(n):
            for q in range(4):
                j = 2 * q + (1 - c)
                cps.append(pltpu.make_async_remote_copy(
                    src_ref=slabs[a](ins[a], j), dst_ref=outs[a].at[q], send_sem=send_sems.at[a, q],
                    recv_sem=recv_sems.at[a, q], device_id=(x, y, 1 - c), device_id_type=MESH))
        for cp in cps:
            cp.start()
        for cp in cps:
            cp.wait_recv()
        for cp in cps:
            cp.wait_send()

    return pl.pallas_call(
        body, in_specs=[ANY] * n, out_specs=[ANY] * n,
        out_shape=[jax.ShapeDtypeStruct((4,) + tuple(shp), s.dtype) for shp, s in zip(slab_shapes, srcs)],
        scratch_shapes=[pltpu.SemaphoreType.DMA((n, 4)), pltpu.SemaphoreType.DMA((n, 4))],
        name="scatter_pair")(*srcs)


def _scatter_chips(parts):
    n = len(parts)

    def body(*refs):
        ins, outs = refs[:n], refs[n:2 * n]
        send_sems, recv_sems = refs[2 * n:]
        x, y, c = _me()
        chips = [(1 - x, y), (x, 1 - y), (1 - x, 1 - y)]
        cps = []
        for a in range(n):
            for k, (px, py) in enumerate(chips):
                cps.append(pltpu.make_async_remote_copy(
                    src_ref=ins[a].at[2 * px + py], dst_ref=outs[a].at[k], send_sem=send_sems.at[a, k],
                    recv_sem=recv_sems.at[a, k], device_id=(px, py, c), device_id_type=MESH))
        for cp in cps:
            cp.start()
        for cp in cps:
            cp.wait_recv()
        for cp in cps:
            cp.wait_send()

    return pl.pallas_call(
        body, in_specs=[ANY] * n, out_specs=[ANY] * n,
        out_shape=[jax.ShapeDtypeStruct((3,) + p.shape[1:], p.dtype) for p in parts],
        scratch_shapes=[pltpu.SemaphoreType.DMA((n, 3)), pltpu.SemaphoreType.DMA((n, 3))],
        name="scatter_chips")(*parts)


def _add_rows(arrs, out_dtype, name):
    shp = arrs[0].shape
    two = [a.reshape(-1, shp[-1]) for a in arrs]
    rows, cols = two[0].shape
    tr = _row_tile(rows, 1024)

    def body(*refs):
        acc = refs[0][...].astype(F32)
        for r in refs[1:-1]:
            acc = acc + r[...].astype(F32)
        refs[-1][...] = acc.astype(out_dtype)

    blk = pl.BlockSpec((tr, cols), lambda i: (i, 0))
    return pl.pallas_call(
        body, grid=(rows // tr,), in_specs=[blk] * len(two), out_specs=blk,
        out_shape=jax.ShapeDtypeStruct((rows, cols), out_dtype),
        compiler_params=_params(("arbitrary",)), name=name)(*two).reshape(shp)


def _exchange(src, name):
    blk = src.shape[-2:]

    def body(src_ref, out_ref, send_sems, recv_sems, loc_sem):
        x, y, c = lax.axis_index("x"), lax.axis_index("y"), lax.axis_index("c")
        me = 4 * x + 2 * y + c

        def block(j):
            return src_ref

        def remote(k, src_blk, dst_blk, peer_xyz):
            return pltpu.make_async_remote_copy(
                src_ref=block(src_blk), dst_ref=out_ref.at[dst_blk], send_sem=send_sems.at[k], recv_sem=recv_sems.at[k],
                device_id=peer_xyz, device_id_type=pl.DeviceIdType.MESH)

        local = pltpu.make_async_copy(block(me), out_ref.at[me], loc_sem)
        local.start()
        peers = []
        for k in range(1, N_DEV):
            px = 1 - x if k & 4 else x
            py = 1 - y if k & 2 else y
            pc = 1 - c if k & 1 else c
            peers.append((4 * px + 2 * py + pc, (px, py, pc)))
        sends = [remote(k, peer, me, xyz) for k, (peer, xyz) in enumerate(peers)]
        for cp in sends:
            cp.start()
        for k, (peer, xyz) in enumerate(peers):
            remote(k, peer, peer, xyz).wait_recv()
        for cp in sends:
            cp.wait_send()
        local.wait()

    return pl.pallas_call(
        body, in_specs=[pl.BlockSpec(memory_space=pl.ANY)], out_specs=pl.BlockSpec(memory_space=pl.ANY),
        out_shape=jax.ShapeDtypeStruct((N_DEV,) + blk, src.dtype),
        scratch_shapes=[pltpu.SemaphoreType.DMA((N_DEV - 1,)), pltpu.SemaphoreType.DMA((N_DEV - 1,)),
                        pltpu.SemaphoreType.DMA(())],
        name=name)(src)


def _row_tile(rows, cap):
    best = 8
    for t in range(8, min(rows, cap) + 1, 8):
        if rows % t == 0:
            best = t
    return best


def _sum_devices(x):
    _, rows, cols = x.shape
    tr = _row_tile(rows, 2048)

    def body(x_ref, o_ref):
        acc = x_ref[0]
        for j in range(1, N_DEV):
            acc = acc + x_ref[j]
        o_ref[...] = acc

    return pl.pallas_call(
        body, grid=(rows // tr,), in_specs=[pl.BlockSpec((N_DEV, tr, cols), lambda i: (0, i, 0))],
        out_specs=pl.BlockSpec((tr, cols), lambda i: (i, 0)), out_shape=jax.ShapeDtypeStruct((rows, cols), F32),
        compiler_params=_params(("arbitrary",)), name="sum_devices")(x)


def _adamw(w, g, m, v, name):
    rows, cols = w.shape
    tr = _row_tile(rows, 512)
    bc1 = 1.0 - ADAM_B1 ** ADAM_STEP
    bc2 = 1.0 - ADAM_B2 ** ADAM_STEP

    def body(w_ref, g_ref, m_ref, v_ref, d_ref, nm_ref, nv_ref):
        g_ = g_ref[...]
        m_ = ADAM_B1 * m_ref[...] + (1.0 - ADAM_B1) * g_
        v_ = ADAM_B2 * v_ref[...] + (1.0 - ADAM_B2) * (g_ * g_)
        d_ref[...] = -ADAM_LR * ((m_ / bc1) / (jnp.sqrt(v_ / bc2) + ADAM_EPS) + ADAM_WD * w_ref[...])
        nm_ref[...] = m_
        nv_ref[...] = v_

    blk = pl.BlockSpec((tr, cols), lambda i: (i, 0))
    return pl.pallas_call(
        body, grid=(rows // tr,), in_specs=[blk] * 4, out_specs=[blk] * 3,
        out_shape=[jax.ShapeDtypeStruct((rows, cols), F32)] * 3,
        compiler_params=_params(("arbitrary",)), name=name)(w, g, m, v)


def _pack(arrs, dtype=F32):
    flat = jnp.concatenate([a.reshape(-1).astype(dtype) for a in arrs])
    rows = -(-flat.shape[0] // (16 * LANES)) * 16
    flat = jnp.concatenate([flat, jnp.zeros((rows * LANES - flat.shape[0],), dtype)])
    return flat.reshape(rows, LANES)


def _unpack(flat, shapes):
    flat = flat.reshape(-1)
    out, off = [], 0
    for s in shapes:
        n = int(np.prod(s))
        out.append(flat[off:off + n].reshape(s))
        off += n
    return out


SMALL = ("c_ctx", "b_mod", "g_pre", "g_post", "ret_norm_g", "sg_w", "sg_b", "sc_conv_w", "gdn_conv_w", "gdn_a_log",
         "gdn_dt_bias", "gdn_norm_g")
ORDER = ("c_ctx", "w_mod", "b_mod", "g_pre", "g_post", "w_in", "w_out", "ret_norm_g", "sg_w", "sg_b", "sc_conv_w",
         "gdn_conv_w", "gdn_a_log", "gdn_dt_bias", "gdn_norm_g")


def kernel(x, c, ctx, c_ctx, w_mod, b_mod, g_pre, g_post, w_in, w_out, ret_norm_g, sg_w, sg_b, sc_conv_w, gdn_conv_w, gdn_a_log, gdn_dt_bias, gdn_norm_g, loss_target, m_c_ctx, m_w_mod, m_b_mod, m_g_pre, m_g_post, m_w_in, m_w_out, m_ret_norm_g, m_sg_w, m_sg_b, m_sc_conv_w, m_gdn_conv_w, m_gdn_a_log, m_gdn_dt_bias, m_gdn_norm_g, v_c_ctx, v_w_mod, v_b_mod, v_g_pre, v_g_post, v_w_in, v_w_out, v_ret_norm_g, v_sg_w, v_sg_b, v_sc_conv_w, v_gdn_conv_w, v_gdn_a_log, v_gdn_dt_bias, v_gdn_norm_g):
    wts = dict(c_ctx=c_ctx, w_mod=w_mod, b_mod=b_mod, g_pre=g_pre, g_post=g_post, w_in=w_in, w_out=w_out,
               ret_norm_g=ret_norm_g, sg_w=sg_w, sg_b=sg_b, sc_conv_w=sc_conv_w, gdn_conv_w=gdn_conv_w,
               gdn_a_log=gdn_a_log, gdn_dt_bias=gdn_dt_bias, gdn_norm_g=gdn_norm_g)
    mom = dict(c_ctx=m_c_ctx, w_mod=m_w_mod, b_mod=m_b_mod, g_pre=m_g_pre, g_post=m_g_post, w_in=m_w_in, w_out=m_w_out,
               ret_norm_g=m_ret_norm_g, sg_w=m_sg_w, sg_b=m_sg_b, sc_conv_w=m_sc_conv_w, gdn_conv_w=m_gdn_conv_w,
               gdn_a_log=m_gdn_a_log, gdn_dt_bias=m_gdn_dt_bias, gdn_norm_g=m_gdn_norm_g)
    var = dict(c_ctx=v_c_ctx, w_mod=v_w_mod, b_mod=v_b_mod, g_pre=v_g_pre, g_post=v_g_post, w_in=v_w_in, w_out=v_w_out,
               ret_norm_g=v_ret_norm_g, sg_w=v_sg_w, sg_b=v_sg_b, sc_conv_w=v_sc_conv_w, gdn_conv_w=v_gdn_conv_w,
               gdn_a_log=v_gdn_a_log, gdn_dt_bias=v_gdn_dt_bias, gdn_norm_g=v_gdn_norm_g)
    depth = w_mod.shape[0]
    n_mod, n_in, n_out = w_mod.shape[2], w_in.shape[2], w_out.shape[1]
    n_sc, n_gdn = sc_conv_w.shape[2], gdn_conv_w.shape[2]
    xi, yi, ci = _me()
    me = 4 * xi + 2 * yi + ci

    conv = _pack([sc_conv_w, gdn_conv_w])
    n_conv = depth * 3 * n_sc
    shards = [w_mod.astype(BF16), w_in.astype(BF16), w_out.astype(BF16), conv]
    fulls = [(depth, D, N_DEV * n_mod), (N_DEV, depth, D, n_in), (depth, N_DEV * n_out, D), (N_DEV,) + conv.shape]
    blocks = [lambda r, j: r.at[:, :, pl.ds(pl.multiple_of(j * n_mod, LANES), n_mod)],
              lambda r, j: r.at[j],
              lambda r, j: r.at[:, pl.ds(pl.multiple_of(j * n_out, 16), n_out), :],
              lambda r, j: r.at[j]]
    wmod_f, win_g, wout_f, conv_g = _gather_weights(shards, fulls, blocks)
    win_f = jnp.pad(win_g.transpose(1, 2, 0, 3).reshape(depth, D, N_DEV * n_in),
                    ((0, 0), (0, 0), (0, W_PAD - N_DEV * n_in)))
    conv_g = conv_g.reshape(N_DEV, -1)
    scw_f = conv_g[:, :n_conv].reshape(N_DEV, depth, 3, n_sc).transpose(1, 2, 0, 3).reshape(depth, 3, -1)
    gdnw_f = conv_g[:, n_conv:n_conv + depth * 3 * n_gdn].reshape(N_DEV, depth, 3, n_gdn).transpose(1, 2, 0, 3)
    gdnw_f = gdnw_f.reshape(depth, 3, -1)

    loss8, dx, g, cvec8 = _local_step(x, c, ctx, c_ctx, wmod_f, b_mod, g_pre, g_post, win_f, wout_f, ret_norm_g, sg_w,
                                      sg_b, scw_f, gdnw_f, gdn_a_log, gdn_dt_bias, gdn_norm_g, loss_target)

    gin = g["win"].astype(BF16).reshape(depth, D, N_DEV, n_in).transpose(2, 0, 1, 3)
    gout = g["wout"].astype(BF16)
    slabs = [lambda r, j: r.at[j], lambda r, j: r.at[:, pl.ds(pl.multiple_of(j * n_out, 16), n_out), :]]
    got_in, got_out = _scatter_pair([gin, gout], slabs, [(depth, D, n_in), (depth, n_out, D)])
    mine_in = lax.dynamic_index_in_dim(gin.reshape(4, 2, depth, D, n_in), ci, axis=1, keepdims=False)
    mine_out = lax.dynamic_index_in_dim(gout.reshape(depth, 4, 2, n_out, D), ci, axis=2, keepdims=False)
    mine_out = mine_out.transpose(1, 0, 2, 3)
    far_in, far_out = _scatter_chips([_add_rows([mine_in, got_in], BF16, "pair_sum_in"),
                                      _add_rows([mine_out, got_out], BF16, "pair_sum_out")])
    chip = 2 * xi + yi
    own = lambda a: lax.dynamic_index_in_dim(a, chip, axis=0, keepdims=False)
    grad = dict(w_in=_add_rows([own(mine_in), own(got_in), far_in[0], far_in[1], far_in[2]], F32, "grad_sum_in"),
                w_out=_add_rows([own(mine_out), own(got_out), far_out[0], far_out[1], far_out[2]], F32,
                                "grad_sum_out"))

    local_small = dict(c_ctx=g["c_ctx"], b_mod=g["bmod"], g_pre=g["gpre"], g_post=g["gpost"], ret_norm_g=g["ret_ng"],
                       sg_w=g["sgw"], sg_b=g["sgb"], sc_conv_w=g["scw"], gdn_conv_w=g["gdnw"], gdn_a_log=g["alog"],
                       gdn_dt_bias=g["dtb"], gdn_norm_g=g["gdn_ng"])
    to_sum = _pack([loss8[0, :1]] + [local_small[k] for k in SMALL])
    rows_sum = to_sum.shape[0]
    as_is = _pack([cvec8[:3], g["dmod"][:, :3, :]])
    everyone = _exchange(jnp.concatenate([to_sum, as_is], axis=0), "gather_small")
    small_sum = _unpack(_sum_devices(everyone[:, :rows_sum]), [(1,)] + [local_small[k].shape for k in SMALL])
    loss = small_sum[0][0]
    for k, val in zip(SMALL, small_sum[1:]):
        grad[k] = val
    grad["sc_conv_w"] = lax.dynamic_slice_in_dim(grad["sc_conv_w"], me * n_sc, n_sc, axis=2)
    grad["gdn_conv_w"] = lax.dynamic_slice_in_dim(grad["gdn_conv_w"], me * n_gdn, n_gdn, axis=2)
    rest =everyone[:, rows_sum:].reshape(N_DEV, -1)
    c_all = rest[:, :3 * D].reshape(N_DEV * 3, D)
    dmod_all = rest[:, 3 * D:3 * D + depth * 9 * D].reshape(N_DEV, depth, 3, 3 * D).transpose(1, 0, 2, 3)
    dmod_mine = lax.dynamic_slice_in_dim(dmod_all.reshape(depth, N_DEV * 3, 3 * D), me * n_mod, n_mod, axis=2)
    grad["w_mod"] = _wmod_grad(_pad_rows(c_all, 32), jnp.pad(dmod_mine, ((0, 0), (0, 32 - N_DEV * 3), (0, 0))))

    delta, new_m, new_v = {}, {}, {}
    for k in ("w_mod", "w_in", "w_out"):
        shp = wts[k].shape
        two = lambda a: a.reshape(-1, shp[-1])
        res = _adamw(two(wts[k]), two(grad[k]), two(mom[k]), two(var[k]), "adamw_" + k)
        delta[k], new_m[k], new_v[k] = [r.reshape(shp) for r in res]
    res = _adamw(*[_pack([d[k] for k in SMALL]) for d in (wts, grad, mom, var)], "adamw_small")
    for dst, flat in zip((delta, new_m, new_v), res):
        for k, val in zip(SMALL, _unpack(flat, [wts[k].shape for k in SMALL])):
            dst[k] = val
    return (loss, dx, *[grad[k] for k in ORDER], *[delta[k] for k in ORDER], *[new_m[k] for k in ORDER],
            *[new_v[k] for k in ORDER])
```

```python
import functools
import math

import jax
import jax.numpy as jnp
import numpy as np
from jax import lax
from jax.experimental import pallas as pl
from jax.experimental.pallas import tpu as pltpu

F32, BF16 = jnp.float32, jnp.bfloat16
HI = lax.Precision.HIGHEST

N_DEV = 8
D = 1024
DEPTH = 4
BRW = 256
HD = 64
NH = 4
LANES = 128
GRID_W = 64
ROPE_BASE = 10000.0
W_IN = 15 * BRW + 4 * NH
W_PAD = 31 * LANES
RC = 128
GC = 64
EPS = 1e-6
LOG_GAMMA = tuple(math.log(1.0 - 2.0 ** (-5.0 - h)) for h in range(NH))
ADAM_LR, ADAM_B1, ADAM_B2, ADAM_EPS, ADAM_WD, ADAM_STEP = 0.001, 0.9, 0.999, 1e-08, 0.01, 10
VMEM_LIMIT = 56 * 1024 * 1024

COL_RET, COL_SG, COL_SC, COL_GDN = 0, 4, 7, 11
COL_A128 = 30


def _params(sem):
    return pltpu.CompilerParams(dimension_semantics=sem, vmem_limit_bytes=VMEM_LIMIT)


def _bdot(a, b, ca, cb):
    if a.ndim == 3:
        dn = (((ca + 1,), (cb + 1,)), ((0,), (0,)))
    else:
        dn = (((ca,), (cb,)), ((), ()))
    return lax.dot_general(a.astype(BF16), b.astype(BF16), dn, preferred_element_type=F32)


@jax.custom_vjp
def mm(a, b):
    return _bdot(a, b, 1, 0)


mm.defvjp(lambda a, b: (_bdot(a, b, 1, 0), (a, b)),
          lambda r, g: (_bdot(g, r[1], 1, 1), _bdot(r[0], g, 0, 0)))


@jax.custom_vjp
def mm_nt(a, b):
    return _bdot(a, b, 1, 1)


mm_nt.defvjp(lambda a, b: (_bdot(a, b, 1, 1), (a, b)),
             lambda r, g: (_bdot(g, r[1], 1, 0), _bdot(g, r[0], 0, 0)))


@jax.custom_vjp
def mm_tn(a, b):
    return _bdot(a, b, 0, 0)


mm_tn.defvjp(lambda a, b: (_bdot(a, b, 0, 0), (a, b)),
             lambda r, g: (_bdot(r[1], g, 1, 1), _bdot(r[0], g, 1, 0)))


def _dotf(a, b):
    return jnp.dot(a, b, precision=HI, preferred_element_type=F32)


def _iota(shape, dim):
    return lax.broadcasted_iota(jnp.int32, shape, dim)


def _head_mask(h, width=BRW):
    return (_iota((1, width), 1) // HD == h).astype(F32)


def _lane_by_head(vals, width=BRW, lane0=0):
    head = (_iota((1, width), 1) + lane0) // HD
    out = jnp.full((1, width), vals[NH - 1], F32)
    for h in range(NH - 2, -1, -1):
        out = jnp.where(head == h, vals[h], out)
    return out


def _block_diag(n, width):
    return (_iota((n, width), 0) // HD == _iota((n, width), 1) // HD).astype(F32)


@jax.custom_vjp
def _head_sum(x):
    w = x.shape[1]
    ones = _block_diag(w, w).astype(BF16)
    hi = x.astype(BF16)
    lo = (x - hi.astype(F32)).astype(BF16)
    return jnp.dot(hi, ones, preferred_element_type=F32) + jnp.dot(lo, ones, preferred_element_type=F32)


_head_sum.defvjp(lambda x: (_head_sum(x), None), lambda _, g: (_head_sum(g),))


def _silu(x):
    return x * jax.nn.sigmoid(x)


def _stack_heads(x):
    return jnp.concatenate([x * _head_mask(h) for h in range(NH)], axis=-2)


@jax.custom_vjp
def _unstack_heads(x):
    n = x.shape[-2] // NH
    return (x[..., 0:n, :] + x[..., n:2 * n, :]) + (x[..., 2 * n:3 * n, :] + x[..., 3 * n:4 * n, :])


_unstack_heads.defvjp(lambda x: (_unstack_heads(x), None), lambda _, g: (_stack_heads(g),))


@jax.custom_vjp
def _rot_half(x):
    n = x.shape[1]
    first = (_iota(x.shape, 1) % 32) < 16
    return jnp.where(first, -pltpu.roll(x, n - 16, 1), pltpu.roll(x, 16, 1))


_rot_half.defvjp(lambda x: (_rot_half(x), None), lambda _, g: (-_rot_half(g),))


def _rotary(x, cos, sin):
    return x * cos + _rot_half(x) * sin


def _make_shifts(seq, t_ctx):
    def dn_raw(x):
        r = _iota(x.shape, 0)
        return jnp.where((r == 0) | (r == t_ctx), 0.0, pltpu.roll(x, 1, 0))

    def up_raw(x):
        r = _iota(x.shape, 0)
        return jnp.where((r == t_ctx - 1) | (r == seq - 1), 0.0, pltpu.roll(x, seq - 1, 0))

    @jax.custom_vjp
    def dn(x):
        return dn_raw(x)

    @jax.custom_vjp
    def up(x):
        return up_raw(x)

    dn.defvjp(lambda x: (dn_raw(x), None), lambda _, g: (up_raw(g),))
    up.defvjp(lambda x: (up_raw(x), None), lambda _, g: (dn_raw(g),))
    return dn, up


def _conv3(t, w0, w1, w2, dn, up):
    return dn(t) * w0 + t * w1 + up(t) * w2


def _acc(ref, val, first, at=()):
    idx = at + (Ellipsis,)

    @pl.when(first)
    def _():
        ref[idx] = val

    @pl.when(jnp.logical_not(first))
    def _():
        ref[idx] += val


def _mod_fwd(cvec8, wmod, bmod):
    depth = wmod.shape[0]

    def body(c_ref, w_ref, b_ref, o_ref):
        sc = _silu(c_ref[...])
        o_ref[0] = jnp.dot(sc.astype(BF16), w_ref[0], preferred_element_type=F32) + b_ref[0]

    return pl.pallas_call(
        body, grid=(depth, 3),
        in_specs=[pl.BlockSpec((8, D), lambda l, j: (0, 0)),
                  pl.BlockSpec((1, D, D), lambda l, j: (l, 0, j)),
                  pl.BlockSpec((1, 1, D), lambda l, j: (l, 0, j))],
        out_specs=pl.BlockSpec((1, 8, D), lambda l, j: (l, 0, j)),
        out_shape=jax.ShapeDtypeStruct((depth, 8, 3 * D), F32),
        compiler_params=_params(("arbitrary", "arbitrary")), name="mod_fwd")(cvec8, wmod, bmod)


def _mod_bwd(dmod, wmod, cvec8):
    depth = wmod.shape[0]

    def body(dm_ref, w_ref, c_ref, dc_ref, db_ref):
        l, j = pl.program_id(0), pl.program_id(1)
        dm = dm_ref[0]
        db_ref[0] = jnp.sum(dm, axis=0, keepdims=True)
        part = _bdot(dm, w_ref[0], 1, 1)
        _acc(dc_ref, part, (l == 0) & (j == 0))

        @pl.when((l == depth - 1) & (j == 2))
        def _():
            c = c_ref[...]
            s = jax.nn.sigmoid(c)
            dc_ref[...] = dc_ref[...] * (s * (1.0 + c * (1.0 - s)))

    return pl.pallas_call(
        body, grid=(depth, 3),
        in_specs=[pl.BlockSpec((1, 8, D), lambda l, j: (l, 0, j)),
                  pl.BlockSpec((1, D, D), lambda l, j: (l, 0, j)),
                  pl.BlockSpec((8, D), lambda l, j: (0, 0))],
        out_specs=[pl.BlockSpec((8, D), lambda l, j: (0, 0)),
                   pl.BlockSpec((1, 1, D), lambda l, j: (l, 0, j))],
        out_shape=[jax.ShapeDtypeStruct((8, D), F32), jax.ShapeDtypeStruct((depth, 1, 3 * D), F32)],
        compiler_params=_params(("arbitrary", "arbitrary")), name="mod_bwd")(dmod, wmod, cvec8)


def _wmod_grad(c_rows, dmod_cols):
    depth, rows, n = dmod_cols.shape

    def body(c_ref, dm_ref, o_ref):
        sc = _silu(c_ref[...])
        o_ref[0] = lax.dot_general(sc, dm_ref[0], (((0,), (0,)), ((), ())), precision=HI,
                                   preferred_element_type=F32)

    return pl.pallas_call(
        body, grid=(depth,),
        in_specs=[pl.BlockSpec((rows, D), lambda l: (0, 0)), pl.BlockSpec((1, rows, n), lambda l: (l, 0, 0))],
        out_specs=pl.BlockSpec((1, D, n), lambda l: (l, 0, 0)),
        out_shape=jax.ShapeDtypeStruct((depth, D, n), F32),
        compiler_params=_params(("arbitrary",)), name="wmod_grad")(c_rows, dmod_cols)


class _Lay:
    def __init__(self, batch, t_ctx, t_lat):
        self.b, self.t_ctx, self.t_lat = batch, t_ctx, t_lat
        self.s = t_ctx + t_lat
        self.tm = min(256, t_ctx)
        self.tpb = self.s // self.tm
        self.nct = t_ctx // self.tm
        self.ntiles = batch * self.tpb
        self.rows = batch * self.s

    def mod_row(self, i):
        return jnp.where(i % self.tpb < self.nct, 2, i // self.tpb)

    def group(self, i):
        return 2 * (i // self.tpb) + jnp.where(i % self.tpb < self.nct, 0, 1)

    def group_first(self, i):
        return (i % self.tpb == 0) | (i % self.tpb == self.nct)


def _norm_mod(x, g, shift, scale):
    r = lax.rsqrt(jnp.mean(x * x, axis=-1, keepdims=True) + EPS)
    return (x * r * g) * (1.0 + scale) + shift


def _inproj_fwd(lay, xc, mod3, gpre, w):
    tm = lay.tm

    def body(x_ref, sh_ref, sc_ref, g_ref, w_ref, p_ref, ht_ref):
        h = _norm_mod(x_ref[...], g_ref[...], sh_ref[0, 0], sc_ref[0, 0])
        ht_ref[...] = h.T.astype(BF16)
        p_ref[...] = jnp.dot(h.astype(BF16), w_ref[...], preferred_element_type=F32)

    return pl.pallas_call(
        body, grid=(lay.ntiles,),
        in_specs=[pl.BlockSpec((tm, D), lambda i: (i, 0)),
                  pl.BlockSpec((1, 1, 1, D), lambda i: (0, lay.mod_row(i), 0, 0)),
                  pl.BlockSpec((1, 1, 1, D), lambda i: (1, lay.mod_row(i), 0, 0)),
                  pl.BlockSpec((1, D), lambda i: (0, 0)),
                  pl.BlockSpec((D, W_PAD), lambda i: (0, 0))],
        out_specs=[pl.BlockSpec((tm, W_PAD), lambda i: (i, 0)), pl.BlockSpec((D, tm), lambda i: (0, i))],
        out_shape=[jax.ShapeDtypeStruct((lay.rows, W_PAD), F32), jax.ShapeDtypeStruct((D, lay.rows), BF16)],
        compiler_params=_params(("arbitrary",)), name="inproj_fwd")(xc, mod3, mod3, gpre, w)


def _inproj_bwd(lay, xc, mod3, gpre, wt, dxc, pieces):
    tm = lay.tm
    npc = len(pieces)
    offs = [off for _, off in pieces]

    def body(*refs):
        x_ref, sh_ref, sc_ref, g_ref, wt_ref, dx_in = refs[:6]
        dps = refs[6:6 + npc]
        dx_ref, dg_ref, dsh_ref, dsc_ref = refs[6 + npc:]
        i = pl.program_id(0)
        dh = None
        for dp_ref, off in zip(dps, offs):
            wd = dp_ref.shape[1]
            part = jnp.dot(dp_ref[...], wt_ref[off:off + wd, :], preferred_element_type=F32)
            dh = part if dh is None else dh + part
        _, vjp = jax.vjp(_norm_mod, x_ref[...], g_ref[...], sh_ref[0, 0], sc_ref[0, 0])
        dx, dg, dsh, dsc = vjp(dh)
        dx_ref[...] = dx_in[...] + dx
        _acc(dg_ref, dg, i == 0)
        first = lay.group_first(i)
        _acc(dsh_ref, dsh, first, at=(0,))
        _acc(dsc_ref, dsc, first, at=(0,))

    return pl.pallas_call(
        body, grid=(lay.ntiles,),
        in_specs=[pl.BlockSpec((tm, D), lambda i: (i, 0)),
                  pl.BlockSpec((1, 1, 1, D), lambda i: (0, lay.mod_row(i), 0, 0)),
                  pl.BlockSpec((1, 1, 1, D), lambda i: (1, lay.mod_row(i), 0, 0)),
                  pl.BlockSpec((1, D), lambda i: (0, 0)),
                  pl.BlockSpec((W_PAD, D), lambda i: (0, 0)),
                  pl.BlockSpec((tm, D), lambda i: (i, 0))]
        + [pl.BlockSpec((tm, dp.shape[1]), lambda i: (i, 0)) for dp, _ in pieces],
        out_specs=[pl.BlockSpec((tm, D), lambda i: (i, 0)),
                   pl.BlockSpec((1, D), lambda i: (0, 0)),
                   pl.BlockSpec((1, 1, D), lambda i: (lay.group(i), 0, 0)),
                   pl.BlockSpec((1, 1, D), lambda i: (lay.group(i), 0, 0))],
        out_shape=[jax.ShapeDtypeStruct((lay.rows, D), F32), jax.ShapeDtypeStruct((1, D), F32),
                   jax.ShapeDtypeStruct((2 * lay.b, 1, D), F32), jax.ShapeDtypeStruct((2 * lay.b, 1, D), F32)],
        compiler_params=_params(("arbitrary",)), name="inproj_bwd",
    )(xc, mod3, mod3, gpre, wt, dxc, *[dp for dp, _ in pieces])


def _weight_grad(lay, ht, dp, name):
    wd = dp.shape[1]
    tn = 512 if wd % 512 == 0 else (256 if wd % 256 == 0 else LANES)
    tr = lay.rows // 3 if lay.rows % (3 * 256) == 0 else lay.tm

    def body(ht_ref, dp_ref, o_ref):
        _acc(o_ref, jnp.dot(ht_ref[...], dp_ref[...], preferred_element_type=F32), pl.program_id(1) == 0)

    return pl.pallas_call(
        body, grid=(wd // tn, lay.rows // tr),
        in_specs=[pl.BlockSpec((D, tr), lambda j, i: (0, i)), pl.BlockSpec((tr, tn), lambda j, i: (i, j))],
        out_specs=pl.BlockSpec((D, tn), lambda j, i: (0, j)),
        out_shape=jax.ShapeDtypeStruct((D, wd), F32),
        compiler_params=_params(("arbitrary", "arbitrary")), name=name)(ht, dp)


def _outproj_post(o, x, gpost, gate):
    r = lax.rsqrt(jnp.mean(o * o, axis=-1, keepdims=True) + EPS)
    return x + gate * (o * r * gpost)


def _outproj_matmul(ys, w_ref):
    o = None
    for k, y in enumerate(ys):
        part = jnp.dot(y[...], w_ref[BRW * k:BRW * (k + 1), :], preferred_element_type=F32)
        o = part if o is None else o + part
    return o


def _outproj_specs(lay):
    tm = lay.tm
    return ([pl.BlockSpec((tm, BRW), lambda i: (i, 0))] * 4
            + [pl.BlockSpec((tm, D), lambda i: (i, 0))]
            + [pl.BlockSpec((D, D), lambda i: (0, 0))]
            + [pl.BlockSpec((1, D), lambda i: (0, 0))]
            + [pl.BlockSpec((1, 1, 1, D), lambda i: (2, lay.mod_row(i), 0, 0))])


def _outproj_fwd(lay, ys, xc, wout, gpost, mod3):
    tm = lay.tm

    def body(y0, y1, y2, y3, x_ref, w_ref, g_ref, gt_ref, o_ref, yt_ref):
        ys_ = (y0, y1, y2, y3)
        o_ref[...] = _outproj_post(_outproj_matmul(ys_, w_ref), x_ref[...], g_ref[...], gt_ref[0, 0])
        for k, y in enumerate(ys_):
            yt_ref[BRW * k:BRW * (k + 1), :] = y[...].astype(F32).T.astype(BF16)

    return pl.pallas_call(
        body, grid=(lay.ntiles,), in_specs=_outproj_specs(lay),
        out_specs=[pl.BlockSpec((tm, D), lambda i: (i, 0)), pl.BlockSpec((D, tm), lambda i: (0, i))],
        out_shape=[jax.ShapeDtypeStruct((lay.rows, D), F32), jax.ShapeDtypeStruct((D, lay.rows), BF16)],
        compiler_params=_params(("arbitrary",)), name="outproj_fwd")(*ys, xc, wout, gpost, mod3)


def _outproj_bwd(lay, ys, xc, wout, gpost, mod3, dxc):
    tm = lay.tm

    def body(y0, y1, y2, y3, x_ref, w_ref, g_ref, gt_ref, dx_ref, d0, d1, d2, d3, do_ref, dg_ref, dgt_ref):
        i = pl.program_id(0)
        o = _outproj_matmul((y0, y1, y2, y3), w_ref)
        _, vjp = jax.vjp(_outproj_post, o, x_ref[...], g_ref[...], gt_ref[0, 0])
        do, _, dg, dgt = vjp(dx_ref[...])
        do = do.astype(BF16)
        do_ref[...] = do
        for k, d in enumerate((d0, d1, d2, d3)):
            d[...] = _bdot(do, w_ref[BRW * k:BRW * (k + 1), :], 1, 1)
        _acc(dg_ref, dg, i == 0)
        _acc(dgt_ref, dgt, lay.group_first(i), at=(0,))

    return pl.pallas_call(
        body, grid=(lay.ntiles,),
        in_specs=_outproj_specs(lay) + [pl.BlockSpec((tm, D), lambda i: (i, 0))],
        out_specs=[pl.BlockSpec((tm, BRW), lambda i: (i, 0))] * 4
        + [pl.BlockSpec((tm, D), lambda i: (i, 0)), pl.BlockSpec((1, D), lambda i: (0, 0)),
           pl.BlockSpec((1, 1, D), lambda i: (lay.group(i), 0, 0))],
        out_shape=[jax.ShapeDtypeStruct((lay.rows, BRW), F32)] * 4
        + [jax.ShapeDtypeStruct((lay.rows, D), BF16), jax.ShapeDtypeStruct((1, D), F32),
           jax.ShapeDtypeStruct((2 * lay.b, 1, D), F32)],
        compiler_params=_params(("arbitrary",)), name="outproj_bwd")(*ys, xc, wout, gpost, mod3, dxc)


def _loss_kernel(lay, xc3, target):
    tm, nct = lay.tm, lay.nct

    def body(x_ref, t_ref, loss_ref, dx_ref):
        b, i = pl.program_id(0), pl.program_id(1)
        lat = i >= nct
        err = x_ref[0] - t_ref[0]
        dx_ref[0] = jnp.where(lat, err * (1.0 / D), 0.0)
        part = jnp.sum(jnp.sum(err * err, axis=1, keepdims=True), axis=0, keepdims=True) * (0.5 / D)
        part = jnp.broadcast_to(jnp.where(lat, part, 0.0), (8, LANES))
        _acc(loss_ref, part, (b == 0) & (i == 0))

    return pl.pallas_call(
        body, grid=(lay.b, lay.tpb),
        in_specs=[pl.BlockSpec((1, tm, D), lambda b, i: (b, i, 0)),
                  pl.BlockSpec((1, tm, D), lambda b, i: (b, jnp.maximum(i - nct, 0), 0))],
        out_specs=[pl.BlockSpec((8, LANES), lambda b, i: (0, 0)), pl.BlockSpec((1, tm, D), lambda b, i: (b, i, 0))],
        out_shape=[jax.ShapeDtypeStruct((8, LANES), F32), jax.ShapeDtypeStruct(xc3.shape, F32)],
        compiler_params=_params(("arbitrary", "arbitrary")), name="loss")(xc3, target)


def _chunk_orders(n_ctx, n_all):
    fwd = list(range(n_all))
    rev = list(range(n_ctx - 1, -1, -1)) + list(range(n_all - 1, n_ctx - 1, -1))
    return fwd, rev


def _ret_state_fn(k, v, cos, sin):
    kt = _rotary(k, cos, sin) * (HD ** -0.5)
    lg = _lane_by_head(LOG_GAMMA)
    j = _iota((RC, 1), 0).astype(F32)
    bd = _block_diag(BRW, BRW)
    af = mm_tn(kt * jnp.exp((RC - 1.0 - j) * lg), v) * bd
    ar = mm_tn(kt * jnp.exp(j * lg), v) * bd
    return af, ar


def _ret_out_fn(q, k, v, z, cos, sin, sf, sr, ng):
    qt = _rotary(q, cos, sin)
    kt = _rotary(k, cos, sin) * (HD ** -0.5)
    diff = (_iota((RC, RC), 0) - _iota((RC, RC), 1)).astype(F32)
    o = None
    for h in range(NH):
        m = _head_mask(h)
        sc = mm_nt(qt * m, kt)
        wgt = sc * jnp.exp(jnp.abs(diff) * LOG_GAMMA[h]) * jnp.where(diff == 0, 2.0, 1.0)
        part = mm(wgt, v * m)
        o = part if o is None else o + part
    lg = _lane_by_head(LOG_GAMMA)
    i = _iota((RC, 1), 0).astype(F32)
    o = o + mm(qt, sf) * jnp.exp((i + 1.0) * lg) + mm(qt, sr) * jnp.exp((RC - i) * lg)
    mu = _head_sum(o) * (1.0 / HD)
    cen = o - mu
    var = _head_sum(cen * cen) * (1.0 / HD)
    return cen * lax.rsqrt(var + EPS) * ng * _silu(z)


def _ret_specs(lay, cols):
    return [pl.BlockSpec((1, RC, BRW), functools.partial(lambda b, i, c: (b, i, c), c=COL_RET + c)) for c in cols]


def _ret_state(lay, p3, cos, sin):
    nc = lay.s // RC

    def body(k_ref, v_ref, c_ref, s_ref, a_ref):
        af, ar = _ret_state_fn(k_ref[0], v_ref[0], c_ref[...], s_ref[...])
        a_ref[0, 0, 0] = af
        a_ref[0, 0, 1] = ar

    tab = pl.BlockSpec((RC, BRW), lambda b, i: (i, 0))
    return pl.pallas_call(
        body, grid=(lay.b, nc), in_specs=_ret_specs(lay, (1, 2)) + [tab, tab],
        out_specs=pl.BlockSpec((1, 1, 2, BRW, BRW), lambda b, i: (b, i, 0, 0, 0)),
        out_shape=jax.ShapeDtypeStruct((lay.b, nc, 2, BRW, BRW), F32),
        compiler_params=_params(("arbitrary", "arbitrary")), name="ret_state")(p3, p3, cos, sin)


def _ret_state_bwd(lay, p3, cos, sin, d_a, dpr):
    nc = lay.s // RC

    def body(k_ref, v_ref, c_ref, s_ref, da_ref, dpr_ref, o_ref):
        _, vjp = jax.vjp(lambda k, v: _ret_state_fn(k, v, c_ref[...], s_ref[...]), k_ref[0], v_ref[0])
        dk, dv = vjp((da_ref[0, 0, 0], da_ref[0, 0, 1]))
        o_ref[0, :, 0:BRW] = dpr_ref[0, :, 0:BRW].astype(BF16)
        o_ref[0, :, BRW:2 * BRW] = (dpr_ref[0, :, BRW:2 * BRW] + dk).astype(BF16)
        o_ref[0, :, 2 * BRW:3 * BRW] = (dpr_ref[0, :, 2 * BRW:3 * BRW] + dv).astype(BF16)
        o_ref[0, :, 3 * BRW:] = dpr_ref[0, :, 3 * BRW:].astype(BF16)

    tab = pl.BlockSpec((RC, BRW), lambda b, i: (i, 0))
    return pl.pallas_call(
        body, grid=(lay.b, nc),
        in_specs=_ret_specs(lay, (1, 2)) + [tab, tab,
                                            pl.BlockSpec((1, 1, 2, BRW, BRW), lambda b, i: (b, i, 0, 0, 0)),
                                            pl.BlockSpec((1, RC, 4 * BRW), lambda b, i: (b, i, 0))],
        out_specs=pl.BlockSpec((1, RC, 4 * BRW), lambda b, i: (b, i, 0)),
        out_shape=jax.ShapeDtypeStruct((lay.b, lay.s, 4 * BRW), BF16),
        compiler_params=_params(("arbitrary", "arbitrary")), name="ret_state_bwd")(p3, p3, cos, sin, d_a, dpr)


def _state_scan(lay, a, nc_ctx, transpose, name):
    b, nc = a.shape[0], a.shape[1]
    orders = _chunk_orders(nc_ctx, nc)

    def body(a_ref, o_ref):
        d, jh = pl.program_id(1), pl.program_id(2)
        head = (_iota((1, LANES), 1) + jh * LANES) // HD
        lg = jnp.full((1, LANES), LOG_GAMMA[NH - 1], F32)
        for h in range(NH - 2, -1, -1):
            lg = jnp.where(head == h, LOG_GAMMA[h], lg)
        dec = jnp.exp(RC * lg)
        for dd in (0, 1):
            @pl.when(d == dd)
            def _(order=orders[dd]):
                acc = jnp.zeros((BRW, LANES), F32)
                if not transpose:
                    for c in order:
                        o_ref[0, c, 0] = acc
                        acc = acc * dec + a_ref[0, c, 0]
                else:
                    for c in reversed(order):
                        o_ref[0, c, 0] = acc
                        acc = a_ref[0, c, 0] + acc * dec

    spec = pl.BlockSpec((1, nc, 1, BRW, LANES), lambda bb, d, jh: (bb, 0, d, 0, jh))
    return pl.pallas_call(
        body, grid=(b, 2, BRW // LANES), in_specs=[spec], out_specs=spec,
        out_shape=jax.ShapeDtypeStruct(a.shape, F32),
        compiler_params=_params(("arbitrary",) * 3), name=name)(a)


def _ret_out(lay, p3, cos, sin, states, ng):
    nc = lay.s // RC

    def body(q_ref, k_ref, v_ref, z_ref, c_ref, s_ref, st_ref, ng_ref, y_ref):
        y = _ret_out_fn(q_ref[0], k_ref[0], v_ref[0], z_ref[0], c_ref[...], s_ref[...],
                        st_ref[0, 0, 0], st_ref[0, 0, 1], ng_ref[...])
        y_ref[0] = y.astype(BF16)

    tab = pl.BlockSpec((RC, BRW), lambda b, i: (i, 0))
    return pl.pallas_call(
        body, grid=(lay.b, nc),
        in_specs=_ret_specs(lay, (0, 1, 2, 3)) + [tab, tab,
                                                  pl.BlockSpec((1, 1, 2, BRW, BRW), lambda b, i: (b, i, 0, 0, 0)),
                                                  pl.BlockSpec((1, BRW), lambda b, i: (0, 0))],
        out_specs=pl.BlockSpec((1, RC, BRW), lambda b, i: (b, i, 0)),
        out_shape=jax.ShapeDtypeStruct((lay.b, lay.s, BRW), BF16),
        compiler_params=_params(("arbitrary", "arbitrary")), name="ret_out")(p3, p3, p3, p3, cos, sin, states, ng)


def _ret_out_bwd(lay, p3, cos, sin, states, ng, dy):
    nc = lay.s // RC

    def body(q_ref, k_ref, v_ref, z_ref, c_ref, s_ref, st_ref, ng_ref, dy_ref, dp_ref, dst_ref, dng_ref):
        b, i = pl.program_id(0), pl.program_id(1)
        fn = lambda q, k, v, z, sf, sr, ng: _ret_out_fn(q, k, v, z, c_ref[...], s_ref[...], sf, sr, ng)
        _, vjp = jax.vjp(fn, q_ref[0], k_ref[0], v_ref[0], z_ref[0], st_ref[0, 0, 0], st_ref[0, 0, 1], ng_ref[...])
        dq, dk, dv, dz, dsf, dsr, dng = vjp(dy_ref[0])
        for n, g in enumerate((dq, dk, dv, dz)):
            dp_ref[0, :, BRW * n:BRW * (n + 1)] = g
        dst_ref[0, 0, 0] = dsf
        dst_ref[0, 0, 1] = dsr
        _acc(dng_ref, dng, (b == 0) & (i == 0))

    tab = pl.BlockSpec((RC, BRW), lambda b, i: (i, 0))
    st = pl.BlockSpec((1, 1, 2, BRW, BRW), lambda b, i: (b, i, 0, 0, 0))
    return pl.pallas_call(
        body, grid=(lay.b, nc),
        in_specs=_ret_specs(lay, (0, 1, 2, 3)) + [tab, tab, st, pl.BlockSpec((1, BRW), lambda b, i: (0, 0)),
                                                  pl.BlockSpec((1, RC, BRW), lambda b, i: (b, i, 0))],
        out_specs=[pl.BlockSpec((1, RC, 4 * BRW), lambda b, i: (b, i, 0)), st,
                   pl.BlockSpec((1, BRW), lambda b, i: (0, 0))],
        out_shape=[jax.ShapeDtypeStruct((lay.b, lay.s, 4 * BRW), F32),
                   jax.ShapeDtypeStruct(states.shape, F32), jax.ShapeDtypeStruct((1, BRW), F32)],
        compiler_params=_params(("arbitrary", "arbitrary")), name="ret_out_bwd",
    )(p3, p3, p3, p3, cos, sin, states, ng, dy)


def _sg_fn(u, v, z, w, b8):
    ug = jax.nn.gelu(u)
    vg = jax.nn.gelu(v)
    mu = jnp.mean(vg, axis=-1, keepdims=True)
    cen = vg - mu
    vn = cen * lax.rsqrt(jnp.mean(cen * cen, axis=-1, keepdims=True) + EPS)
    masks = (_iota((NH, 1, BRW), 2) // HD == _iota((NH, 1, BRW), 0)).astype(F32)
    s = jnp.sum(mm(w, vn[None] * masks), axis=0)
    expand = (_iota((8, BRW), 1) // HD == _iota((8, BRW), 0)).astype(F32)
    bias = lax.dot_general(b8, expand, (((0,), (0,)), ((), ())), precision=HI, preferred_element_type=F32)
    return ug * (s + bias) * _silu(z)


def _sg_specs():
    return ([pl.BlockSpec((1, RC, BRW), functools.partial(lambda b, i, c: (b, i, c), c=COL_SG + c)) for c in range(3)]
            + [pl.BlockSpec((NH, RC, RC), lambda b, i: (0, 0, 0)), pl.BlockSpec((8, RC), lambda b, i: (0, 0))])


def _sg_fwd(lay, p3, sgw, sgb8):
    nc = lay.s // RC

    def body(u_ref, v_ref, z_ref, w_ref, b_ref, y_ref):
        y = _sg_fn(u_ref[0], v_ref[0], z_ref[0], w_ref[...], b_ref[...])
        y_ref[0] = y.astype(BF16)

    return pl.pallas_call(
        body, grid=(lay.b, nc), in_specs=_sg_specs(),
        out_specs=pl.BlockSpec((1, RC, BRW), lambda b, i: (b, i, 0)),
        out_shape=jax.ShapeDtypeStruct((lay.b, lay.s, BRW), BF16),
        compiler_params=_params(("arbitrary", "arbitrary")), name="sg_fwd")(p3, p3, p3, sgw, sgb8)


def _sg_bwd(lay, p3, sgw, sgb8, dy):
    nc = lay.s // RC

    def body(u_ref, v_ref, z_ref, w_ref, b_ref, dy_ref, dp_ref, dw_ref, db_ref):
        first = (pl.program_id(0) == 0) & (pl.program_id(1) == 0)
        _, vjp = jax.vjp(_sg_fn, u_ref[0], v_ref[0], z_ref[0], w_ref[...], b_ref[...])
        g = vjp(dy_ref[0])
        for n in range(3):
            dp_ref[0, :, BRW * n:BRW * (n + 1)] = g[n].astype(BF16)
        _acc(dw_ref, g[3], first)
        _acc(db_ref, g[4], first)

    return pl.pallas_call(
        body, grid=(lay.b, nc),
        in_specs=_sg_specs() + [pl.BlockSpec((1, RC, BRW), lambda b, i: (b, i, 0))],
        out_specs=[pl.BlockSpec((1, RC, 3 * BRW), lambda b, i: (b, i, 0)),
                   pl.BlockSpec((NH, RC, RC), lambda b, i: (0, 0, 0)), pl.BlockSpec((8, RC), lambda b, i: (0, 0))],
        out_shape=[jax.ShapeDtypeStruct((lay.b, lay.s, 3 * BRW), BF16),
                   jax.ShapeDtypeStruct((NH, RC, RC), F32), jax.ShapeDtypeStruct((8, RC), F32)],
        compiler_params=_params(("arbitrary", "arbitrary")), name="sg_bwd")(p3, p3, p3, sgw, sgb8, dy)


def _sc_specs(lay):
    first = COL_SC * BRW // LANES
    blk = [pl.BlockSpec((1, lay.s, LANES), functools.partial(lambda j, b, c: (b, 0, c + j), c=first + 2 * n))
           for n in range(4)]
    return blk + [pl.BlockSpec((8, LANES), lambda j, b: (0, j))]


def _sc_fwd(lay, p3, w8):
    dn, up = _make_shifts(lay.s, lay.t_ctx)

    def fn(b_, c_, h_, z_, w0, w1, w2):
        return b_ * _conv3(c_ * h_, w0, w1, w2, dn, up) * _silu(z_)

    def body(b_ref, c_ref, h_ref, z_ref, w_ref, y_ref):
        y = fn(b_ref[0], c_ref[0], h_ref[0], z_ref[0], w_ref[0:1, :], w_ref[1:2, :], w_ref[2:3, :])
        y_ref[0] = y.astype(BF16)

    return pl.pallas_call(
        body, grid=(BRW // LANES, lay.b), in_specs=_sc_specs(lay),
        out_specs=pl.BlockSpec((1, lay.s, LANES), lambda j, b: (b, 0, j)),
        out_shape=jax.ShapeDtypeStruct((lay.b, lay.s, BRW), BF16),
        compiler_params=_params(("arbitrary", "arbitrary")), name="sc_fwd")(p3, p3, p3, p3, w8)


def _sc_bwd(lay, p3, w8, dy):
    dn, up = _make_shifts(lay.s, lay.t_ctx)

    def fn(b_, c_, h_, z_, w0, w1, w2):
        return b_ * _conv3(c_ * h_, w0, w1, w2, dn, up) * _silu(z_)

    def body(b_ref, c_ref, h_ref, z_ref, w_ref, dy_ref, db_ref, dc_ref, dh_ref, dz_ref, dw_ref):
        _, vjp = jax.vjp(fn, b_ref[0], c_ref[0], h_ref[0], z_ref[0], w_ref[0:1, :], w_ref[1:2, :], w_ref[2:3, :])
        g = vjp(dy_ref[0])
        for ref, val in zip((db_ref, dc_ref, dh_ref, dz_ref), g[:4]):
            ref[0] = val.astype(BF16)
        dw = jnp.concatenate([g[4], g[5], g[6], jnp.zeros((5, LANES), F32)], axis=0)
        _acc(dw_ref, dw, pl.program_id(1) == 0)

    out = pl.BlockSpec((1, lay.s, LANES), lambda j, b: (b, 0, j))
    return pl.pallas_call(
        body, grid=(BRW // LANES, lay.b), in_specs=_sc_specs(lay) + [out],
        out_specs=[out] * 4 + [pl.BlockSpec((8, LANES), lambda j, b: (0, j))],
        out_shape=[jax.ShapeDtypeStruct((lay.b, lay.s, BRW), BF16)] * 4 + [jax.ShapeDtypeStruct((8, BRW), F32)],
        compiler_params=_params(("arbitrary", "arbitrary")), name="sc_bwd")(p3, p3, p3, p3, w8, dy)


def _gdn_conv_fn(x, w0, w1, w2, normed, dn, up):
    a = _silu(_conv3(x, w0, w1, w2, dn, up))
    nrm = a * lax.rsqrt(_head_sum(a * a) + EPS)
    return jnp.where(normed, nrm, a)


def _gdn_conv(lay, p3, w8):
    dn, up = _make_shifts(lay.s, lay.t_ctx)
    first = COL_GDN * BRW // LANES

    def body(x_ref, w_ref, o_ref):
        normed = pl.program_id(0) < 2 * BRW // LANES
        o_ref[0] = _gdn_conv_fn(x_ref[0], w_ref[0:1, :], w_ref[1:2, :], w_ref[2:3, :], normed, dn, up)

    return pl.pallas_call(
        body, grid=(3 * BRW // LANES, lay.b),
        in_specs=[pl.BlockSpec((1, lay.s, LANES), lambda j, b: (b, 0, first + j)),
                  pl.BlockSpec((8, LANES), lambda j, b: (0, j))],
        out_specs=pl.BlockSpec((1, lay.s, LANES), lambda j, b: (b, 0, j)),
        out_shape=jax.ShapeDtypeStruct((lay.b, lay.s, 3 * BRW), F32),
        compiler_params=_params(("arbitrary", "arbitrary")), name="gdn_conv")(p3, w8)


def _gdn_conv_bwd(lay, p3, w8, dqkv):
    dn, up = _make_shifts(lay.s, lay.t_ctx)
    first = COL_GDN * BRW // LANES

    def body(x_ref, w_ref, g_ref, dx_ref, dw_ref):
        normed = pl.program_id(0) < 2 * BRW // LANES
        fn = lambda x, w0, w1, w2: _gdn_conv_fn(x, w0, w1, w2, normed, dn, up)
        _, vjp = jax.vjp(fn, x_ref[0], w_ref[0:1, :], w_ref[1:2, :], w_ref[2:3, :])
        g = vjp(g_ref[0])
        dx_ref[0] = g[0].astype(BF16)
        dw = jnp.concatenate([g[1], g[2], g[3], jnp.zeros((5, LANES), F32)], axis=0)
        _acc(dw_ref, dw, pl.program_id(1) == 0)

    blk = pl.BlockSpec((1, lay.s, LANES), lambda j, b: (b, 0, j))
    return pl.pallas_call(
        body, grid=(3 * BRW // LANES, lay.b),
        in_specs=[pl.BlockSpec((1, lay.s, LANES), lambda j, b: (b, 0, first + j)),
                  pl.BlockSpec((8, LANES), lambda j, b: (0, j)), blk],
        out_specs=[blk, pl.BlockSpec((8, LANES), lambda j, b: (0, j))],
        out_shape=[jax.ShapeDtypeStruct((lay.b, lay.s, 3 * BRW), BF16), jax.ShapeDtypeStruct((8, 3 * BRW), F32)],
        compiler_params=_params(("arbitrary", "arbitrary")), name="gdn_conv_bwd")(p3, w8, dqkv)


def _tri_inverse(low):
    i, j = _iota(low.shape, low.ndim - 2), _iota(low.shape, low.ndim - 1) % GC
    t = (i == j).astype(F32)
    s = 1
    while s < GC:
        pair = (i // (2 * s)) == (j // (2 * s))
        off = pair & (((i // s) % 2) != ((j // s) % 2))
        cb = jnp.where(off, low, 0.0)
        t = t - (cb if s == 1 else _bdot(t, _stack_heads(_bdot(cb, _stack_heads(t), 1, 0)), 1, 0))
        s *= 2
    return t


@jax.custom_vjp
def _tri_solve(low, r1, r2):
    t = _tri_inverse(low)
    return _bdot(t, _stack_heads(r1), 1, 0), _bdot(t, _stack_heads(r2), 1, 0)


def _tri_solve_fwd(low, r1, r2):
    t = _tri_inverse(low)
    x1, x2 = _bdot(t, _stack_heads(r1), 1, 0), _bdot(t, _stack_heads(r2), 1, 0)
    return (x1, x2), (t, x1, x2)


def _tri_solve_bwd(res, g):
    t, x1, x2 = res
    bd = _block_diag(BRW, BRW)
    d1 = _unstack_heads(_bdot(t, g[0], 0, 0) * bd)
    d2 = _unstack_heads(_bdot(t, g[1], 0, 0) * bd)
    dlow = -(_bdot(d1, _stack_heads(x1), 1, 1) + _bdot(d2, _stack_heads(x2), 1, 1))
    return dlow, d1, d2


_tri_solve.defvjp(_tri_solve_fwd, _tri_solve_bwd)

N_PACK = 5


def _gdn_prep_fn(qn, kn, vv, a, alog, dtb):
    n = qn.shape[0]
    col = _iota((1, 1, LANES), 2)
    xx = a + dtb
    softplus = jnp.maximum(xx, 0.0) + jnp.log(1.0 + jnp.exp(-jnp.abs(xx)))
    g_small = jnp.where(col < 8, -jnp.exp(alog) * softplus, 0.0).reshape(n * GC, LANES)
    beta_small = jax.nn.sigmoid(a).reshape(n * GC, LANES)
    sel_col, sel_head = _iota((LANES, BRW), 0), _iota((LANES, BRW), 1) // HD
    g_l, b_l = [], []
    for d in (0, 1):
        g_l.append(_dotf(g_small, (sel_col == 4 * d + sel_head).astype(F32)))
        b_l.append(_dotf(beta_small, (sel_col == 8 + 4 * d + sel_head).astype(F32)))
    g_l = jnp.concatenate(g_l, axis=0).reshape(2 * n, GC, BRW)
    b_l = jnp.concatenate(b_l, axis=0).reshape(2 * n, GC, BRW)
    rev = _iota((2 * n, 1, 1), 0) >= n
    fwd = jnp.logical_not(rev)
    ri, ci = _iota((1, GC, GC), 1), _iota((1, GC, GC), 2)
    tri = ((fwd & (ri >= ci)) | (rev & (ri <= ci))).astype(F32)
    gc_l = lax.dot_general(tri, g_l, (((2,), (1,)), ((0,), (0,))), precision=HI,
                           preferred_element_type=F32)
    gtot_l = jnp.sum(g_l, axis=1, keepdims=True)
    i, j = _iota((1, GC, BRW), 1), _iota((1, GC, BRW), 2) % GC
    gc_t = jnp.sum(jnp.where(i == j, gc_l, 0.0), axis=1, keepdims=True)
    incl = (fwd & (i >= j)) | (rev & (i <= j))
    strict = (fwd & (i > j)) | (rev & (i < j))
    decay = jnp.where(incl, jnp.exp(jnp.where(incl, gc_l - gc_t, 0.0)), 0.0)
    kn2 = jnp.concatenate([kn, kn], axis=0)
    vv2 = jnp.concatenate([vv, vv], axis=0)
    qs = jnp.concatenate([qn, qn], axis=0) * (HD ** -0.5)
    kst = _stack_heads(kn2)
    kb = kn2 * b_l
    low = jnp.where(strict, mm_nt(kb, kst) * decay, 0.0)
    eg = jnp.exp(gc_l)
    u, w = _tri_solve(low, vv2 * b_l, kb * eg)
    k_tail = kn2 * jnp.exp(gtot_l - gc_l)
    intra = mm_nt(qs, kst) * decay
    return (u, w, k_tail, qs * eg, intra), jnp.exp(gtot_l)


def _prep_chunks(lay):
    return 4 if (lay.s // GC) % 4 == 0 else 2


def _gdn_prep_specs(lay):
    rows = _prep_chunks(lay) * GC
    return ([pl.BlockSpec((1, rows, BRW), functools.partial(lambda b, i, c: (b, i, c), c=c)) for c in range(3)]
            + [pl.BlockSpec((1, rows, LANES), lambda b, i: (b, i, COL_A128)),
               pl.BlockSpec((8, LANES), lambda b, i: (0, 0))])


def _gdn_prep(lay, qkv, p3, prm):
    nc, per = lay.s // GC, _prep_chunks(lay)

    def body(q_ref, k_ref, v_ref, a_ref, prm_ref, pack_ref, cd_ref):
        chunks = lambda ref: ref[0].reshape(per, GC, ref.shape[-1])
        pack, cd = _gdn_prep_fn(chunks(q_ref), chunks(k_ref), chunks(v_ref), chunks(a_ref),
                                prm_ref[0:1, :], prm_ref[1:2, :])
        for d in (0, 1):
            for n in range(N_PACK):
                pack_ref[0, :, d, n] = pack[n][per * d:per * (d + 1)]
            cd_ref[0, :, d] = cd[per * d:per * (d + 1)]

    return pl.pallas_call(
        body, grid=(lay.b, nc // per), in_specs=_gdn_prep_specs(lay),
        out_specs=[pl.BlockSpec((1, per, 2, N_PACK, GC, BRW), lambda b, i: (b, i, 0, 0, 0, 0)),
                   pl.BlockSpec((1, per, 2, 1, BRW), lambda b, i: (b, i, 0, 0, 0))],
        out_shape=[jax.ShapeDtypeStruct((lay.b, nc, 2, N_PACK, GC, BRW), F32),
                   jax.ShapeDtypeStruct((lay.b, nc, 2, 1, BRW), F32)],
        compiler_params=_params(("arbitrary", "arbitrary")), name="gdn_prep")(qkv, qkv, qkv, p3, prm)


def _gdn_prep_bwd(lay, qkv, p3, prm, dpacks, dcds):
    nc, per = lay.s // GC, _prep_chunks(lay)

    def body(q_ref, k_ref, v_ref, a_ref, prm_ref, dpf_ref, dpr_ref, dcf_ref, dcr_ref, dqkv_ref, da_ref, dprm_ref):
        first = (pl.program_id(0) == 0) & (pl.program_id(1) == 0)
        chunks = lambda ref: ref[0].reshape(per, GC, ref.shape[-1])
        _, vjp = jax.vjp(_gdn_prep_fn, chunks(q_ref), chunks(k_ref), chunks(v_ref), chunks(a_ref),
                         prm_ref[0:1, :], prm_ref[1:2, :])
        dpack = tuple(jnp.concatenate([dpf_ref[0, :, n], dpr_ref[0, :, n]], axis=0) for n in range(N_PACK))
        dq, dk, dv, da, dalog, ddtb = vjp((dpack, jnp.concatenate([dcf_ref[0], dcr_ref[0]], axis=0)))
        dqkv_ref[0, :, 0:BRW] = dq.reshape(per * GC, BRW)
        dqkv_ref[0, :, BRW:2 * BRW] = dk.reshape(per * GC, BRW)
        dqkv_ref[0, :, 2 * BRW:] = dv.reshape(per * GC, BRW)
        da_ref[0] = da.reshape(per * GC, LANES).astype(BF16)
        _acc(dprm_ref, jnp.concatenate([dalog, ddtb, jnp.zeros((6, LANES), F32)], axis=0), first)

    rows_blk = per * GC
    return pl.pallas_call(
        body, grid=(lay.b, nc // per),
        in_specs=_gdn_prep_specs(lay)
        + [pl.BlockSpec((1, per, N_PACK, GC, BRW), lambda b, i: (b, i, 0, 0, 0))] * 2
        + [pl.BlockSpec((1, per, 1, BRW), lambda b, i: (b, i, 0, 0))] * 2,
        out_specs=[pl.BlockSpec((1, rows_blk, 3 * BRW), lambda b, i: (b, i, 0)),
                   pl.BlockSpec((1, rows_blk, LANES), lambda b, i: (b, i, 0)),
                   pl.BlockSpec((8, LANES), lambda b, i: (0, 0))],
        out_shape=[jax.ShapeDtypeStruct((lay.b, lay.s, 3 * BRW), F32),
                   jax.ShapeDtypeStruct((lay.b, lay.s, LANES), BF16), jax.ShapeDtypeStruct((8, LANES), F32)],
        compiler_params=_params(("arbitrary", "arbitrary")), name="gdn_prep_bwd",
    )(qkv, qkv, qkv, p3, prm, *dpacks, *dcds)


def _gdn_step_fn(s, u, w, k_tail, qd, intra, cdec):
    v_new = u - mm(w, s)
    o = mm(qd, s) + mm(intra, _stack_heads(v_new))
    return s * cdec + mm_tn(k_tail, v_new) * _block_diag(BRW, BRW), o


def _order_index(nc_ctx, nc, d, step):
    rev = jnp.where(step < nc_ctx, nc_ctx - 1 - step, nc + nc_ctx - 1 - step)
    return jnp.where(d == 0, step, rev)


def _gdn_scan(lay, pack, cd):
    nc, nc_ctx = lay.s // GC, lay.t_ctx // GC
    chunk = functools.partial(_order_index, nc_ctx, nc)

    def body(pf_ref, pr_ref, cf_ref, cr_ref, of_ref, or_ref, sf_ref, sr_ref, s_scr):
        @pl.when(pl.program_id(0) == 0)
        def _():
            s_scr[...] = jnp.zeros_like(s_scr)

        nb = lay.b
        s = s_scr[...]
        st = _unstack_heads(s)
        sf_ref[:, 0] = st[:nb]
        sr_ref[:, 0] = st[nb:]
        args = [jnp.concatenate([pf_ref[:, 0, 0, n], pr_ref[:, 0, 0, n]], axis=0) for n in range(N_PACK)]
        s_new, o = _gdn_step_fn(s, *args, jnp.concatenate([cf_ref[:, 0, 0], cr_ref[:, 0, 0]], axis=0))
        of_ref[:, 0] = o[:nb]
        or_ref[:, 0] = o[nb:]
        s_scr[...] = s_new

    def pk(d):
        return pl.BlockSpec((lay.b, 1, 1, N_PACK, GC, BRW), lambda t: (0, chunk(d, t), d, 0, 0, 0))

    def cdb(d):
        return pl.BlockSpec((lay.b, 1, 1, 1, BRW), lambda t: (0, chunk(d, t), d, 0, 0))

    def out(d):
        return pl.BlockSpec((lay.b, 1, GC, BRW), lambda t: (0, chunk(d, t), 0, 0))

    return pl.pallas_call(
        body, grid=(nc,), in_specs=[pk(0), pk(1), cdb(0), cdb(1)],
        out_specs=[out(0), out(1), out(0), out(1)],
        out_shape=[jax.ShapeDtypeStruct((lay.b, nc, GC, BRW), F32)] * 4,
        scratch_shapes=[pltpu.VMEM((2 * lay.b, BRW, BRW), F32)],
        compiler_params=_params(("arbitrary",)), name="gdn_scan")(pack, pack, cd, cd)


def _gdn_scan_bwd(lay, pack, cd, states, do):
    nc, nc_ctx = lay.s // GC, lay.t_ctx // GC

    def chunk(d, t):
        return _order_index(nc_ctx, nc, d, nc - 1 - t)

    def body(pf_ref, pr_ref, cf_ref, cr_ref, sf_ref, sr_ref, dof_ref, dor_ref, dpf_ref, dpr_ref, dcf_ref, dcr_ref,
             ds_scr):
        @pl.when(pl.program_id(0) == 0)
        def _():
            ds_scr[...] = jnp.zeros_like(ds_scr)

        nb = lay.b
        both = lambda f, r: jnp.concatenate([f, r], axis=0)
        args = ([_stack_heads(both(sf_ref[:, 0], sr_ref[:, 0]))]
                + [both(pf_ref[:, 0, 0, n], pr_ref[:, 0, 0, n]) for n in range(N_PACK)]
                + [both(cf_ref[:, 0, 0], cr_ref[:, 0, 0])])
        _, vjp = jax.vjp(_gdn_step_fn, *args)
        g = vjp((ds_scr[...], both(dof_ref[...], dor_ref[...])))
        ds_scr[...] = g[0]
        for n in range(N_PACK):
            dpf_ref[:, 0, n] = g[1 + n][:nb]
            dpr_ref[:, 0, n] = g[1 + n][nb:]
        dcf_ref[:, 0] = g[1 + N_PACK][:nb]
        dcr_ref[:, 0] = g[1 + N_PACK][nb:]

    def pk(d):
        return pl.BlockSpec((lay.b, 1, 1, N_PACK, GC, BRW), lambda t: (0, chunk(d, t), d, 0, 0, 0))

    def cdb(d):
        return pl.BlockSpec((lay.b, 1, 1, 1, BRW), lambda t: (0, chunk(d, t), d, 0, 0))

    def st(d):
        return pl.BlockSpec((lay.b, 1, GC, BRW), lambda t: (0, chunk(d, t), 0, 0))

    def dob(d):
        return pl.BlockSpec((lay.b, GC, BRW), lambda t: (0, chunk(d, t), 0))

    def dpk(d):
        return pl.BlockSpec((lay.b, 1, N_PACK, GC, BRW), lambda t: (0, chunk(d, t), 0, 0, 0))

    def dcb(d):
        return pl.BlockSpec((lay.b, 1, 1, BRW), lambda t: (0, chunk(d, t), 0, 0))

    return pl.pallas_call(
        body, grid=(nc,),
        in_specs=[pk(0), pk(1), cdb(0), cdb(1), st(0), st(1), dob(0), dob(1)],
        out_specs=[dpk(0), dpk(1), dcb(0), dcb(1)],
        out_shape=[jax.ShapeDtypeStruct((lay.b, nc, N_PACK, GC, BRW), F32)] * 2
        + [jax.ShapeDtypeStruct((lay.b, nc, 1, BRW), F32)] * 2,
        scratch_shapes=[pltpu.VMEM((2 * lay.b, BRW, BRW), F32)],
        compiler_params=_params(("arbitrary",)), name="gdn_scan_bwd")(pack, pack, cd, cd, *states, do, do)


def _gdn_finish_fn(o, z, ng):
    return o * lax.rsqrt(_head_sum(o * o) * (1.0 / HD) + EPS) * ng * _silu(z)


GF_CHUNKS = 2


def _gdn_o(of_ref, or_ref):
    return jnp.concatenate([of_ref[0, k] + or_ref[0, k] for k in range(GF_CHUNKS)], axis=0)


def _gdn_finish_specs():
    rows = GF_CHUNKS * GC
    ob = pl.BlockSpec((1, GF_CHUNKS, GC, BRW), lambda b, i: (b, i, 0, 0))
    return [ob, ob, pl.BlockSpec((1, rows, BRW), lambda b, i: (b, i, COL_GDN + 3)),
            pl.BlockSpec((1, BRW), lambda b, i: (0, 0))]


def _gdn_finish(lay, o_f, o_r, p3, ng):
    rows = GF_CHUNKS * GC

    def body(of_ref, or_ref, z_ref, ng_ref, y_ref):
        y_ref[0] = _gdn_finish_fn(_gdn_o(of_ref, or_ref), z_ref[0], ng_ref[...]).astype(BF16)

    return pl.pallas_call(
        body, grid=(lay.b, lay.s // rows), in_specs=_gdn_finish_specs(),
        out_specs=pl.BlockSpec((1, rows, BRW), lambda b, i: (b, i, 0)),
        out_shape=jax.ShapeDtypeStruct((lay.b, lay.s, BRW), BF16),
        compiler_params=_params(("arbitrary", "arbitrary")), name="gdn_finish")(o_f, o_r, p3, ng)


def _gdn_finish_bwd(lay, o_f, o_r, p3, ng, dy):
    rows = GF_CHUNKS * GC

    def body(of_ref, or_ref, z_ref, ng_ref, dy_ref, do_ref, dz_ref, dng_ref):
        first = (pl.program_id(0) == 0) & (pl.program_id(1) == 0)
        _, vjp = jax.vjp(_gdn_finish_fn, _gdn_o(of_ref, or_ref), z_ref[0], ng_ref[...])
        do, dz, dng = vjp(dy_ref[0])
        do_ref[0] = do
        dz_ref[0] = dz.astype(BF16)
        _acc(dng_ref, dng, first)

    blk = pl.BlockSpec((1, rows, BRW), lambda b, i: (b, i, 0))
    return pl.pallas_call(
        body, grid=(lay.b, lay.s // rows), in_specs=_gdn_finish_specs() + [blk],
        out_specs=[blk, blk, pl.BlockSpec((1, BRW), lambda b, i: (0, 0))],
        out_shape=[jax.ShapeDtypeStruct((lay.b, lay.s, BRW), F32), jax.ShapeDtypeStruct((lay.b, lay.s, BRW), BF16),
                   jax.ShapeDtypeStruct((1, BRW), F32)],
        compiler_params=_params(("arbitrary", "arbitrary")), name="gdn_finish_bwd")(o_f, o_r, p3, ng, dy)


def _rope_tables(lay):
    t = jnp.arange(lay.t_lat)
    lane = np.arange(BRW)
    dim = lane % HD
    inv = jnp.asarray(ROPE_BASE ** (-(dim % 16).astype(np.float32) / 16.0), F32)
    pos = jnp.where((dim // 32 == 0)[None, :], (t // GRID_W)[:, None], (t % GRID_W)[:, None]).astype(F32)
    ang = pos * inv[None, :]
    cos = jnp.concatenate([jnp.ones((lay.t_ctx, BRW), F32), jnp.cos(ang)], axis=0)
    sin = jnp.concatenate([jnp.zeros((lay.t_ctx, BRW), F32), jnp.sin(ang)], axis=0)
    return cos, sin


def _pad_rows(a, rows):
    return jnp.concatenate([a, jnp.zeros((rows - a.shape[0],) + a.shape[1:], a.dtype)], axis=0)


def _layer_fwd(lay, xc, wl, cos, sin):
    p, ht = _inproj_fwd(lay, xc, wl["mod3"], wl["gpre"], wl["win"])
    p3 = p.reshape(lay.b, lay.s, W_PAD)
    nctx = lay.t_ctx // RC
    states = _state_scan(lay, _ret_state(lay, p3, cos, sin), nctx, False, "ret_scan")
    y_ret = _ret_out(lay, p3, cos, sin, states, wl["ret_ng"])
    y_sg = _sg_fwd(lay, p3, wl["sgw"], wl["sgb8"])
    y_sc = _sc_fwd(lay, p3, wl["scw8"])
    qkv = _gdn_conv(lay, p3, wl["gdnw8"])
    pack, cd = _gdn_prep(lay, qkv, p3, wl["prm"])
    o_f, o_r, st_f, st_r = _gdn_scan(lay, pack, cd)
    y_gdn = _gdn_finish(lay, o_f, o_r, p3, wl["gdn_ng"])
    ys = [y.reshape(lay.rows, BRW) for y in (y_ret, y_sg, y_sc, y_gdn)]
    xc_new, yt = _outproj_fwd(lay, ys, xc, wl["wout"], wl["gpost"], wl["mod3"])
    saved = dict(xc=xc, p3=p3, ht=ht, yt=yt, states=states, qkv=qkv, pack=pack, cd=cd, o_f=o_f, o_r=o_r, gstates=(st_f, st_r),
                 ys=ys)
    return xc_new, saved


def _layer_bwd(lay, sv, wl, cos, sin, dxc):
    p3 = sv["p3"]
    as3 = lambda a: a.reshape(lay.b, lay.s, a.shape[-1])
    as2 = lambda a: a.reshape(lay.rows, a.shape[-1])
    dy_ret, dy_sg, dy_sc, dy_gdn, do_, dgpost, dgate = _outproj_bwd(
        lay, sv["ys"], sv["xc"], wl["wout"], wl["gpost"], wl["mod3"], dxc)
    dwout = _weight_grad(lay, sv["yt"], do_, "wout_grad")
    nctx = lay.t_ctx // RC
    dpr, dstates, dret_ng = _ret_out_bwd(lay, p3, cos, sin, sv["states"], wl["ret_ng"], as3(dy_ret))
    d_a = _state_scan(lay, dstates, nctx, True, "ret_scan_bwd")
    dp_ret = _ret_state_bwd(lay, p3, cos, sin, d_a, dpr)
    dp_sg, dsgw, dsgb8 = _sg_bwd(lay, p3, wl["sgw"], wl["sgb8"], as3(dy_sg))
    dsb, dsc_, dsh_, dsz, dscw8 = _sc_bwd(lay, p3, wl["scw8"], as3(dy_sc))
    do, dgz, dgdn_ng = _gdn_finish_bwd(lay, sv["o_f"], sv["o_r"], p3, wl["gdn_ng"], as3(dy_gdn))
    dpf, dpr_, dcf, dcr = _gdn_scan_bwd(lay, sv["pack"], sv["cd"], sv["gstates"], do)
    dqkv, da, dprm = _gdn_prep_bwd(lay, sv["qkv"], p3, wl["prm"], (dpf, dpr_), (dcf, dcr))
    dp_gqkv, dgdnw8 = _gdn_conv_bwd(lay, p3, wl["gdnw8"], dqkv)
    pieces = [(as2(dp_ret), 0), (as2(dp_sg), COL_SG * BRW), (as2(dsb), COL_SC * BRW), (as2(dsc_), (COL_SC + 1) * BRW),
              (as2(dsh_), (COL_SC + 2) * BRW), (as2(dsz), (COL_SC + 3) * BRW), (as2(dp_gqkv), COL_GDN * BRW),
              (as2(dgz), (COL_GDN + 3) * BRW), (as2(da), COL_A128 * LANES)]
    dxc_prev, dgpre, dshift, dscale = _inproj_bwd(lay, sv["xc"], wl["mod3"], wl["gpre"], wl["wint"], dxc, pieces)
    dws = [_weight_grad(lay, sv["ht"], dp, "win_grad_%d" % off) for dp, off in pieces]
    dwin = jnp.concatenate(dws[:-1] + [dws[-1][:, :W_IN - COL_A128 * LANES]], axis=1)

    def rows3(g):
        return jnp.concatenate([g[1], g[3], g[0] + g[2]], axis=0)

    dmod = _pad_rows(jnp.concatenate([rows3(dshift), rows3(dscale), rows3(dgate)], axis=1), 8)
    grads = dict(win=dwin, wout=dwout, gpre=dgpre[0], gpost=dgpost[0], ret_ng=dret_ng[0], sgw=dsgw, sgb=dsgb8[:NH],
                 scw=dscw8[:3], gdnw=dgdnw8[:3], alog=dprm[0, :2 * NH].reshape(2, NH),
                 dtb=dprm[1, :2 * NH].reshape(2, NH), gdn_ng=dgdn_ng.reshape(NH, HD).sum(axis=0), dmod=dmod)
    return dxc_prev, grads


def _local_step(x, c, ctx, c_ctx, wmod, bmod, gpre, gpost, win, wout, ret_ng, sgw, sgb, scw, gdnw, alog, dtb,
                gdn_ng, target):
    depth = wmod.shape[0]
    lay = _Lay(x.shape[0], ctx.shape[1], x.shape[1])
    assert lay.b == 2 and lay.t_ctx % RC == 0 and lay.t_lat % RC == 0
    cos, sin = _rope_tables(lay)
    cvec8 = _pad_rows(jnp.concatenate([c, c_ctx[None]], axis=0), 8)
    mod = _mod_fwd(cvec8, wmod, bmod[:, None, :])
    wint = jnp.swapaxes(win, 1, 2)
    xc = jnp.concatenate([ctx, x], axis=1).reshape(lay.rows, D)
    layers, saved = [], []
    for l in range(depth):
        wl = dict(mod3=mod[l].reshape(8, 3, D).transpose(1, 0, 2)[:, :, None, :], gpre=gpre[l][None], gpost=gpost[l][None],
                  win=win[l], wint=wint[l], wout=wout[l], ret_ng=ret_ng[l][None], sgw=sgw[l],
                  sgb8=_pad_rows(sgb[l], 8), scw8=_pad_rows(scw[l], 8), gdnw8=_pad_rows(gdnw[l], 8),
                  prm=_pad_rows(jnp.pad(jnp.stack([alog[l].reshape(-1), dtb[l].reshape(-1)]),
                                        ((0, 0), (0, LANES - 2 * NH))), 8),
                  gdn_ng=jnp.tile(gdn_ng[l], NH)[None])
        xc, sv = _layer_fwd(lay, xc, wl, cos, sin)
        layers.append(wl)
        saved.append(sv)
    loss, dxc3 = _loss_kernel(lay, xc.reshape(lay.b, lay.s, D), target)
    dxc = dxc3.reshape(lay.rows, D)
    grads = [None] * depth
    for l in reversed(range(depth)):
        dxc, grads[l] = _layer_bwd(lay, saved[l], layers[l], cos, sin, dxc)
    stacked = {k: jnp.stack([g[k] for g in grads]) for k in grads[0]}
    dcvec8, dbmod = _mod_bwd(stacked["dmod"], wmod, cvec8)
    stacked["bmod"] = dbmod[:, 0, :]
    stacked["c_ctx"] = dcvec8[2]
    dx = dxc.reshape(lay.b, lay.s, D)[:, lay.t_ctx:, :]
    return loss, dx, stacked, cvec8


MESH = pl.DeviceIdType.MESH
ANY = pl.BlockSpec(memory_space=pl.ANY)


def _me():
    return lax.axis_index("x"), lax.axis_index("y"), lax.axis_index("c")


def _gather_weights(shards, fulls, blocks):
    n = len(shards)

    def body(*refs):
        ins, outs = refs[:n], refs[n:2 * n]
        send_sems, recv_sems, loc_sems = refs[2 * n:]
        x, y, c = _me()
        me, sibling = (x, y, c), (x, y, 1 - c)
        chips = [(1 - x, y), (x, 1 - y), (1 - x, 1 - y)]

        def blk(a, dev):
            return blocks[a](outs[a], 4 * dev[0] + 2 * dev[1] + dev[2])

        def copy(a, k, block, to, src=None):
            return pltpu.make_async_remote_copy(
                src_ref=blk(a, block) if src is None else src, dst_ref=blk(a, block), send_sem=send_sems.at[a, k],
                recv_sem=recv_sems.at[a, k], device_id=to, device_id_type=MESH)

        mine = [pltpu.make_async_copy(ins[a], blk(a, me), loc_sems.at[a]) for a in range(n)]
        for cp in mine:
            cp.start()
        first = []
        for a in range(n):
            first.append(copy(a, 0, me, sibling, src=ins[a]))
            first += [copy(a, 1 + j, me, (*chip, c), src=ins[a]) for j, chip in enumerate(chips)]
        for cp in first:
            cp.start()
        passed = []
        for j, chip in enumerate(chips):
            for a in range(n):
                copy(a, 1 + j, (*chip, c), me).wait_recv()
                fwd = copy(a, 4 + j, (*chip, c), sibling)
                fwd.start()
                passed.append(fwd)
        for a in range(n):
            copy(a, 0, sibling, me).wait_recv()
            for j, chip in enumerate(chips):
                copy(a, 4 + j, (*chip, 1 - c), me).wait_recv()
        for cp in first + passed:
            cp.wait_send()
        for cp in mine:
            cp.wait()

    return pl.pallas_call(
        body, in_specs=[ANY] * n, out_specs=[ANY] * n,
        out_shape=[jax.ShapeDtypeStruct(f, s.dtype) for f, s in zip(fulls, shards)],
        scratch_shapes=[pltpu.SemaphoreType.DMA((n, 7)), pltpu.SemaphoreType.DMA((n, 7)),
                        pltpu.SemaphoreType.DMA((n,))],
        name="gather_weights")(*shards)


def _scatter_pair(srcs, slabs, slab_shapes):
    n = len(srcs)

    def body(*refs):
        ins, outs = refs[:n], refs[n:2 * n]
        send_sems, recv_sems = refs[2 * n:]
        x, y, c = _me()
        cps = []
        for a in range(n):
            for q in range(4):
                j = 2 * q + (1 - c)
                cps.append(pltpu.make_async_remote_copy(
                    src_ref=slabs[a](ins[a], j), dst_ref=outs[a].at[q], send_sem=send_sems.at[a, q],
                    recv_sem=recv_sems.at[a, q], device_id=(x, y, 1 - c), device_id_type=MESH))
        for cp in cps:
            cp.start()
        for cp in cps:
            cp.wait_recv()
        for cp in cps:
            cp.wait_send()

    return pl.pallas_call(
        body, in_specs=[ANY] * n, out_specs=[ANY] * n,
        out_shape=[jax.ShapeDtypeStruct((4,) + tuple(shp), s.dtype) for shp, s in zip(slab_shapes, srcs)],
        scratch_shapes=[pltpu.SemaphoreType.DMA((n, 4)), pltpu.SemaphoreType.DMA((n, 4))],
        name="scatter_pair")(*srcs)


def _scatter_chips(parts):
    n = len(parts)

    def body(*refs):
        ins, outs = refs[:n], refs[n:2 * n]
        send_sems, recv_sems = refs[2 * n:]
        x, y, c = _me()
        chips = [(1 - x, y), (x, 1 - y), (1 - x, 1 - y)]
        cps = []
        for a in range(n):
            for k, (px, py) in enumerate(chips):
                cps.append(pltpu.make_async_remote_copy(
                    src_ref=ins[a].at[2 * px + py], dst_ref=outs[a].at[k], send_sem=send_sems.at[a, k],
                    recv_sem=recv_sems.at[a, k], device_id=(px, py, c), device_id_type=MESH))
        for cp in cps:
            cp.start()
        for cp in cps:
            cp.wait_recv()
        for cp in cps:
            cp.wait_send()

    return pl.pallas_call(
        body, in_specs=[ANY] * n, out_specs=[ANY] * n,
        out_shape=[jax.ShapeDtypeStruct((3,) + p.shape[1:], p.dtype) for p in parts],
        scratch_shapes=[pltpu.SemaphoreType.DMA((n, 3)), pltpu.SemaphoreType.DMA((n, 3))],
        name="scatter_chips")(*parts)


def _add_rows(arrs, out_dtype, name):
    shp = arrs[0].shape
    two = [a.reshape(-1, shp[-1]) for a in arrs]
    rows, cols = two[0].shape
    tr = _row_tile(rows, 1024)

    def body(*refs):
        acc = refs[0][...].astype(F32)
        for r in refs[1:-1]:
            acc = acc + r[...].astype(F32)
        refs[-1][...] = acc.astype(out_dtype)

    blk = pl.BlockSpec((tr, cols), lambda i: (i, 0))
    return pl.pallas_call(
        body, grid=(rows // tr,), in_specs=[blk] * len(two), out_specs=blk,
        out_shape=jax.ShapeDtypeStruct((rows, cols), out_dtype),
        compiler_params=_params(("arbitrary",)), name=name)(*two).reshape(shp)


def _exchange(src, name):
    blk = src.shape[-2:]

    def body(src_ref, out_ref, send_sems, recv_sems, loc_sem):
        x, y, c = lax.axis_index("x"), lax.axis_index("y"), lax.axis_index("c")
        me = 4 * x + 2 * y + c

        def block(j):
            return src_ref

        def remote(k, src_blk, dst_blk, peer_xyz):
            return pltpu.make_async_remote_copy(
                src_ref=block(src_blk), dst_ref=out_ref.at[dst_blk], send_sem=send_sems.at[k], recv_sem=recv_sems.at[k],
                device_id=peer_xyz, device_id_type=pl.DeviceIdType.MESH)

        local = pltpu.make_async_copy(block(me), out_ref.at[me], loc_sem)
        local.start()
        peers = []
        for k in range(1, N_DEV):
            px = 1 - x if k & 4 else x
            py = 1 - y if k & 2 else y
            pc = 1 - c if k & 1 else c
            peers.append((4 * px + 2 * py + pc, (px, py, pc)))
        sends = [remote(k, peer, me, xyz) for k, (peer, xyz) in enumerate(peers)]
        for cp in sends:
            cp.start()
        for k, (peer, xyz) in enumerate(peers):
            remote(k, peer, peer, xyz).wait_recv()
        for cp in sends:
            cp.wait_send()
        local.wait()

    return pl.pallas_call(
        body, in_specs=[pl.BlockSpec(memory_space=pl.ANY)], out_specs=pl.BlockSpec(memory_space=pl.ANY),
        out_shape=jax.ShapeDtypeStruct((N_DEV,) + blk, src.dtype),
        scratch_shapes=[pltpu.SemaphoreType.DMA((N_DEV - 1,)), pltpu.SemaphoreType.DMA((N_DEV - 1,)),
                        pltpu.SemaphoreType.DMA(())],
        name=name)(src)


def _row_tile(rows, cap):
    best = 8
    for t in range(8, min(rows, cap) + 1, 8):
        if rows % t == 0:
            best = t
    return best


def _sum_devices(x):
    _, rows, cols = x.shape
    tr = _row_tile(rows, 2048)

    def body(x_ref, o_ref):
        acc = x_ref[0]
        for j in range(1, N_DEV):
            acc = acc + x_ref[j]
        o_ref[...] = acc

    return pl.pallas_call(
        body, grid=(rows // tr,), in_specs=[pl.BlockSpec((N_DEV, tr, cols), lambda i: (0, i, 0))],
        out_specs=pl.BlockSpec((tr, cols), lambda i: (i, 0)), out_shape=jax.ShapeDtypeStruct((rows, cols), F32),
        compiler_params=_params(("arbitrary",)), name="sum_devices")(x)


def _adamw(w, g, m, v, name):
    rows, cols = w.shape
    tr = _row_tile(rows, 512)
    bc1 = 1.0 - ADAM_B1 ** ADAM_STEP
    bc2 = 1.0 - ADAM_B2 ** ADAM_STEP

    def body(w_ref, g_ref, m_ref, v_ref, d_ref, nm_ref, nv_ref):
        g_ = g_ref[...]
        m_ = ADAM_B1 * m_ref[...] + (1.0 - ADAM_B1) * g_
        v_ = ADAM_B2 * v_ref[...] + (1.0 - ADAM_B2) * (g_ * g_)
        d_ref[...] = -ADAM_LR * ((m_ / bc1) / (jnp.sqrt(v_ / bc2) + ADAM_EPS) + ADAM_WD * w_ref[...])
        nm_ref[...] = m_
        nv_ref[...] = v_

    blk = pl.BlockSpec((tr, cols), lambda i: (i, 0))
    return pl.pallas_call(
        body, grid=(rows // tr,), in_specs=[blk] * 4, out_specs=[blk] * 3,
        out_shape=[jax.ShapeDtypeStruct((rows, cols), F32)] * 3,
        compiler_params=_params(("arbitrary",)), name=name)(w, g, m, v)


def _pack_rows(shape):
    return -(-int(np.prod(shape)) // (16 * LANES)) * 16


def _pack(arrs, dtype=F32):
    blocks = []
    for a in arrs:
        flat = a.reshape(-1).astype(dtype)
        rows = _pack_rows(a.shape)
        blocks.append(jnp.pad(flat, (0, rows * LANES - flat.shape[0])).reshape(rows, LANES))
    return jnp.concatenate(blocks, axis=0)


def _unpack(packed, shapes):
    out, off = [], 0
    for s in shapes:
        rows = _pack_rows(s)
        out.append(packed[off:off + rows].reshape(-1)[:int(np.prod(s))].reshape(s))
        off += rows
    return out


SMALL = ("c_ctx", "b_mod", "g_pre", "g_post", "ret_norm_g", "sg_w", "sg_b", "sc_conv_w", "gdn_conv_w", "gdn_a_log",
         "gdn_dt_bias", "gdn_norm_g")
ORDER = ("c_ctx", "w_mod", "b_mod", "g_pre", "g_post", "w_in", "w_out", "ret_norm_g", "sg_w", "sg_b", "sc_conv_w",
         "gdn_conv_w", "gdn_a_log", "gdn_dt_bias", "gdn_norm_g")


def kernel(x, c, ctx, c_ctx, w_mod, b_mod, g_pre, g_post, w_in, w_out, ret_norm_g, sg_w, sg_b, sc_conv_w, gdn_conv_w, gdn_a_log, gdn_dt_bias, gdn_norm_g, loss_target, m_c_ctx, m_w_mod, m_b_mod, m_g_pre, m_g_post, m_w_in, m_w_out, m_ret_norm_g, m_sg_w, m_sg_b, m_sc_conv_w, m_gdn_conv_w, m_gdn_a_log, m_gdn_dt_bias, m_gdn_norm_g, v_c_ctx, v_w_mod, v_b_mod, v_g_pre, v_g_post, v_w_in, v_w_out, v_ret_norm_g, v_sg_w, v_sg_b, v_sc_conv_w, v_gdn_conv_w, v_gdn_a_log, v_gdn_dt_bias, v_gdn_norm_g):
    wts = dict(c_ctx=c_ctx, w_mod=w_mod, b_mod=b_mod, g_pre=g_pre, g_post=g_post, w_in=w_in, w_out=w_out,
               ret_norm_g=ret_norm_g, sg_w=sg_w, sg_b=sg_b, sc_conv_w=sc_conv_w, gdn_conv_w=gdn_conv_w,
               gdn_a_log=gdn_a_log, gdn_dt_bias=gdn_dt_bias, gdn_norm_g=gdn_norm_g)
    mom = dict(c_ctx=m_c_ctx, w_mod=m_w_mod, b_mod=m_b_mod, g_pre=m_g_pre, g_post=m_g_post, w_in=m_w_in, w_out=m_w_out,
               ret_norm_g=m_ret_norm_g, sg_w=m_sg_w, sg_b=m_sg_b, sc_conv_w=m_sc_conv_w, gdn_conv_w=m_gdn_conv_w,
               gdn_a_log=m_gdn_a_log, gdn_dt_bias=m_gdn_dt_bias, gdn_norm_g=m_gdn_norm_g)
    var = dict(c_ctx=v_c_ctx, w_mod=v_w_mod, b_mod=v_b_mod, g_pre=v_g_pre, g_post=v_g_post, w_in=v_w_in, w_out=v_w_out,
               ret_norm_g=v_ret_norm_g, sg_w=v_sg_w, sg_b=v_sg_b, sc_conv_w=v_sc_conv_w, gdn_conv_w=v_gdn_conv_w,
               gdn_a_log=v_gdn_a_log, gdn_dt_bias=v_gdn_dt_bias, gdn_norm_g=v_gdn_norm_g)
    depth = w_mod.shape[0]
    n_mod, n_in, n_out = w_mod.shape[2], w_in.shape[2], w_out.shape[1]
    n_sc, n_gdn = sc_conv_w.shape[2], gdn_conv_w.shape[2]
    xi, yi, ci = _me()
    me = 4 * xi + 2 * yi + ci

    conv = _pack([sc_conv_w, gdn_conv_w])
    n_conv = depth * 3 * n_sc
    shards = [w_mod.astype(BF16), w_in.astype(BF16), w_out.astype(BF16), conv]
    fulls = [(depth, D, N_DEV * n_mod), (N_DEV, depth, D, n_in), (depth, N_DEV * n_out, D), (N_DEV,) + conv.shape]
    blocks = [lambda r, j: r.at[:, :, pl.ds(pl.multiple_of(j * n_mod, LANES), n_mod)],
              lambda r, j: r.at[j],
              lambda r, j: r.at[:, pl.ds(pl.multiple_of(j * n_out, 16), n_out), :],
              lambda r, j: r.at[j]]
    wmod_f, win_g, wout_f, conv_g = _gather_weights(shards, fulls, blocks)
    win_f = jnp.pad(win_g.transpose(1, 2, 0, 3).reshape(depth, D, N_DEV * n_in),
                    ((0, 0), (0, 0), (0, W_PAD - N_DEV * n_in)))
    r_sc = _pack_rows(sc_conv_w.shape)
    scw_f = conv_g[:, :r_sc].reshape(N_DEV, -1)[:, :n_conv]
    scw_f = scw_f.reshape(N_DEV, depth, 3, n_sc).transpose(1, 2, 0, 3).reshape(depth, 3, -1)
    gdnw_f = conv_g[:, r_sc:].reshape(N_DEV, -1)[:, :depth * 3 * n_gdn]
    gdnw_f = gdnw_f.reshape(N_DEV, depth, 3, n_gdn).transpose(1, 2, 0, 3)
    gdnw_f = gdnw_f.reshape(depth, 3, -1)

    loss8, dx, g, cvec8 = _local_step(x, c, ctx, c_ctx, wmod_f, b_mod, g_pre, g_post, win_f, wout_f, ret_norm_g, sg_w,
                                      sg_b, scw_f, gdnw_f, gdn_a_log, gdn_dt_bias, gdn_norm_g, loss_target)

    gin = g["win"].astype(BF16).reshape(depth, D, N_DEV, n_in).transpose(2, 0, 1, 3)
    gout = g["wout"].astype(BF16)
    slabs = [lambda r, j: r.at[j], lambda r, j: r.at[:, pl.ds(pl.multiple_of(j * n_out, 16), n_out), :]]
    got_in, got_out = _scatter_pair([gin, gout], slabs, [(depth, D, n_in), (depth, n_out, D)])
    mine_in = lax.dynamic_index_in_dim(gin.reshape(4, 2, depth, D, n_in), ci, axis=1, keepdims=False)
    mine_out = lax.dynamic_index_in_dim(gout.reshape(depth, 4, 2, n_out, D), ci, axis=2, keepdims=False)
    mine_out = mine_out.transpose(1, 0, 2, 3)
    far_in, far_out = _scatter_chips([_add_rows([mine_in, got_in], BF16, "pair_sum_in"),
                                      _add_rows([mine_out, got_out], BF16, "pair_sum_out")])
    chip = 2 * xi + yi
    own = lambda a: lax.dynamic_index_in_dim(a, chip, axis=0, keepdims=False)
    grad = dict(w_in=_add_rows([own(mine_in), own(got_in), far_in[0], far_in[1], far_in[2]], F32, "grad_sum_in"),
                w_out=_add_rows([own(mine_out), own(got_out), far_out[0], far_out[1], far_out[2]], F32,
                                "grad_sum_out"))

    local_small = dict(c_ctx=g["c_ctx"], b_mod=g["bmod"], g_pre=g["gpre"], g_post=g["gpost"], ret_norm_g=g["ret_ng"],
                       sg_w=g["sgw"], sg_b=g["sgb"], sc_conv_w=g["scw"], gdn_conv_w=g["gdnw"], gdn_a_log=g["alog"],
                       gdn_dt_bias=g["dtb"], gdn_norm_g=g["gdn_ng"])
    to_sum = _pack([loss8[0, :1]] + [local_small[k] for k in SMALL])
    rows_sum = to_sum.shape[0]
    as_is = _pack([cvec8[:3], g["dmod"][:, :3, :]])
    everyone = _exchange(jnp.concatenate([to_sum, as_is], axis=0), "gather_small")
    small_sum = _unpack(_sum_devices(everyone[:, :rows_sum]), [(1,)] + [local_small[k].shape for k in SMALL])
    loss = small_sum[0][0]
    for k, val in zip(SMALL, small_sum[1:]):
        grad[k] = val
    grad["sc_conv_w"] = lax.dynamic_slice_in_dim(grad["sc_conv_w"], me * n_sc, n_sc, axis=2)
    grad["gdn_conv_w"] = lax.dynamic_slice_in_dim(grad["gdn_conv_w"], me * n_gdn, n_gdn, axis=2)
    r_c = _pack_rows((3, D))
    c_all = everyone[:, rows_sum:rows_sum + r_c].reshape(N_DEV, -1)[:, :3 * D].reshape(N_DEV * 3, D)
    dmod_all = everyone[:, rows_sum + r_c:].reshape(N_DEV, -1)[:, :depth * 9 * D]
    dmod_all = dmod_all.reshape(N_DEV, depth, 3, 3 * D).transpose(1, 0, 2, 3)
    dmod_mine = lax.dynamic_slice_in_dim(dmod_all.reshape(depth, N_DEV * 3, 3 * D), me * n_mod, n_mod, axis=2)
    grad["w_mod"] = _wmod_grad(_pad_rows(c_all, 32), jnp.pad(dmod_mine, ((0, 0), (0, 32 - N_DEV * 3), (0, 0))))

    delta, new_m, new_v = {}, {}, {}
    for k in ("w_mod", "w_in", "w_out"):
        shp = wts[k].shape
        two = lambda a: a.reshape(-1, shp[-1])
        res = _adamw(two(wts[k]), two(grad[k]), two(mom[k]), two(var[k]), "adamw_" + k)
        delta[k], new_m[k], new_v[k] = [r.reshape(shp) for r in res]
    res = _adamw(*[_pack([d[k] for k in SMALL]) for d in (wts, grad, mom, var)], "adamw_small")
    for dst, flat in zip((delta, new_m, new_v), res):
        for k, val in zip(SMALL, _unpack(flat, [wts[k].shape for k in SMALL])):
            dst[k] = val
    return (loss, dx, *[grad[k] for k in ORDER], *[delta[k] for k in ORDER], *[new_m[k] for k in ORDER],
            *[new_v[k] for k in ORDER])
```

```python
import functools
import math

import jax
import jax.numpy as jnp
import numpy as np
from jax import lax
from jax.experimental import pallas as pl
from jax.experimental.pallas import tpu as pltpu

F32, BF16 = jnp.float32, jnp.bfloat16
HI = lax.Precision.HIGHEST

N_DEV = 8
D = 1024
DEPTH = 4
BRW = 256
HD = 64
NH = 4
LANES = 128
GRID_W = 64
ROPE_BASE = 10000.0
W_IN = 15 * BRW + 4 * NH
W_PAD = 31 * LANES
RC = 128
GC = 64
EPS = 1e-6
LOG_GAMMA = tuple(math.log(1.0 - 2.0 ** (-5.0 - h)) for h in range(NH))
ADAM_LR, ADAM_B1, ADAM_B2, ADAM_EPS, ADAM_WD, ADAM_STEP = 0.001, 0.9, 0.999, 1e-08, 0.01, 10
VMEM_LIMIT = 56 * 1024 * 1024

COL_RET, COL_SG, COL_SC, COL_GDN = 0, 4, 7, 11
COL_A128 = 30


def _params(sem):
    return pltpu.CompilerParams(dimension_semantics=sem, vmem_limit_bytes=VMEM_LIMIT)


def _bdot(a, b, ca, cb):
    if a.ndim == 3:
        dn = (((ca + 1,), (cb + 1,)), ((0,), (0,)))
    else:
        dn = (((ca,), (cb,)), ((), ()))
    return lax.dot_general(a.astype(BF16), b.astype(BF16), dn, preferred_element_type=F32)


@jax.custom_vjp
def mm(a, b):
    return _bdot(a, b, 1, 0)


mm.defvjp(lambda a, b: (_bdot(a, b, 1, 0), (a, b)),
          lambda r, g: (_bdot(g, r[1], 1, 1), _bdot(r[0], g, 0, 0)))


@jax.custom_vjp
def mm_nt(a, b):
    return _bdot(a, b, 1, 1)


mm_nt.defvjp(lambda a, b: (_bdot(a, b, 1, 1), (a, b)),
             lambda r, g: (_bdot(g, r[1], 1, 0), _bdot(g, r[0], 0, 0)))


@jax.custom_vjp
def mm_tn(a, b):
    return _bdot(a, b, 0, 0)


mm_tn.defvjp(lambda a, b: (_bdot(a, b, 0, 0), (a, b)),
             lambda r, g: (_bdot(r[1], g, 1, 1), _bdot(r[0], g, 1, 0)))


def _dotf(a, b):
    return jnp.dot(a, b, precision=HI, preferred_element_type=F32)


def _iota(shape, dim):
    return lax.broadcasted_iota(jnp.int32, shape, dim)


def _head_mask(h, width=BRW):
    return (_iota((1, width), 1) // HD == h).astype(F32)


def _lane_by_head(vals, width=BRW, lane0=0):
    head = (_iota((1, width), 1) + lane0) // HD
    out = jnp.full((1, width), vals[NH - 1], F32)
    for h in range(NH - 2, -1, -1):
        out = jnp.where(head == h, vals[h], out)
    return out


def _block_diag(n, width):
    return (_iota((n, width), 0) // HD == _iota((n, width), 1) // HD).astype(F32)


@jax.custom_vjp
def _head_sum(x):
    w = x.shape[1]
    ones = _block_diag(w, w).astype(BF16)
    hi = x.astype(BF16)
    lo = (x - hi.astype(F32)).astype(BF16)
    return jnp.dot(hi, ones, preferred_element_type=F32) + jnp.dot(lo, ones, preferred_element_type=F32)


_head_sum.defvjp(lambda x: (_head_sum(x), None), lambda _, g: (_head_sum(g),))


def _silu(x):
    return x * jax.nn.sigmoid(x)


def _stack_heads(x):
    return jnp.concatenate([x * _head_mask(h) for h in range(NH)], axis=-2)


@jax.custom_vjp
def _unstack_heads(x):
    n = x.shape[-2] // NH
    return (x[..., 0:n, :] + x[..., n:2 * n, :]) + (x[..., 2 * n:3 * n, :] + x[..., 3 * n:4 * n, :])


_unstack_heads.defvjp(lambda x: (_unstack_heads(x), None), lambda _, g: (_stack_heads(g),))


@jax.custom_vjp
def _rot_half(x):
    n = x.shape[1]
    first = (_iota(x.shape, 1) % 32) < 16
    return jnp.where(first, -pltpu.roll(x, n - 16, 1), pltpu.roll(x, 16, 1))


_rot_half.defvjp(lambda x: (_rot_half(x), None), lambda _, g: (-_rot_half(g),))


def _rotary(x, cos, sin):
    return x * cos + _rot_half(x) * sin


def _make_shifts(seq, t_ctx):
    def dn_raw(x):
        r = _iota(x.shape, 0)
        return jnp.where((r == 0) | (r == t_ctx), 0.0, pltpu.roll(x, 1, 0))

    def up_raw(x):
        r = _iota(x.shape, 0)
        return jnp.where((r == t_ctx - 1) | (r == seq - 1), 0.0, pltpu.roll(x, seq - 1, 0))

    @jax.custom_vjp
    def dn(x):
        return dn_raw(x)

    @jax.custom_vjp
    def up(x):
        return up_raw(x)

    dn.defvjp(lambda x: (dn_raw(x), None), lambda _, g: (up_raw(g),))
    up.defvjp(lambda x: (up_raw(x), None), lambda _, g: (dn_raw(g),))
    return dn, up


def _conv3(t, w0, w1, w2, dn, up):
    return dn(t) * w0 + t * w1 + up(t) * w2


def _acc(ref, val, first, at=()):
    idx = at + (Ellipsis,)

    @pl.when(first)
    def _():
        ref[idx] = val

    @pl.when(jnp.logical_not(first))
    def _():
        ref[idx] += val


def _mod_fwd(cvec8, wmod, bmod):
    depth = wmod.shape[0]

    def body(c_ref, w_ref, b_ref, o_ref):
        sc = _silu(c_ref[...])
        o_ref[0] = jnp.dot(sc.astype(BF16), w_ref[0], preferred_element_type=F32) + b_ref[0]

    return pl.pallas_call(
        body, grid=(depth, 3),
        in_specs=[pl.BlockSpec((8, D), lambda l, j: (0, 0)),
                  pl.BlockSpec((1, D, D), lambda l, j: (l, 0, j)),
                  pl.BlockSpec((1, 1, D), lambda l, j: (l, 0, j))],
        out_specs=pl.BlockSpec((1, 8, D), lambda l, j: (l, 0, j)),
        out_shape=jax.ShapeDtypeStruct((depth, 8, 3 * D), F32),
        compiler_params=_params(("arbitrary", "arbitrary")), name="mod_fwd")(cvec8, wmod, bmod)


def _mod_bwd(dmod, wmod, cvec8):
    depth = wmod.shape[0]

    def body(dm_ref, w_ref, c_ref, dc_ref, db_ref):
        l, j = pl.program_id(0), pl.program_id(1)
        dm = dm_ref[0]
        db_ref[0] = jnp.sum(dm, axis=0, keepdims=True)
        part = _bdot(dm, w_ref[0], 1, 1)
        _acc(dc_ref, part, (l == 0) & (j == 0))

        @pl.when((l == depth - 1) & (j == 2))
        def _():
            c = c_ref[...]
            s = jax.nn.sigmoid(c)
            dc_ref[...] = dc_ref[...] * (s * (1.0 + c * (1.0 - s)))

    return pl.pallas_call(
        body, grid=(depth, 3),
        in_specs=[pl.BlockSpec((1, 8, D), lambda l, j: (l, 0, j)),
                  pl.BlockSpec((1, D, D), lambda l, j: (l, 0, j)),
                  pl.BlockSpec((8, D), lambda l, j: (0, 0))],
        out_specs=[pl.BlockSpec((8, D), lambda l, j: (0, 0)),
                   pl.BlockSpec((1, 1, D), lambda l, j: (l, 0, j))],
        out_shape=[jax.ShapeDtypeStruct((8, D), F32), jax.ShapeDtypeStruct((depth, 1, 3 * D), F32)],
        compiler_params=_params(("arbitrary", "arbitrary")), name="mod_bwd")(dmod, wmod, cvec8)


def _wmod_grad(c_rows, dmod_cols):
    depth, rows, n = dmod_cols.shape

    def body(c_ref, dm_ref, o_ref):
        sc = _silu(c_ref[...])
        o_ref[0] = lax.dot_general(sc, dm_ref[0], (((0,), (0,)), ((), ())), precision=HI,
                                   preferred_element_type=F32)

    return pl.pallas_call(
        body, grid=(depth,),
        in_specs=[pl.BlockSpec((rows, D), lambda l: (0, 0)), pl.BlockSpec((1, rows, n), lambda l: (l, 0, 0))],
        out_specs=pl.BlockSpec((1, D, n), lambda l: (l, 0, 0)),
        out_shape=jax.ShapeDtypeStruct((depth, D, n), F32),
        compiler_params=_params(("arbitrary",)), name="wmod_grad")(c_rows, dmod_cols)


class _Lay:
    def __init__(self, batch, t_ctx, t_lat):
        self.b, self.t_ctx, self.t_lat = batch, t_ctx, t_lat
        self.s = t_ctx + t_lat
        self.tm = min(256, t_ctx)
        self.tpb = self.s // self.tm
        self.nct = t_ctx // self.tm
        self.ntiles = batch * self.tpb
        self.rows = batch * self.s

    def mod_row(self, i):
        return jnp.where(i % self.tpb < self.nct, 2, i // self.tpb)

    def group(self, i):
        return 2 * (i // self.tpb) + jnp.where(i % self.tpb < self.nct, 0, 1)

    def group_first(self, i):
        return (i % self.tpb == 0) | (i % self.tpb == self.nct)


def _norm_mod(x, g, shift, scale):
    r = lax.rsqrt(jnp.mean(x * x, axis=-1, keepdims=True) + EPS)
    return (x * r * g) * (1.0 + scale) + shift


def _inproj_fwd(lay, xc, mod3, gpre, w):
    tm = lay.tm

    def body(x_ref, sh_ref, sc_ref, g_ref, w_ref, p_ref, ht_ref):
        h = _norm_mod(x_ref[...], g_ref[...], sh_ref[0, 0], sc_ref[0, 0])
        ht_ref[...] = h.T.astype(BF16)
        p_ref[...] = jnp.dot(h.astype(BF16), w_ref[...], preferred_element_type=F32)

    return pl.pallas_call(
        body, grid=(lay.ntiles,),
        in_specs=[pl.BlockSpec((tm, D), lambda i: (i, 0)),
                  pl.BlockSpec((1, 1, 1, D), lambda i: (0, lay.mod_row(i), 0, 0)),
                  pl.BlockSpec((1, 1, 1, D), lambda i: (1, lay.mod_row(i), 0, 0)),
                  pl.BlockSpec((1, D), lambda i: (0, 0)),
                  pl.BlockSpec((D, W_PAD), lambda i: (0, 0))],
        out_specs=[pl.BlockSpec((tm, W_PAD), lambda i: (i, 0)), pl.BlockSpec((D, tm), lambda i: (0, i))],
        out_shape=[jax.ShapeDtypeStruct((lay.rows, W_PAD), F32), jax.ShapeDtypeStruct((D, lay.rows), BF16)],
        compiler_params=_params(("arbitrary",)), name="inproj_fwd")(xc, mod3, mod3, gpre, w)


def _inproj_bwd(lay, xc, mod3, gpre, wt, dxc, pieces):
    tm = lay.tm
    npc = len(pieces)
    offs = [off for _, off in pieces]

    def body(*refs):
        x_ref, sh_ref, sc_ref, g_ref, wt_ref, dx_in = refs[:6]
        dps = refs[6:6 + npc]
        dx_ref, dg_ref, dsh_ref, dsc_ref = refs[6 + npc:]
        i = pl.program_id(0)
        dh = None
        for dp_ref, off in zip(dps, offs):
            wd = dp_ref.shape[1]
            part = jnp.dot(dp_ref[...], wt_ref[off:off + wd, :], preferred_element_type=F32)
            dh = part if dh is None else dh + part
        _, vjp = jax.vjp(_norm_mod, x_ref[...], g_ref[...], sh_ref[0, 0], sc_ref[0, 0])
        dx, dg, dsh, dsc = vjp(dh)
        dx_ref[...] = dx_in[...] + dx
        _acc(dg_ref, dg, i == 0)
        first = lay.group_first(i)
        _acc(dsh_ref, dsh, first, at=(0,))
        _acc(dsc_ref, dsc, first, at=(0,))

    return pl.pallas_call(
        body, grid=(lay.ntiles,),
        in_specs=[pl.BlockSpec((tm, D), lambda i: (i, 0)),
                  pl.BlockSpec((1, 1, 1, D), lambda i: (0, lay.mod_row(i), 0, 0)),
                  pl.BlockSpec((1, 1, 1, D), lambda i: (1, lay.mod_row(i), 0, 0)),
                  pl.BlockSpec((1, D), lambda i: (0, 0)),
                  pl.BlockSpec((W_PAD, D), lambda i: (0, 0)),
                  pl.BlockSpec((tm, D), lambda i: (i, 0))]
        + [pl.BlockSpec((tm, dp.shape[1]), lambda i: (i, 0)) for dp, _ in pieces],
        out_specs=[pl.BlockSpec((tm, D), lambda i: (i, 0)),
                   pl.BlockSpec((1, D), lambda i: (0, 0)),
                   pl.BlockSpec((1, 1, D), lambda i: (lay.group(i), 0, 0)),
                   pl.BlockSpec((1, 1, D), lambda i: (lay.group(i), 0, 0))],
        out_shape=[jax.ShapeDtypeStruct((lay.rows, D), F32), jax.ShapeDtypeStruct((1, D), F32),
                   jax.ShapeDtypeStruct((2 * lay.b, 1, D), F32), jax.ShapeDtypeStruct((2 * lay.b, 1, D), F32)],
        compiler_params=_params(("arbitrary",)), name="inproj_bwd",
    )(xc, mod3, mod3, gpre, wt, dxc, *[dp for dp, _ in pieces])


def _weight_grad(lay, ht, dp, name):
    wd = dp.shape[1]
    tn = 512 if wd % 512 == 0 else (256 if wd % 256 == 0 else LANES)
    tr = lay.rows // 3 if lay.rows % (3 * 256) == 0 else lay.tm

    def body(ht_ref, dp_ref, o_ref):
        _acc(o_ref, jnp.dot(ht_ref[...], dp_ref[...], preferred_element_type=F32), pl.program_id(1) == 0)

    return pl.pallas_call(
        body, grid=(wd // tn, lay.rows // tr),
        in_specs=[pl.BlockSpec((D, tr), lambda j, i: (0, i)), pl.BlockSpec((tr, tn), lambda j, i: (i, j))],
        out_specs=pl.BlockSpec((D, tn), lambda j, i: (0, j)),
        out_shape=jax.ShapeDtypeStruct((D, wd), F32),
        compiler_params=_params(("arbitrary", "arbitrary")), name=name)(ht, dp)


def _outproj_post(o, x, gpost, gate):
    r = lax.rsqrt(jnp.mean(o * o, axis=-1, keepdims=True) + EPS)
    return x + gate * (o * r * gpost)


def _outproj_matmul(ys, w_ref):
    o = None
    for k, y in enumerate(ys):
        part = jnp.dot(y[...], w_ref[BRW * k:BRW * (k + 1), :], preferred_element_type=F32)
        o = part if o is None else o + part
    return o


def _outproj_specs(lay):
    tm = lay.tm
    return ([pl.BlockSpec((tm, BRW), lambda i: (i, 0))] * 4
            + [pl.BlockSpec((tm, D), lambda i: (i, 0))]
            + [pl.BlockSpec((D, D), lambda i: (0, 0))]
            + [pl.BlockSpec((1, D), lambda i: (0, 0))]
            + [pl.BlockSpec((1, 1, 1, D), lambda i: (2, lay.mod_row(i), 0, 0))])


def _outproj_fwd(lay, ys, xc, wout, gpost, mod3):
    tm = lay.tm

    def body(y0, y1, y2, y3, x_ref, w_ref, g_ref, gt_ref, o_ref, yt_ref):
        ys_ = (y0, y1, y2, y3)
        o_ref[...] = _outproj_post(_outproj_matmul(ys_, w_ref), x_ref[...], g_ref[...], gt_ref[0, 0])
        for k, y in enumerate(ys_):
            yt_ref[BRW * k:BRW * (k + 1), :] = y[...].astype(F32).T.astype(BF16)

    return pl.pallas_call(
        body, grid=(lay.ntiles,), in_specs=_outproj_specs(lay),
        out_specs=[pl.BlockSpec((tm, D), lambda i: (i, 0)), pl.BlockSpec((D, tm), lambda i: (0, i))],
        out_shape=[jax.ShapeDtypeStruct((lay.rows, D), F32), jax.ShapeDtypeStruct((D, lay.rows), BF16)],
        compiler_params=_params(("arbitrary",)), name="outproj_fwd")(*ys, xc, wout, gpost, mod3)


def _outproj_bwd(lay, ys, xc, wout, gpost, mod3, dxc):
    tm = lay.tm

    def body(y0, y1, y2, y3, x_ref, w_ref, g_ref, gt_ref, dx_ref, d0, d1, d2, d3, do_ref, dg_ref, dgt_ref):
        i = pl.program_id(0)
        o = _outproj_matmul((y0, y1, y2, y3), w_ref)
        _, vjp = jax.vjp(_outproj_post, o, x_ref[...], g_ref[...], gt_ref[0, 0])
        do, _, dg, dgt = vjp(dx_ref[...])
        do = do.astype(BF16)
        do_ref[...] = do
        for k, d in enumerate((d0, d1, d2, d3)):
            d[...] = _bdot(do, w_ref[BRW * k:BRW * (k + 1), :], 1, 1)
        _acc(dg_ref, dg, i == 0)
        _acc(dgt_ref, dgt, lay.group_first(i), at=(0,))

    return pl.pallas_call(
        body, grid=(lay.ntiles,),
        in_specs=_outproj_specs(lay) + [pl.BlockSpec((tm, D), lambda i: (i, 0))],
        out_specs=[pl.BlockSpec((tm, BRW), lambda i: (i, 0))] * 4
        + [pl.BlockSpec((tm, D), lambda i: (i, 0)), pl.BlockSpec((1, D), lambda i: (0, 0)),
           pl.BlockSpec((1, 1, D), lambda i: (lay.group(i), 0, 0))],
        out_shape=[jax.ShapeDtypeStruct((lay.rows, BRW), F32)] * 4
        + [jax.ShapeDtypeStruct((lay.rows, D), BF16), jax.ShapeDtypeStruct((1, D), F32),
           jax.ShapeDtypeStruct((2 * lay.b, 1, D), F32)],
        compiler_params=_params(("arbitrary",)), name="outproj_bwd")(*ys, xc, wout, gpost, mod3, dxc)


def _loss_kernel(lay, xc3, target):
    tm, nct = lay.tm, lay.nct

    def body(x_ref, t_ref, loss_ref, dx_ref):
        b, i = pl.program_id(0), pl.program_id(1)
        lat = i >= nct
        err = x_ref[0] - t_ref[0]
        dx_ref[0] = jnp.where(lat, err * (1.0 / D), 0.0)
        part = jnp.sum(jnp.sum(err * err, axis=1, keepdims=True), axis=0, keepdims=True) * (0.5 / D)
        part = jnp.broadcast_to(jnp.where(lat, part, 0.0), (8, LANES))
        _acc(loss_ref, part, (b == 0) & (i == 0))

    return pl.pallas_call(
        body, grid=(lay.b, lay.tpb),
        in_specs=[pl.BlockSpec((1, tm, D), lambda b, i: (b, i, 0)),
                  pl.BlockSpec((1, tm, D), lambda b, i: (b, jnp.maximum(i - nct, 0), 0))],
        out_specs=[pl.BlockSpec((8, LANES), lambda b, i: (0, 0)), pl.BlockSpec((1, tm, D), lambda b, i: (b, i, 0))],
        out_shape=[jax.ShapeDtypeStruct((8, LANES), F32), jax.ShapeDtypeStruct(xc3.shape, F32)],
        compiler_params=_params(("arbitrary", "arbitrary")), name="loss")(xc3, target)


def _chunk_orders(n_ctx, n_all):
    fwd = list(range(n_all))
    rev = list(range(n_ctx - 1, -1, -1)) + list(range(n_all - 1, n_ctx - 1, -1))
    return fwd, rev


def _ret_state_fn(k, v, cos, sin):
    kt = _rotary(k, cos, sin) * (HD ** -0.5)
    lg = _lane_by_head(LOG_GAMMA)
    j = _iota((RC, 1), 0).astype(F32)
    bd = _block_diag(BRW, BRW)
    af = mm_tn(kt * jnp.exp((RC - 1.0 - j) * lg), v) * bd
    ar = mm_tn(kt * jnp.exp(j * lg), v) * bd
    return af, ar


def _ret_out_fn(q, k, v, z, cos, sin, sf, sr, ng):
    qt = _rotary(q, cos, sin)
    kt = _rotary(k, cos, sin) * (HD ** -0.5)
    diff = (_iota((RC, RC), 0) - _iota((RC, RC), 1)).astype(F32)
    o = None
    for h in range(NH):
        m = _head_mask(h)
        sc = mm_nt(qt * m, kt)
        wgt = sc * jnp.exp(jnp.abs(diff) * LOG_GAMMA[h]) * jnp.where(diff == 0, 2.0, 1.0)
        part = mm(wgt, v * m)
        o = part if o is None else o + part
    lg = _lane_by_head(LOG_GAMMA)
    i = _iota((RC, 1), 0).astype(F32)
    o = o + mm(qt, sf) * jnp.exp((i + 1.0) * lg) + mm(qt, sr) * jnp.exp((RC - i) * lg)
    mu = _head_sum(o) * (1.0 / HD)
    cen = o - mu
    var = _head_sum(cen * cen) * (1.0 / HD)
    return cen * lax.rsqrt(var + EPS) * ng * _silu(z)


def _ret_specs(lay, cols):
    return [pl.BlockSpec((1, RC, BRW), functools.partial(lambda b, i, c: (b, i, c), c=COL_RET + c)) for c in cols]


def _ret_state(lay, p3, cos, sin):
    nc = lay.s // RC

    def body(k_ref, v_ref, c_ref, s_ref, a_ref):
        af, ar = _ret_state_fn(k_ref[0], v_ref[0], c_ref[...], s_ref[...])
        a_ref[0, 0, 0] = af
        a_ref[0, 0, 1] = ar

    tab = pl.BlockSpec((RC, BRW), lambda b, i: (i, 0))
    return pl.pallas_call(
        body, grid=(lay.b, nc), in_specs=_ret_specs(lay, (1, 2)) + [tab, tab],
        out_specs=pl.BlockSpec((1, 1, 2, BRW, BRW), lambda b, i: (b, i, 0, 0, 0)),
        out_shape=jax.ShapeDtypeStruct((lay.b, nc, 2, BRW, BRW), F32),
        compiler_params=_params(("arbitrary", "arbitrary")), name="ret_state")(p3, p3, cos, sin)


def _ret_state_bwd(lay, p3, cos, sin, d_a, dpr):
    nc = lay.s // RC

    def body(k_ref, v_ref, c_ref, s_ref, da_ref, dpr_ref, o_ref):
        _, vjp = jax.vjp(lambda k, v: _ret_state_fn(k, v, c_ref[...], s_ref[...]), k_ref[0], v_ref[0])
        dk, dv = vjp((da_ref[0, 0, 0], da_ref[0, 0, 1]))
        o_ref[0, :, 0:BRW] = dpr_ref[0, :, 0:BRW].astype(BF16)
        o_ref[0, :, BRW:2 * BRW] = (dpr_ref[0, :, BRW:2 * BRW] + dk).astype(BF16)
        o_ref[0, :, 2 * BRW:3 * BRW] = (dpr_ref[0, :, 2 * BRW:3 * BRW] + dv).astype(BF16)
        o_ref[0, :, 3 * BRW:] = dpr_ref[0, :, 3 * BRW:].astype(BF16)

    tab = pl.BlockSpec((RC, BRW), lambda b, i: (i, 0))
    return pl.pallas_call(
        body, grid=(lay.b, nc),
        in_specs=_ret_specs(lay, (1, 2)) + [tab, tab,
                                            pl.BlockSpec((1, 1, 2, BRW, BRW), lambda b, i: (b, i, 0, 0, 0)),
                                            pl.BlockSpec((1, RC, 4 * BRW), lambda b, i: (b, i, 0))],
        out_specs=pl.BlockSpec((1, RC, 4 * BRW), lambda b, i: (b, i, 0)),
        out_shape=jax.ShapeDtypeStruct((lay.b, lay.s, 4 * BRW), BF16),
        compiler_params=_params(("arbitrary", "arbitrary")), name="ret_state_bwd")(p3, p3, cos, sin, d_a, dpr)


def _state_scan(lay, a, nc_ctx, transpose, name):
    b, nc = a.shape[0], a.shape[1]
    orders = _chunk_orders(nc_ctx, nc)

    def body(a_ref, o_ref):
        d, jh = pl.program_id(1), pl.program_id(2)
        head = (_iota((1, LANES), 1) + jh * LANES) // HD
        lg = jnp.full((1, LANES), LOG_GAMMA[NH - 1], F32)
        for h in range(NH - 2, -1, -1):
            lg = jnp.where(head == h, LOG_GAMMA[h], lg)
        dec = jnp.exp(RC * lg)
        for dd in (0, 1):
            @pl.when(d == dd)
            def _(order=orders[dd]):
                acc = jnp.zeros((BRW, LANES), F32)
                if not transpose:
                    for c in order:
                        o_ref[0, c, 0] = acc
                        acc = acc * dec + a_ref[0, c, 0]
                else:
                    for c in reversed(order):
                        o_ref[0, c, 0] = acc
                        acc = a_ref[0, c, 0] + acc * dec

    spec = pl.BlockSpec((1, nc, 1, BRW, LANES), lambda bb, d, jh: (bb, 0, d, 0, jh))
    return pl.pallas_call(
        body, grid=(b, 2, BRW // LANES), in_specs=[spec], out_specs=spec,
        out_shape=jax.ShapeDtypeStruct(a.shape, F32),
        compiler_params=_params(("arbitrary",) * 3), name=name)(a)


def _ret_out(lay, p3, cos, sin, states, ng):
    nc = lay.s // RC

    def body(q_ref, k_ref, v_ref, z_ref, c_ref, s_ref, st_ref, ng_ref, y_ref):
        y = _ret_out_fn(q_ref[0], k_ref[0], v_ref[0], z_ref[0], c_ref[...], s_ref[...],
                        st_ref[0, 0, 0], st_ref[0, 0, 1], ng_ref[...])
        y_ref[0] = y.astype(BF16)

    tab = pl.BlockSpec((RC, BRW), lambda b, i: (i, 0))
    return pl.pallas_call(
        body, grid=(lay.b, nc),
        in_specs=_ret_specs(lay, (0, 1, 2, 3)) + [tab, tab,
                                                  pl.BlockSpec((1, 1, 2, BRW, BRW), lambda b, i: (b, i, 0, 0, 0)),
                                                  pl.BlockSpec((1, BRW), lambda b, i: (0, 0))],
        out_specs=pl.BlockSpec((1, RC, BRW), lambda b, i: (b, i, 0)),
        out_shape=jax.ShapeDtypeStruct((lay.b, lay.s, BRW), BF16),
        compiler_params=_params(("arbitrary", "arbitrary")), name="ret_out")(p3, p3, p3, p3, cos, sin, states, ng)


def _ret_out_bwd(lay, p3, cos, sin, states, ng, dy):
    nc = lay.s // RC

    def body(q_ref, k_ref, v_ref, z_ref, c_ref, s_ref, st_ref, ng_ref, dy_ref, dp_ref, dst_ref, dng_ref):
        b, i = pl.program_id(0), pl.program_id(1)
        fn = lambda q, k, v, z, sf, sr, ng: _ret_out_fn(q, k, v, z, c_ref[...], s_ref[...], sf, sr, ng)
        _, vjp = jax.vjp(fn, q_ref[0], k_ref[0], v_ref[0], z_ref[0], st_ref[0, 0, 0], st_ref[0, 0, 1], ng_ref[...])
        dq, dk, dv, dz, dsf, dsr, dng = vjp(dy_ref[0])
        for n, g in enumerate((dq, dk, dv, dz)):
            dp_ref[0, :, BRW * n:BRW * (n + 1)] = g
        dst_ref[0, 0, 0] = dsf
        dst_ref[0, 0, 1] = dsr
        _acc(dng_ref, dng, (b == 0) & (i == 0))

    tab = pl.BlockSpec((RC, BRW), lambda b, i: (i, 0))
    st = pl.BlockSpec((1, 1, 2, BRW, BRW), lambda b, i: (b, i, 0, 0, 0))
    return pl.pallas_call(
        body, grid=(lay.b, nc),
        in_specs=_ret_specs(lay, (0, 1, 2, 3)) + [tab, tab, st, pl.BlockSpec((1, BRW), lambda b, i: (0, 0)),
                                                  pl.BlockSpec((1, RC, BRW), lambda b, i: (b, i, 0))],
        out_specs=[pl.BlockSpec((1, RC, 4 * BRW), lambda b, i: (b, i, 0)), st,
                   pl.BlockSpec((1, BRW), lambda b, i: (0, 0))],
        out_shape=[jax.ShapeDtypeStruct((lay.b, lay.s, 4 * BRW), F32),
                   jax.ShapeDtypeStruct(states.shape, F32), jax.ShapeDtypeStruct((1, BRW), F32)],
        compiler_params=_params(("arbitrary", "arbitrary")), name="ret_out_bwd",
    )(p3, p3, p3, p3, cos, sin, states, ng, dy)


def _sg_fn(u, v, z, w, b8):
    ug = jax.nn.gelu(u)
    vg = jax.nn.gelu(v)
    mu = jnp.mean(vg, axis=-1, keepdims=True)
    cen = vg - mu
    vn = cen * lax.rsqrt(jnp.mean(cen * cen, axis=-1, keepdims=True) + EPS)
    masks = (_iota((NH, 1, BRW), 2) // HD == _iota((NH, 1, BRW), 0)).astype(F32)
    s = jnp.sum(mm(w, vn[None] * masks), axis=0)
    expand = (_iota((8, BRW), 1) // HD == _iota((8, BRW), 0)).astype(F32)
    bias = lax.dot_general(b8, expand, (((0,), (0,)), ((), ())), precision=HI, preferred_element_type=F32)
    return ug * (s + bias) * _silu(z)


def _sg_chunks(lay):
    return 2 if (lay.s // RC) % 2 == 0 else 1


def _sg_specs(lay):
    rows = _sg_chunks(lay) * RC
    return ([pl.BlockSpec((1, rows, BRW), functools.partial(lambda b, i, c: (b, i, c), c=COL_SG + c)) for c in range(3)]
            + [pl.BlockSpec((NH, RC, RC), lambda b, i: (0, 0, 0)), pl.BlockSpec((8, RC), lambda b, i: (0, 0))])


def _sg_fwd(lay, p3, sgw, sgb8):
    per = _sg_chunks(lay)

    def body(u_ref, v_ref, z_ref, w_ref, b_ref, y_ref):
        for k in range(per):
            rows = pl.ds(RC * k, RC)
            y = _sg_fn(u_ref[0, rows, :], v_ref[0, rows, :], z_ref[0, rows, :], w_ref[...], b_ref[...])
            y_ref[0, rows, :] = y.astype(BF16)

    return pl.pallas_call(
        body, grid=(lay.b, lay.s // (per * RC)), in_specs=_sg_specs(lay),
        out_specs=pl.BlockSpec((1, per * RC, BRW), lambda b, i: (b, i, 0)),
        out_shape=jax.ShapeDtypeStruct((lay.b, lay.s, BRW), BF16),
        compiler_params=_params(("arbitrary", "arbitrary")), name="sg_fwd")(p3, p3, p3, sgw, sgb8)


def _sg_bwd(lay, p3, sgw, sgb8, dy):
    per = _sg_chunks(lay)

    def body(u_ref, v_ref, z_ref, w_ref, b_ref, dy_ref, dp_ref, dw_ref, db_ref):
        first = (pl.program_id(0) == 0) & (pl.program_id(1) == 0)
        dw = db = None
        for k in range(per):
            rows = pl.ds(RC * k, RC)
            _, vjp = jax.vjp(_sg_fn, u_ref[0, rows, :], v_ref[0, rows, :], z_ref[0, rows, :], w_ref[...], b_ref[...])
            g = vjp(dy_ref[0, rows, :])
            for n in range(3):
                dp_ref[0, rows, BRW * n:BRW * (n + 1)] = g[n].astype(BF16)
            dw = g[3] if dw is None else dw + g[3]
            db = g[4] if db is None else db + g[4]
        _acc(dw_ref, dw, first)
        _acc(db_ref, db, first)

    return pl.pallas_call(
        body, grid=(lay.b, lay.s // (per * RC)),
        in_specs=_sg_specs(lay) + [pl.BlockSpec((1, per * RC, BRW), lambda b, i: (b, i, 0))],
        out_specs=[pl.BlockSpec((1, per * RC, 3 * BRW), lambda b, i: (b, i, 0)),
                   pl.BlockSpec((NH, RC, RC), lambda b, i: (0, 0, 0)), pl.BlockSpec((8, RC), lambda b, i: (0, 0))],
        out_shape=[jax.ShapeDtypeStruct((lay.b, lay.s, 3 * BRW), BF16),
                   jax.ShapeDtypeStruct((NH, RC, RC), F32), jax.ShapeDtypeStruct((8, RC), F32)],
        compiler_params=_params(("arbitrary", "arbitrary")), name="sg_bwd")(p3, p3, p3, sgw, sgb8, dy)


def _sc_specs(lay):
    first = COL_SC * BRW // LANES
    blk = [pl.BlockSpec((1, lay.s, LANES), functools.partial(lambda j, b, c: (b, 0, c + j), c=first + 2 * n))
           for n in range(4)]
    return blk + [pl.BlockSpec((8, LANES), lambda j, b: (0, j))]


def _sc_fwd(lay, p3, w8):
    dn, up = _make_shifts(lay.s, lay.t_ctx)

    def fn(b_, c_, h_, z_, w0, w1, w2):
        return b_ * _conv3(c_ * h_, w0, w1, w2, dn, up) * _silu(z_)

    def body(b_ref, c_ref, h_ref, z_ref, w_ref, y_ref):
        y = fn(b_ref[0], c_ref[0], h_ref[0], z_ref[0], w_ref[0:1, :], w_ref[1:2, :], w_ref[2:3, :])
        y_ref[0] = y.astype(BF16)

    return pl.pallas_call(
        body, grid=(BRW // LANES, lay.b), in_specs=_sc_specs(lay),
        out_specs=pl.BlockSpec((1, lay.s, LANES), lambda j, b: (b, 0, j)),
        out_shape=jax.ShapeDtypeStruct((lay.b, lay.s, BRW), BF16),
        compiler_params=_params(("arbitrary", "arbitrary")), name="sc_fwd")(p3, p3, p3, p3, w8)


def _sc_bwd(lay, p3, w8, dy):
    dn, up = _make_shifts(lay.s, lay.t_ctx)

    def fn(b_, c_, h_, z_, w0, w1, w2):
        return b_ * _conv3(c_ * h_, w0, w1, w2, dn, up) * _silu(z_)

    def body(b_ref, c_ref, h_ref, z_ref, w_ref, dy_ref, db_ref, dc_ref, dh_ref, dz_ref, dw_ref):
        _, vjp = jax.vjp(fn, b_ref[0], c_ref[0], h_ref[0], z_ref[0], w_ref[0:1, :], w_ref[1:2, :], w_ref[2:3, :])
        g = vjp(dy_ref[0])
        for ref, val in zip((db_ref, dc_ref, dh_ref, dz_ref), g[:4]):
            ref[0] = val.astype(BF16)
        dw = jnp.concatenate([g[4], g[5], g[6], jnp.zeros((5, LANES), F32)], axis=0)
        _acc(dw_ref, dw, pl.program_id(1) == 0)

    out = pl.BlockSpec((1, lay.s, LANES), lambda j, b: (b, 0, j))
    return pl.pallas_call(
        body, grid=(BRW // LANES, lay.b), in_specs=_sc_specs(lay) + [out],
        out_specs=[out] * 4 + [pl.BlockSpec((8, LANES), lambda j, b: (0, j))],
        out_shape=[jax.ShapeDtypeStruct((lay.b, lay.s, BRW), BF16)] * 4 + [jax.ShapeDtypeStruct((8, BRW), F32)],
        compiler_params=_params(("arbitrary", "arbitrary")), name="sc_bwd")(p3, p3, p3, p3, w8, dy)


def _gdn_conv_fn(x, w0, w1, w2, normed, dn, up):
    a = _silu(_conv3(x, w0, w1, w2, dn, up))
    nrm = a * lax.rsqrt(_head_sum(a * a) + EPS)
    return jnp.where(normed, nrm, a)


def _gdn_conv(lay, p3, w8):
    dn, up = _make_shifts(lay.s, lay.t_ctx)
    first = COL_GDN * BRW // LANES

    def body(x_ref, w_ref, o_ref):
        normed = pl.program_id(0) < 2 * BRW // LANES
        o_ref[0] = _gdn_conv_fn(x_ref[0], w_ref[0:1, :], w_ref[1:2, :], w_ref[2:3, :], normed, dn, up)

    return pl.pallas_call(
        body, grid=(3 * BRW // LANES, lay.b),
        in_specs=[pl.BlockSpec((1, lay.s, LANES), lambda j, b: (b, 0, first + j)),
                  pl.BlockSpec((8, LANES), lambda j, b: (0, j))],
        out_specs=pl.BlockSpec((1, lay.s, LANES), lambda j, b: (b, 0, j)),
        out_shape=jax.ShapeDtypeStruct((lay.b, lay.s, 3 * BRW), F32),
        compiler_params=_params(("arbitrary", "arbitrary")), name="gdn_conv")(p3, w8)


def _gdn_conv_bwd(lay, p3, w8, dqkv):
    dn, up = _make_shifts(lay.s, lay.t_ctx)
    first = COL_GDN * BRW // LANES

    def body(x_ref, w_ref, g_ref, dx_ref, dw_ref):
        normed = pl.program_id(0) < 2 * BRW // LANES
        fn = lambda x, w0, w1, w2: _gdn_conv_fn(x, w0, w1, w2, normed, dn, up)
        _, vjp = jax.vjp(fn, x_ref[0], w_ref[0:1, :], w_ref[1:2, :], w_ref[2:3, :])
        g = vjp(g_ref[0])
        dx_ref[0] = g[0].astype(BF16)
        dw = jnp.concatenate([g[1], g[2], g[3], jnp.zeros((5, LANES), F32)], axis=0)
        _acc(dw_ref, dw, pl.program_id(1) == 0)

    blk = pl.BlockSpec((1, lay.s, LANES), lambda j, b: (b, 0, j))
    return pl.pallas_call(
        body, grid=(3 * BRW // LANES, lay.b),
        in_specs=[pl.BlockSpec((1, lay.s, LANES), lambda j, b: (b, 0, first + j)),
                  pl.BlockSpec((8, LANES), lambda j, b: (0, j)), blk],
        out_specs=[blk, pl.BlockSpec((8, LANES), lambda j, b: (0, j))],
        out_shape=[jax.ShapeDtypeStruct((lay.b, lay.s, 3 * BRW), BF16), jax.ShapeDtypeStruct((8, 3 * BRW), F32)],
        compiler_params=_params(("arbitrary", "arbitrary")), name="gdn_conv_bwd")(p3, w8, dqkv)


def _tri_inverse(low):
    i, j = _iota(low.shape, low.ndim - 2), _iota(low.shape, low.ndim - 1) % GC
    t = (i == j).astype(F32)
    s = 1
    while s < GC:
        pair = (i // (2 * s)) == (j // (2 * s))
        off = pair & (((i // s) % 2) != ((j // s) % 2))
        cb = jnp.where(off, low, 0.0)
        t = t - (cb if s == 1 else _bdot(t, _stack_heads(_bdot(cb, _stack_heads(t), 1, 0)), 1, 0))
        s *= 2
    return t


@jax.custom_vjp
def _tri_solve(t, low, r1, r2):
    del low
    return _bdot(t, _stack_heads(r1), 1, 0), _bdot(t, _stack_heads(r2), 1, 0)


def _tri_solve_fwd(t, low, r1, r2):
    del low
    x1, x2 = _bdot(t, _stack_heads(r1), 1, 0), _bdot(t, _stack_heads(r2), 1, 0)
    return (x1, x2), (t, x1, x2)


def _tri_solve_bwd(res, g):
    t, x1, x2 = res
    bd = _block_diag(BRW, BRW)
    d1 = _unstack_heads(_bdot(t, g[0], 0, 0) * bd)
    d2 = _unstack_heads(_bdot(t, g[1], 0, 0) * bd)
    dlow = -(_bdot(d1, _stack_heads(x1), 1, 1) + _bdot(d2, _stack_heads(x2), 1, 1))
    return jnp.zeros_like(t), dlow, d1, d2


_tri_solve.defvjp(_tri_solve_fwd, _tri_solve_bwd)

N_PACK = 5


def _gdn_prep_fn(qn, kn, vv, a, alog, dtb, t=None):
    n = qn.shape[0]
    col = _iota((1, 1, LANES), 2)
    xx = a + dtb
    softplus = jnp.maximum(xx, 0.0) + jnp.log(1.0 + jnp.exp(-jnp.abs(xx)))
    g_small = jnp.where(col < 8, -jnp.exp(alog) * softplus, 0.0).reshape(n * GC, LANES)
    beta_small = jax.nn.sigmoid(a).reshape(n * GC, LANES)
    sel_col, sel_head = _iota((LANES, BRW), 0), _iota((LANES, BRW), 1) // HD
    g_l, b_l = [], []
    for d in (0, 1):
        g_l.append(_dotf(g_small, (sel_col == 4 * d + sel_head).astype(F32)))
        b_l.append(_dotf(beta_small, (sel_col == 8 + 4 * d + sel_head).astype(F32)))
    g_l = jnp.concatenate(g_l, axis=0).reshape(2 * n, GC, BRW)
    b_l = jnp.concatenate(b_l, axis=0).reshape(2 * n, GC, BRW)
    rev = _iota((2 * n, 1, 1), 0) >= n
    fwd = jnp.logical_not(rev)
    ri, ci = _iota((1, GC, GC), 1), _iota((1, GC, GC), 2)
    tri = ((fwd & (ri >= ci)) | (rev & (ri <= ci))).astype(F32)
    gc_l = lax.dot_general(tri, g_l, (((2,), (1,)), ((0,), (0,))), precision=HI,
                           preferred_element_type=F32)
    gtot_l = jnp.sum(g_l, axis=1, keepdims=True)
    i, j = _iota((1, GC, BRW), 1), _iota((1, GC, BRW), 2) % GC
    gc_t = jnp.sum(jnp.where(i == j, gc_l, 0.0), axis=1, keepdims=True)
    incl = (fwd & (i >= j)) | (rev & (i <= j))
    strict = (fwd & (i > j)) | (rev & (i < j))
    decay = jnp.where(incl, jnp.exp(jnp.where(incl, gc_l - gc_t, 0.0)), 0.0)
    kn2 = jnp.concatenate([kn, kn], axis=0)
    vv2 = jnp.concatenate([vv, vv], axis=0)
    qs = jnp.concatenate([qn, qn], axis=0) * (HD ** -0.5)
    kst = _stack_heads(kn2)
    kb = kn2 * b_l
    low = jnp.where(strict, mm_nt(kb, kst) * decay, 0.0)
    eg = jnp.exp(gc_l)
    t_inv = _tri_inverse(low) if t is None else t
    u, w = _tri_solve(t_inv, low, vv2 * b_l, kb * eg)
    k_tail = kn2 * jnp.exp(gtot_l - gc_l)
    intra = mm_nt(qs, kst) * decay
    return (u, w, k_tail, qs * eg, intra), jnp.exp(gtot_l), t_inv


def _prep_chunks(lay):
    return 4 if (lay.s // GC) % 4 == 0 else 2


def _gdn_prep_specs(lay):
    rows = _prep_chunks(lay) * GC
    return ([pl.BlockSpec((1, rows, BRW), functools.partial(lambda b, i, c: (b, i, c), c=c)) for c in range(3)]
            + [pl.BlockSpec((1, rows, LANES), lambda b, i: (b, i, COL_A128)),
               pl.BlockSpec((8, LANES), lambda b, i: (0, 0))])


def _gdn_prep(lay, qkv, p3, prm):
    nc, per = lay.s // GC, _prep_chunks(lay)

    def body(q_ref, k_ref, v_ref, a_ref, prm_ref, pack_ref, cd_ref, t_ref):
        chunks = lambda ref: ref[0].reshape(per, GC, ref.shape[-1])
        pack, cd, t_inv = _gdn_prep_fn(chunks(q_ref), chunks(k_ref), chunks(v_ref), chunks(a_ref),
                                       prm_ref[0:1, :], prm_ref[1:2, :])
        for d in (0, 1):
            for n in range(N_PACK):
                pack_ref[0, :, d, n] = pack[n][per * d:per * (d + 1)]
            cd_ref[0, :, d] = cd[per * d:per * (d + 1)]
            t_ref[0, :, d] = t_inv[per * d:per * (d + 1)]

    return pl.pallas_call(
        body, grid=(lay.b, nc // per), in_specs=_gdn_prep_specs(lay),
        out_specs=[pl.BlockSpec((1, per, 2, N_PACK, GC, BRW), lambda b, i: (b, i, 0, 0, 0, 0)),
                   pl.BlockSpec((1, per, 2, 1, BRW), lambda b, i: (b, i, 0, 0, 0)),
                   pl.BlockSpec((1, per, 2, GC, BRW), lambda b, i: (b, i, 0, 0, 0))],
        out_shape=[jax.ShapeDtypeStruct((lay.b, nc, 2, N_PACK, GC, BRW), F32),
                   jax.ShapeDtypeStruct((lay.b, nc, 2, 1, BRW), F32),
                   jax.ShapeDtypeStruct((lay.b, nc, 2, GC, BRW), F32)],
        compiler_params=_params(("arbitrary", "arbitrary")), name="gdn_prep")(qkv, qkv, qkv, p3, prm)


def _gdn_prep_bwd(lay, qkv, p3, prm, dpacks, dcds, t_inv):
    nc, per = lay.s // GC, _prep_chunks(lay)

    def body(q_ref, k_ref, v_ref, a_ref, prm_ref, dpf_ref, dpr_ref, dcf_ref, dcr_ref, t_ref, dqkv_ref, da_ref,
             dprm_ref):
        first = (pl.program_id(0) == 0) & (pl.program_id(1) == 0)
        chunks = lambda ref: ref[0].reshape(per, GC, ref.shape[-1])
        t_inv = jnp.concatenate([t_ref[0, :, 0], t_ref[0, :, 1]], axis=0)
        fn = lambda q, k, v, a, alog, dtb: _gdn_prep_fn(q, k, v, a, alog, dtb, t_inv)[:2]
        _, vjp = jax.vjp(fn, chunks(q_ref), chunks(k_ref), chunks(v_ref), chunks(a_ref),
                         prm_ref[0:1, :], prm_ref[1:2, :])
        dpack = tuple(jnp.concatenate([dpf_ref[0, :, n], dpr_ref[0, :, n]], axis=0) for n in range(N_PACK))
        dq, dk, dv, da, dalog, ddtb = vjp((dpack, jnp.concatenate([dcf_ref[0], dcr_ref[0]], axis=0)))
        dqkv_ref[0, :, 0:BRW] = dq.reshape(per * GC, BRW)
        dqkv_ref[0, :, BRW:2 * BRW] = dk.reshape(per * GC, BRW)
        dqkv_ref[0, :, 2 * BRW:] = dv.reshape(per * GC, BRW)
        da_ref[0] = da.reshape(per * GC, LANES).astype(BF16)
        _acc(dprm_ref, jnp.concatenate([dalog, ddtb, jnp.zeros((6, LANES), F32)], axis=0), first)

    rows_blk = per * GC
    return pl.pallas_call(
        body, grid=(lay.b, nc // per),
        in_specs=_gdn_prep_specs(lay)
        + [pl.BlockSpec((1, per, N_PACK, GC, BRW), lambda b, i: (b, i, 0, 0, 0))] * 2
        + [pl.BlockSpec((1, per, 1, BRW), lambda b, i: (b, i, 0, 0))] * 2
        + [pl.BlockSpec((1, per, 2, GC, BRW), lambda b, i: (b, i, 0, 0, 0))],
        out_specs=[pl.BlockSpec((1, rows_blk, 3 * BRW), lambda b, i: (b, i, 0)),
                   pl.BlockSpec((1, rows_blk, LANES), lambda b, i: (b, i, 0)),
                   pl.BlockSpec((8, LANES), lambda b, i: (0, 0))],
        out_shape=[jax.ShapeDtypeStruct((lay.b, lay.s, 3 * BRW), F32),
                   jax.ShapeDtypeStruct((lay.b, lay.s, LANES), BF16), jax.ShapeDtypeStruct((8, LANES), F32)],
        compiler_params=_params(("arbitrary", "arbitrary")), name="gdn_prep_bwd",
    )(qkv, qkv, qkv, p3, prm, *dpacks, *dcds, t_inv)


def _gdn_step_fn(s, u, w, k_tail, qd, intra, cdec):
    v_new = u - mm(w, s)
    o = mm(qd, s) + mm(intra, _stack_heads(v_new))
    return s * cdec + mm_tn(k_tail, v_new) * _block_diag(BRW, BRW), o


def _order_index(nc_ctx, nc, d, step):
    rev = jnp.where(step < nc_ctx, nc_ctx - 1 - step, nc + nc_ctx - 1 - step)
    return jnp.where(d == 0, step, rev)


def _gdn_scan(lay, pack, cd):
    nc, nc_ctx = lay.s // GC, lay.t_ctx // GC
    chunk = functools.partial(_order_index, nc_ctx, nc)

    def body(pf_ref, pr_ref, cf_ref, cr_ref, of_ref, or_ref, sf_ref, sr_ref, s_scr):
        @pl.when(pl.program_id(0) == 0)
        def _():
            s_scr[...] = jnp.zeros_like(s_scr)

        nb = lay.b
        s = s_scr[...]
        st = _unstack_heads(s)
        sf_ref[:, 0] = st[:nb]
        sr_ref[:, 0] = st[nb:]
        args = [jnp.concatenate([pf_ref[:, 0, 0, n], pr_ref[:, 0, 0, n]], axis=0) for n in range(N_PACK)]
        s_new, o = _gdn_step_fn(s, *args, jnp.concatenate([cf_ref[:, 0, 0], cr_ref[:, 0, 0]], axis=0))
        of_ref[:, 0] = o[:nb]
        or_ref[:, 0] = o[nb:]
        s_scr[...] = s_new

    def pk(d):
        return pl.BlockSpec((lay.b, 1, 1, N_PACK, GC, BRW), lambda t: (0, chunk(d, t), d, 0, 0, 0))

    def cdb(d):
        return pl.BlockSpec((lay.b, 1, 1, 1, BRW), lambda t: (0, chunk(d, t), d, 0, 0))

    def out(d):
        return pl.BlockSpec((lay.b, 1, GC, BRW), lambda t: (0, chunk(d, t), 0, 0))

    return pl.pallas_call(
        body, grid=(nc,), in_specs=[pk(0), pk(1), cdb(0), cdb(1)],
        out_specs=[out(0), out(1), out(0), out(1)],
        out_shape=[jax.ShapeDtypeStruct((lay.b, nc, GC, BRW), F32)] * 4,
        scratch_shapes=[pltpu.VMEM((2 * lay.b, BRW, BRW), F32)],
        compiler_params=_params(("arbitrary",)), name="gdn_scan")(pack, pack, cd, cd)


def _gdn_scan_bwd(lay, pack, cd, states, do):
    nc, nc_ctx = lay.s // GC, lay.t_ctx // GC

    def chunk(d, t):
        return _order_index(nc_ctx, nc, d, nc - 1 - t)

    def body(pf_ref, pr_ref, cf_ref, cr_ref, sf_ref, sr_ref, dof_ref, dor_ref, dpf_ref, dpr_ref, dcf_ref, dcr_ref,
             ds_scr):
        @pl.when(pl.program_id(0) == 0)
        def _():
            ds_scr[...] = jnp.zeros_like(ds_scr)

        nb = lay.b
        both = lambda f, r: jnp.concatenate([f, r], axis=0)
        args = ([_stack_heads(both(sf_ref[:, 0], sr_ref[:, 0]))]
                + [both(pf_ref[:, 0, 0, n], pr_ref[:, 0, 0, n]) for n in range(N_PACK)]
                + [both(cf_ref[:, 0, 0], cr_ref[:, 0, 0])])
        _, vjp = jax.vjp(_gdn_step_fn, *args)
        g = vjp((ds_scr[...], both(dof_ref[...], dor_ref[...])))
        ds_scr[...] = g[0]
        for n in range(N_PACK):
            dpf_ref[:, 0, n] = g[1 + n][:nb]
            dpr_ref[:, 0, n] = g[1 + n][nb:]
        dcf_ref[:, 0] = g[1 + N_PACK][:nb]
        dcr_ref[:, 0] = g[1 + N_PACK][nb:]

    def pk(d):
        return pl.BlockSpec((lay.b, 1, 1, N_PACK, GC, BRW), lambda t: (0, chunk(d, t), d, 0, 0, 0))

    def cdb(d):
        return pl.BlockSpec((lay.b, 1, 1, 1, BRW), lambda t: (0, chunk(d, t), d, 0, 0))

    def st(d):
        return pl.BlockSpec((lay.b, 1, GC, BRW), lambda t: (0, chunk(d, t), 0, 0))

    def dob(d):
        return pl.BlockSpec((lay.b, GC, BRW), lambda t: (0, chunk(d, t), 0))

    def dpk(d):
        return pl.BlockSpec((lay.b, 1, N_PACK, GC, BRW), lambda t: (0, chunk(d, t), 0, 0, 0))

    def dcb(d):
        return pl.BlockSpec((lay.b, 1, 1, BRW), lambda t: (0, chunk(d, t), 0, 0))

    return pl.pallas_call(
        body, grid=(nc,),
        in_specs=[pk(0), pk(1), cdb(0), cdb(1), st(0), st(1), dob(0), dob(1)],
        out_specs=[dpk(0), dpk(1), dcb(0), dcb(1)],
        out_shape=[jax.ShapeDtypeStruct((lay.b, nc, N_PACK, GC, BRW), F32)] * 2
        + [jax.ShapeDtypeStruct((lay.b, nc, 1, BRW), F32)] * 2,
        scratch_shapes=[pltpu.VMEM((2 * lay.b, BRW, BRW), F32)],
        compiler_params=_params(("arbitrary",)), name="gdn_scan_bwd")(pack, pack, cd, cd, *states, do, do)


def _gdn_finish_fn(o, z, ng):
    return o * lax.rsqrt(_head_sum(o * o) * (1.0 / HD) + EPS) * ng * _silu(z)


def _finish_chunks(lay):
    nc = lay.s // GC
    return 12 if nc % 12 == 0 else (6 if nc % 6 == 0 else 2)


def _gdn_o(of_ref, or_ref):
    return (of_ref[0] + or_ref[0]).reshape(of_ref.shape[1] * GC, BRW)


def _gdn_finish_specs(lay):
    per = _finish_chunks(lay)
    ob = pl.BlockSpec((1, per, GC, BRW), lambda b, i: (b, i, 0, 0))
    return [ob, ob, pl.BlockSpec((1, per * GC, BRW), lambda b, i: (b, i, COL_GDN + 3)),
            pl.BlockSpec((1, BRW), lambda b, i: (0, 0))]


def _gdn_finish(lay, o_f, o_r, p3, ng):
    rows = _finish_chunks(lay) * GC

    def body(of_ref, or_ref, z_ref, ng_ref, y_ref):
        y_ref[0] = _gdn_finish_fn(_gdn_o(of_ref, or_ref), z_ref[0], ng_ref[...]).astype(BF16)

    return pl.pallas_call(
        body, grid=(lay.b, lay.s // rows), in_specs=_gdn_finish_specs(lay),
        out_specs=pl.BlockSpec((1, rows, BRW), lambda b, i: (b, i, 0)),
        out_shape=jax.ShapeDtypeStruct((lay.b, lay.s, BRW), BF16),
        compiler_params=_params(("arbitrary", "arbitrary")), name="gdn_finish")(o_f, o_r, p3, ng)


def _gdn_finish_bwd(lay, o_f, o_r, p3, ng, dy):
    rows = _finish_chunks(lay) * GC

    def body(of_ref, or_ref, z_ref, ng_ref, dy_ref, do_ref, dz_ref, dng_ref):
        first = (pl.program_id(0) == 0) & (pl.program_id(1) == 0)
        _, vjp = jax.vjp(_gdn_finish_fn, _gdn_o(of_ref, or_ref), z_ref[0], ng_ref[...])
        do, dz, dng = vjp(dy_ref[0])
        do_ref[0] = do
        dz_ref[0] = dz.astype(BF16)
        _acc(dng_ref, dng, first)

    blk = pl.BlockSpec((1, rows, BRW), lambda b, i: (b, i, 0))
    return pl.pallas_call(
        body, grid=(lay.b, lay.s // rows), in_specs=_gdn_finish_specs(lay) + [blk],
        out_specs=[blk, blk, pl.BlockSpec((1, BRW), lambda b, i: (0, 0))],
        out_shape=[jax.ShapeDtypeStruct((lay.b, lay.s, BRW), F32), jax.ShapeDtypeStruct((lay.b, lay.s, BRW), BF16),
                   jax.ShapeDtypeStruct((1, BRW), F32)],
        compiler_params=_params(("arbitrary", "arbitrary")), name="gdn_finish_bwd")(o_f, o_r, p3, ng, dy)


def _rope_tables(lay):
    t = jnp.arange(lay.t_lat)
    lane = np.arange(BRW)
    dim = lane % HD
    inv = jnp.asarray(ROPE_BASE ** (-(dim % 16).astype(np.float32) / 16.0), F32)
    pos = jnp.where((dim // 32 == 0)[None, :], (t // GRID_W)[:, None], (t % GRID_W)[:, None]).astype(F32)
    ang = pos * inv[None, :]
    cos = jnp.concatenate([jnp.ones((lay.t_ctx, BRW), F32), jnp.cos(ang)], axis=0)
    sin = jnp.concatenate([jnp.zeros((lay.t_ctx, BRW), F32), jnp.sin(ang)], axis=0)
    return cos, sin


def _pad_rows(a, rows):
    return jnp.concatenate([a, jnp.zeros((rows - a.shape[0],) + a.shape[1:], a.dtype)], axis=0)


def _layer_fwd(lay, xc, wl, cos, sin):
    p, ht = _inproj_fwd(lay, xc, wl["mod3"], wl["gpre"], wl["win"])
    p3 = p.reshape(lay.b, lay.s, W_PAD)
    nctx = lay.t_ctx // RC
    states = _state_scan(lay, _ret_state(lay, p3, cos, sin), nctx, False, "ret_scan")
    y_ret = _ret_out(lay, p3, cos, sin, states, wl["ret_ng"])
    y_sg = _sg_fwd(lay, p3, wl["sgw"], wl["sgb8"])
    y_sc = _sc_fwd(lay, p3, wl["scw8"])
    qkv = _gdn_conv(lay, p3, wl["gdnw8"])
    pack, cd, t_inv = _gdn_prep(lay, qkv, p3, wl["prm"])
    o_f, o_r, st_f, st_r = _gdn_scan(lay, pack, cd)
    y_gdn = _gdn_finish(lay, o_f, o_r, p3, wl["gdn_ng"])
    ys = [y.reshape(lay.rows, BRW) for y in (y_ret, y_sg, y_sc, y_gdn)]
    xc_new, yt = _outproj_fwd(lay, ys, xc, wl["wout"], wl["gpost"], wl["mod3"])
    saved = dict(xc=xc, p3=p3, ht=ht, yt=yt, states=states, qkv=qkv, pack=pack, cd=cd, t_inv=t_inv, o_f=o_f, o_r=o_r, gstates=(st_f, st_r),
                 ys=ys)
    return xc_new, saved


def _layer_bwd(lay, sv, wl, cos, sin, dxc):
    p3 = sv["p3"]
    as3 = lambda a: a.reshape(lay.b, lay.s, a.shape[-1])
    as2 = lambda a: a.reshape(lay.rows, a.shape[-1])
    dy_ret, dy_sg, dy_sc, dy_gdn, do_, dgpost, dgate = _outproj_bwd(
        lay, sv["ys"], sv["xc"], wl["wout"], wl["gpost"], wl["mod3"], dxc)
    dwout = _weight_grad(lay, sv["yt"], do_, "wout_grad")
    nctx = lay.t_ctx // RC
    dpr, dstates, dret_ng = _ret_out_bwd(lay, p3, cos, sin, sv["states"], wl["ret_ng"], as3(dy_ret))
    d_a = _state_scan(lay, dstates, nctx, True, "ret_scan_bwd")
    dp_ret = _ret_state_bwd(lay, p3, cos, sin, d_a, dpr)
    dp_sg, dsgw, dsgb8 = _sg_bwd(lay, p3, wl["sgw"], wl["sgb8"], as3(dy_sg))
    dsb, dsc_, dsh_, dsz, dscw8 = _sc_bwd(lay, p3, wl["scw8"], as3(dy_sc))
    do, dgz, dgdn_ng = _gdn_finish_bwd(lay, sv["o_f"], sv["o_r"], p3, wl["gdn_ng"], as3(dy_gdn))
    dpf, dpr_, dcf, dcr = _gdn_scan_bwd(lay, sv["pack"], sv["cd"], sv["gstates"], do)
    dqkv, da, dprm = _gdn_prep_bwd(lay, sv["qkv"], p3, wl["prm"], (dpf, dpr_), (dcf, dcr), sv["t_inv"])
    dp_gqkv, dgdnw8 = _gdn_conv_bwd(lay, p3, wl["gdnw8"], dqkv)
    pieces = [(as2(dp_ret), 0), (as2(dp_sg), COL_SG * BRW), (as2(dsb), COL_SC * BRW), (as2(dsc_), (COL_SC + 1) * BRW),
              (as2(dsh_), (COL_SC + 2) * BRW), (as2(dsz), (COL_SC + 3) * BRW), (as2(dp_gqkv), COL_GDN * BRW),
              (as2(dgz), (COL_GDN + 3) * BRW), (as2(da), COL_A128 * LANES)]
    dxc_prev, dgpre, dshift, dscale = _inproj_bwd(lay, sv["xc"], wl["mod3"], wl["gpre"], wl["wint"], dxc, pieces)
    dws = [_weight_grad(lay, sv["ht"], dp, "win_grad_%d" % off) for dp, off in pieces]
    dwin = jnp.concatenate(dws[:-1] + [dws[-1][:, :W_IN - COL_A128 * LANES]], axis=1)

    def rows3(g):
        return jnp.concatenate([g[1], g[3], g[0] + g[2]], axis=0)

    dmod = _pad_rows(jnp.concatenate([rows3(dshift), rows3(dscale), rows3(dgate)], axis=1), 8)
    grads = dict(win=dwin, wout=dwout, gpre=dgpre[0], gpost=dgpost[0], ret_ng=dret_ng[0], sgw=dsgw, sgb=dsgb8[:NH],
                 scw=dscw8[:3], gdnw=dgdnw8[:3], alog=dprm[0, :2 * NH].reshape(2, NH),
                 dtb=dprm[1, :2 * NH].reshape(2, NH), gdn_ng=dgdn_ng.reshape(NH, HD).sum(axis=0), dmod=dmod)
    return dxc_prev, grads


def _local_step(x, c, ctx, c_ctx, wmod, bmod, gpre, gpost, win, wout, ret_ng, sgw, sgb, scw, gdnw, alog, dtb,
                gdn_ng, target):
    depth = wmod.shape[0]
    lay = _Lay(x.shape[0], ctx.shape[1], x.shape[1])
    assert lay.b == 2 and lay.t_ctx % RC == 0 and lay.t_lat % RC == 0
    cos, sin = _rope_tables(lay)
    cvec8 = _pad_rows(jnp.concatenate([c, c_ctx[None]], axis=0), 8)
    mod = _mod_fwd(cvec8, wmod, bmod[:, None, :])
    wint = jnp.swapaxes(win, 1, 2)
    xc = jnp.concatenate([ctx, x], axis=1).reshape(lay.rows, D)
    layers, saved = [], []
    for l in range(depth):
        wl = dict(mod3=mod[l].reshape(8, 3, D).transpose(1, 0, 2)[:, :, None, :], gpre=gpre[l][None], gpost=gpost[l][None],
                  win=win[l], wint=wint[l], wout=wout[l], ret_ng=ret_ng[l][None], sgw=sgw[l],
                  sgb8=_pad_rows(sgb[l], 8), scw8=_pad_rows(scw[l], 8), gdnw8=_pad_rows(gdnw[l], 8),
                  prm=_pad_rows(jnp.pad(jnp.stack([alog[l].reshape(-1), dtb[l].reshape(-1)]),
                                        ((0, 0), (0, LANES - 2 * NH))), 8),
                  gdn_ng=jnp.tile(gdn_ng[l], NH)[None])
        xc, sv = _layer_fwd(lay, xc, wl, cos, sin)
        layers.append(wl)
        saved.append(sv)
    loss, dxc3 = _loss_kernel(lay, xc.reshape(lay.b, lay.s, D), target)
    dxc = dxc3.reshape(lay.rows, D)
    grads = [None] * depth
    for l in reversed(range(depth)):
        dxc, grads[l] = _layer_bwd(lay, saved[l], layers[l], cos, sin, dxc)
    stacked = {k: jnp.stack([g[k] for g in grads]) for k in grads[0]}
    dcvec8, dbmod = _mod_bwd(stacked["dmod"], wmod, cvec8)
    stacked["bmod"] = dbmod[:, 0, :]
    stacked["c_ctx"] = dcvec8[2]
    dx = dxc.reshape(lay.b, lay.s, D)[:, lay.t_ctx:, :]
    return loss, dx, stacked, cvec8


MESH = pl.DeviceIdType.MESH
ANY = pl.BlockSpec(memory_space=pl.ANY)


def _me():
    return lax.axis_index("x"), lax.axis_index("y"), lax.axis_index("c")


def _gather_weights(shards, fulls, blocks):
    n = len(shards)

    def body(*refs):
        ins, outs = refs[:n], refs[n:2 * n]
        send_sems, recv_sems, loc_sems = refs[2 * n:]
        x, y, c = _me()
        me, sibling = (x, y, c), (x, y, 1 - c)
        chips = [(1 - x, y), (x, 1 - y), (1 - x, 1 - y)]

        def blk(a, dev):
            return blocks[a](outs[a], 4 * dev[0] + 2 * dev[1] + dev[2])

        def copy(a, k, block, to, src=None):
            return pltpu.make_async_remote_copy(
                src_ref=blk(a, block) if src is None else src, dst_ref=blk(a, block), send_sem=send_sems.at[a, k],
                recv_sem=recv_sems.at[a, k], device_id=to, device_id_type=MESH)

        mine = [pltpu.make_async_copy(ins[a], blk(a, me), loc_sems.at[a]) for a in range(n)]
        for cp in mine:
            cp.start()
        first = []
        for a in range(n):
            first.append(copy(a, 0, me, sibling, src=ins[a]))
            first += [copy(a, 1 + j, me, (*chip, c), src=ins[a]) for j, chip in enumerate(chips)]
        for cp in first:
            cp.start()
        passed = []
        for j, chip in enumerate(chips):
            for a in range(n):
                copy(a, 1 + j, (*chip, c), me).wait_recv()
                fwd = copy(a, 4 + j, (*chip, c), sibling)
                fwd.start()
                passed.append(fwd)
        for a in range(n):
            copy(a, 0, sibling, me).wait_recv()
            for j, chip in enumerate(chips):
                copy(a, 4 + j, (*chip, 1 - c), me).wait_recv()
        for cp in first + passed:
            cp.wait_send()
        for cp in mine:
            cp.wait()

    return pl.pallas_call(
        body, in_specs=[ANY] * n, out_specs=[ANY] * n,
        out_shape=[jax.ShapeDtypeStruct(f, s.dtype) for f, s in zip(fulls, shards)],
        scratch_shapes=[pltpu.SemaphoreType.DMA((n, 7)), pltpu.SemaphoreType.DMA((n, 7)),
                        pltpu.SemaphoreType.DMA((n,))],
        name="gather_weights")(*shards)


def _scatter_pair(srcs, slabs, slab_shapes):
    n = len(srcs)

    def body(*refs):
        ins, outs = refs[:n], refs[n:2 * n]
        send_sems, recv_sems = refs[2 * n:]
        x, y, c = _me()
        cps = []
        for a in range(n):
            for q in range(4):
                j = 2 * q + (1 - c)
                cps.append(pltpu.make_async_remote_copy(
                    src_ref=slabs[a](ins[a], j), dst_ref=outs[a].at[q], send_sem=send_sems.at[a, q],
                    recv_sem=recv_sems.at[a, q], device_id=(x, y, 1 - c), device_id_type=MESH))
        for cp in cps:
            cp.start()
        for cp in cps:
            cp.wait_recv()
        for cp in cps:
            cp.wait_send()

    return pl.pallas_call(
        body, in_specs=[ANY] * n, out_specs=[ANY] * n,
        out_shape=[jax.ShapeDtypeStruct((4,) + tuple(shp), s.dtype) for shp, s in zip(slab_shapes, srcs)],
        scratch_shapes=[pltpu.SemaphoreType.DMA((n, 4)), pltpu.SemaphoreType.DMA((n, 4))],
        name="scatter_pair")(*srcs)


def _scatter_chips(parts, small):
    n = len(parts)

    def body(*refs):
        ins, small_ref = refs[:n], refs[n]
        outs, all_ref = refs[n + 1:2 * n + 1], refs[2 * n + 1]
        send_sems, recv_sems, g_send, g_recv, loc_sem = refs[2 * n + 2:]
        x, y, c = _me()
        me = 4 * x + 2 * y + c
        chips = [(1 - x, y), (x, 1 - y), (1 - x, 1 - y)]
        cps = []
        for a in range(n):
            for k, (px, py) in enumerate(chips):
                cps.append(pltpu.make_async_remote_copy(
                    src_ref=ins[a].at[2 * px + py], dst_ref=outs[a].at[k], send_sem=send_sems.at[a, k],
                    recv_sem=recv_sems.at[a, k], device_id=(px, py, c), device_id_type=MESH))

        def gather(k, dst_blk, peer_xyz):
            return pltpu.make_async_remote_copy(
                src_ref=small_ref, dst_ref=all_ref.at[dst_blk], send_sem=g_send.at[k], recv_sem=g_recv.at[k],
                device_id=peer_xyz, device_id_type=MESH)

        local = pltpu.make_async_copy(small_ref, all_ref.at[me], loc_sem)
        local.start()
        peers = []
        for k in range(1, N_DEV):
            px = 1 - x if k & 4 else x
            py = 1 - y if k & 2 else y
            pc = 1 - c if k & 1 else c
            peers.append((4 * px + 2 * py + pc, (px, py, pc)))
        sends = [gather(k, me, xyz) for k, (_, xyz) in enumerate(peers)]
        for cp in sends + cps:
            cp.start()
        for k, (peer, xyz) in enumerate(peers):
            gather(k, peer, xyz).wait_recv()
        for cp in cps:
            cp.wait_recv()
        for cp in sends + cps:
            cp.wait_send()
        local.wait()

    return pl.pallas_call(
        body, in_specs=[ANY] * (n + 1), out_specs=[ANY] * (n + 1),
        out_shape=[jax.ShapeDtypeStruct((3,) + p.shape[1:], p.dtype) for p in parts]
        + [jax.ShapeDtypeStruct((N_DEV,) + small.shape, small.dtype)],
        scratch_shapes=[pltpu.SemaphoreType.DMA((n, 3)), pltpu.SemaphoreType.DMA((n, 3)),
                        pltpu.SemaphoreType.DMA((N_DEV - 1,)), pltpu.SemaphoreType.DMA((N_DEV - 1,)),
                        pltpu.SemaphoreType.DMA(())],
        name="scatter_chips")(*parts, small)


def _add_rows(arrs, out_dtype, name):
    shp = arrs[0].shape
    two = [a.reshape(-1, shp[-1]) for a in arrs]
    rows, cols = two[0].shape
    tr = _row_tile(rows, 1024)

    def body(*refs):
        acc = refs[0][...].astype(F32)
        for r in refs[1:-1]:
            acc = acc + r[...].astype(F32)
        refs[-1][...] = acc.astype(out_dtype)

    blk = pl.BlockSpec((tr, cols), lambda i: (i, 0))
    return pl.pallas_call(
        body, grid=(rows // tr,), in_specs=[blk] * len(two), out_specs=blk,
        out_shape=jax.ShapeDtypeStruct((rows, cols), out_dtype),
        compiler_params=_params(("arbitrary",)), name=name)(*two).reshape(shp)


def _row_tile(rows, cap):
    best = 8
    for t in range(8, min(rows, cap) + 1, 8):
        if rows % t == 0:
            best = t
    return best


def _sum_devices(x):
    _, rows, cols = x.shape
    tr = _row_tile(rows, 2048)

    def body(x_ref, o_ref):
        acc = x_ref[0]
        for j in range(1, N_DEV):
            acc = acc + x_ref[j]
        o_ref[...] = acc

    return pl.pallas_call(
        body, grid=(rows // tr,), in_specs=[pl.BlockSpec((N_DEV, tr, cols), lambda i: (0, i, 0))],
        out_specs=pl.BlockSpec((tr, cols), lambda i: (i, 0)), out_shape=jax.ShapeDtypeStruct((rows, cols), F32),
        compiler_params=_params(("arbitrary",)), name="sum_devices")(x)


def _adamw(w, g, m, v, name):
    rows, cols = w.shape
    tr = _row_tile(rows, 512)
    bc1 = 1.0 - ADAM_B1 ** ADAM_STEP
    bc2 = 1.0 - ADAM_B2 ** ADAM_STEP

    def body(w_ref, g_ref, m_ref, v_ref, d_ref, nm_ref, nv_ref):
        g_ = g_ref[...]
        m_ = ADAM_B1 * m_ref[...] + (1.0 - ADAM_B1) * g_
        v_ = ADAM_B2 * v_ref[...] + (1.0 - ADAM_B2) * (g_ * g_)
        d_ref[...] = -ADAM_LR * ((m_ / bc1) / (jnp.sqrt(v_ / bc2) + ADAM_EPS) + ADAM_WD * w_ref[...])
        nm_ref[...] = m_
        nv_ref[...] = v_

    blk = pl.BlockSpec((tr, cols), lambda i: (i, 0))
    return pl.pallas_call(
        body, grid=(rows // tr,), in_specs=[blk] * 4, out_specs=[blk] * 3,
        out_shape=[jax.ShapeDtypeStruct((rows, cols), F32)] * 3,
        compiler_params=_params(("arbitrary",)), name=name)(w, g, m, v)


def _pack_rows(shape):
    return -(-int(np.prod(shape)) // (16 * LANES)) * 16


def _pack(arrs, dtype=F32):
    blocks = []
    for a in arrs:
        flat = a.reshape(-1).astype(dtype)
        rows = _pack_rows(a.shape)
        blocks.append(jnp.pad(flat, (0, rows * LANES - flat.shape[0])).reshape(rows, LANES))
    return jnp.concatenate(blocks, axis=0)


def _unpack(packed, shapes):
    out, off = [], 0
    for s in shapes:
        rows = _pack_rows(s)
        out.append(packed[off:off + rows].reshape(-1)[:int(np.prod(s))].reshape(s))
        off += rows
    return out


SMALL = ("c_ctx", "b_mod", "g_pre", "g_post", "ret_norm_g", "sg_w", "sg_b", "sc_conv_w", "gdn_conv_w", "gdn_a_log",
         "gdn_dt_bias", "gdn_norm_g")
ORDER = ("c_ctx", "w_mod", "b_mod", "g_pre", "g_post", "w_in", "w_out", "ret_norm_g", "sg_w", "sg_b", "sc_conv_w",
         "gdn_conv_w", "gdn_a_log", "gdn_dt_bias", "gdn_norm_g")


def kernel(x, c, ctx, c_ctx, w_mod, b_mod, g_pre, g_post, w_in, w_out, ret_norm_g, sg_w, sg_b, sc_conv_w, gdn_conv_w, gdn_a_log, gdn_dt_bias, gdn_norm_g, loss_target, m_c_ctx, m_w_mod, m_b_mod, m_g_pre, m_g_post, m_w_in, m_w_out, m_ret_norm_g, m_sg_w, m_sg_b, m_sc_conv_w, m_gdn_conv_w, m_gdn_a_log, m_gdn_dt_bias, m_gdn_norm_g, v_c_ctx, v_w_mod, v_b_mod, v_g_pre, v_g_post, v_w_in, v_w_out, v_ret_norm_g, v_sg_w, v_sg_b, v_sc_conv_w, v_gdn_conv_w, v_gdn_a_log, v_gdn_dt_bias, v_gdn_norm_g):
    wts = dict(c_ctx=c_ctx, w_mod=w_mod, b_mod=b_mod, g_pre=g_pre, g_post=g_post, w_in=w_in, w_out=w_out,
               ret_norm_g=ret_norm_g, sg_w=sg_w, sg_b=sg_b, sc_conv_w=sc_conv_w, gdn_conv_w=gdn_conv_w,
               gdn_a_log=gdn_a_log, gdn_dt_bias=gdn_dt_bias, gdn_norm_g=gdn_norm_g)
    mom = dict(c_ctx=m_c_ctx, w_mod=m_w_mod, b_mod=m_b_mod, g_pre=m_g_pre, g_post=m_g_post, w_in=m_w_in, w_out=m_w_out,
               ret_norm_g=m_ret_norm_g, sg_w=m_sg_w, sg_b=m_sg_b, sc_conv_w=m_sc_conv_w, gdn_conv_w=m_gdn_conv_w,
               gdn_a_log=m_gdn_a_log, gdn_dt_bias=m_gdn_dt_bias, gdn_norm_g=m_gdn_norm_g)
    var = dict(c_ctx=v_c_ctx, w_mod=v_w_mod, b_mod=v_b_mod, g_pre=v_g_pre, g_post=v_g_post, w_in=v_w_in, w_out=v_w_out,
               ret_norm_g=v_ret_norm_g, sg_w=v_sg_w, sg_b=v_sg_b, sc_conv_w=v_sc_conv_w, gdn_conv_w=v_gdn_conv_w,
               gdn_a_log=v_gdn_a_log, gdn_dt_bias=v_gdn_dt_bias, gdn_norm_g=v_gdn_norm_g)
    depth = w_mod.shape[0]
    n_mod, n_in, n_out = w_mod.shape[2], w_in.shape[2], w_out.shape[1]
    n_sc, n_gdn = sc_conv_w.shape[2], gdn_conv_w.shape[2]
    xi, yi, ci = _me()
    me = 4 * xi + 2 * yi + ci

    conv = _pack([sc_conv_w, gdn_conv_w])
    n_conv = depth * 3 * n_sc
    shards = [w_mod.astype(BF16), w_in.astype(BF16), w_out.astype(BF16), conv]
    fulls = [(depth, D, N_DEV * n_mod), (N_DEV, depth, D, n_in), (depth, N_DEV * n_out, D), (N_DEV,) + conv.shape]
    blocks = [lambda r, j: r.at[:, :, pl.ds(pl.multiple_of(j * n_mod, LANES), n_mod)],
              lambda r, j: r.at[j],
              lambda r, j: r.at[:, pl.ds(pl.multiple_of(j * n_out, 16), n_out), :],
              lambda r, j: r.at[j]]
    wmod_f, win_g, wout_f, conv_g = _gather_weights(shards, fulls, blocks)
    win_f = jnp.pad(win_g.transpose(1, 2, 0, 3).reshape(depth, D, N_DEV * n_in),
                    ((0, 0), (0, 0), (0, W_PAD - N_DEV * n_in)))
    r_sc = _pack_rows(sc_conv_w.shape)
    scw_f = conv_g[:, :r_sc].reshape(N_DEV, -1)[:, :n_conv]
    scw_f = scw_f.reshape(N_DEV, depth, 3, n_sc).transpose(1, 2, 0, 3).reshape(depth, 3, -1)
    gdnw_f = conv_g[:, r_sc:].reshape(N_DEV, -1)[:, :depth * 3 * n_gdn]
    gdnw_f = gdnw_f.reshape(N_DEV, depth, 3, n_gdn).transpose(1, 2, 0, 3)
    gdnw_f = gdnw_f.reshape(depth, 3, -1)

    loss8, dx, g, cvec8 = _local_step(x, c, ctx, c_ctx, wmod_f, b_mod, g_pre, g_post, win_f, wout_f, ret_norm_g, sg_w,
                                      sg_b, scw_f, gdnw_f, gdn_a_log, gdn_dt_bias, gdn_norm_g, loss_target)

    gin = g["win"].astype(BF16).reshape(depth, D, N_DEV, n_in).transpose(2, 0, 1, 3)
    gout = g["wout"].astype(BF16)
    slabs = [lambda r, j: r.at[j], lambda r, j: r.at[:, pl.ds(pl.multiple_of(j * n_out, 16), n_out), :]]
    got_in, got_out = _scatter_pair([gin, gout], slabs, [(depth, D, n_in), (depth, n_out, D)])
    mine_in = lax.dynamic_index_in_dim(gin.reshape(4, 2, depth, D, n_in), ci, axis=1, keepdims=False)
    mine_out = lax.dynamic_index_in_dim(gout.reshape(depth, 4, 2, n_out, D), ci, axis=2, keepdims=False)
    mine_out = mine_out.transpose(1, 0, 2, 3)
    local_small = dict(c_ctx=g["c_ctx"], b_mod=g["bmod"], g_pre=g["gpre"], g_post=g["gpost"], ret_norm_g=g["ret_ng"],
                       sg_w=g["sgw"], sg_b=g["sgb"], sc_conv_w=g["scw"], gdn_conv_w=g["gdnw"], gdn_a_log=g["alog"],
                       gdn_dt_bias=g["dtb"], gdn_norm_g=g["gdn_ng"])
    to_sum = _pack([loss8[0, :1]] + [local_small[k] for k in SMALL])
    rows_sum = to_sum.shape[0]
    as_is = _pack([cvec8[:3], g["dmod"][:, :3, :]])
    far_in, far_out, everyone = _scatter_chips([_add_rows([mine_in, got_in], BF16, "pair_sum_in"),
                                                _add_rows([mine_out, got_out], BF16, "pair_sum_out")],
                                               jnp.concatenate([to_sum, as_is], axis=0))
    chip = 2 * xi + yi
    own = lambda a: lax.dynamic_index_in_dim(a, chip, axis=0, keepdims=False)
    grad = dict(w_in=_add_rows([own(mine_in), own(got_in), far_in[0], far_in[1], far_in[2]], F32, "grad_sum_in"),
                w_out=_add_rows([own(mine_out), own(got_out), far_out[0], far_out[1], far_out[2]], F32,
                                "grad_sum_out"))

    small_sum = _unpack(_sum_devices(everyone[:, :rows_sum]), [(1,)] + [local_small[k].shape for k in SMALL])
    loss = small_sum[0][0]
    for k, val in zip(SMALL, small_sum[1:]):
        grad[k] = val
    grad["sc_conv_w"] = lax.dynamic_slice_in_dim(grad["sc_conv_w"], me * n_sc, n_sc, axis=2)
    grad["gdn_conv_w"] = lax.dynamic_slice_in_dim(grad["gdn_conv_w"], me * n_gdn, n_gdn, axis=2)
    r_c = _pack_rows((3, D))
    c_all = everyone[:, rows_sum:rows_sum + r_c].reshape(N_DEV, -1)[:, :3 * D].reshape(N_DEV * 3, D)
    dmod_all = everyone[:, rows_sum + r_c:].reshape(N_DEV, -1)[:, :depth * 9 * D]
    dmod_all = dmod_all.reshape(N_DEV, depth, 3, 3 * D).transpose(1, 0, 2, 3)
    dmod_mine = lax.dynamic_slice_in_dim(dmod_all.reshape(depth, N_DEV * 3, 3 * D), me * n_mod, n_mod, axis=2)
    grad["w_mod"] = _wmod_grad(_pad_rows(c_all, 32), jnp.pad(dmod_mine, ((0, 0), (0, 32 - N_DEV * 3), (0, 0))))

    delta, new_m, new_v = {}, {}, {}
    for k in ("w_mod", "w_in", "w_out"):
        shp = wts[k].shape
        two = lambda a: a.reshape(-1, shp[-1])
        res = _adamw(two(wts[k]), two(grad[k]), two(mom[k]), two(var[k]), "adamw_" + k)
        delta[k], new_m[k], new_v[k] = [r.reshape(shp) for r in res]
    res = _adamw(*[_pack([d[k] for k in SMALL]) for d in (wts, grad, mom, var)], "adamw_small")
    for dst, flat in zip((delta, new_m, new_v), res):
        for k, val in zip(SMALL, _unpack(flat, [wts[k].shape for k in SMALL])):
            dst[k] = val
    return (loss, dx, *[grad[k] for k in ORDER], *[delta[k] for k in ORDER], *[new_m[k] for k in ORDER],
            *[new_v[k] for k in ORDER])
```

```python
import functools
import math

import jax
import jax.numpy as jnp
import numpy as np
from jax import lax
from jax.experimental import pallas as pl
from jax.experimental.pallas import tpu as pltpu

F32, BF16 = jnp.float32, jnp.bfloat16
HI = lax.Precision.HIGHEST

N_DEV = 8
D = 1024
DEPTH = 4
BRW = 256
HD = 64
NH = 4
LANES = 128
GRID_W = 64
ROPE_BASE = 10000.0
W_IN = 15 * BRW + 4 * NH
W_PAD = 31 * LANES
RC = 128
GC = 64
EPS = 1e-6
LOG_GAMMA = tuple(math.log(1.0 - 2.0 ** (-5.0 - h)) for h in range(NH))
ADAM_LR, ADAM_B1, ADAM_B2, ADAM_EPS, ADAM_WD, ADAM_STEP = 0.001, 0.9, 0.999, 1e-08, 0.01, 10
VMEM_LIMIT = 56 * 1024 * 1024

COL_RET, COL_SG, COL_SC, COL_GDN = 0, 4, 7, 11
COL_A128 = 30


def _params(sem):
    return pltpu.CompilerParams(dimension_semantics=sem, vmem_limit_bytes=VMEM_LIMIT)


def _bdot(a, b, ca, cb):
    if a.ndim == 3:
        dn = (((ca + 1,), (cb + 1,)), ((0,), (0,)))
    else:
        dn = (((ca,), (cb,)), ((), ()))
    return lax.dot_general(a.astype(BF16), b.astype(BF16), dn, preferred_element_type=F32)


@jax.custom_vjp
def mm(a, b):
    return _bdot(a, b, 1, 0)


mm.defvjp(lambda a, b: (_bdot(a, b, 1, 0), (a, b)),
          lambda r, g: (_bdot(g, r[1], 1, 1), _bdot(r[0], g, 0, 0)))


@jax.custom_vjp
def mm_nt(a, b):
    return _bdot(a, b, 1, 1)


mm_nt.defvjp(lambda a, b: (_bdot(a, b, 1, 1), (a, b)),
             lambda r, g: (_bdot(g, r[1], 1, 0), _bdot(g, r[0], 0, 0)))


@jax.custom_vjp
def mm_tn(a, b):
    return _bdot(a, b, 0, 0)


mm_tn.defvjp(lambda a, b: (_bdot(a, b, 0, 0), (a, b)),
             lambda r, g: (_bdot(r[1], g, 1, 1), _bdot(r[0], g, 1, 0)))


def _dotf(a, b):
    return jnp.dot(a, b, precision=HI, preferred_element_type=F32)


def _iota(shape, dim):
    return lax.broadcasted_iota(jnp.int32, shape, dim)


def _head_mask(h, width=BRW):
    return (_iota((1, width), 1) // HD == h).astype(F32)


def _lane_by_head(vals, width=BRW, lane0=0):
    head = (_iota((1, width), 1) + lane0) // HD
    out = jnp.full((1, width), vals[NH - 1], F32)
    for h in range(NH - 2, -1, -1):
        out = jnp.where(head == h, vals[h], out)
    return out


def _block_diag(n, width):
    return (_iota((n, width), 0) // HD == _iota((n, width), 1) // HD).astype(F32)


@jax.custom_vjp
def _head_sum(x):
    w = x.shape[1]
    ones = _block_diag(w, w).astype(BF16)
    hi = x.astype(BF16)
    lo = (x - hi.astype(F32)).astype(BF16)
    return jnp.dot(hi, ones, preferred_element_type=F32) + jnp.dot(lo, ones, preferred_element_type=F32)


_head_sum.defvjp(lambda x: (_head_sum(x), None), lambda _, g: (_head_sum(g),))


def _silu(x):
    return x * jax.nn.sigmoid(x)


def _stack_heads(x):
    return jnp.concatenate([x * _head_mask(h) for h in range(NH)], axis=-2)


@jax.custom_vjp
def _unstack_heads(x):
    n = x.shape[-2] // NH
    return (x[..., 0:n, :] + x[..., n:2 * n, :]) + (x[..., 2 * n:3 * n, :] + x[..., 3 * n:4 * n, :])


_unstack_heads.defvjp(lambda x: (_unstack_heads(x), None), lambda _, g: (_stack_heads(g),))


@jax.custom_vjp
def _rot_half(x):
    n = x.shape[1]
    first = (_iota(x.shape, 1) % 32) < 16
    return jnp.where(first, -pltpu.roll(x, n - 16, 1), pltpu.roll(x, 16, 1))


_rot_half.defvjp(lambda x: (_rot_half(x), None), lambda _, g: (-_rot_half(g),))


def _rotary(x, cos, sin):
    return x * cos + _rot_half(x) * sin


def _make_shifts(seq, t_ctx):
    def dn_raw(x):
        r = _iota(x.shape, 0)
        return jnp.where((r == 0) | (r == t_ctx), 0.0, pltpu.roll(x, 1, 0))

    def up_raw(x):
        r = _iota(x.shape, 0)
        return jnp.where((r == t_ctx - 1) | (r == seq - 1), 0.0, pltpu.roll(x, seq - 1, 0))

    @jax.custom_vjp
    def dn(x):
        return dn_raw(x)

    @jax.custom_vjp
    def up(x):
        return up_raw(x)

    dn.defvjp(lambda x: (dn_raw(x), None), lambda _, g: (up_raw(g),))
    up.defvjp(lambda x: (up_raw(x), None), lambda _, g: (dn_raw(g),))
    return dn, up


def _conv3(t, w0, w1, w2, dn, up):
    return dn(t) * w0 + t * w1 + up(t) * w2


def _acc(ref, val, first, at=()):
    idx = at + (Ellipsis,)

    @pl.when(first)
    def _():
        ref[idx] = val

    @pl.when(jnp.logical_not(first))
    def _():
        ref[idx] += val


def _mod_fwd(cvec8, wmod, bmod):
    depth = wmod.shape[0]

    def body(c_ref, w_ref, b_ref, o_ref):
        sc = _silu(c_ref[...])
        o_ref[0] = jnp.dot(sc.astype(BF16), w_ref[0], preferred_element_type=F32) + b_ref[0]

    return pl.pallas_call(
        body, grid=(depth, 3),
        in_specs=[pl.BlockSpec((8, D), lambda l, j: (0, 0)),
                  pl.BlockSpec((1, D, D), lambda l, j: (l, 0, j)),
                  pl.BlockSpec((1, 1, D), lambda l, j: (l, 0, j))],
        out_specs=pl.BlockSpec((1, 8, D), lambda l, j: (l, 0, j)),
        out_shape=jax.ShapeDtypeStruct((depth, 8, 3 * D), F32),
        compiler_params=_params(("arbitrary", "arbitrary")), name="mod_fwd")(cvec8, wmod, bmod)


def _mod_bwd(dmod, wmod, cvec8):
    depth = wmod.shape[0]

    def body(dm_ref, w_ref, c_ref, dc_ref, db_ref):
        l, j = pl.program_id(0), pl.program_id(1)
        dm = dm_ref[0]
        db_ref[0] = jnp.sum(dm, axis=0, keepdims=True)
        part = _bdot(dm, w_ref[0], 1, 1)
        _acc(dc_ref, part, (l == 0) & (j == 0))

        @pl.when((l == depth - 1) & (j == 2))
        def _():
            c = c_ref[...]
            s = jax.nn.sigmoid(c)
            dc_ref[...] = dc_ref[...] * (s * (1.0 + c * (1.0 - s)))

    return pl.pallas_call(
        body, grid=(depth, 3),
        in_specs=[pl.BlockSpec((1, 8, D), lambda l, j: (l, 0, j)),
                  pl.BlockSpec((1, D, D), lambda l, j: (l, 0, j)),
                  pl.BlockSpec((8, D), lambda l, j: (0, 0))],
        out_specs=[pl.BlockSpec((8, D), lambda l, j: (0, 0)),
                   pl.BlockSpec((1, 1, D), lambda l, j: (l, 0, j))],
        out_shape=[jax.ShapeDtypeStruct((8, D), F32), jax.ShapeDtypeStruct((depth, 1, 3 * D), F32)],
        compiler_params=_params(("arbitrary", "arbitrary")), name="mod_bwd")(dmod, wmod, cvec8)


def _wmod_grad(c_rows, dmod_cols):
    depth, rows, n = dmod_cols.shape

    def body(c_ref, dm_ref, o_ref):
        sc = _silu(c_ref[...])
        o_ref[0] = lax.dot_general(sc, dm_ref[0], (((0,), (0,)), ((), ())), precision=HI,
                                   preferred_element_type=F32)

    return pl.pallas_call(
        body, grid=(depth,),
        in_specs=[pl.BlockSpec((rows, D), lambda l: (0, 0)), pl.BlockSpec((1, rows, n), lambda l: (l, 0, 0))],
        out_specs=pl.BlockSpec((1, D, n), lambda l: (l, 0, 0)),
        out_shape=jax.ShapeDtypeStruct((depth, D, n), F32),
        compiler_params=_params(("arbitrary",)), name="wmod_grad")(c_rows, dmod_cols)


class _Lay:
    def __init__(self, batch, t_ctx, t_lat):
        self.b, self.t_ctx, self.t_lat = batch, t_ctx, t_lat
        self.s = t_ctx + t_lat
        self.tm = min(256, t_ctx)
        self.tpb = self.s // self.tm
        self.nct = t_ctx // self.tm
        self.ntiles = batch * self.tpb
        self.rows = batch * self.s

    def mod_row(self, i):
        return jnp.where(i % self.tpb < self.nct, 2, i // self.tpb)

    def group(self, i):
        return 2 * (i // self.tpb) + jnp.where(i % self.tpb < self.nct, 0, 1)

    def group_first(self, i):
        return (i % self.tpb == 0) | (i % self.tpb == self.nct)


def _norm_mod(x, g, shift, scale):
    r = lax.rsqrt(jnp.mean(x * x, axis=-1, keepdims=True) + EPS)
    return (x * r * g) * (1.0 + scale) + shift


def _inproj_fwd(lay, xc, mod3, gpre, w):
    tm = lay.tm

    def body(x_ref, sh_ref, sc_ref, g_ref, w_ref, p_ref, ht_ref):
        h = _norm_mod(x_ref[...], g_ref[...], sh_ref[0, 0], sc_ref[0, 0])
        ht_ref[...] = h.T.astype(BF16)
        p_ref[...] = jnp.dot(h.astype(BF16), w_ref[...], preferred_element_type=F32)

    return pl.pallas_call(
        body, grid=(lay.ntiles,),
        in_specs=[pl.BlockSpec((tm, D), lambda i: (i, 0)),
                  pl.BlockSpec((1, 1, 1, D), lambda i: (0, lay.mod_row(i), 0, 0)),
                  pl.BlockSpec((1, 1, 1, D), lambda i: (1, lay.mod_row(i), 0, 0)),
                  pl.BlockSpec((1, D), lambda i: (0, 0)),
                  pl.BlockSpec((D, W_PAD), lambda i: (0, 0))],
        out_specs=[pl.BlockSpec((tm, W_PAD), lambda i: (i, 0)), pl.BlockSpec((D, tm), lambda i: (0, i))],
        out_shape=[jax.ShapeDtypeStruct((lay.rows, W_PAD), F32), jax.ShapeDtypeStruct((D, lay.rows), BF16)],
        compiler_params=_params(("arbitrary",)), name="inproj_fwd")(xc, mod3, mod3, gpre, w)


def _inproj_bwd(lay, xc, mod3, gpre, wt, dxc, pieces):
    tm = lay.tm
    npc = len(pieces)
    offs = [off for _, off in pieces]

    def body(*refs):
        x_ref, sh_ref, sc_ref, g_ref, wt_ref, dx_in = refs[:6]
        dps = refs[6:6 + npc]
        dx_ref, dg_ref, dsh_ref, dsc_ref = refs[6 + npc:]
        i = pl.program_id(0)
        dh = None
        for dp_ref, off in zip(dps, offs):
            wd = dp_ref.shape[1]
            part = jnp.dot(dp_ref[...], wt_ref[off:off + wd, :], preferred_element_type=F32)
            dh = part if dh is None else dh + part
        _, vjp = jax.vjp(_norm_mod, x_ref[...], g_ref[...], sh_ref[0, 0], sc_ref[0, 0])
        dx, dg, dsh, dsc = vjp(dh)
        dx_ref[...] = dx_in[...] + dx
        _acc(dg_ref, dg, i == 0)
        first = lay.group_first(i)
        _acc(dsh_ref, dsh, first, at=(0,))
        _acc(dsc_ref, dsc, first, at=(0,))

    return pl.pallas_call(
        body, grid=(lay.ntiles,),
        in_specs=[pl.BlockSpec((tm, D), lambda i: (i, 0)),
                  pl.BlockSpec((1, 1, 1, D), lambda i: (0, lay.mod_row(i), 0, 0)),
                  pl.BlockSpec((1, 1, 1, D), lambda i: (1, lay.mod_row(i), 0, 0)),
                  pl.BlockSpec((1, D), lambda i: (0, 0)),
                  pl.BlockSpec((W_PAD, D), lambda i: (0, 0)),
                  pl.BlockSpec((tm, D), lambda i: (i, 0))]
        + [pl.BlockSpec((tm, dp.shape[1]), lambda i: (i, 0)) for dp, _ in pieces],
        out_specs=[pl.BlockSpec((tm, D), lambda i: (i, 0)),
                   pl.BlockSpec((1, D), lambda i: (0, 0)),
                   pl.BlockSpec((1, 1, D), lambda i: (lay.group(i), 0, 0)),
                   pl.BlockSpec((1, 1, D), lambda i: (lay.group(i), 0, 0))],
        out_shape=[jax.ShapeDtypeStruct((lay.rows, D), F32), jax.ShapeDtypeStruct((1, D), F32),
                   jax.ShapeDtypeStruct((2 * lay.b, 1, D), F32), jax.ShapeDtypeStruct((2 * lay.b, 1, D), F32)],
        compiler_params=_params(("arbitrary",)), name="inproj_bwd",
    )(xc, mod3, mod3, gpre, wt, dxc, *[dp for dp, _ in pieces])


def _weight_grad(lay, ht, dp, name):
    wd = dp.shape[1]
    tn = 512 if wd % 512 == 0 else (256 if wd % 256 == 0 else LANES)
    tr = lay.rows // 3 if lay.rows % (3 * 256) == 0 else lay.tm

    def body(ht_ref, dp_ref, o_ref):
        _acc(o_ref, jnp.dot(ht_ref[...], dp_ref[...], preferred_element_type=F32), pl.program_id(1) == 0)

    return pl.pallas_call(
        body, grid=(wd // tn, lay.rows // tr),
        in_specs=[pl.BlockSpec((D, tr), lambda j, i: (0, i)), pl.BlockSpec((tr, tn), lambda j, i: (i, j))],
        out_specs=pl.BlockSpec((D, tn), lambda j, i: (0, j)),
        out_shape=jax.ShapeDtypeStruct((D, wd), F32),
        compiler_params=_params(("arbitrary", "arbitrary")), name=name)(ht, dp)


def _outproj_post(o, x, gpost, gate):
    r = lax.rsqrt(jnp.mean(o * o, axis=-1, keepdims=True) + EPS)
    return x + gate * (o * r * gpost)


def _outproj_matmul(ys, w_ref):
    o = None
    for k, y in enumerate(ys):
        part = jnp.dot(y[...], w_ref[BRW * k:BRW * (k + 1), :], preferred_element_type=F32)
        o = part if o is None else o + part
    return o


def _outproj_specs(lay):
    tm = lay.tm
    return ([pl.BlockSpec((tm, BRW), lambda i: (i, 0))] * 4
            + [pl.BlockSpec((tm, D), lambda i: (i, 0))]
            + [pl.BlockSpec((D, D), lambda i: (0, 0))]
            + [pl.BlockSpec((1, D), lambda i: (0, 0))]
            + [pl.BlockSpec((1, 1, 1, D), lambda i: (2, lay.mod_row(i), 0, 0))])


def _outproj_fwd(lay, ys, xc, wout, gpost, mod3):
    tm = lay.tm

    def body(y0, y1, y2, y3, x_ref, w_ref, g_ref, gt_ref, o_ref, yt_ref):
        ys_ = (y0, y1, y2, y3)
        o_ref[...] = _outproj_post(_outproj_matmul(ys_, w_ref), x_ref[...], g_ref[...], gt_ref[0, 0])
        for k, y in enumerate(ys_):
            yt_ref[BRW * k:BRW * (k + 1), :] = y[...].astype(F32).T.astype(BF16)

    return pl.pallas_call(
        body, grid=(lay.ntiles,), in_specs=_outproj_specs(lay),
        out_specs=[pl.BlockSpec((tm, D), lambda i: (i, 0)), pl.BlockSpec((D, tm), lambda i: (0, i))],
        out_shape=[jax.ShapeDtypeStruct((lay.rows, D), F32), jax.ShapeDtypeStruct((D, lay.rows), BF16)],
        compiler_params=_params(("arbitrary",)), name="outproj_fwd")(*ys, xc, wout, gpost, mod3)


def _outproj_bwd(lay, ys, xc, wout, gpost, mod3, dxc):
    tm = lay.tm

    def body(y0, y1, y2, y3, x_ref, w_ref, g_ref, gt_ref, dx_ref, d0, d1, d2, d3, do_ref, dg_ref, dgt_ref):
        i = pl.program_id(0)
        o = _outproj_matmul((y0, y1, y2, y3), w_ref)
        _, vjp = jax.vjp(_outproj_post, o, x_ref[...], g_ref[...], gt_ref[0, 0])
        do, _, dg, dgt = vjp(dx_ref[...])
        do = do.astype(BF16)
        do_ref[...] = do
        for k, d in enumerate((d0, d1, d2, d3)):
            d[...] = _bdot(do, w_ref[BRW * k:BRW * (k + 1), :], 1, 1)
        _acc(dg_ref, dg, i == 0)
        _acc(dgt_ref, dgt, lay.group_first(i), at=(0,))

    return pl.pallas_call(
        body, grid=(lay.ntiles,),
        in_specs=_outproj_specs(lay) + [pl.BlockSpec((tm, D), lambda i: (i, 0))],
        out_specs=[pl.BlockSpec((tm, BRW), lambda i: (i, 0))] * 4
        + [pl.BlockSpec((tm, D), lambda i: (i, 0)), pl.BlockSpec((1, D), lambda i: (0, 0)),
           pl.BlockSpec((1, 1, D), lambda i: (lay.group(i), 0, 0))],
        out_shape=[jax.ShapeDtypeStruct((lay.rows, BRW), F32)] * 4
        + [jax.ShapeDtypeStruct((lay.rows, D), BF16), jax.ShapeDtypeStruct((1, D), F32),
           jax.ShapeDtypeStruct((2 * lay.b, 1, D), F32)],
        compiler_params=_params(("arbitrary",)), name="outproj_bwd")(*ys, xc, wout, gpost, mod3, dxc)


def _loss_kernel(lay, xc3, target):
    tm, nct = lay.tm, lay.nct

    def body(x_ref, t_ref, loss_ref, dx_ref):
        b, i = pl.program_id(0), pl.program_id(1)
        lat = i >= nct
        err = x_ref[0] - t_ref[0]
        dx_ref[0] = jnp.where(lat, err * (1.0 / D), 0.0)
        part = jnp.sum(jnp.sum(err * err, axis=1, keepdims=True), axis=0, keepdims=True) * (0.5 / D)
        part = jnp.broadcast_to(jnp.where(lat, part, 0.0), (8, LANES))
        _acc(loss_ref, part, (b == 0) & (i == 0))

    return pl.pallas_call(
        body, grid=(lay.b, lay.tpb),
        in_specs=[pl.BlockSpec((1, tm, D), lambda b, i: (b, i, 0)),
                  pl.BlockSpec((1, tm, D), lambda b, i: (b, jnp.maximum(i - nct, 0), 0))],
        out_specs=[pl.BlockSpec((8, LANES), lambda b, i: (0, 0)), pl.BlockSpec((1, tm, D), lambda b, i: (b, i, 0))],
        out_shape=[jax.ShapeDtypeStruct((8, LANES), F32), jax.ShapeDtypeStruct(xc3.shape, F32)],
        compiler_params=_params(("arbitrary", "arbitrary")), name="loss")(xc3, target)


def _chunk_orders(n_ctx, n_all):
    fwd = list(range(n_all))
    rev = list(range(n_ctx - 1, -1, -1)) + list(range(n_all - 1, n_ctx - 1, -1))
    return fwd, rev


def _ret_state_fn(k, v, cos, sin):
    kt = _rotary(k, cos, sin) * (HD ** -0.5)
    lg = _lane_by_head(LOG_GAMMA)
    j = _iota((RC, 1), 0).astype(F32)
    bd = _block_diag(BRW, BRW)
    af = mm_tn(kt * jnp.exp((RC - 1.0 - j) * lg), v) * bd
    ar = mm_tn(kt * jnp.exp(j * lg), v) * bd
    return af, ar


def _ret_out_fn(q, k, v, z, cos, sin, sf, sr, ng):
    qt = _rotary(q, cos, sin)
    kt = _rotary(k, cos, sin) * (HD ** -0.5)
    diff = (_iota((RC, RC), 0) - _iota((RC, RC), 1)).astype(F32)
    o = None
    for h in range(NH):
        m = _head_mask(h)
        sc = mm_nt(qt * m, kt)
        wgt = sc * jnp.exp(jnp.abs(diff) * LOG_GAMMA[h]) * jnp.where(diff == 0, 2.0, 1.0)
        part = mm(wgt, v * m)
        o = part if o is None else o + part
    lg = _lane_by_head(LOG_GAMMA)
    i = _iota((RC, 1), 0).astype(F32)
    o = o + mm(qt, sf) * jnp.exp((i + 1.0) * lg) + mm(qt, sr) * jnp.exp((RC - i) * lg)
    mu = _head_sum(o) * (1.0 / HD)
    cen = o - mu
    var = _head_sum(cen * cen) * (1.0 / HD)
    return cen * lax.rsqrt(var + EPS) * ng * _silu(z)


def _ret_specs(lay, cols):
    return [pl.BlockSpec((1, RC, BRW), functools.partial(lambda b, i, c: (b, i, c), c=COL_RET + c)) for c in cols]


def _ret_state(lay, p3, cos, sin):
    nc = lay.s // RC

    def body(k_ref, v_ref, c_ref, s_ref, a_ref):
        af, ar = _ret_state_fn(k_ref[0], v_ref[0], c_ref[...], s_ref[...])
        a_ref[0, 0, 0] = af
        a_ref[0, 0, 1] = ar

    tab = pl.BlockSpec((RC, BRW), lambda b, i: (i, 0))
    return pl.pallas_call(
        body, grid=(lay.b, nc), in_specs=_ret_specs(lay, (1, 2)) + [tab, tab],
        out_specs=pl.BlockSpec((1, 1, 2, BRW, BRW), lambda b, i: (b, i, 0, 0, 0)),
        out_shape=jax.ShapeDtypeStruct((lay.b, nc, 2, BRW, BRW), F32),
        compiler_params=_params(("arbitrary", "arbitrary")), name="ret_state")(p3, p3, cos, sin)


def _ret_state_bwd(lay, p3, cos, sin, d_a, dpr):
    nc = lay.s // RC

    def body(k_ref, v_ref, c_ref, s_ref, da_ref, dpr_ref, o_ref):
        _, vjp = jax.vjp(lambda k, v: _ret_state_fn(k, v, c_ref[...], s_ref[...]), k_ref[0], v_ref[0])
        dk, dv = vjp((da_ref[0, 0, 0], da_ref[0, 0, 1]))
        o_ref[0, :, 0:BRW] = dpr_ref[0, :, 0:BRW].astype(BF16)
        o_ref[0, :, BRW:2 * BRW] = (dpr_ref[0, :, BRW:2 * BRW] + dk).astype(BF16)
        o_ref[0, :, 2 * BRW:3 * BRW] = (dpr_ref[0, :, 2 * BRW:3 * BRW] + dv).astype(BF16)
        o_ref[0, :, 3 * BRW:] = dpr_ref[0, :, 3 * BRW:].astype(BF16)

    tab = pl.BlockSpec((RC, BRW), lambda b, i: (i, 0))
    return pl.pallas_call(
        body, grid=(lay.b, nc),
        in_specs=_ret_specs(lay, (1, 2)) + [tab, tab,
                                            pl.BlockSpec((1, 1, 2, BRW, BRW), lambda b, i: (b, i, 0, 0, 0)),
                                            pl.BlockSpec((1, RC, 4 * BRW), lambda b, i: (b, i, 0))],
        out_specs=pl.BlockSpec((1, RC, 4 * BRW), lambda b, i: (b, i, 0)),
        out_shape=jax.ShapeDtypeStruct((lay.b, lay.s, 4 * BRW), BF16),
        compiler_params=_params(("arbitrary", "arbitrary")), name="ret_state_bwd")(p3, p3, cos, sin, d_a, dpr)


def _state_scan(lay, a, nc_ctx, transpose, name):
    b, nc = a.shape[0], a.shape[1]
    orders = _chunk_orders(nc_ctx, nc)

    def body(a_ref, o_ref):
        d, jh = pl.program_id(1), pl.program_id(2)
        head = (_iota((1, LANES), 1) + jh * LANES) // HD
        lg = jnp.full((1, LANES), LOG_GAMMA[NH - 1], F32)
        for h in range(NH - 2, -1, -1):
            lg = jnp.where(head == h, LOG_GAMMA[h], lg)
        dec = jnp.exp(RC * lg)
        for dd in (0, 1):
            @pl.when(d == dd)
            def _(order=orders[dd]):
                acc = jnp.zeros((BRW, LANES), F32)
                if not transpose:
                    for c in order:
                        o_ref[0, c, 0] = acc
                        acc = acc * dec + a_ref[0, c, 0]
                else:
                    for c in reversed(order):
                        o_ref[0, c, 0] = acc
                        acc = a_ref[0, c, 0] + acc * dec

    spec = pl.BlockSpec((1, nc, 1, BRW, LANES), lambda bb, d, jh: (bb, 0, d, 0, jh))
    return pl.pallas_call(
        body, grid=(b, 2, BRW // LANES), in_specs=[spec], out_specs=spec,
        out_shape=jax.ShapeDtypeStruct(a.shape, F32),
        compiler_params=_params(("arbitrary",) * 3), name=name)(a)


def _ret_out(lay, p3, cos, sin, states, ng):
    nc = lay.s // RC

    def body(q_ref, k_ref, v_ref, z_ref, c_ref, s_ref, st_ref, ng_ref, y_ref):
        y = _ret_out_fn(q_ref[0], k_ref[0], v_ref[0], z_ref[0], c_ref[...], s_ref[...],
                        st_ref[0, 0, 0], st_ref[0, 0, 1], ng_ref[...])
        y_ref[0] = y.astype(BF16)

    tab = pl.BlockSpec((RC, BRW), lambda b, i: (i, 0))
    return pl.pallas_call(
        body, grid=(lay.b, nc),
        in_specs=_ret_specs(lay, (0, 1, 2, 3)) + [tab, tab,
                                                  pl.BlockSpec((1, 1, 2, BRW, BRW), lambda b, i: (b, i, 0, 0, 0)),
                                                  pl.BlockSpec((1, BRW), lambda b, i: (0, 0))],
        out_specs=pl.BlockSpec((1, RC, BRW), lambda b, i: (b, i, 0)),
        out_shape=jax.ShapeDtypeStruct((lay.b, lay.s, BRW), BF16),
        compiler_params=_params(("arbitrary", "arbitrary")), name="ret_out")(p3, p3, p3, p3, cos, sin, states, ng)


def _ret_out_bwd(lay, p3, cos, sin, states, ng, dy):
    nc = lay.s // RC

    def body(q_ref, k_ref, v_ref, z_ref, c_ref, s_ref, st_ref, ng_ref, dy_ref, dp_ref, dst_ref, dng_ref):
        b, i = pl.program_id(0), pl.program_id(1)
        fn = lambda q, k, v, z, sf, sr, ng: _ret_out_fn(q, k, v, z, c_ref[...], s_ref[...], sf, sr, ng)
        _, vjp = jax.vjp(fn, q_ref[0], k_ref[0], v_ref[0], z_ref[0], st_ref[0, 0, 0], st_ref[0, 0, 1], ng_ref[...])
        dq, dk, dv, dz, dsf, dsr, dng = vjp(dy_ref[0])
        for n, g in enumerate((dq, dk, dv, dz)):
            dp_ref[0, :, BRW * n:BRW * (n + 1)] = g
        dst_ref[0, 0, 0] = dsf
        dst_ref[0, 0, 1] = dsr
        _acc(dng_ref, dng, (b == 0) & (i == 0))

    tab = pl.BlockSpec((RC, BRW), lambda b, i: (i, 0))
    st = pl.BlockSpec((1, 1, 2, BRW, BRW), lambda b, i: (b, i, 0, 0, 0))
    return pl.pallas_call(
        body, grid=(lay.b, nc),
        in_specs=_ret_specs(lay, (0, 1, 2, 3)) + [tab, tab, st, pl.BlockSpec((1, BRW), lambda b, i: (0, 0)),
                                                  pl.BlockSpec((1, RC, BRW), lambda b, i: (b, i, 0))],
        out_specs=[pl.BlockSpec((1, RC, 4 * BRW), lambda b, i: (b, i, 0)), st,
                   pl.BlockSpec((1, BRW), lambda b, i: (0, 0))],
        out_shape=[jax.ShapeDtypeStruct((lay.b, lay.s, 4 * BRW), F32),
                   jax.ShapeDtypeStruct(states.shape, F32), jax.ShapeDtypeStruct((1, BRW), F32)],
        compiler_params=_params(("arbitrary", "arbitrary")), name="ret_out_bwd",
    )(p3, p3, p3, p3, cos, sin, states, ng, dy)


def _sg_fn(u, v, z, w, b8):
    ug = jax.nn.gelu(u)
    vg = jax.nn.gelu(v)
    mu = jnp.mean(vg, axis=-1, keepdims=True)
    cen = vg - mu
    vn = cen * lax.rsqrt(jnp.mean(cen * cen, axis=-1, keepdims=True) + EPS)
    masks = (_iota((NH, 1, BRW), 2) // HD == _iota((NH, 1, BRW), 0)).astype(F32)
    s = jnp.sum(mm(w, vn[None] * masks), axis=0)
    expand = (_iota((8, BRW), 1) // HD == _iota((8, BRW), 0)).astype(F32)
    bias = lax.dot_general(b8, expand, (((0,), (0,)), ((), ())), precision=HI, preferred_element_type=F32)
    return ug * (s + bias) * _silu(z)


def _sg_chunks(lay):
    return 2 if (lay.s // RC) % 2 == 0 else 1


def _sg_specs(lay):
    rows = _sg_chunks(lay) * RC
    return ([pl.BlockSpec((1, rows, BRW), functools.partial(lambda b, i, c: (b, i, c), c=COL_SG + c)) for c in range(3)]
            + [pl.BlockSpec((NH, RC, RC), lambda b, i: (0, 0, 0)), pl.BlockSpec((8, RC), lambda b, i: (0, 0))])


def _sg_fwd(lay, p3, sgw, sgb8):
    per = _sg_chunks(lay)

    def body(u_ref, v_ref, z_ref, w_ref, b_ref, y_ref):
        for k in range(per):
            rows = pl.ds(RC * k, RC)
            y = _sg_fn(u_ref[0, rows, :], v_ref[0, rows, :], z_ref[0, rows, :], w_ref[...], b_ref[...])
            y_ref[0, rows, :] = y.astype(BF16)

    return pl.pallas_call(
        body, grid=(lay.b, lay.s // (per * RC)), in_specs=_sg_specs(lay),
        out_specs=pl.BlockSpec((1, per * RC, BRW), lambda b, i: (b, i, 0)),
        out_shape=jax.ShapeDtypeStruct((lay.b, lay.s, BRW), BF16),
        compiler_params=_params(("arbitrary", "arbitrary")), name="sg_fwd")(p3, p3, p3, sgw, sgb8)


def _sg_bwd(lay, p3, sgw, sgb8, dy):
    per = _sg_chunks(lay)

    def body(u_ref, v_ref, z_ref, w_ref, b_ref, dy_ref, dp_ref, dw_ref, db_ref):
        first = (pl.program_id(0) == 0) & (pl.program_id(1) == 0)
        dw = db = None
        for k in range(per):
            rows = pl.ds(RC * k, RC)
            _, vjp = jax.vjp(_sg_fn, u_ref[0, rows, :], v_ref[0, rows, :], z_ref[0, rows, :], w_ref[...], b_ref[...])
            g = vjp(dy_ref[0, rows, :])
            for n in range(3):
                dp_ref[0, rows, BRW * n:BRW * (n + 1)] = g[n].astype(BF16)
            dw = g[3] if dw is None else dw + g[3]
            db = g[4] if db is None else db + g[4]
        _acc(dw_ref, dw, first)
        _acc(db_ref, db, first)

    return pl.pallas_call(
        body, grid=(lay.b, lay.s // (per * RC)),
        in_specs=_sg_specs(lay) + [pl.BlockSpec((1, per * RC, BRW), lambda b, i: (b, i, 0))],
        out_specs=[pl.BlockSpec((1, per * RC, 3 * BRW), lambda b, i: (b, i, 0)),
                   pl.BlockSpec((NH, RC, RC), lambda b, i: (0, 0, 0)), pl.BlockSpec((8, RC), lambda b, i: (0, 0))],
        out_shape=[jax.ShapeDtypeStruct((lay.b, lay.s, 3 * BRW), BF16),
                   jax.ShapeDtypeStruct((NH, RC, RC), F32), jax.ShapeDtypeStruct((8, RC), F32)],
        compiler_params=_params(("arbitrary", "arbitrary")), name="sg_bwd")(p3, p3, p3, sgw, sgb8, dy)


def _sc_specs(lay):
    first = COL_SC * BRW // LANES
    blk = [pl.BlockSpec((1, lay.s, LANES), functools.partial(lambda j, b, c: (b, 0, c + j), c=first + 2 * n))
           for n in range(4)]
    return blk + [pl.BlockSpec((8, LANES), lambda j, b: (0, j))]


def _sc_fwd(lay, p3, w8):
    dn, up = _make_shifts(lay.s, lay.t_ctx)

    def fn(b_, c_, h_, z_, w0, w1, w2):
        return b_ * _conv3(c_ * h_, w0, w1, w2, dn, up) * _silu(z_)

    def body(b_ref, c_ref, h_ref, z_ref, w_ref, y_ref):
        y = fn(b_ref[0], c_ref[0], h_ref[0], z_ref[0], w_ref[0:1, :], w_ref[1:2, :], w_ref[2:3, :])
        y_ref[0] = y.astype(BF16)

    return pl.pallas_call(
        body, grid=(BRW // LANES, lay.b), in_specs=_sc_specs(lay),
        out_specs=pl.BlockSpec((1, lay.s, LANES), lambda j, b: (b, 0, j)),
        out_shape=jax.ShapeDtypeStruct((lay.b, lay.s, BRW), BF16),
        compiler_params=_params(("arbitrary", "arbitrary")), name="sc_fwd")(p3, p3, p3, p3, w8)


def _sc_bwd(lay, p3, w8, dy):
    dn, up = _make_shifts(lay.s, lay.t_ctx)

    def fn(b_, c_, h_, z_, w0, w1, w2):
        return b_ * _conv3(c_ * h_, w0, w1, w2, dn, up) * _silu(z_)

    def body(b_ref, c_ref, h_ref, z_ref, w_ref, dy_ref, db_ref, dc_ref, dh_ref, dz_ref, dw_ref):
        _, vjp = jax.vjp(fn, b_ref[0], c_ref[0], h_ref[0], z_ref[0], w_ref[0:1, :], w_ref[1:2, :], w_ref[2:3, :])
        g = vjp(dy_ref[0])
        for ref, val in zip((db_ref, dc_ref, dh_ref, dz_ref), g[:4]):
            ref[0] = val.astype(BF16)
        dw = jnp.concatenate([g[4], g[5], g[6], jnp.zeros((5, LANES), F32)], axis=0)
        _acc(dw_ref, dw, pl.program_id(1) == 0)

    out = pl.BlockSpec((1, lay.s, LANES), lambda j, b: (b, 0, j))
    return pl.pallas_call(
        body, grid=(BRW // LANES, lay.b), in_specs=_sc_specs(lay) + [out],
        out_specs=[out] * 4 + [pl.BlockSpec((8, LANES), lambda j, b: (0, j))],
        out_shape=[jax.ShapeDtypeStruct((lay.b, lay.s, BRW), BF16)] * 4 + [jax.ShapeDtypeStruct((8, BRW), F32)],
        compiler_params=_params(("arbitrary", "arbitrary")), name="sc_bwd")(p3, p3, p3, p3, w8, dy)


def _gdn_conv_fn(x, w0, w1, w2, normed, dn, up):
    a = _silu(_conv3(x, w0, w1, w2, dn, up))
    nrm = a * lax.rsqrt(_head_sum(a * a) + EPS)
    return jnp.where(normed, nrm, a)


def _gdn_conv(lay, p3, w8):
    dn, up = _make_shifts(lay.s, lay.t_ctx)
    first = COL_GDN * BRW // LANES

    def body(x_ref, w_ref, o_ref):
        normed = pl.program_id(0) < 2 * BRW // LANES
        o_ref[0] = _gdn_conv_fn(x_ref[0], w_ref[0:1, :], w_ref[1:2, :], w_ref[2:3, :], normed, dn, up)

    return pl.pallas_call(
        body, grid=(3 * BRW // LANES, lay.b),
        in_specs=[pl.BlockSpec((1, lay.s, LANES), lambda j, b: (b, 0, first + j)),
                  pl.BlockSpec((8, LANES), lambda j, b: (0, j))],
        out_specs=pl.BlockSpec((1, lay.s, LANES), lambda j, b: (b, 0, j)),
        out_shape=jax.ShapeDtypeStruct((lay.b, lay.s, 3 * BRW), F32),
        compiler_params=_params(("arbitrary", "arbitrary")), name="gdn_conv")(p3, w8)


def _gdn_conv_bwd(lay, p3, w8, dqkv):
    dn, up = _make_shifts(lay.s, lay.t_ctx)
    first = COL_GDN * BRW // LANES

    def body(x_ref, w_ref, g_ref, dx_ref, dw_ref):
        normed = pl.program_id(0) < 2 * BRW // LANES
        fn = lambda x, w0, w1, w2: _gdn_conv_fn(x, w0, w1, w2, normed, dn, up)
        _, vjp = jax.vjp(fn, x_ref[0], w_ref[0:1, :], w_ref[1:2, :], w_ref[2:3, :])
        g = vjp(g_ref[0])
        dx_ref[0] = g[0].astype(BF16)
        dw = jnp.concatenate([g[1], g[2], g[3], jnp.zeros((5, LANES), F32)], axis=0)
        _acc(dw_ref, dw, pl.program_id(1) == 0)

    blk = pl.BlockSpec((1, lay.s, LANES), lambda j, b: (b, 0, j))
    return pl.pallas_call(
        body, grid=(3 * BRW // LANES, lay.b),
        in_specs=[pl.BlockSpec((1, lay.s, LANES), lambda j, b: (b, 0, first + j)),
                  pl.BlockSpec((8, LANES), lambda j, b: (0, j)), blk],
        out_specs=[blk, pl.BlockSpec((8, LANES), lambda j, b: (0, j))],
        out_shape=[jax.ShapeDtypeStruct((lay.b, lay.s, 3 * BRW), BF16), jax.ShapeDtypeStruct((8, 3 * BRW), F32)],
        compiler_params=_params(("arbitrary", "arbitrary")), name="gdn_conv_bwd")(p3, w8, dqkv)


def _tri_inverse(low):
    i, j = _iota(low.shape, low.ndim - 2), _iota(low.shape, low.ndim - 1) % GC
    t = (i == j).astype(F32)
    s = 1
    while s < GC:
        pair = (i // (2 * s)) == (j // (2 * s))
        off = pair & (((i // s) % 2) != ((j // s) % 2))
        cb = jnp.where(off, low, 0.0)
        t = t - (cb if s == 1 else _bdot(t, _stack_heads(_bdot(cb, _stack_heads(t), 1, 0)), 1, 0))
        s *= 2
    return t


@jax.custom_vjp
def _tri_solve(t, low, r1, r2):
    del low
    return _bdot(t, _stack_heads(r1), 1, 0), _bdot(t, _stack_heads(r2), 1, 0)


def _tri_solve_fwd(t, low, r1, r2):
    del low
    x1, x2 = _bdot(t, _stack_heads(r1), 1, 0), _bdot(t, _stack_heads(r2), 1, 0)
    return (x1, x2), (t, x1, x2)


def _tri_solve_bwd(res, g):
    t, x1, x2 = res
    bd = _block_diag(BRW, BRW)
    d1 = _unstack_heads(_bdot(t, g[0], 0, 0) * bd)
    d2 = _unstack_heads(_bdot(t, g[1], 0, 0) * bd)
    dlow = -(_bdot(d1, _stack_heads(x1), 1, 1) + _bdot(d2, _stack_heads(x2), 1, 1))
    return jnp.zeros_like(t), dlow, d1, d2


_tri_solve.defvjp(_tri_solve_fwd, _tri_solve_bwd)

N_PACK = 5


def _gdn_prep_fn(qn, kn, vv, a, alog, dtb, t=None):
    n = qn.shape[0]
    col = _iota((1, 1, LANES), 2)
    xx = a + dtb
    softplus = jnp.maximum(xx, 0.0) + jnp.log(1.0 + jnp.exp(-jnp.abs(xx)))
    g_small = jnp.where(col < 8, -jnp.exp(alog) * softplus, 0.0).reshape(n * GC, LANES)
    beta_small = jax.nn.sigmoid(a).reshape(n * GC, LANES)
    sel_col, sel_head = _iota((LANES, BRW), 0), _iota((LANES, BRW), 1) // HD
    g_l, b_l = [], []
    for d in (0, 1):
        g_l.append(_dotf(g_small, (sel_col == 4 * d + sel_head).astype(F32)))
        b_l.append(_dotf(beta_small, (sel_col == 8 + 4 * d + sel_head).astype(F32)))
    g_l = jnp.concatenate(g_l, axis=0).reshape(2 * n, GC, BRW)
    b_l = jnp.concatenate(b_l, axis=0).reshape(2 * n, GC, BRW)
    rev = _iota((2 * n, 1, 1), 0) >= n
    fwd = jnp.logical_not(rev)
    ri, ci = _iota((1, GC, GC), 1), _iota((1, GC, GC), 2)
    tri = ((fwd & (ri >= ci)) | (rev & (ri <= ci))).astype(F32)
    gc_l = lax.dot_general(tri, g_l, (((2,), (1,)), ((0,), (0,))), precision=HI,
                           preferred_element_type=F32)
    gtot_l = jnp.sum(g_l, axis=1, keepdims=True)
    i, j = _iota((1, GC, BRW), 1), _iota((1, GC, BRW), 2) % GC
    gc_t = jnp.sum(jnp.where(i == j, gc_l, 0.0), axis=1, keepdims=True)
    incl = (fwd & (i >= j)) | (rev & (i <= j))
    strict = (fwd & (i > j)) | (rev & (i < j))
    decay = jnp.where(incl, jnp.exp(jnp.where(incl, gc_l - gc_t, 0.0)), 0.0)
    kn2 = jnp.concatenate([kn, kn], axis=0)
    vv2 = jnp.concatenate([vv, vv], axis=0)
    qs = jnp.concatenate([qn, qn], axis=0) * (HD ** -0.5)
    kst = _stack_heads(kn2)
    kb = kn2 * b_l
    low = jnp.where(strict, mm_nt(kb, kst) * decay, 0.0)
    eg = jnp.exp(gc_l)
    t_inv = _tri_inverse(low) if t is None else t
    u, w = _tri_solve(t_inv, low, vv2 * b_l, kb * eg)
    k_tail = kn2 * jnp.exp(gtot_l - gc_l)
    intra = mm_nt(qs, kst) * decay
    return (u, w, k_tail, qs * eg, intra), jnp.exp(gtot_l), t_inv


def _prep_chunks(lay):
    return 4 if (lay.s // GC) % 4 == 0 else 2


def _gdn_prep_specs(lay):
    rows = _prep_chunks(lay) * GC
    return ([pl.BlockSpec((1, rows, BRW), functools.partial(lambda b, i, c: (b, i, c), c=c)) for c in range(3)]
            + [pl.BlockSpec((1, rows, LANES), lambda b, i: (b, i, COL_A128)),
               pl.BlockSpec((8, LANES), lambda b, i: (0, 0))])


def _gdn_prep(lay, qkv, p3, prm):
    nc, per = lay.s // GC, _prep_chunks(lay)

    def body(q_ref, k_ref, v_ref, a_ref, prm_ref, pack_ref, cd_ref, t_ref):
        chunks = lambda ref: ref[0].reshape(per, GC, ref.shape[-1])
        pack, cd, t_inv = _gdn_prep_fn(chunks(q_ref), chunks(k_ref), chunks(v_ref), chunks(a_ref),
                                       prm_ref[0:1, :], prm_ref[1:2, :])
        for d in (0, 1):
            for n in range(N_PACK):
                pack_ref[0, :, d, n] = pack[n][per * d:per * (d + 1)]
            cd_ref[0, :, d] = cd[per * d:per * (d + 1)]
            t_ref[0, :, d] = t_inv[per * d:per * (d + 1)]

    return pl.pallas_call(
        body, grid=(lay.b, nc // per), in_specs=_gdn_prep_specs(lay),
        out_specs=[pl.BlockSpec((1, per, 2, N_PACK, GC, BRW), lambda b, i: (b, i, 0, 0, 0, 0)),
                   pl.BlockSpec((1, per, 2, 1, BRW), lambda b, i: (b, i, 0, 0, 0)),
                   pl.BlockSpec((1, per, 2, GC, BRW), lambda b, i: (b, i, 0, 0, 0))],
        out_shape=[jax.ShapeDtypeStruct((lay.b, nc, 2, N_PACK, GC, BRW), F32),
                   jax.ShapeDtypeStruct((lay.b, nc, 2, 1, BRW), F32),
                   jax.ShapeDtypeStruct((lay.b, nc, 2, GC, BRW), F32)],
        compiler_params=_params(("arbitrary", "arbitrary")), name="gdn_prep")(qkv, qkv, qkv, p3, prm)


def _gdn_prep_bwd(lay, qkv, p3, prm, dpacks, dcds, t_inv):
    nc, per = lay.s // GC, _prep_chunks(lay)

    def body(q_ref, k_ref, v_ref, a_ref, prm_ref, dpf_ref, dpr_ref, dcf_ref, dcr_ref, t_ref, dqkv_ref, da_ref,
             dprm_ref):
        first = (pl.program_id(0) == 0) & (pl.program_id(1) == 0)
        chunks = lambda ref: ref[0].reshape(per, GC, ref.shape[-1])
        t_inv = jnp.concatenate([t_ref[0, :, 0], t_ref[0, :, 1]], axis=0)
        fn = lambda q, k, v, a, alog, dtb: _gdn_prep_fn(q, k, v, a, alog, dtb, t_inv)[:2]
        _, vjp = jax.vjp(fn, chunks(q_ref), chunks(k_ref), chunks(v_ref), chunks(a_ref),
                         prm_ref[0:1, :], prm_ref[1:2, :])
        dpack = tuple(jnp.concatenate([dpf_ref[0, :, n], dpr_ref[0, :, n]], axis=0) for n in range(N_PACK))
        dq, dk, dv, da, dalog, ddtb = vjp((dpack, jnp.concatenate([dcf_ref[0], dcr_ref[0]], axis=0)))
        dqkv_ref[0, :, 0:BRW] = dq.reshape(per * GC, BRW)
        dqkv_ref[0, :, BRW:2 * BRW] = dk.reshape(per * GC, BRW)
        dqkv_ref[0, :, 2 * BRW:] = dv.reshape(per * GC, BRW)
        da_ref[0] = da.reshape(per * GC, LANES).astype(BF16)
        _acc(dprm_ref, jnp.concatenate([dalog, ddtb, jnp.zeros((6, LANES), F32)], axis=0), first)

    rows_blk = per * GC
    return pl.pallas_call(
        body, grid=(lay.b, nc // per),
        in_specs=_gdn_prep_specs(lay)
        + [pl.BlockSpec((1, per, N_PACK, GC, BRW), lambda b, i: (b, i, 0, 0, 0))] * 2
        + [pl.BlockSpec((1, per, 1, BRW), lambda b, i: (b, i, 0, 0))] * 2
        + [pl.BlockSpec((1, per, 2, GC, BRW), lambda b, i: (b, i, 0, 0, 0))],
        out_specs=[pl.BlockSpec((1, rows_blk, 3 * BRW), lambda b, i: (b, i, 0)),
                   pl.BlockSpec((1, rows_blk, LANES), lambda b, i: (b, i, 0)),
                   pl.BlockSpec((8, LANES), lambda b, i: (0, 0))],
        out_shape=[jax.ShapeDtypeStruct((lay.b, lay.s, 3 * BRW), F32),
                   jax.ShapeDtypeStruct((lay.b, lay.s, LANES), BF16), jax.ShapeDtypeStruct((8, LANES), F32)],
        compiler_params=_params(("arbitrary", "arbitrary")), name="gdn_prep_bwd",
    )(qkv, qkv, qkv, p3, prm, *dpacks, *dcds, t_inv)


def _gdn_step_fn(s, u, w, k_tail, qd, intra, cdec):
    v_new = u - mm(w, s)
    o = mm(qd, s) + mm(intra, _stack_heads(v_new))
    return s * cdec + mm_tn(k_tail, v_new) * _block_diag(BRW, BRW), o


def _order_index(nc_ctx, nc, d, step):
    rev = jnp.where(step < nc_ctx, nc_ctx - 1 - step, nc + nc_ctx - 1 - step)
    return jnp.where(d == 0, step, rev)


def _gdn_scan(lay, pack, cd):
    nc, nc_ctx = lay.s // GC, lay.t_ctx // GC
    chunk = functools.partial(_order_index, nc_ctx, nc)

    def body(pf_ref, pr_ref, cf_ref, cr_ref, of_ref, or_ref, sf_ref, sr_ref, s_scr):
        @pl.when(pl.program_id(0) == 0)
        def _():
            s_scr[...] = jnp.zeros_like(s_scr)

        nb = lay.b
        s = s_scr[...]
        st = _unstack_heads(s)
        sf_ref[:, 0] = st[:nb]
        sr_ref[:, 0] = st[nb:]
        args = [jnp.concatenate([pf_ref[:, 0, 0, n], pr_ref[:, 0, 0, n]], axis=0) for n in range(N_PACK)]
        s_new, o = _gdn_step_fn(s, *args, jnp.concatenate([cf_ref[:, 0, 0], cr_ref[:, 0, 0]], axis=0))
        of_ref[:, 0] = o[:nb]
        or_ref[:, 0] = o[nb:]
        s_scr[...] = s_new

    def pk(d):
        return pl.BlockSpec((lay.b, 1, 1, N_PACK, GC, BRW), lambda t: (0, chunk(d, t), d, 0, 0, 0))

    def cdb(d):
        return pl.BlockSpec((lay.b, 1, 1, 1, BRW), lambda t: (0, chunk(d, t), d, 0, 0))

    def out(d):
        return pl.BlockSpec((lay.b, 1, GC, BRW), lambda t: (0, chunk(d, t), 0, 0))

    return pl.pallas_call(
        body, grid=(nc,), in_specs=[pk(0), pk(1), cdb(0), cdb(1)],
        out_specs=[out(0), out(1), out(0), out(1)],
        out_shape=[jax.ShapeDtypeStruct((lay.b, nc, GC, BRW), F32)] * 4,
        scratch_shapes=[pltpu.VMEM((2 * lay.b, BRW, BRW), F32)],
        compiler_params=_params(("arbitrary",)), name="gdn_scan")(pack, pack, cd, cd)


def _gdn_scan_bwd(lay, pack, cd, states, do):
    nc, nc_ctx = lay.s // GC, lay.t_ctx // GC

    def chunk(d, t):
        return _order_index(nc_ctx, nc, d, nc - 1 - t)

    def body(pf_ref, pr_ref, cf_ref, cr_ref, sf_ref, sr_ref, dof_ref, dor_ref, dpf_ref, dpr_ref, dcf_ref, dcr_ref,
             ds_scr):
        @pl.when(pl.program_id(0) == 0)
        def _():
            ds_scr[...] = jnp.zeros_like(ds_scr)

        nb = lay.b
        both = lambda f, r: jnp.concatenate([f, r], axis=0)
        args = ([_stack_heads(both(sf_ref[:, 0], sr_ref[:, 0]))]
                + [both(pf_ref[:, 0, 0, n], pr_ref[:, 0, 0, n]) for n in range(N_PACK)]
                + [both(cf_ref[:, 0, 0], cr_ref[:, 0, 0])])
        _, vjp = jax.vjp(_gdn_step_fn, *args)
        g = vjp((ds_scr[...], both(dof_ref[...], dor_ref[...])))
        ds_scr[...] = g[0]
        for n in range(N_PACK):
            dpf_ref[:, 0, n] = g[1 + n][:nb]
            dpr_ref[:, 0, n] = g[1 + n][nb:]
        dcf_ref[:, 0] = g[1 + N_PACK][:nb]
        dcr_ref[:, 0] = g[1 + N_PACK][nb:]

    def pk(d):
        return pl.BlockSpec((lay.b, 1, 1, N_PACK, GC, BRW), lambda t: (0, chunk(d, t), d, 0, 0, 0))

    def cdb(d):
        return pl.BlockSpec((lay.b, 1, 1, 1, BRW), lambda t: (0, chunk(d, t), d, 0, 0))

    def st(d):
        return pl.BlockSpec((lay.b, 1, GC, BRW), lambda t: (0, chunk(d, t), 0, 0))

    def dob(d):
        return pl.BlockSpec((lay.b, GC, BRW), lambda t: (0, chunk(d, t), 0))

    def dpk(d):
        return pl.BlockSpec((lay.b, 1, N_PACK, GC, BRW), lambda t: (0, chunk(d, t), 0, 0, 0))

    def dcb(d):
        return pl.BlockSpec((lay.b, 1, 1, BRW), lambda t: (0, chunk(d, t), 0, 0))

    return pl.pallas_call(
        body, grid=(nc,),
        in_specs=[pk(0), pk(1), cdb(0), cdb(1), st(0), st(1), dob(0), dob(1)],
        out_specs=[dpk(0), dpk(1), dcb(0), dcb(1)],
        out_shape=[jax.ShapeDtypeStruct((lay.b, nc, N_PACK, GC, BRW), F32)] * 2
        + [jax.ShapeDtypeStruct((lay.b, nc, 1, BRW), F32)] * 2,
        scratch_shapes=[pltpu.VMEM((2 * lay.b, BRW, BRW), F32)],
        compiler_params=_params(("arbitrary",)), name="gdn_scan_bwd")(pack, pack, cd, cd, *states, do, do)


def _gdn_finish_fn(o, z, ng):
    return o * lax.rsqrt(_head_sum(o * o) * (1.0 / HD) + EPS) * ng * _silu(z)


def _finish_chunks(lay):
    nc = lay.s // GC
    return 12 if nc % 12 == 0 else (6 if nc % 6 == 0 else 2)


def _gdn_o(of_ref, or_ref):
    return (of_ref[0] + or_ref[0]).reshape(of_ref.shape[1] * GC, BRW)


def _gdn_finish_specs(lay):
    per = _finish_chunks(lay)
    ob = pl.BlockSpec((1, per, GC, BRW), lambda b, i: (b, i, 0, 0))
    return [ob, ob, pl.BlockSpec((1, per * GC, BRW), lambda b, i: (b, i, COL_GDN + 3)),
            pl.BlockSpec((1, BRW), lambda b, i: (0, 0))]


def _gdn_finish(lay, o_f, o_r, p3, ng):
    rows = _finish_chunks(lay) * GC

    def body(of_ref, or_ref, z_ref, ng_ref, y_ref):
        y_ref[0] = _gdn_finish_fn(_gdn_o(of_ref, or_ref), z_ref[0], ng_ref[...]).astype(BF16)

    return pl.pallas_call(
        body, grid=(lay.b, lay.s // rows), in_specs=_gdn_finish_specs(lay),
        out_specs=pl.BlockSpec((1, rows, BRW), lambda b, i: (b, i, 0)),
        out_shape=jax.ShapeDtypeStruct((lay.b, lay.s, BRW), BF16),
        compiler_params=_params(("arbitrary", "arbitrary")), name="gdn_finish")(o_f, o_r, p3, ng)


def _gdn_finish_bwd(lay, o_f, o_r, p3, ng, dy):
    rows = _finish_chunks(lay) * GC

    def body(of_ref, or_ref, z_ref, ng_ref, dy_ref, do_ref, dz_ref, dng_ref):
        first = (pl.program_id(0) == 0) & (pl.program_id(1) == 0)
        _, vjp = jax.vjp(_gdn_finish_fn, _gdn_o(of_ref, or_ref), z_ref[0], ng_ref[...])
        do, dz, dng = vjp(dy_ref[0])
        do_ref[0] = do
        dz_ref[0] = dz.astype(BF16)
        _acc(dng_ref, dng, first)

    blk = pl.BlockSpec((1, rows, BRW), lambda b, i: (b, i, 0))
    return pl.pallas_call(
        body, grid=(lay.b, lay.s // rows), in_specs=_gdn_finish_specs(lay) + [blk],
        out_specs=[blk, blk, pl.BlockSpec((1, BRW), lambda b, i: (0, 0))],
        out_shape=[jax.ShapeDtypeStruct((lay.b, lay.s, BRW), F32), jax.ShapeDtypeStruct((lay.b, lay.s, BRW), BF16),
                   jax.ShapeDtypeStruct((1, BRW), F32)],
        compiler_params=_params(("arbitrary", "arbitrary")), name="gdn_finish_bwd")(o_f, o_r, p3, ng, dy)


def _rope_tables(lay):
    t = jnp.arange(lay.t_lat)
    lane = np.arange(BRW)
    dim = lane % HD
    inv = jnp.asarray(ROPE_BASE ** (-(dim % 16).astype(np.float32) / 16.0), F32)
    pos = jnp.where((dim // 32 == 0)[None, :], (t // GRID_W)[:, None], (t % GRID_W)[:, None]).astype(F32)
    ang = pos * inv[None, :]
    cos = jnp.concatenate([jnp.ones((lay.t_ctx, BRW), F32), jnp.cos(ang)], axis=0)
    sin = jnp.concatenate([jnp.zeros((lay.t_ctx, BRW), F32), jnp.sin(ang)], axis=0)
    return cos, sin


def _pad_rows(a, rows):
    return jnp.concatenate([a, jnp.zeros((rows - a.shape[0],) + a.shape[1:], a.dtype)], axis=0)


def _layer_fwd(lay, xc, wl, cos, sin):
    p, ht = _inproj_fwd(lay, xc, wl["mod3"], wl["gpre"], wl["win"])
    p3 = p.reshape(lay.b, lay.s, W_PAD)
    nctx = lay.t_ctx // RC
    states = _state_scan(lay, _ret_state(lay, p3, cos, sin), nctx, False, "ret_scan")
    y_ret = _ret_out(lay, p3, cos, sin, states, wl["ret_ng"])
    y_sg = _sg_fwd(lay, p3, wl["sgw"], wl["sgb8"])
    y_sc = _sc_fwd(lay, p3, wl["scw8"])
    qkv = _gdn_conv(lay, p3, wl["gdnw8"])
    pack, cd, t_inv = _gdn_prep(lay, qkv, p3, wl["prm"])
    o_f, o_r, st_f, st_r = _gdn_scan(lay, pack, cd)
    y_gdn = _gdn_finish(lay, o_f, o_r, p3, wl["gdn_ng"])
    ys = [y.reshape(lay.rows, BRW) for y in (y_ret, y_sg, y_sc, y_gdn)]
    xc_new, yt = _outproj_fwd(lay, ys, xc, wl["wout"], wl["gpost"], wl["mod3"])
    saved = dict(xc=xc, p3=p3, ht=ht, yt=yt, states=states, qkv=qkv, pack=pack, cd=cd, t_inv=t_inv, o_f=o_f, o_r=o_r, gstates=(st_f, st_r),
                 ys=ys)
    return xc_new, saved


def _layer_bwd(lay, sv, wl, cos, sin, dxc):
    p3 = sv["p3"]
    as3 = lambda a: a.reshape(lay.b, lay.s, a.shape[-1])
    as2 = lambda a: a.reshape(lay.rows, a.shape[-1])
    dy_ret, dy_sg, dy_sc, dy_gdn, do_, dgpost, dgate = _outproj_bwd(
        lay, sv["ys"], sv["xc"], wl["wout"], wl["gpost"], wl["mod3"], dxc)
    dwout = _weight_grad(lay, sv["yt"], do_, "wout_grad")
    nctx = lay.t_ctx // RC
    dpr, dstates, dret_ng = _ret_out_bwd(lay, p3, cos, sin, sv["states"], wl["ret_ng"], as3(dy_ret))
    d_a = _state_scan(lay, dstates, nctx, True, "ret_scan_bwd")
    dp_ret = _ret_state_bwd(lay, p3, cos, sin, d_a, dpr)
    dp_sg, dsgw, dsgb8 = _sg_bwd(lay, p3, wl["sgw"], wl["sgb8"], as3(dy_sg))
    dsb, dsc_, dsh_, dsz, dscw8 = _sc_bwd(lay, p3, wl["scw8"], as3(dy_sc))
    do, dgz, dgdn_ng = _gdn_finish_bwd(lay, sv["o_f"], sv["o_r"], p3, wl["gdn_ng"], as3(dy_gdn))
    dpf, dpr_, dcf, dcr = _gdn_scan_bwd(lay, sv["pack"], sv["cd"], sv["gstates"], do)
    dqkv, da, dprm = _gdn_prep_bwd(lay, sv["qkv"], p3, wl["prm"], (dpf, dpr_), (dcf, dcr), sv["t_inv"])
    dp_gqkv, dgdnw8 = _gdn_conv_bwd(lay, p3, wl["gdnw8"], dqkv)
    pieces = [(as2(dp_ret), 0), (as2(dp_sg), COL_SG * BRW), (as2(dsb), COL_SC * BRW), (as2(dsc_), (COL_SC + 1) * BRW),
              (as2(dsh_), (COL_SC + 2) * BRW), (as2(dsz), (COL_SC + 3) * BRW), (as2(dp_gqkv), COL_GDN * BRW),
              (as2(dgz), (COL_GDN + 3) * BRW), (as2(da), COL_A128 * LANES)]
    dxc_prev, dgpre, dshift, dscale = _inproj_bwd(lay, sv["xc"], wl["mod3"], wl["gpre"], wl["wint"], dxc, pieces)
    dws = [_weight_grad(lay, sv["ht"], dp, "win_grad_%d" % off) for dp, off in pieces]
    dwin = jnp.concatenate(dws[:-1] + [dws[-1][:, :W_IN - COL_A128 * LANES]], axis=1)

    def rows3(g):
        return jnp.concatenate([g[1], g[3], g[0] + g[2]], axis=0)

    dmod = _pad_rows(jnp.concatenate([rows3(dshift), rows3(dscale), rows3(dgate)], axis=1), 8)
    grads = dict(win=dwin, wout=dwout, gpre=dgpre[0], gpost=dgpost[0], ret_ng=dret_ng[0], sgw=dsgw, sgb=dsgb8[:NH],
                 scw=dscw8[:3], gdnw=dgdnw8[:3], alog=dprm[0, :2 * NH].reshape(2, NH),
                 dtb=dprm[1, :2 * NH].reshape(2, NH), gdn_ng=dgdn_ng.reshape(NH, HD).sum(axis=0), dmod=dmod)
    return dxc_prev, grads


def _local_step(x, c, ctx, c_ctx, first, later, token, bmod, gpre, gpost, ret_ng, sgw, sgb, scw, gdnw, alog, dtb,
                gdn_ng, target):
    depth = bmod.shape[0]
    lay = _Lay(x.shape[0], ctx.shape[1], x.shape[1])
    assert lay.b == 2 and lay.t_ctx % RC == 0 and lay.t_lat % RC == 0
    cos, sin = _rope_tables(lay)
    cvec8 = _pad_rows(jnp.concatenate([c, c_ctx[None]], axis=0), 8) + token[0, 0]
    wmod = first[0]
    mod = _mod_fwd(cvec8, wmod, bmod[:1, None, :])
    xc = jnp.concatenate([ctx, x], axis=1).reshape(lay.rows, D)
    layers, saved = [], []
    for l in range(depth):
        if l == 1:
            rest = later(xc)
            wmod = jnp.concatenate([first[0], rest[0]], axis=0)
            mod = jnp.concatenate([mod, _mod_fwd(cvec8, rest[0], bmod[1:, None, :])], axis=0)
        win, wout = (first[1][0], first[2][0]) if l == 0 else (rest[1][l - 1], rest[2][l - 1])
        wl = dict(mod3=mod[l].reshape(8, 3, D).transpose(1, 0, 2)[:, :, None, :], gpre=gpre[l][None], gpost=gpost[l][None],
                  win=win, wint=jnp.swapaxes(win, 0, 1), wout=wout, ret_ng=ret_ng[l][None], sgw=sgw[l],
                  sgb8=_pad_rows(sgb[l], 8), scw8=_pad_rows(scw[l], 8), gdnw8=_pad_rows(gdnw[l], 8),
                  prm=_pad_rows(jnp.pad(jnp.stack([alog[l].reshape(-1), dtb[l].reshape(-1)]),
                                        ((0, 0), (0, LANES - 2 * NH))), 8),
                  gdn_ng=jnp.tile(gdn_ng[l], NH)[None])
        xc, sv = _layer_fwd(lay, xc, wl, cos, sin)
        layers.append(wl)
        saved.append(sv)
    loss, dxc3 = _loss_kernel(lay, xc.reshape(lay.b, lay.s, D), target)
    dxc = dxc3.reshape(lay.rows, D)
    grads = [None] * depth
    for l in reversed(range(depth)):
        dxc, grads[l] = _layer_bwd(lay, saved[l], layers[l], cos, sin, dxc)
    stacked = {k: jnp.stack([g[k] for g in grads]) for k in grads[0]}
    dcvec8, dbmod = _mod_bwd(stacked["dmod"], wmod, cvec8)
    stacked["bmod"] = dbmod[:, 0, :]
    stacked["c_ctx"] = dcvec8[2]
    dx = dxc.reshape(lay.b, lay.s, D)[:, lay.t_ctx:, :]
    return loss, dx, stacked, cvec8


MESH = pl.DeviceIdType.MESH
ANY = pl.BlockSpec(memory_space=pl.ANY)


def _me():
    return lax.axis_index("x"), lax.axis_index("y"), lax.axis_index("c")


def _gather_weights(shards, fulls, blocks):
    n = len(shards)

    def body(*refs):
        ins, outs = refs[:n], refs[n:2 * n]
        send_sems, recv_sems, loc_sems = refs[2 * n:]
        x, y, c = _me()
        me, sibling = (x, y, c), (x, y, 1 - c)
        chips = [(1 - x, y), (x, 1 - y), (1 - x, 1 - y)]

        def blk(a, dev):
            return blocks[a](outs[a], 4 * dev[0] + 2 * dev[1] + dev[2])

        def copy(a, k, block, to, src=None):
            return pltpu.make_async_remote_copy(
                src_ref=blk(a, block) if src is None else src, dst_ref=blk(a, block), send_sem=send_sems.at[a, k],
                recv_sem=recv_sems.at[a, k], device_id=to, device_id_type=MESH)

        mine = [pltpu.make_async_copy(ins[a], blk(a, me), loc_sems.at[a]) for a in range(n)]
        for cp in mine:
            cp.start()
        first = []
        for a in range(n):
            first.append(copy(a, 0, me, sibling, src=ins[a]))
            first += [copy(a, 1 + j, me, (*chip, c), src=ins[a]) for j, chip in enumerate(chips)]
        for cp in first:
            cp.start()
        passed = []
        for j, chip in enumerate(chips):
            for a in range(n):
                copy(a, 1 + j, (*chip, c), me).wait_recv()
                fwd = copy(a, 4 + j, (*chip, c), sibling)
                fwd.start()
                passed.append(fwd)
        for a in range(n):
            copy(a, 0, sibling, me).wait_recv()
            for j, chip in enumerate(chips):
                copy(a, 4 + j, (*chip, 1 - c), me).wait_recv()
        for cp in first + passed:
            cp.wait_send()
        for cp in mine:
            cp.wait()

    return pl.pallas_call(
        body, in_specs=[ANY] * n, out_specs=[ANY] * n,
        out_shape=[jax.ShapeDtypeStruct(f, s.dtype) for f, s in zip(fulls, shards)],
        scratch_shapes=[pltpu.SemaphoreType.DMA((n, 7)), pltpu.SemaphoreType.DMA((n, 7)),
                        pltpu.SemaphoreType.DMA((n,))],
        name="gather_weights")(*shards)


HBM = pl.BlockSpec(memory_space=pltpu.HBM)
SEM = pl.BlockSpec(memory_space=pltpu.SEMAPHORE)


def _peer(k, x, y, c):
    return (1 - x if k & 4 else x, 1 - y if k & 2 else y, 1 - c if k & 1 else c)


def _gather_start(shards, lands, blocks):
    n = len(shards)

    def body(*refs):
        ins, land = refs[:n], refs[n:2 * n]
        send_sems, recv_sems = refs[2 * n], refs[2 * n + 1]
        token = refs[-1]
        x, y, c = _me()
        me = 4 * x + 2 * y + c
        for a in range(n):
            for k in range(1, N_DEV):
                pltpu.make_async_remote_copy(
                    src_ref=ins[a], dst_ref=blocks[a](land[a], me), send_sem=send_sems.at[7 * a + k - 1],
                    recv_sem=recv_sems.at[7 * a + k - 1], device_id=_peer(k, x, y, c), device_id_type=MESH).start()
        token[...] = jnp.zeros_like(token)

    args = [pltpu.with_memory_space_constraint(a, pltpu.HBM) for a in list(shards) + list(lands)]
    out = pl.pallas_call(
        body, name="gather_start",
        out_shape=[pltpu.SemaphoreType.DMA((7 * n,)), pltpu.SemaphoreType.DMA((7 * n,))]
        + [pltpu.HBM(a.shape, a.dtype) for a in args] + [jax.ShapeDtypeStruct((8, LANES), F32)],
        in_specs=[HBM] * (2 * n), out_specs=[SEM, SEM] + [HBM] * (2 * n) + [pl.BlockSpec(memory_space=pltpu.VMEM)],
        input_output_aliases={i: 2 + i for i in range(2 * n)},
        compiler_params=pltpu.CompilerParams(has_side_effects=pltpu.SideEffectType.DATAFLOW_SIDE_EFFECTING),
    )(*args)
    return out[0], out[1], out[2:2 + n], out[2 + n:2 + 2 * n], out[-1]


def _gather_wait(started, after, blocks):
    send_sems, recv_sems, shards, lands, _ = started
    n = len(shards)

    def body(*refs):
        ins, land = refs[:n], refs[n:2 * n]
        send_sems, recv_sems = refs[2 * n], refs[2 * n + 1]
        x, y, c = _me()
        for a in range(n):
            for k in range(1, N_DEV):
                px, py, pc = _peer(k, x, y, c)
                cp = pltpu.make_async_remote_copy(
                    src_ref=ins[a], dst_ref=blocks[a](land[a], 4 * px + 2 * py + pc), send_sem=send_sems.at[7 * a + k - 1],
                    recv_sem=recv_sems.at[7 * a + k - 1], device_id=(px, py, pc), device_id_type=MESH)
                cp.wait_send()
                cp.wait_recv()

    out = pl.pallas_call(
        body, name="gather_wait",
        out_shape=[pltpu.HBM(a.shape, a.dtype) for a in list(shards) + list(lands)],
        in_specs=[HBM] * (2 * n) + [SEM, SEM, ANY], out_specs=[HBM] * (2 * n),
        input_output_aliases={i: i for i in range(2 * n)},
        compiler_params=pltpu.CompilerParams(has_side_effects=pltpu.SideEffectType.DATAFLOW_SIDE_EFFECTING),
    )(*shards, *lands, send_sems, recv_sems, after)
    return out[n:]


def _scatter_pair(srcs, slabs, slab_shapes):
    n = len(srcs)

    def body(*refs):
        ins, outs = refs[:n], refs[n:2 * n]
        send_sems, recv_sems = refs[2 * n:]
        x, y, c = _me()
        cps = []
        for a in range(n):
            for q in range(4):
                j = 2 * q + (1 - c)
                cps.append(pltpu.make_async_remote_copy(
                    src_ref=slabs[a](ins[a], j), dst_ref=outs[a].at[q], send_sem=send_sems.at[a, q],
                    recv_sem=recv_sems.at[a, q], device_id=(x, y, 1 - c), device_id_type=MESH))
        for cp in cps:
            cp.start()
        for cp in cps:
            cp.wait_recv()
        for cp in cps:
            cp.wait_send()

    return pl.pallas_call(
        body, in_specs=[ANY] * n, out_specs=[ANY] * n,
        out_shape=[jax.ShapeDtypeStruct((4,) + tuple(shp), s.dtype) for shp, s in zip(slab_shapes, srcs)],
        scratch_shapes=[pltpu.SemaphoreType.DMA((n, 4)), pltpu.SemaphoreType.DMA((n, 4))],
        name="scatter_pair")(*srcs)


def _scatter_chips(parts, small):
    n = len(parts)

    def body(*refs):
        ins, small_ref = refs[:n], refs[n]
        outs, all_ref = refs[n + 1:2 * n + 1], refs[2 * n + 1]
        send_sems, recv_sems, g_send, g_recv, loc_sem = refs[2 * n + 2:]
        x, y, c = _me()
        me = 4 * x + 2 * y + c
        chips = [(1 - x, y), (x, 1 - y), (1 - x, 1 - y)]
        cps = []
        for a in range(n):
            for k, (px, py) in enumerate(chips):
                cps.append(pltpu.make_async_remote_copy(
                    src_ref=ins[a].at[2 * px + py], dst_ref=outs[a].at[k], send_sem=send_sems.at[a, k],
                    recv_sem=recv_sems.at[a, k], device_id=(px, py, c), device_id_type=MESH))

        def gather(k, dst_blk, peer_xyz):
            return pltpu.make_async_remote_copy(
                src_ref=small_ref, dst_ref=all_ref.at[dst_blk], send_sem=g_send.at[k], recv_sem=g_recv.at[k],
                device_id=peer_xyz, device_id_type=MESH)

        local = pltpu.make_async_copy(small_ref, all_ref.at[me], loc_sem)
        local.start()
        peers = []
        for k in range(1, N_DEV):
            px = 1 - x if k & 4 else x
            py = 1 - y if k & 2 else y
            pc = 1 - c if k & 1 else c
            peers.append((4 * px + 2 * py + pc, (px, py, pc)))
        sends = [gather(k, me, xyz) for k, (_, xyz) in enumerate(peers)]
        for cp in sends + cps:
            cp.start()
        for k, (peer, xyz) in enumerate(peers):
            gather(k, peer, xyz).wait_recv()
        for cp in cps:
            cp.wait_recv()
        for cp in sends + cps:
            cp.wait_send()
        local.wait()

    return pl.pallas_call(
        body, in_specs=[ANY] * (n + 1), out_specs=[ANY] * (n + 1),
        out_shape=[jax.ShapeDtypeStruct((3,) + p.shape[1:], p.dtype) for p in parts]
        + [jax.ShapeDtypeStruct((N_DEV,) + small.shape, small.dtype)],
        scratch_shapes=[pltpu.SemaphoreType.DMA((n, 3)), pltpu.SemaphoreType.DMA((n, 3)),
                        pltpu.SemaphoreType.DMA((N_DEV - 1,)), pltpu.SemaphoreType.DMA((N_DEV - 1,)),
                        pltpu.SemaphoreType.DMA(())],
        name="scatter_chips")(*parts, small)


def _add_rows(arrs, out_dtype, name):
    shp = arrs[0].shape
    two = [a.reshape(-1, shp[-1]) for a in arrs]
    rows, cols = two[0].shape
    tr = _row_tile(rows, 1024)

    def body(*refs):
        acc = refs[0][...].astype(F32)
        for r in refs[1:-1]:
            acc = acc + r[...].astype(F32)
        refs[-1][...] = acc.astype(out_dtype)

    blk = pl.BlockSpec((tr, cols), lambda i: (i, 0))
    return pl.pallas_call(
        body, grid=(rows // tr,), in_specs=[blk] * len(two), out_specs=blk,
        out_shape=jax.ShapeDtypeStruct((rows, cols), out_dtype),
        compiler_params=_params(("arbitrary",)), name=name)(*two).reshape(shp)


def _row_tile(rows, cap):
    best = 8
    for t in range(8, min(rows, cap) + 1, 8):
        if rows % t == 0:
            best = t
    return best


def _sum_devices(x):
    _, rows, cols = x.shape
    tr = _row_tile(rows, 2048)

    def body(x_ref, o_ref):
        acc = x_ref[0]
        for j in range(1, N_DEV):
            acc = acc + x_ref[j]
        o_ref[...] = acc

    return pl.pallas_call(
        body, grid=(rows // tr,), in_specs=[pl.BlockSpec((N_DEV, tr, cols), lambda i: (0, i, 0))],
        out_specs=pl.BlockSpec((tr, cols), lambda i: (i, 0)), out_shape=jax.ShapeDtypeStruct((rows, cols), F32),
        compiler_params=_params(("arbitrary",)), name="sum_devices")(x)


def _adamw(w, g, m, v, name):
    rows, cols = w.shape
    tr = _row_tile(rows, 512)
    bc1 = 1.0 - ADAM_B1 ** ADAM_STEP
    bc2 = 1.0 - ADAM_B2 ** ADAM_STEP

    def body(w_ref, g_ref, m_ref, v_ref, d_ref, nm_ref, nv_ref):
        g_ = g_ref[...]
        m_ = ADAM_B1 * m_ref[...] + (1.0 - ADAM_B1) * g_
        v_ = ADAM_B2 * v_ref[...] + (1.0 - ADAM_B2) * (g_ * g_)
        d_ref[...] = -ADAM_LR * ((m_ / bc1) / (jnp.sqrt(v_ / bc2) + ADAM_EPS) + ADAM_WD * w_ref[...])
        nm_ref[...] = m_
        nv_ref[...] = v_

    blk = pl.BlockSpec((tr, cols), lambda i: (i, 0))
    return pl.pallas_call(
        body, grid=(rows // tr,), in_specs=[blk] * 4, out_specs=[blk] * 3,
        out_shape=[jax.ShapeDtypeStruct((rows, cols), F32)] * 3,
        compiler_params=_params(("arbitrary",)), name=name)(w, g, m, v)


def _pack_rows(shape):
    return -(-int(np.prod(shape)) // (16 * LANES)) * 16


def _pack(arrs, dtype=F32):
    blocks = []
    for a in arrs:
        flat = a.reshape(-1).astype(dtype)
        rows = _pack_rows(a.shape)
        blocks.append(jnp.pad(flat, (0, rows * LANES - flat.shape[0])).reshape(rows, LANES))
    return jnp.concatenate(blocks, axis=0)


def _unpack(packed, shapes):
    out, off = [], 0
    for s in shapes:
        rows = _pack_rows(s)
        out.append(packed[off:off + rows].reshape(-1)[:int(np.prod(s))].reshape(s))
        off += rows
    return out


SMALL = ("c_ctx", "b_mod", "g_pre", "g_post", "ret_norm_g", "sg_w", "sg_b", "sc_conv_w", "gdn_conv_w", "gdn_a_log",
         "gdn_dt_bias", "gdn_norm_g")
ORDER = ("c_ctx", "w_mod", "b_mod", "g_pre", "g_post", "w_in", "w_out", "ret_norm_g", "sg_w", "sg_b", "sc_conv_w",
         "gdn_conv_w", "gdn_a_log", "gdn_dt_bias", "gdn_norm_g")


def kernel(x, c, ctx, c_ctx, w_mod, b_mod, g_pre, g_post, w_in, w_out, ret_norm_g, sg_w, sg_b, sc_conv_w, gdn_conv_w, gdn_a_log, gdn_dt_bias, gdn_norm_g, loss_target, m_c_ctx, m_w_mod, m_b_mod, m_g_pre, m_g_post, m_w_in, m_w_out, m_ret_norm_g, m_sg_w, m_sg_b, m_sc_conv_w, m_gdn_conv_w, m_gdn_a_log, m_gdn_dt_bias, m_gdn_norm_g, v_c_ctx, v_w_mod, v_b_mod, v_g_pre, v_g_post, v_w_in, v_w_out, v_ret_norm_g, v_sg_w, v_sg_b, v_sc_conv_w, v_gdn_conv_w, v_gdn_a_log, v_gdn_dt_bias, v_gdn_norm_g):
    wts = dict(c_ctx=c_ctx, w_mod=w_mod, b_mod=b_mod, g_pre=g_pre, g_post=g_post, w_in=w_in, w_out=w_out,
               ret_norm_g=ret_norm_g, sg_w=sg_w, sg_b=sg_b, sc_conv_w=sc_conv_w, gdn_conv_w=gdn_conv_w,
               gdn_a_log=gdn_a_log, gdn_dt_bias=gdn_dt_bias, gdn_norm_g=gdn_norm_g)
    mom = dict(c_ctx=m_c_ctx, w_mod=m_w_mod, b_mod=m_b_mod, g_pre=m_g_pre, g_post=m_g_post, w_in=m_w_in, w_out=m_w_out,
               ret_norm_g=m_ret_norm_g, sg_w=m_sg_w, sg_b=m_sg_b, sc_conv_w=m_sc_conv_w, gdn_conv_w=m_gdn_conv_w,
               gdn_a_log=m_gdn_a_log, gdn_dt_bias=m_gdn_dt_bias, gdn_norm_g=m_gdn_norm_g)
    var = dict(c_ctx=v_c_ctx, w_mod=v_w_mod, b_mod=v_b_mod, g_pre=v_g_pre, g_post=v_g_post, w_in=v_w_in, w_out=v_w_out,
               ret_norm_g=v_ret_norm_g, sg_w=v_sg_w, sg_b=v_sg_b, sc_conv_w=v_sc_conv_w, gdn_conv_w=v_gdn_conv_w,
               gdn_a_log=v_gdn_a_log, gdn_dt_bias=v_gdn_dt_bias, gdn_norm_g=v_gdn_norm_g)
    depth = w_mod.shape[0]
    n_mod, n_in, n_out = w_mod.shape[2], w_in.shape[2], w_out.shape[1]
    n_sc, n_gdn = sc_conv_w.shape[2], gdn_conv_w.shape[2]
    xi, yi, ci = _me()
    me = 4 * xi + 2 * yi + ci

    conv = _pack([sc_conv_w, gdn_conv_w])
    n_conv = depth * 3 * n_sc
    rest = depth - 1
    blocks = [lambda r, j: r.at[:, :, pl.ds(pl.multiple_of(j * n_mod, LANES), n_mod)],
              lambda r, j: r.at[j],
              lambda r, j: r.at[:, pl.ds(pl.multiple_of(j * n_out, 16), n_out), :],
              lambda r, j: r.at[j]]

    def in_place(g):
        return jnp.pad(g.transpose(1, 2, 0, 3).reshape(g.shape[1], D, N_DEV * n_in),
                       ((0, 0), (0, 0), (0, W_PAD - N_DEV * n_in)))

    wmod_0, win_g, wout_0, conv_g = _gather_weights(
        [w_mod[:1].astype(BF16), w_in[:1].astype(BF16), w_out[:1].astype(BF16), conv],
        [(1, D, N_DEV * n_mod), (N_DEV, 1, D, n_in), (1, N_DEV * n_out, D), (N_DEV,) + conv.shape], blocks)
    later_shards = [w_mod[1:].astype(BF16), w_in[1:].astype(BF16), w_out[1:].astype(BF16)]
    zero = jnp.zeros((), jnp.int32)
    lands = [lax.dynamic_update_slice(lax.empty((rest, D, N_DEV * n_mod), BF16), later_shards[0],
                                      (zero, zero, me * n_mod)),
             lax.dynamic_update_slice(lax.empty((N_DEV, rest, D, n_in), BF16), later_shards[1][None],
                                      (me, zero, zero, zero)),
             lax.dynamic_update_slice(lax.empty((rest, N_DEV * n_out, D), BF16), later_shards[2],
                                      (zero, me * n_out, zero))]
    started = _gather_start(later_shards, lands, blocks[:3])

    def later(stream):
        wmod_r, win_r, wout_r = _gather_wait(started, stream, blocks[:3])
        return wmod_r, in_place(win_r), wout_r

    r_sc = _pack_rows(sc_conv_w.shape)
    scw_f = conv_g[:, :r_sc].reshape(N_DEV, -1)[:, :n_conv]
    scw_f = scw_f.reshape(N_DEV, depth, 3, n_sc).transpose(1, 2, 0, 3).reshape(depth, 3, -1)
    gdnw_f = conv_g[:, r_sc:].reshape(N_DEV, -1)[:, :depth * 3 * n_gdn]
    gdnw_f = gdnw_f.reshape(N_DEV, depth, 3, n_gdn).transpose(1, 2, 0, 3)
    gdnw_f = gdnw_f.reshape(depth, 3, -1)

    loss8, dx, g, cvec8 = _local_step(x, c, ctx, c_ctx, (wmod_0, in_place(win_g), wout_0), later, started[4], b_mod,
                                      g_pre, g_post, ret_norm_g, sg_w, sg_b, scw_f, gdnw_f, gdn_a_log, gdn_dt_bias,
                                      gdn_norm_g, loss_target)

    gin = g["win"].astype(BF16).reshape(depth, D, N_DEV, n_in).transpose(2, 0, 1, 3)
    gout = g["wout"].astype(BF16)
    slabs = [lambda r, j: r.at[j], lambda r, j: r.at[:, pl.ds(pl.multiple_of(j * n_out, 16), n_out), :]]
    got_in, got_out = _scatter_pair([gin, gout], slabs, [(depth, D, n_in), (depth, n_out, D)])
    mine_in = lax.dynamic_index_in_dim(gin.reshape(4, 2, depth, D, n_in), ci, axis=1, keepdims=False)
    mine_out = lax.dynamic_index_in_dim(gout.reshape(depth, 4, 2, n_out, D), ci, axis=2, keepdims=False)
    mine_out = mine_out.transpose(1, 0, 2, 3)
    local_small = dict(c_ctx=g["c_ctx"], b_mod=g["bmod"], g_pre=g["gpre"], g_post=g["gpost"], ret_norm_g=g["ret_ng"],
                       sg_w=g["sgw"], sg_b=g["sgb"], sc_conv_w=g["scw"], gdn_conv_w=g["gdnw"], gdn_a_log=g["alog"],
                       gdn_dt_bias=g["dtb"], gdn_norm_g=g["gdn_ng"])
    to_sum = _pack([loss8[0, :1]] + [local_small[k] for k in SMALL])
    rows_sum = to_sum.shape[0]
    as_is = _pack([cvec8[:3], g["dmod"][:, :3, :]])
    far_in, far_out, everyone = _scatter_chips([_add_rows([mine_in, got_in], BF16, "pair_sum_in"),
                                                _add_rows([mine_out, got_out], BF16, "pair_sum_out")],
                                               jnp.concatenate([to_sum, as_is], axis=0))
    chip = 2 * xi + yi
    own = lambda a: lax.dynamic_index_in_dim(a, chip, axis=0, keepdims=False)
    grad = dict(w_in=_add_rows([own(mine_in), own(got_in), far_in[0], far_in[1], far_in[2]], F32, "grad_sum_in"),
                w_out=_add_rows([own(mine_out), own(got_out), far_out[0], far_out[1], far_out[2]], F32,
                                "grad_sum_out"))

    small_sum = _unpack(_sum_devices(everyone[:, :rows_sum]), [(1,)] + [local_small[k].shape for k in SMALL])
    loss = small_sum[0][0]
    for k, val in zip(SMALL, small_sum[1:]):
        grad[k] = val
    grad["sc_conv_w"] = lax.dynamic_slice_in_dim(grad["sc_conv_w"], me * n_sc, n_sc, axis=2)
    grad["gdn_conv_w"] = lax.dynamic_slice_in_dim(grad["gdn_conv_w"], me * n_gdn, n_gdn, axis=2)
    r_c = _pack_rows((3, D))
    c_all = everyone[:, rows_sum:rows_sum + r_c].reshape(N_DEV, -1)[:, :3 * D].reshape(N_DEV * 3, D)
    dmod_all = everyone[:, rows_sum + r_c:].reshape(N_DEV, -1)[:, :depth * 9 * D]
    dmod_all = dmod_all.reshape(N_DEV, depth, 3, 3 * D).transpose(1, 0, 2, 3)
    dmod_mine = lax.dynamic_slice_in_dim(dmod_all.reshape(depth, N_DEV * 3, 3 * D), me * n_mod, n_mod, axis=2)
    grad["w_mod"] = _wmod_grad(_pad_rows(c_all, 32), jnp.pad(dmod_mine, ((0, 0), (0, 32 - N_DEV * 3), (0, 0))))

    delta, new_m, new_v = {}, {}, {}
    for k in ("w_mod", "w_in", "w_out"):
        shp = wts[k].shape
        two = lambda a: a.reshape(-1, shp[-1])
        res = _adamw(two(wts[k]), two(grad[k]), two(mom[k]), two(var[k]), "adamw_" + k)
        delta[k], new_m[k], new_v[k] = [r.reshape(shp) for r in res]
    res = _adamw(*[_pack([d[k] for k in SMALL]) for d in (wts, grad, mom, var)], "adamw_small")
    for dst, flat in zip((delta, new_m, new_v), res):
        for k, val in zip(SMALL, _unpack(flat, [wts[k].shape for k in SMALL])):
            dst[k] = val
    return (loss, dx, *[grad[k] for k in ORDER], *[delta[k] for k in ORDER], *[new_m[k] for k in ORDER],
            *[new_v[k] for k in ORDER])
```

```python
import functools
import math

import jax
import jax.numpy as jnp
import numpy as np
from jax import lax
from jax.experimental import pallas as pl
from jax.experimental.pallas import tpu as pltpu

F32, BF16 = jnp.float32, jnp.bfloat16
HI = lax.Precision.HIGHEST

N_DEV = 8
D = 1024
DEPTH = 4
BRW = 256
HD = 64
NH = 4
LANES = 128
GRID_W = 64
ROPE_BASE = 10000.0
W_IN = 15 * BRW + 4 * NH
W_PAD = 31 * LANES
RC = 128
GC = 64
EPS = 1e-6
LOG_GAMMA = tuple(math.log(1.0 - 2.0 ** (-5.0 - h)) for h in range(NH))
ADAM_LR, ADAM_B1, ADAM_B2, ADAM_EPS, ADAM_WD, ADAM_STEP = 0.001, 0.9, 0.999, 1e-08, 0.01, 10
VMEM_LIMIT = 56 * 1024 * 1024

COL_RET, COL_SG, COL_SC, COL_GDN = 0, 4, 7, 11
COL_A128 = 30


def _params(sem):
    return pltpu.CompilerParams(dimension_semantics=sem, vmem_limit_bytes=VMEM_LIMIT)


def _bdot(a, b, ca, cb):
    if a.ndim == 3:
        dn = (((ca + 1,), (cb + 1,)), ((0,), (0,)))
    else:
        dn = (((ca,), (cb,)), ((), ()))
    return lax.dot_general(a.astype(BF16), b.astype(BF16), dn, preferred_element_type=F32)


@jax.custom_vjp
def mm(a, b):
    return _bdot(a, b, 1, 0)


mm.defvjp(lambda a, b: (_bdot(a, b, 1, 0), (a, b)),
          lambda r, g: (_bdot(g, r[1], 1, 1), _bdot(r[0], g, 0, 0)))


@jax.custom_vjp
def mm_nt(a, b):
    return _bdot(a, b, 1, 1)


mm_nt.defvjp(lambda a, b: (_bdot(a, b, 1, 1), (a, b)),
             lambda r, g: (_bdot(g, r[1], 1, 0), _bdot(g, r[0], 0, 0)))


@jax.custom_vjp
def mm_tn(a, b):
    return _bdot(a, b, 0, 0)


mm_tn.defvjp(lambda a, b: (_bdot(a, b, 0, 0), (a, b)),
             lambda r, g: (_bdot(r[1], g, 1, 1), _bdot(r[0], g, 1, 0)))


def _dotf(a, b):
    return jnp.dot(a, b, precision=HI, preferred_element_type=F32)


def _iota(shape, dim):
    return lax.broadcasted_iota(jnp.int32, shape, dim)


def _head_mask(h, width=BRW):
    return (_iota((1, width), 1) // HD == h).astype(F32)


def _lane_by_head(vals, width=BRW, lane0=0):
    head = (_iota((1, width), 1) + lane0) // HD
    out = jnp.full((1, width), vals[NH - 1], F32)
    for h in range(NH - 2, -1, -1):
        out = jnp.where(head == h, vals[h], out)
    return out


def _block_diag(n, width):
    return (_iota((n, width), 0) // HD == _iota((n, width), 1) // HD).astype(F32)


@jax.custom_vjp
def _head_sum(x):
    w = x.shape[1]
    ones = _block_diag(w, w).astype(BF16)
    hi = x.astype(BF16)
    lo = (x - hi.astype(F32)).astype(BF16)
    return jnp.dot(hi, ones, preferred_element_type=F32) + jnp.dot(lo, ones, preferred_element_type=F32)


_head_sum.defvjp(lambda x: (_head_sum(x), None), lambda _, g: (_head_sum(g),))


def _silu(x):
    return x * jax.nn.sigmoid(x)


def _stack_heads(x):
    return jnp.concatenate([x * _head_mask(h) for h in range(NH)], axis=-2)


@jax.custom_vjp
def _unstack_heads(x):
    n = x.shape[-2] // NH
    return (x[..., 0:n, :] + x[..., n:2 * n, :]) + (x[..., 2 * n:3 * n, :] + x[..., 3 * n:4 * n, :])


_unstack_heads.defvjp(lambda x: (_unstack_heads(x), None), lambda _, g: (_stack_heads(g),))


@jax.custom_vjp
def _rot_half(x):
    n = x.shape[1]
    first = (_iota(x.shape, 1) % 32) < 16
    return jnp.where(first, -pltpu.roll(x, n - 16, 1), pltpu.roll(x, 16, 1))


_rot_half.defvjp(lambda x: (_rot_half(x), None), lambda _, g: (-_rot_half(g),))


def _rotary(x, cos, sin):
    return x * cos + _rot_half(x) * sin


def _make_shifts(seq, t_ctx):
    def dn_raw(x):
        r = _iota(x.shape, 0)
        return jnp.where((r == 0) | (r == t_ctx), 0.0, pltpu.roll(x, 1, 0))

    def up_raw(x):
        r = _iota(x.shape, 0)
        return jnp.where((r == t_ctx - 1) | (r == seq - 1), 0.0, pltpu.roll(x, seq - 1, 0))

    @jax.custom_vjp
    def dn(x):
        return dn_raw(x)

    @jax.custom_vjp
    def up(x):
        return up_raw(x)

    dn.defvjp(lambda x: (dn_raw(x), None), lambda _, g: (up_raw(g),))
    up.defvjp(lambda x: (up_raw(x), None), lambda _, g: (dn_raw(g),))
    return dn, up


def _conv3(t, w0, w1, w2, dn, up):
    return dn(t) * w0 + t * w1 + up(t) * w2


def _acc(ref, val, first, at=()):
    idx = at + (Ellipsis,)

    @pl.when(first)
    def _():
        ref[idx] = val

    @pl.when(jnp.logical_not(first))
    def _():
        ref[idx] += val


def _mod_fwd(cvec8, wmod, bmod):
    depth = wmod.shape[0]

    def body(c_ref, w_ref, b_ref, o_ref):
        sc = _silu(c_ref[...])
        o_ref[0] = jnp.dot(sc.astype(BF16), w_ref[0], preferred_element_type=F32) + b_ref[0]

    return pl.pallas_call(
        body, grid=(depth, 3),
        in_specs=[pl.BlockSpec((8, D), lambda l, j: (0, 0)),
                  pl.BlockSpec((1, D, D), lambda l, j: (l, 0, j)),
                  pl.BlockSpec((1, 1, D), lambda l, j: (l, 0, j))],
        out_specs=pl.BlockSpec((1, 8, D), lambda l, j: (l, 0, j)),
        out_shape=jax.ShapeDtypeStruct((depth, 8, 3 * D), F32),
        compiler_params=_params(("arbitrary", "arbitrary")), name="mod_fwd")(cvec8, wmod, bmod)


def _mod_bwd(dmod, wmod, cvec8):
    depth = wmod.shape[0]

    def body(dm_ref, w_ref, c_ref, dc_ref, db_ref):
        l, j = pl.program_id(0), pl.program_id(1)
        dm = dm_ref[0]
        db_ref[0] = jnp.sum(dm, axis=0, keepdims=True)
        part = _bdot(dm, w_ref[0], 1, 1)
        _acc(dc_ref, part, (l == 0) & (j == 0))

        @pl.when((l == depth - 1) & (j == 2))
        def _():
            c = c_ref[...]
            s = jax.nn.sigmoid(c)
            dc_ref[...] = dc_ref[...] * (s * (1.0 + c * (1.0 - s)))

    return pl.pallas_call(
        body, grid=(depth, 3),
        in_specs=[pl.BlockSpec((1, 8, D), lambda l, j: (l, 0, j)),
                  pl.BlockSpec((1, D, D), lambda l, j: (l, 0, j)),
                  pl.BlockSpec((8, D), lambda l, j: (0, 0))],
        out_specs=[pl.BlockSpec((8, D), lambda l, j: (0, 0)),
                   pl.BlockSpec((1, 1, D), lambda l, j: (l, 0, j))],
        out_shape=[jax.ShapeDtypeStruct((8, D), F32), jax.ShapeDtypeStruct((depth, 1, 3 * D), F32)],
        compiler_params=_params(("arbitrary", "arbitrary")), name="mod_bwd")(dmod, wmod, cvec8)


def _wmod_grad(c_rows, dmod_cols):
    depth, rows, n = dmod_cols.shape

    def body(c_ref, dm_ref, o_ref):
        sc = _silu(c_ref[...])
        o_ref[0] = lax.dot_general(sc, dm_ref[0], (((0,), (0,)), ((), ())), precision=HI,
                                   preferred_element_type=F32)

    return pl.pallas_call(
        body, grid=(depth,),
        in_specs=[pl.BlockSpec((rows, D), lambda l: (0, 0)), pl.BlockSpec((1, rows, n), lambda l: (l, 0, 0))],
        out_specs=pl.BlockSpec((1, D, n), lambda l: (l, 0, 0)),
        out_shape=jax.ShapeDtypeStruct((depth, D, n), F32),
        compiler_params=_params(("arbitrary",)), name="wmod_grad")(c_rows, dmod_cols)


class _Lay:
    def __init__(self, batch, t_ctx, t_lat):
        self.b, self.t_ctx, self.t_lat = batch, t_ctx, t_lat
        self.s = t_ctx + t_lat
        self.tm = min(256, t_ctx)
        self.tpb = self.s // self.tm
        self.nct = t_ctx // self.tm
        self.ntiles = batch * self.tpb
        self.rows = batch * self.s

    def mod_row(self, i):
        return jnp.where(i % self.tpb < self.nct, 2, i // self.tpb)

    def group(self, i):
        return 2 * (i // self.tpb) + jnp.where(i % self.tpb < self.nct, 0, 1)

    def group_first(self, i):
        return (i % self.tpb == 0) | (i % self.tpb == self.nct)


def _norm_mod(x, g, shift, scale):
    r = lax.rsqrt(jnp.mean(x * x, axis=-1, keepdims=True) + EPS)
    return (x * r * g) * (1.0 + scale) + shift


def _inproj_fwd(lay, xc, mod3, gpre, w):
    tm = lay.tm

    def body(x_ref, sh_ref, sc_ref, g_ref, w_ref, p_ref, ht_ref):
        h = _norm_mod(x_ref[...], g_ref[...], sh_ref[0, 0], sc_ref[0, 0])
        ht_ref[...] = h.T.astype(BF16)
        p_ref[...] = jnp.dot(h.astype(BF16), w_ref[...], preferred_element_type=F32)

    return pl.pallas_call(
        body, grid=(lay.ntiles,),
        in_specs=[pl.BlockSpec((tm, D), lambda i: (i, 0)),
                  pl.BlockSpec((1, 1, 1, D), lambda i: (0, lay.mod_row(i), 0, 0)),
                  pl.BlockSpec((1, 1, 1, D), lambda i: (1, lay.mod_row(i), 0, 0)),
                  pl.BlockSpec((1, D), lambda i: (0, 0)),
                  pl.BlockSpec((D, W_PAD), lambda i: (0, 0))],
        out_specs=[pl.BlockSpec((tm, W_PAD), lambda i: (i, 0)), pl.BlockSpec((D, tm), lambda i: (0, i))],
        out_shape=[jax.ShapeDtypeStruct((lay.rows, W_PAD), F32), jax.ShapeDtypeStruct((D, lay.rows), BF16)],
        compiler_params=_params(("arbitrary",)), name="inproj_fwd")(xc, mod3, mod3, gpre, w)


def _inproj_bwd(lay, xc, mod3, gpre, wt, dxc, pieces):
    tm = lay.tm
    npc = len(pieces)
    offs = [off for _, off in pieces]

    def body(*refs):
        x_ref, sh_ref, sc_ref, g_ref, wt_ref, dx_in = refs[:6]
        dps = refs[6:6 + npc]
        dx_ref, dg_ref, dsh_ref, dsc_ref = refs[6 + npc:]
        i = pl.program_id(0)
        dh = None
        for dp_ref, off in zip(dps, offs):
            wd = dp_ref.shape[1]
            part = jnp.dot(dp_ref[...], wt_ref[off:off + wd, :], preferred_element_type=F32)
            dh = part if dh is None else dh + part
        _, vjp = jax.vjp(_norm_mod, x_ref[...], g_ref[...], sh_ref[0, 0], sc_ref[0, 0])
        dx, dg, dsh, dsc = vjp(dh)
        dx_ref[...] = dx_in[...] + dx
        _acc(dg_ref, dg, i == 0)
        first = lay.group_first(i)
        _acc(dsh_ref, dsh, first, at=(0,))
        _acc(dsc_ref, dsc, first, at=(0,))

    return pl.pallas_call(
        body, grid=(lay.ntiles,),
        in_specs=[pl.BlockSpec((tm, D), lambda i: (i, 0)),
                  pl.BlockSpec((1, 1, 1, D), lambda i: (0, lay.mod_row(i), 0, 0)),
                  pl.BlockSpec((1, 1, 1, D), lambda i: (1, lay.mod_row(i), 0, 0)),
                  pl.BlockSpec((1, D), lambda i: (0, 0)),
                  pl.BlockSpec((W_PAD, D), lambda i: (0, 0)),
                  pl.BlockSpec((tm, D), lambda i: (i, 0))]
        + [pl.BlockSpec((tm, dp.shape[1]), lambda i: (i, 0)) for dp, _ in pieces],
        out_specs=[pl.BlockSpec((tm, D), lambda i: (i, 0)),
                   pl.BlockSpec((1, D), lambda i: (0, 0)),
                   pl.BlockSpec((1, 1, D), lambda i: (lay.group(i), 0, 0)),
                   pl.BlockSpec((1, 1, D), lambda i: (lay.group(i), 0, 0))],
        out_shape=[jax.ShapeDtypeStruct((lay.rows, D), F32), jax.ShapeDtypeStruct((1, D), F32),
                   jax.ShapeDtypeStruct((2 * lay.b, 1, D), F32), jax.ShapeDtypeStruct((2 * lay.b, 1, D), F32)],
        compiler_params=_params(("arbitrary",)), name="inproj_bwd",
    )(xc, mod3, mod3, gpre, wt, dxc, *[dp for dp, _ in pieces])


def _weight_grad(lay, ht, dp, name):
    wd = dp.shape[1]
    tn = 512 if wd % 512 == 0 else (256 if wd % 256 == 0 else LANES)
    tr = lay.rows // 3 if lay.rows % (3 * 256) == 0 else lay.tm

    def body(ht_ref, dp_ref, o_ref):
        _acc(o_ref, jnp.dot(ht_ref[...], dp_ref[...], preferred_element_type=F32), pl.program_id(1) == 0)

    return pl.pallas_call(
        body, grid=(wd // tn, lay.rows // tr),
        in_specs=[pl.BlockSpec((D, tr), lambda j, i: (0, i)), pl.BlockSpec((tr, tn), lambda j, i: (i, j))],
        out_specs=pl.BlockSpec((D, tn), lambda j, i: (0, j)),
        out_shape=jax.ShapeDtypeStruct((D, wd), F32),
        compiler_params=_params(("arbitrary", "arbitrary")), name=name)(ht, dp)


def _outproj_post(o, x, gpost, gate):
    r = lax.rsqrt(jnp.mean(o * o, axis=-1, keepdims=True) + EPS)
    return x + gate * (o * r * gpost)


def _outproj_matmul(ys, w_ref):
    o = None
    for k, y in enumerate(ys):
        part = jnp.dot(y[...], w_ref[BRW * k:BRW * (k + 1), :], preferred_element_type=F32)
        o = part if o is None else o + part
    return o


def _outproj_specs(lay):
    tm = lay.tm
    return ([pl.BlockSpec((tm, BRW), lambda i: (i, 0))] * 4
            + [pl.BlockSpec((tm, D), lambda i: (i, 0))]
            + [pl.BlockSpec((D, D), lambda i: (0, 0))]
            + [pl.BlockSpec((1, D), lambda i: (0, 0))]
            + [pl.BlockSpec((1, 1, 1, D), lambda i: (2, lay.mod_row(i), 0, 0))])


def _outproj_fwd(lay, ys, xc, wout, gpost, mod3):
    tm = lay.tm

    def body(y0, y1, y2, y3, x_ref, w_ref, g_ref, gt_ref, o_ref, yt_ref):
        ys_ = (y0, y1, y2, y3)
        o_ref[...] = _outproj_post(_outproj_matmul(ys_, w_ref), x_ref[...], g_ref[...], gt_ref[0, 0])
        for k, y in enumerate(ys_):
            yt_ref[BRW * k:BRW * (k + 1), :] = y[...].astype(F32).T.astype(BF16)

    return pl.pallas_call(
        body, grid=(lay.ntiles,), in_specs=_outproj_specs(lay),
        out_specs=[pl.BlockSpec((tm, D), lambda i: (i, 0)), pl.BlockSpec((D, tm), lambda i: (0, i))],
        out_shape=[jax.ShapeDtypeStruct((lay.rows, D), F32), jax.ShapeDtypeStruct((D, lay.rows), BF16)],
        compiler_params=_params(("arbitrary",)), name="outproj_fwd")(*ys, xc, wout, gpost, mod3)


def _outproj_bwd(lay, ys, xc, wout, gpost, mod3, dxc):
    tm = lay.tm

    def body(y0, y1, y2, y3, x_ref, w_ref, g_ref, gt_ref, dx_ref, d0, d1, d2, d3, do_ref, dg_ref, dgt_ref):
        i = pl.program_id(0)
        o = _outproj_matmul((y0, y1, y2, y3), w_ref)
        _, vjp = jax.vjp(_outproj_post, o, x_ref[...], g_ref[...], gt_ref[0, 0])
        do, _, dg, dgt = vjp(dx_ref[...])
        do = do.astype(BF16)
        do_ref[...] = do
        for k, d in enumerate((d0, d1, d2, d3)):
            d[...] = _bdot(do, w_ref[BRW * k:BRW * (k + 1), :], 1, 1)
        _acc(dg_ref, dg, i == 0)
        _acc(dgt_ref, dgt, lay.group_first(i), at=(0,))

    return pl.pallas_call(
        body, grid=(lay.ntiles,),
        in_specs=_outproj_specs(lay) + [pl.BlockSpec((tm, D), lambda i: (i, 0))],
        out_specs=[pl.BlockSpec((tm, BRW), lambda i: (i, 0))] * 4
        + [pl.BlockSpec((tm, D), lambda i: (i, 0)), pl.BlockSpec((1, D), lambda i: (0, 0)),
           pl.BlockSpec((1, 1, D), lambda i: (lay.group(i), 0, 0))],
        out_shape=[jax.ShapeDtypeStruct((lay.rows, BRW), F32)] * 4
        + [jax.ShapeDtypeStruct((lay.rows, D), BF16), jax.ShapeDtypeStruct((1, D), F32),
           jax.ShapeDtypeStruct((2 * lay.b, 1, D), F32)],
        compiler_params=_params(("arbitrary",)), name="outproj_bwd")(*ys, xc, wout, gpost, mod3, dxc)


def _loss_kernel(lay, xc3, target):
    tm, nct = lay.tm, lay.nct

    def body(x_ref, t_ref, loss_ref, dx_ref):
        b, i = pl.program_id(0), pl.program_id(1)
        lat = i >= nct
        err = x_ref[0] - t_ref[0]
        dx_ref[0] = jnp.where(lat, err * (1.0 / D), 0.0)
        part = jnp.sum(jnp.sum(err * err, axis=1, keepdims=True), axis=0, keepdims=True) * (0.5 / D)
        part = jnp.broadcast_to(jnp.where(lat, part, 0.0), (8, LANES))
        _acc(loss_ref, part, (b == 0) & (i == 0))

    return pl.pallas_call(
        body, grid=(lay.b, lay.tpb),
        in_specs=[pl.BlockSpec((1, tm, D), lambda b, i: (b, i, 0)),
                  pl.BlockSpec((1, tm, D), lambda b, i: (b, jnp.maximum(i - nct, 0), 0))],
        out_specs=[pl.BlockSpec((8, LANES), lambda b, i: (0, 0)), pl.BlockSpec((1, tm, D), lambda b, i: (b, i, 0))],
        out_shape=[jax.ShapeDtypeStruct((8, LANES), F32), jax.ShapeDtypeStruct(xc3.shape, F32)],
        compiler_params=_params(("arbitrary", "arbitrary")), name="loss")(xc3, target)


def _chunk_orders(n_ctx, n_all):
    fwd = list(range(n_all))
    rev = list(range(n_ctx - 1, -1, -1)) + list(range(n_all - 1, n_ctx - 1, -1))
    return fwd, rev


def _ret_state_fn(k, v, cos, sin):
    kt = _rotary(k, cos, sin) * (HD ** -0.5)
    lg = _lane_by_head(LOG_GAMMA)
    j = _iota((RC, 1), 0).astype(F32)
    bd = _block_diag(BRW, BRW)
    af = mm_tn(kt * jnp.exp((RC - 1.0 - j) * lg), v) * bd
    ar = mm_tn(kt * jnp.exp(j * lg), v) * bd
    return af, ar


def _ret_out_fn(q, k, v, z, cos, sin, sf, sr, ng):
    qt = _rotary(q, cos, sin)
    kt = _rotary(k, cos, sin) * (HD ** -0.5)
    diff = (_iota((RC, RC), 0) - _iota((RC, RC), 1)).astype(F32)
    o = None
    for h in range(NH):
        m = _head_mask(h)
        sc = mm_nt(qt * m, kt)
        wgt = sc * jnp.exp(jnp.abs(diff) * LOG_GAMMA[h]) * jnp.where(diff == 0, 2.0, 1.0)
        part = mm(wgt, v * m)
        o = part if o is None else o + part
    lg = _lane_by_head(LOG_GAMMA)
    i = _iota((RC, 1), 0).astype(F32)
    o = o + mm(qt, sf) * jnp.exp((i + 1.0) * lg) + mm(qt, sr) * jnp.exp((RC - i) * lg)
    mu = _head_sum(o) * (1.0 / HD)
    cen = o - mu
    var = _head_sum(cen * cen) * (1.0 / HD)
    return cen * lax.rsqrt(var + EPS) * ng * _silu(z)


def _ret_chunks(lay):
    return 2 if (lay.s // RC) % 2 == 0 else 1


def _ret_specs(lay, cols):
    rows = _ret_chunks(lay) * RC
    return [pl.BlockSpec((1, rows, BRW), functools.partial(lambda b, i, c: (b, i, c), c=COL_RET + c)) for c in cols]


def _ret_state(lay, p3, cos, sin):
    nc, per = lay.s // RC, _ret_chunks(lay)

    def body(k_ref, v_ref, c_ref, s_ref, a_ref):
        for n in range(per):
            rows = pl.ds(RC * n, RC)
            af, ar = _ret_state_fn(k_ref[0, rows, :], v_ref[0, rows, :], c_ref[rows, :], s_ref[rows, :])
            a_ref[0, n, 0] = af
            a_ref[0, n, 1] = ar

    tab = pl.BlockSpec((per * RC, BRW), lambda b, i: (i, 0))
    return pl.pallas_call(
        body, grid=(lay.b, nc // per), in_specs=_ret_specs(lay, (1, 2)) + [tab, tab],
        out_specs=pl.BlockSpec((1, per, 2, BRW, BRW), lambda b, i: (b, i, 0, 0, 0)),
        out_shape=jax.ShapeDtypeStruct((lay.b, nc, 2, BRW, BRW), F32),
        compiler_params=_params(("arbitrary", "arbitrary")), name="ret_state")(p3, p3, cos, sin)


def _ret_state_bwd(lay, p3, cos, sin, d_a, dpr):
    nc, per = lay.s // RC, _ret_chunks(lay)

    def body(k_ref, v_ref, c_ref, s_ref, da_ref, dpr_ref, o_ref):
        for n in range(per):
            rows = pl.ds(RC * n, RC)
            cos_, sin_ = c_ref[rows, :], s_ref[rows, :]
            _, vjp = jax.vjp(lambda k, v: _ret_state_fn(k, v, cos_, sin_), k_ref[0, rows, :], v_ref[0, rows, :])
            dk, dv = vjp((da_ref[0, n, 0], da_ref[0, n, 1]))
            o_ref[0, rows, 0:BRW] = dpr_ref[0, rows, 0:BRW].astype(BF16)
            o_ref[0, rows, BRW:2 * BRW] = (dpr_ref[0, rows, BRW:2 * BRW] + dk).astype(BF16)
            o_ref[0, rows, 2 * BRW:3 * BRW] = (dpr_ref[0, rows, 2 * BRW:3 * BRW] + dv).astype(BF16)
            o_ref[0, rows, 3 * BRW:] = dpr_ref[0, rows, 3 * BRW:].astype(BF16)

    tab = pl.BlockSpec((per * RC, BRW), lambda b, i: (i, 0))
    return pl.pallas_call(
        body, grid=(lay.b, nc // per),
        in_specs=_ret_specs(lay, (1, 2)) + [tab, tab,
                                            pl.BlockSpec((1, per, 2, BRW, BRW), lambda b, i: (b, i, 0, 0, 0)),
                                            pl.BlockSpec((1, per * RC, 4 * BRW), lambda b, i: (b, i, 0))],
        out_specs=pl.BlockSpec((1, per * RC, 4 * BRW), lambda b, i: (b, i, 0)),
        out_shape=jax.ShapeDtypeStruct((lay.b, lay.s, 4 * BRW), BF16),
        compiler_params=_params(("arbitrary", "arbitrary")), name="ret_state_bwd")(p3, p3, cos, sin, d_a, dpr)


def _state_scan(lay, a, nc_ctx, transpose, name):
    b, nc = a.shape[0], a.shape[1]
    orders = _chunk_orders(nc_ctx, nc)

    def body(a_ref, o_ref):
        d, jh = pl.program_id(1), pl.program_id(2)
        head = (_iota((1, LANES), 1) + jh * LANES) // HD
        lg = jnp.full((1, LANES), LOG_GAMMA[NH - 1], F32)
        for h in range(NH - 2, -1, -1):
            lg = jnp.where(head == h, LOG_GAMMA[h], lg)
        dec = jnp.exp(RC * lg)
        for dd in (0, 1):
            @pl.when(d == dd)
            def _(order=orders[dd]):
                acc = jnp.zeros((BRW, LANES), F32)
                if not transpose:
                    for c in order:
                        o_ref[0, c, 0] = acc
                        acc = acc * dec + a_ref[0, c, 0]
                else:
                    for c in reversed(order):
                        o_ref[0, c, 0] = acc
                        acc = a_ref[0, c, 0] + acc * dec

    spec = pl.BlockSpec((1, nc, 1, BRW, LANES), lambda bb, d, jh: (bb, 0, d, 0, jh))
    return pl.pallas_call(
        body, grid=(b, 2, BRW // LANES), in_specs=[spec], out_specs=spec,
        out_shape=jax.ShapeDtypeStruct(a.shape, F32),
        compiler_params=_params(("arbitrary",) * 3), name=name)(a)


def _ret_out(lay, p3, cos, sin, states, ng):
    nc, per = lay.s // RC, _ret_chunks(lay)

    def body(q_ref, k_ref, v_ref, z_ref, c_ref, s_ref, st_ref, ng_ref, y_ref):
        for n in range(per):
            rows = pl.ds(RC * n, RC)
            y = _ret_out_fn(q_ref[0, rows, :], k_ref[0, rows, :], v_ref[0, rows, :], z_ref[0, rows, :],
                            c_ref[rows, :], s_ref[rows, :], st_ref[0, n, 0], st_ref[0, n, 1], ng_ref[...])
            y_ref[0, rows, :] = y.astype(BF16)

    tab = pl.BlockSpec((per * RC, BRW), lambda b, i: (i, 0))
    return pl.pallas_call(
        body, grid=(lay.b, nc // per),
        in_specs=_ret_specs(lay, (0, 1, 2, 3)) + [tab, tab,
                                                  pl.BlockSpec((1, per, 2, BRW, BRW), lambda b, i: (b, i, 0, 0, 0)),
                                                  pl.BlockSpec((1, BRW), lambda b, i: (0, 0))],
        out_specs=pl.BlockSpec((1, per * RC, BRW), lambda b, i: (b, i, 0)),
        out_shape=jax.ShapeDtypeStruct((lay.b, lay.s, BRW), BF16),
        compiler_params=_params(("arbitrary", "arbitrary")), name="ret_out")(p3, p3, p3, p3, cos, sin, states, ng)


def _ret_out_bwd(lay, p3, cos, sin, states, ng, dy):
    nc, per = lay.s // RC, _ret_chunks(lay)

    def body(q_ref, k_ref, v_ref, z_ref, c_ref, s_ref, st_ref, ng_ref, dy_ref, dp_ref, dst_ref, dng_ref):
        b, i = pl.program_id(0), pl.program_id(1)
        dng_sum = None
        for n in range(per):
            rows = pl.ds(RC * n, RC)
            cos_, sin_ = c_ref[rows, :], s_ref[rows, :]
            fn = lambda q, k, v, z, sf, sr, ng: _ret_out_fn(q, k, v, z, cos_, sin_, sf, sr, ng)
            _, vjp = jax.vjp(fn, q_ref[0, rows, :], k_ref[0, rows, :], v_ref[0, rows, :], z_ref[0, rows, :],
                             st_ref[0, n, 0], st_ref[0, n, 1], ng_ref[...])
            dq, dk, dv, dz, dsf, dsr, dng = vjp(dy_ref[0, rows, :])
            for m, g in enumerate((dq, dk, dv, dz)):
                dp_ref[0, rows, BRW * m:BRW * (m + 1)] = g
            dst_ref[0, n, 0] = dsf
            dst_ref[0, n, 1] = dsr
            dng_sum = dng if dng_sum is None else dng_sum + dng
        _acc(dng_ref, dng_sum, (b == 0) & (i == 0))

    tab = pl.BlockSpec((per * RC, BRW), lambda b, i: (i, 0))
    st = pl.BlockSpec((1, per, 2, BRW, BRW), lambda b, i: (b, i, 0, 0, 0))
    return pl.pallas_call(
        body, grid=(lay.b, nc // per),
        in_specs=_ret_specs(lay, (0, 1, 2, 3)) + [tab, tab, st, pl.BlockSpec((1, BRW), lambda b, i: (0, 0)),
                                                  pl.BlockSpec((1, per * RC, BRW), lambda b, i: (b, i, 0))],
        out_specs=[pl.BlockSpec((1, per * RC, 4 * BRW), lambda b, i: (b, i, 0)), st,
                   pl.BlockSpec((1, BRW), lambda b, i: (0, 0))],
        out_shape=[jax.ShapeDtypeStruct((lay.b, lay.s, 4 * BRW), F32),
                   jax.ShapeDtypeStruct(states.shape, F32), jax.ShapeDtypeStruct((1, BRW), F32)],
        compiler_params=_params(("arbitrary", "arbitrary")), name="ret_out_bwd",
    )(p3, p3, p3, p3, cos, sin, states, ng, dy)


def _sg_fn(u, v, z, w, b8):
    ug = jax.nn.gelu(u)
    vg = jax.nn.gelu(v)
    mu = jnp.mean(vg, axis=-1, keepdims=True)
    cen = vg - mu
    vn = cen * lax.rsqrt(jnp.mean(cen * cen, axis=-1, keepdims=True) + EPS)
    masks = (_iota((NH, 1, BRW), 2) // HD == _iota((NH, 1, BRW), 0)).astype(F32)
    s = jnp.sum(mm(w, vn[None] * masks), axis=0)
    expand = (_iota((8, BRW), 1) // HD == _iota((8, BRW), 0)).astype(F32)
    bias = lax.dot_general(b8, expand, (((0,), (0,)), ((), ())), precision=HI, preferred_element_type=F32)
    return ug * (s + bias) * _silu(z)


def _sg_chunks(lay):
    return 2 if (lay.s // RC) % 2 == 0 else 1


def _sg_specs(lay):
    rows = _sg_chunks(lay) * RC
    return ([pl.BlockSpec((1, rows, BRW), functools.partial(lambda b, i, c: (b, i, c), c=COL_SG + c)) for c in range(3)]
            + [pl.BlockSpec((NH, RC, RC), lambda b, i: (0, 0, 0)), pl.BlockSpec((8, RC), lambda b, i: (0, 0))])


def _sg_fwd(lay, p3, sgw, sgb8):
    per = _sg_chunks(lay)

    def body(u_ref, v_ref, z_ref, w_ref, b_ref, y_ref):
        for k in range(per):
            rows = pl.ds(RC * k, RC)
            y = _sg_fn(u_ref[0, rows, :], v_ref[0, rows, :], z_ref[0, rows, :], w_ref[...], b_ref[...])
            y_ref[0, rows, :] = y.astype(BF16)

    return pl.pallas_call(
        body, grid=(lay.b, lay.s // (per * RC)), in_specs=_sg_specs(lay),
        out_specs=pl.BlockSpec((1, per * RC, BRW), lambda b, i: (b, i, 0)),
        out_shape=jax.ShapeDtypeStruct((lay.b, lay.s, BRW), BF16),
        compiler_params=_params(("arbitrary", "arbitrary")), name="sg_fwd")(p3, p3, p3, sgw, sgb8)


def _sg_bwd(lay, p3, sgw, sgb8, dy):
    per = _sg_chunks(lay)

    def body(u_ref, v_ref, z_ref, w_ref, b_ref, dy_ref, dp_ref, dw_ref, db_ref):
        first = (pl.program_id(0) == 0) & (pl.program_id(1) == 0)
        dw = db = None
        for k in range(per):
            rows = pl.ds(RC * k, RC)
            _, vjp = jax.vjp(_sg_fn, u_ref[0, rows, :], v_ref[0, rows, :], z_ref[0, rows, :], w_ref[...], b_ref[...])
            g = vjp(dy_ref[0, rows, :])
            for n in range(3):
                dp_ref[0, rows, BRW * n:BRW * (n + 1)] = g[n].astype(BF16)
            dw = g[3] if dw is None else dw + g[3]
            db = g[4] if db is None else db + g[4]
        _acc(dw_ref, dw, first)
        _acc(db_ref, db, first)

    return pl.pallas_call(
        body, grid=(lay.b, lay.s // (per * RC)),
        in_specs=_sg_specs(lay) + [pl.BlockSpec((1, per * RC, BRW), lambda b, i: (b, i, 0))],
        out_specs=[pl.BlockSpec((1, per * RC, 3 * BRW), lambda b, i: (b, i, 0)),
                   pl.BlockSpec((NH, RC, RC), lambda b, i: (0, 0, 0)), pl.BlockSpec((8, RC), lambda b, i: (0, 0))],
        out_shape=[jax.ShapeDtypeStruct((lay.b, lay.s, 3 * BRW), BF16),
                   jax.ShapeDtypeStruct((NH, RC, RC), F32), jax.ShapeDtypeStruct((8, RC), F32)],
        compiler_params=_params(("arbitrary", "arbitrary")), name="sg_bwd")(p3, p3, p3, sgw, sgb8, dy)


def _sc_specs(lay):
    first = COL_SC * BRW // LANES
    blk = [pl.BlockSpec((1, lay.s, LANES), functools.partial(lambda j, b, c: (b, 0, c + j), c=first + 2 * n))
           for n in range(4)]
    return blk + [pl.BlockSpec((8, LANES), lambda j, b: (0, j))]


def _sc_fwd(lay, p3, w8):
    dn, up = _make_shifts(lay.s, lay.t_ctx)

    def fn(b_, c_, h_, z_, w0, w1, w2):
        return b_ * _conv3(c_ * h_, w0, w1, w2, dn, up) * _silu(z_)

    def body(b_ref, c_ref, h_ref, z_ref, w_ref, y_ref):
        y = fn(b_ref[0], c_ref[0], h_ref[0], z_ref[0], w_ref[0:1, :], w_ref[1:2, :], w_ref[2:3, :])
        y_ref[0] = y.astype(BF16)

    return pl.pallas_call(
        body, grid=(BRW // LANES, lay.b), in_specs=_sc_specs(lay),
        out_specs=pl.BlockSpec((1, lay.s, LANES), lambda j, b: (b, 0, j)),
        out_shape=jax.ShapeDtypeStruct((lay.b, lay.s, BRW), BF16),
        compiler_params=_params(("arbitrary", "arbitrary")), name="sc_fwd")(p3, p3, p3, p3, w8)


def _sc_bwd(lay, p3, w8, dy):
    dn, up = _make_shifts(lay.s, lay.t_ctx)

    def fn(b_, c_, h_, z_, w0, w1, w2):
        return b_ * _conv3(c_ * h_, w0, w1, w2, dn, up) * _silu(z_)

    def body(b_ref, c_ref, h_ref, z_ref, w_ref, dy_ref, db_ref, dc_ref, dh_ref, dz_ref, dw_ref):
        _, vjp = jax.vjp(fn, b_ref[0], c_ref[0], h_ref[0], z_ref[0], w_ref[0:1, :], w_ref[1:2, :], w_ref[2:3, :])
        g = vjp(dy_ref[0])
        for ref, val in zip((db_ref, dc_ref, dh_ref, dz_ref), g[:4]):
            ref[0] = val.astype(BF16)
        dw = jnp.concatenate([g[4], g[5], g[6], jnp.zeros((5, LANES), F32)], axis=0)
        _acc(dw_ref, dw, pl.program_id(1) == 0)

    out = pl.BlockSpec((1, lay.s, LANES), lambda j, b: (b, 0, j))
    return pl.pallas_call(
        body, grid=(BRW // LANES, lay.b), in_specs=_sc_specs(lay) + [out],
        out_specs=[out] * 4 + [pl.BlockSpec((8, LANES), lambda j, b: (0, j))],
        out_shape=[jax.ShapeDtypeStruct((lay.b, lay.s, BRW), BF16)] * 4 + [jax.ShapeDtypeStruct((8, BRW), F32)],
        compiler_params=_params(("arbitrary", "arbitrary")), name="sc_bwd")(p3, p3, p3, p3, w8, dy)


def _gdn_conv_fn(x, w0, w1, w2, normed, dn, up):
    a = _silu(_conv3(x, w0, w1, w2, dn, up))
    nrm = a * lax.rsqrt(_head_sum(a * a) + EPS)
    return jnp.where(normed, nrm, a)


def _gdn_conv(lay, p3, w8):
    dn, up = _make_shifts(lay.s, lay.t_ctx)
    first = COL_GDN * BRW // LANES

    def body(x_ref, w_ref, o_ref):
        normed = pl.program_id(0) < 2 * BRW // LANES
        o_ref[0] = _gdn_conv_fn(x_ref[0], w_ref[0:1, :], w_ref[1:2, :], w_ref[2:3, :], normed, dn, up)

    return pl.pallas_call(
        body, grid=(3 * BRW // LANES, lay.b),
        in_specs=[pl.BlockSpec((1, lay.s, LANES), lambda j, b: (b, 0, first + j)),
                  pl.BlockSpec((8, LANES), lambda j, b: (0, j))],
        out_specs=pl.BlockSpec((1, lay.s, LANES), lambda j, b: (b, 0, j)),
        out_shape=jax.ShapeDtypeStruct((lay.b, lay.s, 3 * BRW), F32),
        compiler_params=_params(("arbitrary", "arbitrary")), name="gdn_conv")(p3, w8)


def _gdn_conv_bwd(lay, p3, w8, dqkv):
    dn, up = _make_shifts(lay.s, lay.t_ctx)
    first = COL_GDN * BRW // LANES

    def body(x_ref, w_ref, g_ref, dx_ref, dw_ref):
        normed = pl.program_id(0) < 2 * BRW // LANES
        fn = lambda x, w0, w1, w2: _gdn_conv_fn(x, w0, w1, w2, normed, dn, up)
        _, vjp = jax.vjp(fn, x_ref[0], w_ref[0:1, :], w_ref[1:2, :], w_ref[2:3, :])
        g = vjp(g_ref[0])
        dx_ref[0] = g[0].astype(BF16)
        dw = jnp.concatenate([g[1], g[2], g[3], jnp.zeros((5, LANES), F32)], axis=0)
        _acc(dw_ref, dw, pl.program_id(1) == 0)

    blk = pl.BlockSpec((1, lay.s, LANES), lambda j, b: (b, 0, j))
    return pl.pallas_call(
        body, grid=(3 * BRW // LANES, lay.b),
        in_specs=[pl.BlockSpec((1, lay.s, LANES), lambda j, b: (b, 0, first + j)),
                  pl.BlockSpec((8, LANES), lambda j, b: (0, j)), blk],
        out_specs=[blk, pl.BlockSpec((8, LANES), lambda j, b: (0, j))],
        out_shape=[jax.ShapeDtypeStruct((lay.b, lay.s, 3 * BRW), BF16), jax.ShapeDtypeStruct((8, 3 * BRW), F32)],
        compiler_params=_params(("arbitrary", "arbitrary")), name="gdn_conv_bwd")(p3, w8, dqkv)


def _tri_inverse(low):
    i, j = _iota(low.shape, low.ndim - 2), _iota(low.shape, low.ndim - 1) % GC
    t = (i == j).astype(F32)
    s = 1
    while s < GC:
        pair = (i // (2 * s)) == (j // (2 * s))
        off = pair & (((i // s) % 2) != ((j // s) % 2))
        cb = jnp.where(off, low, 0.0)
        t = t - (cb if s == 1 else _bdot(t, _stack_heads(_bdot(cb, _stack_heads(t), 1, 0)), 1, 0))
        s *= 2
    return t


@jax.custom_vjp
def _tri_solve(t, low, r1, r2):
    del low
    return _bdot(t, _stack_heads(r1), 1, 0), _bdot(t, _stack_heads(r2), 1, 0)


def _tri_solve_fwd(t, low, r1, r2):
    del low
    x1, x2 = _bdot(t, _stack_heads(r1), 1, 0), _bdot(t, _stack_heads(r2), 1, 0)
    return (x1, x2), (t, x1, x2)


def _tri_solve_bwd(res, g):
    t, x1, x2 = res
    bd = _block_diag(BRW, BRW)
    d1 = _unstack_heads(_bdot(t, g[0], 0, 0) * bd)
    d2 = _unstack_heads(_bdot(t, g[1], 0, 0) * bd)
    dlow = -(_bdot(d1, _stack_heads(x1), 1, 1) + _bdot(d2, _stack_heads(x2), 1, 1))
    return jnp.zeros_like(t), dlow, d1, d2


_tri_solve.defvjp(_tri_solve_fwd, _tri_solve_bwd)

N_PACK = 5


def _gdn_prep_fn(qn, kn, vv, a, alog, dtb, t=None):
    n = qn.shape[0]
    col = _iota((1, 1, LANES), 2)
    xx = a + dtb
    softplus = jnp.maximum(xx, 0.0) + jnp.log(1.0 + jnp.exp(-jnp.abs(xx)))
    g_small = jnp.where(col < 8, -jnp.exp(alog) * softplus, 0.0).reshape(n * GC, LANES)
    beta_small = jax.nn.sigmoid(a).reshape(n * GC, LANES)
    sel_col, sel_head = _iota((LANES, BRW), 0), _iota((LANES, BRW), 1) // HD
    g_l, b_l = [], []
    for d in (0, 1):
        g_l.append(_dotf(g_small, (sel_col == 4 * d + sel_head).astype(F32)))
        b_l.append(_dotf(beta_small, (sel_col == 8 + 4 * d + sel_head).astype(F32)))
    g_l = jnp.concatenate(g_l, axis=0).reshape(2 * n, GC, BRW)
    b_l = jnp.concatenate(b_l, axis=0).reshape(2 * n, GC, BRW)
    rev = _iota((2 * n, 1, 1), 0) >= n
    fwd = jnp.logical_not(rev)
    ri, ci = _iota((1, GC, GC), 1), _iota((1, GC, GC), 2)
    tri = ((fwd & (ri >= ci)) | (rev & (ri <= ci))).astype(F32)
    gc_l = lax.dot_general(tri, g_l, (((2,), (1,)), ((0,), (0,))), precision=HI,
                           preferred_element_type=F32)
    gtot_l = jnp.sum(g_l, axis=1, keepdims=True)
    i, j = _iota((1, GC, BRW), 1), _iota((1, GC, BRW), 2) % GC
    gc_t = jnp.sum(jnp.where(i == j, gc_l, 0.0), axis=1, keepdims=True)
    incl = (fwd & (i >= j)) | (rev & (i <= j))
    strict = (fwd & (i > j)) | (rev & (i < j))
    decay = jnp.where(incl, jnp.exp(jnp.where(incl, gc_l - gc_t, 0.0)), 0.0)
    kn2 = jnp.concatenate([kn, kn], axis=0)
    vv2 = jnp.concatenate([vv, vv], axis=0)
    qs = jnp.concatenate([qn, qn], axis=0) * (HD ** -0.5)
    kst = _stack_heads(kn2)
    kb = kn2 * b_l
    low = jnp.where(strict, mm_nt(kb, kst) * decay, 0.0)
    eg = jnp.exp(gc_l)
    t_inv = _tri_inverse(low) if t is None else t
    u, w = _tri_solve(t_inv, low, vv2 * b_l, kb * eg)
    k_tail = kn2 * jnp.exp(gtot_l - gc_l)
    intra = mm_nt(qs, kst) * decay
    return (u, w, k_tail, qs * eg, intra), jnp.exp(gtot_l), t_inv


def _prep_chunks(lay):
    return 4 if (lay.s // GC) % 4 == 0 else 2


def _gdn_prep_specs(lay):
    rows = _prep_chunks(lay) * GC
    return ([pl.BlockSpec((1, rows, BRW), functools.partial(lambda b, i, c: (b, i, c), c=c)) for c in range(3)]
            + [pl.BlockSpec((1, rows, LANES), lambda b, i: (b, i, COL_A128)),
               pl.BlockSpec((8, LANES), lambda b, i: (0, 0))])


def _gdn_prep(lay, qkv, p3, prm):
    nc, per = lay.s // GC, _prep_chunks(lay)

    def body(q_ref, k_ref, v_ref, a_ref, prm_ref, pack_ref, cd_ref, t_ref):
        chunks = lambda ref: ref[0].reshape(per, GC, ref.shape[-1])
        pack, cd, t_inv = _gdn_prep_fn(chunks(q_ref), chunks(k_ref), chunks(v_ref), chunks(a_ref),
                                       prm_ref[0:1, :], prm_ref[1:2, :])
        for d in (0, 1):
            for n in range(N_PACK):
                pack_ref[0, :, d, n] = pack[n][per * d:per * (d + 1)]
            cd_ref[0, :, d] = cd[per * d:per * (d + 1)]
            t_ref[0, :, d] = t_inv[per * d:per * (d + 1)]

    return pl.pallas_call(
        body, grid=(lay.b, nc // per), in_specs=_gdn_prep_specs(lay),
        out_specs=[pl.BlockSpec((1, per, 2, N_PACK, GC, BRW), lambda b, i: (b, i, 0, 0, 0, 0)),
                   pl.BlockSpec((1, per, 2, 1, BRW), lambda b, i: (b, i, 0, 0, 0)),
                   pl.BlockSpec((1, per, 2, GC, BRW), lambda b, i: (b, i, 0, 0, 0))],
        out_shape=[jax.ShapeDtypeStruct((lay.b, nc, 2, N_PACK, GC, BRW), F32),
                   jax.ShapeDtypeStruct((lay.b, nc, 2, 1, BRW), F32),
                   jax.ShapeDtypeStruct((lay.b, nc, 2, GC, BRW), F32)],
        compiler_params=_params(("arbitrary", "arbitrary")), name="gdn_prep")(qkv, qkv, qkv, p3, prm)


def _gdn_prep_bwd(lay, qkv, p3, prm, dpacks, dcds, t_inv):
    nc, per = lay.s // GC, _prep_chunks(lay)

    def body(q_ref, k_ref, v_ref, a_ref, prm_ref, dpf_ref, dpr_ref, dcf_ref, dcr_ref, t_ref, dqkv_ref, da_ref,
             dprm_ref):
        first = (pl.program_id(0) == 0) & (pl.program_id(1) == 0)
        chunks = lambda ref: ref[0].reshape(per, GC, ref.shape[-1])
        t_inv = jnp.concatenate([t_ref[0, :, 0], t_ref[0, :, 1]], axis=0)
        fn = lambda q, k, v, a, alog, dtb: _gdn_prep_fn(q, k, v, a, alog, dtb, t_inv)[:2]
        _, vjp = jax.vjp(fn, chunks(q_ref), chunks(k_ref), chunks(v_ref), chunks(a_ref),
                         prm_ref[0:1, :], prm_ref[1:2, :])
        dpack = tuple(jnp.concatenate([dpf_ref[0, :, n], dpr_ref[0, :, n]], axis=0) for n in range(N_PACK))
        dq, dk, dv, da, dalog, ddtb = vjp((dpack, jnp.concatenate([dcf_ref[0], dcr_ref[0]], axis=0)))
        dqkv_ref[0, :, 0:BRW] = dq.reshape(per * GC, BRW)
        dqkv_ref[0, :, BRW:2 * BRW] = dk.reshape(per * GC, BRW)
        dqkv_ref[0, :, 2 * BRW:] = dv.reshape(per * GC, BRW)
        da_ref[0] = da.reshape(per * GC, LANES).astype(BF16)
        _acc(dprm_ref, jnp.concatenate([dalog, ddtb, jnp.zeros((6, LANES), F32)], axis=0), first)

    rows_blk = per * GC
    return pl.pallas_call(
        body, grid=(lay.b, nc // per),
        in_specs=_gdn_prep_specs(lay)
        + [pl.BlockSpec((1, per, N_PACK, GC, BRW), lambda b, i: (b, i, 0, 0, 0))] * 2
        + [pl.BlockSpec((1, per, 1, BRW), lambda b, i: (b, i, 0, 0))] * 2
        + [pl.BlockSpec((1, per, 2, GC, BRW), lambda b, i: (b, i, 0, 0, 0))],
        out_specs=[pl.BlockSpec((1, rows_blk, 3 * BRW), lambda b, i: (b, i, 0)),
                   pl.BlockSpec((1, rows_blk, LANES), lambda b, i: (b, i, 0)),
                   pl.BlockSpec((8, LANES), lambda b, i: (0, 0))],
        out_shape=[jax.ShapeDtypeStruct((lay.b, lay.s, 3 * BRW), F32),
                   jax.ShapeDtypeStruct((lay.b, lay.s, LANES), BF16), jax.ShapeDtypeStruct((8, LANES), F32)],
        compiler_params=_params(("arbitrary", "arbitrary")), name="gdn_prep_bwd",
    )(qkv, qkv, qkv, p3, prm, *dpacks, *dcds, t_inv)


def _gdn_step_fn(s, u, w, k_tail, qd, intra, cdec):
    v_new = u - mm(w, s)
    o = mm(qd, s) + mm(intra, _stack_heads(v_new))
    return s * cdec + mm_tn(k_tail, v_new) * _block_diag(BRW, BRW), o


def _order_index(nc_ctx, nc, d, step):
    rev = jnp.where(step < nc_ctx, nc_ctx - 1 - step, nc + nc_ctx - 1 - step)
    return jnp.where(d == 0, step, rev)


def _gdn_scan(lay, pack, cd):
    nc, nc_ctx = lay.s // GC, lay.t_ctx // GC
    chunk = functools.partial(_order_index, nc_ctx, nc)

    def body(pf_ref, pr_ref, cf_ref, cr_ref, of_ref, or_ref, sf_ref, sr_ref, s_scr):
        @pl.when(pl.program_id(0) == 0)
        def _():
            s_scr[...] = jnp.zeros_like(s_scr)

        nb = lay.b
        s = s_scr[...]
        st = _unstack_heads(s)
        sf_ref[:, 0] = st[:nb]
        sr_ref[:, 0] = st[nb:]
        args = [jnp.concatenate([pf_ref[:, 0, 0, n], pr_ref[:, 0, 0, n]], axis=0) for n in range(N_PACK)]
        s_new, o = _gdn_step_fn(s, *args, jnp.concatenate([cf_ref[:, 0, 0], cr_ref[:, 0, 0]], axis=0))
        of_ref[:, 0] = o[:nb]
        or_ref[:, 0] = o[nb:]
        s_scr[...] = s_new

    def pk(d):
        return pl.BlockSpec((lay.b, 1, 1, N_PACK, GC, BRW), lambda t: (0, chunk(d, t), d, 0, 0, 0))

    def cdb(d):
        return pl.BlockSpec((lay.b, 1, 1, 1, BRW), lambda t: (0, chunk(d, t), d, 0, 0))

    def out(d):
        return pl.BlockSpec((lay.b, 1, GC, BRW), lambda t: (0, chunk(d, t), 0, 0))

    return pl.pallas_call(
        body, grid=(nc,), in_specs=[pk(0), pk(1), cdb(0), cdb(1)],
        out_specs=[out(0), out(1), out(0), out(1)],
        out_shape=[jax.ShapeDtypeStruct((lay.b, nc, GC, BRW), F32)] * 4,
        scratch_shapes=[pltpu.VMEM((2 * lay.b, BRW, BRW), F32)],
        compiler_params=_params(("arbitrary",)), name="gdn_scan")(pack, pack, cd, cd)


def _gdn_scan_bwd(lay, pack, cd, states, do):
    nc, nc_ctx = lay.s // GC, lay.t_ctx // GC

    def chunk(d, t):
        return _order_index(nc_ctx, nc, d, nc - 1 - t)

    def body(pf_ref, pr_ref, cf_ref, cr_ref, sf_ref, sr_ref, dof_ref, dor_ref, dpf_ref, dpr_ref, dcf_ref, dcr_ref,
             ds_scr):
        @pl.when(pl.program_id(0) == 0)
        def _():
            ds_scr[...] = jnp.zeros_like(ds_scr)

        nb = lay.b
        both = lambda f, r: jnp.concatenate([f, r], axis=0)
        args = ([_stack_heads(both(sf_ref[:, 0], sr_ref[:, 0]))]
                + [both(pf_ref[:, 0, 0, n], pr_ref[:, 0, 0, n]) for n in range(N_PACK)]
                + [both(cf_ref[:, 0, 0], cr_ref[:, 0, 0])])
        _, vjp = jax.vjp(_gdn_step_fn, *args)
        g = vjp((ds_scr[...], both(dof_ref[...], dor_ref[...])))
        ds_scr[...] = g[0]
        for n in range(N_PACK):
            dpf_ref[:, 0, n] = g[1 + n][:nb]
            dpr_ref[:, 0, n] = g[1 + n][nb:]
        dcf_ref[:, 0] = g[1 + N_PACK][:nb]
        dcr_ref[:, 0] = g[1 + N_PACK][nb:]

    def pk(d):
        return pl.BlockSpec((lay.b, 1, 1, N_PACK, GC, BRW), lambda t: (0, chunk(d, t), d, 0, 0, 0))

    def cdb(d):
        return pl.BlockSpec((lay.b, 1, 1, 1, BRW), lambda t: (0, chunk(d, t), d, 0, 0))

    def st(d):
        return pl.BlockSpec((lay.b, 1, GC, BRW), lambda t: (0, chunk(d, t), 0, 0))

    def dob(d):
        return pl.BlockSpec((lay.b, GC, BRW), lambda t: (0, chunk(d, t), 0))

    def dpk(d):
        return pl.BlockSpec((lay.b, 1, N_PACK, GC, BRW), lambda t: (0, chunk(d, t), 0, 0, 0))

    def dcb(d):
        return pl.BlockSpec((lay.b, 1, 1, BRW), lambda t: (0, chunk(d, t), 0, 0))

    return pl.pallas_call(
        body, grid=(nc,),
        in_specs=[pk(0), pk(1), cdb(0), cdb(1), st(0), st(1), dob(0), dob(1)],
        out_specs=[dpk(0), dpk(1), dcb(0), dcb(1)],
        out_shape=[jax.ShapeDtypeStruct((lay.b, nc, N_PACK, GC, BRW), F32)] * 2
        + [jax.ShapeDtypeStruct((lay.b, nc, 1, BRW), F32)] * 2,
        scratch_shapes=[pltpu.VMEM((2 * lay.b, BRW, BRW), F32)],
        compiler_params=_params(("arbitrary",)), name="gdn_scan_bwd")(pack, pack, cd, cd, *states, do, do)


def _gdn_finish_fn(o, z, ng):
    return o * lax.rsqrt(_head_sum(o * o) * (1.0 / HD) + EPS) * ng * _silu(z)


def _finish_chunks(lay):
    nc = lay.s // GC
    return 12 if nc % 12 == 0 else (6 if nc % 6 == 0 else 2)


def _gdn_o(of_ref, or_ref):
    return (of_ref[0] + or_ref[0]).reshape(of_ref.shape[1] * GC, BRW)


def _gdn_finish_specs(lay):
    per = _finish_chunks(lay)
    ob = pl.BlockSpec((1, per, GC, BRW), lambda b, i: (b, i, 0, 0))
    return [ob, ob, pl.BlockSpec((1, per * GC, BRW), lambda b, i: (b, i, COL_GDN + 3)),
            pl.BlockSpec((1, BRW), lambda b, i: (0, 0))]


def _gdn_finish(lay, o_f, o_r, p3, ng):
    rows = _finish_chunks(lay) * GC

    def body(of_ref, or_ref, z_ref, ng_ref, y_ref):
        y_ref[0] = _gdn_finish_fn(_gdn_o(of_ref, or_ref), z_ref[0], ng_ref[...]).astype(BF16)

    return pl.pallas_call(
        body, grid=(lay.b, lay.s // rows), in_specs=_gdn_finish_specs(lay),
        out_specs=pl.BlockSpec((1, rows, BRW), lambda b, i: (b, i, 0)),
        out_shape=jax.ShapeDtypeStruct((lay.b, lay.s, BRW), BF16),
        compiler_params=_params(("arbitrary", "arbitrary")), name="gdn_finish")(o_f, o_r, p3, ng)


def _gdn_finish_bwd(lay, o_f, o_r, p3, ng, dy):
    rows = _finish_chunks(lay) * GC

    def body(of_ref, or_ref, z_ref, ng_ref, dy_ref, do_ref, dz_ref, dng_ref):
        first = (pl.program_id(0) == 0) & (pl.program_id(1) == 0)
        _, vjp = jax.vjp(_gdn_finish_fn, _gdn_o(of_ref, or_ref), z_ref[0], ng_ref[...])
        do, dz, dng = vjp(dy_ref[0])
        do_ref[0] = do
        dz_ref[0] = dz.astype(BF16)
        _acc(dng_ref, dng, first)

    blk = pl.BlockSpec((1, rows, BRW), lambda b, i: (b, i, 0))
    return pl.pallas_call(
        body, grid=(lay.b, lay.s // rows), in_specs=_gdn_finish_specs(lay) + [blk],
        out_specs=[blk, blk, pl.BlockSpec((1, BRW), lambda b, i: (0, 0))],
        out_shape=[jax.ShapeDtypeStruct((lay.b, lay.s, BRW), F32), jax.ShapeDtypeStruct((lay.b, lay.s, BRW), BF16),
                   jax.ShapeDtypeStruct((1, BRW), F32)],
        compiler_params=_params(("arbitrary", "arbitrary")), name="gdn_finish_bwd")(o_f, o_r, p3, ng, dy)


def _rope_tables(lay):
    t = jnp.arange(lay.t_lat)
    lane = np.arange(BRW)
    dim = lane % HD
    inv = jnp.asarray(ROPE_BASE ** (-(dim % 16).astype(np.float32) / 16.0), F32)
    pos = jnp.where((dim // 32 == 0)[None, :], (t // GRID_W)[:, None], (t % GRID_W)[:, None]).astype(F32)
    ang = pos * inv[None, :]
    cos = jnp.concatenate([jnp.ones((lay.t_ctx, BRW), F32), jnp.cos(ang)], axis=0)
    sin = jnp.concatenate([jnp.zeros((lay.t_ctx, BRW), F32), jnp.sin(ang)], axis=0)
    return cos, sin


def _pad_rows(a, rows):
    return jnp.concatenate([a, jnp.zeros((rows - a.shape[0],) + a.shape[1:], a.dtype)], axis=0)


def _layer_fwd(lay, xc, wl, cos, sin):
    p, ht = _inproj_fwd(lay, xc, wl["mod3"], wl["gpre"], wl["win"])
    p3 = p.reshape(lay.b, lay.s, W_PAD)
    nctx = lay.t_ctx // RC
    states = _state_scan(lay, _ret_state(lay, p3, cos, sin), nctx, False, "ret_scan")
    y_ret = _ret_out(lay, p3, cos, sin, states, wl["ret_ng"])
    y_sg = _sg_fwd(lay, p3, wl["sgw"], wl["sgb8"])
    y_sc = _sc_fwd(lay, p3, wl["scw8"])
    qkv = _gdn_conv(lay, p3, wl["gdnw8"])
    pack, cd, t_inv = _gdn_prep(lay, qkv, p3, wl["prm"])
    o_f, o_r, st_f, st_r = _gdn_scan(lay, pack, cd)
    y_gdn = _gdn_finish(lay, o_f, o_r, p3, wl["gdn_ng"])
    ys = [y.reshape(lay.rows, BRW) for y in (y_ret, y_sg, y_sc, y_gdn)]
    xc_new, yt = _outproj_fwd(lay, ys, xc, wl["wout"], wl["gpost"], wl["mod3"])
    saved = dict(xc=xc, p3=p3, ht=ht, yt=yt, states=states, qkv=qkv, pack=pack, cd=cd, t_inv=t_inv, o_f=o_f, o_r=o_r, gstates=(st_f, st_r),
                 ys=ys)
    return xc_new, saved


def _layer_bwd(lay, sv, wl, cos, sin, dxc):
    p3 = sv["p3"]
    as3 = lambda a: a.reshape(lay.b, lay.s, a.shape[-1])
    as2 = lambda a: a.reshape(lay.rows, a.shape[-1])
    dy_ret, dy_sg, dy_sc, dy_gdn, do_, dgpost, dgate = _outproj_bwd(
        lay, sv["ys"], sv["xc"], wl["wout"], wl["gpost"], wl["mod3"], dxc)
    dwout = _weight_grad(lay, sv["yt"], do_, "wout_grad")
    nctx = lay.t_ctx // RC
    dpr, dstates, dret_ng = _ret_out_bwd(lay, p3, cos, sin, sv["states"], wl["ret_ng"], as3(dy_ret))
    d_a = _state_scan(lay, dstates, nctx, True, "ret_scan_bwd")
    dp_ret = _ret_state_bwd(lay, p3, cos, sin, d_a, dpr)
    dp_sg, dsgw, dsgb8 = _sg_bwd(lay, p3, wl["sgw"], wl["sgb8"], as3(dy_sg))
    dsb, dsc_, dsh_, dsz, dscw8 = _sc_bwd(lay, p3, wl["scw8"], as3(dy_sc))
    do, dgz, dgdn_ng = _gdn_finish_bwd(lay, sv["o_f"], sv["o_r"], p3, wl["gdn_ng"], as3(dy_gdn))
    dpf, dpr_, dcf, dcr = _gdn_scan_bwd(lay, sv["pack"], sv["cd"], sv["gstates"], do)
    dqkv, da, dprm = _gdn_prep_bwd(lay, sv["qkv"], p3, wl["prm"], (dpf, dpr_), (dcf, dcr), sv["t_inv"])
    dp_gqkv, dgdnw8 = _gdn_conv_bwd(lay, p3, wl["gdnw8"], dqkv)
    pieces = [(as2(dp_ret), 0), (as2(dp_sg), COL_SG * BRW), (as2(dsb), COL_SC * BRW), (as2(dsc_), (COL_SC + 1) * BRW),
              (as2(dsh_), (COL_SC + 2) * BRW), (as2(dsz), (COL_SC + 3) * BRW), (as2(dp_gqkv), COL_GDN * BRW),
              (as2(dgz), (COL_GDN + 3) * BRW), (as2(da), COL_A128 * LANES)]
    dxc_prev, dgpre, dshift, dscale = _inproj_bwd(lay, sv["xc"], wl["mod3"], wl["gpre"], wl["wint"], dxc, pieces)
    dws = [_weight_grad(lay, sv["ht"], dp, "win_grad_%d" % off) for dp, off in pieces]
    dwin = jnp.concatenate(dws[:-1] + [dws[-1][:, :W_IN - COL_A128 * LANES]], axis=1)

    def rows3(g):
        return jnp.concatenate([g[1], g[3], g[0] + g[2]], axis=0)

    dmod = _pad_rows(jnp.concatenate([rows3(dshift), rows3(dscale), rows3(dgate)], axis=1), 8)
    grads = dict(win=dwin, wout=dwout, gpre=dgpre[0], gpost=dgpost[0], ret_ng=dret_ng[0], sgw=dsgw, sgb=dsgb8[:NH],
                 scw=dscw8[:3], gdnw=dgdnw8[:3], alog=dprm[0, :2 * NH].reshape(2, NH),
                 dtb=dprm[1, :2 * NH].reshape(2, NH), gdn_ng=dgdn_ng.reshape(NH, HD).sum(axis=0), dmod=dmod)
    return dxc_prev, grads


def _local_step(x, c, ctx, c_ctx, first, later, token, bmod, gpre, gpost, ret_ng, sgw, sgb, scw, gdnw, alog, dtb,
                gdn_ng, target):
    depth = bmod.shape[0]
    lay = _Lay(x.shape[0], ctx.shape[1], x.shape[1])
    assert lay.b == 2 and lay.t_ctx % RC == 0 and lay.t_lat % RC == 0
    cos, sin = _rope_tables(lay)
    cvec8 = _pad_rows(jnp.concatenate([c, c_ctx[None]], axis=0), 8) + token[0, 0]
    wmod = first[0]
    mod = _mod_fwd(cvec8, wmod, bmod[:1, None, :])
    xc = jnp.concatenate([ctx, x], axis=1).reshape(lay.rows, D)
    layers, saved = [], []
    for l in range(depth):
        if l == 1:
            rest = later(xc)
            wmod = jnp.concatenate([first[0], rest[0]], axis=0)
            mod = jnp.concatenate([mod, _mod_fwd(cvec8, rest[0], bmod[1:, None, :])], axis=0)
        win, wout = (first[1][0], first[2][0]) if l == 0 else (rest[1][l - 1], rest[2][l - 1])
        wl = dict(mod3=mod[l].reshape(8, 3, D).transpose(1, 0, 2)[:, :, None, :], gpre=gpre[l][None], gpost=gpost[l][None],
                  win=win, wint=jnp.swapaxes(win, 0, 1), wout=wout, ret_ng=ret_ng[l][None], sgw=sgw[l],
                  sgb8=_pad_rows(sgb[l], 8), scw8=_pad_rows(scw[l], 8), gdnw8=_pad_rows(gdnw[l], 8),
                  prm=_pad_rows(jnp.pad(jnp.stack([alog[l].reshape(-1), dtb[l].reshape(-1)]),
                                        ((0, 0), (0, LANES - 2 * NH))), 8),
                  gdn_ng=jnp.tile(gdn_ng[l], NH)[None])
        xc, sv = _layer_fwd(lay, xc, wl, cos, sin)
        layers.append(wl)
        saved.append(sv)
    loss, dxc3 = _loss_kernel(lay, xc.reshape(lay.b, lay.s, D), target)
    dxc = dxc3.reshape(lay.rows, D)
    grads = [None] * depth
    for l in reversed(range(depth)):
        dxc, grads[l] = _layer_bwd(lay, saved[l], layers[l], cos, sin, dxc)
    stacked = {k: jnp.stack([g[k] for g in grads]) for k in grads[0] if k not in ("win", "wout")}
    stacked["win"] = [g["win"] for g in grads]
    stacked["wout"] = [g["wout"] for g in grads]
    dcvec8, dbmod = _mod_bwd(stacked["dmod"], wmod, cvec8)
    stacked["bmod"] = dbmod[:, 0, :]
    stacked["c_ctx"] = dcvec8[2]
    dx = dxc.reshape(lay.b, lay.s, D)[:, lay.t_ctx:, :]
    return loss, dx, stacked, cvec8


MESH = pl.DeviceIdType.MESH
ANY = pl.BlockSpec(memory_space=pl.ANY)


def _me():
    return lax.axis_index("x"), lax.axis_index("y"), lax.axis_index("c")


def _gather_weights(shards, fulls, blocks):
    n = len(shards)

    def body(*refs):
        ins, outs = refs[:n], refs[n:2 * n]
        send_sems, recv_sems, loc_sems = refs[2 * n:]
        x, y, c = _me()
        me, sibling = (x, y, c), (x, y, 1 - c)
        chips = [(1 - x, y), (x, 1 - y), (1 - x, 1 - y)]

        def blk(a, dev):
            return blocks[a](outs[a], 4 * dev[0] + 2 * dev[1] + dev[2])

        def copy(a, k, block, to, src=None):
            return pltpu.make_async_remote_copy(
                src_ref=blk(a, block) if src is None else src, dst_ref=blk(a, block), send_sem=send_sems.at[a, k],
                recv_sem=recv_sems.at[a, k], device_id=to, device_id_type=MESH)

        mine = [pltpu.make_async_copy(ins[a], blk(a, me), loc_sems.at[a]) for a in range(n)]
        for cp in mine:
            cp.start()
        first = []
        for a in range(n):
            first.append(copy(a, 0, me, sibling, src=ins[a]))
            first += [copy(a, 1 + j, me, (*chip, c), src=ins[a]) for j, chip in enumerate(chips)]
        for cp in first:
            cp.start()
        passed = []
        for j, chip in enumerate(chips):
            for a in range(n):
                copy(a, 1 + j, (*chip, c), me).wait_recv()
                fwd = copy(a, 4 + j, (*chip, c), sibling)
                fwd.start()
                passed.append(fwd)
        for a in range(n):
            copy(a, 0, sibling, me).wait_recv()
            for j, chip in enumerate(chips):
                copy(a, 4 + j, (*chip, 1 - c), me).wait_recv()
        for cp in first + passed:
            cp.wait_send()
        for cp in mine:
            cp.wait()

    return pl.pallas_call(
        body, in_specs=[ANY] * n, out_specs=[ANY] * n,
        out_shape=[jax.ShapeDtypeStruct(f, s.dtype) for f, s in zip(fulls, shards)],
        scratch_shapes=[pltpu.SemaphoreType.DMA((n, 7)), pltpu.SemaphoreType.DMA((n, 7)),
                        pltpu.SemaphoreType.DMA((n,))],
        name="gather_weights")(*shards)


HBM = pl.BlockSpec(memory_space=pltpu.HBM)
SEM = pl.BlockSpec(memory_space=pltpu.SEMAPHORE)


def _peer(k, x, y, c):
    return (1 - x if k & 4 else x, 1 - y if k & 2 else y, 1 - c if k & 1 else c)


def _whole(ref, j):
    del j
    return ref


def _gather_start(shards, lands, blocks, name, parts=None):
    n = len(shards)
    parts = parts or [_whole] * n

    def body(*refs):
        ins, land = refs[:n], refs[n:2 * n]
        send_sems, recv_sems = refs[2 * n], refs[2 * n + 1]
        token = refs[-1]
        x, y, c = _me()
        me = 4 * x + 2 * y + c
        for a in range(n):
            for k in range(1, N_DEV):
                px, py, pc = _peer(k, x, y, c)
                pltpu.make_async_remote_copy(
                    src_ref=parts[a](ins[a], 4 * px + 2 * py + pc), dst_ref=blocks[a](land[a], me),
                    send_sem=send_sems.at[7 * a + k - 1], recv_sem=recv_sems.at[7 * a + k - 1],
                    device_id=(px, py, pc), device_id_type=MESH).start()
        token[...] = jnp.zeros_like(token)

    args = [pltpu.with_memory_space_constraint(a, pltpu.HBM) for a in list(shards) + list(lands)]
    out = pl.pallas_call(
        body, name=name,
        out_shape=[pltpu.SemaphoreType.DMA((7 * n,)), pltpu.SemaphoreType.DMA((7 * n,))]
        + [pltpu.HBM(a.shape, a.dtype) for a in args] + [jax.ShapeDtypeStruct((8, LANES), F32)],
        in_specs=[HBM] * (2 * n), out_specs=[SEM, SEM] + [HBM] * (2 * n) + [pl.BlockSpec(memory_space=pltpu.VMEM)],
        input_output_aliases={i: 2 + i for i in range(2 * n)},
        compiler_params=pltpu.CompilerParams(has_side_effects=pltpu.SideEffectType.DATAFLOW_SIDE_EFFECTING),
    )(*args)
    return out[0], out[1], out[2:2 + n], out[2 + n:2 + 2 * n], out[-1]


def _gather_wait(started, after, blocks, name, parts=None):
    send_sems, recv_sems, shards, lands, _ = started
    n = len(shards)
    parts = parts or [_whole] * n

    def body(*refs):
        ins, land = refs[:n], refs[n:2 * n]
        send_sems, recv_sems = refs[2 * n], refs[2 * n + 1]
        x, y, c = _me()
        for a in range(n):
            for k in range(1, N_DEV):
                px, py, pc = _peer(k, x, y, c)
                peer = 4 * px + 2 * py + pc
                cp = pltpu.make_async_remote_copy(
                    src_ref=parts[a](ins[a], peer), dst_ref=blocks[a](land[a], peer),
                    send_sem=send_sems.at[7 * a + k - 1], recv_sem=recv_sems.at[7 * a + k - 1],
                    device_id=(px, py, pc), device_id_type=MESH)
                cp.wait_send()
                cp.wait_recv()

    out = pl.pallas_call(
        body, name=name,
        out_shape=[pltpu.HBM(a.shape, a.dtype) for a in list(shards) + list(lands)],
        in_specs=[HBM] * (2 * n) + [SEM, SEM, ANY], out_specs=[HBM] * (2 * n),
        input_output_aliases={i: i for i in range(2 * n)},
        compiler_params=pltpu.CompilerParams(has_side_effects=pltpu.SideEffectType.DATAFLOW_SIDE_EFFECTING),
    )(*shards, *lands, send_sems, recv_sems, after)
    return out[:n], out[n:]


def _scatter_pair(srcs, slabs, slab_shapes):
    n = len(srcs)

    def body(*refs):
        ins, outs = refs[:n], refs[n:2 * n]
        send_sems, recv_sems = refs[2 * n:]
        x, y, c = _me()
        cps = []
        for a in range(n):
            for q in range(4):
                j = 2 * q + (1 - c)
                cps.append(pltpu.make_async_remote_copy(
                    src_ref=slabs[a](ins[a], j), dst_ref=outs[a].at[q], send_sem=send_sems.at[a, q],
                    recv_sem=recv_sems.at[a, q], device_id=(x, y, 1 - c), device_id_type=MESH))
        for cp in cps:
            cp.start()
        for cp in cps:
            cp.wait_recv()
        for cp in cps:
            cp.wait_send()

    return pl.pallas_call(
        body, in_specs=[ANY] * n, out_specs=[ANY] * n,
        out_shape=[jax.ShapeDtypeStruct((4,) + tuple(shp), s.dtype) for shp, s in zip(slab_shapes, srcs)],
        scratch_shapes=[pltpu.SemaphoreType.DMA((n, 4)), pltpu.SemaphoreType.DMA((n, 4))],
        name="scatter_pair")(*srcs)


def _scatter_chips(parts, small):
    n = len(parts)

    def body(*refs):
        ins, small_ref = refs[:n], refs[n]
        outs, all_ref = refs[n + 1:2 * n + 1], refs[2 * n + 1]
        send_sems, recv_sems, g_send, g_recv, loc_sem = refs[2 * n + 2:]
        x, y, c = _me()
        me = 4 * x + 2 * y + c
        chips = [(1 - x, y), (x, 1 - y), (1 - x, 1 - y)]
        cps = []
        for a in range(n):
            for k, (px, py) in enumerate(chips):
                cps.append(pltpu.make_async_remote_copy(
                    src_ref=ins[a].at[2 * px + py], dst_ref=outs[a].at[k], send_sem=send_sems.at[a, k],
                    recv_sem=recv_sems.at[a, k], device_id=(px, py, c), device_id_type=MESH))

        def gather(k, dst_blk, peer_xyz):
            return pltpu.make_async_remote_copy(
                src_ref=small_ref, dst_ref=all_ref.at[dst_blk], send_sem=g_send.at[k], recv_sem=g_recv.at[k],
                device_id=peer_xyz, device_id_type=MESH)

        local = pltpu.make_async_copy(small_ref, all_ref.at[me], loc_sem)
        local.start()
        peers = []
        for k in range(1, N_DEV):
            px = 1 - x if k & 4 else x
            py = 1 - y if k & 2 else y
            pc = 1 - c if k & 1 else c
            peers.append((4 * px + 2 * py + pc, (px, py, pc)))
        sends = [gather(k, me, xyz) for k, (_, xyz) in enumerate(peers)]
        for cp in sends + cps:
            cp.start()
        for k, (peer, xyz) in enumerate(peers):
            gather(k, peer, xyz).wait_recv()
        for cp in cps:
            cp.wait_recv()
        for cp in sends + cps:
            cp.wait_send()
        local.wait()

    return pl.pallas_call(
        body, in_specs=[ANY] * (n + 1), out_specs=[ANY] * (n + 1),
        out_shape=[jax.ShapeDtypeStruct((3,) + p.shape[1:], p.dtype) for p in parts]
        + [jax.ShapeDtypeStruct((N_DEV,) + small.shape, small.dtype)],
        scratch_shapes=[pltpu.SemaphoreType.DMA((n, 3)), pltpu.SemaphoreType.DMA((n, 3)),
                        pltpu.SemaphoreType.DMA((N_DEV - 1,)), pltpu.SemaphoreType.DMA((N_DEV - 1,)),
                        pltpu.SemaphoreType.DMA(())],
        name="scatter_chips")(*parts, small)


def _add_rows(arrs, out_dtype, name):
    shp = arrs[0].shape
    two = [a.reshape(-1, shp[-1]) for a in arrs]
    rows, cols = two[0].shape
    tr = _row_tile(rows, 1024)

    def body(*refs):
        acc = refs[0][...].astype(F32)
        for r in refs[1:-1]:
            acc = acc + r[...].astype(F32)
        refs[-1][...] = acc.astype(out_dtype)

    blk = pl.BlockSpec((tr, cols), lambda i: (i, 0))
    return pl.pallas_call(
        body, grid=(rows // tr,), in_specs=[blk] * len(two), out_specs=blk,
        out_shape=jax.ShapeDtypeStruct((rows, cols), out_dtype),
        compiler_params=_params(("arbitrary",)), name=name)(*two).reshape(shp)


def _row_tile(rows, cap):
    best = 8
    for t in range(8, min(rows, cap) + 1, 8):
        if rows % t == 0:
            best = t
    return best


def _sum_devices(x):
    _, rows, cols = x.shape
    tr = _row_tile(rows, 2048)

    def body(x_ref, o_ref):
        acc = x_ref[0]
        for j in range(1, N_DEV):
            acc = acc + x_ref[j]
        o_ref[...] = acc

    return pl.pallas_call(
        body, grid=(rows // tr,), in_specs=[pl.BlockSpec((N_DEV, tr, cols), lambda i: (0, i, 0))],
        out_specs=pl.BlockSpec((tr, cols), lambda i: (i, 0)), out_shape=jax.ShapeDtypeStruct((rows, cols), F32),
        compiler_params=_params(("arbitrary",)), name="sum_devices")(x)


def _adamw(w, g, m, v, name):
    rows, cols = w.shape
    tr = _row_tile(rows, 512)
    bc1 = 1.0 - ADAM_B1 ** ADAM_STEP
    bc2 = 1.0 - ADAM_B2 ** ADAM_STEP

    def body(w_ref, g_ref, m_ref, v_ref, d_ref, nm_ref, nv_ref):
        g_ = g_ref[...]
        m_ = ADAM_B1 * m_ref[...] + (1.0 - ADAM_B1) * g_
        v_ = ADAM_B2 * v_ref[...] + (1.0 - ADAM_B2) * (g_ * g_)
        d_ref[...] = -ADAM_LR * ((m_ / bc1) / (jnp.sqrt(v_ / bc2) + ADAM_EPS) + ADAM_WD * w_ref[...])
        nm_ref[...] = m_
        nv_ref[...] = v_

    blk = pl.BlockSpec((tr, cols), lambda i: (i, 0))
    return pl.pallas_call(
        body, grid=(rows // tr,), in_specs=[blk] * 4, out_specs=[blk] * 3,
        out_shape=[jax.ShapeDtypeStruct((rows, cols), F32)] * 3,
        compiler_params=_params(("arbitrary",)), name=name)(w, g, m, v)


def _pack_rows(shape):
    return -(-int(np.prod(shape)) // (16 * LANES)) * 16


def _pack(arrs, dtype=F32):
    blocks = []
    for a in arrs:
        flat = a.reshape(-1).astype(dtype)
        rows = _pack_rows(a.shape)
        blocks.append(jnp.pad(flat, (0, rows * LANES - flat.shape[0])).reshape(rows, LANES))
    return jnp.concatenate(blocks, axis=0)


def _unpack(packed, shapes):
    out, off = [], 0
    for s in shapes:
        rows = _pack_rows(s)
        out.append(packed[off:off + rows].reshape(-1)[:int(np.prod(s))].reshape(s))
        off += rows
    return out


SMALL = ("c_ctx", "b_mod", "g_pre", "g_post", "ret_norm_g", "sg_w", "sg_b", "sc_conv_w", "gdn_conv_w", "gdn_a_log",
         "gdn_dt_bias", "gdn_norm_g")
ORDER = ("c_ctx", "w_mod", "b_mod", "g_pre", "g_post", "w_in", "w_out", "ret_norm_g", "sg_w", "sg_b", "sc_conv_w",
         "gdn_conv_w", "gdn_a_log", "gdn_dt_bias", "gdn_norm_g")


def kernel(x, c, ctx, c_ctx, w_mod, b_mod, g_pre, g_post, w_in, w_out, ret_norm_g, sg_w, sg_b, sc_conv_w, gdn_conv_w, gdn_a_log, gdn_dt_bias, gdn_norm_g, loss_target, m_c_ctx, m_w_mod, m_b_mod, m_g_pre, m_g_post, m_w_in, m_w_out, m_ret_norm_g, m_sg_w, m_sg_b, m_sc_conv_w, m_gdn_conv_w, m_gdn_a_log, m_gdn_dt_bias, m_gdn_norm_g, v_c_ctx, v_w_mod, v_b_mod, v_g_pre, v_g_post, v_w_in, v_w_out, v_ret_norm_g, v_sg_w, v_sg_b, v_sc_conv_w, v_gdn_conv_w, v_gdn_a_log, v_gdn_dt_bias, v_gdn_norm_g):
    wts = dict(c_ctx=c_ctx, w_mod=w_mod, b_mod=b_mod, g_pre=g_pre, g_post=g_post, w_in=w_in, w_out=w_out,
               ret_norm_g=ret_norm_g, sg_w=sg_w, sg_b=sg_b, sc_conv_w=sc_conv_w, gdn_conv_w=gdn_conv_w,
               gdn_a_log=gdn_a_log, gdn_dt_bias=gdn_dt_bias, gdn_norm_g=gdn_norm_g)
    mom = dict(c_ctx=m_c_ctx, w_mod=m_w_mod, b_mod=m_b_mod, g_pre=m_g_pre, g_post=m_g_post, w_in=m_w_in, w_out=m_w_out,
               ret_norm_g=m_ret_norm_g, sg_w=m_sg_w, sg_b=m_sg_b, sc_conv_w=m_sc_conv_w, gdn_conv_w=m_gdn_conv_w,
               gdn_a_log=m_gdn_a_log, gdn_dt_bias=m_gdn_dt_bias, gdn_norm_g=m_gdn_norm_g)
    var = dict(c_ctx=v_c_ctx, w_mod=v_w_mod, b_mod=v_b_mod, g_pre=v_g_pre, g_post=v_g_post, w_in=v_w_in, w_out=v_w_out,
               ret_norm_g=v_ret_norm_g, sg_w=v_sg_w, sg_b=v_sg_b, sc_conv_w=v_sc_conv_w, gdn_conv_w=v_gdn_conv_w,
               gdn_a_log=v_gdn_a_log, gdn_dt_bias=v_gdn_dt_bias, gdn_norm_g=v_gdn_norm_g)
    depth = w_mod.shape[0]
    n_mod, n_in, n_out = w_mod.shape[2], w_in.shape[2], w_out.shape[1]
    n_sc, n_gdn = sc_conv_w.shape[2], gdn_conv_w.shape[2]
    xi, yi, ci = _me()
    me = 4 * xi + 2 * yi + ci

    conv = _pack([sc_conv_w, gdn_conv_w])
    n_conv = depth * 3 * n_sc
    rest = depth - 1
    blocks = [lambda r, j: r.at[:, :, pl.ds(pl.multiple_of(j * n_mod, LANES), n_mod)],
              lambda r, j: r.at[j],
              lambda r, j: r.at[:, pl.ds(pl.multiple_of(j * n_out, 16), n_out), :],
              lambda r, j: r.at[j]]

    def in_place(g):
        return jnp.pad(g.transpose(1, 2, 0, 3).reshape(g.shape[1], D, N_DEV * n_in),
                       ((0, 0), (0, 0), (0, W_PAD - N_DEV * n_in)))

    wmod_0, win_g, wout_0, conv_g = _gather_weights(
        [w_mod[:1].astype(BF16), w_in[:1].astype(BF16), w_out[:1].astype(BF16), conv],
        [(1, D, N_DEV * n_mod), (N_DEV, 1, D, n_in), (1, N_DEV * n_out, D), (N_DEV,) + conv.shape], blocks)
    later_shards = [w_mod[1:].astype(BF16), w_in[1:].astype(BF16), w_out[1:].astype(BF16)]
    zero = jnp.zeros((), jnp.int32)
    lands = [lax.dynamic_update_slice(lax.empty((rest, D, N_DEV * n_mod), BF16), later_shards[0],
                                      (zero, zero, me * n_mod)),
             lax.dynamic_update_slice(lax.empty((N_DEV, rest, D, n_in), BF16), later_shards[1][None],
                                      (me, zero, zero, zero)),
             lax.dynamic_update_slice(lax.empty((rest, N_DEV * n_out, D), BF16), later_shards[2],
                                      (zero, me * n_out, zero))]
    started = _gather_start(later_shards, lands, blocks[:3], "gather_start")

    def later(stream):
        wmod_r, win_r, wout_r = _gather_wait(started, stream, blocks[:3], "gather_wait")[1]
        return wmod_r, in_place(win_r), wout_r

    slabs = [lambda r, j: r.at[j], lambda r, j: r.at[:, pl.ds(pl.multiple_of(j * n_out, 16), n_out), :]]

    r_sc = _pack_rows(sc_conv_w.shape)
    scw_f = conv_g[:, :r_sc].reshape(N_DEV, -1)[:, :n_conv]
    scw_f = scw_f.reshape(N_DEV, depth, 3, n_sc).transpose(1, 2, 0, 3).reshape(depth, 3, -1)
    gdnw_f = conv_g[:, r_sc:].reshape(N_DEV, -1)[:, :depth * 3 * n_gdn]
    gdnw_f = gdnw_f.reshape(N_DEV, depth, 3, n_gdn).transpose(1, 2, 0, 3)
    gdnw_f = gdnw_f.reshape(depth, 3, -1)

    loss8, dx, g, cvec8 = _local_step(x, c, ctx, c_ctx, (wmod_0, in_place(win_g), wout_0), later, started[4], b_mod,
                                      g_pre, g_post, ret_norm_g, sg_w, sg_b, scw_f, gdnw_f, gdn_a_log, gdn_dt_bias,
                                      gdn_norm_g, loss_target)

    gin = jnp.stack(g["win"]).astype(BF16).reshape(depth, D, N_DEV, n_in).transpose(2, 0, 1, 3)
    gout = jnp.stack(g["wout"]).astype(BF16)
    got_in, got_out = _scatter_pair([gin, gout], slabs, [(depth, D, n_in), (depth, n_out, D)])
    mine_in = lax.dynamic_index_in_dim(gin.reshape(4, 2, depth, D, n_in), ci, axis=1, keepdims=False)
    mine_out = lax.dynamic_index_in_dim(gout.reshape(depth, 4, 2, n_out, D), ci, axis=2, keepdims=False)
    mine_out = mine_out.transpose(1, 0, 2, 3)
    local_small = dict(c_ctx=g["c_ctx"], b_mod=g["bmod"], g_pre=g["gpre"], g_post=g["gpost"], ret_norm_g=g["ret_ng"],
                       sg_w=g["sgw"], sg_b=g["sgb"], sc_conv_w=g["scw"], gdn_conv_w=g["gdnw"], gdn_a_log=g["alog"],
                       gdn_dt_bias=g["dtb"], gdn_norm_g=g["gdn_ng"])
    to_sum = _pack([loss8[0, :1]] + [local_small[k] for k in SMALL])
    rows_sum = to_sum.shape[0]
    as_is = _pack([cvec8[:3], g["dmod"][:, :3, :]])
    far_in, far_out, everyone = _scatter_chips([_add_rows([mine_in, got_in], BF16, "pair_sum_in"),
                                                _add_rows([mine_out, got_out], BF16, "pair_sum_out")],
                                               jnp.concatenate([to_sum, as_is], axis=0))
    chip = 2 * xi + yi
    own = lambda a: lax.dynamic_index_in_dim(a, chip, axis=0, keepdims=False)
    grad = dict(w_in=_add_rows([own(mine_in), own(got_in), far_in[0], far_in[1], far_in[2]], F32, "grad_sum_in"),
                w_out=_add_rows([own(mine_out), own(got_out), far_out[0], far_out[1], far_out[2]], F32,
                                "grad_sum_out"))

    small_sum = _unpack(_sum_devices(everyone[:, :rows_sum]), [(1,)] + [local_small[k].shape for k in SMALL])
    loss = small_sum[0][0]
    for k, val in zip(SMALL, small_sum[1:]):
        grad[k] = val
    grad["sc_conv_w"] = lax.dynamic_slice_in_dim(grad["sc_conv_w"], me * n_sc, n_sc, axis=2)
    grad["gdn_conv_w"] = lax.dynamic_slice_in_dim(grad["gdn_conv_w"], me * n_gdn, n_gdn, axis=2)
    r_c = _pack_rows((3, D))
    c_all = everyone[:, rows_sum:rows_sum + r_c].reshape(N_DEV, -1)[:, :3 * D].reshape(N_DEV * 3, D)
    dmod_all = everyone[:, rows_sum + r_c:].reshape(N_DEV, -1)[:, :depth * 9 * D]
    dmod_all = dmod_all.reshape(N_DEV, depth, 3, 3 * D).transpose(1, 0, 2, 3)
    dmod_mine = lax.dynamic_slice_in_dim(dmod_all.reshape(depth, N_DEV * 3, 3 * D), me * n_mod, n_mod, axis=2)
    grad["w_mod"] = _wmod_grad(_pad_rows(c_all, 32), jnp.pad(dmod_mine, ((0, 0), (0, 32 - N_DEV * 3), (0, 0))))

    delta, new_m, new_v = {}, {}, {}
    for k in ("w_mod", "w_in", "w_out"):
        shp = wts[k].shape
        two = lambda a: a.reshape(-1, shp[-1])
        res = _adamw(two(wts[k]), two(grad[k]), two(mom[k]), two(var[k]), "adamw_" + k)
        delta[k], new_m[k], new_v[k] = [r.reshape(shp) for r in res]
    res = _adamw(*[_pack([d[k] for k in SMALL]) for d in (wts, grad, mom, var)], "adamw_small")
    for dst, flat in zip((delta, new_m, new_v), res):
        for k, val in zip(SMALL, _unpack(flat, [wts[k].shape for k in SMALL])):
            dst[k] = val
    return (loss, dx, *[grad[k] for k in ORDER], *[delta[k] for k in ORDER], *[new_m[k] for k in ORDER],
            *[new_v[k] for k in ORDER])
```

```python
import functools
import math

import jax
import jax.numpy as jnp
import numpy as np
from jax import lax
from jax.experimental import pallas as pl
from jax.experimental.pallas import tpu as pltpu

F32, BF16 = jnp.float32, jnp.bfloat16
HI = lax.Precision.HIGHEST

N_DEV = 8
D = 1024
DEPTH = 4
BRW = 256
HD = 64
NH = 4
LANES = 128
GRID_W = 64
ROPE_BASE = 10000.0
W_IN = 15 * BRW + 4 * NH
W_PAD = 31 * LANES
RC = 128
GC = 64
EPS = 1e-6
LOG_GAMMA = tuple(math.log(1.0 - 2.0 ** (-5.0 - h)) for h in range(NH))
ADAM_LR, ADAM_B1, ADAM_B2, ADAM_EPS, ADAM_WD, ADAM_STEP = 0.001, 0.9, 0.999, 1e-08, 0.01, 10
VMEM_LIMIT = 56 * 1024 * 1024

COL_RET, COL_SG, COL_SC, COL_GDN = 0, 4, 7, 11
COL_A128 = 30


def _params(sem):
    return pltpu.CompilerParams(dimension_semantics=sem, vmem_limit_bytes=VMEM_LIMIT)


def _bdot(a, b, ca, cb):
    if a.ndim == 3:
        dn = (((ca + 1,), (cb + 1,)), ((0,), (0,)))
    else:
        dn = (((ca,), (cb,)), ((), ()))
    return lax.dot_general(a.astype(BF16), b.astype(BF16), dn, preferred_element_type=F32)


@jax.custom_vjp
def mm(a, b):
    return _bdot(a, b, 1, 0)


mm.defvjp(lambda a, b: (_bdot(a, b, 1, 0), (a, b)),
          lambda r, g: (_bdot(g, r[1], 1, 1), _bdot(r[0], g, 0, 0)))


@jax.custom_vjp
def mm_nt(a, b):
    return _bdot(a, b, 1, 1)


mm_nt.defvjp(lambda a, b: (_bdot(a, b, 1, 1), (a, b)),
             lambda r, g: (_bdot(g, r[1], 1, 0), _bdot(g, r[0], 0, 0)))


@jax.custom_vjp
def mm_tn(a, b):
    return _bdot(a, b, 0, 0)


mm_tn.defvjp(lambda a, b: (_bdot(a, b, 0, 0), (a, b)),
             lambda r, g: (_bdot(r[1], g, 1, 1), _bdot(r[0], g, 1, 0)))


def _dotf(a, b):
    return jnp.dot(a, b, precision=HI, preferred_element_type=F32)


def _iota(shape, dim):
    return lax.broadcasted_iota(jnp.int32, shape, dim)


def _head_mask(h, width=BRW):
    return (_iota((1, width), 1) // HD == h).astype(F32)


def _lane_by_head(vals, width=BRW, lane0=0):
    head = (_iota((1, width), 1) + lane0) // HD
    out = jnp.full((1, width), vals[NH - 1], F32)
    for h in range(NH - 2, -1, -1):
        out = jnp.where(head == h, vals[h], out)
    return out


def _block_diag(n, width):
    return (_iota((n, width), 0) // HD == _iota((n, width), 1) // HD).astype(F32)


@jax.custom_vjp
def _head_sum(x):
    w = x.shape[1]
    ones = _block_diag(w, w).astype(BF16)
    hi = x.astype(BF16)
    lo = (x - hi.astype(F32)).astype(BF16)
    return jnp.dot(hi, ones, preferred_element_type=F32) + jnp.dot(lo, ones, preferred_element_type=F32)


_head_sum.defvjp(lambda x: (_head_sum(x), None), lambda _, g: (_head_sum(g),))


def _silu(x):
    return x * jax.nn.sigmoid(x)


def _stack_heads(x):
    return jnp.concatenate([x * _head_mask(h) for h in range(NH)], axis=-2)


@jax.custom_vjp
def _unstack_heads(x):
    n = x.shape[-2] // NH
    return (x[..., 0:n, :] + x[..., n:2 * n, :]) + (x[..., 2 * n:3 * n, :] + x[..., 3 * n:4 * n, :])


_unstack_heads.defvjp(lambda x: (_unstack_heads(x), None), lambda _, g: (_stack_heads(g),))


@jax.custom_vjp
def _rot_half(x):
    n = x.shape[1]
    first = (_iota(x.shape, 1) % 32) < 16
    return jnp.where(first, -pltpu.roll(x, n - 16, 1), pltpu.roll(x, 16, 1))


_rot_half.defvjp(lambda x: (_rot_half(x), None), lambda _, g: (-_rot_half(g),))


def _rotary(x, cos, sin):
    return x * cos + _rot_half(x) * sin


def _make_shifts(seq, t_ctx):
    def dn_raw(x):
        r = _iota(x.shape, 0)
        return jnp.where((r == 0) | (r == t_ctx), 0.0, pltpu.roll(x, 1, 0))

    def up_raw(x):
        r = _iota(x.shape, 0)
        return jnp.where((r == t_ctx - 1) | (r == seq - 1), 0.0, pltpu.roll(x, seq - 1, 0))

    @jax.custom_vjp
    def dn(x):
        return dn_raw(x)

    @jax.custom_vjp
    def up(x):
        return up_raw(x)

    dn.defvjp(lambda x: (dn_raw(x), None), lambda _, g: (up_raw(g),))
    up.defvjp(lambda x: (up_raw(x), None), lambda _, g: (dn_raw(g),))
    return dn, up


def _conv3(t, w0, w1, w2, dn, up):
    return dn(t) * w0 + t * w1 + up(t) * w2


def _acc(ref, val, first, at=()):
    idx = at + (Ellipsis,)

    @pl.when(first)
    def _():
        ref[idx] = val

    @pl.when(jnp.logical_not(first))
    def _():
        ref[idx] += val


def _mod_fwd(cvec8, wmod, bmod):
    depth = wmod.shape[0]

    def body(c_ref, w_ref, b_ref, o_ref):
        sc = _silu(c_ref[...])
        o_ref[0] = jnp.dot(sc.astype(BF16), w_ref[0], preferred_element_type=F32) + b_ref[0]

    return pl.pallas_call(
        body, grid=(depth, 3),
        in_specs=[pl.BlockSpec((8, D), lambda l, j: (0, 0)),
                  pl.BlockSpec((1, D, D), lambda l, j: (l, 0, j)),
                  pl.BlockSpec((1, 1, D), lambda l, j: (l, 0, j))],
        out_specs=pl.BlockSpec((1, 8, D), lambda l, j: (l, 0, j)),
        out_shape=jax.ShapeDtypeStruct((depth, 8, 3 * D), F32),
        compiler_params=_params(("arbitrary", "arbitrary")), name="mod_fwd")(cvec8, wmod, bmod)


def _mod_bwd(dmod, wmod, cvec8):
    depth = wmod.shape[0]

    def body(dm_ref, w_ref, c_ref, dc_ref, db_ref):
        l, j = pl.program_id(0), pl.program_id(1)
        dm = dm_ref[0]
        db_ref[0] = jnp.sum(dm, axis=0, keepdims=True)
        part = _bdot(dm, w_ref[0], 1, 1)
        _acc(dc_ref, part, (l == 0) & (j == 0))

        @pl.when((l == depth - 1) & (j == 2))
        def _():
            c = c_ref[...]
            s = jax.nn.sigmoid(c)
            dc_ref[...] = dc_ref[...] * (s * (1.0 + c * (1.0 - s)))

    return pl.pallas_call(
        body, grid=(depth, 3),
        in_specs=[pl.BlockSpec((1, 8, D), lambda l, j: (l, 0, j)),
                  pl.BlockSpec((1, D, D), lambda l, j: (l, 0, j)),
                  pl.BlockSpec((8, D), lambda l, j: (0, 0))],
        out_specs=[pl.BlockSpec((8, D), lambda l, j: (0, 0)),
                   pl.BlockSpec((1, 1, D), lambda l, j: (l, 0, j))],
        out_shape=[jax.ShapeDtypeStruct((8, D), F32), jax.ShapeDtypeStruct((depth, 1, 3 * D), F32)],
        compiler_params=_params(("arbitrary", "arbitrary")), name="mod_bwd")(dmod, wmod, cvec8)


def _wmod_grad(c_rows, dmod_cols):
    depth, rows, n = dmod_cols.shape

    def body(c_ref, dm_ref, o_ref):
        sc = _silu(c_ref[...])
        o_ref[0] = lax.dot_general(sc, dm_ref[0], (((0,), (0,)), ((), ())), precision=HI,
                                   preferred_element_type=F32)

    return pl.pallas_call(
        body, grid=(depth,),
        in_specs=[pl.BlockSpec((rows, D), lambda l: (0, 0)), pl.BlockSpec((1, rows, n), lambda l: (l, 0, 0))],
        out_specs=pl.BlockSpec((1, D, n), lambda l: (l, 0, 0)),
        out_shape=jax.ShapeDtypeStruct((depth, D, n), F32),
        compiler_params=_params(("arbitrary",)), name="wmod_grad")(c_rows, dmod_cols)


class _Lay:
    def __init__(self, batch, t_ctx, t_lat):
        self.b, self.t_ctx, self.t_lat = batch, t_ctx, t_lat
        self.s = t_ctx + t_lat
        self.tm = min(256, t_ctx)
        self.tpb = self.s // self.tm
        self.nct = t_ctx // self.tm
        self.ntiles = batch * self.tpb
        self.rows = batch * self.s

    def mod_row(self, i):
        return jnp.where(i % self.tpb < self.nct, 2, i // self.tpb)

    def group(self, i):
        return 2 * (i // self.tpb) + jnp.where(i % self.tpb < self.nct, 0, 1)

    def group_first(self, i):
        return (i % self.tpb == 0) | (i % self.tpb == self.nct)


def _norm_mod(x, g, shift, scale):
    r = lax.rsqrt(jnp.mean(x * x, axis=-1, keepdims=True) + EPS)
    return (x * r * g) * (1.0 + scale) + shift


def _inproj_fwd(lay, xc, mod3, gpre, w):
    tm = lay.tm

    def body(x_ref, sh_ref, sc_ref, g_ref, w_ref, p_ref, ht_ref):
        h = _norm_mod(x_ref[...], g_ref[...], sh_ref[0, 0], sc_ref[0, 0])
        ht_ref[...] = h.T.astype(BF16)
        p_ref[...] = jnp.dot(h.astype(BF16), w_ref[...], preferred_element_type=F32)

    return pl.pallas_call(
        body, grid=(lay.ntiles,),
        in_specs=[pl.BlockSpec((tm, D), lambda i: (i, 0)),
                  pl.BlockSpec((1, 1, 1, D), lambda i: (0, lay.mod_row(i), 0, 0)),
                  pl.BlockSpec((1, 1, 1, D), lambda i: (1, lay.mod_row(i), 0, 0)),
                  pl.BlockSpec((1, D), lambda i: (0, 0)),
                  pl.BlockSpec((D, W_PAD), lambda i: (0, 0))],
        out_specs=[pl.BlockSpec((tm, W_PAD), lambda i: (i, 0)), pl.BlockSpec((D, tm), lambda i: (0, i))],
        out_shape=[jax.ShapeDtypeStruct((lay.rows, W_PAD), F32), jax.ShapeDtypeStruct((D, lay.rows), BF16)],
        compiler_params=_params(("arbitrary",)), name="inproj_fwd")(xc, mod3, mod3, gpre, w)


def _inproj_bwd(lay, xc, mod3, gpre, wt, dxc, pieces):
    tm = lay.tm
    npc = len(pieces)
    offs = [off for _, off in pieces]

    def body(*refs):
        x_ref, sh_ref, sc_ref, g_ref, wt_ref, dx_in = refs[:6]
        dps = refs[6:6 + npc]
        dx_ref, dg_ref, dsh_ref, dsc_ref = refs[6 + npc:]
        i = pl.program_id(0)
        dh = None
        for dp_ref, off in zip(dps, offs):
            wd = dp_ref.shape[1]
            part = jnp.dot(dp_ref[...], wt_ref[off:off + wd, :], preferred_element_type=F32)
            dh = part if dh is None else dh + part
        _, vjp = jax.vjp(_norm_mod, x_ref[...], g_ref[...], sh_ref[0, 0], sc_ref[0, 0])
        dx, dg, dsh, dsc = vjp(dh)
        dx_ref[...] = dx_in[...] + dx
        _acc(dg_ref, dg, i == 0)
        first = lay.group_first(i)
        _acc(dsh_ref, dsh, first, at=(0,))
        _acc(dsc_ref, dsc, first, at=(0,))

    return pl.pallas_call(
        body, grid=(lay.ntiles,),
        in_specs=[pl.BlockSpec((tm, D), lambda i: (i, 0)),
                  pl.BlockSpec((1, 1, 1, D), lambda i: (0, lay.mod_row(i), 0, 0)),
                  pl.BlockSpec((1, 1, 1, D), lambda i: (1, lay.mod_row(i), 0, 0)),
                  pl.BlockSpec((1, D), lambda i: (0, 0)),
                  pl.BlockSpec((W_PAD, D), lambda i: (0, 0)),
                  pl.BlockSpec((tm, D), lambda i: (i, 0))]
        + [pl.BlockSpec((tm, dp.shape[1]), lambda i: (i, 0)) for dp, _ in pieces],
        out_specs=[pl.BlockSpec((tm, D), lambda i: (i, 0)),
                   pl.BlockSpec((1, D), lambda i: (0, 0)),
                   pl.BlockSpec((1, 1, D), lambda i: (lay.group(i), 0, 0)),
                   pl.BlockSpec((1, 1, D), lambda i: (lay.group(i), 0, 0))],
        out_shape=[jax.ShapeDtypeStruct((lay.rows, D), F32), jax.ShapeDtypeStruct((1, D), F32),
                   jax.ShapeDtypeStruct((2 * lay.b, 1, D), F32), jax.ShapeDtypeStruct((2 * lay.b, 1, D), F32)],
        compiler_params=_params(("arbitrary",)), name="inproj_bwd",
    )(xc, mod3, mod3, gpre, wt, dxc, *[dp for dp, _ in pieces])


def _weight_grad(lay, ht, dp, name):
    wd = dp.shape[1]
    tn = 512 if wd % 512 == 0 else (256 if wd % 256 == 0 else LANES)
    tr = lay.rows // 3 if lay.rows % (3 * 256) == 0 else lay.tm

    def body(ht_ref, dp_ref, o_ref):
        _acc(o_ref, jnp.dot(ht_ref[...], dp_ref[...], preferred_element_type=F32), pl.program_id(1) == 0)

    return pl.pallas_call(
        body, grid=(wd // tn, lay.rows // tr),
        in_specs=[pl.BlockSpec((D, tr), lambda j, i: (0, i)), pl.BlockSpec((tr, tn), lambda j, i: (i, j))],
        out_specs=pl.BlockSpec((D, tn), lambda j, i: (0, j)),
        out_shape=jax.ShapeDtypeStruct((D, wd), F32),
        compiler_params=_params(("arbitrary", "arbitrary")), name=name)(ht, dp)


def _outproj_post(o, x, gpost, gate):
    r = lax.rsqrt(jnp.mean(o * o, axis=-1, keepdims=True) + EPS)
    return x + gate * (o * r * gpost)


def _outproj_matmul(ys, w_ref):
    o = None
    for k, y in enumerate(ys):
        part = jnp.dot(y[...], w_ref[BRW * k:BRW * (k + 1), :], preferred_element_type=F32)
        o = part if o is None else o + part
    return o


def _outproj_specs(lay):
    tm = lay.tm
    return ([pl.BlockSpec((tm, BRW), lambda i: (i, 0))] * 4
            + [pl.BlockSpec((tm, D), lambda i: (i, 0))]
            + [pl.BlockSpec((D, D), lambda i: (0, 0))]
            + [pl.BlockSpec((1, D), lambda i: (0, 0))]
            + [pl.BlockSpec((1, 1, 1, D), lambda i: (2, lay.mod_row(i), 0, 0))])


def _outproj_fwd(lay, ys, xc, wout, gpost, mod3):
    tm = lay.tm

    def body(y0, y1, y2, y3, x_ref, w_ref, g_ref, gt_ref, o_ref, yt_ref):
        ys_ = (y0, y1, y2, y3)
        o_ref[...] = _outproj_post(_outproj_matmul(ys_, w_ref), x_ref[...], g_ref[...], gt_ref[0, 0])
        for k, y in enumerate(ys_):
            yt_ref[BRW * k:BRW * (k + 1), :] = y[...].astype(F32).T.astype(BF16)

    return pl.pallas_call(
        body, grid=(lay.ntiles,), in_specs=_outproj_specs(lay),
        out_specs=[pl.BlockSpec((tm, D), lambda i: (i, 0)), pl.BlockSpec((D, tm), lambda i: (0, i))],
        out_shape=[jax.ShapeDtypeStruct((lay.rows, D), F32), jax.ShapeDtypeStruct((D, lay.rows), BF16)],
        compiler_params=_params(("arbitrary",)), name="outproj_fwd")(*ys, xc, wout, gpost, mod3)


def _outproj_bwd(lay, ys, xc, wout, gpost, mod3, dxc):
    tm = lay.tm

    def body(y0, y1, y2, y3, x_ref, w_ref, g_ref, gt_ref, dx_ref, d0, d1, d2, d3, do_ref, dg_ref, dgt_ref):
        i = pl.program_id(0)
        o = _outproj_matmul((y0, y1, y2, y3), w_ref)
        _, vjp = jax.vjp(_outproj_post, o, x_ref[...], g_ref[...], gt_ref[0, 0])
        do, _, dg, dgt = vjp(dx_ref[...])
        do = do.astype(BF16)
        do_ref[...] = do
        for k, d in enumerate((d0, d1, d2, d3)):
            d[...] = _bdot(do, w_ref[BRW * k:BRW * (k + 1), :], 1, 1)
        _acc(dg_ref, dg, i == 0)
        _acc(dgt_ref, dgt, lay.group_first(i), at=(0,))

    return pl.pallas_call(
        body, grid=(lay.ntiles,),
        in_specs=_outproj_specs(lay) + [pl.BlockSpec((tm, D), lambda i: (i, 0))],
        out_specs=[pl.BlockSpec((tm, BRW), lambda i: (i, 0))] * 4
        + [pl.BlockSpec((tm, D), lambda i: (i, 0)), pl.BlockSpec((1, D), lambda i: (0, 0)),
           pl.BlockSpec((1, 1, D), lambda i: (lay.group(i), 0, 0))],
        out_shape=[jax.ShapeDtypeStruct((lay.rows, BRW), F32)] * 4
        + [jax.ShapeDtypeStruct((lay.rows, D), BF16), jax.ShapeDtypeStruct((1, D), F32),
           jax.ShapeDtypeStruct((2 * lay.b, 1, D), F32)],
        compiler_params=_params(("arbitrary",)), name="outproj_bwd")(*ys, xc, wout, gpost, mod3, dxc)


def _loss_kernel(lay, xc3, target):
    tm, nct = lay.tm, lay.nct

    def body(x_ref, t_ref, loss_ref, dx_ref):
        b, i = pl.program_id(0), pl.program_id(1)
        lat = i >= nct
        err = x_ref[0] - t_ref[0]
        dx_ref[0] = jnp.where(lat, err * (1.0 / D), 0.0)
        part = jnp.sum(jnp.sum(err * err, axis=1, keepdims=True), axis=0, keepdims=True) * (0.5 / D)
        part = jnp.broadcast_to(jnp.where(lat, part, 0.0), (8, LANES))
        _acc(loss_ref, part, (b == 0) & (i == 0))

    return pl.pallas_call(
        body, grid=(lay.b, lay.tpb),
        in_specs=[pl.BlockSpec((1, tm, D), lambda b, i: (b, i, 0)),
                  pl.BlockSpec((1, tm, D), lambda b, i: (b, jnp.maximum(i - nct, 0), 0))],
        out_specs=[pl.BlockSpec((8, LANES), lambda b, i: (0, 0)), pl.BlockSpec((1, tm, D), lambda b, i: (b, i, 0))],
        out_shape=[jax.ShapeDtypeStruct((8, LANES), F32), jax.ShapeDtypeStruct(xc3.shape, F32)],
        compiler_params=_params(("arbitrary", "arbitrary")), name="loss")(xc3, target)


def _chunk_orders(n_ctx, n_all):
    fwd = list(range(n_all))
    rev = list(range(n_ctx - 1, -1, -1)) + list(range(n_all - 1, n_ctx - 1, -1))
    return fwd, rev


def _ret_state_fn(k, v, cos, sin):
    kt = _rotary(k, cos, sin) * (HD ** -0.5)
    lg = _lane_by_head(LOG_GAMMA)
    j = _iota((RC, 1), 0).astype(F32)
    bd = _block_diag(BRW, BRW)
    af = mm_tn(kt * jnp.exp((RC - 1.0 - j) * lg), v) * bd
    ar = mm_tn(kt * jnp.exp(j * lg), v) * bd
    return af, ar


def _ret_out_fn(q, k, v, z, cos, sin, sf, sr, ng):
    qt = _rotary(q, cos, sin)
    kt = _rotary(k, cos, sin) * (HD ** -0.5)
    diff = (_iota((RC, RC), 0) - _iota((RC, RC), 1)).astype(F32)
    o = None
    for h in range(NH):
        m = _head_mask(h)
        sc = mm_nt(qt * m, kt)
        wgt = sc * jnp.exp(jnp.abs(diff) * LOG_GAMMA[h]) * jnp.where(diff == 0, 2.0, 1.0)
        part = mm(wgt, v * m)
        o = part if o is None else o + part
    lg = _lane_by_head(LOG_GAMMA)
    i = _iota((RC, 1), 0).astype(F32)
    o = o + mm(qt, sf) * jnp.exp((i + 1.0) * lg) + mm(qt, sr) * jnp.exp((RC - i) * lg)
    mu = _head_sum(o) * (1.0 / HD)
    cen = o - mu
    var = _head_sum(cen * cen) * (1.0 / HD)
    return cen * lax.rsqrt(var + EPS) * ng * _silu(z)


def _ret_chunks(lay):
    nc = lay.s // RC
    return 6 if nc % 6 == 0 else (2 if nc % 2 == 0 else 1)


def _ret_specs(lay, cols):
    rows = _ret_chunks(lay) * RC
    return [pl.BlockSpec((1, rows, BRW), functools.partial(lambda b, i, c: (b, i, c), c=COL_RET + c)) for c in cols]


def _ret_state(lay, p3, cos, sin):
    nc, per = lay.s // RC, _ret_chunks(lay)

    def body(k_ref, v_ref, c_ref, s_ref, a_ref):
        for n in range(per):
            rows = pl.ds(RC * n, RC)
            af, ar = _ret_state_fn(k_ref[0, rows, :], v_ref[0, rows, :], c_ref[rows, :], s_ref[rows, :])
            a_ref[0, n, 0] = af
            a_ref[0, n, 1] = ar

    tab = pl.BlockSpec((per * RC, BRW), lambda b, i: (i, 0))
    return pl.pallas_call(
        body, grid=(lay.b, nc // per), in_specs=_ret_specs(lay, (1, 2)) + [tab, tab],
        out_specs=pl.BlockSpec((1, per, 2, BRW, BRW), lambda b, i: (b, i, 0, 0, 0)),
        out_shape=jax.ShapeDtypeStruct((lay.b, nc, 2, BRW, BRW), F32),
        compiler_params=_params(("arbitrary", "arbitrary")), name="ret_state")(p3, p3, cos, sin)


def _ret_state_bwd(lay, p3, cos, sin, d_a, dpr):
    nc, per = lay.s // RC, _ret_chunks(lay)

    def body(k_ref, v_ref, c_ref, s_ref, da_ref, dpr_ref, o_ref):
        for n in range(per):
            rows = pl.ds(RC * n, RC)
            cos_, sin_ = c_ref[rows, :], s_ref[rows, :]
            _, vjp = jax.vjp(lambda k, v: _ret_state_fn(k, v, cos_, sin_), k_ref[0, rows, :], v_ref[0, rows, :])
            dk, dv = vjp((da_ref[0, n, 0], da_ref[0, n, 1]))
            o_ref[0, rows, 0:BRW] = dpr_ref[0, rows, 0:BRW].astype(BF16)
            o_ref[0, rows, BRW:2 * BRW] = (dpr_ref[0, rows, BRW:2 * BRW] + dk).astype(BF16)
            o_ref[0, rows, 2 * BRW:3 * BRW] = (dpr_ref[0, rows, 2 * BRW:3 * BRW] + dv).astype(BF16)
            o_ref[0, rows, 3 * BRW:] = dpr_ref[0, rows, 3 * BRW:].astype(BF16)

    tab = pl.BlockSpec((per * RC, BRW), lambda b, i: (i, 0))
    return pl.pallas_call(
        body, grid=(lay.b, nc // per),
        in_specs=_ret_specs(lay, (1, 2)) + [tab, tab,
                                            pl.BlockSpec((1, per, 2, BRW, BRW), lambda b, i: (b, i, 0, 0, 0)),
                                            pl.BlockSpec((1, per * RC, 4 * BRW), lambda b, i: (b, i, 0))],
        out_specs=pl.BlockSpec((1, per * RC, 4 * BRW), lambda b, i: (b, i, 0)),
        out_shape=jax.ShapeDtypeStruct((lay.b, lay.s, 4 * BRW), BF16),
        compiler_params=_params(("arbitrary", "arbitrary")), name="ret_state_bwd")(p3, p3, cos, sin, d_a, dpr)


def _state_scan(lay, a, nc_ctx, transpose, name):
    b, nc = a.shape[0], a.shape[1]
    orders = _chunk_orders(nc_ctx, nc)

    def body(a_ref, o_ref):
        d, jh = pl.program_id(1), pl.program_id(2)
        head = (_iota((1, LANES), 1) + jh * LANES) // HD
        lg = jnp.full((1, LANES), LOG_GAMMA[NH - 1], F32)
        for h in range(NH - 2, -1, -1):
            lg = jnp.where(head == h, LOG_GAMMA[h], lg)
        dec = jnp.exp(RC * lg)
        for dd in (0, 1):
            @pl.when(d == dd)
            def _(order=orders[dd]):
                acc = jnp.zeros((BRW, LANES), F32)
                if not transpose:
                    for c in order:
                        o_ref[0, c, 0] = acc
                        acc = acc * dec + a_ref[0, c, 0]
                else:
                    for c in reversed(order):
                        o_ref[0, c, 0] = acc
                        acc = a_ref[0, c, 0] + acc * dec

    spec = pl.BlockSpec((1, nc, 1, BRW, LANES), lambda bb, d, jh: (bb, 0, d, 0, jh))
    return pl.pallas_call(
        body, grid=(b, 2, BRW // LANES), in_specs=[spec], out_specs=spec,
        out_shape=jax.ShapeDtypeStruct(a.shape, F32),
        compiler_params=_params(("arbitrary",) * 3), name=name)(a)


def _ret_out(lay, p3, cos, sin, states, ng):
    nc, per = lay.s // RC, _ret_chunks(lay)

    def body(q_ref, k_ref, v_ref, z_ref, c_ref, s_ref, st_ref, ng_ref, y_ref):
        for n in range(per):
            rows = pl.ds(RC * n, RC)
            y = _ret_out_fn(q_ref[0, rows, :], k_ref[0, rows, :], v_ref[0, rows, :], z_ref[0, rows, :],
                            c_ref[rows, :], s_ref[rows, :], st_ref[0, n, 0], st_ref[0, n, 1], ng_ref[...])
            y_ref[0, rows, :] = y.astype(BF16)

    tab = pl.BlockSpec((per * RC, BRW), lambda b, i: (i, 0))
    return pl.pallas_call(
        body, grid=(lay.b, nc // per),
        in_specs=_ret_specs(lay, (0, 1, 2, 3)) + [tab, tab,
                                                  pl.BlockSpec((1, per, 2, BRW, BRW), lambda b, i: (b, i, 0, 0, 0)),
                                                  pl.BlockSpec((1, BRW), lambda b, i: (0, 0))],
        out_specs=pl.BlockSpec((1, per * RC, BRW), lambda b, i: (b, i, 0)),
        out_shape=jax.ShapeDtypeStruct((lay.b, lay.s, BRW), BF16),
        compiler_params=_params(("arbitrary", "arbitrary")), name="ret_out")(p3, p3, p3, p3, cos, sin, states, ng)


def _ret_out_bwd(lay, p3, cos, sin, states, ng, dy):
    nc, per = lay.s // RC, _ret_chunks(lay)

    def body(q_ref, k_ref, v_ref, z_ref, c_ref, s_ref, st_ref, ng_ref, dy_ref, dp_ref, dst_ref, dng_ref):
        b, i = pl.program_id(0), pl.program_id(1)
        dng_sum = None
        for n in range(per):
            rows = pl.ds(RC * n, RC)
            cos_, sin_ = c_ref[rows, :], s_ref[rows, :]
            fn = lambda q, k, v, z, sf, sr, ng: _ret_out_fn(q, k, v, z, cos_, sin_, sf, sr, ng)
            _, vjp = jax.vjp(fn, q_ref[0, rows, :], k_ref[0, rows, :], v_ref[0, rows, :], z_ref[0, rows, :],
                             st_ref[0, n, 0], st_ref[0, n, 1], ng_ref[...])
            dq, dk, dv, dz, dsf, dsr, dng = vjp(dy_ref[0, rows, :])
            for m, g in enumerate((dq, dk, dv, dz)):
                dp_ref[0, rows, BRW * m:BRW * (m + 1)] = g
            dst_ref[0, n, 0] = dsf
            dst_ref[0, n, 1] = dsr
            dng_sum = dng if dng_sum is None else dng_sum + dng
        _acc(dng_ref, dng_sum, (b == 0) & (i == 0))

    tab = pl.BlockSpec((per * RC, BRW), lambda b, i: (i, 0))
    st = pl.BlockSpec((1, per, 2, BRW, BRW), lambda b, i: (b, i, 0, 0, 0))
    return pl.pallas_call(
        body, grid=(lay.b, nc // per),
        in_specs=_ret_specs(lay, (0, 1, 2, 3)) + [tab, tab, st, pl.BlockSpec((1, BRW), lambda b, i: (0, 0)),
                                                  pl.BlockSpec((1, per * RC, BRW), lambda b, i: (b, i, 0))],
        out_specs=[pl.BlockSpec((1, per * RC, 4 * BRW), lambda b, i: (b, i, 0)), st,
                   pl.BlockSpec((1, BRW), lambda b, i: (0, 0))],
        out_shape=[jax.ShapeDtypeStruct((lay.b, lay.s, 4 * BRW), F32),
                   jax.ShapeDtypeStruct(states.shape, F32), jax.ShapeDtypeStruct((1, BRW), F32)],
        compiler_params=_params(("arbitrary", "arbitrary")), name="ret_out_bwd",
    )(p3, p3, p3, p3, cos, sin, states, ng, dy)


def _sg_fn(u, v, z, w, b8):
    ug = jax.nn.gelu(u)
    vg = jax.nn.gelu(v)
    mu = jnp.mean(vg, axis=-1, keepdims=True)
    cen = vg - mu
    vn = cen * lax.rsqrt(jnp.mean(cen * cen, axis=-1, keepdims=True) + EPS)
    masks = (_iota((NH, 1, BRW), 2) // HD == _iota((NH, 1, BRW), 0)).astype(F32)
    s = jnp.sum(mm(w, vn[None] * masks), axis=0)
    expand = (_iota((8, BRW), 1) // HD == _iota((8, BRW), 0)).astype(F32)
    bias = lax.dot_general(b8, expand, (((0,), (0,)), ((), ())), precision=HI, preferred_element_type=F32)
    return ug * (s + bias) * _silu(z)


def _sg_chunks(lay):
    nc = lay.s // RC
    return 6 if nc % 6 == 0 else (2 if nc % 2 == 0 else 1)


def _sg_specs(lay):
    rows = _sg_chunks(lay) * RC
    return ([pl.BlockSpec((1, rows, BRW), functools.partial(lambda b, i, c: (b, i, c), c=COL_SG + c)) for c in range(3)]
            + [pl.BlockSpec((NH, RC, RC), lambda b, i: (0, 0, 0)), pl.BlockSpec((8, RC), lambda b, i: (0, 0))])


def _sg_fwd(lay, p3, sgw, sgb8):
    per = _sg_chunks(lay)

    def body(u_ref, v_ref, z_ref, w_ref, b_ref, y_ref):
        for k in range(per):
            rows = pl.ds(RC * k, RC)
            y = _sg_fn(u_ref[0, rows, :], v_ref[0, rows, :], z_ref[0, rows, :], w_ref[...], b_ref[...])
            y_ref[0, rows, :] = y.astype(BF16)

    return pl.pallas_call(
        body, grid=(lay.b, lay.s // (per * RC)), in_specs=_sg_specs(lay),
        out_specs=pl.BlockSpec((1, per * RC, BRW), lambda b, i: (b, i, 0)),
        out_shape=jax.ShapeDtypeStruct((lay.b, lay.s, BRW), BF16),
        compiler_params=_params(("arbitrary", "arbitrary")), name="sg_fwd")(p3, p3, p3, sgw, sgb8)


def _sg_bwd(lay, p3, sgw, sgb8, dy):
    per = _sg_chunks(lay)

    def body(u_ref, v_ref, z_ref, w_ref, b_ref, dy_ref, dp_ref, dw_ref, db_ref):
        first = (pl.program_id(0) == 0) & (pl.program_id(1) == 0)
        dw = db = None
        for k in range(per):
            rows = pl.ds(RC * k, RC)
            _, vjp = jax.vjp(_sg_fn, u_ref[0, rows, :], v_ref[0, rows, :], z_ref[0, rows, :], w_ref[...], b_ref[...])
            g = vjp(dy_ref[0, rows, :])
            for n in range(3):
                dp_ref[0, rows, BRW * n:BRW * (n + 1)] = g[n].astype(BF16)
            dw = g[3] if dw is None else dw + g[3]
            db = g[4] if db is None else db + g[4]
        _acc(dw_ref, dw, first)
        _acc(db_ref, db, first)

    return pl.pallas_call(
        body, grid=(lay.b, lay.s // (per * RC)),
        in_specs=_sg_specs(lay) + [pl.BlockSpec((1, per * RC, BRW), lambda b, i: (b, i, 0))],
        out_specs=[pl.BlockSpec((1, per * RC, 3 * BRW), lambda b, i: (b, i, 0)),
                   pl.BlockSpec((NH, RC, RC), lambda b, i: (0, 0, 0)), pl.BlockSpec((8, RC), lambda b, i: (0, 0))],
        out_shape=[jax.ShapeDtypeStruct((lay.b, lay.s, 3 * BRW), BF16),
                   jax.ShapeDtypeStruct((NH, RC, RC), F32), jax.ShapeDtypeStruct((8, RC), F32)],
        compiler_params=_params(("arbitrary", "arbitrary")), name="sg_bwd")(p3, p3, p3, sgw, sgb8, dy)


def _sc_specs(lay):
    first = COL_SC * BRW // LANES
    blk = [pl.BlockSpec((1, lay.s, LANES), functools.partial(lambda j, b, c: (b, 0, c + j), c=first + 2 * n))
           for n in range(4)]
    return blk + [pl.BlockSpec((8, LANES), lambda j, b: (0, j))]


def _sc_fwd(lay, p3, w8):
    dn, up = _make_shifts(lay.s, lay.t_ctx)

    def fn(b_, c_, h_, z_, w0, w1, w2):
        return b_ * _conv3(c_ * h_, w0, w1, w2, dn, up) * _silu(z_)

    def body(b_ref, c_ref, h_ref, z_ref, w_ref, y_ref):
        y = fn(b_ref[0], c_ref[0], h_ref[0], z_ref[0], w_ref[0:1, :], w_ref[1:2, :], w_ref[2:3, :])
        y_ref[0] = y.astype(BF16)

    return pl.pallas_call(
        body, grid=(BRW // LANES, lay.b), in_specs=_sc_specs(lay),
        out_specs=pl.BlockSpec((1, lay.s, LANES), lambda j, b: (b, 0, j)),
        out_shape=jax.ShapeDtypeStruct((lay.b, lay.s, BRW), BF16),
        compiler_params=_params(("arbitrary", "arbitrary")), name="sc_fwd")(p3, p3, p3, p3, w8)


def _sc_bwd(lay, p3, w8, dy):
    dn, up = _make_shifts(lay.s, lay.t_ctx)

    def fn(b_, c_, h_, z_, w0, w1, w2):
        return b_ * _conv3(c_ * h_, w0, w1, w2, dn, up) * _silu(z_)

    def body(b_ref, c_ref, h_ref, z_ref, w_ref, dy_ref, db_ref, dc_ref, dh_ref, dz_ref, dw_ref):
        _, vjp = jax.vjp(fn, b_ref[0], c_ref[0], h_ref[0], z_ref[0], w_ref[0:1, :], w_ref[1:2, :], w_ref[2:3, :])
        g = vjp(dy_ref[0])
        for ref, val in zip((db_ref, dc_ref, dh_ref, dz_ref), g[:4]):
            ref[0] = val.astype(BF16)
        dw = jnp.concatenate([g[4], g[5], g[6], jnp.zeros((5, LANES), F32)], axis=0)
        _acc(dw_ref, dw, pl.program_id(1) == 0)

    out = pl.BlockSpec((1, lay.s, LANES), lambda j, b: (b, 0, j))
    return pl.pallas_call(
        body, grid=(BRW // LANES, lay.b), in_specs=_sc_specs(lay) + [out],
        out_specs=[out] * 4 + [pl.BlockSpec((8, LANES), lambda j, b: (0, j))],
        out_shape=[jax.ShapeDtypeStruct((lay.b, lay.s, BRW), BF16)] * 4 + [jax.ShapeDtypeStruct((8, BRW), F32)],
        compiler_params=_params(("arbitrary", "arbitrary")), name="sc_bwd")(p3, p3, p3, p3, w8, dy)


def _gdn_conv_fn(x, w0, w1, w2, normed, dn, up):
    a = _silu(_conv3(x, w0, w1, w2, dn, up))
    nrm = a * lax.rsqrt(_head_sum(a * a) + EPS)
    return jnp.where(normed, nrm, a)


def _gdn_conv(lay, p3, w8):
    dn, up = _make_shifts(lay.s, lay.t_ctx)
    first = COL_GDN * BRW // LANES

    def body(x_ref, w_ref, o_ref):
        normed = pl.program_id(0) < 2 * BRW // LANES
        o_ref[0] = _gdn_conv_fn(x_ref[0], w_ref[0:1, :], w_ref[1:2, :], w_ref[2:3, :], normed, dn, up)

    return pl.pallas_call(
        body, grid=(3 * BRW // LANES, lay.b),
        in_specs=[pl.BlockSpec((1, lay.s, LANES), lambda j, b: (b, 0, first + j)),
                  pl.BlockSpec((8, LANES), lambda j, b: (0, j))],
        out_specs=pl.BlockSpec((1, lay.s, LANES), lambda j, b: (b, 0, j)),
        out_shape=jax.ShapeDtypeStruct((lay.b, lay.s, 3 * BRW), F32),
        compiler_params=_params(("arbitrary", "arbitrary")), name="gdn_conv")(p3, w8)


def _gdn_conv_bwd(lay, p3, w8, dqkv):
    dn, up = _make_shifts(lay.s, lay.t_ctx)
    first = COL_GDN * BRW // LANES

    def body(x_ref, w_ref, g_ref, dx_ref, dw_ref):
        normed = pl.program_id(0) < 2 * BRW // LANES
        fn = lambda x, w0, w1, w2: _gdn_conv_fn(x, w0, w1, w2, normed, dn, up)
        _, vjp = jax.vjp(fn, x_ref[0], w_ref[0:1, :], w_ref[1:2, :], w_ref[2:3, :])
        g = vjp(g_ref[0])
        dx_ref[0] = g[0].astype(BF16)
        dw = jnp.concatenate([g[1], g[2], g[3], jnp.zeros((5, LANES), F32)], axis=0)
        _acc(dw_ref, dw, pl.program_id(1) == 0)

    blk = pl.BlockSpec((1, lay.s, LANES), lambda j, b: (b, 0, j))
    return pl.pallas_call(
        body, grid=(3 * BRW // LANES, lay.b),
        in_specs=[pl.BlockSpec((1, lay.s, LANES), lambda j, b: (b, 0, first + j)),
                  pl.BlockSpec((8, LANES), lambda j, b: (0, j)), blk],
        out_specs=[blk, pl.BlockSpec((8, LANES), lambda j, b: (0, j))],
        out_shape=[jax.ShapeDtypeStruct((lay.b, lay.s, 3 * BRW), BF16), jax.ShapeDtypeStruct((8, 3 * BRW), F32)],
        compiler_params=_params(("arbitrary", "arbitrary")), name="gdn_conv_bwd")(p3, w8, dqkv)


def _tri_inverse(low):
    i, j = _iota(low.shape, low.ndim - 2), _iota(low.shape, low.ndim - 1) % GC
    t = (i == j).astype(F32)
    s = 1
    while s < GC:
        pair = (i // (2 * s)) == (j // (2 * s))
        off = pair & (((i // s) % 2) != ((j // s) % 2))
        cb = jnp.where(off, low, 0.0)
        t = t - (cb if s == 1 else _bdot(t, _stack_heads(_bdot(cb, _stack_heads(t), 1, 0)), 1, 0))
        s *= 2
    return t


@jax.custom_vjp
def _tri_solve(t, low, r1, r2):
    del low
    return _bdot(t, _stack_heads(r1), 1, 0), _bdot(t, _stack_heads(r2), 1, 0)


def _tri_solve_fwd(t, low, r1, r2):
    del low
    x1, x2 = _bdot(t, _stack_heads(r1), 1, 0), _bdot(t, _stack_heads(r2), 1, 0)
    return (x1, x2), (t, x1, x2)


def _tri_solve_bwd(res, g):
    t, x1, x2 = res
    bd = _block_diag(BRW, BRW)
    d1 = _unstack_heads(_bdot(t, g[0], 0, 0) * bd)
    d2 = _unstack_heads(_bdot(t, g[1], 0, 0) * bd)
    dlow = -(_bdot(d1, _stack_heads(x1), 1, 1) + _bdot(d2, _stack_heads(x2), 1, 1))
    return jnp.zeros_like(t), dlow, d1, d2


_tri_solve.defvjp(_tri_solve_fwd, _tri_solve_bwd)

N_PACK = 5


def _gdn_prep_fn(qn, kn, vv, a, alog, dtb, t=None):
    n = qn.shape[0]
    col = _iota((1, 1, LANES), 2)
    xx = a + dtb
    softplus = jnp.maximum(xx, 0.0) + jnp.log(1.0 + jnp.exp(-jnp.abs(xx)))
    g_small = jnp.where(col < 8, -jnp.exp(alog) * softplus, 0.0).reshape(n * GC, LANES)
    beta_small = jax.nn.sigmoid(a).reshape(n * GC, LANES)
    sel_col, sel_head = _iota((LANES, BRW), 0), _iota((LANES, BRW), 1) // HD
    g_l, b_l = [], []
    for d in (0, 1):
        g_l.append(_dotf(g_small, (sel_col == 4 * d + sel_head).astype(F32)))
        b_l.append(_dotf(beta_small, (sel_col == 8 + 4 * d + sel_head).astype(F32)))
    g_l = jnp.concatenate(g_l, axis=0).reshape(2 * n, GC, BRW)
    b_l = jnp.concatenate(b_l, axis=0).reshape(2 * n, GC, BRW)
    rev = _iota((2 * n, 1, 1), 0) >= n
    fwd = jnp.logical_not(rev)
    ri, ci = _iota((1, GC, GC), 1), _iota((1, GC, GC), 2)
    tri = ((fwd & (ri >= ci)) | (rev & (ri <= ci))).astype(F32)
    gc_l = lax.dot_general(tri, g_l, (((2,), (1,)), ((0,), (0,))), precision=HI,
                           preferred_element_type=F32)
    gtot_l = jnp.sum(g_l, axis=1, keepdims=True)
    i, j = _iota((1, GC, BRW), 1), _iota((1, GC, BRW), 2) % GC
    gc_t = jnp.sum(jnp.where(i == j, gc_l, 0.0), axis=1, keepdims=True)
    incl = (fwd & (i >= j)) | (rev & (i <= j))
    strict = (fwd & (i > j)) | (rev & (i < j))
    decay = jnp.where(incl, jnp.exp(jnp.where(incl, gc_l - gc_t, 0.0)), 0.0)
    kn2 = jnp.concatenate([kn, kn], axis=0)
    vv2 = jnp.concatenate([vv, vv], axis=0)
    qs = jnp.concatenate([qn, qn], axis=0) * (HD ** -0.5)
    kst = _stack_heads(kn2)
    kb = kn2 * b_l
    low = jnp.where(strict, mm_nt(kb, kst) * decay, 0.0)
    eg = jnp.exp(gc_l)
    t_inv = _tri_inverse(low) if t is None else t
    u, w = _tri_solve(t_inv, low, vv2 * b_l, kb * eg)
    k_tail = kn2 * jnp.exp(gtot_l - gc_l)
    intra = mm_nt(qs, kst) * decay
    return (u, w, k_tail, qs * eg, intra), jnp.exp(gtot_l), t_inv


def _prep_chunks(lay):
    return 4 if (lay.s // GC) % 4 == 0 else 2


def _gdn_prep_specs(lay):
    rows = _prep_chunks(lay) * GC
    return ([pl.BlockSpec((1, rows, BRW), functools.partial(lambda b, i, c: (b, i, c), c=c)) for c in range(3)]
            + [pl.BlockSpec((1, rows, LANES), lambda b, i: (b, i, COL_A128)),
               pl.BlockSpec((8, LANES), lambda b, i: (0, 0))])


def _gdn_prep(lay, qkv, p3, prm):
    nc, per = lay.s // GC, _prep_chunks(lay)

    def body(q_ref, k_ref, v_ref, a_ref, prm_ref, pack_ref, cd_ref, t_ref):
        chunks = lambda ref: ref[0].reshape(per, GC, ref.shape[-1])
        pack, cd, t_inv = _gdn_prep_fn(chunks(q_ref), chunks(k_ref), chunks(v_ref), chunks(a_ref),
                                       prm_ref[0:1, :], prm_ref[1:2, :])
        for d in (0, 1):
            for n in range(N_PACK):
                pack_ref[0, :, d, n] = pack[n][per * d:per * (d + 1)]
            cd_ref[0, :, d] = cd[per * d:per * (d + 1)]
            t_ref[0, :, d] = t_inv[per * d:per * (d + 1)]

    return pl.pallas_call(
        body, grid=(lay.b, nc // per), in_specs=_gdn_prep_specs(lay),
        out_specs=[pl.BlockSpec((1, per, 2, N_PACK, GC, BRW), lambda b, i: (b, i, 0, 0, 0, 0)),
                   pl.BlockSpec((1, per, 2, 1, BRW), lambda b, i: (b, i, 0, 0, 0)),
                   pl.BlockSpec((1, per, 2, GC, BRW), lambda b, i: (b, i, 0, 0, 0))],
        out_shape=[jax.ShapeDtypeStruct((lay.b, nc, 2, N_PACK, GC, BRW), F32),
                   jax.ShapeDtypeStruct((lay.b, nc, 2, 1, BRW), F32),
                   jax.ShapeDtypeStruct((lay.b, nc, 2, GC, BRW), F32)],
        compiler_params=_params(("arbitrary", "arbitrary")), name="gdn_prep")(qkv, qkv, qkv, p3, prm)


def _gdn_prep_bwd(lay, qkv, p3, prm, dpacks, dcds, t_inv):
    nc, per = lay.s // GC, _prep_chunks(lay)

    def body(q_ref, k_ref, v_ref, a_ref, prm_ref, dpf_ref, dpr_ref, dcf_ref, dcr_ref, t_ref, dqkv_ref, da_ref,
             dprm_ref):
        first = (pl.program_id(0) == 0) & (pl.program_id(1) == 0)
        chunks = lambda ref: ref[0].reshape(per, GC, ref.shape[-1])
        t_inv = jnp.concatenate([t_ref[0, :, 0], t_ref[0, :, 1]], axis=0)
        fn = lambda q, k, v, a, alog, dtb: _gdn_prep_fn(q, k, v, a, alog, dtb, t_inv)[:2]
        _, vjp = jax.vjp(fn, chunks(q_ref), chunks(k_ref), chunks(v_ref), chunks(a_ref),
                         prm_ref[0:1, :], prm_ref[1:2, :])
        dpack = tuple(jnp.concatenate([dpf_ref[0, :, n], dpr_ref[0, :, n]], axis=0) for n in range(N_PACK))
        dq, dk, dv, da, dalog, ddtb = vjp((dpack, jnp.concatenate([dcf_ref[0], dcr_ref[0]], axis=0)))
        dqkv_ref[0, :, 0:BRW] = dq.reshape(per * GC, BRW)
        dqkv_ref[0, :, BRW:2 * BRW] = dk.reshape(per * GC, BRW)
        dqkv_ref[0, :, 2 * BRW:] = dv.reshape(per * GC, BRW)
        da_ref[0] = da.reshape(per * GC, LANES).astype(BF16)
        _acc(dprm_ref, jnp.concatenate([dalog, ddtb, jnp.zeros((6, LANES), F32)], axis=0), first)

    rows_blk = per * GC
    return pl.pallas_call(
        body, grid=(lay.b, nc // per),
        in_specs=_gdn_prep_specs(lay)
        + [pl.BlockSpec((1, per, N_PACK, GC, BRW), lambda b, i: (b, i, 0, 0, 0))] * 2
        + [pl.BlockSpec((1, per, 1, BRW), lambda b, i: (b, i, 0, 0))] * 2
        + [pl.BlockSpec((1, per, 2, GC, BRW), lambda b, i: (b, i, 0, 0, 0))],
        out_specs=[pl.BlockSpec((1, rows_blk, 3 * BRW), lambda b, i: (b, i, 0)),
                   pl.BlockSpec((1, rows_blk, LANES), lambda b, i: (b, i, 0)),
                   pl.BlockSpec((8, LANES), lambda b, i: (0, 0))],
        out_shape=[jax.ShapeDtypeStruct((lay.b, lay.s, 3 * BRW), F32),
                   jax.ShapeDtypeStruct((lay.b, lay.s, LANES), BF16), jax.ShapeDtypeStruct((8, LANES), F32)],
        compiler_params=_params(("arbitrary", "arbitrary")), name="gdn_prep_bwd",
    )(qkv, qkv, qkv, p3, prm, *dpacks, *dcds, t_inv)


def _gdn_step_fn(s, u, w, k_tail, qd, intra, cdec):
    v_new = u - mm(w, s)
    o = mm(qd, s) + mm(intra, _stack_heads(v_new))
    return s * cdec + mm_tn(k_tail, v_new) * _block_diag(BRW, BRW), o


def _order_index(nc_ctx, nc, d, step):
    rev = jnp.where(step < nc_ctx, nc_ctx - 1 - step, nc + nc_ctx - 1 - step)
    return jnp.where(d == 0, step, rev)


def _gdn_scan(lay, pack, cd):
    nc, nc_ctx = lay.s // GC, lay.t_ctx // GC
    chunk = functools.partial(_order_index, nc_ctx, nc)

    def body(pf_ref, pr_ref, cf_ref, cr_ref, of_ref, or_ref, sf_ref, sr_ref, s_scr):
        @pl.when(pl.program_id(0) == 0)
        def _():
            s_scr[...] = jnp.zeros_like(s_scr)

        nb = lay.b
        s = s_scr[...]
        st = _unstack_heads(s)
        sf_ref[:, 0] = st[:nb]
        sr_ref[:, 0] = st[nb:]
        args = [jnp.concatenate([pf_ref[:, 0, 0, n], pr_ref[:, 0, 0, n]], axis=0) for n in range(N_PACK)]
        s_new, o = _gdn_step_fn(s, *args, jnp.concatenate([cf_ref[:, 0, 0], cr_ref[:, 0, 0]], axis=0))
        of_ref[:, 0] = o[:nb]
        or_ref[:, 0] = o[nb:]
        s_scr[...] = s_new

    def pk(d):
        return pl.BlockSpec((lay.b, 1, 1, N_PACK, GC, BRW), lambda t: (0, chunk(d, t), d, 0, 0, 0))

    def cdb(d):
        return pl.BlockSpec((lay.b, 1, 1, 1, BRW), lambda t: (0, chunk(d, t), d, 0, 0))

    def out(d):
        return pl.BlockSpec((lay.b, 1, GC, BRW), lambda t: (0, chunk(d, t), 0, 0))

    return pl.pallas_call(
        body, grid=(nc,), in_specs=[pk(0), pk(1), cdb(0), cdb(1)],
        out_specs=[out(0), out(1), out(0), out(1)],
        out_shape=[jax.ShapeDtypeStruct((lay.b, nc, GC, BRW), F32)] * 4,
        scratch_shapes=[pltpu.VMEM((2 * lay.b, BRW, BRW), F32)],
        compiler_params=_params(("arbitrary",)), name="gdn_scan")(pack, pack, cd, cd)


def _gdn_scan_bwd(lay, pack, cd, states, do):
    nc, nc_ctx = lay.s // GC, lay.t_ctx // GC

    def chunk(d, t):
        return _order_index(nc_ctx, nc, d, nc - 1 - t)

    def body(pf_ref, pr_ref, cf_ref, cr_ref, sf_ref, sr_ref, dof_ref, dor_ref, dpf_ref, dpr_ref, dcf_ref, dcr_ref,
             ds_scr):
        @pl.when(pl.program_id(0) == 0)
        def _():
            ds_scr[...] = jnp.zeros_like(ds_scr)

        nb = lay.b
        both = lambda f, r: jnp.concatenate([f, r], axis=0)
        args = ([_stack_heads(both(sf_ref[:, 0], sr_ref[:, 0]))]
                + [both(pf_ref[:, 0, 0, n], pr_ref[:, 0, 0, n]) for n in range(N_PACK)]
                + [both(cf_ref[:, 0, 0], cr_ref[:, 0, 0])])
        _, vjp = jax.vjp(_gdn_step_fn, *args)
        g = vjp((ds_scr[...], both(dof_ref[...], dor_ref[...])))
        ds_scr[...] = g[0]
        for n in range(N_PACK):
            dpf_ref[:, 0, n] = g[1 + n][:nb]
            dpr_ref[:, 0, n] = g[1 + n][nb:]
        dcf_ref[:, 0] = g[1 + N_PACK][:nb]
        dcr_ref[:, 0] = g[1 + N_PACK][nb:]

    def pk(d):
        return pl.BlockSpec((lay.b, 1, 1, N_PACK, GC, BRW), lambda t: (0, chunk(d, t), d, 0, 0, 0))

    def cdb(d):
        return pl.BlockSpec((lay.b, 1, 1, 1, BRW), lambda t: (0, chunk(d, t), d, 0, 0))

    def st(d):
        return pl.BlockSpec((lay.b, 1, GC, BRW), lambda t: (0, chunk(d, t), 0, 0))

    def dob(d):
        return pl.BlockSpec((lay.b, GC, BRW), lambda t: (0, chunk(d, t), 0))

    def dpk(d):
        return pl.BlockSpec((lay.b, 1, N_PACK, GC, BRW), lambda t: (0, chunk(d, t), 0, 0, 0))

    def dcb(d):
        return pl.BlockSpec((lay.b, 1, 1, BRW), lambda t: (0, chunk(d, t), 0, 0))

    return pl.pallas_call(
        body, grid=(nc,),
        in_specs=[pk(0), pk(1), cdb(0), cdb(1), st(0), st(1), dob(0), dob(1)],
        out_specs=[dpk(0), dpk(1), dcb(0), dcb(1)],
        out_shape=[jax.ShapeDtypeStruct((lay.b, nc, N_PACK, GC, BRW), F32)] * 2
        + [jax.ShapeDtypeStruct((lay.b, nc, 1, BRW), F32)] * 2,
        scratch_shapes=[pltpu.VMEM((2 * lay.b, BRW, BRW), F32)],
        compiler_params=_params(("arbitrary",)), name="gdn_scan_bwd")(pack, pack, cd, cd, *states, do, do)


def _gdn_finish_fn(o, z, ng):
    return o * lax.rsqrt(_head_sum(o * o) * (1.0 / HD) + EPS) * ng * _silu(z)


def _finish_chunks(lay):
    nc = lay.s // GC
    return 12 if nc % 12 == 0 else (6 if nc % 6 == 0 else 2)


def _gdn_o(of_ref, or_ref):
    return (of_ref[0] + or_ref[0]).reshape(of_ref.shape[1] * GC, BRW)


def _gdn_finish_specs(lay):
    per = _finish_chunks(lay)
    ob = pl.BlockSpec((1, per, GC, BRW), lambda b, i: (b, i, 0, 0))
    return [ob, ob, pl.BlockSpec((1, per * GC, BRW), lambda b, i: (b, i, COL_GDN + 3)),
            pl.BlockSpec((1, BRW), lambda b, i: (0, 0))]


def _gdn_finish(lay, o_f, o_r, p3, ng):
    rows = _finish_chunks(lay) * GC

    def body(of_ref, or_ref, z_ref, ng_ref, y_ref):
        y_ref[0] = _gdn_finish_fn(_gdn_o(of_ref, or_ref), z_ref[0], ng_ref[...]).astype(BF16)

    return pl.pallas_call(
        body, grid=(lay.b, lay.s // rows), in_specs=_gdn_finish_specs(lay),
        out_specs=pl.BlockSpec((1, rows, BRW), lambda b, i: (b, i, 0)),
        out_shape=jax.ShapeDtypeStruct((lay.b, lay.s, BRW), BF16),
        compiler_params=_params(("arbitrary", "arbitrary")), name="gdn_finish")(o_f, o_r, p3, ng)


def _gdn_finish_bwd(lay, o_f, o_r, p3, ng, dy):
    rows = _finish_chunks(lay) * GC

    def body(of_ref, or_ref, z_ref, ng_ref, dy_ref, do_ref, dz_ref, dng_ref):
        first = (pl.program_id(0) == 0) & (pl.program_id(1) == 0)
        _, vjp = jax.vjp(_gdn_finish_fn, _gdn_o(of_ref, or_ref), z_ref[0], ng_ref[...])
        do, dz, dng = vjp(dy_ref[0])
        do_ref[0] = do
        dz_ref[0] = dz.astype(BF16)
        _acc(dng_ref, dng, first)

    blk = pl.BlockSpec((1, rows, BRW), lambda b, i: (b, i, 0))
    return pl.pallas_call(
        body, grid=(lay.b, lay.s // rows), in_specs=_gdn_finish_specs(lay) + [blk],
        out_specs=[blk, blk, pl.BlockSpec((1, BRW), lambda b, i: (0, 0))],
        out_shape=[jax.ShapeDtypeStruct((lay.b, lay.s, BRW), F32), jax.ShapeDtypeStruct((lay.b, lay.s, BRW), BF16),
                   jax.ShapeDtypeStruct((1, BRW), F32)],
        compiler_params=_params(("arbitrary", "arbitrary")), name="gdn_finish_bwd")(o_f, o_r, p3, ng, dy)


def _rope_tables(lay):
    t = jnp.arange(lay.t_lat)
    lane = np.arange(BRW)
    dim = lane % HD
    inv = jnp.asarray(ROPE_BASE ** (-(dim % 16).astype(np.float32) / 16.0), F32)
    pos = jnp.where((dim // 32 == 0)[None, :], (t // GRID_W)[:, None], (t % GRID_W)[:, None]).astype(F32)
    ang = pos * inv[None, :]
    cos = jnp.concatenate([jnp.ones((lay.t_ctx, BRW), F32), jnp.cos(ang)], axis=0)
    sin = jnp.concatenate([jnp.zeros((lay.t_ctx, BRW), F32), jnp.sin(ang)], axis=0)
    return cos, sin


def _pad_rows(a, rows):
    return jnp.concatenate([a, jnp.zeros((rows - a.shape[0],) + a.shape[1:], a.dtype)], axis=0)


def _layer_fwd(lay, xc, wl, cos, sin):
    p, ht = _inproj_fwd(lay, xc, wl["mod3"], wl["gpre"], wl["win"])
    p3 = p.reshape(lay.b, lay.s, W_PAD)
    nctx = lay.t_ctx // RC
    states = _state_scan(lay, _ret_state(lay, p3, cos, sin), nctx, False, "ret_scan")
    y_ret = _ret_out(lay, p3, cos, sin, states, wl["ret_ng"])
    y_sg = _sg_fwd(lay, p3, wl["sgw"], wl["sgb8"])
    y_sc = _sc_fwd(lay, p3, wl["scw8"])
    qkv = _gdn_conv(lay, p3, wl["gdnw8"])
    pack, cd, t_inv = _gdn_prep(lay, qkv, p3, wl["prm"])
    o_f, o_r, st_f, st_r = _gdn_scan(lay, pack, cd)
    y_gdn = _gdn_finish(lay, o_f, o_r, p3, wl["gdn_ng"])
    ys = [y.reshape(lay.rows, BRW) for y in (y_ret, y_sg, y_sc, y_gdn)]
    xc_new, yt = _outproj_fwd(lay, ys, xc, wl["wout"], wl["gpost"], wl["mod3"])
    saved = dict(xc=xc, p3=p3, ht=ht, yt=yt, states=states, qkv=qkv, pack=pack, cd=cd, t_inv=t_inv, o_f=o_f, o_r=o_r, gstates=(st_f, st_r),
                 ys=ys)
    return xc_new, saved


def _layer_bwd(lay, sv, wl, cos, sin, dxc):
    p3 = sv["p3"]
    as3 = lambda a: a.reshape(lay.b, lay.s, a.shape[-1])
    as2 = lambda a: a.reshape(lay.rows, a.shape[-1])
    dy_ret, dy_sg, dy_sc, dy_gdn, do_, dgpost, dgate = _outproj_bwd(
        lay, sv["ys"], sv["xc"], wl["wout"], wl["gpost"], wl["mod3"], dxc)
    dwout = _weight_grad(lay, sv["yt"], do_, "wout_grad")
    nctx = lay.t_ctx // RC
    dpr, dstates, dret_ng = _ret_out_bwd(lay, p3, cos, sin, sv["states"], wl["ret_ng"], as3(dy_ret))
    d_a = _state_scan(lay, dstates, nctx, True, "ret_scan_bwd")
    dp_ret = _ret_state_bwd(lay, p3, cos, sin, d_a, dpr)
    dp_sg, dsgw, dsgb8 = _sg_bwd(lay, p3, wl["sgw"], wl["sgb8"], as3(dy_sg))
    dsb, dsc_, dsh_, dsz, dscw8 = _sc_bwd(lay, p3, wl["scw8"], as3(dy_sc))
    do, dgz, dgdn_ng = _gdn_finish_bwd(lay, sv["o_f"], sv["o_r"], p3, wl["gdn_ng"], as3(dy_gdn))
    dpf, dpr_, dcf, dcr = _gdn_scan_bwd(lay, sv["pack"], sv["cd"], sv["gstates"], do)
    dqkv, da, dprm = _gdn_prep_bwd(lay, sv["qkv"], p3, wl["prm"], (dpf, dpr_), (dcf, dcr), sv["t_inv"])
    dp_gqkv, dgdnw8 = _gdn_conv_bwd(lay, p3, wl["gdnw8"], dqkv)
    pieces = [(as2(dp_ret), 0), (as2(dp_sg), COL_SG * BRW), (as2(dsb), COL_SC * BRW), (as2(dsc_), (COL_SC + 1) * BRW),
              (as2(dsh_), (COL_SC + 2) * BRW), (as2(dsz), (COL_SC + 3) * BRW), (as2(dp_gqkv), COL_GDN * BRW),
              (as2(dgz), (COL_GDN + 3) * BRW), (as2(da), COL_A128 * LANES)]
    dxc_prev, dgpre, dshift, dscale = _inproj_bwd(lay, sv["xc"], wl["mod3"], wl["gpre"], wl["wint"], dxc, pieces)
    dws = [_weight_grad(lay, sv["ht"], dp, "win_grad_%d" % off) for dp, off in pieces]
    dwin = jnp.concatenate(dws[:-1] + [dws[-1][:, :W_IN - COL_A128 * LANES]], axis=1)

    def rows3(g):
        return jnp.concatenate([g[1], g[3], g[0] + g[2]], axis=0)

    dmod = _pad_rows(jnp.concatenate([rows3(dshift), rows3(dscale), rows3(dgate)], axis=1), 8)
    grads = dict(win=dwin, wout=dwout, gpre=dgpre[0], gpost=dgpost[0], ret_ng=dret_ng[0], sgw=dsgw, sgb=dsgb8[:NH],
                 scw=dscw8[:3], gdnw=dgdnw8[:3], alog=dprm[0, :2 * NH].reshape(2, NH),
                 dtb=dprm[1, :2 * NH].reshape(2, NH), gdn_ng=dgdn_ng.reshape(NH, HD).sum(axis=0), dmod=dmod)
    return dxc_prev, grads


def _local_step(x, c, ctx, c_ctx, first, later, token, bmod, gpre, gpost, ret_ng, sgw, sgb, scw, gdnw, alog, dtb,
                gdn_ng, target):
    depth = bmod.shape[0]
    lay = _Lay(x.shape[0], ctx.shape[1], x.shape[1])
    assert lay.b == 2 and lay.t_ctx % RC == 0 and lay.t_lat % RC == 0
    cos, sin = _rope_tables(lay)
    cvec8 = _pad_rows(jnp.concatenate([c, c_ctx[None]], axis=0), 8) + token[0, 0]
    wmod = first[0]
    mod = _mod_fwd(cvec8, wmod, bmod[:1, None, :])
    xc = jnp.concatenate([ctx, x], axis=1).reshape(lay.rows, D)
    layers, saved = [], []
    for l in range(depth):
        if l == 1:
            rest = later(xc)
            wmod = jnp.concatenate([first[0], rest[0]], axis=0)
            mod = jnp.concatenate([mod, _mod_fwd(cvec8, rest[0], bmod[1:, None, :])], axis=0)
        win, wout = (first[1][0], first[2][0]) if l == 0 else (rest[1][l - 1], rest[2][l - 1])
        wl = dict(mod3=mod[l].reshape(8, 3, D).transpose(1, 0, 2)[:, :, None, :], gpre=gpre[l][None], gpost=gpost[l][None],
                  win=win, wint=jnp.swapaxes(win, 0, 1), wout=wout, ret_ng=ret_ng[l][None], sgw=sgw[l],
                  sgb8=_pad_rows(sgb[l], 8), scw8=_pad_rows(scw[l], 8), gdnw8=_pad_rows(gdnw[l], 8),
                  prm=_pad_rows(jnp.pad(jnp.stack([alog[l].reshape(-1), dtb[l].reshape(-1)]),
                                        ((0, 0), (0, LANES - 2 * NH))), 8),
                  gdn_ng=jnp.tile(gdn_ng[l], NH)[None])
        xc, sv = _layer_fwd(lay, xc, wl, cos, sin)
        layers.append(wl)
        saved.append(sv)
    loss, dxc3 = _loss_kernel(lay, xc.reshape(lay.b, lay.s, D), target)
    dxc = dxc3.reshape(lay.rows, D)
    grads = [None] * depth
    for l in reversed(range(depth)):
        dxc, grads[l] = _layer_bwd(lay, saved[l], layers[l], cos, sin, dxc)
    stacked = {k: jnp.stack([g[k] for g in grads]) for k in grads[0] if k not in ("win", "wout")}
    stacked["win"] = [g["win"] for g in grads]
    stacked["wout"] = [g["wout"] for g in grads]
    dcvec8, dbmod = _mod_bwd(stacked["dmod"], wmod, cvec8)
    stacked["bmod"] = dbmod[:, 0, :]
    stacked["c_ctx"] = dcvec8[2]
    dx = dxc.reshape(lay.b, lay.s, D)[:, lay.t_ctx:, :]
    return loss, dx, stacked, cvec8


MESH = pl.DeviceIdType.MESH
ANY = pl.BlockSpec(memory_space=pl.ANY)


def _me():
    return lax.axis_index("x"), lax.axis_index("y"), lax.axis_index("c")


def _gather_weights(shards, fulls, blocks):
    n = len(shards)

    def body(*refs):
        ins, outs = refs[:n], refs[n:2 * n]
        send_sems, recv_sems, loc_sems = refs[2 * n:]
        x, y, c = _me()
        me, sibling = (x, y, c), (x, y, 1 - c)
        chips = [(1 - x, y), (x, 1 - y), (1 - x, 1 - y)]

        def blk(a, dev):
            return blocks[a](outs[a], 4 * dev[0] + 2 * dev[1] + dev[2])

        def copy(a, k, block, to, src=None):
            return pltpu.make_async_remote_copy(
                src_ref=blk(a, block) if src is None else src, dst_ref=blk(a, block), send_sem=send_sems.at[a, k],
                recv_sem=recv_sems.at[a, k], device_id=to, device_id_type=MESH)

        mine = [pltpu.make_async_copy(ins[a], blk(a, me), loc_sems.at[a]) for a in range(n)]
        for cp in mine:
            cp.start()
        first = []
        for a in range(n):
            first.append(copy(a, 0, me, sibling, src=ins[a]))
            first += [copy(a, 1 + j, me, (*chip, c), src=ins[a]) for j, chip in enumerate(chips)]
        for cp in first:
            cp.start()
        passed = []
        for j, chip in enumerate(chips):
            for a in range(n):
                copy(a, 1 + j, (*chip, c), me).wait_recv()
                fwd = copy(a, 4 + j, (*chip, c), sibling)
                fwd.start()
                passed.append(fwd)
        for a in range(n):
            copy(a, 0, sibling, me).wait_recv()
            for j, chip in enumerate(chips):
                copy(a, 4 + j, (*chip, 1 - c), me).wait_recv()
        for cp in first + passed:
            cp.wait_send()
        for cp in mine:
            cp.wait()

    return pl.pallas_call(
        body, in_specs=[ANY] * n, out_specs=[ANY] * n,
        out_shape=[jax.ShapeDtypeStruct(f, s.dtype) for f, s in zip(fulls, shards)],
        scratch_shapes=[pltpu.SemaphoreType.DMA((n, 7)), pltpu.SemaphoreType.DMA((n, 7)),
                        pltpu.SemaphoreType.DMA((n,))],
        name="gather_weights")(*shards)


HBM = pl.BlockSpec(memory_space=pltpu.HBM)
SEM = pl.BlockSpec(memory_space=pltpu.SEMAPHORE)


def _peer(k, x, y, c):
    return (1 - x if k & 4 else x, 1 - y if k & 2 else y, 1 - c if k & 1 else c)


def _whole(ref, j):
    del j
    return ref


def _gather_start(shards, lands, blocks, name, parts=None):
    n = len(shards)
    parts = parts or [_whole] * n

    def body(*refs):
        ins, land = refs[:n], refs[n:2 * n]
        send_sems, recv_sems = refs[2 * n], refs[2 * n + 1]
        token = refs[-1]
        x, y, c = _me()
        me = 4 * x + 2 * y + c
        for a in range(n):
            for k in range(1, N_DEV):
                px, py, pc = _peer(k, x, y, c)
                pltpu.make_async_remote_copy(
                    src_ref=parts[a](ins[a], 4 * px + 2 * py + pc), dst_ref=blocks[a](land[a], me),
                    send_sem=send_sems.at[7 * a + k - 1], recv_sem=recv_sems.at[7 * a + k - 1],
                    device_id=(px, py, pc), device_id_type=MESH).start()
        token[...] = jnp.zeros_like(token)

    args = [pltpu.with_memory_space_constraint(a, pltpu.HBM) for a in list(shards) + list(lands)]
    out = pl.pallas_call(
        body, name=name,
        out_shape=[pltpu.SemaphoreType.DMA((7 * n,)), pltpu.SemaphoreType.DMA((7 * n,))]
        + [pltpu.HBM(a.shape, a.dtype) for a in args] + [jax.ShapeDtypeStruct((8, LANES), F32)],
        in_specs=[HBM] * (2 * n), out_specs=[SEM, SEM] + [HBM] * (2 * n) + [pl.BlockSpec(memory_space=pltpu.VMEM)],
        input_output_aliases={i: 2 + i for i in range(2 * n)},
        compiler_params=pltpu.CompilerParams(has_side_effects=pltpu.SideEffectType.DATAFLOW_SIDE_EFFECTING),
    )(*args)
    return out[0], out[1], out[2:2 + n], out[2 + n:2 + 2 * n], out[-1]


def _gather_wait(started, after, blocks, name, parts=None):
    send_sems, recv_sems, shards, lands, _ = started
    n = len(shards)
    parts = parts or [_whole] * n

    def body(*refs):
        ins, land = refs[:n], refs[n:2 * n]
        send_sems, recv_sems = refs[2 * n], refs[2 * n + 1]
        x, y, c = _me()
        for a in range(n):
            for k in range(1, N_DEV):
                px, py, pc = _peer(k, x, y, c)
                peer = 4 * px + 2 * py + pc
                cp = pltpu.make_async_remote_copy(
                    src_ref=parts[a](ins[a], peer), dst_ref=blocks[a](land[a], peer),
                    send_sem=send_sems.at[7 * a + k - 1], recv_sem=recv_sems.at[7 * a + k - 1],
                    device_id=(px, py, pc), device_id_type=MESH)
                cp.wait_send()
                cp.wait_recv()

    out = pl.pallas_call(
        body, name=name,
        out_shape=[pltpu.HBM(a.shape, a.dtype) for a in list(shards) + list(lands)],
        in_specs=[HBM] * (2 * n) + [SEM, SEM, ANY], out_specs=[HBM] * (2 * n),
        input_output_aliases={i: i for i in range(2 * n)},
        compiler_params=pltpu.CompilerParams(has_side_effects=pltpu.SideEffectType.DATAFLOW_SIDE_EFFECTING),
    )(*shards, *lands, send_sems, recv_sems, after)
    return out[:n], out[n:]


def _scatter_pair(srcs, slabs, slab_shapes):
    n = len(srcs)

    def body(*refs):
        ins, outs = refs[:n], refs[n:2 * n]
        send_sems, recv_sems = refs[2 * n:]
        x, y, c = _me()
        cps = []
        for a in range(n):
            for q in range(4):
                j = 2 * q + (1 - c)
                cps.append(pltpu.make_async_remote_copy(
                    src_ref=slabs[a](ins[a], j), dst_ref=outs[a].at[q], send_sem=send_sems.at[a, q],
                    recv_sem=recv_sems.at[a, q], device_id=(x, y, 1 - c), device_id_type=MESH))
        for cp in cps:
            cp.start()
        for cp in cps:
            cp.wait_recv()
        for cp in cps:
            cp.wait_send()

    return pl.pallas_call(
        body, in_specs=[ANY] * n, out_specs=[ANY] * n,
        out_shape=[jax.ShapeDtypeStruct((4,) + tuple(shp), s.dtype) for shp, s in zip(slab_shapes, srcs)],
        scratch_shapes=[pltpu.SemaphoreType.DMA((n, 4)), pltpu.SemaphoreType.DMA((n, 4))],
        name="scatter_pair")(*srcs)


def _scatter_chips(parts, small):
    n = len(parts)

    def body(*refs):
        ins, small_ref = refs[:n], refs[n]
        outs, all_ref = refs[n + 1:2 * n + 1], refs[2 * n + 1]
        send_sems, recv_sems, g_send, g_recv, loc_sem = refs[2 * n + 2:]
        x, y, c = _me()
        me = 4 * x + 2 * y + c
        chips = [(1 - x, y), (x, 1 - y), (1 - x, 1 - y)]
        cps = []
        for a in range(n):
            for k, (px, py) in enumerate(chips):
                cps.append(pltpu.make_async_remote_copy(
                    src_ref=ins[a].at[2 * px + py], dst_ref=outs[a].at[k], send_sem=send_sems.at[a, k],
                    recv_sem=recv_sems.at[a, k], device_id=(px, py, c), device_id_type=MESH))

        def gather(k, dst_blk, peer_xyz):
            return pltpu.make_async_remote_copy(
                src_ref=small_ref, dst_ref=all_ref.at[dst_blk], send_sem=g_send.at[k], recv_sem=g_recv.at[k],
                device_id=peer_xyz, device_id_type=MESH)

        local = pltpu.make_async_copy(small_ref, all_ref.at[me], loc_sem)
        local.start()
        peers = []
        for k in range(1, N_DEV):
            px = 1 - x if k & 4 else x
            py = 1 - y if k & 2 else y
            pc = 1 - c if k & 1 else c
            peers.append((4 * px + 2 * py + pc, (px, py, pc)))
        sends = [gather(k, me, xyz) for k, (_, xyz) in enumerate(peers)]
        for cp in sends + cps:
            cp.start()
        for k, (peer, xyz) in enumerate(peers):
            gather(k, peer, xyz).wait_recv()
        for cp in cps:
            cp.wait_recv()
        for cp in sends + cps:
            cp.wait_send()
        local.wait()

    return pl.pallas_call(
        body, in_specs=[ANY] * (n + 1), out_specs=[ANY] * (n + 1),
        out_shape=[jax.ShapeDtypeStruct((3,) + p.shape[1:], p.dtype) for p in parts]
        + [jax.ShapeDtypeStruct((N_DEV,) + small.shape, small.dtype)],
        scratch_shapes=[pltpu.SemaphoreType.DMA((n, 3)), pltpu.SemaphoreType.DMA((n, 3)),
                        pltpu.SemaphoreType.DMA((N_DEV - 1,)), pltpu.SemaphoreType.DMA((N_DEV - 1,)),
                        pltpu.SemaphoreType.DMA(())],
        name="scatter_chips")(*parts, small)


def _add_rows(arrs, out_dtype, name):
    shp = arrs[0].shape
    two = [a.reshape(-1, shp[-1]) for a in arrs]
    rows, cols = two[0].shape
    tr = _row_tile(rows, 1024)

    def body(*refs):
        acc = refs[0][...].astype(F32)
        for r in refs[1:-1]:
            acc = acc + r[...].astype(F32)
        refs[-1][...] = acc.astype(out_dtype)

    blk = pl.BlockSpec((tr, cols), lambda i: (i, 0))
    return pl.pallas_call(
        body, grid=(rows // tr,), in_specs=[blk] * len(two), out_specs=blk,
        out_shape=jax.ShapeDtypeStruct((rows, cols), out_dtype),
        compiler_params=_params(("arbitrary",)), name=name)(*two).reshape(shp)


def _row_tile(rows, cap):
    best = 8
    for t in range(8, min(rows, cap) + 1, 8):
        if rows % t == 0:
            best = t
    return best


def _sum_devices(x):
    _, rows, cols = x.shape
    tr = _row_tile(rows, 2048)

    def body(x_ref, o_ref):
        acc = x_ref[0]
        for j in range(1, N_DEV):
            acc = acc + x_ref[j]
        o_ref[...] = acc

    return pl.pallas_call(
        body, grid=(rows // tr,), in_specs=[pl.BlockSpec((N_DEV, tr, cols), lambda i: (0, i, 0))],
        out_specs=pl.BlockSpec((tr, cols), lambda i: (i, 0)), out_shape=jax.ShapeDtypeStruct((rows, cols), F32),
        compiler_params=_params(("arbitrary",)), name="sum_devices")(x)


def _adamw(w, g, m, v, name):
    rows, cols = w.shape
    tr = _row_tile(rows, 512)
    bc1 = 1.0 - ADAM_B1 ** ADAM_STEP
    bc2 = 1.0 - ADAM_B2 ** ADAM_STEP

    def body(w_ref, g_ref, m_ref, v_ref, d_ref, nm_ref, nv_ref):
        g_ = g_ref[...]
        m_ = ADAM_B1 * m_ref[...] + (1.0 - ADAM_B1) * g_
        v_ = ADAM_B2 * v_ref[...] + (1.0 - ADAM_B2) * (g_ * g_)
        d_ref[...] = -ADAM_LR * ((m_ / bc1) / (jnp.sqrt(v_ / bc2) + ADAM_EPS) + ADAM_WD * w_ref[...])
        nm_ref[...] = m_
        nv_ref[...] = v_

    blk = pl.BlockSpec((tr, cols), lambda i: (i, 0))
    return pl.pallas_call(
        body, grid=(rows // tr,), in_specs=[blk] * 4, out_specs=[blk] * 3,
        out_shape=[jax.ShapeDtypeStruct((rows, cols), F32)] * 3,
        compiler_params=_params(("arbitrary",)), name=name)(w, g, m, v)


def _pack_rows(shape):
    return -(-int(np.prod(shape)) // (16 * LANES)) * 16


def _pack(arrs, dtype=F32):
    blocks = []
    for a in arrs:
        flat = a.reshape(-1).astype(dtype)
        rows = _pack_rows(a.shape)
        blocks.append(jnp.pad(flat, (0, rows * LANES - flat.shape[0])).reshape(rows, LANES))
    return jnp.concatenate(blocks, axis=0)


def _unpack(packed, shapes):
    out, off = [], 0
    for s in shapes:
        rows = _pack_rows(s)
        out.append(packed[off:off + rows].reshape(-1)[:int(np.prod(s))].reshape(s))
        off += rows
    return out


SMALL = ("c_ctx", "b_mod", "g_pre", "g_post", "ret_norm_g", "sg_w", "sg_b", "sc_conv_w", "gdn_conv_w", "gdn_a_log",
         "gdn_dt_bias", "gdn_norm_g")
ORDER = ("c_ctx", "w_mod", "b_mod", "g_pre", "g_post", "w_in", "w_out", "ret_norm_g", "sg_w", "sg_b", "sc_conv_w",
         "gdn_conv_w", "gdn_a_log", "gdn_dt_bias", "gdn_norm_g")


def kernel(x, c, ctx, c_ctx, w_mod, b_mod, g_pre, g_post, w_in, w_out, ret_norm_g, sg_w, sg_b, sc_conv_w, gdn_conv_w, gdn_a_log, gdn_dt_bias, gdn_norm_g, loss_target, m_c_ctx, m_w_mod, m_b_mod, m_g_pre, m_g_post, m_w_in, m_w_out, m_ret_norm_g, m_sg_w, m_sg_b, m_sc_conv_w, m_gdn_conv_w, m_gdn_a_log, m_gdn_dt_bias, m_gdn_norm_g, v_c_ctx, v_w_mod, v_b_mod, v_g_pre, v_g_post, v_w_in, v_w_out, v_ret_norm_g, v_sg_w, v_sg_b, v_sc_conv_w, v_gdn_conv_w, v_gdn_a_log, v_gdn_dt_bias, v_gdn_norm_g):
    wts = dict(c_ctx=c_ctx, w_mod=w_mod, b_mod=b_mod, g_pre=g_pre, g_post=g_post, w_in=w_in, w_out=w_out,
               ret_norm_g=ret_norm_g, sg_w=sg_w, sg_b=sg_b, sc_conv_w=sc_conv_w, gdn_conv_w=gdn_conv_w,
               gdn_a_log=gdn_a_log, gdn_dt_bias=gdn_dt_bias, gdn_norm_g=gdn_norm_g)
    mom = dict(c_ctx=m_c_ctx, w_mod=m_w_mod, b_mod=m_b_mod, g_pre=m_g_pre, g_post=m_g_post, w_in=m_w_in, w_out=m_w_out,
               ret_norm_g=m_ret_norm_g, sg_w=m_sg_w, sg_b=m_sg_b, sc_conv_w=m_sc_conv_w, gdn_conv_w=m_gdn_conv_w,
               gdn_a_log=m_gdn_a_log, gdn_dt_bias=m_gdn_dt_bias, gdn_norm_g=m_gdn_norm_g)
    var = dict(c_ctx=v_c_ctx, w_mod=v_w_mod, b_mod=v_b_mod, g_pre=v_g_pre, g_post=v_g_post, w_in=v_w_in, w_out=v_w_out,
               ret_norm_g=v_ret_norm_g, sg_w=v_sg_w, sg_b=v_sg_b, sc_conv_w=v_sc_conv_w, gdn_conv_w=v_gdn_conv_w,
               gdn_a_log=v_gdn_a_log, gdn_dt_bias=v_gdn_dt_bias, gdn_norm_g=v_gdn_norm_g)
    depth = w_mod.shape[0]
    n_mod, n_in, n_out = w_mod.shape[2], w_in.shape[2], w_out.shape[1]
    n_sc, n_gdn = sc_conv_w.shape[2], gdn_conv_w.shape[2]
    xi, yi, ci = _me()
    me = 4 * xi + 2 * yi + ci

    conv = _pack([sc_conv_w, gdn_conv_w])
    n_conv = depth * 3 * n_sc
    rest = depth - 1
    blocks = [lambda r, j: r.at[:, :, pl.ds(pl.multiple_of(j * n_mod, LANES), n_mod)],
              lambda r, j: r.at[j],
              lambda r, j: r.at[:, pl.ds(pl.multiple_of(j * n_out, 16), n_out), :],
              lambda r, j: r.at[j]]

    def in_place(g):
        return jnp.pad(g.transpose(1, 2, 0, 3).reshape(g.shape[1], D, N_DEV * n_in),
                       ((0, 0), (0, 0), (0, W_PAD - N_DEV * n_in)))

    wmod_0, win_g, wout_0, conv_g = _gather_weights(
        [w_mod[:1].astype(BF16), w_in[:1].astype(BF16), w_out[:1].astype(BF16), conv],
        [(1, D, N_DEV * n_mod), (N_DEV, 1, D, n_in), (1, N_DEV * n_out, D), (N_DEV,) + conv.shape], blocks)
    later_shards = [w_mod[1:].astype(BF16), w_in[1:].astype(BF16), w_out[1:].astype(BF16)]
    zero = jnp.zeros((), jnp.int32)
    lands = [lax.dynamic_update_slice(lax.empty((rest, D, N_DEV * n_mod), BF16), later_shards[0],
                                      (zero, zero, me * n_mod)),
             lax.dynamic_update_slice(lax.empty((N_DEV, rest, D, n_in), BF16), later_shards[1][None],
                                      (me, zero, zero, zero)),
             lax.dynamic_update_slice(lax.empty((rest, N_DEV * n_out, D), BF16), later_shards[2],
                                      (zero, me * n_out, zero))]
    started = _gather_start(later_shards, lands, blocks[:3], "gather_start")

    def later(stream):
        wmod_r, win_r, wout_r = _gather_wait(started, stream, blocks[:3], "gather_wait")[1]
        return wmod_r, in_place(win_r), wout_r

    slabs = [lambda r, j: r.at[j], lambda r, j: r.at[:, pl.ds(pl.multiple_of(j * n_out, 16), n_out), :]]

    r_sc = _pack_rows(sc_conv_w.shape)
    scw_f = conv_g[:, :r_sc].reshape(N_DEV, -1)[:, :n_conv]
    scw_f = scw_f.reshape(N_DEV, depth, 3, n_sc).transpose(1, 2, 0, 3).reshape(depth, 3, -1)
    gdnw_f = conv_g[:, r_sc:].reshape(N_DEV, -1)[:, :depth * 3 * n_gdn]
    gdnw_f = gdnw_f.reshape(N_DEV, depth, 3, n_gdn).transpose(1, 2, 0, 3)
    gdnw_f = gdnw_f.reshape(depth, 3, -1)

    loss8, dx, g, cvec8 = _local_step(x, c, ctx, c_ctx, (wmod_0, in_place(win_g), wout_0), later, started[4], b_mod,
                                      g_pre, g_post, ret_norm_g, sg_w, sg_b, scw_f, gdnw_f, gdn_a_log, gdn_dt_bias,
                                      gdn_norm_g, loss_target)

    gin = jnp.stack(g["win"]).astype(BF16).reshape(depth, D, N_DEV, n_in).transpose(2, 0, 1, 3)
    gout = jnp.stack(g["wout"]).astype(BF16)
    got_in, got_out = _scatter_pair([gin, gout], slabs, [(depth, D, n_in), (depth, n_out, D)])
    mine_in = lax.dynamic_index_in_dim(gin.reshape(4, 2, depth, D, n_in), ci, axis=1, keepdims=False)
    mine_out = lax.dynamic_index_in_dim(gout.reshape(depth, 4, 2, n_out, D), ci, axis=2, keepdims=False)
    mine_out = mine_out.transpose(1, 0, 2, 3)
    local_small = dict(c_ctx=g["c_ctx"], b_mod=g["bmod"], g_pre=g["gpre"], g_post=g["gpost"], ret_norm_g=g["ret_ng"],
                       sg_w=g["sgw"], sg_b=g["sgb"], sc_conv_w=g["scw"], gdn_conv_w=g["gdnw"], gdn_a_log=g["alog"],
                       gdn_dt_bias=g["dtb"], gdn_norm_g=g["gdn_ng"])
    to_sum = _pack([loss8[0, :1]] + [local_small[k] for k in SMALL])
    rows_sum = to_sum.shape[0]
    as_is = _pack([cvec8[:3], g["dmod"][:, :3, :]])
    far_in, far_out, everyone = _scatter_chips([_add_rows([mine_in, got_in], BF16, "pair_sum_in"),
                                                _add_rows([mine_out, got_out], BF16, "pair_sum_out")],
                                               jnp.concatenate([to_sum, as_is], axis=0))
    chip = 2 * xi + yi
    own = lambda a: lax.dynamic_index_in_dim(a, chip, axis=0, keepdims=False)
    grad = dict(w_in=_add_rows([own(mine_in), own(got_in), far_in[0], far_in[1], far_in[2]], F32, "grad_sum_in"),
                w_out=_add_rows([own(mine_out), own(got_out), far_out[0], far_out[1], far_out[2]], F32,
                                "grad_sum_out"))

    small_sum = _unpack(_sum_devices(everyone[:, :rows_sum]), [(1,)] + [local_small[k].shape for k in SMALL])
    loss = small_sum[0][0]
    for k, val in zip(SMALL, small_sum[1:]):
        grad[k] = val
    grad["sc_conv_w"] = lax.dynamic_slice_in_dim(grad["sc_conv_w"], me * n_sc, n_sc, axis=2)
    grad["gdn_conv_w"] = lax.dynamic_slice_in_dim(grad["gdn_conv_w"], me * n_gdn, n_gdn, axis=2)
    r_c = _pack_rows((3, D))
    c_all = everyone[:, rows_sum:rows_sum + r_c].reshape(N_DEV, -1)[:, :3 * D].reshape(N_DEV * 3, D)
    dmod_all = everyone[:, rows_sum + r_c:].reshape(N_DEV, -1)[:, :depth * 9 * D]
    dmod_all = dmod_all.reshape(N_DEV, depth, 3, 3 * D).transpose(1, 0, 2, 3)
    dmod_mine = lax.dynamic_slice_in_dim(dmod_all.reshape(depth, N_DEV * 3, 3 * D), me * n_mod, n_mod, axis=2)
    grad["w_mod"] = _wmod_grad(_pad_rows(c_all, 32), jnp.pad(dmod_mine, ((0, 0), (0, 32 - N_DEV * 3), (0, 0))))

    delta, new_m, new_v = {}, {}, {}
    for k in ("w_mod", "w_in", "w_out"):
        shp = wts[k].shape
        two = lambda a: a.reshape(-1, shp[-1])
        res = _adamw(two(wts[k]), two(grad[k]), two(mom[k]), two(var[k]), "adamw_" + k)
        delta[k], new_m[k], new_v[k] = [r.reshape(shp) for r in res]
    res = _adamw(*[_pack([d[k] for k in SMALL]) for d in (wts, grad, mom, var)], "adamw_small")
    for dst, flat in zip((delta, new_m, new_v), res):
        for k, val in zip(SMALL, _unpack(flat, [wts[k].shape for k in SMALL])):
            dst[k] = val
    return (loss, dx, *[grad[k] for k in ORDER], *[delta[k] for k in ORDER], *[new_m[k] for k in ORDER],
            *[new_v[k] for k in ORDER])
```

```python
import functools
import math

import jax
import jax.numpy as jnp
import numpy as np
from jax import lax
from jax.experimental import pallas as pl
from jax.experimental.pallas import tpu as pltpu

F32, BF16 = jnp.float32, jnp.bfloat16
HI = lax.Precision.HIGHEST

N_DEV = 8
D = 1024
DEPTH = 4
BRW = 256
HD = 64
NH = 4
LANES = 128
GRID_W = 64
ROPE_BASE = 10000.0
W_IN = 15 * BRW + 4 * NH
W_PAD = 31 * LANES
RC = 128
GC = 64
EPS = 1e-6
LOG_GAMMA = tuple(math.log(1.0 - 2.0 ** (-5.0 - h)) for h in range(NH))
ADAM_LR, ADAM_B1, ADAM_B2, ADAM_EPS, ADAM_WD, ADAM_STEP = 0.001, 0.9, 0.999, 1e-08, 0.01, 10
VMEM_LIMIT = 56 * 1024 * 1024

COL_RET, COL_SG, COL_SC, COL_GDN = 0, 4, 7, 11
COL_A128 = 30


def _params(sem):
    return pltpu.CompilerParams(dimension_semantics=sem, vmem_limit_bytes=VMEM_LIMIT)


def _bdot(a, b, ca, cb):
    if a.ndim == 3:
        dn = (((ca + 1,), (cb + 1,)), ((0,), (0,)))
    else:
        dn = (((ca,), (cb,)), ((), ()))
    return lax.dot_general(a.astype(BF16), b.astype(BF16), dn, preferred_element_type=F32)


@jax.custom_vjp
def mm(a, b):
    return _bdot(a, b, 1, 0)


mm.defvjp(lambda a, b: (_bdot(a, b, 1, 0), (a, b)),
          lambda r, g: (_bdot(g, r[1], 1, 1), _bdot(r[0], g, 0, 0)))


@jax.custom_vjp
def mm_nt(a, b):
    return _bdot(a, b, 1, 1)


mm_nt.defvjp(lambda a, b: (_bdot(a, b, 1, 1), (a, b)),
             lambda r, g: (_bdot(g, r[1], 1, 0), _bdot(g, r[0], 0, 0)))


@jax.custom_vjp
def mm_tn(a, b):
    return _bdot(a, b, 0, 0)


mm_tn.defvjp(lambda a, b: (_bdot(a, b, 0, 0), (a, b)),
             lambda r, g: (_bdot(r[1], g, 1, 1), _bdot(r[0], g, 1, 0)))


def _dotf(a, b):
    return jnp.dot(a, b, precision=HI, preferred_element_type=F32)


def _iota(shape, dim):
    return lax.broadcasted_iota(jnp.int32, shape, dim)


def _head_mask(h, width=BRW):
    return (_iota((1, width), 1) // HD == h).astype(F32)


def _lane_by_head(vals, width=BRW, lane0=0):
    head = (_iota((1, width), 1) + lane0) // HD
    out = jnp.full((1, width), vals[NH - 1], F32)
    for h in range(NH - 2, -1, -1):
        out = jnp.where(head == h, vals[h], out)
    return out


def _block_diag(n, width):
    return (_iota((n, width), 0) // HD == _iota((n, width), 1) // HD).astype(F32)


@jax.custom_vjp
def _head_sum(x):
    w = x.shape[1]
    ones = _block_diag(w, w).astype(BF16)
    hi = x.astype(BF16)
    lo = (x - hi.astype(F32)).astype(BF16)
    return jnp.dot(hi, ones, preferred_element_type=F32) + jnp.dot(lo, ones, preferred_element_type=F32)


_head_sum.defvjp(lambda x: (_head_sum(x), None), lambda _, g: (_head_sum(g),))


def _silu(x):
    return x * jax.nn.sigmoid(x)


def _stack_heads(x):
    return jnp.concatenate([x * _head_mask(h) for h in range(NH)], axis=-2)


@jax.custom_vjp
def _unstack_heads(x):
    n = x.shape[-2] // NH
    return (x[..., 0:n, :] + x[..., n:2 * n, :]) + (x[..., 2 * n:3 * n, :] + x[..., 3 * n:4 * n, :])


_unstack_heads.defvjp(lambda x: (_unstack_heads(x), None), lambda _, g: (_stack_heads(g),))


@jax.custom_vjp
def _rot_half(x):
    n = x.shape[1]
    first = (_iota(x.shape, 1) % 32) < 16
    return jnp.where(first, -pltpu.roll(x, n - 16, 1), pltpu.roll(x, 16, 1))


_rot_half.defvjp(lambda x: (_rot_half(x), None), lambda _, g: (-_rot_half(g),))


def _rotary(x, cos, sin):
    return x * cos + _rot_half(x) * sin


def _make_shifts(seq, t_ctx):
    def dn_raw(x):
        r = _iota(x.shape, 0)
        return jnp.where((r == 0) | (r == t_ctx), 0.0, pltpu.roll(x, 1, 0))

    def up_raw(x):
        r = _iota(x.shape, 0)
        return jnp.where((r == t_ctx - 1) | (r == seq - 1), 0.0, pltpu.roll(x, seq - 1, 0))

    @jax.custom_vjp
    def dn(x):
        return dn_raw(x)

    @jax.custom_vjp
    def up(x):
        return up_raw(x)

    dn.defvjp(lambda x: (dn_raw(x), None), lambda _, g: (up_raw(g),))
    up.defvjp(lambda x: (up_raw(x), None), lambda _, g: (dn_raw(g),))
    return dn, up


def _conv3(t, w0, w1, w2, dn, up):
    return dn(t) * w0 + t * w1 + up(t) * w2


def _acc(ref, val, first, at=()):
    idx = at + (Ellipsis,)

    @pl.when(first)
    def _():
        ref[idx] = val

    @pl.when(jnp.logical_not(first))
    def _():
        ref[idx] += val


def _mod_fwd(cvec8, wmod, bmod):
    depth = wmod.shape[0]

    def body(c_ref, w_ref, b_ref, o_ref):
        sc = _silu(c_ref[...])
        o_ref[0] = jnp.dot(sc.astype(BF16), w_ref[0], preferred_element_type=F32) + b_ref[0]

    return pl.pallas_call(
        body, grid=(depth, 3),
        in_specs=[pl.BlockSpec((8, D), lambda l, j: (0, 0)),
                  pl.BlockSpec((1, D, D), lambda l, j: (l, 0, j)),
                  pl.BlockSpec((1, 1, D), lambda l, j: (l, 0, j))],
        out_specs=pl.BlockSpec((1, 8, D), lambda l, j: (l, 0, j)),
        out_shape=jax.ShapeDtypeStruct((depth, 8, 3 * D), F32),
        compiler_params=_params(("arbitrary", "arbitrary")), name="mod_fwd")(cvec8, wmod, bmod)


def _mod_bwd(dmod, wmod, cvec8):
    depth = wmod.shape[0]

    def body(dm_ref, w_ref, c_ref, dc_ref, db_ref):
        l, j = pl.program_id(0), pl.program_id(1)
        dm = dm_ref[0]
        db_ref[0] = jnp.sum(dm, axis=0, keepdims=True)
        part = _bdot(dm, w_ref[0], 1, 1)
        _acc(dc_ref, part, (l == 0) & (j == 0))

        @pl.when((l == depth - 1) & (j == 2))
        def _():
            c = c_ref[...]
            s = jax.nn.sigmoid(c)
            dc_ref[...] = dc_ref[...] * (s * (1.0 + c * (1.0 - s)))

    return pl.pallas_call(
        body, grid=(depth, 3),
        in_specs=[pl.BlockSpec((1, 8, D), lambda l, j: (l, 0, j)),
                  pl.BlockSpec((1, D, D), lambda l, j: (l, 0, j)),
                  pl.BlockSpec((8, D), lambda l, j: (0, 0))],
        out_specs=[pl.BlockSpec((8, D), lambda l, j: (0, 0)),
                   pl.BlockSpec((1, 1, D), lambda l, j: (l, 0, j))],
        out_shape=[jax.ShapeDtypeStruct((8, D), F32), jax.ShapeDtypeStruct((depth, 1, 3 * D), F32)],
        compiler_params=_params(("arbitrary", "arbitrary")), name="mod_bwd")(dmod, wmod, cvec8)


def _wmod_grad(c_rows, dmod_cols):
    depth, rows, n = dmod_cols.shape

    def body(c_ref, dm_ref, o_ref):
        sc = _silu(c_ref[...])
        o_ref[0] = lax.dot_general(sc, dm_ref[0], (((0,), (0,)), ((), ())), precision=HI,
                                   preferred_element_type=F32)

    return pl.pallas_call(
        body, grid=(depth,),
        in_specs=[pl.BlockSpec((rows, D), lambda l: (0, 0)), pl.BlockSpec((1, rows, n), lambda l: (l, 0, 0))],
        out_specs=pl.BlockSpec((1, D, n), lambda l: (l, 0, 0)),
        out_shape=jax.ShapeDtypeStruct((depth, D, n), F32),
        compiler_params=_params(("arbitrary",)), name="wmod_grad")(c_rows, dmod_cols)


class _Lay:
    def __init__(self, batch, t_ctx, t_lat):
        self.b, self.t_ctx, self.t_lat = batch, t_ctx, t_lat
        self.s = t_ctx + t_lat
        self.tm = min(256, t_ctx)
        self.tpb = self.s // self.tm
        self.nct = t_ctx // self.tm
        self.ntiles = batch * self.tpb
        self.rows = batch * self.s

    def mod_row(self, i):
        return jnp.where(i % self.tpb < self.nct, 2, i // self.tpb)

    def group(self, i):
        return 2 * (i // self.tpb) + jnp.where(i % self.tpb < self.nct, 0, 1)

    def group_first(self, i):
        return (i % self.tpb == 0) | (i % self.tpb == self.nct)


def _norm_mod(x, g, shift, scale):
    r = lax.rsqrt(jnp.mean(x * x, axis=-1, keepdims=True) + EPS)
    return (x * r * g) * (1.0 + scale) + shift


def _inproj_fwd(lay, xc, mod3, gpre, w):
    tm = lay.tm

    def body(x_ref, sh_ref, sc_ref, g_ref, w_ref, p_ref, ht_ref):
        h = _norm_mod(x_ref[...], g_ref[...], sh_ref[0, 0], sc_ref[0, 0])
        ht_ref[...] = h.T.astype(BF16)
        p_ref[...] = jnp.dot(h.astype(BF16), w_ref[...], preferred_element_type=F32)

    return pl.pallas_call(
        body, grid=(lay.ntiles,),
        in_specs=[pl.BlockSpec((tm, D), lambda i: (i, 0)),
                  pl.BlockSpec((1, 1, 1, D), lambda i: (0, lay.mod_row(i), 0, 0)),
                  pl.BlockSpec((1, 1, 1, D), lambda i: (1, lay.mod_row(i), 0, 0)),
                  pl.BlockSpec((1, D), lambda i: (0, 0)),
                  pl.BlockSpec((D, W_PAD), lambda i: (0, 0))],
        out_specs=[pl.BlockSpec((tm, W_PAD), lambda i: (i, 0)), pl.BlockSpec((D, tm), lambda i: (0, i))],
        out_shape=[jax.ShapeDtypeStruct((lay.rows, W_PAD), F32), jax.ShapeDtypeStruct((D, lay.rows), BF16)],
        compiler_params=_params(("arbitrary",)), name="inproj_fwd")(xc, mod3, mod3, gpre, w)


def _inproj_bwd(lay, xc, mod3, gpre, wt, dxc, pieces):
    tm = lay.tm
    npc = len(pieces)
    offs = [off for _, off in pieces]

    def body(*refs):
        x_ref, sh_ref, sc_ref, g_ref, wt_ref, dx_in = refs[:6]
        dps = refs[6:6 + npc]
        dx_ref, dg_ref, dsh_ref, dsc_ref = refs[6 + npc:]
        i = pl.program_id(0)
        dh = None
        for dp_ref, off in zip(dps, offs):
            wd = dp_ref.shape[1]
            part = jnp.dot(dp_ref[...], wt_ref[off:off + wd, :], preferred_element_type=F32)
            dh = part if dh is None else dh + part
        _, vjp = jax.vjp(_norm_mod, x_ref[...], g_ref[...], sh_ref[0, 0], sc_ref[0, 0])
        dx, dg, dsh, dsc = vjp(dh)
        dx_ref[...] = dx_in[...] + dx
        _acc(dg_ref, dg, i == 0)
        first = lay.group_first(i)
        _acc(dsh_ref, dsh, first, at=(0,))
        _acc(dsc_ref, dsc, first, at=(0,))

    return pl.pallas_call(
        body, grid=(lay.ntiles,),
        in_specs=[pl.BlockSpec((tm, D), lambda i: (i, 0)),
                  pl.BlockSpec((1, 1, 1, D), lambda i: (0, lay.mod_row(i), 0, 0)),
                  pl.BlockSpec((1, 1, 1, D), lambda i: (1, lay.mod_row(i), 0, 0)),
                  pl.BlockSpec((1, D), lambda i: (0, 0)),
                  pl.BlockSpec((W_PAD, D), lambda i: (0, 0)),
                  pl.BlockSpec((tm, D), lambda i: (i, 0))]
        + [pl.BlockSpec((tm, dp.shape[1]), lambda i: (i, 0)) for dp, _ in pieces],
        out_specs=[pl.BlockSpec((tm, D), lambda i: (i, 0)),
                   pl.BlockSpec((1, D), lambda i: (0, 0)),
                   pl.BlockSpec((1, 1, D), lambda i: (lay.group(i), 0, 0)),
                   pl.BlockSpec((1, 1, D), lambda i: (lay.group(i), 0, 0))],
        out_shape=[jax.ShapeDtypeStruct((lay.rows, D), F32), jax.ShapeDtypeStruct((1, D), F32),
                   jax.ShapeDtypeStruct((2 * lay.b, 1, D), F32), jax.ShapeDtypeStruct((2 * lay.b, 1, D), F32)],
        compiler_params=_params(("arbitrary",)), name="inproj_bwd",
    )(xc, mod3, mod3, gpre, wt, dxc, *[dp for dp, _ in pieces])


def _weight_grad(lay, ht, dp, name):
    wd = dp.shape[1]
    tn = 512 if wd % 512 == 0 else (256 if wd % 256 == 0 else LANES)
    tr = lay.rows // 3 if lay.rows % (3 * 256) == 0 else lay.tm

    def body(ht_ref, dp_ref, o_ref):
        _acc(o_ref, jnp.dot(ht_ref[...], dp_ref[...], preferred_element_type=F32), pl.program_id(1) == 0)

    return pl.pallas_call(
        body, grid=(wd // tn, lay.rows // tr),
        in_specs=[pl.BlockSpec((D, tr), lambda j, i: (0, i)), pl.BlockSpec((tr, tn), lambda j, i: (i, j))],
        out_specs=pl.BlockSpec((D, tn), lambda j, i: (0, j)),
        out_shape=jax.ShapeDtypeStruct((D, wd), F32),
        compiler_params=_params(("arbitrary", "arbitrary")), name=name)(ht, dp)


def _outproj_post(o, x, gpost, gate):
    r = lax.rsqrt(jnp.mean(o * o, axis=-1, keepdims=True) + EPS)
    return x + gate * (o * r * gpost)


def _outproj_matmul(ys, w_ref):
    o = None
    for k, y in enumerate(ys):
        part = jnp.dot(y[...], w_ref[BRW * k:BRW * (k + 1), :], preferred_element_type=F32)
        o = part if o is None else o + part
    return o


def _outproj_specs(lay):
    tm = lay.tm
    return ([pl.BlockSpec((tm, BRW), lambda i: (i, 0))] * 4
            + [pl.BlockSpec((tm, D), lambda i: (i, 0))]
            + [pl.BlockSpec((D, D), lambda i: (0, 0))]
            + [pl.BlockSpec((1, D), lambda i: (0, 0))]
            + [pl.BlockSpec((1, 1, 1, D), lambda i: (2, lay.mod_row(i), 0, 0))])


def _outproj_fwd(lay, ys, xc, wout, gpost, mod3):
    tm = lay.tm

    def body(y0, y1, y2, y3, x_ref, w_ref, g_ref, gt_ref, o_ref, yt_ref):
        ys_ = (y0, y1, y2, y3)
        o_ref[...] = _outproj_post(_outproj_matmul(ys_, w_ref), x_ref[...], g_ref[...], gt_ref[0, 0])
        for k, y in enumerate(ys_):
            yt_ref[BRW * k:BRW * (k + 1), :] = y[...].astype(F32).T.astype(BF16)

    return pl.pallas_call(
        body, grid=(lay.ntiles,), in_specs=_outproj_specs(lay),
        out_specs=[pl.BlockSpec((tm, D), lambda i: (i, 0)), pl.BlockSpec((D, tm), lambda i: (0, i))],
        out_shape=[jax.ShapeDtypeStruct((lay.rows, D), F32), jax.ShapeDtypeStruct((D, lay.rows), BF16)],
        compiler_params=_params(("arbitrary",)), name="outproj_fwd")(*ys, xc, wout, gpost, mod3)


def _outproj_bwd(lay, ys, xc, wout, gpost, mod3, dxc):
    tm = lay.tm

    def body(y0, y1, y2, y3, x_ref, w_ref, g_ref, gt_ref, dx_ref, d0, d1, d2, d3, do_ref, dg_ref, dgt_ref):
        i = pl.program_id(0)
        o = _outproj_matmul((y0, y1, y2, y3), w_ref)
        _, vjp = jax.vjp(_outproj_post, o, x_ref[...], g_ref[...], gt_ref[0, 0])
        do, _, dg, dgt = vjp(dx_ref[...])
        do = do.astype(BF16)
        do_ref[...] = do
        for k, d in enumerate((d0, d1, d2, d3)):
            d[...] = _bdot(do, w_ref[BRW * k:BRW * (k + 1), :], 1, 1)
        _acc(dg_ref, dg, i == 0)
        _acc(dgt_ref, dgt, lay.group_first(i), at=(0,))

    return pl.pallas_call(
        body, grid=(lay.ntiles,),
        in_specs=_outproj_specs(lay) + [pl.BlockSpec((tm, D), lambda i: (i, 0))],
        out_specs=[pl.BlockSpec((tm, BRW), lambda i: (i, 0))] * 4
        + [pl.BlockSpec((tm, D), lambda i: (i, 0)), pl.BlockSpec((1, D), lambda i: (0, 0)),
           pl.BlockSpec((1, 1, D), lambda i: (lay.group(i), 0, 0))],
        out_shape=[jax.ShapeDtypeStruct((lay.rows, BRW), F32)] * 4
        + [jax.ShapeDtypeStruct((lay.rows, D), BF16), jax.ShapeDtypeStruct((1, D), F32),
           jax.ShapeDtypeStruct((2 * lay.b, 1, D), F32)],
        compiler_params=_params(("arbitrary",)), name="outproj_bwd")(*ys, xc, wout, gpost, mod3, dxc)


def _loss_kernel(lay, xc3, target):
    tm, nct = lay.tm, lay.nct

    def body(x_ref, t_ref, loss_ref, dx_ref):
        b, i = pl.program_id(0), pl.program_id(1)
        lat = i >= nct
        err = x_ref[0] - t_ref[0]
        dx_ref[0] = jnp.where(lat, err * (1.0 / D), 0.0)
        part = jnp.sum(jnp.sum(err * err, axis=1, keepdims=True), axis=0, keepdims=True) * (0.5 / D)
        part = jnp.broadcast_to(jnp.where(lat, part, 0.0), (8, LANES))
        _acc(loss_ref, part, (b == 0) & (i == 0))

    return pl.pallas_call(
        body, grid=(lay.b, lay.tpb),
        in_specs=[pl.BlockSpec((1, tm, D), lambda b, i: (b, i, 0)),
                  pl.BlockSpec((1, tm, D), lambda b, i: (b, jnp.maximum(i - nct, 0), 0))],
        out_specs=[pl.BlockSpec((8, LANES), lambda b, i: (0, 0)), pl.BlockSpec((1, tm, D), lambda b, i: (b, i, 0))],
        out_shape=[jax.ShapeDtypeStruct((8, LANES), F32), jax.ShapeDtypeStruct(xc3.shape, F32)],
        compiler_params=_params(("arbitrary", "arbitrary")), name="loss")(xc3, target)


def _chunk_orders(n_ctx, n_all):
    fwd = list(range(n_all))
    rev = list(range(n_ctx - 1, -1, -1)) + list(range(n_all - 1, n_ctx - 1, -1))
    return fwd, rev


def _ret_state_fn(k, v, cos, sin):
    kt = _rotary(k, cos, sin) * (HD ** -0.5)
    lg = _lane_by_head(LOG_GAMMA)
    j = _iota((RC, 1), 0).astype(F32)
    bd = _block_diag(BRW, BRW)
    af = mm_tn(kt * jnp.exp((RC - 1.0 - j) * lg), v) * bd
    ar = mm_tn(kt * jnp.exp(j * lg), v) * bd
    return af, ar


def _ret_out_fn(q, k, v, z, cos, sin, sf, sr, ng):
    qt = _rotary(q, cos, sin)
    kt = _rotary(k, cos, sin) * (HD ** -0.5)
    diff = (_iota((RC, RC), 0) - _iota((RC, RC), 1)).astype(F32)
    o = None
    for h in range(NH):
        m = _head_mask(h)
        sc = mm_nt(qt * m, kt)
        wgt = sc * jnp.exp(jnp.abs(diff) * LOG_GAMMA[h]) * jnp.where(diff == 0, 2.0, 1.0)
        part = mm(wgt, v * m)
        o = part if o is None else o + part
    lg = _lane_by_head(LOG_GAMMA)
    i = _iota((RC, 1), 0).astype(F32)
    o = o + mm(qt, sf) * jnp.exp((i + 1.0) * lg) + mm(qt, sr) * jnp.exp((RC - i) * lg)
    mu = _head_sum(o) * (1.0 / HD)
    cen = o - mu
    var = _head_sum(cen * cen) * (1.0 / HD)
    return cen * lax.rsqrt(var + EPS) * ng * _silu(z)


def _ret_chunks(lay):
    nc = lay.s // RC
    return 6 if nc % 6 == 0 else (2 if nc % 2 == 0 else 1)


def _ret_specs(lay, cols):
    rows = _ret_chunks(lay) * RC
    return [pl.BlockSpec((1, rows, BRW), functools.partial(lambda b, i, c: (b, i, c), c=COL_RET + c)) for c in cols]


def _ret_state(lay, p3, cos, sin):
    nc, per = lay.s // RC, _ret_chunks(lay)

    def body(k_ref, v_ref, c_ref, s_ref, a_ref):
        for n in range(per):
            rows = pl.ds(RC * n, RC)
            af, ar = _ret_state_fn(k_ref[0, rows, :], v_ref[0, rows, :], c_ref[rows, :], s_ref[rows, :])
            a_ref[0, n, 0] = af
            a_ref[0, n, 1] = ar

    tab = pl.BlockSpec((per * RC, BRW), lambda b, i: (i, 0))
    return pl.pallas_call(
        body, grid=(lay.b, nc // per), in_specs=_ret_specs(lay, (1, 2)) + [tab, tab],
        out_specs=pl.BlockSpec((1, per, 2, BRW, BRW), lambda b, i: (b, i, 0, 0, 0)),
        out_shape=jax.ShapeDtypeStruct((lay.b, nc, 2, BRW, BRW), F32),
        compiler_params=_params(("arbitrary", "arbitrary")), name="ret_state")(p3, p3, cos, sin)


def _ret_state_bwd(lay, p3, cos, sin, d_a, dpr):
    nc, per = lay.s // RC, _ret_chunks(lay)

    def body(k_ref, v_ref, c_ref, s_ref, da_ref, dpr_ref, o_ref):
        for n in range(per):
            rows = pl.ds(RC * n, RC)
            cos_, sin_ = c_ref[rows, :], s_ref[rows, :]
            _, vjp = jax.vjp(lambda k, v: _ret_state_fn(k, v, cos_, sin_), k_ref[0, rows, :], v_ref[0, rows, :])
            dk, dv = vjp((da_ref[0, n, 0], da_ref[0, n, 1]))
            o_ref[0, rows, 0:BRW] = dpr_ref[0, rows, 0:BRW].astype(BF16)
            o_ref[0, rows, BRW:2 * BRW] = (dpr_ref[0, rows, BRW:2 * BRW] + dk).astype(BF16)
            o_ref[0, rows, 2 * BRW:3 * BRW] = (dpr_ref[0, rows, 2 * BRW:3 * BRW] + dv).astype(BF16)
            o_ref[0, rows, 3 * BRW:] = dpr_ref[0, rows, 3 * BRW:].astype(BF16)

    tab = pl.BlockSpec((per * RC, BRW), lambda b, i: (i, 0))
    return pl.pallas_call(
        body, grid=(lay.b, nc // per),
        in_specs=_ret_specs(lay, (1, 2)) + [tab, tab,
                                            pl.BlockSpec((1, per, 2, BRW, BRW), lambda b, i: (b, i, 0, 0, 0)),
                                            pl.BlockSpec((1, per * RC, 4 * BRW), lambda b, i: (b, i, 0))],
        out_specs=pl.BlockSpec((1, per * RC, 4 * BRW), lambda b, i: (b, i, 0)),
        out_shape=jax.ShapeDtypeStruct((lay.b, lay.s, 4 * BRW), BF16),
        compiler_params=_params(("arbitrary", "arbitrary")), name="ret_state_bwd")(p3, p3, cos, sin, d_a, dpr)


def _state_scan(lay, a, nc_ctx, transpose, name):
    b, nc = a.shape[0], a.shape[1]
    orders = _chunk_orders(nc_ctx, nc)

    def body(a_ref, o_ref):
        d, jh = pl.program_id(1), pl.program_id(2)
        head = (_iota((1, LANES), 1) + jh * LANES) // HD
        lg = jnp.full((1, LANES), LOG_GAMMA[NH - 1], F32)
        for h in range(NH - 2, -1, -1):
            lg = jnp.where(head == h, LOG_GAMMA[h], lg)
        dec = jnp.exp(RC * lg)
        for dd in (0, 1):
            @pl.when(d == dd)
            def _(order=orders[dd]):
                acc = jnp.zeros((BRW, LANES), F32)
                if not transpose:
                    for c in order:
                        o_ref[0, c, 0] = acc
                        acc = acc * dec + a_ref[0, c, 0]
                else:
                    for c in reversed(order):
                        o_ref[0, c, 0] = acc
                        acc = a_ref[0, c, 0] + acc * dec

    spec = pl.BlockSpec((1, nc, 1, BRW, LANES), lambda bb, d, jh: (bb, 0, d, 0, jh))
    return pl.pallas_call(
        body, grid=(b, 2, BRW // LANES), in_specs=[spec], out_specs=spec,
        out_shape=jax.ShapeDtypeStruct(a.shape, F32),
        compiler_params=_params(("arbitrary",) * 3), name=name)(a)


def _ret_out(lay, p3, cos, sin, states, ng):
    nc, per = lay.s // RC, _ret_chunks(lay)

    def body(q_ref, k_ref, v_ref, z_ref, c_ref, s_ref, st_ref, ng_ref, y_ref):
        for n in range(per):
            rows = pl.ds(RC * n, RC)
            y = _ret_out_fn(q_ref[0, rows, :], k_ref[0, rows, :], v_ref[0, rows, :], z_ref[0, rows, :],
                            c_ref[rows, :], s_ref[rows, :], st_ref[0, n, 0], st_ref[0, n, 1], ng_ref[...])
            y_ref[0, rows, :] = y.astype(BF16)

    tab = pl.BlockSpec((per * RC, BRW), lambda b, i: (i, 0))
    return pl.pallas_call(
        body, grid=(lay.b, nc // per),
        in_specs=_ret_specs(lay, (0, 1, 2, 3)) + [tab, tab,
                                                  pl.BlockSpec((1, per, 2, BRW, BRW), lambda b, i: (b, i, 0, 0, 0)),
                                                  pl.BlockSpec((1, BRW), lambda b, i: (0, 0))],
        out_specs=pl.BlockSpec((1, per * RC, BRW), lambda b, i: (b, i, 0)),
        out_shape=jax.ShapeDtypeStruct((lay.b, lay.s, BRW), BF16),
        compiler_params=_params(("arbitrary", "arbitrary")), name="ret_out")(p3, p3, p3, p3, cos, sin, states, ng)


def _ret_out_bwd(lay, p3, cos, sin, states, ng, dy):
    nc, per = lay.s // RC, _ret_chunks(lay)

    def body(q_ref, k_ref, v_ref, z_ref, c_ref, s_ref, st_ref, ng_ref, dy_ref, dp_ref, dst_ref, dng_ref):
        b, i = pl.program_id(0), pl.program_id(1)
        dng_sum = None
        for n in range(per):
            rows = pl.ds(RC * n, RC)
            cos_, sin_ = c_ref[rows, :], s_ref[rows, :]
            fn = lambda q, k, v, z, sf, sr, ng: _ret_out_fn(q, k, v, z, cos_, sin_, sf, sr, ng)
            _, vjp = jax.vjp(fn, q_ref[0, rows, :], k_ref[0, rows, :], v_ref[0, rows, :], z_ref[0, rows, :],
                             st_ref[0, n, 0], st_ref[0, n, 1], ng_ref[...])
            dq, dk, dv, dz, dsf, dsr, dng = vjp(dy_ref[0, rows, :])
            for m, g in enumerate((dq, dk, dv, dz)):
                dp_ref[0, rows, BRW * m:BRW * (m + 1)] = g
            dst_ref[0, n, 0] = dsf
            dst_ref[0, n, 1] = dsr
            dng_sum = dng if dng_sum is None else dng_sum + dng
        _acc(dng_ref, dng_sum, (b == 0) & (i == 0))

    tab = pl.BlockSpec((per * RC, BRW), lambda b, i: (i, 0))
    st = pl.BlockSpec((1, per, 2, BRW, BRW), lambda b, i: (b, i, 0, 0, 0))
    return pl.pallas_call(
        body, grid=(lay.b, nc // per),
        in_specs=_ret_specs(lay, (0, 1, 2, 3)) + [tab, tab, st, pl.BlockSpec((1, BRW), lambda b, i: (0, 0)),
                                                  pl.BlockSpec((1, per * RC, BRW), lambda b, i: (b, i, 0))],
        out_specs=[pl.BlockSpec((1, per * RC, 4 * BRW), lambda b, i: (b, i, 0)), st,
                   pl.BlockSpec((1, BRW), lambda b, i: (0, 0))],
        out_shape=[jax.ShapeDtypeStruct((lay.b, lay.s, 4 * BRW), F32),
                   jax.ShapeDtypeStruct(states.shape, F32), jax.ShapeDtypeStruct((1, BRW), F32)],
        compiler_params=_params(("arbitrary", "arbitrary")), name="ret_out_bwd",
    )(p3, p3, p3, p3, cos, sin, states, ng, dy)


def _sg_fn(u, v, z, w, b8):
    ug = jax.nn.gelu(u)
    vg = jax.nn.gelu(v)
    mu = jnp.mean(vg, axis=-1, keepdims=True)
    cen = vg - mu
    vn = cen * lax.rsqrt(jnp.mean(cen * cen, axis=-1, keepdims=True) + EPS)
    masks = (_iota((NH, 1, BRW), 2) // HD == _iota((NH, 1, BRW), 0)).astype(F32)
    s = jnp.sum(mm(w, vn[None] * masks), axis=0)
    expand = (_iota((8, BRW), 1) // HD == _iota((8, BRW), 0)).astype(F32)
    bias = lax.dot_general(b8, expand, (((0,), (0,)), ((), ())), precision=HI, preferred_element_type=F32)
    return ug * (s + bias) * _silu(z)


def _sg_chunks(lay):
    nc = lay.s // RC
    return 9 if nc % 9 == 0 else (6 if nc % 6 == 0 else (2 if nc % 2 == 0 else 1))


def _sg_specs(lay):
    rows = _sg_chunks(lay) * RC
    return ([pl.BlockSpec((1, rows, BRW), functools.partial(lambda b, i, c: (b, i, c), c=COL_SG + c)) for c in range(3)]
            + [pl.BlockSpec((NH, RC, RC), lambda b, i: (0, 0, 0)), pl.BlockSpec((8, RC), lambda b, i: (0, 0))])


def _sg_fwd(lay, p3, sgw, sgb8):
    per = _sg_chunks(lay)

    def body(u_ref, v_ref, z_ref, w_ref, b_ref, y_ref):
        for k in range(per):
            rows = pl.ds(RC * k, RC)
            y = _sg_fn(u_ref[0, rows, :], v_ref[0, rows, :], z_ref[0, rows, :], w_ref[...], b_ref[...])
            y_ref[0, rows, :] = y.astype(BF16)

    return pl.pallas_call(
        body, grid=(lay.b, lay.s // (per * RC)), in_specs=_sg_specs(lay),
        out_specs=pl.BlockSpec((1, per * RC, BRW), lambda b, i: (b, i, 0)),
        out_shape=jax.ShapeDtypeStruct((lay.b, lay.s, BRW), BF16),
        compiler_params=_params(("arbitrary", "arbitrary")), name="sg_fwd")(p3, p3, p3, sgw, sgb8)


def _sg_bwd(lay, p3, sgw, sgb8, dy):
    per = _sg_chunks(lay)

    def body(u_ref, v_ref, z_ref, w_ref, b_ref, dy_ref, dp_ref, dw_ref, db_ref):
        first = (pl.program_id(0) == 0) & (pl.program_id(1) == 0)
        dw = db = None
        for k in range(per):
            rows = pl.ds(RC * k, RC)
            _, vjp = jax.vjp(_sg_fn, u_ref[0, rows, :], v_ref[0, rows, :], z_ref[0, rows, :], w_ref[...], b_ref[...])
            g = vjp(dy_ref[0, rows, :])
            for n in range(3):
                dp_ref[0, rows, BRW * n:BRW * (n + 1)] = g[n].astype(BF16)
            dw = g[3] if dw is None else dw + g[3]
            db = g[4] if db is None else db + g[4]
        _acc(dw_ref, dw, first)
        _acc(db_ref, db, first)

    return pl.pallas_call(
        body, grid=(lay.b, lay.s // (per * RC)),
        in_specs=_sg_specs(lay) + [pl.BlockSpec((1, per * RC, BRW), lambda b, i: (b, i, 0))],
        out_specs=[pl.BlockSpec((1, per * RC, 3 * BRW), lambda b, i: (b, i, 0)),
                   pl.BlockSpec((NH, RC, RC), lambda b, i: (0, 0, 0)), pl.BlockSpec((8, RC), lambda b, i: (0, 0))],
        out_shape=[jax.ShapeDtypeStruct((lay.b, lay.s, 3 * BRW), BF16),
                   jax.ShapeDtypeStruct((NH, RC, RC), F32), jax.ShapeDtypeStruct((8, RC), F32)],
        compiler_params=_params(("arbitrary", "arbitrary")), name="sg_bwd")(p3, p3, p3, sgw, sgb8, dy)


def _sc_specs(lay):
    first = COL_SC * BRW // LANES
    blk = [pl.BlockSpec((1, lay.s, LANES), functools.partial(lambda j, b, c: (b, 0, c + j), c=first + 2 * n))
           for n in range(4)]
    return blk + [pl.BlockSpec((8, LANES), lambda j, b: (0, j))]


def _sc_fwd(lay, p3, w8):
    dn, up = _make_shifts(lay.s, lay.t_ctx)

    def fn(b_, c_, h_, z_, w0, w1, w2):
        return b_ * _conv3(c_ * h_, w0, w1, w2, dn, up) * _silu(z_)

    def body(b_ref, c_ref, h_ref, z_ref, w_ref, y_ref):
        y = fn(b_ref[0], c_ref[0], h_ref[0], z_ref[0], w_ref[0:1, :], w_ref[1:2, :], w_ref[2:3, :])
        y_ref[0] = y.astype(BF16)

    return pl.pallas_call(
        body, grid=(BRW // LANES, lay.b), in_specs=_sc_specs(lay),
        out_specs=pl.BlockSpec((1, lay.s, LANES), lambda j, b: (b, 0, j)),
        out_shape=jax.ShapeDtypeStruct((lay.b, lay.s, BRW), BF16),
        compiler_params=_params(("arbitrary", "arbitrary")), name="sc_fwd")(p3, p3, p3, p3, w8)


def _sc_bwd(lay, p3, w8, dy):
    dn, up = _make_shifts(lay.s, lay.t_ctx)

    def fn(b_, c_, h_, z_, w0, w1, w2):
        return b_ * _conv3(c_ * h_, w0, w1, w2, dn, up) * _silu(z_)

    def body(b_ref, c_ref, h_ref, z_ref, w_ref, dy_ref, db_ref, dc_ref, dh_ref, dz_ref, dw_ref):
        _, vjp = jax.vjp(fn, b_ref[0], c_ref[0], h_ref[0], z_ref[0], w_ref[0:1, :], w_ref[1:2, :], w_ref[2:3, :])
        g = vjp(dy_ref[0])
        for ref, val in zip((db_ref, dc_ref, dh_ref, dz_ref), g[:4]):
            ref[0] = val.astype(BF16)
        dw = jnp.concatenate([g[4], g[5], g[6], jnp.zeros((5, LANES), F32)], axis=0)
        _acc(dw_ref, dw, pl.program_id(1) == 0)

    out = pl.BlockSpec((1, lay.s, LANES), lambda j, b: (b, 0, j))
    return pl.pallas_call(
        body, grid=(BRW // LANES, lay.b), in_specs=_sc_specs(lay) + [out],
        out_specs=[out] * 4 + [pl.BlockSpec((8, LANES), lambda j, b: (0, j))],
        out_shape=[jax.ShapeDtypeStruct((lay.b, lay.s, BRW), BF16)] * 4 + [jax.ShapeDtypeStruct((8, BRW), F32)],
        compiler_params=_params(("arbitrary", "arbitrary")), name="sc_bwd")(p3, p3, p3, p3, w8, dy)


def _gdn_conv_fn(x, w0, w1, w2, normed, dn, up):
    a = _silu(_conv3(x, w0, w1, w2, dn, up))
    nrm = a * lax.rsqrt(_head_sum(a * a) + EPS)
    return jnp.where(normed, nrm, a)


def _gdn_conv(lay, p3, w8):
    dn, up = _make_shifts(lay.s, lay.t_ctx)
    first = COL_GDN * BRW // LANES

    def body(x_ref, w_ref, o_ref):
        normed = pl.program_id(0) < 2 * BRW // LANES
        o_ref[0] = _gdn_conv_fn(x_ref[0], w_ref[0:1, :], w_ref[1:2, :], w_ref[2:3, :], normed, dn, up)

    return pl.pallas_call(
        body, grid=(3 * BRW // LANES, lay.b),
        in_specs=[pl.BlockSpec((1, lay.s, LANES), lambda j, b: (b, 0, first + j)),
                  pl.BlockSpec((8, LANES), lambda j, b: (0, j))],
        out_specs=pl.BlockSpec((1, lay.s, LANES), lambda j, b: (b, 0, j)),
        out_shape=jax.ShapeDtypeStruct((lay.b, lay.s, 3 * BRW), F32),
        compiler_params=_params(("arbitrary", "arbitrary")), name="gdn_conv")(p3, w8)


def _gdn_conv_bwd(lay, p3, w8, dqkv):
    dn, up = _make_shifts(lay.s, lay.t_ctx)
    first = COL_GDN * BRW // LANES

    def body(x_ref, w_ref, g_ref, dx_ref, dw_ref):
        normed = pl.program_id(0) < 2 * BRW // LANES
        fn = lambda x, w0, w1, w2: _gdn_conv_fn(x, w0, w1, w2, normed, dn, up)
        _, vjp = jax.vjp(fn, x_ref[0], w_ref[0:1, :], w_ref[1:2, :], w_ref[2:3, :])
        g = vjp(g_ref[0])
        dx_ref[0] = g[0].astype(BF16)
        dw = jnp.concatenate([g[1], g[2], g[3], jnp.zeros((5, LANES), F32)], axis=0)
        _acc(dw_ref, dw, pl.program_id(1) == 0)

    blk = pl.BlockSpec((1, lay.s, LANES), lambda j, b: (b, 0, j))
    return pl.pallas_call(
        body, grid=(3 * BRW // LANES, lay.b),
        in_specs=[pl.BlockSpec((1, lay.s, LANES), lambda j, b: (b, 0, first + j)),
                  pl.BlockSpec((8, LANES), lambda j, b: (0, j)), blk],
        out_specs=[blk, pl.BlockSpec((8, LANES), lambda j, b: (0, j))],
        out_shape=[jax.ShapeDtypeStruct((lay.b, lay.s, 3 * BRW), BF16), jax.ShapeDtypeStruct((8, 3 * BRW), F32)],
        compiler_params=_params(("arbitrary", "arbitrary")), name="gdn_conv_bwd")(p3, w8, dqkv)


def _tri_inverse(low):
    i, j = _iota(low.shape, low.ndim - 2), _iota(low.shape, low.ndim - 1) % GC
    t = (i == j).astype(F32)
    s = 1
    while s < GC:
        pair = (i // (2 * s)) == (j // (2 * s))
        off = pair & (((i // s) % 2) != ((j // s) % 2))
        cb = jnp.where(off, low, 0.0)
        t = t - (cb if s == 1 else _bdot(t, _stack_heads(_bdot(cb, _stack_heads(t), 1, 0)), 1, 0))
        s *= 2
    return t


@jax.custom_vjp
def _tri_solve(t, low, r1, r2):
    del low
    return _bdot(t, _stack_heads(r1), 1, 0), _bdot(t, _stack_heads(r2), 1, 0)


def _tri_solve_fwd(t, low, r1, r2):
    del low
    x1, x2 = _bdot(t, _stack_heads(r1), 1, 0), _bdot(t, _stack_heads(r2), 1, 0)
    return (x1, x2), (t, x1, x2)


def _tri_solve_bwd(res, g):
    t, x1, x2 = res
    bd = _block_diag(BRW, BRW)
    d1 = _unstack_heads(_bdot(t, g[0], 0, 0) * bd)
    d2 = _unstack_heads(_bdot(t, g[1], 0, 0) * bd)
    dlow = -(_bdot(d1, _stack_heads(x1), 1, 1) + _bdot(d2, _stack_heads(x2), 1, 1))
    return jnp.zeros_like(t), dlow, d1, d2


_tri_solve.defvjp(_tri_solve_fwd, _tri_solve_bwd)

N_PACK = 5


def _gdn_prep_fn(qn, kn, vv, a, alog, dtb, t=None):
    n = qn.shape[0]
    col = _iota((1, 1, LANES), 2)
    xx = a + dtb
    softplus = jnp.maximum(xx, 0.0) + jnp.log(1.0 + jnp.exp(-jnp.abs(xx)))
    g_small = jnp.where(col < 8, -jnp.exp(alog) * softplus, 0.0).reshape(n * GC, LANES)
    beta_small = jax.nn.sigmoid(a).reshape(n * GC, LANES)
    sel_col, sel_head = _iota((LANES, BRW), 0), _iota((LANES, BRW), 1) // HD
    g_l, b_l = [], []
    for d in (0, 1):
        g_l.append(_dotf(g_small, (sel_col == 4 * d + sel_head).astype(F32)))
        b_l.append(_dotf(beta_small, (sel_col == 8 + 4 * d + sel_head).astype(F32)))
    g_l = jnp.concatenate(g_l, axis=0).reshape(2 * n, GC, BRW)
    b_l = jnp.concatenate(b_l, axis=0).reshape(2 * n, GC, BRW)
    rev = _iota((2 * n, 1, 1), 0) >= n
    fwd = jnp.logical_not(rev)
    ri, ci = _iota((1, GC, GC), 1), _iota((1, GC, GC), 2)
    tri = ((fwd & (ri >= ci)) | (rev & (ri <= ci))).astype(F32)
    gc_l = lax.dot_general(tri, g_l, (((2,), (1,)), ((0,), (0,))), precision=HI,
                           preferred_element_type=F32)
    gtot_l = jnp.sum(g_l, axis=1, keepdims=True)
    i, j = _iota((1, GC, BRW), 1), _iota((1, GC, BRW), 2) % GC
    gc_t = jnp.sum(jnp.where(i == j, gc_l, 0.0), axis=1, keepdims=True)
    incl = (fwd & (i >= j)) | (rev & (i <= j))
    strict = (fwd & (i > j)) | (rev & (i < j))
    decay = jnp.where(incl, jnp.exp(jnp.where(incl, gc_l - gc_t, 0.0)), 0.0)
    kn2 = jnp.concatenate([kn, kn], axis=0)
    vv2 = jnp.concatenate([vv, vv], axis=0)
    qs = jnp.concatenate([qn, qn], axis=0) * (HD ** -0.5)
    kst = _stack_heads(kn2)
    kb = kn2 * b_l
    low = jnp.where(strict, mm_nt(kb, kst) * decay, 0.0)
    eg = jnp.exp(gc_l)
    t_inv = _tri_inverse(low) if t is None else t
    u, w = _tri_solve(t_inv, low, vv2 * b_l, kb * eg)
    k_tail = kn2 * jnp.exp(gtot_l - gc_l)
    intra = mm_nt(qs, kst) * decay
    return (u, w, k_tail, qs * eg, intra), jnp.exp(gtot_l), t_inv


def _prep_chunks(lay):
    nc = lay.s // GC
    return 6 if nc % 6 == 0 else (4 if nc % 4 == 0 else 2)


def _gdn_prep_specs(lay):
    rows = _prep_chunks(lay) * GC
    return ([pl.BlockSpec((1, rows, BRW), functools.partial(lambda b, i, c: (b, i, c), c=c)) for c in range(3)]
            + [pl.BlockSpec((1, rows, LANES), lambda b, i: (b, i, COL_A128)),
               pl.BlockSpec((8, LANES), lambda b, i: (0, 0))])


def _gdn_prep(lay, qkv, p3, prm):
    nc, per = lay.s // GC, _prep_chunks(lay)

    def body(q_ref, k_ref, v_ref, a_ref, prm_ref, pack_ref, cd_ref, t_ref):
        chunks = lambda ref: ref[0].reshape(per, GC, ref.shape[-1])
        pack, cd, t_inv = _gdn_prep_fn(chunks(q_ref), chunks(k_ref), chunks(v_ref), chunks(a_ref),
                                       prm_ref[0:1, :], prm_ref[1:2, :])
        for d in (0, 1):
            for n in range(N_PACK):
                pack_ref[0, :, d, n] = pack[n][per * d:per * (d + 1)]
            cd_ref[0, :, d] = cd[per * d:per * (d + 1)]
            t_ref[0, :, d] = t_inv[per * d:per * (d + 1)]

    return pl.pallas_call(
        body, grid=(lay.b, nc // per), in_specs=_gdn_prep_specs(lay),
        out_specs=[pl.BlockSpec((1, per, 2, N_PACK, GC, BRW), lambda b, i: (b, i, 0, 0, 0, 0)),
                   pl.BlockSpec((1, per, 2, 1, BRW), lambda b, i: (b, i, 0, 0, 0)),
                   pl.BlockSpec((1, per, 2, GC, BRW), lambda b, i: (b, i, 0, 0, 0))],
        out_shape=[jax.ShapeDtypeStruct((lay.b, nc, 2, N_PACK, GC, BRW), F32),
                   jax.ShapeDtypeStruct((lay.b, nc, 2, 1, BRW), F32),
                   jax.ShapeDtypeStruct((lay.b, nc, 2, GC, BRW), F32)],
        compiler_params=_params(("arbitrary", "arbitrary")), name="gdn_prep")(qkv, qkv, qkv, p3, prm)


def _gdn_prep_bwd(lay, qkv, p3, prm, dpacks, dcds, t_inv):
    nc, per = lay.s // GC, _prep_chunks(lay)

    def body(q_ref, k_ref, v_ref, a_ref, prm_ref, dpf_ref, dpr_ref, dcf_ref, dcr_ref, t_ref, dqkv_ref, da_ref,
             dprm_ref):
        first = (pl.program_id(0) == 0) & (pl.program_id(1) == 0)
        chunks = lambda ref: ref[0].reshape(per, GC, ref.shape[-1])
        t_inv = jnp.concatenate([t_ref[0, :, 0], t_ref[0, :, 1]], axis=0)
        fn = lambda q, k, v, a, alog, dtb: _gdn_prep_fn(q, k, v, a, alog, dtb, t_inv)[:2]
        _, vjp = jax.vjp(fn, chunks(q_ref), chunks(k_ref), chunks(v_ref), chunks(a_ref),
                         prm_ref[0:1, :], prm_ref[1:2, :])
        dpack = tuple(jnp.concatenate([dpf_ref[0, :, n], dpr_ref[0, :, n]], axis=0) for n in range(N_PACK))
        dq, dk, dv, da, dalog, ddtb = vjp((dpack, jnp.concatenate([dcf_ref[0], dcr_ref[0]], axis=0)))
        dqkv_ref[0, :, 0:BRW] = dq.reshape(per * GC, BRW)
        dqkv_ref[0, :, BRW:2 * BRW] = dk.reshape(per * GC, BRW)
        dqkv_ref[0, :, 2 * BRW:] = dv.reshape(per * GC, BRW)
        da_ref[0] = da.reshape(per * GC, LANES).astype(BF16)
        _acc(dprm_ref, jnp.concatenate([dalog, ddtb, jnp.zeros((6, LANES), F32)], axis=0), first)

    rows_blk = per * GC
    return pl.pallas_call(
        body, grid=(lay.b, nc // per),
        in_specs=_gdn_prep_specs(lay)
        + [pl.BlockSpec((1, per, N_PACK, GC, BRW), lambda b, i: (b, i, 0, 0, 0))] * 2
        + [pl.BlockSpec((1, per, 1, BRW), lambda b, i: (b, i, 0, 0))] * 2
        + [pl.BlockSpec((1, per, 2, GC, BRW), lambda b, i: (b, i, 0, 0, 0))],
        out_specs=[pl.BlockSpec((1, rows_blk, 3 * BRW), lambda b, i: (b, i, 0)),
                   pl.BlockSpec((1, rows_blk, LANES), lambda b, i: (b, i, 0)),
                   pl.BlockSpec((8, LANES), lambda b, i: (0, 0))],
        out_shape=[jax.ShapeDtypeStruct((lay.b, lay.s, 3 * BRW), F32),
                   jax.ShapeDtypeStruct((lay.b, lay.s, LANES), BF16), jax.ShapeDtypeStruct((8, LANES), F32)],
        compiler_params=_params(("arbitrary", "arbitrary")), name="gdn_prep_bwd",
    )(qkv, qkv, qkv, p3, prm, *dpacks, *dcds, t_inv)


def _gdn_step_fn(s, u, w, k_tail, qd, intra, cdec):
    v_new = u - mm(w, s)
    o = mm(qd, s) + mm(intra, _stack_heads(v_new))
    return s * cdec + mm_tn(k_tail, v_new) * _block_diag(BRW, BRW), o


def _order_index(nc_ctx, nc, d, step):
    rev = jnp.where(step < nc_ctx, nc_ctx - 1 - step, nc + nc_ctx - 1 - step)
    return jnp.where(d == 0, step, rev)


def _gdn_scan(lay, pack, cd):
    nc, nc_ctx = lay.s // GC, lay.t_ctx // GC
    chunk = functools.partial(_order_index, nc_ctx, nc)

    def body(pf_ref, pr_ref, cf_ref, cr_ref, of_ref, or_ref, sf_ref, sr_ref, s_scr):
        @pl.when(pl.program_id(0) == 0)
        def _():
            s_scr[...] = jnp.zeros_like(s_scr)

        nb = lay.b
        s = s_scr[...]
        st = _unstack_heads(s)
        sf_ref[:, 0] = st[:nb]
        sr_ref[:, 0] = st[nb:]
        args = [jnp.concatenate([pf_ref[:, 0, 0, n], pr_ref[:, 0, 0, n]], axis=0) for n in range(N_PACK)]
        s_new, o = _gdn_step_fn(s, *args, jnp.concatenate([cf_ref[:, 0, 0], cr_ref[:, 0, 0]], axis=0))
        of_ref[:, 0] = o[:nb]
        or_ref[:, 0] = o[nb:]
        s_scr[...] = s_new

    def pk(d):
        return pl.BlockSpec((lay.b, 1, 1, N_PACK, GC, BRW), lambda t: (0, chunk(d, t), d, 0, 0, 0))

    def cdb(d):
        return pl.BlockSpec((lay.b, 1, 1, 1, BRW), lambda t: (0, chunk(d, t), d, 0, 0))

    def out(d):
        return pl.BlockSpec((lay.b, 1, GC, BRW), lambda t: (0, chunk(d, t), 0, 0))

    return pl.pallas_call(
        body, grid=(nc,), in_specs=[pk(0), pk(1), cdb(0), cdb(1)],
        out_specs=[out(0), out(1), out(0), out(1)],
        out_shape=[jax.ShapeDtypeStruct((lay.b, nc, GC, BRW), F32)] * 4,
        scratch_shapes=[pltpu.VMEM((2 * lay.b, BRW, BRW), F32)],
        compiler_params=_params(("arbitrary",)), name="gdn_scan")(pack, pack, cd, cd)


def _gdn_scan_bwd(lay, pack, cd, states, do):
    nc, nc_ctx = lay.s // GC, lay.t_ctx // GC

    def chunk(d, t):
        return _order_index(nc_ctx, nc, d, nc - 1 - t)

    def body(pf_ref, pr_ref, cf_ref, cr_ref, sf_ref, sr_ref, dof_ref, dor_ref, dpf_ref, dpr_ref, dcf_ref, dcr_ref,
             ds_scr):
        @pl.when(pl.program_id(0) == 0)
        def _():
            ds_scr[...] = jnp.zeros_like(ds_scr)

        nb = lay.b
        both = lambda f, r: jnp.concatenate([f, r], axis=0)
        args = ([_stack_heads(both(sf_ref[:, 0], sr_ref[:, 0]))]
                + [both(pf_ref[:, 0, 0, n], pr_ref[:, 0, 0, n]) for n in range(N_PACK)]
                + [both(cf_ref[:, 0, 0], cr_ref[:, 0, 0])])
        _, vjp = jax.vjp(_gdn_step_fn, *args)
        g = vjp((ds_scr[...], both(dof_ref[...], dor_ref[...])))
        ds_scr[...] = g[0]
        for n in range(N_PACK):
            dpf_ref[:, 0, n] = g[1 + n][:nb]
            dpr_ref[:, 0, n] = g[1 + n][nb:]
        dcf_ref[:, 0] = g[1 + N_PACK][:nb]
        dcr_ref[:, 0] = g[1 + N_PACK][nb:]

    def pk(d):
        return pl.BlockSpec((lay.b, 1, 1, N_PACK, GC, BRW), lambda t: (0, chunk(d, t), d, 0, 0, 0))

    def cdb(d):
        return pl.BlockSpec((lay.b, 1, 1, 1, BRW), lambda t: (0, chunk(d, t), d, 0, 0))

    def st(d):
        return pl.BlockSpec((lay.b, 1, GC, BRW), lambda t: (0, chunk(d, t), 0, 0))

    def dob(d):
        return pl.BlockSpec((lay.b, GC, BRW), lambda t: (0, chunk(d, t), 0))

    def dpk(d):
        return pl.BlockSpec((lay.b, 1, N_PACK, GC, BRW), lambda t: (0, chunk(d, t), 0, 0, 0))

    def dcb(d):
        return pl.BlockSpec((lay.b, 1, 1, BRW), lambda t: (0, chunk(d, t), 0, 0))

    return pl.pallas_call(
        body, grid=(nc,),
        in_specs=[pk(0), pk(1), cdb(0), cdb(1), st(0), st(1), dob(0), dob(1)],
        out_specs=[dpk(0), dpk(1), dcb(0), dcb(1)],
        out_shape=[jax.ShapeDtypeStruct((lay.b, nc, N_PACK, GC, BRW), F32)] * 2
        + [jax.ShapeDtypeStruct((lay.b, nc, 1, BRW), F32)] * 2,
        scratch_shapes=[pltpu.VMEM((2 * lay.b, BRW, BRW), F32)],
        compiler_params=_params(("arbitrary",)), name="gdn_scan_bwd")(pack, pack, cd, cd, *states, do, do)


def _gdn_finish_fn(o, z, ng):
    return o * lax.rsqrt(_head_sum(o * o) * (1.0 / HD) + EPS) * ng * _silu(z)


def _finish_chunks(lay):
    nc = lay.s // GC
    return 12 if nc % 12 == 0 else (6 if nc % 6 == 0 else 2)


def _gdn_o(of_ref, or_ref):
    return (of_ref[0] + or_ref[0]).reshape(of_ref.shape[1] * GC, BRW)


def _gdn_finish_specs(lay):
    per = _finish_chunks(lay)
    ob = pl.BlockSpec((1, per, GC, BRW), lambda b, i: (b, i, 0, 0))
    return [ob, ob, pl.BlockSpec((1, per * GC, BRW), lambda b, i: (b, i, COL_GDN + 3)),
            pl.BlockSpec((1, BRW), lambda b, i: (0, 0))]


def _gdn_finish(lay, o_f, o_r, p3, ng):
    rows = _finish_chunks(lay) * GC

    def body(of_ref, or_ref, z_ref, ng_ref, y_ref):
        y_ref[0] = _gdn_finish_fn(_gdn_o(of_ref, or_ref), z_ref[0], ng_ref[...]).astype(BF16)

    return pl.pallas_call(
        body, grid=(lay.b, lay.s // rows), in_specs=_gdn_finish_specs(lay),
        out_specs=pl.BlockSpec((1, rows, BRW), lambda b, i: (b, i, 0)),
        out_shape=jax.ShapeDtypeStruct((lay.b, lay.s, BRW), BF16),
        compiler_params=_params(("arbitrary", "arbitrary")), name="gdn_finish")(o_f, o_r, p3, ng)


def _gdn_finish_bwd(lay, o_f, o_r, p3, ng, dy):
    rows = _finish_chunks(lay) * GC

    def body(of_ref, or_ref, z_ref, ng_ref, dy_ref, do_ref, dz_ref, dng_ref):
        first = (pl.program_id(0) == 0) & (pl.program_id(1) == 0)
        _, vjp = jax.vjp(_gdn_finish_fn, _gdn_o(of_ref, or_ref), z_ref[0], ng_ref[...])
        do, dz, dng = vjp(dy_ref[0])
        do_ref[0] = do
        dz_ref[0] = dz.astype(BF16)
        _acc(dng_ref, dng, first)

    blk = pl.BlockSpec((1, rows, BRW), lambda b, i: (b, i, 0))
    return pl.pallas_call(
        body, grid=(lay.b, lay.s // rows), in_specs=_gdn_finish_specs(lay) + [blk],
        out_specs=[blk, blk, pl.BlockSpec((1, BRW), lambda b, i: (0, 0))],
        out_shape=[jax.ShapeDtypeStruct((lay.b, lay.s, BRW), F32), jax.ShapeDtypeStruct((lay.b, lay.s, BRW), BF16),
                   jax.ShapeDtypeStruct((1, BRW), F32)],
        compiler_params=_params(("arbitrary", "arbitrary")), name="gdn_finish_bwd")(o_f, o_r, p3, ng, dy)


def _rope_tables(lay):
    t = jnp.arange(lay.t_lat)
    lane = np.arange(BRW)
    dim = lane % HD
    inv = jnp.asarray(ROPE_BASE ** (-(dim % 16).astype(np.float32) / 16.0), F32)
    pos = jnp.where((dim // 32 == 0)[None, :], (t // GRID_W)[:, None], (t % GRID_W)[:, None]).astype(F32)
    ang = pos * inv[None, :]
    cos = jnp.concatenate([jnp.ones((lay.t_ctx, BRW), F32), jnp.cos(ang)], axis=0)
    sin = jnp.concatenate([jnp.zeros((lay.t_ctx, BRW), F32), jnp.sin(ang)], axis=0)
    return cos, sin


def _pad_rows(a, rows):
    return jnp.concatenate([a, jnp.zeros((rows - a.shape[0],) + a.shape[1:], a.dtype)], axis=0)


def _layer_fwd(lay, xc, wl, cos, sin):
    p, ht = _inproj_fwd(lay, xc, wl["mod3"], wl["gpre"], wl["win"])
    p3 = p.reshape(lay.b, lay.s, W_PAD)
    nctx = lay.t_ctx // RC
    states = _state_scan(lay, _ret_state(lay, p3, cos, sin), nctx, False, "ret_scan")
    y_ret = _ret_out(lay, p3, cos, sin, states, wl["ret_ng"])
    y_sg = _sg_fwd(lay, p3, wl["sgw"], wl["sgb8"])
    y_sc = _sc_fwd(lay, p3, wl["scw8"])
    qkv = _gdn_conv(lay, p3, wl["gdnw8"])
    pack, cd, t_inv = _gdn_prep(lay, qkv, p3, wl["prm"])
    o_f, o_r, st_f, st_r = _gdn_scan(lay, pack, cd)
    y_gdn = _gdn_finish(lay, o_f, o_r, p3, wl["gdn_ng"])
    ys = [y.reshape(lay.rows, BRW) for y in (y_ret, y_sg, y_sc, y_gdn)]
    xc_new, yt = _outproj_fwd(lay, ys, xc, wl["wout"], wl["gpost"], wl["mod3"])
    saved = dict(xc=xc, p3=p3, ht=ht, yt=yt, states=states, qkv=qkv, pack=pack, cd=cd, t_inv=t_inv, o_f=o_f, o_r=o_r, gstates=(st_f, st_r),
                 ys=ys)
    return xc_new, saved


def _layer_bwd(lay, sv, wl, cos, sin, dxc):
    p3 = sv["p3"]
    as3 = lambda a: a.reshape(lay.b, lay.s, a.shape[-1])
    as2 = lambda a: a.reshape(lay.rows, a.shape[-1])
    dy_ret, dy_sg, dy_sc, dy_gdn, do_, dgpost, dgate = _outproj_bwd(
        lay, sv["ys"], sv["xc"], wl["wout"], wl["gpost"], wl["mod3"], dxc)
    dwout = _weight_grad(lay, sv["yt"], do_, "wout_grad")
    nctx = lay.t_ctx // RC
    dpr, dstates, dret_ng = _ret_out_bwd(lay, p3, cos, sin, sv["states"], wl["ret_ng"], as3(dy_ret))
    d_a = _state_scan(lay, dstates, nctx, True, "ret_scan_bwd")
    dp_ret = _ret_state_bwd(lay, p3, cos, sin, d_a, dpr)
    dp_sg, dsgw, dsgb8 = _sg_bwd(lay, p3, wl["sgw"], wl["sgb8"], as3(dy_sg))
    dsb, dsc_, dsh_, dsz, dscw8 = _sc_bwd(lay, p3, wl["scw8"], as3(dy_sc))
    do, dgz, dgdn_ng = _gdn_finish_bwd(lay, sv["o_f"], sv["o_r"], p3, wl["gdn_ng"], as3(dy_gdn))
    dpf, dpr_, dcf, dcr = _gdn_scan_bwd(lay, sv["pack"], sv["cd"], sv["gstates"], do)
    dqkv, da, dprm = _gdn_prep_bwd(lay, sv["qkv"], p3, wl["prm"], (dpf, dpr_), (dcf, dcr), sv["t_inv"])
    dp_gqkv, dgdnw8 = _gdn_conv_bwd(lay, p3, wl["gdnw8"], dqkv)
    pieces = [(as2(dp_ret), 0), (as2(dp_sg), COL_SG * BRW), (as2(dsb), COL_SC * BRW), (as2(dsc_), (COL_SC + 1) * BRW),
              (as2(dsh_), (COL_SC + 2) * BRW), (as2(dsz), (COL_SC + 3) * BRW), (as2(dp_gqkv), COL_GDN * BRW),
              (as2(dgz), (COL_GDN + 3) * BRW), (as2(da), COL_A128 * LANES)]
    dxc_prev, dgpre, dshift, dscale = _inproj_bwd(lay, sv["xc"], wl["mod3"], wl["gpre"], wl["wint"], dxc, pieces)
    dws = [_weight_grad(lay, sv["ht"], dp, "win_grad_%d" % off) for dp, off in pieces]
    dwin = jnp.concatenate(dws[:-1] + [dws[-1][:, :W_IN - COL_A128 * LANES]], axis=1)

    def rows3(g):
        return jnp.concatenate([g[1], g[3], g[0] + g[2]], axis=0)

    dmod = _pad_rows(jnp.concatenate([rows3(dshift), rows3(dscale), rows3(dgate)], axis=1), 8)
    grads = dict(win=dwin, wout=dwout, gpre=dgpre[0], gpost=dgpost[0], ret_ng=dret_ng[0], sgw=dsgw, sgb=dsgb8[:NH],
                 scw=dscw8[:3], gdnw=dgdnw8[:3], alog=dprm[0, :2 * NH].reshape(2, NH),
                 dtb=dprm[1, :2 * NH].reshape(2, NH), gdn_ng=dgdn_ng.reshape(NH, HD).sum(axis=0), dmod=dmod)
    return dxc_prev, grads


def _local_step(x, c, ctx, c_ctx, first, later, token, bmod, gpre, gpost, ret_ng, sgw, sgb, scw, gdnw, alog, dtb,
                gdn_ng, target):
    depth = bmod.shape[0]
    lay = _Lay(x.shape[0], ctx.shape[1], x.shape[1])
    assert lay.b == 2 and lay.t_ctx % RC == 0 and lay.t_lat % RC == 0
    cos, sin = _rope_tables(lay)
    cvec8 = _pad_rows(jnp.concatenate([c, c_ctx[None]], axis=0), 8) + token[0, 0]
    wmod = first[0]
    mod = _mod_fwd(cvec8, wmod, bmod[:1, None, :])
    xc = jnp.concatenate([ctx, x], axis=1).reshape(lay.rows, D)
    layers, saved = [], []
    for l in range(depth):
        if l == 1:
            rest = later(xc)
            wmod = jnp.concatenate([first[0], rest[0]], axis=0)
            mod = jnp.concatenate([mod, _mod_fwd(cvec8, rest[0], bmod[1:, None, :])], axis=0)
        win, wout = (first[1][0], first[2][0]) if l == 0 else (rest[1][l - 1], rest[2][l - 1])
        wl = dict(mod3=mod[l].reshape(8, 3, D).transpose(1, 0, 2)[:, :, None, :], gpre=gpre[l][None], gpost=gpost[l][None],
                  win=win, wint=jnp.swapaxes(win, 0, 1), wout=wout, ret_ng=ret_ng[l][None], sgw=sgw[l],
                  sgb8=_pad_rows(sgb[l], 8), scw8=_pad_rows(scw[l], 8), gdnw8=_pad_rows(gdnw[l], 8),
                  prm=_pad_rows(jnp.pad(jnp.stack([alog[l].reshape(-1), dtb[l].reshape(-1)]),
                                        ((0, 0), (0, LANES - 2 * NH))), 8),
                  gdn_ng=jnp.tile(gdn_ng[l], NH)[None])
        xc, sv = _layer_fwd(lay, xc, wl, cos, sin)
        layers.append(wl)
        saved.append(sv)
    loss, dxc3 = _loss_kernel(lay, xc.reshape(lay.b, lay.s, D), target)
    dxc = dxc3.reshape(lay.rows, D)
    grads = [None] * depth
    for l in reversed(range(depth)):
        dxc, grads[l] = _layer_bwd(lay, saved[l], layers[l], cos, sin, dxc)
    stacked = {k: jnp.stack([g[k] for g in grads]) for k in grads[0] if k not in ("win", "wout")}
    stacked["win"] = [g["win"] for g in grads]
    stacked["wout"] = [g["wout"] for g in grads]
    dcvec8, dbmod = _mod_bwd(stacked["dmod"], wmod, cvec8)
    stacked["bmod"] = dbmod[:, 0, :]
    stacked["c_ctx"] = dcvec8[2]
    dx = dxc.reshape(lay.b, lay.s, D)[:, lay.t_ctx:, :]
    return loss, dx, stacked, cvec8


MESH = pl.DeviceIdType.MESH
ANY = pl.BlockSpec(memory_space=pl.ANY)


def _me():
    return lax.axis_index("x"), lax.axis_index("y"), lax.axis_index("c")


def _gather_weights(shards, fulls, blocks):
    n = len(shards)

    def body(*refs):
        ins, outs = refs[:n], refs[n:2 * n]
        send_sems, recv_sems, loc_sems = refs[2 * n:]
        x, y, c = _me()
        me, sibling = (x, y, c), (x, y, 1 - c)
        chips = [(1 - x, y), (x, 1 - y), (1 - x, 1 - y)]

        def blk(a, dev):
            return blocks[a](outs[a], 4 * dev[0] + 2 * dev[1] + dev[2])

        def copy(a, k, block, to, src=None):
            return pltpu.make_async_remote_copy(
                src_ref=blk(a, block) if src is None else src, dst_ref=blk(a, block), send_sem=send_sems.at[a, k],
                recv_sem=recv_sems.at[a, k], device_id=to, device_id_type=MESH)

        mine = [pltpu.make_async_copy(ins[a], blk(a, me), loc_sems.at[a]) for a in range(n)]
        for cp in mine:
            cp.start()
        first = []
        for a in range(n):
            first.append(copy(a, 0, me, sibling, src=ins[a]))
            first += [copy(a, 1 + j, me, (*chip, c), src=ins[a]) for j, chip in enumerate(chips)]
        for cp in first:
            cp.start()
        passed = []
        for j, chip in enumerate(chips):
            for a in range(n):
                copy(a, 1 + j, (*chip, c), me).wait_recv()
                fwd = copy(a, 4 + j, (*chip, c), sibling)
                fwd.start()
                passed.append(fwd)
        for a in range(n):
            copy(a, 0, sibling, me).wait_recv()
            for j, chip in enumerate(chips):
                copy(a, 4 + j, (*chip, 1 - c), me).wait_recv()
        for cp in first + passed:
            cp.wait_send()
        for cp in mine:
            cp.wait()

    return pl.pallas_call(
        body, in_specs=[ANY] * n, out_specs=[ANY] * n,
        out_shape=[jax.ShapeDtypeStruct(f, s.dtype) for f, s in zip(fulls, shards)],
        scratch_shapes=[pltpu.SemaphoreType.DMA((n, 7)), pltpu.SemaphoreType.DMA((n, 7)),
                        pltpu.SemaphoreType.DMA((n,))],
        name="gather_weights")(*shards)


HBM = pl.BlockSpec(memory_space=pltpu.HBM)
SEM = pl.BlockSpec(memory_space=pltpu.SEMAPHORE)


def _peer(k, x, y, c):
    return (1 - x if k & 4 else x, 1 - y if k & 2 else y, 1 - c if k & 1 else c)


def _whole(ref, j):
    del j
    return ref


def _gather_start(shards, lands, blocks, name, parts=None):
    n = len(shards)
    parts = parts or [_whole] * n

    def body(*refs):
        ins, land = refs[:n], refs[n:2 * n]
        send_sems, recv_sems = refs[2 * n], refs[2 * n + 1]
        token = refs[-1]
        x, y, c = _me()
        me = 4 * x + 2 * y + c
        for a in range(n):
            for k in range(1, N_DEV):
                px, py, pc = _peer(k, x, y, c)
                pltpu.make_async_remote_copy(
                    src_ref=parts[a](ins[a], 4 * px + 2 * py + pc), dst_ref=blocks[a](land[a], me),
                    send_sem=send_sems.at[7 * a + k - 1], recv_sem=recv_sems.at[7 * a + k - 1],
                    device_id=(px, py, pc), device_id_type=MESH).start()
        token[...] = jnp.zeros_like(token)

    args = [pltpu.with_memory_space_constraint(a, pltpu.HBM) for a in list(shards) + list(lands)]
    out = pl.pallas_call(
        body, name=name,
        out_shape=[pltpu.SemaphoreType.DMA((7 * n,)), pltpu.SemaphoreType.DMA((7 * n,))]
        + [pltpu.HBM(a.shape, a.dtype) for a in args] + [jax.ShapeDtypeStruct((8, LANES), F32)],
        in_specs=[HBM] * (2 * n), out_specs=[SEM, SEM] + [HBM] * (2 * n) + [pl.BlockSpec(memory_space=pltpu.VMEM)],
        input_output_aliases={i: 2 + i for i in range(2 * n)},
        compiler_params=pltpu.CompilerParams(has_side_effects=pltpu.SideEffectType.DATAFLOW_SIDE_EFFECTING),
    )(*args)
    return out[0], out[1], out[2:2 + n], out[2 + n:2 + 2 * n], out[-1]


def _gather_wait(started, after, blocks, name, parts=None):
    send_sems, recv_sems, shards, lands, _ = started
    n = len(shards)
    parts = parts or [_whole] * n

    def body(*refs):
        ins, land = refs[:n], refs[n:2 * n]
        send_sems, recv_sems = refs[2 * n], refs[2 * n + 1]
        x, y, c = _me()
        for a in range(n):
            for k in range(1, N_DEV):
                px, py, pc = _peer(k, x, y, c)
                peer = 4 * px + 2 * py + pc
                cp = pltpu.make_async_remote_copy(
                    src_ref=parts[a](ins[a], peer), dst_ref=blocks[a](land[a], peer),
                    send_sem=send_sems.at[7 * a + k - 1], recv_sem=recv_sems.at[7 * a + k - 1],
                    device_id=(px, py, pc), device_id_type=MESH)
                cp.wait_send()
                cp.wait_recv()

    out = pl.pallas_call(
        body, name=name,
        out_shape=[pltpu.HBM(a.shape, a.dtype) for a in list(shards) + list(lands)],
        in_specs=[HBM] * (2 * n) + [SEM, SEM, ANY], out_specs=[HBM] * (2 * n),
        input_output_aliases={i: i for i in range(2 * n)},
        compiler_params=pltpu.CompilerParams(has_side_effects=pltpu.SideEffectType.DATAFLOW_SIDE_EFFECTING),
    )(*shards, *lands, send_sems, recv_sems, after)
    return out[:n], out[n:]


def _scatter_pair(srcs, slabs, slab_shapes):
    n = len(srcs)

    def body(*refs):
        ins, outs = refs[:n], refs[n:2 * n]
        send_sems, recv_sems = refs[2 * n:]
        x, y, c = _me()
        cps = []
        for a in range(n):
            for q in range(4):
                j = 2 * q + (1 - c)
                cps.append(pltpu.make_async_remote_copy(
                    src_ref=slabs[a](ins[a], j), dst_ref=outs[a].at[q], send_sem=send_sems.at[a, q],
                    recv_sem=recv_sems.at[a, q], device_id=(x, y, 1 - c), device_id_type=MESH))
        for cp in cps:
            cp.start()
        for cp in cps:
            cp.wait_recv()
        for cp in cps:
            cp.wait_send()

    return pl.pallas_call(
        body, in_specs=[ANY] * n, out_specs=[ANY] * n,
        out_shape=[jax.ShapeDtypeStruct((4,) + tuple(shp), s.dtype) for shp, s in zip(slab_shapes, srcs)],
        scratch_shapes=[pltpu.SemaphoreType.DMA((n, 4)), pltpu.SemaphoreType.DMA((n, 4))],
        name="scatter_pair")(*srcs)


def _scatter_chips(parts, small):
    n = len(parts)

    def body(*refs):
        ins, small_ref = refs[:n], refs[n]
        outs, all_ref = refs[n + 1:2 * n + 1], refs[2 * n + 1]
        send_sems, recv_sems, g_send, g_recv, loc_sem = refs[2 * n + 2:]
        x, y, c = _me()
        me = 4 * x + 2 * y + c
        chips = [(1 - x, y), (x, 1 - y), (1 - x, 1 - y)]
        cps = []
        for a in range(n):
            for k, (px, py) in enumerate(chips):
                cps.append(pltpu.make_async_remote_copy(
                    src_ref=ins[a].at[2 * px + py], dst_ref=outs[a].at[k], send_sem=send_sems.at[a, k],
                    recv_sem=recv_sems.at[a, k], device_id=(px, py, c), device_id_type=MESH))

        def gather(k, dst_blk, peer_xyz):
            return pltpu.make_async_remote_copy(
                src_ref=small_ref, dst_ref=all_ref.at[dst_blk], send_sem=g_send.at[k], recv_sem=g_recv.at[k],
                device_id=peer_xyz, device_id_type=MESH)

        local = pltpu.make_async_copy(small_ref, all_ref.at[me], loc_sem)
        local.start()
        peers = []
        for k in range(1, N_DEV):
            px = 1 - x if k & 4 else x
            py = 1 - y if k & 2 else y
            pc = 1 - c if k & 1 else c
            peers.append((4 * px + 2 * py + pc, (px, py, pc)))
        sends = [gather(k, me, xyz) for k, (_, xyz) in enumerate(peers)]
        for cp in sends + cps:
            cp.start()
        for k, (peer, xyz) in enumerate(peers):
            gather(k, peer, xyz).wait_recv()
        for cp in cps:
            cp.wait_recv()
        for cp in sends + cps:
            cp.wait_send()
        local.wait()

    return pl.pallas_call(
        body, in_specs=[ANY] * (n + 1), out_specs=[ANY] * (n + 1),
        out_shape=[jax.ShapeDtypeStruct((3,) + p.shape[1:], p.dtype) for p in parts]
        + [jax.ShapeDtypeStruct((N_DEV,) + small.shape, small.dtype)],
        scratch_shapes=[pltpu.SemaphoreType.DMA((n, 3)), pltpu.SemaphoreType.DMA((n, 3)),
                        pltpu.SemaphoreType.DMA((N_DEV - 1,)), pltpu.SemaphoreType.DMA((N_DEV - 1,)),
                        pltpu.SemaphoreType.DMA(())],
        name="scatter_chips")(*parts, small)


def _add_rows(arrs, out_dtype, name):
    shp = arrs[0].shape
    two = [a.reshape(-1, shp[-1]) for a in arrs]
    rows, cols = two[0].shape
    tr = _row_tile(rows, 1024)

    def body(*refs):
        acc = refs[0][...].astype(F32)
        for r in refs[1:-1]:
            acc = acc + r[...].astype(F32)
        refs[-1][...] = acc.astype(out_dtype)

    blk = pl.BlockSpec((tr, cols), lambda i: (i, 0))
    return pl.pallas_call(
        body, grid=(rows // tr,), in_specs=[blk] * len(two), out_specs=blk,
        out_shape=jax.ShapeDtypeStruct((rows, cols), out_dtype),
        compiler_params=_params(("arbitrary",)), name=name)(*two).reshape(shp)


def _row_tile(rows, cap):
    best = 8
    for t in range(8, min(rows, cap) + 1, 8):
        if rows % t == 0:
            best = t
    return best


def _sum_devices(x):
    _, rows, cols = x.shape
    tr = _row_tile(rows, 2048)

    def body(x_ref, o_ref):
        acc = x_ref[0]
        for j in range(1, N_DEV):
            acc = acc + x_ref[j]
        o_ref[...] = acc

    return pl.pallas_call(
        body, grid=(rows // tr,), in_specs=[pl.BlockSpec((N_DEV, tr, cols), lambda i: (0, i, 0))],
        out_specs=pl.BlockSpec((tr, cols), lambda i: (i, 0)), out_shape=jax.ShapeDtypeStruct((rows, cols), F32),
        compiler_params=_params(("arbitrary",)), name="sum_devices")(x)


def _adamw(w, g, m, v, name):
    rows, cols = w.shape
    tr = _row_tile(rows, 512)
    bc1 = 1.0 - ADAM_B1 ** ADAM_STEP
    bc2 = 1.0 - ADAM_B2 ** ADAM_STEP

    def body(w_ref, g_ref, m_ref, v_ref, d_ref, nm_ref, nv_ref):
        g_ = g_ref[...]
        m_ = ADAM_B1 * m_ref[...] + (1.0 - ADAM_B1) * g_
        v_ = ADAM_B2 * v_ref[...] + (1.0 - ADAM_B2) * (g_ * g_)
        d_ref[...] = -ADAM_LR * ((m_ / bc1) / (jnp.sqrt(v_ / bc2) + ADAM_EPS) + ADAM_WD * w_ref[...])
        nm_ref[...] = m_
        nv_ref[...] = v_

    blk = pl.BlockSpec((tr, cols), lambda i: (i, 0))
    return pl.pallas_call(
        body, grid=(rows // tr,), in_specs=[blk] * 4, out_specs=[blk] * 3,
        out_shape=[jax.ShapeDtypeStruct((rows, cols), F32)] * 3,
        compiler_params=_params(("arbitrary",)), name=name)(w, g, m, v)


def _pack_rows(shape):
    return -(-int(np.prod(shape)) // (16 * LANES)) * 16


def _pack(arrs, dtype=F32):
    blocks = []
    for a in arrs:
        flat = a.reshape(-1).astype(dtype)
        rows = _pack_rows(a.shape)
        blocks.append(jnp.pad(flat, (0, rows * LANES - flat.shape[0])).reshape(rows, LANES))
    return jnp.concatenate(blocks, axis=0)


def _unpack(packed, shapes):
    out, off = [], 0
    for s in shapes:
        rows = _pack_rows(s)
        out.append(packed[off:off + rows].reshape(-1)[:int(np.prod(s))].reshape(s))
        off += rows
    return out


SMALL = ("c_ctx", "b_mod", "g_pre", "g_post", "ret_norm_g", "sg_w", "sg_b", "sc_conv_w", "gdn_conv_w", "gdn_a_log",
         "gdn_dt_bias", "gdn_norm_g")
ORDER = ("c_ctx", "w_mod", "b_mod", "g_pre", "g_post", "w_in", "w_out", "ret_norm_g", "sg_w", "sg_b", "sc_conv_w",
         "gdn_conv_w", "gdn_a_log", "gdn_dt_bias", "gdn_norm_g")


def kernel(x, c, ctx, c_ctx, w_mod, b_mod, g_pre, g_post, w_in, w_out, ret_norm_g, sg_w, sg_b, sc_conv_w, gdn_conv_w, gdn_a_log, gdn_dt_bias, gdn_norm_g, loss_target, m_c_ctx, m_w_mod, m_b_mod, m_g_pre, m_g_post, m_w_in, m_w_out, m_ret_norm_g, m_sg_w, m_sg_b, m_sc_conv_w, m_gdn_conv_w, m_gdn_a_log, m_gdn_dt_bias, m_gdn_norm_g, v_c_ctx, v_w_mod, v_b_mod, v_g_pre, v_g_post, v_w_in, v_w_out, v_ret_norm_g, v_sg_w, v_sg_b, v_sc_conv_w, v_gdn_conv_w, v_gdn_a_log, v_gdn_dt_bias, v_gdn_norm_g):
    wts = dict(c_ctx=c_ctx, w_mod=w_mod, b_mod=b_mod, g_pre=g_pre, g_post=g_post, w_in=w_in, w_out=w_out,
               ret_norm_g=ret_norm_g, sg_w=sg_w, sg_b=sg_b, sc_conv_w=sc_conv_w, gdn_conv_w=gdn_conv_w,
               gdn_a_log=gdn_a_log, gdn_dt_bias=gdn_dt_bias, gdn_norm_g=gdn_norm_g)
    mom = dict(c_ctx=m_c_ctx, w_mod=m_w_mod, b_mod=m_b_mod, g_pre=m_g_pre, g_post=m_g_post, w_in=m_w_in, w_out=m_w_out,
               ret_norm_g=m_ret_norm_g, sg_w=m_sg_w, sg_b=m_sg_b, sc_conv_w=m_sc_conv_w, gdn_conv_w=m_gdn_conv_w,
               gdn_a_log=m_gdn_a_log, gdn_dt_bias=m_gdn_dt_bias, gdn_norm_g=m_gdn_norm_g)
    var = dict(c_ctx=v_c_ctx, w_mod=v_w_mod, b_mod=v_b_mod, g_pre=v_g_pre, g_post=v_g_post, w_in=v_w_in, w_out=v_w_out,
               ret_norm_g=v_ret_norm_g, sg_w=v_sg_w, sg_b=v_sg_b, sc_conv_w=v_sc_conv_w, gdn_conv_w=v_gdn_conv_w,
               gdn_a_log=v_gdn_a_log, gdn_dt_bias=v_gdn_dt_bias, gdn_norm_g=v_gdn_norm_g)
    depth = w_mod.shape[0]
    n_mod, n_in, n_out = w_mod.shape[2], w_in.shape[2], w_out.shape[1]
    n_sc, n_gdn = sc_conv_w.shape[2], gdn_conv_w.shape[2]
    xi, yi, ci = _me()
    me = 4 * xi + 2 * yi + ci

    conv = _pack([sc_conv_w, gdn_conv_w])
    n_conv = depth * 3 * n_sc
    rest = depth - 1
    blocks = [lambda r, j: r.at[:, :, pl.ds(pl.multiple_of(j * n_mod, LANES), n_mod)],
              lambda r, j: r.at[j],
              lambda r, j: r.at[:, pl.ds(pl.multiple_of(j * n_out, 16), n_out), :],
              lambda r, j: r.at[j]]

    def in_place(g):
        return jnp.pad(g.transpose(1, 2, 0, 3).reshape(g.shape[1], D, N_DEV * n_in),
                       ((0, 0), (0, 0), (0, W_PAD - N_DEV * n_in)))

    wmod_0, win_g, wout_0, conv_g = _gather_weights(
        [w_mod[:1].astype(BF16), w_in[:1].astype(BF16), w_out[:1].astype(BF16), conv],
        [(1, D, N_DEV * n_mod), (N_DEV, 1, D, n_in), (1, N_DEV * n_out, D), (N_DEV,) + conv.shape], blocks)
    later_shards = [w_mod[1:].astype(BF16), w_in[1:].astype(BF16), w_out[1:].astype(BF16)]
    zero = jnp.zeros((), jnp.int32)
    lands = [lax.dynamic_update_slice(lax.empty((rest, D, N_DEV * n_mod), BF16), later_shards[0],
                                      (zero, zero, me * n_mod)),
             lax.dynamic_update_slice(lax.empty((N_DEV, rest, D, n_in), BF16), later_shards[1][None],
                                      (me, zero, zero, zero)),
             lax.dynamic_update_slice(lax.empty((rest, N_DEV * n_out, D), BF16), later_shards[2],
                                      (zero, me * n_out, zero))]
    started = _gather_start(later_shards, lands, blocks[:3], "gather_start")

    def later(stream):
        wmod_r, win_r, wout_r = _gather_wait(started, stream, blocks[:3], "gather_wait")[1]
        return wmod_r, in_place(win_r), wout_r

    slabs = [lambda r, j: r.at[j], lambda r, j: r.at[:, pl.ds(pl.multiple_of(j * n_out, 16), n_out), :]]

    r_sc = _pack_rows(sc_conv_w.shape)
    scw_f = conv_g[:, :r_sc].reshape(N_DEV, -1)[:, :n_conv]
    scw_f = scw_f.reshape(N_DEV, depth, 3, n_sc).transpose(1, 2, 0, 3).reshape(depth, 3, -1)
    gdnw_f = conv_g[:, r_sc:].reshape(N_DEV, -1)[:, :depth * 3 * n_gdn]
    gdnw_f = gdnw_f.reshape(N_DEV, depth, 3, n_gdn).transpose(1, 2, 0, 3)
    gdnw_f = gdnw_f.reshape(depth, 3, -1)

    loss8, dx, g, cvec8 = _local_step(x, c, ctx, c_ctx, (wmod_0, in_place(win_g), wout_0), later, started[4], b_mod,
                                      g_pre, g_post, ret_norm_g, sg_w, sg_b, scw_f, gdnw_f, gdn_a_log, gdn_dt_bias,
                                      gdn_norm_g, loss_target)

    gin = jnp.stack(g["win"]).astype(BF16).reshape(depth, D, N_DEV, n_in).transpose(2, 0, 1, 3)
    gout = jnp.stack(g["wout"]).astype(BF16)
    got_in, got_out = _scatter_pair([gin, gout], slabs, [(depth, D, n_in), (depth, n_out, D)])
    mine_in = lax.dynamic_index_in_dim(gin.reshape(4, 2, depth, D, n_in), ci, axis=1, keepdims=False)
    mine_out = lax.dynamic_index_in_dim(gout.reshape(depth, 4, 2, n_out, D), ci, axis=2, keepdims=False)
    mine_out = mine_out.transpose(1, 0, 2, 3)
    local_small = dict(c_ctx=g["c_ctx"], b_mod=g["bmod"], g_pre=g["gpre"], g_post=g["gpost"], ret_norm_g=g["ret_ng"],
                       sg_w=g["sgw"], sg_b=g["sgb"], sc_conv_w=g["scw"], gdn_conv_w=g["gdnw"], gdn_a_log=g["alog"],
                       gdn_dt_bias=g["dtb"], gdn_norm_g=g["gdn_ng"])
    to_sum = _pack([loss8[0, :1]] + [local_small[k] for k in SMALL])
    rows_sum = to_sum.shape[0]
    as_is = _pack([cvec8[:3], g["dmod"][:, :3, :]])
    far_in, far_out, everyone = _scatter_chips([_add_rows([mine_in, got_in], BF16, "pair_sum_in"),
                                                _add_rows([mine_out, got_out], BF16, "pair_sum_out")],
                                               jnp.concatenate([to_sum, as_is], axis=0))
    chip = 2 * xi + yi
    own = lambda a: lax.dynamic_index_in_dim(a, chip, axis=0, keepdims=False)
    grad = dict(w_in=_add_rows([own(mine_in), own(got_in), far_in[0], far_in[1], far_in[2]], F32, "grad_sum_in"),
                w_out=_add_rows([own(mine_out), own(got_out), far_out[0], far_out[1], far_out[2]], F32,
                                "grad_sum_out"))

    small_sum = _unpack(_sum_devices(everyone[:, :rows_sum]), [(1,)] + [local_small[k].shape for k in SMALL])
    loss = small_sum[0][0]
    for k, val in zip(SMALL, small_sum[1:]):
        grad[k] = val
    grad["sc_conv_w"] = lax.dynamic_slice_in_dim(grad["sc_conv_w"], me * n_sc, n_sc, axis=2)
    grad["gdn_conv_w"] = lax.dynamic_slice_in_dim(grad["gdn_conv_w"], me * n_gdn, n_gdn, axis=2)
    r_c = _pack_rows((3, D))
    c_all = everyone[:, rows_sum:rows_sum + r_c].reshape(N_DEV, -1)[:, :3 * D].reshape(N_DEV * 3, D)
    dmod_all = everyone[:, rows_sum + r_c:].reshape(N_DEV, -1)[:, :depth * 9 * D]
    dmod_all = dmod_all.reshape(N_DEV, depth, 3, 3 * D).transpose(1, 0, 2, 3)
    dmod_mine = lax.dynamic_slice_in_dim(dmod_all.reshape(depth, N_DEV * 3, 3 * D), me * n_mod, n_mod, axis=2)
    grad["w_mod"] = _wmod_grad(_pad_rows(c_all, 32), jnp.pad(dmod_mine, ((0, 0), (0, 32 - N_DEV * 3), (0, 0))))

    delta, new_m, new_v = {}, {}, {}
    for k in ("w_mod", "w_in", "w_out"):
        shp = wts[k].shape
        two = lambda a: a.reshape(-1, shp[-1])
        res = _adamw(two(wts[k]), two(grad[k]), two(mom[k]), two(var[k]), "adamw_" + k)
        delta[k], new_m[k], new_v[k] = [r.reshape(shp) for r in res]
    res = _adamw(*[_pack([d[k] for k in SMALL]) for d in (wts, grad, mom, var)], "adamw_small")
    for dst, flat in zip((delta, new_m, new_v), res):
        for k, val in zip(SMALL, _unpack(flat, [wts[k].shape for k in SMALL])):
            dst[k] = val
    return (loss, dx, *[grad[k] for k in ORDER], *[delta[k] for k in ORDER], *[new_m[k] for k in ORDER],
            *[new_v[k] for k in ORDER])
```

```python
import functools
import math

import jax
import jax.numpy as jnp
import numpy as np
from jax import lax
from jax.experimental import pallas as pl
from jax.experimental.pallas import tpu as pltpu

F32, BF16 = jnp.float32, jnp.bfloat16
HI = lax.Precision.HIGHEST

N_DEV = 8
D = 1024
DEPTH = 4
BRW = 256
HD = 64
NH = 4
LANES = 128
GRID_W = 64
ROPE_BASE = 10000.0
W_IN = 15 * BRW + 4 * NH
W_PAD = 31 * LANES
RC = 128
GC = 64
EPS = 1e-6
LOG_GAMMA = tuple(math.log(1.0 - 2.0 ** (-5.0 - h)) for h in range(NH))
ADAM_LR, ADAM_B1, ADAM_B2, ADAM_EPS, ADAM_WD, ADAM_STEP = 0.001, 0.9, 0.999, 1e-08, 0.01, 10
VMEM_LIMIT = 56 * 1024 * 1024

COL_RET, COL_SG, COL_SC, COL_GDN = 0, 4, 7, 11
COL_A128 = 30


def _params(sem):
    return pltpu.CompilerParams(dimension_semantics=sem, vmem_limit_bytes=VMEM_LIMIT)


def _bdot(a, b, ca, cb):
    if a.ndim == 3:
        dn = (((ca + 1,), (cb + 1,)), ((0,), (0,)))
    else:
        dn = (((ca,), (cb,)), ((), ()))
    return lax.dot_general(a.astype(BF16), b.astype(BF16), dn, preferred_element_type=F32)


@jax.custom_vjp
def mm(a, b):
    return _bdot(a, b, 1, 0)


mm.defvjp(lambda a, b: (_bdot(a, b, 1, 0), (a, b)),
          lambda r, g: (_bdot(g, r[1], 1, 1), _bdot(r[0], g, 0, 0)))


@jax.custom_vjp
def mm_nt(a, b):
    return _bdot(a, b, 1, 1)


mm_nt.defvjp(lambda a, b: (_bdot(a, b, 1, 1), (a, b)),
             lambda r, g: (_bdot(g, r[1], 1, 0), _bdot(g, r[0], 0, 0)))


@jax.custom_vjp
def mm_tn(a, b):
    return _bdot(a, b, 0, 0)


mm_tn.defvjp(lambda a, b: (_bdot(a, b, 0, 0), (a, b)),
             lambda r, g: (_bdot(r[1], g, 1, 1), _bdot(r[0], g, 1, 0)))


def _dotf(a, b):
    return jnp.dot(a, b, precision=HI, preferred_element_type=F32)


def _iota(shape, dim):
    return lax.broadcasted_iota(jnp.int32, shape, dim)


def _head_mask(h, width=BRW):
    return (_iota((1, width), 1) // HD == h).astype(F32)


def _lane_by_head(vals, width=BRW, lane0=0):
    head = (_iota((1, width), 1) + lane0) // HD
    out = jnp.full((1, width), vals[NH - 1], F32)
    for h in range(NH - 2, -1, -1):
        out = jnp.where(head == h, vals[h], out)
    return out


def _block_diag(n, width):
    return (_iota((n, width), 0) // HD == _iota((n, width), 1) // HD).astype(F32)


@jax.custom_vjp
def _head_sum(x):
    w = x.shape[1]
    ones = _block_diag(w, w).astype(BF16)
    hi = x.astype(BF16)
    lo = (x - hi.astype(F32)).astype(BF16)
    return jnp.dot(hi, ones, preferred_element_type=F32) + jnp.dot(lo, ones, preferred_element_type=F32)


_head_sum.defvjp(lambda x: (_head_sum(x), None), lambda _, g: (_head_sum(g),))


def _silu(x):
    return x * jax.nn.sigmoid(x)


def _stack_heads(x):
    return jnp.concatenate([x * _head_mask(h) for h in range(NH)], axis=-2)


@jax.custom_vjp
def _unstack_heads(x):
    n = x.shape[-2] // NH
    return (x[..., 0:n, :] + x[..., n:2 * n, :]) + (x[..., 2 * n:3 * n, :] + x[..., 3 * n:4 * n, :])


_unstack_heads.defvjp(lambda x: (_unstack_heads(x), None), lambda _, g: (_stack_heads(g),))


@jax.custom_vjp
def _rot_half(x):
    n = x.shape[1]
    first = (_iota(x.shape, 1) % 32) < 16
    return jnp.where(first, -pltpu.roll(x, n - 16, 1), pltpu.roll(x, 16, 1))


_rot_half.defvjp(lambda x: (_rot_half(x), None), lambda _, g: (-_rot_half(g),))


def _rotary(x, cos, sin):
    return x * cos + _rot_half(x) * sin


def _make_shifts(seq, t_ctx):
    def dn_raw(x):
        r = _iota(x.shape, 0)
        return jnp.where((r == 0) | (r == t_ctx), 0.0, pltpu.roll(x, 1, 0))

    def up_raw(x):
        r = _iota(x.shape, 0)
        return jnp.where((r == t_ctx - 1) | (r == seq - 1), 0.0, pltpu.roll(x, seq - 1, 0))

    @jax.custom_vjp
    def dn(x):
        return dn_raw(x)

    @jax.custom_vjp
    def up(x):
        return up_raw(x)

    dn.defvjp(lambda x: (dn_raw(x), None), lambda _, g: (up_raw(g),))
    up.defvjp(lambda x: (up_raw(x), None), lambda _, g: (dn_raw(g),))
    return dn, up


def _conv3(t, w0, w1, w2, dn, up):
    return dn(t) * w0 + t * w1 + up(t) * w2


def _acc(ref, val, first, at=()):
    idx = at + (Ellipsis,)

    @pl.when(first)
    def _():
        ref[idx] = val

    @pl.when(jnp.logical_not(first))
    def _():
        ref[idx] += val


def _mod_fwd(cvec8, wmod, bmod):
    depth = wmod.shape[0]

    def body(c_ref, w_ref, b_ref, o_ref):
        sc = _silu(c_ref[...])
        o_ref[0] = jnp.dot(sc.astype(BF16), w_ref[0], preferred_element_type=F32) + b_ref[0]

    return pl.pallas_call(
        body, grid=(depth, 3),
        in_specs=[pl.BlockSpec((8, D), lambda l, j: (0, 0)),
                  pl.BlockSpec((1, D, D), lambda l, j: (l, 0, j)),
                  pl.BlockSpec((1, 1, D), lambda l, j: (l, 0, j))],
        out_specs=pl.BlockSpec((1, 8, D), lambda l, j: (l, 0, j)),
        out_shape=jax.ShapeDtypeStruct((depth, 8, 3 * D), F32),
        compiler_params=_params(("arbitrary", "arbitrary")), name="mod_fwd")(cvec8, wmod, bmod)


def _mod_bwd(dmod, wmod, cvec8):
    depth = wmod.shape[0]

    def body(dm_ref, w_ref, c_ref, dc_ref, db_ref):
        l, j = pl.program_id(0), pl.program_id(1)
        dm = dm_ref[0]
        db_ref[0] = jnp.sum(dm, axis=0, keepdims=True)
        part = _bdot(dm, w_ref[0], 1, 1)
        _acc(dc_ref, part, (l == 0) & (j == 0))

        @pl.when((l == depth - 1) & (j == 2))
        def _():
            c = c_ref[...]
            s = jax.nn.sigmoid(c)
            dc_ref[...] = dc_ref[...] * (s * (1.0 + c * (1.0 - s)))

    return pl.pallas_call(
        body, grid=(depth, 3),
        in_specs=[pl.BlockSpec((1, 8, D), lambda l, j: (l, 0, j)),
                  pl.BlockSpec((1, D, D), lambda l, j: (l, 0, j)),
                  pl.BlockSpec((8, D), lambda l, j: (0, 0))],
        out_specs=[pl.BlockSpec((8, D), lambda l, j: (0, 0)),
                   pl.BlockSpec((1, 1, D), lambda l, j: (l, 0, j))],
        out_shape=[jax.ShapeDtypeStruct((8, D), F32), jax.ShapeDtypeStruct((depth, 1, 3 * D), F32)],
        compiler_params=_params(("arbitrary", "arbitrary")), name="mod_bwd")(dmod, wmod, cvec8)


def _wmod_grad(c_rows, dmod_cols):
    depth, rows, n = dmod_cols.shape

    def body(c_ref, dm_ref, o_ref):
        sc = _silu(c_ref[...])
        o_ref[0] = lax.dot_general(sc, dm_ref[0], (((0,), (0,)), ((), ())), precision=HI,
                                   preferred_element_type=F32)

    return pl.pallas_call(
        body, grid=(depth,),
        in_specs=[pl.BlockSpec((rows, D), lambda l: (0, 0)), pl.BlockSpec((1, rows, n), lambda l: (l, 0, 0))],
        out_specs=pl.BlockSpec((1, D, n), lambda l: (l, 0, 0)),
        out_shape=jax.ShapeDtypeStruct((depth, D, n), F32),
        compiler_params=_params(("arbitrary",)), name="wmod_grad")(c_rows, dmod_cols)


class _Lay:
    def __init__(self, batch, t_ctx, t_lat):
        self.b, self.t_ctx, self.t_lat = batch, t_ctx, t_lat
        self.s = t_ctx + t_lat
        self.tm = min(256, t_ctx)
        self.tpb = self.s // self.tm
        self.nct = t_ctx // self.tm
        self.ntiles = batch * self.tpb
        self.rows = batch * self.s

    def mod_row(self, i):
        return jnp.where(i % self.tpb < self.nct, 2, i // self.tpb)

    def group(self, i):
        return 2 * (i // self.tpb) + jnp.where(i % self.tpb < self.nct, 0, 1)

    def group_first(self, i):
        return (i % self.tpb == 0) | (i % self.tpb == self.nct)


def _norm_mod(x, g, shift, scale):
    r = lax.rsqrt(jnp.mean(x * x, axis=-1, keepdims=True) + EPS)
    return (x * r * g) * (1.0 + scale) + shift


def _inproj_fwd(lay, xc, mod3, gpre, w):
    tm = lay.tm

    def body(x_ref, sh_ref, sc_ref, g_ref, w_ref, p_ref, ht_ref):
        h = _norm_mod(x_ref[...], g_ref[...], sh_ref[0, 0], sc_ref[0, 0])
        ht_ref[...] = h.T.astype(BF16)
        p_ref[...] = jnp.dot(h.astype(BF16), w_ref[...], preferred_element_type=F32)

    return pl.pallas_call(
        body, grid=(lay.ntiles,),
        in_specs=[pl.BlockSpec((tm, D), lambda i: (i, 0)),
                  pl.BlockSpec((1, 1, 1, D), lambda i: (0, lay.mod_row(i), 0, 0)),
                  pl.BlockSpec((1, 1, 1, D), lambda i: (1, lay.mod_row(i), 0, 0)),
                  pl.BlockSpec((1, D), lambda i: (0, 0)),
                  pl.BlockSpec((D, W_PAD), lambda i: (0, 0))],
        out_specs=[pl.BlockSpec((tm, W_PAD), lambda i: (i, 0)), pl.BlockSpec((D, tm), lambda i: (0, i))],
        out_shape=[jax.ShapeDtypeStruct((lay.rows, W_PAD), F32), jax.ShapeDtypeStruct((D, lay.rows), BF16)],
        compiler_params=_params(("arbitrary",)), name="inproj_fwd")(xc, mod3, mod3, gpre, w)


def _inproj_bwd(lay, xc, mod3, gpre, wt, dxc, pieces):
    tm = lay.tm
    npc = len(pieces)
    offs = [off for _, off in pieces]

    def body(*refs):
        x_ref, sh_ref, sc_ref, g_ref, wt_ref, dx_in = refs[:6]
        dps = refs[6:6 + npc]
        dx_ref, dg_ref, dsh_ref, dsc_ref = refs[6 + npc:]
        i = pl.program_id(0)
        dh = None
        for dp_ref, off in zip(dps, offs):
            wd = dp_ref.shape[1]
            part = jnp.dot(dp_ref[...], wt_ref[off:off + wd, :], preferred_element_type=F32)
            dh = part if dh is None else dh + part
        _, vjp = jax.vjp(_norm_mod, x_ref[...], g_ref[...], sh_ref[0, 0], sc_ref[0, 0])
        dx, dg, dsh, dsc = vjp(dh)
        dx_ref[...] = dx_in[...] + dx
        _acc(dg_ref, dg, i == 0)
        first = lay.group_first(i)
        _acc(dsh_ref, dsh, first, at=(0,))
        _acc(dsc_ref, dsc, first, at=(0,))

    return pl.pallas_call(
        body, grid=(lay.ntiles,),
        in_specs=[pl.BlockSpec((tm, D), lambda i: (i, 0)),
                  pl.BlockSpec((1, 1, 1, D), lambda i: (0, lay.mod_row(i), 0, 0)),
                  pl.BlockSpec((1, 1, 1, D), lambda i: (1, lay.mod_row(i), 0, 0)),
                  pl.BlockSpec((1, D), lambda i: (0, 0)),
                  pl.BlockSpec((W_PAD, D), lambda i: (0, 0)),
                  pl.BlockSpec((tm, D), lambda i: (i, 0))]
        + [pl.BlockSpec((tm, dp.shape[1]), lambda i: (i, 0)) for dp, _ in pieces],
        out_specs=[pl.BlockSpec((tm, D), lambda i: (i, 0)),
                   pl.BlockSpec((1, D), lambda i: (0, 0)),
                   pl.BlockSpec((1, 1, D), lambda i: (lay.group(i), 0, 0)),
                   pl.BlockSpec((1, 1, D), lambda i: (lay.group(i), 0, 0))],
        out_shape=[jax.ShapeDtypeStruct((lay.rows, D), F32), jax.ShapeDtypeStruct((1, D), F32),
                   jax.ShapeDtypeStruct((2 * lay.b, 1, D), F32), jax.ShapeDtypeStruct((2 * lay.b, 1, D), F32)],
        compiler_params=_params(("arbitrary",)), name="inproj_bwd",
    )(xc, mod3, mod3, gpre, wt, dxc, *[dp for dp, _ in pieces])


def _weight_grad(lay, ht, dp, name):
    wd = dp.shape[1]
    tn = wd
    tr = lay.rows // 3 if lay.rows % (3 * 256) == 0 else lay.tm

    def body(ht_ref, dp_ref, o_ref):
        _acc(o_ref, jnp.dot(ht_ref[...], dp_ref[...], preferred_element_type=F32), pl.program_id(1) == 0)

    return pl.pallas_call(
        body, grid=(wd // tn, lay.rows // tr),
        in_specs=[pl.BlockSpec((D, tr), lambda j, i: (0, i)), pl.BlockSpec((tr, tn), lambda j, i: (i, j))],
        out_specs=pl.BlockSpec((D, tn), lambda j, i: (0, j)),
        out_shape=jax.ShapeDtypeStruct((D, wd), F32),
        compiler_params=_params(("arbitrary", "arbitrary")), name=name)(ht, dp)


def _outproj_post(o, x, gpost, gate):
    r = lax.rsqrt(jnp.mean(o * o, axis=-1, keepdims=True) + EPS)
    return x + gate * (o * r * gpost)


def _outproj_matmul(ys, w_ref):
    o = None
    for k, y in enumerate(ys):
        part = jnp.dot(y[...], w_ref[BRW * k:BRW * (k + 1), :], preferred_element_type=F32)
        o = part if o is None else o + part
    return o


def _outproj_specs(lay):
    tm = lay.tm
    return ([pl.BlockSpec((tm, BRW), lambda i: (i, 0))] * 4
            + [pl.BlockSpec((tm, D), lambda i: (i, 0))]
            + [pl.BlockSpec((D, D), lambda i: (0, 0))]
            + [pl.BlockSpec((1, D), lambda i: (0, 0))]
            + [pl.BlockSpec((1, 1, 1, D), lambda i: (2, lay.mod_row(i), 0, 0))])


def _outproj_fwd(lay, ys, xc, wout, gpost, mod3):
    tm = lay.tm

    def body(y0, y1, y2, y3, x_ref, w_ref, g_ref, gt_ref, o_ref, yt_ref):
        ys_ = (y0, y1, y2, y3)
        o_ref[...] = _outproj_post(_outproj_matmul(ys_, w_ref), x_ref[...], g_ref[...], gt_ref[0, 0])
        for k, y in enumerate(ys_):
            yt_ref[BRW * k:BRW * (k + 1), :] = y[...].astype(F32).T.astype(BF16)

    return pl.pallas_call(
        body, grid=(lay.ntiles,), in_specs=_outproj_specs(lay),
        out_specs=[pl.BlockSpec((tm, D), lambda i: (i, 0)), pl.BlockSpec((D, tm), lambda i: (0, i))],
        out_shape=[jax.ShapeDtypeStruct((lay.rows, D), F32), jax.ShapeDtypeStruct((D, lay.rows), BF16)],
        compiler_params=_params(("arbitrary",)), name="outproj_fwd")(*ys, xc, wout, gpost, mod3)


def _outproj_bwd(lay, ys, xc, wout, gpost, mod3, dxc):
    tm = lay.tm

    def body(y0, y1, y2, y3, x_ref, w_ref, g_ref, gt_ref, dx_ref, d0, d1, d2, d3, do_ref, dg_ref, dgt_ref):
        i = pl.program_id(0)
        o = _outproj_matmul((y0, y1, y2, y3), w_ref)
        _, vjp = jax.vjp(_outproj_post, o, x_ref[...], g_ref[...], gt_ref[0, 0])
        do, _, dg, dgt = vjp(dx_ref[...])
        do = do.astype(BF16)
        do_ref[...] = do
        for k, d in enumerate((d0, d1, d2, d3)):
            d[...] = _bdot(do, w_ref[BRW * k:BRW * (k + 1), :], 1, 1)
        _acc(dg_ref, dg, i == 0)
        _acc(dgt_ref, dgt, lay.group_first(i), at=(0,))

    return pl.pallas_call(
        body, grid=(lay.ntiles,),
        in_specs=_outproj_specs(lay) + [pl.BlockSpec((tm, D), lambda i: (i, 0))],
        out_specs=[pl.BlockSpec((tm, BRW), lambda i: (i, 0))] * 4
        + [pl.BlockSpec((tm, D), lambda i: (i, 0)), pl.BlockSpec((1, D), lambda i: (0, 0)),
           pl.BlockSpec((1, 1, D), lambda i: (lay.group(i), 0, 0))],
        out_shape=[jax.ShapeDtypeStruct((lay.rows, BRW), F32)] * 4
        + [jax.ShapeDtypeStruct((lay.rows, D), BF16), jax.ShapeDtypeStruct((1, D), F32),
           jax.ShapeDtypeStruct((2 * lay.b, 1, D), F32)],
        compiler_params=_params(("arbitrary",)), name="outproj_bwd")(*ys, xc, wout, gpost, mod3, dxc)


def _loss_kernel(lay, xc3, target):
    tm, nct = lay.tm, lay.nct

    def body(x_ref, t_ref, loss_ref, dx_ref):
        b, i = pl.program_id(0), pl.program_id(1)
        lat = i >= nct
        err = x_ref[0] - t_ref[0]
        dx_ref[0] = jnp.where(lat, err * (1.0 / D), 0.0)
        part = jnp.sum(jnp.sum(err * err, axis=1, keepdims=True), axis=0, keepdims=True) * (0.5 / D)
        part = jnp.broadcast_to(jnp.where(lat, part, 0.0), (8, LANES))
        _acc(loss_ref, part, (b == 0) & (i == 0))

    return pl.pallas_call(
        body, grid=(lay.b, lay.tpb),
        in_specs=[pl.BlockSpec((1, tm, D), lambda b, i: (b, i, 0)),
                  pl.BlockSpec((1, tm, D), lambda b, i: (b, jnp.maximum(i - nct, 0), 0))],
        out_specs=[pl.BlockSpec((8, LANES), lambda b, i: (0, 0)), pl.BlockSpec((1, tm, D), lambda b, i: (b, i, 0))],
        out_shape=[jax.ShapeDtypeStruct((8, LANES), F32), jax.ShapeDtypeStruct(xc3.shape, F32)],
        compiler_params=_params(("arbitrary", "arbitrary")), name="loss")(xc3, target)


def _chunk_orders(n_ctx, n_all):
    fwd = list(range(n_all))
    rev = list(range(n_ctx - 1, -1, -1)) + list(range(n_all - 1, n_ctx - 1, -1))
    return fwd, rev


def _ret_state_fn(k, v, cos, sin):
    kt = _rotary(k, cos, sin) * (HD ** -0.5)
    lg = _lane_by_head(LOG_GAMMA)
    j = _iota((RC, 1), 0).astype(F32)
    bd = _block_diag(BRW, BRW)
    af = mm_tn(kt * jnp.exp((RC - 1.0 - j) * lg), v) * bd
    ar = mm_tn(kt * jnp.exp(j * lg), v) * bd
    return af, ar


def _ret_out_fn(q, k, v, z, cos, sin, sf, sr, ng):
    qt = _rotary(q, cos, sin)
    kt = _rotary(k, cos, sin) * (HD ** -0.5)
    diff = (_iota((RC, RC), 0) - _iota((RC, RC), 1)).astype(F32)
    o = None
    for h in range(NH):
        m = _head_mask(h)
        sc = mm_nt(qt * m, kt)
        wgt = sc * jnp.exp(jnp.abs(diff) * LOG_GAMMA[h]) * jnp.where(diff == 0, 2.0, 1.0)
        part = mm(wgt, v * m)
        o = part if o is None else o + part
    lg = _lane_by_head(LOG_GAMMA)
    i = _iota((RC, 1), 0).astype(F32)
    o = o + mm(qt, sf) * jnp.exp((i + 1.0) * lg) + mm(qt, sr) * jnp.exp((RC - i) * lg)
    mu = _head_sum(o) * (1.0 / HD)
    cen = o - mu
    var = _head_sum(cen * cen) * (1.0 / HD)
    return cen * lax.rsqrt(var + EPS) * ng * _silu(z)


def _ret_chunks(lay):
    nc = lay.s // RC
    return 6 if nc % 6 == 0 else (2 if nc % 2 == 0 else 1)


def _ret_specs(lay, cols):
    rows = _ret_chunks(lay) * RC
    return [pl.BlockSpec((1, rows, BRW), functools.partial(lambda b, i, c: (b, i, c), c=COL_RET + c)) for c in cols]


def _ret_state(lay, p3, cos, sin):
    nc, per = lay.s // RC, _ret_chunks(lay)

    def body(k_ref, v_ref, c_ref, s_ref, a_ref):
        for n in range(per):
            rows = pl.ds(RC * n, RC)
            af, ar = _ret_state_fn(k_ref[0, rows, :], v_ref[0, rows, :], c_ref[rows, :], s_ref[rows, :])
            a_ref[0, n, 0] = af
            a_ref[0, n, 1] = ar

    tab = pl.BlockSpec((per * RC, BRW), lambda b, i: (i, 0))
    return pl.pallas_call(
        body, grid=(lay.b, nc // per), in_specs=_ret_specs(lay, (1, 2)) + [tab, tab],
        out_specs=pl.BlockSpec((1, per, 2, BRW, BRW), lambda b, i: (b, i, 0, 0, 0)),
        out_shape=jax.ShapeDtypeStruct((lay.b, nc, 2, BRW, BRW), F32),
        compiler_params=_params(("arbitrary", "arbitrary")), name="ret_state")(p3, p3, cos, sin)


def _ret_state_bwd(lay, p3, cos, sin, d_a, dpr):
    nc, per = lay.s // RC, _ret_chunks(lay)

    def body(k_ref, v_ref, c_ref, s_ref, da_ref, dpr_ref, o_ref):
        for n in range(per):
            rows = pl.ds(RC * n, RC)
            cos_, sin_ = c_ref[rows, :], s_ref[rows, :]
            _, vjp = jax.vjp(lambda k, v: _ret_state_fn(k, v, cos_, sin_), k_ref[0, rows, :], v_ref[0, rows, :])
            dk, dv = vjp((da_ref[0, n, 0], da_ref[0, n, 1]))
            o_ref[0, rows, 0:BRW] = dpr_ref[0, rows, 0:BRW].astype(BF16)
            o_ref[0, rows, BRW:2 * BRW] = (dpr_ref[0, rows, BRW:2 * BRW] + dk).astype(BF16)
            o_ref[0, rows, 2 * BRW:3 * BRW] = (dpr_ref[0, rows, 2 * BRW:3 * BRW] + dv).astype(BF16)
            o_ref[0, rows, 3 * BRW:] = dpr_ref[0, rows, 3 * BRW:].astype(BF16)

    tab = pl.BlockSpec((per * RC, BRW), lambda b, i: (i, 0))
    return pl.pallas_call(
        body, grid=(lay.b, nc // per),
        in_specs=_ret_specs(lay, (1, 2)) + [tab, tab,
                                            pl.BlockSpec((1, per, 2, BRW, BRW), lambda b, i: (b, i, 0, 0, 0)),
                                            pl.BlockSpec((1, per * RC, 4 * BRW), lambda b, i: (b, i, 0))],
        out_specs=pl.BlockSpec((1, per * RC, 4 * BRW), lambda b, i: (b, i, 0)),
        out_shape=jax.ShapeDtypeStruct((lay.b, lay.s, 4 * BRW), BF16),
        compiler_params=_params(("arbitrary", "arbitrary")), name="ret_state_bwd")(p3, p3, cos, sin, d_a, dpr)


def _state_scan(lay, a, nc_ctx, transpose, name):
    b, nc = a.shape[0], a.shape[1]
    orders = _chunk_orders(nc_ctx, nc)

    def body(a_ref, o_ref):
        d, jh = pl.program_id(1), pl.program_id(2)
        head = (_iota((1, LANES), 1) + jh * LANES) // HD
        lg = jnp.full((1, LANES), LOG_GAMMA[NH - 1], F32)
        for h in range(NH - 2, -1, -1):
            lg = jnp.where(head == h, LOG_GAMMA[h], lg)
        dec = jnp.exp(RC * lg)
        for dd in (0, 1):
            @pl.when(d == dd)
            def _(order=orders[dd]):
                acc = jnp.zeros((BRW, LANES), F32)
                if not transpose:
                    for c in order:
                        o_ref[0, c, 0] = acc
                        acc = acc * dec + a_ref[0, c, 0]
                else:
                    for c in reversed(order):
                        o_ref[0, c, 0] = acc
                        acc = a_ref[0, c, 0] + acc * dec

    spec = pl.BlockSpec((1, nc, 1, BRW, LANES), lambda bb, d, jh: (bb, 0, d, 0, jh))
    return pl.pallas_call(
        body, grid=(b, 2, BRW // LANES), in_specs=[spec], out_specs=spec,
        out_shape=jax.ShapeDtypeStruct(a.shape, F32),
        compiler_params=_params(("arbitrary",) * 3), name=name)(a)


def _ret_out(lay, p3, cos, sin, states, ng):
    nc, per = lay.s // RC, _ret_chunks(lay)

    def body(q_ref, k_ref, v_ref, z_ref, c_ref, s_ref, st_ref, ng_ref, y_ref):
        for n in range(per):
            rows = pl.ds(RC * n, RC)
            y = _ret_out_fn(q_ref[0, rows, :], k_ref[0, rows, :], v_ref[0, rows, :], z_ref[0, rows, :],
                            c_ref[rows, :], s_ref[rows, :], st_ref[0, n, 0], st_ref[0, n, 1], ng_ref[...])
            y_ref[0, rows, :] = y.astype(BF16)

    tab = pl.BlockSpec((per * RC, BRW), lambda b, i: (i, 0))
    return pl.pallas_call(
        body, grid=(lay.b, nc // per),
        in_specs=_ret_specs(lay, (0, 1, 2, 3)) + [tab, tab,
                                                  pl.BlockSpec((1, per, 2, BRW, BRW), lambda b, i: (b, i, 0, 0, 0)),
                                                  pl.BlockSpec((1, BRW), lambda b, i: (0, 0))],
        out_specs=pl.BlockSpec((1, per * RC, BRW), lambda b, i: (b, i, 0)),
        out_shape=jax.ShapeDtypeStruct((lay.b, lay.s, BRW), BF16),
        compiler_params=_params(("arbitrary", "arbitrary")), name="ret_out")(p3, p3, p3, p3, cos, sin, states, ng)


def _ret_out_bwd(lay, p3, cos, sin, states, ng, dy):
    nc, per = lay.s // RC, _ret_chunks(lay)

    def body(q_ref, k_ref, v_ref, z_ref, c_ref, s_ref, st_ref, ng_ref, dy_ref, dp_ref, dst_ref, dng_ref):
        b, i = pl.program_id(0), pl.program_id(1)
        dng_sum = None
        for n in range(per):
            rows = pl.ds(RC * n, RC)
            cos_, sin_ = c_ref[rows, :], s_ref[rows, :]
            fn = lambda q, k, v, z, sf, sr, ng: _ret_out_fn(q, k, v, z, cos_, sin_, sf, sr, ng)
            _, vjp = jax.vjp(fn, q_ref[0, rows, :], k_ref[0, rows, :], v_ref[0, rows, :], z_ref[0, rows, :],
                             st_ref[0, n, 0], st_ref[0, n, 1], ng_ref[...])
            dq, dk, dv, dz, dsf, dsr, dng = vjp(dy_ref[0, rows, :])
            for m, g in enumerate((dq, dk, dv, dz)):
                dp_ref[0, rows, BRW * m:BRW * (m + 1)] = g
            dst_ref[0, n, 0] = dsf
            dst_ref[0, n, 1] = dsr
            dng_sum = dng if dng_sum is None else dng_sum + dng
        _acc(dng_ref, dng_sum, (b == 0) & (i == 0))

    tab = pl.BlockSpec((per * RC, BRW), lambda b, i: (i, 0))
    st = pl.BlockSpec((1, per, 2, BRW, BRW), lambda b, i: (b, i, 0, 0, 0))
    return pl.pallas_call(
        body, grid=(lay.b, nc // per),
        in_specs=_ret_specs(lay, (0, 1, 2, 3)) + [tab, tab, st, pl.BlockSpec((1, BRW), lambda b, i: (0, 0)),
                                                  pl.BlockSpec((1, per * RC, BRW), lambda b, i: (b, i, 0))],
        out_specs=[pl.BlockSpec((1, per * RC, 4 * BRW), lambda b, i: (b, i, 0)), st,
                   pl.BlockSpec((1, BRW), lambda b, i: (0, 0))],
        out_shape=[jax.ShapeDtypeStruct((lay.b, lay.s, 4 * BRW), F32),
                   jax.ShapeDtypeStruct(states.shape, F32), jax.ShapeDtypeStruct((1, BRW), F32)],
        compiler_params=_params(("arbitrary", "arbitrary")), name="ret_out_bwd",
    )(p3, p3, p3, p3, cos, sin, states, ng, dy)


def _sg_fn(u, v, z, w, b8):
    ug = jax.nn.gelu(u)
    vg = jax.nn.gelu(v)
    mu = jnp.mean(vg, axis=-1, keepdims=True)
    cen = vg - mu
    vn = cen * lax.rsqrt(jnp.mean(cen * cen, axis=-1, keepdims=True) + EPS)
    masks = (_iota((NH, 1, BRW), 2) // HD == _iota((NH, 1, BRW), 0)).astype(F32)
    s = jnp.sum(mm(w, vn[None] * masks), axis=0)
    expand = (_iota((8, BRW), 1) // HD == _iota((8, BRW), 0)).astype(F32)
    bias = lax.dot_general(b8, expand, (((0,), (0,)), ((), ())), precision=HI, preferred_element_type=F32)
    return ug * (s + bias) * _silu(z)


def _sg_chunks(lay):
    nc = lay.s // RC
    return 6 if nc % 6 == 0 else (2 if nc % 2 == 0 else 1)


def _sg_specs(lay):
    rows = _sg_chunks(lay) * RC
    return ([pl.BlockSpec((1, rows, BRW), functools.partial(lambda b, i, c: (b, i, c), c=COL_SG + c)) for c in range(3)]
            + [pl.BlockSpec((NH, RC, RC), lambda b, i: (0, 0, 0)), pl.BlockSpec((8, RC), lambda b, i: (0, 0))])


def _sg_fwd(lay, p3, sgw, sgb8):
    per = _sg_chunks(lay)

    def body(u_ref, v_ref, z_ref, w_ref, b_ref, y_ref):
        for k in range(per):
            rows = pl.ds(RC * k, RC)
            y = _sg_fn(u_ref[0, rows, :], v_ref[0, rows, :], z_ref[0, rows, :], w_ref[...], b_ref[...])
            y_ref[0, rows, :] = y.astype(BF16)

    return pl.pallas_call(
        body, grid=(lay.b, lay.s // (per * RC)), in_specs=_sg_specs(lay),
        out_specs=pl.BlockSpec((1, per * RC, BRW), lambda b, i: (b, i, 0)),
        out_shape=jax.ShapeDtypeStruct((lay.b, lay.s, BRW), BF16),
        compiler_params=_params(("arbitrary", "arbitrary")), name="sg_fwd")(p3, p3, p3, sgw, sgb8)


def _sg_bwd(lay, p3, sgw, sgb8, dy):
    per = _sg_chunks(lay)

    def body(u_ref, v_ref, z_ref, w_ref, b_ref, dy_ref, dp_ref, dw_ref, db_ref):
        first = (pl.program_id(0) == 0) & (pl.program_id(1) == 0)
        dw = db = None
        for k in range(per):
            rows = pl.ds(RC * k, RC)
            _, vjp = jax.vjp(_sg_fn, u_ref[0, rows, :], v_ref[0, rows, :], z_ref[0, rows, :], w_ref[...], b_ref[...])
            g = vjp(dy_ref[0, rows, :])
            for n in range(3):
                dp_ref[0, rows, BRW * n:BRW * (n + 1)] = g[n].astype(BF16)
            dw = g[3] if dw is None else dw + g[3]
            db = g[4] if db is None else db + g[4]
        _acc(dw_ref, dw, first)
        _acc(db_ref, db, first)

    return pl.pallas_call(
        body, grid=(lay.b, lay.s // (per * RC)),
        in_specs=_sg_specs(lay) + [pl.BlockSpec((1, per * RC, BRW), lambda b, i: (b, i, 0))],
        out_specs=[pl.BlockSpec((1, per * RC, 3 * BRW), lambda b, i: (b, i, 0)),
                   pl.BlockSpec((NH, RC, RC), lambda b, i: (0, 0, 0)), pl.BlockSpec((8, RC), lambda b, i: (0, 0))],
        out_shape=[jax.ShapeDtypeStruct((lay.b, lay.s, 3 * BRW), BF16),
                   jax.ShapeDtypeStruct((NH, RC, RC), F32), jax.ShapeDtypeStruct((8, RC), F32)],
        compiler_params=_params(("arbitrary", "arbitrary")), name="sg_bwd")(p3, p3, p3, sgw, sgb8, dy)


def _sc_specs(lay):
    first = COL_SC * BRW // LANES
    blk = [pl.BlockSpec((1, lay.s, LANES), functools.partial(lambda j, b, c: (b, 0, c + j), c=first + 2 * n))
           for n in range(4)]
    return blk + [pl.BlockSpec((8, LANES), lambda j, b: (0, j))]


def _sc_fwd(lay, p3, w8):
    dn, up = _make_shifts(lay.s, lay.t_ctx)

    def fn(b_, c_, h_, z_, w0, w1, w2):
        return b_ * _conv3(c_ * h_, w0, w1, w2, dn, up) * _silu(z_)

    def body(b_ref, c_ref, h_ref, z_ref, w_ref, y_ref):
        y = fn(b_ref[0], c_ref[0], h_ref[0], z_ref[0], w_ref[0:1, :], w_ref[1:2, :], w_ref[2:3, :])
        y_ref[0] = y.astype(BF16)

    return pl.pallas_call(
        body, grid=(BRW // LANES, lay.b), in_specs=_sc_specs(lay),
        out_specs=pl.BlockSpec((1, lay.s, LANES), lambda j, b: (b, 0, j)),
        out_shape=jax.ShapeDtypeStruct((lay.b, lay.s, BRW), BF16),
        compiler_params=_params(("arbitrary", "arbitrary")), name="sc_fwd")(p3, p3, p3, p3, w8)


def _sc_bwd(lay, p3, w8, dy):
    dn, up = _make_shifts(lay.s, lay.t_ctx)

    def fn(b_, c_, h_, z_, w0, w1, w2):
        return b_ * _conv3(c_ * h_, w0, w1, w2, dn, up) * _silu(z_)

    def body(b_ref, c_ref, h_ref, z_ref, w_ref, dy_ref, db_ref, dc_ref, dh_ref, dz_ref, dw_ref):
        _, vjp = jax.vjp(fn, b_ref[0], c_ref[0], h_ref[0], z_ref[0], w_ref[0:1, :], w_ref[1:2, :], w_ref[2:3, :])
        g = vjp(dy_ref[0])
        for ref, val in zip((db_ref, dc_ref, dh_ref, dz_ref), g[:4]):
            ref[0] = val.astype(BF16)
        dw = jnp.concatenate([g[4], g[5], g[6], jnp.zeros((5, LANES), F32)], axis=0)
        _acc(dw_ref, dw, pl.program_id(1) == 0)

    out = pl.BlockSpec((1, lay.s, LANES), lambda j, b: (b, 0, j))
    return pl.pallas_call(
        body, grid=(BRW // LANES, lay.b), in_specs=_sc_specs(lay) + [out],
        out_specs=[out] * 4 + [pl.BlockSpec((8, LANES), lambda j, b: (0, j))],
        out_shape=[jax.ShapeDtypeStruct((lay.b, lay.s, BRW), BF16)] * 4 + [jax.ShapeDtypeStruct((8, BRW), F32)],
        compiler_params=_params(("arbitrary", "arbitrary")), name="sc_bwd")(p3, p3, p3, p3, w8, dy)


def _gdn_conv_fn(x, w0, w1, w2, normed, dn, up):
    a = _silu(_conv3(x, w0, w1, w2, dn, up))
    nrm = a * lax.rsqrt(_head_sum(a * a) + EPS)
    return jnp.where(normed, nrm, a)


def _gdn_conv(lay, p3, w8):
    dn, up = _make_shifts(lay.s, lay.t_ctx)
    first = COL_GDN * BRW // LANES

    def body(x_ref, w_ref, o_ref):
        normed = pl.program_id(0) < 2 * BRW // LANES
        o_ref[0] = _gdn_conv_fn(x_ref[0], w_ref[0:1, :], w_ref[1:2, :], w_ref[2:3, :], normed, dn, up)

    return pl.pallas_call(
        body, grid=(3 * BRW // LANES, lay.b),
        in_specs=[pl.BlockSpec((1, lay.s, LANES), lambda j, b: (b, 0, first + j)),
                  pl.BlockSpec((8, LANES), lambda j, b: (0, j))],
        out_specs=pl.BlockSpec((1, lay.s, LANES), lambda j, b: (b, 0, j)),
        out_shape=jax.ShapeDtypeStruct((lay.b, lay.s, 3 * BRW), F32),
        compiler_params=_params(("arbitrary", "arbitrary")), name="gdn_conv")(p3, w8)


def _gdn_conv_bwd(lay, p3, w8, dqkv):
    dn, up = _make_shifts(lay.s, lay.t_ctx)
    first = COL_GDN * BRW // LANES

    def body(x_ref, w_ref, g_ref, dx_ref, dw_ref):
        normed = pl.program_id(0) < 2 * BRW // LANES
        fn = lambda x, w0, w1, w2: _gdn_conv_fn(x, w0, w1, w2, normed, dn, up)
        _, vjp = jax.vjp(fn, x_ref[0], w_ref[0:1, :], w_ref[1:2, :], w_ref[2:3, :])
        g = vjp(g_ref[0])
        dx_ref[0] = g[0].astype(BF16)
        dw = jnp.concatenate([g[1], g[2], g[3], jnp.zeros((5, LANES), F32)], axis=0)
        _acc(dw_ref, dw, pl.program_id(1) == 0)

    blk = pl.BlockSpec((1, lay.s, LANES), lambda j, b: (b, 0, j))
    return pl.pallas_call(
        body, grid=(3 * BRW // LANES, lay.b),
        in_specs=[pl.BlockSpec((1, lay.s, LANES), lambda j, b: (b, 0, first + j)),
                  pl.BlockSpec((8, LANES), lambda j, b: (0, j)), blk],
        out_specs=[blk, pl.BlockSpec((8, LANES), lambda j, b: (0, j))],
        out_shape=[jax.ShapeDtypeStruct((lay.b, lay.s, 3 * BRW), BF16), jax.ShapeDtypeStruct((8, 3 * BRW), F32)],
        compiler_params=_params(("arbitrary", "arbitrary")), name="gdn_conv_bwd")(p3, w8, dqkv)


def _tri_inverse(low):
    i, j = _iota(low.shape, low.ndim - 2), _iota(low.shape, low.ndim - 1) % GC
    t = (i == j).astype(F32)
    s = 1
    while s < GC:
        pair = (i // (2 * s)) == (j // (2 * s))
        off = pair & (((i // s) % 2) != ((j // s) % 2))
        cb = jnp.where(off, low, 0.0)
        t = t - (cb if s == 1 else _bdot(t, _stack_heads(_bdot(cb, _stack_heads(t), 1, 0)), 1, 0))
        s *= 2
    return t


@jax.custom_vjp
def _tri_solve(t, low, r1, r2):
    del low
    return _bdot(t, _stack_heads(r1), 1, 0), _bdot(t, _stack_heads(r2), 1, 0)


def _tri_solve_fwd(t, low, r1, r2):
    del low
    x1, x2 = _bdot(t, _stack_heads(r1), 1, 0), _bdot(t, _stack_heads(r2), 1, 0)
    return (x1, x2), (t, x1, x2)


def _tri_solve_bwd(res, g):
    t, x1, x2 = res
    bd = _block_diag(BRW, BRW)
    d1 = _unstack_heads(_bdot(t, g[0], 0, 0) * bd)
    d2 = _unstack_heads(_bdot(t, g[1], 0, 0) * bd)
    dlow = -(_bdot(d1, _stack_heads(x1), 1, 1) + _bdot(d2, _stack_heads(x2), 1, 1))
    return jnp.zeros_like(t), dlow, d1, d2


_tri_solve.defvjp(_tri_solve_fwd, _tri_solve_bwd)

N_PACK = 5


def _gdn_prep_fn(qn, kn, vv, a, alog, dtb, t=None):
    n = qn.shape[0]
    col = _iota((1, 1, LANES), 2)
    xx = a + dtb
    softplus = jnp.maximum(xx, 0.0) + jnp.log(1.0 + jnp.exp(-jnp.abs(xx)))
    g_small = jnp.where(col < 8, -jnp.exp(alog) * softplus, 0.0).reshape(n * GC, LANES)
    beta_small = jax.nn.sigmoid(a).reshape(n * GC, LANES)
    sel_col, sel_head = _iota((LANES, BRW), 0), _iota((LANES, BRW), 1) // HD
    g_l, b_l = [], []
    for d in (0, 1):
        g_l.append(_dotf(g_small, (sel_col == 4 * d + sel_head).astype(F32)))
        b_l.append(_dotf(beta_small, (sel_col == 8 + 4 * d + sel_head).astype(F32)))
    g_l = jnp.concatenate(g_l, axis=0).reshape(2 * n, GC, BRW)
    b_l = jnp.concatenate(b_l, axis=0).reshape(2 * n, GC, BRW)
    rev = _iota((2 * n, 1, 1), 0) >= n
    fwd = jnp.logical_not(rev)
    ri, ci = _iota((1, GC, GC), 1), _iota((1, GC, GC), 2)
    tri = ((fwd & (ri >= ci)) | (rev & (ri <= ci))).astype(F32)
    gc_l = lax.dot_general(tri, g_l, (((2,), (1,)), ((0,), (0,))), precision=HI,
                           preferred_element_type=F32)
    gtot_l = jnp.sum(g_l, axis=1, keepdims=True)
    i, j = _iota((1, GC, BRW), 1), _iota((1, GC, BRW), 2) % GC
    gc_t = jnp.sum(jnp.where(i == j, gc_l, 0.0), axis=1, keepdims=True)
    incl = (fwd & (i >= j)) | (rev & (i <= j))
    strict = (fwd & (i > j)) | (rev & (i < j))
    decay = jnp.where(incl, jnp.exp(jnp.where(incl, gc_l - gc_t, 0.0)), 0.0)
    kn2 = jnp.concatenate([kn, kn], axis=0)
    vv2 = jnp.concatenate([vv, vv], axis=0)
    qs = jnp.concatenate([qn, qn], axis=0) * (HD ** -0.5)
    kst = _stack_heads(kn2)
    kb = kn2 * b_l
    low = jnp.where(strict, mm_nt(kb, kst) * decay, 0.0)
    eg = jnp.exp(gc_l)
    t_inv = _tri_inverse(low) if t is None else t
    u, w = _tri_solve(t_inv, low, vv2 * b_l, kb * eg)
    k_tail = kn2 * jnp.exp(gtot_l - gc_l)
    intra = mm_nt(qs, kst) * decay
    return (u, w, k_tail, qs * eg, intra), jnp.exp(gtot_l), t_inv


def _prep_chunks(lay):
    return 4 if (lay.s // GC) % 4 == 0 else 2


def _gdn_prep_specs(lay):
    rows = _prep_chunks(lay) * GC
    return ([pl.BlockSpec((1, rows, BRW), functools.partial(lambda b, i, c: (b, i, c), c=c)) for c in range(3)]
            + [pl.BlockSpec((1, rows, LANES), lambda b, i: (b, i, COL_A128)),
               pl.BlockSpec((8, LANES), lambda b, i: (0, 0))])


def _gdn_prep(lay, qkv, p3, prm):
    nc, per = lay.s // GC, _prep_chunks(lay)

    def body(q_ref, k_ref, v_ref, a_ref, prm_ref, pack_ref, cd_ref, t_ref):
        chunks = lambda ref: ref[0].reshape(per, GC, ref.shape[-1])
        pack, cd, t_inv = _gdn_prep_fn(chunks(q_ref), chunks(k_ref), chunks(v_ref), chunks(a_ref),
                                       prm_ref[0:1, :], prm_ref[1:2, :])
        for d in (0, 1):
            for n in range(N_PACK):
                pack_ref[0, :, d, n] = pack[n][per * d:per * (d + 1)]
            cd_ref[0, :, d] = cd[per * d:per * (d + 1)]
            t_ref[0, :, d] = t_inv[per * d:per * (d + 1)]

    return pl.pallas_call(
        body, grid=(lay.b, nc // per), in_specs=_gdn_prep_specs(lay),
        out_specs=[pl.BlockSpec((1, per, 2, N_PACK, GC, BRW), lambda b, i: (b, i, 0, 0, 0, 0)),
                   pl.BlockSpec((1, per, 2, 1, BRW), lambda b, i: (b, i, 0, 0, 0)),
                   pl.BlockSpec((1, per, 2, GC, BRW), lambda b, i: (b, i, 0, 0, 0))],
        out_shape=[jax.ShapeDtypeStruct((lay.b, nc, 2, N_PACK, GC, BRW), F32),
                   jax.ShapeDtypeStruct((lay.b, nc, 2, 1, BRW), F32),
                   jax.ShapeDtypeStruct((lay.b, nc, 2, GC, BRW), F32)],
        compiler_params=_params(("arbitrary", "arbitrary")), name="gdn_prep")(qkv, qkv, qkv, p3, prm)


def _gdn_prep_bwd(lay, qkv, p3, prm, dpacks, dcds, t_inv):
    nc, per = lay.s // GC, _prep_chunks(lay)

    def body(q_ref, k_ref, v_ref, a_ref, prm_ref, dpf_ref, dpr_ref, dcf_ref, dcr_ref, t_ref, dqkv_ref, da_ref,
             dprm_ref):
        first = (pl.program_id(0) == 0) & (pl.program_id(1) == 0)
        chunks = lambda ref: ref[0].reshape(per, GC, ref.shape[-1])
        t_inv = jnp.concatenate([t_ref[0, :, 0], t_ref[0, :, 1]], axis=0)
        fn = lambda q, k, v, a, alog, dtb: _gdn_prep_fn(q, k, v, a, alog, dtb, t_inv)[:2]
        _, vjp = jax.vjp(fn, chunks(q_ref), chunks(k_ref), chunks(v_ref), chunks(a_ref),
                         prm_ref[0:1, :], prm_ref[1:2, :])
        dpack = tuple(jnp.concatenate([dpf_ref[0, :, n], dpr_ref[0, :, n]], axis=0) for n in range(N_PACK))
        dq, dk, dv, da, dalog, ddtb = vjp((dpack, jnp.concatenate([dcf_ref[0], dcr_ref[0]], axis=0)))
        dqkv_ref[0, :, 0:BRW] = dq.reshape(per * GC, BRW)
        dqkv_ref[0, :, BRW:2 * BRW] = dk.reshape(per * GC, BRW)
        dqkv_ref[0, :, 2 * BRW:] = dv.reshape(per * GC, BRW)
        da_ref[0] = da.reshape(per * GC, LANES).astype(BF16)
        _acc(dprm_ref, jnp.concatenate([dalog, ddtb, jnp.zeros((6, LANES), F32)], axis=0), first)

    rows_blk = per * GC
    return pl.pallas_call(
        body, grid=(lay.b, nc // per),
        in_specs=_gdn_prep_specs(lay)
        + [pl.BlockSpec((1, per, N_PACK, GC, BRW), lambda b, i: (b, i, 0, 0, 0))] * 2
        + [pl.BlockSpec((1, per, 1, BRW), lambda b, i: (b, i, 0, 0))] * 2
        + [pl.BlockSpec((1, per, 2, GC, BRW), lambda b, i: (b, i, 0, 0, 0))],
        out_specs=[pl.BlockSpec((1, rows_blk, 3 * BRW), lambda b, i: (b, i, 0)),
                   pl.BlockSpec((1, rows_blk, LANES), lambda b, i: (b, i, 0)),
                   pl.BlockSpec((8, LANES), lambda b, i: (0, 0))],
        out_shape=[jax.ShapeDtypeStruct((lay.b, lay.s, 3 * BRW), F32),
                   jax.ShapeDtypeStruct((lay.b, lay.s, LANES), BF16), jax.ShapeDtypeStruct((8, LANES), F32)],
        compiler_params=_params(("arbitrary", "arbitrary")), name="gdn_prep_bwd",
    )(qkv, qkv, qkv, p3, prm, *dpacks, *dcds, t_inv)


def _gdn_step_fn(s, u, w, k_tail, qd, intra, cdec):
    v_new = u - mm(w, s)
    o = mm(qd, s) + mm(intra, _stack_heads(v_new))
    return s * cdec + mm_tn(k_tail, v_new) * _block_diag(BRW, BRW), o


def _order_index(nc_ctx, nc, d, step):
    rev = jnp.where(step < nc_ctx, nc_ctx - 1 - step, nc + nc_ctx - 1 - step)
    return jnp.where(d == 0, step, rev)


def _gdn_scan(lay, pack, cd):
    nc, nc_ctx = lay.s // GC, lay.t_ctx // GC
    chunk = functools.partial(_order_index, nc_ctx, nc)

    def body(pf_ref, pr_ref, cf_ref, cr_ref, of_ref, or_ref, sf_ref, sr_ref, s_scr):
        @pl.when(pl.program_id(0) == 0)
        def _():
            s_scr[...] = jnp.zeros_like(s_scr)

        nb = lay.b
        s = s_scr[...]
        st = _unstack_heads(s)
        sf_ref[:, 0] = st[:nb]
        sr_ref[:, 0] = st[nb:]
        args = [jnp.concatenate([pf_ref[:, 0, 0, n], pr_ref[:, 0, 0, n]], axis=0) for n in range(N_PACK)]
        s_new, o = _gdn_step_fn(s, *args, jnp.concatenate([cf_ref[:, 0, 0], cr_ref[:, 0, 0]], axis=0))
        of_ref[:, 0] = o[:nb]
        or_ref[:, 0] = o[nb:]
        s_scr[...] = s_new

    def pk(d):
        return pl.BlockSpec((lay.b, 1, 1, N_PACK, GC, BRW), lambda t: (0, chunk(d, t), d, 0, 0, 0))

    def cdb(d):
        return pl.BlockSpec((lay.b, 1, 1, 1, BRW), lambda t: (0, chunk(d, t), d, 0, 0))

    def out(d):
        return pl.BlockSpec((lay.b, 1, GC, BRW), lambda t: (0, chunk(d, t), 0, 0))

    return pl.pallas_call(
        body, grid=(nc,), in_specs=[pk(0), pk(1), cdb(0), cdb(1)],
        out_specs=[out(0), out(1), out(0), out(1)],
        out_shape=[jax.ShapeDtypeStruct((lay.b, nc, GC, BRW), F32)] * 4,
        scratch_shapes=[pltpu.VMEM((2 * lay.b, BRW, BRW), F32)],
        compiler_params=_params(("arbitrary",)), name="gdn_scan")(pack, pack, cd, cd)


def _gdn_scan_bwd(lay, pack, cd, states, do):
    nc, nc_ctx = lay.s // GC, lay.t_ctx // GC

    def chunk(d, t):
        return _order_index(nc_ctx, nc, d, nc - 1 - t)

    def body(pf_ref, pr_ref, cf_ref, cr_ref, sf_ref, sr_ref, dof_ref, dor_ref, dpf_ref, dpr_ref, dcf_ref, dcr_ref,
             ds_scr):
        @pl.when(pl.program_id(0) == 0)
        def _():
            ds_scr[...] = jnp.zeros_like(ds_scr)

        nb = lay.b
        both = lambda f, r: jnp.concatenate([f, r], axis=0)
        args = ([_stack_heads(both(sf_ref[:, 0], sr_ref[:, 0]))]
                + [both(pf_ref[:, 0, 0, n], pr_ref[:, 0, 0, n]) for n in range(N_PACK)]
                + [both(cf_ref[:, 0, 0], cr_ref[:, 0, 0])])
        _, vjp = jax.vjp(_gdn_step_fn, *args)
        g = vjp((ds_scr[...], both(dof_ref[...], dor_ref[...])))
        ds_scr[...] = g[0]
        for n in range(N_PACK):
            dpf_ref[:, 0, n] = g[1 + n][:nb]
            dpr_ref[:, 0, n] = g[1 + n][nb:]
        dcf_ref[:, 0] = g[1 + N_PACK][:nb]
        dcr_ref[:, 0] = g[1 + N_PACK][nb:]

    def pk(d):
        return pl.BlockSpec((lay.b, 1, 1, N_PACK, GC, BRW), lambda t: (0, chunk(d, t), d, 0, 0, 0))

    def cdb(d):
        return pl.BlockSpec((lay.b, 1, 1, 1, BRW), lambda t: (0, chunk(d, t), d, 0, 0))

    def st(d):
        return pl.BlockSpec((lay.b, 1, GC, BRW), lambda t: (0, chunk(d, t), 0, 0))

    def dob(d):
        return pl.BlockSpec((lay.b, GC, BRW), lambda t: (0, chunk(d, t), 0))

    def dpk(d):
        return pl.BlockSpec((lay.b, 1, N_PACK, GC, BRW), lambda t: (0, chunk(d, t), 0, 0, 0))

    def dcb(d):
        return pl.BlockSpec((lay.b, 1, 1, BRW), lambda t: (0, chunk(d, t), 0, 0))

    return pl.pallas_call(
        body, grid=(nc,),
        in_specs=[pk(0), pk(1), cdb(0), cdb(1), st(0), st(1), dob(0), dob(1)],
        out_specs=[dpk(0), dpk(1), dcb(0), dcb(1)],
        out_shape=[jax.ShapeDtypeStruct((lay.b, nc, N_PACK, GC, BRW), F32)] * 2
        + [jax.ShapeDtypeStruct((lay.b, nc, 1, BRW), F32)] * 2,
        scratch_shapes=[pltpu.VMEM((2 * lay.b, BRW, BRW), F32)],
        compiler_params=_params(("arbitrary",)), name="gdn_scan_bwd")(pack, pack, cd, cd, *states, do, do)


def _gdn_finish_fn(o, z, ng):
    return o * lax.rsqrt(_head_sum(o * o) * (1.0 / HD) + EPS) * ng * _silu(z)


def _finish_chunks(lay):
    nc = lay.s // GC
    return 12 if nc % 12 == 0 else (6 if nc % 6 == 0 else 2)


def _gdn_o(of_ref, or_ref):
    return (of_ref[0] + or_ref[0]).reshape(of_ref.shape[1] * GC, BRW)


def _gdn_finish_specs(lay):
    per = _finish_chunks(lay)
    ob = pl.BlockSpec((1, per, GC, BRW), lambda b, i: (b, i, 0, 0))
    return [ob, ob, pl.BlockSpec((1, per * GC, BRW), lambda b, i: (b, i, COL_GDN + 3)),
            pl.BlockSpec((1, BRW), lambda b, i: (0, 0))]


def _gdn_finish(lay, o_f, o_r, p3, ng):
    rows = _finish_chunks(lay) * GC

    def body(of_ref, or_ref, z_ref, ng_ref, y_ref):
        y_ref[0] = _gdn_finish_fn(_gdn_o(of_ref, or_ref), z_ref[0], ng_ref[...]).astype(BF16)

    return pl.pallas_call(
        body, grid=(lay.b, lay.s // rows), in_specs=_gdn_finish_specs(lay),
        out_specs=pl.BlockSpec((1, rows, BRW), lambda b, i: (b, i, 0)),
        out_shape=jax.ShapeDtypeStruct((lay.b, lay.s, BRW), BF16),
        compiler_params=_params(("arbitrary", "arbitrary")), name="gdn_finish")(o_f, o_r, p3, ng)


def _gdn_finish_bwd(lay, o_f, o_r, p3, ng, dy):
    rows = _finish_chunks(lay) * GC

    def body(of_ref, or_ref, z_ref, ng_ref, dy_ref, do_ref, dz_ref, dng_ref):
        first = (pl.program_id(0) == 0) & (pl.program_id(1) == 0)
        _, vjp = jax.vjp(_gdn_finish_fn, _gdn_o(of_ref, or_ref), z_ref[0], ng_ref[...])
        do, dz, dng = vjp(dy_ref[0])
        do_ref[0] = do
        dz_ref[0] = dz.astype(BF16)
        _acc(dng_ref, dng, first)

    blk = pl.BlockSpec((1, rows, BRW), lambda b, i: (b, i, 0))
    return pl.pallas_call(
        body, grid=(lay.b, lay.s // rows), in_specs=_gdn_finish_specs(lay) + [blk],
        out_specs=[blk, blk, pl.BlockSpec((1, BRW), lambda b, i: (0, 0))],
        out_shape=[jax.ShapeDtypeStruct((lay.b, lay.s, BRW), F32), jax.ShapeDtypeStruct((lay.b, lay.s, BRW), BF16),
                   jax.ShapeDtypeStruct((1, BRW), F32)],
        compiler_params=_params(("arbitrary", "arbitrary")), name="gdn_finish_bwd")(o_f, o_r, p3, ng, dy)


def _rope_tables(lay):
    t = jnp.arange(lay.t_lat)
    lane = np.arange(BRW)
    dim = lane % HD
    inv = jnp.asarray(ROPE_BASE ** (-(dim % 16).astype(np.float32) / 16.0), F32)
    pos = jnp.where((dim // 32 == 0)[None, :], (t // GRID_W)[:, None], (t % GRID_W)[:, None]).astype(F32)
    ang = pos * inv[None, :]
    cos = jnp.concatenate([jnp.ones((lay.t_ctx, BRW), F32), jnp.cos(ang)], axis=0)
    sin = jnp.concatenate([jnp.zeros((lay.t_ctx, BRW), F32), jnp.sin(ang)], axis=0)
    return cos, sin


def _pad_rows(a, rows):
    return jnp.concatenate([a, jnp.zeros((rows - a.shape[0],) + a.shape[1:], a.dtype)], axis=0)


def _layer_fwd(lay, xc, wl, cos, sin):
    p, ht = _inproj_fwd(lay, xc, wl["mod3"], wl["gpre"], wl["win"])
    p3 = p.reshape(lay.b, lay.s, W_PAD)
    nctx = lay.t_ctx // RC
    states = _state_scan(lay, _ret_state(lay, p3, cos, sin), nctx, False, "ret_scan")
    y_ret = _ret_out(lay, p3, cos, sin, states, wl["ret_ng"])
    y_sg = _sg_fwd(lay, p3, wl["sgw"], wl["sgb8"])
    y_sc = _sc_fwd(lay, p3, wl["scw8"])
    qkv = _gdn_conv(lay, p3, wl["gdnw8"])
    pack, cd, t_inv = _gdn_prep(lay, qkv, p3, wl["prm"])
    o_f, o_r, st_f, st_r = _gdn_scan(lay, pack, cd)
    y_gdn = _gdn_finish(lay, o_f, o_r, p3, wl["gdn_ng"])
    ys = [y.reshape(lay.rows, BRW) for y in (y_ret, y_sg, y_sc, y_gdn)]
    xc_new, yt = _outproj_fwd(lay, ys, xc, wl["wout"], wl["gpost"], wl["mod3"])
    saved = dict(xc=xc, p3=p3, ht=ht, yt=yt, states=states, qkv=qkv, pack=pack, cd=cd, t_inv=t_inv, o_f=o_f, o_r=o_r, gstates=(st_f, st_r),
                 ys=ys)
    return xc_new, saved


def _layer_bwd(lay, sv, wl, cos, sin, dxc):
    p3 = sv["p3"]
    as3 = lambda a: a.reshape(lay.b, lay.s, a.shape[-1])
    as2 = lambda a: a.reshape(lay.rows, a.shape[-1])
    dy_ret, dy_sg, dy_sc, dy_gdn, do_, dgpost, dgate = _outproj_bwd(
        lay, sv["ys"], sv["xc"], wl["wout"], wl["gpost"], wl["mod3"], dxc)
    dwout = _weight_grad(lay, sv["yt"], do_, "wout_grad")
    nctx = lay.t_ctx // RC
    dpr, dstates, dret_ng = _ret_out_bwd(lay, p3, cos, sin, sv["states"], wl["ret_ng"], as3(dy_ret))
    d_a = _state_scan(lay, dstates, nctx, True, "ret_scan_bwd")
    dp_ret = _ret_state_bwd(lay, p3, cos, sin, d_a, dpr)
    dp_sg, dsgw, dsgb8 = _sg_bwd(lay, p3, wl["sgw"], wl["sgb8"], as3(dy_sg))
    dsb, dsc_, dsh_, dsz, dscw8 = _sc_bwd(lay, p3, wl["scw8"], as3(dy_sc))
    do, dgz, dgdn_ng = _gdn_finish_bwd(lay, sv["o_f"], sv["o_r"], p3, wl["gdn_ng"], as3(dy_gdn))
    dpf, dpr_, dcf, dcr = _gdn_scan_bwd(lay, sv["pack"], sv["cd"], sv["gstates"], do)
    dqkv, da, dprm = _gdn_prep_bwd(lay, sv["qkv"], p3, wl["prm"], (dpf, dpr_), (dcf, dcr), sv["t_inv"])
    dp_gqkv, dgdnw8 = _gdn_conv_bwd(lay, p3, wl["gdnw8"], dqkv)
    pieces = [(as2(dp_ret), 0), (as2(dp_sg), COL_SG * BRW), (as2(dsb), COL_SC * BRW), (as2(dsc_), (COL_SC + 1) * BRW),
              (as2(dsh_), (COL_SC + 2) * BRW), (as2(dsz), (COL_SC + 3) * BRW), (as2(dp_gqkv), COL_GDN * BRW),
              (as2(dgz), (COL_GDN + 3) * BRW), (as2(da), COL_A128 * LANES)]
    dxc_prev, dgpre, dshift, dscale = _inproj_bwd(lay, sv["xc"], wl["mod3"], wl["gpre"], wl["wint"], dxc, pieces)
    dws = [_weight_grad(lay, sv["ht"], dp, "win_grad_%d" % off) for dp, off in pieces]
    dwin = jnp.concatenate(dws[:-1] + [dws[-1][:, :W_IN - COL_A128 * LANES]], axis=1)

    def rows3(g):
        return jnp.concatenate([g[1], g[3], g[0] + g[2]], axis=0)

    dmod = _pad_rows(jnp.concatenate([rows3(dshift), rows3(dscale), rows3(dgate)], axis=1), 8)
    grads = dict(win=dwin, wout=dwout, gpre=dgpre[0], gpost=dgpost[0], ret_ng=dret_ng[0], sgw=dsgw, sgb=dsgb8[:NH],
                 scw=dscw8[:3], gdnw=dgdnw8[:3], alog=dprm[0, :2 * NH].reshape(2, NH),
                 dtb=dprm[1, :2 * NH].reshape(2, NH), gdn_ng=dgdn_ng.reshape(NH, HD).sum(axis=0), dmod=dmod)
    return dxc_prev, grads


def _local_step(x, c, ctx, c_ctx, first, later, token, bmod, gpre, gpost, ret_ng, sgw, sgb, scw, gdnw, alog, dtb,
                gdn_ng, target):
    depth = bmod.shape[0]
    lay = _Lay(x.shape[0], ctx.shape[1], x.shape[1])
    assert lay.b == 2 and lay.t_ctx % RC == 0 and lay.t_lat % RC == 0
    cos, sin = _rope_tables(lay)
    cvec8 = _pad_rows(jnp.concatenate([c, c_ctx[None]], axis=0), 8) + token[0, 0]
    wmod = first[0]
    mod = _mod_fwd(cvec8, wmod, bmod[:1, None, :])
    xc = jnp.concatenate([ctx, x], axis=1).reshape(lay.rows, D)
    layers, saved = [], []
    for l in range(depth):
        if l == 1:
            rest = later(xc)
            wmod = jnp.concatenate([first[0], rest[0]], axis=0)
            mod = jnp.concatenate([mod, _mod_fwd(cvec8, rest[0], bmod[1:, None, :])], axis=0)
        win, wout = (first[1][0], first[2][0]) if l == 0 else (rest[1][l - 1], rest[2][l - 1])
        wl = dict(mod3=mod[l].reshape(8, 3, D).transpose(1, 0, 2)[:, :, None, :], gpre=gpre[l][None], gpost=gpost[l][None],
                  win=win, wint=jnp.swapaxes(win, 0, 1), wout=wout, ret_ng=ret_ng[l][None], sgw=sgw[l],
                  sgb8=_pad_rows(sgb[l], 8), scw8=_pad_rows(scw[l], 8), gdnw8=_pad_rows(gdnw[l], 8),
                  prm=_pad_rows(jnp.pad(jnp.stack([alog[l].reshape(-1), dtb[l].reshape(-1)]),
                                        ((0, 0), (0, LANES - 2 * NH))), 8),
                  gdn_ng=jnp.tile(gdn_ng[l], NH)[None])
        xc, sv = _layer_fwd(lay, xc, wl, cos, sin)
        layers.append(wl)
        saved.append(sv)
    loss, dxc3 = _loss_kernel(lay, xc.reshape(lay.b, lay.s, D), target)
    dxc = dxc3.reshape(lay.rows, D)
    grads = [None] * depth
    for l in reversed(range(depth)):
        dxc, grads[l] = _layer_bwd(lay, saved[l], layers[l], cos, sin, dxc)
    stacked = {k: jnp.stack([g[k] for g in grads]) for k in grads[0] if k not in ("win", "wout")}
    stacked["win"] = [g["win"] for g in grads]
    stacked["wout"] = [g["wout"] for g in grads]
    dcvec8, dbmod = _mod_bwd(stacked["dmod"], wmod, cvec8)
    stacked["bmod"] = dbmod[:, 0, :]
    stacked["c_ctx"] = dcvec8[2]
    dx = dxc.reshape(lay.b, lay.s, D)[:, lay.t_ctx:, :]
    return loss, dx, stacked, cvec8


MESH = pl.DeviceIdType.MESH
ANY = pl.BlockSpec(memory_space=pl.ANY)


def _me():
    return lax.axis_index("x"), lax.axis_index("y"), lax.axis_index("c")


def _gather_weights(shards, fulls, blocks):
    n = len(shards)

    def body(*refs):
        ins, outs = refs[:n], refs[n:2 * n]
        send_sems, recv_sems, loc_sems = refs[2 * n:]
        x, y, c = _me()
        me, sibling = (x, y, c), (x, y, 1 - c)
        chips = [(1 - x, y), (x, 1 - y), (1 - x, 1 - y)]

        def blk(a, dev):
            return blocks[a](outs[a], 4 * dev[0] + 2 * dev[1] + dev[2])

        def copy(a, k, block, to, src=None):
            return pltpu.make_async_remote_copy(
                src_ref=blk(a, block) if src is None else src, dst_ref=blk(a, block), send_sem=send_sems.at[a, k],
                recv_sem=recv_sems.at[a, k], device_id=to, device_id_type=MESH)

        mine = [pltpu.make_async_copy(ins[a], blk(a, me), loc_sems.at[a]) for a in range(n)]
        for cp in mine:
            cp.start()
        first = []
        for a in range(n):
            first.append(copy(a, 0, me, sibling, src=ins[a]))
            first += [copy(a, 1 + j, me, (*chip, c), src=ins[a]) for j, chip in enumerate(chips)]
        for cp in first:
            cp.start()
        passed = []
        for j, chip in enumerate(chips):
            for a in range(n):
                copy(a, 1 + j, (*chip, c), me).wait_recv()
                fwd = copy(a, 4 + j, (*chip, c), sibling)
                fwd.start()
                passed.append(fwd)
        for a in range(n):
            copy(a, 0, sibling, me).wait_recv()
            for j, chip in enumerate(chips):
                copy(a, 4 + j, (*chip, 1 - c), me).wait_recv()
        for cp in first + passed:
            cp.wait_send()
        for cp in mine:
            cp.wait()

    return pl.pallas_call(
        body, in_specs=[ANY] * n, out_specs=[ANY] * n,
        out_shape=[jax.ShapeDtypeStruct(f, s.dtype) for f, s in zip(fulls, shards)],
        scratch_shapes=[pltpu.SemaphoreType.DMA((n, 7)), pltpu.SemaphoreType.DMA((n, 7)),
                        pltpu.SemaphoreType.DMA((n,))],
        name="gather_weights")(*shards)


HBM = pl.BlockSpec(memory_space=pltpu.HBM)
SEM = pl.BlockSpec(memory_space=pltpu.SEMAPHORE)


def _peer(k, x, y, c):
    return (1 - x if k & 4 else x, 1 - y if k & 2 else y, 1 - c if k & 1 else c)


def _whole(ref, j):
    del j
    return ref


def _gather_start(shards, lands, blocks, name, parts=None):
    n = len(shards)
    parts = parts or [_whole] * n

    def body(*refs):
        ins, land = refs[:n], refs[n:2 * n]
        send_sems, recv_sems = refs[2 * n], refs[2 * n + 1]
        token = refs[-1]
        x, y, c = _me()
        me = 4 * x + 2 * y + c
        for a in range(n):
            for k in range(1, N_DEV):
                px, py, pc = _peer(k, x, y, c)
                pltpu.make_async_remote_copy(
                    src_ref=parts[a](ins[a], 4 * px + 2 * py + pc), dst_ref=blocks[a](land[a], me),
                    send_sem=send_sems.at[7 * a + k - 1], recv_sem=recv_sems.at[7 * a + k - 1],
                    device_id=(px, py, pc), device_id_type=MESH).start()
        token[...] = jnp.zeros_like(token)

    args = [pltpu.with_memory_space_constraint(a, pltpu.HBM) for a in list(shards) + list(lands)]
    out = pl.pallas_call(
        body, name=name,
        out_shape=[pltpu.SemaphoreType.DMA((7 * n,)), pltpu.SemaphoreType.DMA((7 * n,))]
        + [pltpu.HBM(a.shape, a.dtype) for a in args] + [jax.ShapeDtypeStruct((8, LANES), F32)],
        in_specs=[HBM] * (2 * n), out_specs=[SEM, SEM] + [HBM] * (2 * n) + [pl.BlockSpec(memory_space=pltpu.VMEM)],
        input_output_aliases={i: 2 + i for i in range(2 * n)},
        compiler_params=pltpu.CompilerParams(has_side_effects=pltpu.SideEffectType.DATAFLOW_SIDE_EFFECTING),
    )(*args)
    return out[0], out[1], out[2:2 + n], out[2 + n:2 + 2 * n], out[-1]


def _gather_wait(started, after, blocks, name, parts=None):
    send_sems, recv_sems, shards, lands, _ = started
    n = len(shards)
    parts = parts or [_whole] * n

    def body(*refs):
        ins, land = refs[:n], refs[n:2 * n]
        send_sems, recv_sems = refs[2 * n], refs[2 * n + 1]
        x, y, c = _me()
        for a in range(n):
            for k in range(1, N_DEV):
                px, py, pc = _peer(k, x, y, c)
                peer = 4 * px + 2 * py + pc
                cp = pltpu.make_async_remote_copy(
                    src_ref=parts[a](ins[a], peer), dst_ref=blocks[a](land[a], peer),
                    send_sem=send_sems.at[7 * a + k - 1], recv_sem=recv_sems.at[7 * a + k - 1],
                    device_id=(px, py, pc), device_id_type=MESH)
                cp.wait_send()
                cp.wait_recv()

    out = pl.pallas_call(
        body, name=name,
        out_shape=[pltpu.HBM(a.shape, a.dtype) for a in list(shards) + list(lands)],
        in_specs=[HBM] * (2 * n) + [SEM, SEM, ANY], out_specs=[HBM] * (2 * n),
        input_output_aliases={i: i for i in range(2 * n)},
        compiler_params=pltpu.CompilerParams(has_side_effects=pltpu.SideEffectType.DATAFLOW_SIDE_EFFECTING),
    )(*shards, *lands, send_sems, recv_sems, after)
    return out[:n], out[n:]


def _scatter_pair(srcs, slabs, slab_shapes):
    n = len(srcs)

    def body(*refs):
        ins, outs = refs[:n], refs[n:2 * n]
        send_sems, recv_sems = refs[2 * n:]
        x, y, c = _me()
        cps = []
        for a in range(n):
            for q in range(4):
                j = 2 * q + (1 - c)
                cps.append(pltpu.make_async_remote_copy(
                    src_ref=slabs[a](ins[a], j), dst_ref=outs[a].at[q], send_sem=send_sems.at[a, q],
                    recv_sem=recv_sems.at[a, q], device_id=(x, y, 1 - c), device_id_type=MESH))
        for cp in cps:
            cp.start()
        for cp in cps:
            cp.wait_recv()
        for cp in cps:
            cp.wait_send()

    return pl.pallas_call(
        body, in_specs=[ANY] * n, out_specs=[ANY] * n,
        out_shape=[jax.ShapeDtypeStruct((4,) + tuple(shp), s.dtype) for shp, s in zip(slab_shapes, srcs)],
        scratch_shapes=[pltpu.SemaphoreType.DMA((n, 4)), pltpu.SemaphoreType.DMA((n, 4))],
        name="scatter_pair")(*srcs)


def _scatter_chips(parts, small):
    n = len(parts)

    def body(*refs):
        ins, small_ref = refs[:n], refs[n]
        outs, all_ref = refs[n + 1:2 * n + 1], refs[2 * n + 1]
        send_sems, recv_sems, g_send, g_recv, loc_sem = refs[2 * n + 2:]
        x, y, c = _me()
        me = 4 * x + 2 * y + c
        chips = [(1 - x, y), (x, 1 - y), (1 - x, 1 - y)]
        cps = []
        for a in range(n):
            for k, (px, py) in enumerate(chips):
                cps.append(pltpu.make_async_remote_copy(
                    src_ref=ins[a].at[2 * px + py], dst_ref=outs[a].at[k], send_sem=send_sems.at[a, k],
                    recv_sem=recv_sems.at[a, k], device_id=(px, py, c), device_id_type=MESH))

        def gather(k, dst_blk, peer_xyz):
            return pltpu.make_async_remote_copy(
                src_ref=small_ref, dst_ref=all_ref.at[dst_blk], send_sem=g_send.at[k], recv_sem=g_recv.at[k],
                device_id=peer_xyz, device_id_type=MESH)

        local = pltpu.make_async_copy(small_ref, all_ref.at[me], loc_sem)
        local.start()
        peers = []
        for k in range(1, N_DEV):
            px = 1 - x if k & 4 else x
            py = 1 - y if k & 2 else y
            pc = 1 - c if k & 1 else c
            peers.append((4 * px + 2 * py + pc, (px, py, pc)))
        sends = [gather(k, me, xyz) for k, (_, xyz) in enumerate(peers)]
        for cp in sends + cps:
            cp.start()
        for k, (peer, xyz) in enumerate(peers):
            gather(k, peer, xyz).wait_recv()
        for cp in cps:
            cp.wait_recv()
        for cp in sends + cps:
            cp.wait_send()
        local.wait()

    return pl.pallas_call(
        body, in_specs=[ANY] * (n + 1), out_specs=[ANY] * (n + 1),
        out_shape=[jax.ShapeDtypeStruct((3,) + p.shape[1:], p.dtype) for p in parts]
        + [jax.ShapeDtypeStruct((N_DEV,) + small.shape, small.dtype)],
        scratch_shapes=[pltpu.SemaphoreType.DMA((n, 3)), pltpu.SemaphoreType.DMA((n, 3)),
                        pltpu.SemaphoreType.DMA((N_DEV - 1,)), pltpu.SemaphoreType.DMA((N_DEV - 1,)),
                        pltpu.SemaphoreType.DMA(())],
        name="scatter_chips")(*parts, small)


def _add_rows(arrs, out_dtype, name):
    shp = arrs[0].shape
    two = [a.reshape(-1, shp[-1]) for a in arrs]
    rows, cols = two[0].shape
    tr = _row_tile(rows, 1024)

    def body(*refs):
        acc = refs[0][...].astype(F32)
        for r in refs[1:-1]:
            acc = acc + r[...].astype(F32)
        refs[-1][...] = acc.astype(out_dtype)

    blk = pl.BlockSpec((tr, cols), lambda i: (i, 0))
    return pl.pallas_call(
        body, grid=(rows // tr,), in_specs=[blk] * len(two), out_specs=blk,
        out_shape=jax.ShapeDtypeStruct((rows, cols), out_dtype),
        compiler_params=_params(("arbitrary",)), name=name)(*two).reshape(shp)


def _row_tile(rows, cap):
    best = 8
    for t in range(8, min(rows, cap) + 1, 8):
        if rows % t == 0:
            best = t
    return best


def _sum_devices(x):
    _, rows, cols = x.shape
    tr = _row_tile(rows, 2048)

    def body(x_ref, o_ref):
        acc = x_ref[0]
        for j in range(1, N_DEV):
            acc = acc + x_ref[j]
        o_ref[...] = acc

    return pl.pallas_call(
        body, grid=(rows // tr,), in_specs=[pl.BlockSpec((N_DEV, tr, cols), lambda i: (0, i, 0))],
        out_specs=pl.BlockSpec((tr, cols), lambda i: (i, 0)), out_shape=jax.ShapeDtypeStruct((rows, cols), F32),
        compiler_params=_params(("arbitrary",)), name="sum_devices")(x)


def _adamw(w, g, m, v, name):
    rows, cols = w.shape
    tr = _row_tile(rows, 512)
    bc1 = 1.0 - ADAM_B1 ** ADAM_STEP
    bc2 = 1.0 - ADAM_B2 ** ADAM_STEP

    def body(w_ref, g_ref, m_ref, v_ref, d_ref, nm_ref, nv_ref):
        g_ = g_ref[...]
        m_ = ADAM_B1 * m_ref[...] + (1.0 - ADAM_B1) * g_
        v_ = ADAM_B2 * v_ref[...] + (1.0 - ADAM_B2) * (g_ * g_)
        d_ref[...] = -ADAM_LR * ((m_ / bc1) / (jnp.sqrt(v_ / bc2) + ADAM_EPS) + ADAM_WD * w_ref[...])
        nm_ref[...] = m_
        nv_ref[...] = v_

    blk = pl.BlockSpec((tr, cols), lambda i: (i, 0))
    return pl.pallas_call(
        body, grid=(rows // tr,), in_specs=[blk] * 4, out_specs=[blk] * 3,
        out_shape=[jax.ShapeDtypeStruct((rows, cols), F32)] * 3,
        compiler_params=_params(("arbitrary",)), name=name)(w, g, m, v)


def _pack_rows(shape):
    return -(-int(np.prod(shape)) // (16 * LANES)) * 16


def _pack(arrs, dtype=F32):
    blocks = []
    for a in arrs:
        flat = a.reshape(-1).astype(dtype)
        rows = _pack_rows(a.shape)
        blocks.append(jnp.pad(flat, (0, rows * LANES - flat.shape[0])).reshape(rows, LANES))
    return jnp.concatenate(blocks, axis=0)


def _unpack(packed, shapes):
    out, off = [], 0
    for s in shapes:
        rows = _pack_rows(s)
        out.append(packed[off:off + rows].reshape(-1)[:int(np.prod(s))].reshape(s))
        off += rows
    return out


SMALL = ("c_ctx", "b_mod", "g_pre", "g_post", "ret_norm_g", "sg_w", "sg_b", "sc_conv_w", "gdn_conv_w", "gdn_a_log",
         "gdn_dt_bias", "gdn_norm_g")
ORDER = ("c_ctx", "w_mod", "b_mod", "g_pre", "g_post", "w_in", "w_out", "ret_norm_g", "sg_w", "sg_b", "sc_conv_w",
         "gdn_conv_w", "gdn_a_log", "gdn_dt_bias", "gdn_norm_g")


def kernel(x, c, ctx, c_ctx, w_mod, b_mod, g_pre, g_post, w_in, w_out, ret_norm_g, sg_w, sg_b, sc_conv_w, gdn_conv_w, gdn_a_log, gdn_dt_bias, gdn_norm_g, loss_target, m_c_ctx, m_w_mod, m_b_mod, m_g_pre, m_g_post, m_w_in, m_w_out, m_ret_norm_g, m_sg_w, m_sg_b, m_sc_conv_w, m_gdn_conv_w, m_gdn_a_log, m_gdn_dt_bias, m_gdn_norm_g, v_c_ctx, v_w_mod, v_b_mod, v_g_pre, v_g_post, v_w_in, v_w_out, v_ret_norm_g, v_sg_w, v_sg_b, v_sc_conv_w, v_gdn_conv_w, v_gdn_a_log, v_gdn_dt_bias, v_gdn_norm_g):
    wts = dict(c_ctx=c_ctx, w_mod=w_mod, b_mod=b_mod, g_pre=g_pre, g_post=g_post, w_in=w_in, w_out=w_out,
               ret_norm_g=ret_norm_g, sg_w=sg_w, sg_b=sg_b, sc_conv_w=sc_conv_w, gdn_conv_w=gdn_conv_w,
               gdn_a_log=gdn_a_log, gdn_dt_bias=gdn_dt_bias, gdn_norm_g=gdn_norm_g)
    mom = dict(c_ctx=m_c_ctx, w_mod=m_w_mod, b_mod=m_b_mod, g_pre=m_g_pre, g_post=m_g_post, w_in=m_w_in, w_out=m_w_out,
               ret_norm_g=m_ret_norm_g, sg_w=m_sg_w, sg_b=m_sg_b, sc_conv_w=m_sc_conv_w, gdn_conv_w=m_gdn_conv_w,
               gdn_a_log=m_gdn_a_log, gdn_dt_bias=m_gdn_dt_bias, gdn_norm_g=m_gdn_norm_g)
    var = dict(c_ctx=v_c_ctx, w_mod=v_w_mod, b_mod=v_b_mod, g_pre=v_g_pre, g_post=v_g_post, w_in=v_w_in, w_out=v_w_out,
               ret_norm_g=v_ret_norm_g, sg_w=v_sg_w, sg_b=v_sg_b, sc_conv_w=v_sc_conv_w, gdn_conv_w=v_gdn_conv_w,
               gdn_a_log=v_gdn_a_log, gdn_dt_bias=v_gdn_dt_bias, gdn_norm_g=v_gdn_norm_g)
    depth = w_mod.shape[0]
    n_mod, n_in, n_out = w_mod.shape[2], w_in.shape[2], w_out.shape[1]
    n_sc, n_gdn = sc_conv_w.shape[2], gdn_conv_w.shape[2]
    xi, yi, ci = _me()
    me = 4 * xi + 2 * yi + ci

    conv = _pack([sc_conv_w, gdn_conv_w])
    n_conv = depth * 3 * n_sc
    rest = depth - 1
    blocks = [lambda r, j: r.at[:, :, pl.ds(pl.multiple_of(j * n_mod, LANES), n_mod)],
              lambda r, j: r.at[j],
              lambda r, j: r.at[:, pl.ds(pl.multiple_of(j * n_out, 16), n_out), :],
              lambda r, j: r.at[j]]

    def in_place(g):
        return jnp.pad(g.transpose(1, 2, 0, 3).reshape(g.shape[1], D, N_DEV * n_in),
                       ((0, 0), (0, 0), (0, W_PAD - N_DEV * n_in)))

    wmod_0, win_g, wout_0, conv_g = _gather_weights(
        [w_mod[:1].astype(BF16), w_in[:1].astype(BF16), w_out[:1].astype(BF16), conv],
        [(1, D, N_DEV * n_mod), (N_DEV, 1, D, n_in), (1, N_DEV * n_out, D), (N_DEV,) + conv.shape], blocks)
    later_shards = [w_mod[1:].astype(BF16), w_in[1:].astype(BF16), w_out[1:].astype(BF16)]
    zero = jnp.zeros((), jnp.int32)
    lands = [lax.dynamic_update_slice(lax.empty((rest, D, N_DEV * n_mod), BF16), later_shards[0],
                                      (zero, zero, me * n_mod)),
             lax.dynamic_update_slice(lax.empty((N_DEV, rest, D, n_in), BF16), later_shards[1][None],
                                      (me, zero, zero, zero)),
             lax.dynamic_update_slice(lax.empty((rest, N_DEV * n_out, D), BF16), later_shards[2],
                                      (zero, me * n_out, zero))]
    started = _gather_start(later_shards, lands, blocks[:3], "gather_start")

    def later(stream):
        wmod_r, win_r, wout_r = _gather_wait(started, stream, blocks[:3], "gather_wait")[1]
        return wmod_r, in_place(win_r), wout_r

    slabs = [lambda r, j: r.at[j], lambda r, j: r.at[:, pl.ds(pl.multiple_of(j * n_out, 16), n_out), :]]

    r_sc = _pack_rows(sc_conv_w.shape)
    scw_f = conv_g[:, :r_sc].reshape(N_DEV, -1)[:, :n_conv]
    scw_f = scw_f.reshape(N_DEV, depth, 3, n_sc).transpose(1, 2, 0, 3).reshape(depth, 3, -1)
    gdnw_f = conv_g[:, r_sc:].reshape(N_DEV, -1)[:, :depth * 3 * n_gdn]
    gdnw_f = gdnw_f.reshape(N_DEV, depth, 3, n_gdn).transpose(1, 2, 0, 3)
    gdnw_f = gdnw_f.reshape(depth, 3, -1)

    loss8, dx, g, cvec8 = _local_step(x, c, ctx, c_ctx, (wmod_0, in_place(win_g), wout_0), later, started[4], b_mod,
                                      g_pre, g_post, ret_norm_g, sg_w, sg_b, scw_f, gdnw_f, gdn_a_log, gdn_dt_bias,
                                      gdn_norm_g, loss_target)

    gin = jnp.stack(g["win"]).astype(BF16).reshape(depth, D, N_DEV, n_in).transpose(2, 0, 1, 3)
    gout = jnp.stack(g["wout"]).astype(BF16)
    got_in, got_out = _scatter_pair([gin, gout], slabs, [(depth, D, n_in), (depth, n_out, D)])
    mine_in = lax.dynamic_index_in_dim(gin.reshape(4, 2, depth, D, n_in), ci, axis=1, keepdims=False)
    mine_out = lax.dynamic_index_in_dim(gout.reshape(depth, 4, 2, n_out, D), ci, axis=2, keepdims=False)
    mine_out = mine_out.transpose(1, 0, 2, 3)
    local_small = dict(c_ctx=g["c_ctx"], b_mod=g["bmod"], g_pre=g["gpre"], g_post=g["gpost"], ret_norm_g=g["ret_ng"],
                       sg_w=g["sgw"], sg_b=g["sgb"], sc_conv_w=g["scw"], gdn_conv_w=g["gdnw"], gdn_a_log=g["alog"],
                       gdn_dt_bias=g["dtb"], gdn_norm_g=g["gdn_ng"])
    to_sum = _pack([loss8[0, :1]] + [local_small[k] for k in SMALL])
    rows_sum = to_sum.shape[0]
    as_is = _pack([cvec8[:3], g["dmod"][:, :3, :]])
    far_in, far_out, everyone = _scatter_chips([_add_rows([mine_in, got_in], BF16, "pair_sum_in"),
                                                _add_rows([mine_out, got_out], BF16, "pair_sum_out")],
                                               jnp.concatenate([to_sum, as_is], axis=0))
    chip = 2 * xi + yi
    own = lambda a: lax.dynamic_index_in_dim(a, chip, axis=0, keepdims=False)
    grad = dict(w_in=_add_rows([own(mine_in), own(got_in), far_in[0], far_in[1], far_in[2]], F32, "grad_sum_in"),
                w_out=_add_rows([own(mine_out), own(got_out), far_out[0], far_out[1], far_out[2]], F32,
                                "grad_sum_out"))

    small_sum = _unpack(_sum_devices(everyone[:, :rows_sum]), [(1,)] + [local_small[k].shape for k in SMALL])
    loss = small_sum[0][0]
    for k, val in zip(SMALL, small_sum[1:]):
        grad[k] = val
    grad["sc_conv_w"] = lax.dynamic_slice_in_dim(grad["sc_conv_w"], me * n_sc, n_sc, axis=2)
    grad["gdn_conv_w"] = lax.dynamic_slice_in_dim(grad["gdn_conv_w"], me * n_gdn, n_gdn, axis=2)
    r_c = _pack_rows((3, D))
    c_all = everyone[:, rows_sum:rows_sum + r_c].reshape(N_DEV, -1)[:, :3 * D].reshape(N_DEV * 3, D)
    dmod_all = everyone[:, rows_sum + r_c:].reshape(N_DEV, -1)[:, :depth * 9 * D]
    dmod_all = dmod_all.reshape(N_DEV, depth, 3, 3 * D).transpose(1, 0, 2, 3)
    dmod_mine = lax.dynamic_slice_in_dim(dmod_all.reshape(depth, N_DEV * 3, 3 * D), me * n_mod, n_mod, axis=2)
    grad["w_mod"] = _wmod_grad(_pad_rows(c_all, 32), jnp.pad(dmod_mine, ((0, 0), (0, 32 - N_DEV * 3), (0, 0))))

    delta, new_m, new_v = {}, {}, {}
    for k in ("w_mod", "w_in", "w_out"):
        shp = wts[k].shape
        two = lambda a: a.reshape(-1, shp[-1])
        res = _adamw(two(wts[k]), two(grad[k]), two(mom[k]), two(var[k]), "adamw_" + k)
        delta[k], new_m[k], new_v[k] = [r.reshape(shp) for r in res]
    res = _adamw(*[_pack([d[k] for k in SMALL]) for d in (wts, grad, mom, var)], "adamw_small")
    for dst, flat in zip((delta, new_m, new_v), res):
        for k, val in zip(SMALL, _unpack(flat, [wts[k].shape for k in SMALL])):
            dst[k] = val
    return (loss, dx, *[grad[k] for k in ORDER], *[delta[k] for k in ORDER], *[new_m[k] for k in ORDER],
            *[new_v[k] for k in ORDER])
```

```python
import functools
import math

import jax
import jax.numpy as jnp
import numpy as np
from jax import lax
from jax.experimental import pallas as pl
from jax.experimental.pallas import tpu as pltpu

F32, BF16 = jnp.float32, jnp.bfloat16
HI = lax.Precision.HIGHEST

N_DEV = 8
D = 1024
DEPTH = 4
BRW = 256
HD = 64
NH = 4
LANES = 128
GRID_W = 64
ROPE_BASE = 10000.0
W_IN = 15 * BRW + 4 * NH
W_PAD = 31 * LANES
RC = 128
GC = 64
EPS = 1e-6
LOG_GAMMA = tuple(math.log(1.0 - 2.0 ** (-5.0 - h)) for h in range(NH))
ADAM_LR, ADAM_B1, ADAM_B2, ADAM_EPS, ADAM_WD, ADAM_STEP = 0.001, 0.9, 0.999, 1e-08, 0.01, 10
VMEM_LIMIT = 56 * 1024 * 1024

COL_RET, COL_SG, COL_SC, COL_GDN = 0, 4, 7, 11
COL_A128 = 30


def _params(sem):
    return pltpu.CompilerParams(dimension_semantics=sem, vmem_limit_bytes=VMEM_LIMIT)


def _bdot(a, b, ca, cb):
    if a.ndim == 3:
        dn = (((ca + 1,), (cb + 1,)), ((0,), (0,)))
    else:
        dn = (((ca,), (cb,)), ((), ()))
    return lax.dot_general(a.astype(BF16), b.astype(BF16), dn, preferred_element_type=F32)


@jax.custom_vjp
def mm(a, b):
    return _bdot(a, b, 1, 0)


mm.defvjp(lambda a, b: (_bdot(a, b, 1, 0), (a, b)),
          lambda r, g: (_bdot(g, r[1], 1, 1), _bdot(r[0], g, 0, 0)))


@jax.custom_vjp
def mm_nt(a, b):
    return _bdot(a, b, 1, 1)


mm_nt.defvjp(lambda a, b: (_bdot(a, b, 1, 1), (a, b)),
             lambda r, g: (_bdot(g, r[1], 1, 0), _bdot(g, r[0], 0, 0)))


@jax.custom_vjp
def mm_tn(a, b):
    return _bdot(a, b, 0, 0)


mm_tn.defvjp(lambda a, b: (_bdot(a, b, 0, 0), (a, b)),
             lambda r, g: (_bdot(r[1], g, 1, 1), _bdot(r[0], g, 1, 0)))


def _dotf(a, b):
    return jnp.dot(a, b, precision=HI, preferred_element_type=F32)


def _iota(shape, dim):
    return lax.broadcasted_iota(jnp.int32, shape, dim)


def _head_mask(h, width=BRW):
    return (_iota((1, width), 1) // HD == h).astype(F32)


def _lane_by_head(vals, width=BRW, lane0=0):
    head = (_iota((1, width), 1) + lane0) // HD
    out = jnp.full((1, width), vals[NH - 1], F32)
    for h in range(NH - 2, -1, -1):
        out = jnp.where(head == h, vals[h], out)
    return out


def _block_diag(n, width):
    return (_iota((n, width), 0) // HD == _iota((n, width), 1) // HD).astype(F32)


@jax.custom_vjp
def _head_sum(x):
    w = x.shape[1]
    ones = _block_diag(w, w).astype(BF16)
    hi = x.astype(BF16)
    lo = (x - hi.astype(F32)).astype(BF16)
    return jnp.dot(hi, ones, preferred_element_type=F32) + jnp.dot(lo, ones, preferred_element_type=F32)


_head_sum.defvjp(lambda x: (_head_sum(x), None), lambda _, g: (_head_sum(g),))


def _silu(x):
    return x * jax.nn.sigmoid(x)


def _stack_heads(x):
    return jnp.concatenate([x * _head_mask(h) for h in range(NH)], axis=-2)


@jax.custom_vjp
def _unstack_heads(x):
    n = x.shape[-2] // NH
    return (x[..., 0:n, :] + x[..., n:2 * n, :]) + (x[..., 2 * n:3 * n, :] + x[..., 3 * n:4 * n, :])


_unstack_heads.defvjp(lambda x: (_unstack_heads(x), None), lambda _, g: (_stack_heads(g),))


@jax.custom_vjp
def _rot_half(x):
    n = x.shape[1]
    first = (_iota(x.shape, 1) % 32) < 16
    return jnp.where(first, -pltpu.roll(x, n - 16, 1), pltpu.roll(x, 16, 1))


_rot_half.defvjp(lambda x: (_rot_half(x), None), lambda _, g: (-_rot_half(g),))


def _rotary(x, cos, sin):
    return x * cos + _rot_half(x) * sin


def _make_shifts(seq, t_ctx):
    def dn_raw(x):
        r = _iota(x.shape, 0)
        return jnp.where((r == 0) | (r == t_ctx), 0.0, pltpu.roll(x, 1, 0))

    def up_raw(x):
        r = _iota(x.shape, 0)
        return jnp.where((r == t_ctx - 1) | (r == seq - 1), 0.0, pltpu.roll(x, seq - 1, 0))

    @jax.custom_vjp
    def dn(x):
        return dn_raw(x)

    @jax.custom_vjp
    def up(x):
        return up_raw(x)

    dn.defvjp(lambda x: (dn_raw(x), None), lambda _, g: (up_raw(g),))
    up.defvjp(lambda x: (up_raw(x), None), lambda _, g: (dn_raw(g),))
    return dn, up


def _conv3(t, w0, w1, w2, dn, up):
    return dn(t) * w0 + t * w1 + up(t) * w2


def _acc(ref, val, first, at=()):
    idx = at + (Ellipsis,)

    @pl.when(first)
    def _():
        ref[idx] = val

    @pl.when(jnp.logical_not(first))
    def _():
        ref[idx] += val


def _mod_fwd(cvec8, wmod, bmod):
    depth = wmod.shape[0]

    def body(c_ref, w_ref, b_ref, o_ref):
        sc = _silu(c_ref[...])
        o_ref[0] = jnp.dot(sc.astype(BF16), w_ref[0], preferred_element_type=F32) + b_ref[0]

    return pl.pallas_call(
        body, grid=(depth, 3),
        in_specs=[pl.BlockSpec((8, D), lambda l, j: (0, 0)),
                  pl.BlockSpec((1, D, D), lambda l, j: (l, 0, j)),
                  pl.BlockSpec((1, 1, D), lambda l, j: (l, 0, j))],
        out_specs=pl.BlockSpec((1, 8, D), lambda l, j: (l, 0, j)),
        out_shape=jax.ShapeDtypeStruct((depth, 8, 3 * D), F32),
        compiler_params=_params(("arbitrary", "arbitrary")), name="mod_fwd")(cvec8, wmod, bmod)


def _mod_bwd(dmod, wmod, cvec8):
    depth = wmod.shape[0]

    def body(dm_ref, w_ref, c_ref, dc_ref, db_ref):
        l, j = pl.program_id(0), pl.program_id(1)
        dm = dm_ref[0]
        db_ref[0] = jnp.sum(dm, axis=0, keepdims=True)
        part = _bdot(dm, w_ref[0], 1, 1)
        _acc(dc_ref, part, (l == 0) & (j == 0))

        @pl.when((l == depth - 1) & (j == 2))
        def _():
            c = c_ref[...]
            s = jax.nn.sigmoid(c)
            dc_ref[...] = dc_ref[...] * (s * (1.0 + c * (1.0 - s)))

    return pl.pallas_call(
        body, grid=(depth, 3),
        in_specs=[pl.BlockSpec((1, 8, D), lambda l, j: (l, 0, j)),
                  pl.BlockSpec((1, D, D), lambda l, j: (l, 0, j)),
                  pl.BlockSpec((8, D), lambda l, j: (0, 0))],
        out_specs=[pl.BlockSpec((8, D), lambda l, j: (0, 0)),
                   pl.BlockSpec((1, 1, D), lambda l, j: (l, 0, j))],
        out_shape=[jax.ShapeDtypeStruct((8, D), F32), jax.ShapeDtypeStruct((depth, 1, 3 * D), F32)],
        compiler_params=_params(("arbitrary", "arbitrary")), name="mod_bwd")(dmod, wmod, cvec8)


def _wmod_grad(c_rows, dmod_cols):
    depth, rows, n = dmod_cols.shape

    def body(c_ref, dm_ref, o_ref):
        sc = _silu(c_ref[...])
        o_ref[0] = lax.dot_general(sc, dm_ref[0], (((0,), (0,)), ((), ())), precision=HI,
                                   preferred_element_type=F32)

    return pl.pallas_call(
        body, grid=(depth,),
        in_specs=[pl.BlockSpec((rows, D), lambda l: (0, 0)), pl.BlockSpec((1, rows, n), lambda l: (l, 0, 0))],
        out_specs=pl.BlockSpec((1, D, n), lambda l: (l, 0, 0)),
        out_shape=jax.ShapeDtypeStruct((depth, D, n), F32),
        compiler_params=_params(("arbitrary",)), name="wmod_grad")(c_rows, dmod_cols)


class _Lay:
    def __init__(self, batch, t_ctx, t_lat):
        self.b, self.t_ctx, self.t_lat = batch, t_ctx, t_lat
        self.s = t_ctx + t_lat
        self.tm = min(256, t_ctx)
        self.tpb = self.s // self.tm
        self.nct = t_ctx // self.tm
        self.ntiles = batch * self.tpb
        self.rows = batch * self.s

    def mod_row(self, i):
        return jnp.where(i % self.tpb < self.nct, 2, i // self.tpb)

    def group(self, i):
        return 2 * (i // self.tpb) + jnp.where(i % self.tpb < self.nct, 0, 1)

    def group_first(self, i):
        return (i % self.tpb == 0) | (i % self.tpb == self.nct)


def _norm_mod(x, g, shift, scale):
    r = lax.rsqrt(jnp.mean(x * x, axis=-1, keepdims=True) + EPS)
    return (x * r * g) * (1.0 + scale) + shift


def _inproj_fwd(lay, xc, mod3, gpre, w):
    tm = lay.tm

    def body(x_ref, sh_ref, sc_ref, g_ref, w_ref, p_ref, ht_ref):
        h = _norm_mod(x_ref[...], g_ref[...], sh_ref[0, 0], sc_ref[0, 0])
        ht_ref[...] = h.T.astype(BF16)
        p_ref[...] = jnp.dot(h.astype(BF16), w_ref[...], preferred_element_type=F32)

    return pl.pallas_call(
        body, grid=(lay.ntiles,),
        in_specs=[pl.BlockSpec((tm, D), lambda i: (i, 0)),
                  pl.BlockSpec((1, 1, 1, D), lambda i: (0, lay.mod_row(i), 0, 0)),
                  pl.BlockSpec((1, 1, 1, D), lambda i: (1, lay.mod_row(i), 0, 0)),
                  pl.BlockSpec((1, D), lambda i: (0, 0)),
                  pl.BlockSpec((D, W_PAD), lambda i: (0, 0))],
        out_specs=[pl.BlockSpec((tm, W_PAD), lambda i: (i, 0)), pl.BlockSpec((D, tm), lambda i: (0, i))],
        out_shape=[jax.ShapeDtypeStruct((lay.rows, W_PAD), F32), jax.ShapeDtypeStruct((D, lay.rows), BF16)],
        compiler_params=_params(("arbitrary",)), name="inproj_fwd")(xc, mod3, mod3, gpre, w)


def _inproj_bwd(lay, xc, mod3, gpre, wt, dxc, pieces):
    tm = lay.tm
    npc = len(pieces)
    offs = [off for _, off in pieces]

    def body(*refs):
        x_ref, sh_ref, sc_ref, g_ref, wt_ref, dx_in = refs[:6]
        dps = refs[6:6 + npc]
        dx_ref, dg_ref, dsh_ref, dsc_ref = refs[6 + npc:]
        i = pl.program_id(0)
        dh = None
        for dp_ref, off in zip(dps, offs):
            wd = dp_ref.shape[1]
            part = jnp.dot(dp_ref[...], wt_ref[off:off + wd, :], preferred_element_type=F32)
            dh = part if dh is None else dh + part
        _, vjp = jax.vjp(_norm_mod, x_ref[...], g_ref[...], sh_ref[0, 0], sc_ref[0, 0])
        dx, dg, dsh, dsc = vjp(dh)
        dx_ref[...] = dx_in[...] + dx
        _acc(dg_ref, dg, i == 0)
        first = lay.group_first(i)
        _acc(dsh_ref, dsh, first, at=(0,))
        _acc(dsc_ref, dsc, first, at=(0,))

    return pl.pallas_call(
        body, grid=(lay.ntiles,),
        in_specs=[pl.BlockSpec((tm, D), lambda i: (i, 0)),
                  pl.BlockSpec((1, 1, 1, D), lambda i: (0, lay.mod_row(i), 0, 0)),
                  pl.BlockSpec((1, 1, 1, D), lambda i: (1, lay.mod_row(i), 0, 0)),
                  pl.BlockSpec((1, D), lambda i: (0, 0)),
                  pl.BlockSpec((W_PAD, D), lambda i: (0, 0)),
                  pl.BlockSpec((tm, D), lambda i: (i, 0))]
        + [pl.BlockSpec((tm, dp.shape[1]), lambda i: (i, 0)) for dp, _ in pieces],
        out_specs=[pl.BlockSpec((tm, D), lambda i: (i, 0)),
                   pl.BlockSpec((1, D), lambda i: (0, 0)),
                   pl.BlockSpec((1, 1, D), lambda i: (lay.group(i), 0, 0)),
                   pl.BlockSpec((1, 1, D), lambda i: (lay.group(i), 0, 0))],
        out_shape=[jax.ShapeDtypeStruct((lay.rows, D), F32), jax.ShapeDtypeStruct((1, D), F32),
                   jax.ShapeDtypeStruct((2 * lay.b, 1, D), F32), jax.ShapeDtypeStruct((2 * lay.b, 1, D), F32)],
        compiler_params=_params(("arbitrary",)), name="inproj_bwd",
    )(xc, mod3, mod3, gpre, wt, dxc, *[dp for dp, _ in pieces])


def _weight_grad(lay, ht, dp, name):
    wd = dp.shape[1]
    tn = wd
    tr = lay.rows // 2 if lay.rows % (2 * 256) == 0 else lay.tm

    def body(ht_ref, dp_ref, o_ref):
        _acc(o_ref, jnp.dot(ht_ref[...], dp_ref[...], preferred_element_type=F32), pl.program_id(1) == 0)

    return pl.pallas_call(
        body, grid=(wd // tn, lay.rows // tr),
        in_specs=[pl.BlockSpec((D, tr), lambda j, i: (0, i)), pl.BlockSpec((tr, tn), lambda j, i: (i, j))],
        out_specs=pl.BlockSpec((D, tn), lambda j, i: (0, j)),
        out_shape=jax.ShapeDtypeStruct((D, wd), F32),
        compiler_params=_params(("arbitrary", "arbitrary")), name=name)(ht, dp)


def _outproj_post(o, x, gpost, gate):
    r = lax.rsqrt(jnp.mean(o * o, axis=-1, keepdims=True) + EPS)
    return x + gate * (o * r * gpost)


def _outproj_matmul(ys, w_ref):
    o = None
    for k, y in enumerate(ys):
        part = jnp.dot(y[...], w_ref[BRW * k:BRW * (k + 1), :], preferred_element_type=F32)
        o = part if o is None else o + part
    return o


def _outproj_specs(lay):
    tm = lay.tm
    return ([pl.BlockSpec((tm, BRW), lambda i: (i, 0))] * 4
            + [pl.BlockSpec((tm, D), lambda i: (i, 0))]
            + [pl.BlockSpec((D, D), lambda i: (0, 0))]
            + [pl.BlockSpec((1, D), lambda i: (0, 0))]
            + [pl.BlockSpec((1, 1, 1, D), lambda i: (2, lay.mod_row(i), 0, 0))])


def _outproj_fwd(lay, ys, xc, wout, gpost, mod3):
    tm = lay.tm

    def body(y0, y1, y2, y3, x_ref, w_ref, g_ref, gt_ref, o_ref, yt_ref):
        ys_ = (y0, y1, y2, y3)
        o_ref[...] = _outproj_post(_outproj_matmul(ys_, w_ref), x_ref[...], g_ref[...], gt_ref[0, 0])
        for k, y in enumerate(ys_):
            yt_ref[BRW * k:BRW * (k + 1), :] = y[...].astype(F32).T.astype(BF16)

    return pl.pallas_call(
        body, grid=(lay.ntiles,), in_specs=_outproj_specs(lay),
        out_specs=[pl.BlockSpec((tm, D), lambda i: (i, 0)), pl.BlockSpec((D, tm), lambda i: (0, i))],
        out_shape=[jax.ShapeDtypeStruct((lay.rows, D), F32), jax.ShapeDtypeStruct((D, lay.rows), BF16)],
        compiler_params=_params(("arbitrary",)), name="outproj_fwd")(*ys, xc, wout, gpost, mod3)


def _outproj_bwd(lay, ys, xc, wout, gpost, mod3, dxc):
    tm = lay.tm

    def body(y0, y1, y2, y3, x_ref, w_ref, g_ref, gt_ref, dx_ref, d0, d1, d2, d3, do_ref, dg_ref, dgt_ref):
        i = pl.program_id(0)
        o = _outproj_matmul((y0, y1, y2, y3), w_ref)
        _, vjp = jax.vjp(_outproj_post, o, x_ref[...], g_ref[...], gt_ref[0, 0])
        do, _, dg, dgt = vjp(dx_ref[...])
        do = do.astype(BF16)
        do_ref[...] = do
        for k, d in enumerate((d0, d1, d2, d3)):
            d[...] = _bdot(do, w_ref[BRW * k:BRW * (k + 1), :], 1, 1)
        _acc(dg_ref, dg, i == 0)
        _acc(dgt_ref, dgt, lay.group_first(i), at=(0,))

    return pl.pallas_call(
        body, grid=(lay.ntiles,),
        in_specs=_outproj_specs(lay) + [pl.BlockSpec((tm, D), lambda i: (i, 0))],
        out_specs=[pl.BlockSpec((tm, BRW), lambda i: (i, 0))] * 4
        + [pl.BlockSpec((tm, D), lambda i: (i, 0)), pl.BlockSpec((1, D), lambda i: (0, 0)),
           pl.BlockSpec((1, 1, D), lambda i: (lay.group(i), 0, 0))],
        out_shape=[jax.ShapeDtypeStruct((lay.rows, BRW), F32)] * 4
        + [jax.ShapeDtypeStruct((lay.rows, D), BF16), jax.ShapeDtypeStruct((1, D), F32),
           jax.ShapeDtypeStruct((2 * lay.b, 1, D), F32)],
        compiler_params=_params(("arbitrary",)), name="outproj_bwd")(*ys, xc, wout, gpost, mod3, dxc)


def _loss_kernel(lay, xc3, target):
    tm, nct = lay.tm, lay.nct

    def body(x_ref, t_ref, loss_ref, dx_ref):
        b, i = pl.program_id(0), pl.program_id(1)
        lat = i >= nct
        err = x_ref[0] - t_ref[0]
        dx_ref[0] = jnp.where(lat, err * (1.0 / D), 0.0)
        part = jnp.sum(jnp.sum(err * err, axis=1, keepdims=True), axis=0, keepdims=True) * (0.5 / D)
        part = jnp.broadcast_to(jnp.where(lat, part, 0.0), (8, LANES))
        _acc(loss_ref, part, (b == 0) & (i == 0))

    return pl.pallas_call(
        body, grid=(lay.b, lay.tpb),
        in_specs=[pl.BlockSpec((1, tm, D), lambda b, i: (b, i, 0)),
                  pl.BlockSpec((1, tm, D), lambda b, i: (b, jnp.maximum(i - nct, 0), 0))],
        out_specs=[pl.BlockSpec((8, LANES), lambda b, i: (0, 0)), pl.BlockSpec((1, tm, D), lambda b, i: (b, i, 0))],
        out_shape=[jax.ShapeDtypeStruct((8, LANES), F32), jax.ShapeDtypeStruct(xc3.shape, F32)],
        compiler_params=_params(("arbitrary", "arbitrary")), name="loss")(xc3, target)


def _chunk_orders(n_ctx, n_all):
    fwd = list(range(n_all))
    rev = list(range(n_ctx - 1, -1, -1)) + list(range(n_all - 1, n_ctx - 1, -1))
    return fwd, rev


def _ret_state_fn(k, v, cos, sin):
    kt = _rotary(k, cos, sin) * (HD ** -0.5)
    lg = _lane_by_head(LOG_GAMMA)
    j = _iota((RC, 1), 0).astype(F32)
    bd = _block_diag(BRW, BRW)
    af = mm_tn(kt * jnp.exp((RC - 1.0 - j) * lg), v) * bd
    ar = mm_tn(kt * jnp.exp(j * lg), v) * bd
    return af, ar


def _ret_out_fn(q, k, v, z, cos, sin, sf, sr, ng):
    qt = _rotary(q, cos, sin)
    kt = _rotary(k, cos, sin) * (HD ** -0.5)
    diff = (_iota((RC, RC), 0) - _iota((RC, RC), 1)).astype(F32)
    o = None
    for h in range(NH):
        m = _head_mask(h)
        sc = mm_nt(qt * m, kt)
        wgt = sc * jnp.exp(jnp.abs(diff) * LOG_GAMMA[h]) * jnp.where(diff == 0, 2.0, 1.0)
        part = mm(wgt, v * m)
        o = part if o is None else o + part
    lg = _lane_by_head(LOG_GAMMA)
    i = _iota((RC, 1), 0).astype(F32)
    o = o + mm(qt, sf) * jnp.exp((i + 1.0) * lg) + mm(qt, sr) * jnp.exp((RC - i) * lg)
    mu = _head_sum(o) * (1.0 / HD)
    cen = o - mu
    var = _head_sum(cen * cen) * (1.0 / HD)
    return cen * lax.rsqrt(var + EPS) * ng * _silu(z)


def _ret_chunks(lay):
    nc = lay.s // RC
    return 6 if nc % 6 == 0 else (2 if nc % 2 == 0 else 1)


def _ret_specs(lay, cols):
    rows = _ret_chunks(lay) * RC
    return [pl.BlockSpec((1, rows, BRW), functools.partial(lambda b, i, c: (b, i, c), c=COL_RET + c)) for c in cols]


def _ret_state(lay, p3, cos, sin):
    nc, per = lay.s // RC, _ret_chunks(lay)

    def body(k_ref, v_ref, c_ref, s_ref, a_ref):
        for n in range(per):
            rows = pl.ds(RC * n, RC)
            af, ar = _ret_state_fn(k_ref[0, rows, :], v_ref[0, rows, :], c_ref[rows, :], s_ref[rows, :])
            a_ref[0, n, 0] = af
            a_ref[0, n, 1] = ar

    tab = pl.BlockSpec((per * RC, BRW), lambda b, i: (i, 0))
    return pl.pallas_call(
        body, grid=(lay.b, nc // per), in_specs=_ret_specs(lay, (1, 2)) + [tab, tab],
        out_specs=pl.BlockSpec((1, per, 2, BRW, BRW), lambda b, i: (b, i, 0, 0, 0)),
        out_shape=jax.ShapeDtypeStruct((lay.b, nc, 2, BRW, BRW), F32),
        compiler_params=_params(("arbitrary", "arbitrary")), name="ret_state")(p3, p3, cos, sin)


def _ret_state_bwd(lay, p3, cos, sin, d_a, dpr):
    nc, per = lay.s // RC, _ret_chunks(lay)

    def body(k_ref, v_ref, c_ref, s_ref, da_ref, dpr_ref, o_ref):
        for n in range(per):
            rows = pl.ds(RC * n, RC)
            cos_, sin_ = c_ref[rows, :], s_ref[rows, :]
            _, vjp = jax.vjp(lambda k, v: _ret_state_fn(k, v, cos_, sin_), k_ref[0, rows, :], v_ref[0, rows, :])
            dk, dv = vjp((da_ref[0, n, 0], da_ref[0, n, 1]))
            o_ref[0, rows, 0:BRW] = dpr_ref[0, rows, 0:BRW].astype(BF16)
            o_ref[0, rows, BRW:2 * BRW] = (dpr_ref[0, rows, BRW:2 * BRW] + dk).astype(BF16)
            o_ref[0, rows, 2 * BRW:3 * BRW] = (dpr_ref[0, rows, 2 * BRW:3 * BRW] + dv).astype(BF16)
            o_ref[0, rows, 3 * BRW:] = dpr_ref[0, rows, 3 * BRW:].astype(BF16)

    tab = pl.BlockSpec((per * RC, BRW), lambda b, i: (i, 0))
    return pl.pallas_call(
        body, grid=(lay.b, nc // per),
        in_specs=_ret_specs(lay, (1, 2)) + [tab, tab,
                                            pl.BlockSpec((1, per, 2, BRW, BRW), lambda b, i: (b, i, 0, 0, 0)),
                                            pl.BlockSpec((1, per * RC, 4 * BRW), lambda b, i: (b, i, 0))],
        out_specs=pl.BlockSpec((1, per * RC, 4 * BRW), lambda b, i: (b, i, 0)),
        out_shape=jax.ShapeDtypeStruct((lay.b, lay.s, 4 * BRW), BF16),
        compiler_params=_params(("arbitrary", "arbitrary")), name="ret_state_bwd")(p3, p3, cos, sin, d_a, dpr)


def _state_scan(lay, a, nc_ctx, transpose, name):
    b, nc = a.shape[0], a.shape[1]
    orders = _chunk_orders(nc_ctx, nc)

    def body(a_ref, o_ref):
        d, jh = pl.program_id(1), pl.program_id(2)
        head = (_iota((1, LANES), 1) + jh * LANES) // HD
        lg = jnp.full((1, LANES), LOG_GAMMA[NH - 1], F32)
        for h in range(NH - 2, -1, -1):
            lg = jnp.where(head == h, LOG_GAMMA[h], lg)
        dec = jnp.exp(RC * lg)
        for dd in (0, 1):
            @pl.when(d == dd)
            def _(order=orders[dd]):
                acc = jnp.zeros((BRW, LANES), F32)
                if not transpose:
                    for c in order:
                        o_ref[0, c, 0] = acc
                        acc = acc * dec + a_ref[0, c, 0]
                else:
                    for c in reversed(order):
                        o_ref[0, c, 0] = acc
                        acc = a_ref[0, c, 0] + acc * dec

    spec = pl.BlockSpec((1, nc, 1, BRW, LANES), lambda bb, d, jh: (bb, 0, d, 0, jh))
    return pl.pallas_call(
        body, grid=(b, 2, BRW // LANES), in_specs=[spec], out_specs=spec,
        out_shape=jax.ShapeDtypeStruct(a.shape, F32),
        compiler_params=_params(("arbitrary",) * 3), name=name)(a)


def _ret_out(lay, p3, cos, sin, states, ng):
    nc, per = lay.s // RC, _ret_chunks(lay)

    def body(q_ref, k_ref, v_ref, z_ref, c_ref, s_ref, st_ref, ng_ref, y_ref):
        for n in range(per):
            rows = pl.ds(RC * n, RC)
            y = _ret_out_fn(q_ref[0, rows, :], k_ref[0, rows, :], v_ref[0, rows, :], z_ref[0, rows, :],
                            c_ref[rows, :], s_ref[rows, :], st_ref[0, n, 0], st_ref[0, n, 1], ng_ref[...])
            y_ref[0, rows, :] = y.astype(BF16)

    tab = pl.BlockSpec((per * RC, BRW), lambda b, i: (i, 0))
    return pl.pallas_call(
        body, grid=(lay.b, nc // per),
        in_specs=_ret_specs(lay, (0, 1, 2, 3)) + [tab, tab,
                                                  pl.BlockSpec((1, per, 2, BRW, BRW), lambda b, i: (b, i, 0, 0, 0)),
                                                  pl.BlockSpec((1, BRW), lambda b, i: (0, 0))],
        out_specs=pl.BlockSpec((1, per * RC, BRW), lambda b, i: (b, i, 0)),
        out_shape=jax.ShapeDtypeStruct((lay.b, lay.s, BRW), BF16),
        compiler_params=_params(("arbitrary", "arbitrary")), name="ret_out")(p3, p3, p3, p3, cos, sin, states, ng)


def _ret_out_bwd(lay, p3, cos, sin, states, ng, dy):
    nc, per = lay.s // RC, _ret_chunks(lay)

    def body(q_ref, k_ref, v_ref, z_ref, c_ref, s_ref, st_ref, ng_ref, dy_ref, dp_ref, dst_ref, dng_ref):
        b, i = pl.program_id(0), pl.program_id(1)
        dng_sum = None
        for n in range(per):
            rows = pl.ds(RC * n, RC)
            cos_, sin_ = c_ref[rows, :], s_ref[rows, :]
            fn = lambda q, k, v, z, sf, sr, ng: _ret_out_fn(q, k, v, z, cos_, sin_, sf, sr, ng)
            _, vjp = jax.vjp(fn, q_ref[0, rows, :], k_ref[0, rows, :], v_ref[0, rows, :], z_ref[0, rows, :],
                             st_ref[0, n, 0], st_ref[0, n, 1], ng_ref[...])
            dq, dk, dv, dz, dsf, dsr, dng = vjp(dy_ref[0, rows, :])
            for m, g in enumerate((dq, dk, dv, dz)):
                dp_ref[0, rows, BRW * m:BRW * (m + 1)] = g
            dst_ref[0, n, 0] = dsf
            dst_ref[0, n, 1] = dsr
            dng_sum = dng if dng_sum is None else dng_sum + dng
        _acc(dng_ref, dng_sum, (b == 0) & (i == 0))

    tab = pl.BlockSpec((per * RC, BRW), lambda b, i: (i, 0))
    st = pl.BlockSpec((1, per, 2, BRW, BRW), lambda b, i: (b, i, 0, 0, 0))
    return pl.pallas_call(
        body, grid=(lay.b, nc // per),
        in_specs=_ret_specs(lay, (0, 1, 2, 3)) + [tab, tab, st, pl.BlockSpec((1, BRW), lambda b, i: (0, 0)),
                                                  pl.BlockSpec((1, per * RC, BRW), lambda b, i: (b, i, 0))],
        out_specs=[pl.BlockSpec((1, per * RC, 4 * BRW), lambda b, i: (b, i, 0)), st,
                   pl.BlockSpec((1, BRW), lambda b, i: (0, 0))],
        out_shape=[jax.ShapeDtypeStruct((lay.b, lay.s, 4 * BRW), F32),
                   jax.ShapeDtypeStruct(states.shape, F32), jax.ShapeDtypeStruct((1, BRW), F32)],
        compiler_params=_params(("arbitrary", "arbitrary")), name="ret_out_bwd",
    )(p3, p3, p3, p3, cos, sin, states, ng, dy)


def _sg_fn(u, v, z, w, b8):
    ug = jax.nn.gelu(u)
    vg = jax.nn.gelu(v)
    mu = jnp.mean(vg, axis=-1, keepdims=True)
    cen = vg - mu
    vn = cen * lax.rsqrt(jnp.mean(cen * cen, axis=-1, keepdims=True) + EPS)
    masks = (_iota((NH, 1, BRW), 2) // HD == _iota((NH, 1, BRW), 0)).astype(F32)
    s = jnp.sum(mm(w, vn[None] * masks), axis=0)
    expand = (_iota((8, BRW), 1) // HD == _iota((8, BRW), 0)).astype(F32)
    bias = lax.dot_general(b8, expand, (((0,), (0,)), ((), ())), precision=HI, preferred_element_type=F32)
    return ug * (s + bias) * _silu(z)


def _sg_chunks(lay):
    nc = lay.s // RC
    return 6 if nc % 6 == 0 else (2 if nc % 2 == 0 else 1)


def _sg_specs(lay):
    rows = _sg_chunks(lay) * RC
    return ([pl.BlockSpec((1, rows, BRW), functools.partial(lambda b, i, c: (b, i, c), c=COL_SG + c)) for c in range(3)]
            + [pl.BlockSpec((NH, RC, RC), lambda b, i: (0, 0, 0)), pl.BlockSpec((8, RC), lambda b, i: (0, 0))])


def _sg_fwd(lay, p3, sgw, sgb8):
    per = _sg_chunks(lay)

    def body(u_ref, v_ref, z_ref, w_ref, b_ref, y_ref):
        for k in range(per):
            rows = pl.ds(RC * k, RC)
            y = _sg_fn(u_ref[0, rows, :], v_ref[0, rows, :], z_ref[0, rows, :], w_ref[...], b_ref[...])
            y_ref[0, rows, :] = y.astype(BF16)

    return pl.pallas_call(
        body, grid=(lay.b, lay.s // (per * RC)), in_specs=_sg_specs(lay),
        out_specs=pl.BlockSpec((1, per * RC, BRW), lambda b, i: (b, i, 0)),
        out_shape=jax.ShapeDtypeStruct((lay.b, lay.s, BRW), BF16),
        compiler_params=_params(("arbitrary", "arbitrary")), name="sg_fwd")(p3, p3, p3, sgw, sgb8)


def _sg_bwd(lay, p3, sgw, sgb8, dy):
    per = _sg_chunks(lay)

    def body(u_ref, v_ref, z_ref, w_ref, b_ref, dy_ref, dp_ref, dw_ref, db_ref):
        first = (pl.program_id(0) == 0) & (pl.program_id(1) == 0)
        dw = db = None
        for k in range(per):
            rows = pl.ds(RC * k, RC)
            _, vjp = jax.vjp(_sg_fn, u_ref[0, rows, :], v_ref[0, rows, :], z_ref[0, rows, :], w_ref[...], b_ref[...])
            g = vjp(dy_ref[0, rows, :])
            for n in range(3):
                dp_ref[0, rows, BRW * n:BRW * (n + 1)] = g[n].astype(BF16)
            dw = g[3] if dw is None else dw + g[3]
            db = g[4] if db is None else db + g[4]
        _acc(dw_ref, dw, first)
        _acc(db_ref, db, first)

    return pl.pallas_call(
        body, grid=(lay.b, lay.s // (per * RC)),
        in_specs=_sg_specs(lay) + [pl.BlockSpec((1, per * RC, BRW), lambda b, i: (b, i, 0))],
        out_specs=[pl.BlockSpec((1, per * RC, 3 * BRW), lambda b, i: (b, i, 0)),
                   pl.BlockSpec((NH, RC, RC), lambda b, i: (0, 0, 0)), pl.BlockSpec((8, RC), lambda b, i: (0, 0))],
        out_shape=[jax.ShapeDtypeStruct((lay.b, lay.s, 3 * BRW), BF16),
                   jax.ShapeDtypeStruct((NH, RC, RC), F32), jax.ShapeDtypeStruct((8, RC), F32)],
        compiler_params=_params(("arbitrary", "arbitrary")), name="sg_bwd")(p3, p3, p3, sgw, sgb8, dy)


def _sc_specs(lay):
    first = COL_SC * BRW // LANES
    blk = [pl.BlockSpec((1, lay.s, LANES), functools.partial(lambda j, b, c: (b, 0, c + j), c=first + 2 * n))
           for n in range(4)]
    return blk + [pl.BlockSpec((8, LANES), lambda j, b: (0, j))]


def _sc_fwd(lay, p3, w8):
    dn, up = _make_shifts(lay.s, lay.t_ctx)

    def fn(b_, c_, h_, z_, w0, w1, w2):
        return b_ * _conv3(c_ * h_, w0, w1, w2, dn, up) * _silu(z_)

    def body(b_ref, c_ref, h_ref, z_ref, w_ref, y_ref):
        y = fn(b_ref[0], c_ref[0], h_ref[0], z_ref[0], w_ref[0:1, :], w_ref[1:2, :], w_ref[2:3, :])
        y_ref[0] = y.astype(BF16)

    return pl.pallas_call(
        body, grid=(BRW // LANES, lay.b), in_specs=_sc_specs(lay),
        out_specs=pl.BlockSpec((1, lay.s, LANES), lambda j, b: (b, 0, j)),
        out_shape=jax.ShapeDtypeStruct((lay.b, lay.s, BRW), BF16),
        compiler_params=_params(("arbitrary", "arbitrary")), name="sc_fwd")(p3, p3, p3, p3, w8)


def _sc_bwd(lay, p3, w8, dy):
    dn, up = _make_shifts(lay.s, lay.t_ctx)

    def fn(b_, c_, h_, z_, w0, w1, w2):
        return b_ * _conv3(c_ * h_, w0, w1, w2, dn, up) * _silu(z_)

    def body(b_ref, c_ref, h_ref, z_ref, w_ref, dy_ref, db_ref, dc_ref, dh_ref, dz_ref, dw_ref):
        _, vjp = jax.vjp(fn, b_ref[0], c_ref[0], h_ref[0], z_ref[0], w_ref[0:1, :], w_ref[1:2, :], w_ref[2:3, :])
        g = vjp(dy_ref[0])
        for ref, val in zip((db_ref, dc_ref, dh_ref, dz_ref), g[:4]):
            ref[0] = val.astype(BF16)
        dw = jnp.concatenate([g[4], g[5], g[6], jnp.zeros((5, LANES), F32)], axis=0)
        _acc(dw_ref, dw, pl.program_id(1) == 0)

    out = pl.BlockSpec((1, lay.s, LANES), lambda j, b: (b, 0, j))
    return pl.pallas_call(
        body, grid=(BRW // LANES, lay.b), in_specs=_sc_specs(lay) + [out],
        out_specs=[out] * 4 + [pl.BlockSpec((8, LANES), lambda j, b: (0, j))],
        out_shape=[jax.ShapeDtypeStruct((lay.b, lay.s, BRW), BF16)] * 4 + [jax.ShapeDtypeStruct((8, BRW), F32)],
        compiler_params=_params(("arbitrary", "arbitrary")), name="sc_bwd")(p3, p3, p3, p3, w8, dy)


def _gdn_conv_fn(x, w0, w1, w2, normed, dn, up):
    a = _silu(_conv3(x, w0, w1, w2, dn, up))
    nrm = a * lax.rsqrt(_head_sum(a * a) + EPS)
    return jnp.where(normed, nrm, a)


def _gdn_conv(lay, p3, w8):
    dn, up = _make_shifts(lay.s, lay.t_ctx)
    first = COL_GDN * BRW // LANES

    def body(x_ref, w_ref, o_ref):
        normed = pl.program_id(0) < 2 * BRW // LANES
        o_ref[0] = _gdn_conv_fn(x_ref[0], w_ref[0:1, :], w_ref[1:2, :], w_ref[2:3, :], normed, dn, up)

    return pl.pallas_call(
        body, grid=(3 * BRW // LANES, lay.b),
        in_specs=[pl.BlockSpec((1, lay.s, LANES), lambda j, b: (b, 0, first + j)),
                  pl.BlockSpec((8, LANES), lambda j, b: (0, j))],
        out_specs=pl.BlockSpec((1, lay.s, LANES), lambda j, b: (b, 0, j)),
        out_shape=jax.ShapeDtypeStruct((lay.b, lay.s, 3 * BRW), F32),
        compiler_params=_params(("arbitrary", "arbitrary")), name="gdn_conv")(p3, w8)


def _gdn_conv_bwd(lay, p3, w8, dqkv):
    dn, up = _make_shifts(lay.s, lay.t_ctx)
    first = COL_GDN * BRW // LANES

    def body(x_ref, w_ref, g_ref, dx_ref, dw_ref):
        normed = pl.program_id(0) < 2 * BRW // LANES
        fn = lambda x, w0, w1, w2: _gdn_conv_fn(x, w0, w1, w2, normed, dn, up)
        _, vjp = jax.vjp(fn, x_ref[0], w_ref[0:1, :], w_ref[1:2, :], w_ref[2:3, :])
        g = vjp(g_ref[0])
        dx_ref[0] = g[0].astype(BF16)
        dw = jnp.concatenate([g[1], g[2], g[3], jnp.zeros((5, LANES), F32)], axis=0)
        _acc(dw_ref, dw, pl.program_id(1) == 0)

    blk = pl.BlockSpec((1, lay.s, LANES), lambda j, b: (b, 0, j))
    return pl.pallas_call(
        body, grid=(3 * BRW // LANES, lay.b),
        in_specs=[pl.BlockSpec((1, lay.s, LANES), lambda j, b: (b, 0, first + j)),
                  pl.BlockSpec((8, LANES), lambda j, b: (0, j)), blk],
        out_specs=[blk, pl.BlockSpec((8, LANES), lambda j, b: (0, j))],
        out_shape=[jax.ShapeDtypeStruct((lay.b, lay.s, 3 * BRW), BF16), jax.ShapeDtypeStruct((8, 3 * BRW), F32)],
        compiler_params=_params(("arbitrary", "arbitrary")), name="gdn_conv_bwd")(p3, w8, dqkv)


def _tri_inverse(low):
    i, j = _iota(low.shape, low.ndim - 2), _iota(low.shape, low.ndim - 1) % GC
    t = (i == j).astype(F32)
    s = 1
    while s < GC:
        pair = (i // (2 * s)) == (j // (2 * s))
        off = pair & (((i // s) % 2) != ((j // s) % 2))
        cb = jnp.where(off, low, 0.0)
        t = t - (cb if s == 1 else _bdot(t, _stack_heads(_bdot(cb, _stack_heads(t), 1, 0)), 1, 0))
        s *= 2
    return t


@jax.custom_vjp
def _tri_solve(t, low, r1, r2):
    del low
    return _bdot(t, _stack_heads(r1), 1, 0), _bdot(t, _stack_heads(r2), 1, 0)


def _tri_solve_fwd(t, low, r1, r2):
    del low
    x1, x2 = _bdot(t, _stack_heads(r1), 1, 0), _bdot(t, _stack_heads(r2), 1, 0)
    return (x1, x2), (t, x1, x2)


def _tri_solve_bwd(res, g):
    t, x1, x2 = res
    bd = _block_diag(BRW, BRW)
    d1 = _unstack_heads(_bdot(t, g[0], 0, 0) * bd)
    d2 = _unstack_heads(_bdot(t, g[1], 0, 0) * bd)
    dlow = -(_bdot(d1, _stack_heads(x1), 1, 1) + _bdot(d2, _stack_heads(x2), 1, 1))
    return jnp.zeros_like(t), dlow, d1, d2


_tri_solve.defvjp(_tri_solve_fwd, _tri_solve_bwd)

N_PACK = 5


def _gdn_prep_fn(qn, kn, vv, a, alog, dtb, t=None):
    n = qn.shape[0]
    col = _iota((1, 1, LANES), 2)
    xx = a + dtb
    softplus = jnp.maximum(xx, 0.0) + jnp.log(1.0 + jnp.exp(-jnp.abs(xx)))
    g_small = jnp.where(col < 8, -jnp.exp(alog) * softplus, 0.0).reshape(n * GC, LANES)
    beta_small = jax.nn.sigmoid(a).reshape(n * GC, LANES)
    sel_col, sel_head = _iota((LANES, BRW), 0), _iota((LANES, BRW), 1) // HD
    g_l, b_l = [], []
    for d in (0, 1):
        g_l.append(_dotf(g_small, (sel_col == 4 * d + sel_head).astype(F32)))
        b_l.append(_dotf(beta_small, (sel_col == 8 + 4 * d + sel_head).astype(F32)))
    g_l = jnp.concatenate(g_l, axis=0).reshape(2 * n, GC, BRW)
    b_l = jnp.concatenate(b_l, axis=0).reshape(2 * n, GC, BRW)
    rev = _iota((2 * n, 1, 1), 0) >= n
    fwd = jnp.logical_not(rev)
    ri, ci = _iota((1, GC, GC), 1), _iota((1, GC, GC), 2)
    tri = ((fwd & (ri >= ci)) | (rev & (ri <= ci))).astype(F32)
    gc_l = lax.dot_general(tri, g_l, (((2,), (1,)), ((0,), (0,))), precision=HI,
                           preferred_element_type=F32)
    gtot_l = jnp.sum(g_l, axis=1, keepdims=True)
    i, j = _iota((1, GC, BRW), 1), _iota((1, GC, BRW), 2) % GC
    gc_t = jnp.sum(jnp.where(i == j, gc_l, 0.0), axis=1, keepdims=True)
    incl = (fwd & (i >= j)) | (rev & (i <= j))
    strict = (fwd & (i > j)) | (rev & (i < j))
    decay = jnp.where(incl, jnp.exp(jnp.where(incl, gc_l - gc_t, 0.0)), 0.0)
    kn2 = jnp.concatenate([kn, kn], axis=0)
    vv2 = jnp.concatenate([vv, vv], axis=0)
    qs = jnp.concatenate([qn, qn], axis=0) * (HD ** -0.5)
    kst = _stack_heads(kn2)
    kb = kn2 * b_l
    low = jnp.where(strict, mm_nt(kb, kst) * decay, 0.0)
    eg = jnp.exp(gc_l)
    t_inv = _tri_inverse(low) if t is None else t
    u, w = _tri_solve(t_inv, low, vv2 * b_l, kb * eg)
    k_tail = kn2 * jnp.exp(gtot_l - gc_l)
    intra = mm_nt(qs, kst) * decay
    return (u, w, k_tail, qs * eg, intra), jnp.exp(gtot_l), t_inv


def _prep_chunks(lay):
    return 4 if (lay.s // GC) % 4 == 0 else 2


def _gdn_prep_specs(lay):
    rows = _prep_chunks(lay) * GC
    return ([pl.BlockSpec((1, rows, BRW), functools.partial(lambda b, i, c: (b, i, c), c=c)) for c in range(3)]
            + [pl.BlockSpec((1, rows, LANES), lambda b, i: (b, i, COL_A128)),
               pl.BlockSpec((8, LANES), lambda b, i: (0, 0))])


def _gdn_prep(lay, qkv, p3, prm):
    nc, per = lay.s // GC, _prep_chunks(lay)

    def body(q_ref, k_ref, v_ref, a_ref, prm_ref, pack_ref, cd_ref, t_ref):
        chunks = lambda ref: ref[0].reshape(per, GC, ref.shape[-1])
        pack, cd, t_inv = _gdn_prep_fn(chunks(q_ref), chunks(k_ref), chunks(v_ref), chunks(a_ref),
                                       prm_ref[0:1, :], prm_ref[1:2, :])
        for d in (0, 1):
            for n in range(N_PACK):
                pack_ref[0, :, d, n] = pack[n][per * d:per * (d + 1)]
            cd_ref[0, :, d] = cd[per * d:per * (d + 1)]
            t_ref[0, :, d] = t_inv[per * d:per * (d + 1)]

    return pl.pallas_call(
        body, grid=(lay.b, nc // per), in_specs=_gdn_prep_specs(lay),
        out_specs=[pl.BlockSpec((1, per, 2, N_PACK, GC, BRW), lambda b, i: (b, i, 0, 0, 0, 0)),
                   pl.BlockSpec((1, per, 2, 1, BRW), lambda b, i: (b, i, 0, 0, 0)),
                   pl.BlockSpec((1, per, 2, GC, BRW), lambda b, i: (b, i, 0, 0, 0))],
        out_shape=[jax.ShapeDtypeStruct((lay.b, nc, 2, N_PACK, GC, BRW), F32),
                   jax.ShapeDtypeStruct((lay.b, nc, 2, 1, BRW), F32),
                   jax.ShapeDtypeStruct((lay.b, nc, 2, GC, BRW), F32)],
        compiler_params=_params(("arbitrary", "arbitrary")), name="gdn_prep")(qkv, qkv, qkv, p3, prm)


def _gdn_prep_bwd(lay, qkv, p3, prm, dpacks, dcds, t_inv):
    nc, per = lay.s // GC, _prep_chunks(lay)

    def body(q_ref, k_ref, v_ref, a_ref, prm_ref, dpf_ref, dpr_ref, dcf_ref, dcr_ref, t_ref, dqkv_ref, da_ref,
             dprm_ref):
        first = (pl.program_id(0) == 0) & (pl.program_id(1) == 0)
        chunks = lambda ref: ref[0].reshape(per, GC, ref.shape[-1])
        t_inv = jnp.concatenate([t_ref[0, :, 0], t_ref[0, :, 1]], axis=0)
        fn = lambda q, k, v, a, alog, dtb: _gdn_prep_fn(q, k, v, a, alog, dtb, t_inv)[:2]
        _, vjp = jax.vjp(fn, chunks(q_ref), chunks(k_ref), chunks(v_ref), chunks(a_ref),
                         prm_ref[0:1, :], prm_ref[1:2, :])
        dpack = tuple(jnp.concatenate([dpf_ref[0, :, n], dpr_ref[0, :, n]], axis=0) for n in range(N_PACK))
        dq, dk, dv, da, dalog, ddtb = vjp((dpack, jnp.concatenate([dcf_ref[0], dcr_ref[0]], axis=0)))
        dqkv_ref[0, :, 0:BRW] = dq.reshape(per * GC, BRW)
        dqkv_ref[0, :, BRW:2 * BRW] = dk.reshape(per * GC, BRW)
        dqkv_ref[0, :, 2 * BRW:] = dv.reshape(per * GC, BRW)
        da_ref[0] = da.reshape(per * GC, LANES).astype(BF16)
        _acc(dprm_ref, jnp.concatenate([dalog, ddtb, jnp.zeros((6, LANES), F32)], axis=0), first)

    rows_blk = per * GC
    return pl.pallas_call(
        body, grid=(lay.b, nc // per),
        in_specs=_gdn_prep_specs(lay)
        + [pl.BlockSpec((1, per, N_PACK, GC, BRW), lambda b, i: (b, i, 0, 0, 0))] * 2
        + [pl.BlockSpec((1, per, 1, BRW), lambda b, i: (b, i, 0, 0))] * 2
        + [pl.BlockSpec((1, per, 2, GC, BRW), lambda b, i: (b, i, 0, 0, 0))],
        out_specs=[pl.BlockSpec((1, rows_blk, 3 * BRW), lambda b, i: (b, i, 0)),
                   pl.BlockSpec((1, rows_blk, LANES), lambda b, i: (b, i, 0)),
                   pl.BlockSpec((8, LANES), lambda b, i: (0, 0))],
        out_shape=[jax.ShapeDtypeStruct((lay.b, lay.s, 3 * BRW), F32),
                   jax.ShapeDtypeStruct((lay.b, lay.s, LANES), BF16), jax.ShapeDtypeStruct((8, LANES), F32)],
        compiler_params=_params(("arbitrary", "arbitrary")), name="gdn_prep_bwd",
    )(qkv, qkv, qkv, p3, prm, *dpacks, *dcds, t_inv)


def _gdn_step_fn(s, u, w, k_tail, qd, intra, cdec):
    v_new = u - mm(w, s)
    o = mm(qd, s) + mm(intra, _stack_heads(v_new))
    return s * cdec + mm_tn(k_tail, v_new) * _block_diag(BRW, BRW), o


def _order_index(nc_ctx, nc, d, step):
    rev = jnp.where(step < nc_ctx, nc_ctx - 1 - step, nc + nc_ctx - 1 - step)
    return jnp.where(d == 0, step, rev)


def _gdn_scan(lay, pack, cd):
    nc, nc_ctx = lay.s // GC, lay.t_ctx // GC
    chunk = functools.partial(_order_index, nc_ctx, nc)

    def body(pf_ref, pr_ref, cf_ref, cr_ref, of_ref, or_ref, sf_ref, sr_ref, s_scr):
        @pl.when(pl.program_id(0) == 0)
        def _():
            s_scr[...] = jnp.zeros_like(s_scr)

        nb = lay.b
        s = s_scr[...]
        st = _unstack_heads(s)
        sf_ref[:, 0] = st[:nb]
        sr_ref[:, 0] = st[nb:]
        args = [jnp.concatenate([pf_ref[:, 0, 0, n], pr_ref[:, 0, 0, n]], axis=0) for n in range(N_PACK)]
        s_new, o = _gdn_step_fn(s, *args, jnp.concatenate([cf_ref[:, 0, 0], cr_ref[:, 0, 0]], axis=0))
        of_ref[:, 0] = o[:nb]
        or_ref[:, 0] = o[nb:]
        s_scr[...] = s_new

    def pk(d):
        return pl.BlockSpec((lay.b, 1, 1, N_PACK, GC, BRW), lambda t: (0, chunk(d, t), d, 0, 0, 0))

    def cdb(d):
        return pl.BlockSpec((lay.b, 1, 1, 1, BRW), lambda t: (0, chunk(d, t), d, 0, 0))

    def out(d):
        return pl.BlockSpec((lay.b, 1, GC, BRW), lambda t: (0, chunk(d, t), 0, 0))

    return pl.pallas_call(
        body, grid=(nc,), in_specs=[pk(0), pk(1), cdb(0), cdb(1)],
        out_specs=[out(0), out(1), out(0), out(1)],
        out_shape=[jax.ShapeDtypeStruct((lay.b, nc, GC, BRW), F32)] * 4,
        scratch_shapes=[pltpu.VMEM((2 * lay.b, BRW, BRW), F32)],
        compiler_params=_params(("arbitrary",)), name="gdn_scan")(pack, pack, cd, cd)


def _gdn_scan_bwd(lay, pack, cd, states, do):
    nc, nc_ctx = lay.s // GC, lay.t_ctx // GC

    def chunk(d, t):
        return _order_index(nc_ctx, nc, d, nc - 1 - t)

    def body(pf_ref, pr_ref, cf_ref, cr_ref, sf_ref, sr_ref, dof_ref, dor_ref, dpf_ref, dpr_ref, dcf_ref, dcr_ref,
             ds_scr):
        @pl.when(pl.program_id(0) == 0)
        def _():
            ds_scr[...] = jnp.zeros_like(ds_scr)

        nb = lay.b
        both = lambda f, r: jnp.concatenate([f, r], axis=0)
        args = ([_stack_heads(both(sf_ref[:, 0], sr_ref[:, 0]))]
                + [both(pf_ref[:, 0, 0, n], pr_ref[:, 0, 0, n]) for n in range(N_PACK)]
                + [both(cf_ref[:, 0, 0], cr_ref[:, 0, 0])])
        _, vjp = jax.vjp(_gdn_step_fn, *args)
        g = vjp((ds_scr[...], both(dof_ref[...], dor_ref[...])))
        ds_scr[...] = g[0]
        for n in range(N_PACK):
            dpf_ref[:, 0, n] = g[1 + n][:nb]
            dpr_ref[:, 0, n] = g[1 + n][nb:]
        dcf_ref[:, 0] = g[1 + N_PACK][:nb]
        dcr_ref[:, 0] = g[1 + N_PACK][nb:]

    def pk(d):
        return pl.BlockSpec((lay.b, 1, 1, N_PACK, GC, BRW), lambda t: (0, chunk(d, t), d, 0, 0, 0))

    def cdb(d):
        return pl.BlockSpec((lay.b, 1, 1, 1, BRW), lambda t: (0, chunk(d, t), d, 0, 0))

    def st(d):
        return pl.BlockSpec((lay.b, 1, GC, BRW), lambda t: (0, chunk(d, t), 0, 0))

    def dob(d):
        return pl.BlockSpec((lay.b, GC, BRW), lambda t: (0, chunk(d, t), 0))

    def dpk(d):
        return pl.BlockSpec((lay.b, 1, N_PACK, GC, BRW), lambda t: (0, chunk(d, t), 0, 0, 0))

    def dcb(d):
        return pl.BlockSpec((lay.b, 1, 1, BRW), lambda t: (0, chunk(d, t), 0, 0))

    return pl.pallas_call(
        body, grid=(nc,),
        in_specs=[pk(0), pk(1), cdb(0), cdb(1), st(0), st(1), dob(0), dob(1)],
        out_specs=[dpk(0), dpk(1), dcb(0), dcb(1)],
        out_shape=[jax.ShapeDtypeStruct((lay.b, nc, N_PACK, GC, BRW), F32)] * 2
        + [jax.ShapeDtypeStruct((lay.b, nc, 1, BRW), F32)] * 2,
        scratch_shapes=[pltpu.VMEM((2 * lay.b, BRW, BRW), F32)],
        compiler_params=_params(("arbitrary",)), name="gdn_scan_bwd")(pack, pack, cd, cd, *states, do, do)


def _gdn_finish_fn(o, z, ng):
    return o * lax.rsqrt(_head_sum(o * o) * (1.0 / HD) + EPS) * ng * _silu(z)


def _finish_chunks(lay):
    nc = lay.s // GC
    return 12 if nc % 12 == 0 else (6 if nc % 6 == 0 else 2)


def _gdn_o(of_ref, or_ref):
    return (of_ref[0] + or_ref[0]).reshape(of_ref.shape[1] * GC, BRW)


def _gdn_finish_specs(lay):
    per = _finish_chunks(lay)
    ob = pl.BlockSpec((1, per, GC, BRW), lambda b, i: (b, i, 0, 0))
    return [ob, ob, pl.BlockSpec((1, per * GC, BRW), lambda b, i: (b, i, COL_GDN + 3)),
            pl.BlockSpec((1, BRW), lambda b, i: (0, 0))]


def _gdn_finish(lay, o_f, o_r, p3, ng):
    rows = _finish_chunks(lay) * GC

    def body(of_ref, or_ref, z_ref, ng_ref, y_ref):
        y_ref[0] = _gdn_finish_fn(_gdn_o(of_ref, or_ref), z_ref[0], ng_ref[...]).astype(BF16)

    return pl.pallas_call(
        body, grid=(lay.b, lay.s // rows), in_specs=_gdn_finish_specs(lay),
        out_specs=pl.BlockSpec((1, rows, BRW), lambda b, i: (b, i, 0)),
        out_shape=jax.ShapeDtypeStruct((lay.b, lay.s, BRW), BF16),
        compiler_params=_params(("arbitrary", "arbitrary")), name="gdn_finish")(o_f, o_r, p3, ng)


def _gdn_finish_bwd(lay, o_f, o_r, p3, ng, dy):
    rows = _finish_chunks(lay) * GC

    def body(of_ref, or_ref, z_ref, ng_ref, dy_ref, do_ref, dz_ref, dng_ref):
        first = (pl.program_id(0) == 0) & (pl.program_id(1) == 0)
        _, vjp = jax.vjp(_gdn_finish_fn, _gdn_o(of_ref, or_ref), z_ref[0], ng_ref[...])
        do, dz, dng = vjp(dy_ref[0])
        do_ref[0] = do
        dz_ref[0] = dz.astype(BF16)
        _acc(dng_ref, dng, first)

    blk = pl.BlockSpec((1, rows, BRW), lambda b, i: (b, i, 0))
    return pl.pallas_call(
        body, grid=(lay.b, lay.s // rows), in_specs=_gdn_finish_specs(lay) + [blk],
        out_specs=[blk, blk, pl.BlockSpec((1, BRW), lambda b, i: (0, 0))],
        out_shape=[jax.ShapeDtypeStruct((lay.b, lay.s, BRW), F32), jax.ShapeDtypeStruct((lay.b, lay.s, BRW), BF16),
                   jax.ShapeDtypeStruct((1, BRW), F32)],
        compiler_params=_params(("arbitrary", "arbitrary")), name="gdn_finish_bwd")(o_f, o_r, p3, ng, dy)


def _rope_tables(lay):
    t = jnp.arange(lay.t_lat)
    lane = np.arange(BRW)
    dim = lane % HD
    inv = jnp.asarray(ROPE_BASE ** (-(dim % 16).astype(np.float32) / 16.0), F32)
    pos = jnp.where((dim // 32 == 0)[None, :], (t // GRID_W)[:, None], (t % GRID_W)[:, None]).astype(F32)
    ang = pos * inv[None, :]
    cos = jnp.concatenate([jnp.ones((lay.t_ctx, BRW), F32), jnp.cos(ang)], axis=0)
    sin = jnp.concatenate([jnp.zeros((lay.t_ctx, BRW), F32), jnp.sin(ang)], axis=0)
    return cos, sin


def _pad_rows(a, rows):
    return jnp.concatenate([a, jnp.zeros((rows - a.shape[0],) + a.shape[1:], a.dtype)], axis=0)


def _layer_fwd(lay, xc, wl, cos, sin):
    p, ht = _inproj_fwd(lay, xc, wl["mod3"], wl["gpre"], wl["win"])
    p3 = p.reshape(lay.b, lay.s, W_PAD)
    nctx = lay.t_ctx // RC
    states = _state_scan(lay, _ret_state(lay, p3, cos, sin), nctx, False, "ret_scan")
    y_ret = _ret_out(lay, p3, cos, sin, states, wl["ret_ng"])
    y_sg = _sg_fwd(lay, p3, wl["sgw"], wl["sgb8"])
    y_sc = _sc_fwd(lay, p3, wl["scw8"])
    qkv = _gdn_conv(lay, p3, wl["gdnw8"])
    pack, cd, t_inv = _gdn_prep(lay, qkv, p3, wl["prm"])
    o_f, o_r, st_f, st_r = _gdn_scan(lay, pack, cd)
    y_gdn = _gdn_finish(lay, o_f, o_r, p3, wl["gdn_ng"])
    ys = [y.reshape(lay.rows, BRW) for y in (y_ret, y_sg, y_sc, y_gdn)]
    xc_new, yt = _outproj_fwd(lay, ys, xc, wl["wout"], wl["gpost"], wl["mod3"])
    saved = dict(xc=xc, p3=p3, ht=ht, yt=yt, states=states, qkv=qkv, pack=pack, cd=cd, t_inv=t_inv, o_f=o_f, o_r=o_r, gstates=(st_f, st_r),
                 ys=ys)
    return xc_new, saved


def _layer_bwd(lay, sv, wl, cos, sin, dxc):
    p3 = sv["p3"]
    as3 = lambda a: a.reshape(lay.b, lay.s, a.shape[-1])
    as2 = lambda a: a.reshape(lay.rows, a.shape[-1])
    dy_ret, dy_sg, dy_sc, dy_gdn, do_, dgpost, dgate = _outproj_bwd(
        lay, sv["ys"], sv["xc"], wl["wout"], wl["gpost"], wl["mod3"], dxc)
    dwout = _weight_grad(lay, sv["yt"], do_, "wout_grad")
    nctx = lay.t_ctx // RC
    dpr, dstates, dret_ng = _ret_out_bwd(lay, p3, cos, sin, sv["states"], wl["ret_ng"], as3(dy_ret))
    d_a = _state_scan(lay, dstates, nctx, True, "ret_scan_bwd")
    dp_ret = _ret_state_bwd(lay, p3, cos, sin, d_a, dpr)
    dp_sg, dsgw, dsgb8 = _sg_bwd(lay, p3, wl["sgw"], wl["sgb8"], as3(dy_sg))
    dsb, dsc_, dsh_, dsz, dscw8 = _sc_bwd(lay, p3, wl["scw8"], as3(dy_sc))
    do, dgz, dgdn_ng = _gdn_finish_bwd(lay, sv["o_f"], sv["o_r"], p3, wl["gdn_ng"], as3(dy_gdn))
    dpf, dpr_, dcf, dcr = _gdn_scan_bwd(lay, sv["pack"], sv["cd"], sv["gstates"], do)
    dqkv, da, dprm = _gdn_prep_bwd(lay, sv["qkv"], p3, wl["prm"], (dpf, dpr_), (dcf, dcr), sv["t_inv"])
    dp_gqkv, dgdnw8 = _gdn_conv_bwd(lay, p3, wl["gdnw8"], dqkv)
    pieces = [(as2(dp_ret), 0), (as2(dp_sg), COL_SG * BRW), (as2(dsb), COL_SC * BRW), (as2(dsc_), (COL_SC + 1) * BRW),
              (as2(dsh_), (COL_SC + 2) * BRW), (as2(dsz), (COL_SC + 3) * BRW), (as2(dp_gqkv), COL_GDN * BRW),
              (as2(dgz), (COL_GDN + 3) * BRW), (as2(da), COL_A128 * LANES)]
    dxc_prev, dgpre, dshift, dscale = _inproj_bwd(lay, sv["xc"], wl["mod3"], wl["gpre"], wl["wint"], dxc, pieces)
    dws = [_weight_grad(lay, sv["ht"], dp, "win_grad_%d" % off) for dp, off in pieces]
    dwin = jnp.concatenate(dws[:-1] + [dws[-1][:, :W_IN - COL_A128 * LANES]], axis=1)

    def rows3(g):
        return jnp.concatenate([g[1], g[3], g[0] + g[2]], axis=0)

    dmod = _pad_rows(jnp.concatenate([rows3(dshift), rows3(dscale), rows3(dgate)], axis=1), 8)
    grads = dict(win=dwin, wout=dwout, gpre=dgpre[0], gpost=dgpost[0], ret_ng=dret_ng[0], sgw=dsgw, sgb=dsgb8[:NH],
                 scw=dscw8[:3], gdnw=dgdnw8[:3], alog=dprm[0, :2 * NH].reshape(2, NH),
                 dtb=dprm[1, :2 * NH].reshape(2, NH), gdn_ng=dgdn_ng.reshape(NH, HD).sum(axis=0), dmod=dmod)
    return dxc_prev, grads


def _local_step(x, c, ctx, c_ctx, first, later, token, bmod, gpre, gpost, ret_ng, sgw, sgb, scw, gdnw, alog, dtb,
                gdn_ng, target):
    depth = bmod.shape[0]
    lay = _Lay(x.shape[0], ctx.shape[1], x.shape[1])
    assert lay.b == 2 and lay.t_ctx % RC == 0 and lay.t_lat % RC == 0
    cos, sin = _rope_tables(lay)
    cvec8 = _pad_rows(jnp.concatenate([c, c_ctx[None]], axis=0), 8) + token[0, 0]
    wmod = first[0]
    mod = _mod_fwd(cvec8, wmod, bmod[:1, None, :])
    xc = jnp.concatenate([ctx, x], axis=1).reshape(lay.rows, D)
    layers, saved = [], []
    for l in range(depth):
        if l == 1:
            rest = later(xc)
            wmod = jnp.concatenate([first[0], rest[0]], axis=0)
            mod = jnp.concatenate([mod, _mod_fwd(cvec8, rest[0], bmod[1:, None, :])], axis=0)
        win, wout = (first[1][0], first[2][0]) if l == 0 else (rest[1][l - 1], rest[2][l - 1])
        wl = dict(mod3=mod[l].reshape(8, 3, D).transpose(1, 0, 2)[:, :, None, :], gpre=gpre[l][None], gpost=gpost[l][None],
                  win=win, wint=jnp.swapaxes(win, 0, 1), wout=wout, ret_ng=ret_ng[l][None], sgw=sgw[l],
                  sgb8=_pad_rows(sgb[l], 8), scw8=_pad_rows(scw[l], 8), gdnw8=_pad_rows(gdnw[l], 8),
                  prm=_pad_rows(jnp.pad(jnp.stack([alog[l].reshape(-1), dtb[l].reshape(-1)]),
                                        ((0, 0), (0, LANES - 2 * NH))), 8),
                  gdn_ng=jnp.tile(gdn_ng[l], NH)[None])
        xc, sv = _layer_fwd(lay, xc, wl, cos, sin)
        layers.append(wl)
        saved.append(sv)
    loss, dxc3 = _loss_kernel(lay, xc.reshape(lay.b, lay.s, D), target)
    dxc = dxc3.reshape(lay.rows, D)
    grads = [None] * depth
    for l in reversed(range(depth)):
        dxc, grads[l] = _layer_bwd(lay, saved[l], layers[l], cos, sin, dxc)
    stacked = {k: jnp.stack([g[k] for g in grads]) for k in grads[0] if k not in ("win", "wout")}
    stacked["win"] = [g["win"] for g in grads]
    stacked["wout"] = [g["wout"] for g in grads]
    dcvec8, dbmod = _mod_bwd(stacked["dmod"], wmod, cvec8)
    stacked["bmod"] = dbmod[:, 0, :]
    stacked["c_ctx"] = dcvec8[2]
    dx = dxc.reshape(lay.b, lay.s, D)[:, lay.t_ctx:, :]
    return loss, dx, stacked, cvec8


MESH = pl.DeviceIdType.MESH
ANY = pl.BlockSpec(memory_space=pl.ANY)


def _me():
    return lax.axis_index("x"), lax.axis_index("y"), lax.axis_index("c")


def _gather_weights(shards, fulls, blocks):
    n = len(shards)

    def body(*refs):
        ins, outs = refs[:n], refs[n:2 * n]
        send_sems, recv_sems, loc_sems = refs[2 * n:]
        x, y, c = _me()
        me, sibling = (x, y, c), (x, y, 1 - c)
        chips = [(1 - x, y), (x, 1 - y), (1 - x, 1 - y)]

        def blk(a, dev):
            return blocks[a](outs[a], 4 * dev[0] + 2 * dev[1] + dev[2])

        def copy(a, k, block, to, src=None):
            return pltpu.make_async_remote_copy(
                src_ref=blk(a, block) if src is None else src, dst_ref=blk(a, block), send_sem=send_sems.at[a, k],
                recv_sem=recv_sems.at[a, k], device_id=to, device_id_type=MESH)

        mine = [pltpu.make_async_copy(ins[a], blk(a, me), loc_sems.at[a]) for a in range(n)]
        for cp in mine:
            cp.start()
        first = []
        for a in range(n):
            first.append(copy(a, 0, me, sibling, src=ins[a]))
            first += [copy(a, 1 + j, me, (*chip, c), src=ins[a]) for j, chip in enumerate(chips)]
        for cp in first:
            cp.start()
        passed = []
        for j, chip in enumerate(chips):
            for a in range(n):
                copy(a, 1 + j, (*chip, c), me).wait_recv()
                fwd = copy(a, 4 + j, (*chip, c), sibling)
                fwd.start()
                passed.append(fwd)
        for a in range(n):
            copy(a, 0, sibling, me).wait_recv()
            for j, chip in enumerate(chips):
                copy(a, 4 + j, (*chip, 1 - c), me).wait_recv()
        for cp in first + passed:
            cp.wait_send()
        for cp in mine:
            cp.wait()

    return pl.pallas_call(
        body, in_specs=[ANY] * n, out_specs=[ANY] * n,
        out_shape=[jax.ShapeDtypeStruct(f, s.dtype) for f, s in zip(fulls, shards)],
        scratch_shapes=[pltpu.SemaphoreType.DMA((n, 7)), pltpu.SemaphoreType.DMA((n, 7)),
                        pltpu.SemaphoreType.DMA((n,))],
        name="gather_weights")(*shards)


HBM = pl.BlockSpec(memory_space=pltpu.HBM)
SEM = pl.BlockSpec(memory_space=pltpu.SEMAPHORE)


def _peer(k, x, y, c):
    return (1 - x if k & 4 else x, 1 - y if k & 2 else y, 1 - c if k & 1 else c)


def _whole(ref, j):
    del j
    return ref


def _gather_start(shards, lands, blocks, name, parts=None):
    n = len(shards)
    parts = parts or [_whole] * n

    def body(*refs):
        ins, land = refs[:n], refs[n:2 * n]
        send_sems, recv_sems = refs[2 * n], refs[2 * n + 1]
        token = refs[-1]
        x, y, c = _me()
        me = 4 * x + 2 * y + c
        for a in range(n):
            for k in range(1, N_DEV):
                px, py, pc = _peer(k, x, y, c)
                pltpu.make_async_remote_copy(
                    src_ref=parts[a](ins[a], 4 * px + 2 * py + pc), dst_ref=blocks[a](land[a], me),
                    send_sem=send_sems.at[7 * a + k - 1], recv_sem=recv_sems.at[7 * a + k - 1],
                    device_id=(px, py, pc), device_id_type=MESH).start()
        token[...] = jnp.zeros_like(token)

    args = [pltpu.with_memory_space_constraint(a, pltpu.HBM) for a in list(shards) + list(lands)]
    out = pl.pallas_call(
        body, name=name,
        out_shape=[pltpu.SemaphoreType.DMA((7 * n,)), pltpu.SemaphoreType.DMA((7 * n,))]
        + [pltpu.HBM(a.shape, a.dtype) for a in args] + [jax.ShapeDtypeStruct((8, LANES), F32)],
        in_specs=[HBM] * (2 * n), out_specs=[SEM, SEM] + [HBM] * (2 * n) + [pl.BlockSpec(memory_space=pltpu.VMEM)],
        input_output_aliases={i: 2 + i for i in range(2 * n)},
        compiler_params=pltpu.CompilerParams(has_side_effects=pltpu.SideEffectType.DATAFLOW_SIDE_EFFECTING),
    )(*args)
    return out[0], out[1], out[2:2 + n], out[2 + n:2 + 2 * n], out[-1]


def _gather_wait(started, after, blocks, name, parts=None):
    send_sems, recv_sems, shards, lands, _ = started
    n = len(shards)
    parts = parts or [_whole] * n

    def body(*refs):
        ins, land = refs[:n], refs[n:2 * n]
        send_sems, recv_sems = refs[2 * n], refs[2 * n + 1]
        x, y, c = _me()
        for a in range(n):
            for k in range(1, N_DEV):
                px, py, pc = _peer(k, x, y, c)
                peer = 4 * px + 2 * py + pc
                cp = pltpu.make_async_remote_copy(
                    src_ref=parts[a](ins[a], peer), dst_ref=blocks[a](land[a], peer),
                    send_sem=send_sems.at[7 * a + k - 1], recv_sem=recv_sems.at[7 * a + k - 1],
                    device_id=(px, py, pc), device_id_type=MESH)
                cp.wait_send()
                cp.wait_recv()

    out = pl.pallas_call(
        body, name=name,
        out_shape=[pltpu.HBM(a.shape, a.dtype) for a in list(shards) + list(lands)],
        in_specs=[HBM] * (2 * n) + [SEM, SEM, ANY], out_specs=[HBM] * (2 * n),
        input_output_aliases={i: i for i in range(2 * n)},
        compiler_params=pltpu.CompilerParams(has_side_effects=pltpu.SideEffectType.DATAFLOW_SIDE_EFFECTING),
    )(*shards, *lands, send_sems, recv_sems, after)
    return out[:n], out[n:]


def _scatter_pair(srcs, slabs, slab_shapes):
    n = len(srcs)

    def body(*refs):
        ins, outs = refs[:n], refs[n:2 * n]
        send_sems, recv_sems = refs[2 * n:]
        x, y, c = _me()
        cps = []
        for a in range(n):
            for q in range(4):
                j = 2 * q + (1 - c)
                cps.append(pltpu.make_async_remote_copy(
                    src_ref=slabs[a](ins[a], j), dst_ref=outs[a].at[q], send_sem=send_sems.at[a, q],
                    recv_sem=recv_sems.at[a, q], device_id=(x, y, 1 - c), device_id_type=MESH))
        for cp in cps:
            cp.start()
        for cp in cps:
            cp.wait_recv()
        for cp in cps:
            cp.wait_send()

    return pl.pallas_call(
        body, in_specs=[ANY] * n, out_specs=[ANY] * n,
        out_shape=[jax.ShapeDtypeStruct((4,) + tuple(shp), s.dtype) for shp, s in zip(slab_shapes, srcs)],
        scratch_shapes=[pltpu.SemaphoreType.DMA((n, 4)), pltpu.SemaphoreType.DMA((n, 4))],
        name="scatter_pair")(*srcs)


def _scatter_chips(parts, small):
    n = len(parts)

    def body(*refs):
        ins, small_ref = refs[:n], refs[n]
        outs, all_ref = refs[n + 1:2 * n + 1], refs[2 * n + 1]
        send_sems, recv_sems, g_send, g_recv, loc_sem = refs[2 * n + 2:]
        x, y, c = _me()
        me = 4 * x + 2 * y + c
        chips = [(1 - x, y), (x, 1 - y), (1 - x, 1 - y)]
        cps = []
        for a in range(n):
            for k, (px, py) in enumerate(chips):
                cps.append(pltpu.make_async_remote_copy(
                    src_ref=ins[a].at[2 * px + py], dst_ref=outs[a].at[k], send_sem=send_sems.at[a, k],
                    recv_sem=recv_sems.at[a, k], device_id=(px, py, c), device_id_type=MESH))

        def gather(k, dst_blk, peer_xyz):
            return pltpu.make_async_remote_copy(
                src_ref=small_ref, dst_ref=all_ref.at[dst_blk], send_sem=g_send.at[k], recv_sem=g_recv.at[k],
                device_id=peer_xyz, device_id_type=MESH)

        local = pltpu.make_async_copy(small_ref, all_ref.at[me], loc_sem)
        local.start()
        peers = []
        for k in range(1, N_DEV):
            px = 1 - x if k & 4 else x
            py = 1 - y if k & 2 else y
            pc = 1 - c if k & 1 else c
            peers.append((4 * px + 2 * py + pc, (px, py, pc)))
        sends = [gather(k, me, xyz) for k, (_, xyz) in enumerate(peers)]
        for cp in sends + cps:
            cp.start()
        for k, (peer, xyz) in enumerate(peers):
            gather(k, peer, xyz).wait_recv()
        for cp in cps:
            cp.wait_recv()
        for cp in sends + cps:
            cp.wait_send()
        local.wait()

    return pl.pallas_call(
        body, in_specs=[ANY] * (n + 1), out_specs=[ANY] * (n + 1),
        out_shape=[jax.ShapeDtypeStruct((3,) + p.shape[1:], p.dtype) for p in parts]
        + [jax.ShapeDtypeStruct((N_DEV,) + small.shape, small.dtype)],
        scratch_shapes=[pltpu.SemaphoreType.DMA((n, 3)), pltpu.SemaphoreType.DMA((n, 3)),
                        pltpu.SemaphoreType.DMA((N_DEV - 1,)), pltpu.SemaphoreType.DMA((N_DEV - 1,)),
                        pltpu.SemaphoreType.DMA(())],
        name="scatter_chips")(*parts, small)


def _add_rows(arrs, out_dtype, name):
    shp = arrs[0].shape
    two = [a.reshape(-1, shp[-1]) for a in arrs]
    rows, cols = two[0].shape
    tr = _row_tile(rows, 1024)

    def body(*refs):
        acc = refs[0][...].astype(F32)
        for r in refs[1:-1]:
            acc = acc + r[...].astype(F32)
        refs[-1][...] = acc.astype(out_dtype)

    blk = pl.BlockSpec((tr, cols), lambda i: (i, 0))
    return pl.pallas_call(
        body, grid=(rows // tr,), in_specs=[blk] * len(two), out_specs=blk,
        out_shape=jax.ShapeDtypeStruct((rows, cols), out_dtype),
        compiler_params=_params(("arbitrary",)), name=name)(*two).reshape(shp)


def _row_tile(rows, cap):
    best = 8
    for t in range(8, min(rows, cap) + 1, 8):
        if rows % t == 0:
            best = t
    return best


def _sum_devices(x):
    _, rows, cols = x.shape
    tr = _row_tile(rows, 2048)

    def body(x_ref, o_ref):
        acc = x_ref[0]
        for j in range(1, N_DEV):
            acc = acc + x_ref[j]
        o_ref[...] = acc

    return pl.pallas_call(
        body, grid=(rows // tr,), in_specs=[pl.BlockSpec((N_DEV, tr, cols), lambda i: (0, i, 0))],
        out_specs=pl.BlockSpec((tr, cols), lambda i: (i, 0)), out_shape=jax.ShapeDtypeStruct((rows, cols), F32),
        compiler_params=_params(("arbitrary",)), name="sum_devices")(x)


def _adamw(w, g, m, v, name):
    rows, cols = w.shape
    tr = _row_tile(rows, 512)
    bc1 = 1.0 - ADAM_B1 ** ADAM_STEP
    bc2 = 1.0 - ADAM_B2 ** ADAM_STEP

    def body(w_ref, g_ref, m_ref, v_ref, d_ref, nm_ref, nv_ref):
        g_ = g_ref[...]
        m_ = ADAM_B1 * m_ref[...] + (1.0 - ADAM_B1) * g_
        v_ = ADAM_B2 * v_ref[...] + (1.0 - ADAM_B2) * (g_ * g_)
        d_ref[...] = -ADAM_LR * ((m_ / bc1) / (jnp.sqrt(v_ / bc2) + ADAM_EPS) + ADAM_WD * w_ref[...])
        nm_ref[...] = m_
        nv_ref[...] = v_

    blk = pl.BlockSpec((tr, cols), lambda i: (i, 0))
    return pl.pallas_call(
        body, grid=(rows // tr,), in_specs=[blk] * 4, out_specs=[blk] * 3,
        out_shape=[jax.ShapeDtypeStruct((rows, cols), F32)] * 3,
        compiler_params=_params(("arbitrary",)), name=name)(w, g, m, v)


def _pack_rows(shape):
    return -(-int(np.prod(shape)) // (16 * LANES)) * 16


def _pack(arrs, dtype=F32):
    blocks = []
    for a in arrs:
        flat = a.reshape(-1).astype(dtype)
        rows = _pack_rows(a.shape)
        blocks.append(jnp.pad(flat, (0, rows * LANES - flat.shape[0])).reshape(rows, LANES))
    return jnp.concatenate(blocks, axis=0)


def _unpack(packed, shapes):
    out, off = [], 0
    for s in shapes:
        rows = _pack_rows(s)
        out.append(packed[off:off + rows].reshape(-1)[:int(np.prod(s))].reshape(s))
        off += rows
    return out


SMALL = ("c_ctx", "b_mod", "g_pre", "g_post", "ret_norm_g", "sg_w", "sg_b", "sc_conv_w", "gdn_conv_w", "gdn_a_log",
         "gdn_dt_bias", "gdn_norm_g")
ORDER = ("c_ctx", "w_mod", "b_mod", "g_pre", "g_post", "w_in", "w_out", "ret_norm_g", "sg_w", "sg_b", "sc_conv_w",
         "gdn_conv_w", "gdn_a_log", "gdn_dt_bias", "gdn_norm_g")


def kernel(x, c, ctx, c_ctx, w_mod, b_mod, g_pre, g_post, w_in, w_out, ret_norm_g, sg_w, sg_b, sc_conv_w, gdn_conv_w, gdn_a_log, gdn_dt_bias, gdn_norm_g, loss_target, m_c_ctx, m_w_mod, m_b_mod, m_g_pre, m_g_post, m_w_in, m_w_out, m_ret_norm_g, m_sg_w, m_sg_b, m_sc_conv_w, m_gdn_conv_w, m_gdn_a_log, m_gdn_dt_bias, m_gdn_norm_g, v_c_ctx, v_w_mod, v_b_mod, v_g_pre, v_g_post, v_w_in, v_w_out, v_ret_norm_g, v_sg_w, v_sg_b, v_sc_conv_w, v_gdn_conv_w, v_gdn_a_log, v_gdn_dt_bias, v_gdn_norm_g):
    wts = dict(c_ctx=c_ctx, w_mod=w_mod, b_mod=b_mod, g_pre=g_pre, g_post=g_post, w_in=w_in, w_out=w_out,
               ret_norm_g=ret_norm_g, sg_w=sg_w, sg_b=sg_b, sc_conv_w=sc_conv_w, gdn_conv_w=gdn_conv_w,
               gdn_a_log=gdn_a_log, gdn_dt_bias=gdn_dt_bias, gdn_norm_g=gdn_norm_g)
    mom = dict(c_ctx=m_c_ctx, w_mod=m_w_mod, b_mod=m_b_mod, g_pre=m_g_pre, g_post=m_g_post, w_in=m_w_in, w_out=m_w_out,
               ret_norm_g=m_ret_norm_g, sg_w=m_sg_w, sg_b=m_sg_b, sc_conv_w=m_sc_conv_w, gdn_conv_w=m_gdn_conv_w,
               gdn_a_log=m_gdn_a_log, gdn_dt_bias=m_gdn_dt_bias, gdn_norm_g=m_gdn_norm_g)
    var = dict(c_ctx=v_c_ctx, w_mod=v_w_mod, b_mod=v_b_mod, g_pre=v_g_pre, g_post=v_g_post, w_in=v_w_in, w_out=v_w_out,
               ret_norm_g=v_ret_norm_g, sg_w=v_sg_w, sg_b=v_sg_b, sc_conv_w=v_sc_conv_w, gdn_conv_w=v_gdn_conv_w,
               gdn_a_log=v_gdn_a_log, gdn_dt_bias=v_gdn_dt_bias, gdn_norm_g=v_gdn_norm_g)
    depth = w_mod.shape[0]
    n_mod, n_in, n_out = w_mod.shape[2], w_in.shape[2], w_out.shape[1]
    n_sc, n_gdn = sc_conv_w.shape[2], gdn_conv_w.shape[2]
    xi, yi, ci = _me()
    me = 4 * xi + 2 * yi + ci

    conv = _pack([sc_conv_w, gdn_conv_w])
    n_conv = depth * 3 * n_sc
    rest = depth - 1
    blocks = [lambda r, j: r.at[:, :, pl.ds(pl.multiple_of(j * n_mod, LANES), n_mod)],
              lambda r, j: r.at[j],
              lambda r, j: r.at[:, pl.ds(pl.multiple_of(j * n_out, 16), n_out), :],
              lambda r, j: r.at[j]]

    def in_place(g):
        return jnp.pad(g.transpose(1, 2, 0, 3).reshape(g.shape[1], D, N_DEV * n_in),
                       ((0, 0), (0, 0), (0, W_PAD - N_DEV * n_in)))

    wmod_0, win_g, wout_0, conv_g = _gather_weights(
        [w_mod[:1].astype(BF16), w_in[:1].astype(BF16), w_out[:1].astype(BF16), conv],
        [(1, D, N_DEV * n_mod), (N_DEV, 1, D, n_in), (1, N_DEV * n_out, D), (N_DEV,) + conv.shape], blocks)
    later_shards = [w_mod[1:].astype(BF16), w_in[1:].astype(BF16), w_out[1:].astype(BF16)]
    zero = jnp.zeros((), jnp.int32)
    lands = [lax.dynamic_update_slice(lax.empty((rest, D, N_DEV * n_mod), BF16), later_shards[0],
                                      (zero, zero, me * n_mod)),
             lax.dynamic_update_slice(lax.empty((N_DEV, rest, D, n_in), BF16), later_shards[1][None],
                                      (me, zero, zero, zero)),
             lax.dynamic_update_slice(lax.empty((rest, N_DEV * n_out, D), BF16), later_shards[2],
                                      (zero, me * n_out, zero))]
    started = _gather_start(later_shards, lands, blocks[:3], "gather_start")

    def later(stream):
        wmod_r, win_r, wout_r = _gather_wait(started, stream, blocks[:3], "gather_wait")[1]
        return wmod_r, in_place(win_r), wout_r

    slabs = [lambda r, j: r.at[j], lambda r, j: r.at[:, pl.ds(pl.multiple_of(j * n_out, 16), n_out), :]]

    r_sc = _pack_rows(sc_conv_w.shape)
    scw_f = conv_g[:, :r_sc].reshape(N_DEV, -1)[:, :n_conv]
    scw_f = scw_f.reshape(N_DEV, depth, 3, n_sc).transpose(1, 2, 0, 3).reshape(depth, 3, -1)
    gdnw_f = conv_g[:, r_sc:].reshape(N_DEV, -1)[:, :depth * 3 * n_gdn]
    gdnw_f = gdnw_f.reshape(N_DEV, depth, 3, n_gdn).transpose(1, 2, 0, 3)
    gdnw_f = gdnw_f.reshape(depth, 3, -1)

    loss8, dx, g, cvec8 = _local_step(x, c, ctx, c_ctx, (wmod_0, in_place(win_g), wout_0), later, started[4], b_mod,
                                      g_pre, g_post, ret_norm_g, sg_w, sg_b, scw_f, gdnw_f, gdn_a_log, gdn_dt_bias,
                                      gdn_norm_g, loss_target)

    gin = jnp.stack(g["win"]).astype(BF16).reshape(depth, D, N_DEV, n_in).transpose(2, 0, 1, 3)
    gout = jnp.stack(g["wout"]).astype(BF16)
    got_in, got_out = _scatter_pair([gin, gout], slabs, [(depth, D, n_in), (depth, n_out, D)])
    mine_in = lax.dynamic_index_in_dim(gin.reshape(4, 2, depth, D, n_in), ci, axis=1, keepdims=False)
    mine_out = lax.dynamic_index_in_dim(gout.reshape(depth, 4, 2, n_out, D), ci, axis=2, keepdims=False)
    mine_out = mine_out.transpose(1, 0, 2, 3)
    local_small = dict(c_ctx=g["c_ctx"], b_mod=g["bmod"], g_pre=g["gpre"], g_post=g["gpost"], ret_norm_g=g["ret_ng"],
                       sg_w=g["sgw"], sg_b=g["sgb"], sc_conv_w=g["scw"], gdn_conv_w=g["gdnw"], gdn_a_log=g["alog"],
                       gdn_dt_bias=g["dtb"], gdn_norm_g=g["gdn_ng"])
    to_sum = _pack([loss8[0, :1]] + [local_small[k] for k in SMALL])
    rows_sum = to_sum.shape[0]
    as_is = _pack([cvec8[:3], g["dmod"][:, :3, :]])
    far_in, far_out, everyone = _scatter_chips([_add_rows([mine_in, got_in], BF16, "pair_sum_in"),
                                                _add_rows([mine_out, got_out], BF16, "pair_sum_out")],
                                               jnp.concatenate([to_sum, as_is], axis=0))
    chip = 2 * xi + yi
    own = lambda a: lax.dynamic_index_in_dim(a, chip, axis=0, keepdims=False)
    grad = dict(w_in=_add_rows([own(mine_in), own(got_in), far_in[0], far_in[1], far_in[2]], F32, "grad_sum_in"),
                w_out=_add_rows([own(mine_out), own(got_out), far_out[0], far_out[1], far_out[2]], F32,
                                "grad_sum_out"))

    small_sum = _unpack(_sum_devices(everyone[:, :rows_sum]), [(1,)] + [local_small[k].shape for k in SMALL])
    loss = small_sum[0][0]
    for k, val in zip(SMALL, small_sum[1:]):
        grad[k] = val
    grad["sc_conv_w"] = lax.dynamic_slice_in_dim(grad["sc_conv_w"], me * n_sc, n_sc, axis=2)
    grad["gdn_conv_w"] = lax.dynamic_slice_in_dim(grad["gdn_conv_w"], me * n_gdn, n_gdn, axis=2)
    r_c = _pack_rows((3, D))
    c_all = everyone[:, rows_sum:rows_sum + r_c].reshape(N_DEV, -1)[:, :3 * D].reshape(N_DEV * 3, D)
    dmod_all = everyone[:, rows_sum + r_c:].reshape(N_DEV, -1)[:, :depth * 9 * D]
    dmod_all = dmod_all.reshape(N_DEV, depth, 3, 3 * D).transpose(1, 0, 2, 3)
    dmod_mine = lax.dynamic_slice_in_dim(dmod_all.reshape(depth, N_DEV * 3, 3 * D), me * n_mod, n_mod, axis=2)
    grad["w_mod"] = _wmod_grad(_pad_rows(c_all, 32), jnp.pad(dmod_mine, ((0, 0), (0, 32 - N_DEV * 3), (0, 0))))

    delta, new_m, new_v = {}, {}, {}
    for k in ("w_mod", "w_in", "w_out"):
        shp = wts[k].shape
        two = lambda a: a.reshape(-1, shp[-1])
        res = _adamw(two(wts[k]), two(grad[k]), two(mom[k]), two(var[k]), "adamw_" + k)
        delta[k], new_m[k], new_v[k] = [r.reshape(shp) for r in res]
    res = _adamw(*[_pack([d[k] for k in SMALL]) for d in (wts, grad, mom, var)], "adamw_small")
    for dst, flat in zip((delta, new_m, new_v), res):
        for k, val in zip(SMALL, _unpack(flat, [wts[k].shape for k in SMALL])):
            dst[k] = val
    return (loss, dx, *[grad[k] for k in ORDER], *[delta[k] for k in ORDER], *[new_m[k] for k in ORDER],
            *[new_v[k] for k in ORDER])
```

```python
import functools
import math

import jax
import jax.numpy as jnp
import numpy as np
from jax import lax
from jax.experimental import pallas as pl
from jax.experimental.pallas import tpu as pltpu

F32, BF16 = jnp.float32, jnp.bfloat16
HI = lax.Precision.HIGHEST

N_DEV = 8
D = 1024
DEPTH = 4
BRW = 256
HD = 64
NH = 4
LANES = 128
GRID_W = 64
ROPE_BASE = 10000.0
W_IN = 15 * BRW + 4 * NH
W_PAD = 31 * LANES
RC = 128
GC = 64
EPS = 1e-6
LOG_GAMMA = tuple(math.log(1.0 - 2.0 ** (-5.0 - h)) for h in range(NH))
ADAM_LR, ADAM_B1, ADAM_B2, ADAM_EPS, ADAM_WD, ADAM_STEP = 0.001, 0.9, 0.999, 1e-08, 0.01, 10
VMEM_LIMIT = 56 * 1024 * 1024

COL_RET, COL_SG, COL_SC, COL_GDN = 0, 4, 7, 11
COL_A128 = 30


def _params(sem):
    return pltpu.CompilerParams(dimension_semantics=sem, vmem_limit_bytes=VMEM_LIMIT)


def _bdot(a, b, ca, cb):
    if a.ndim == 3:
        dn = (((ca + 1,), (cb + 1,)), ((0,), (0,)))
    else:
        dn = (((ca,), (cb,)), ((), ()))
    return lax.dot_general(a.astype(BF16), b.astype(BF16), dn, preferred_element_type=F32)


@jax.custom_vjp
def mm(a, b):
    return _bdot(a, b, 1, 0)


mm.defvjp(lambda a, b: (_bdot(a, b, 1, 0), (a, b)),
          lambda r, g: (_bdot(g, r[1], 1, 1), _bdot(r[0], g, 0, 0)))


@jax.custom_vjp
def mm_nt(a, b):
    return _bdot(a, b, 1, 1)


mm_nt.defvjp(lambda a, b: (_bdot(a, b, 1, 1), (a, b)),
             lambda r, g: (_bdot(g, r[1], 1, 0), _bdot(g, r[0], 0, 0)))


@jax.custom_vjp
def mm_tn(a, b):
    return _bdot(a, b, 0, 0)


mm_tn.defvjp(lambda a, b: (_bdot(a, b, 0, 0), (a, b)),
             lambda r, g: (_bdot(r[1], g, 1, 1), _bdot(r[0], g, 1, 0)))


def _dotf(a, b):
    return jnp.dot(a, b, precision=HI, preferred_element_type=F32)


def _iota(shape, dim):
    return lax.broadcasted_iota(jnp.int32, shape, dim)


def _head_mask(h, width=BRW):
    return (_iota((1, width), 1) // HD == h).astype(F32)


def _lane_by_head(vals, width=BRW, lane0=0):
    head = (_iota((1, width), 1) + lane0) // HD
    out = jnp.full((1, width), vals[NH - 1], F32)
    for h in range(NH - 2, -1, -1):
        out = jnp.where(head == h, vals[h], out)
    return out


def _block_diag(n, width):
    return (_iota((n, width), 0) // HD == _iota((n, width), 1) // HD).astype(F32)


@jax.custom_vjp
def _head_sum(x):
    w = x.shape[1]
    ones = _block_diag(w, w).astype(BF16)
    hi = x.astype(BF16)
    lo = (x - hi.astype(F32)).astype(BF16)
    return jnp.dot(hi, ones, preferred_element_type=F32) + jnp.dot(lo, ones, preferred_element_type=F32)


_head_sum.defvjp(lambda x: (_head_sum(x), None), lambda _, g: (_head_sum(g),))


def _silu(x):
    return x * jax.nn.sigmoid(x)


def _stack_heads(x):
    return jnp.concatenate([x * _head_mask(h) for h in range(NH)], axis=-2)


@jax.custom_vjp
def _unstack_heads(x):
    n = x.shape[-2] // NH
    return (x[..., 0:n, :] + x[..., n:2 * n, :]) + (x[..., 2 * n:3 * n, :] + x[..., 3 * n:4 * n, :])


_unstack_heads.defvjp(lambda x: (_unstack_heads(x), None), lambda _, g: (_stack_heads(g),))


@jax.custom_vjp
def _rot_half(x):
    n = x.shape[1]
    first = (_iota(x.shape, 1) % 32) < 16
    return jnp.where(first, -pltpu.roll(x, n - 16, 1), pltpu.roll(x, 16, 1))


_rot_half.defvjp(lambda x: (_rot_half(x), None), lambda _, g: (-_rot_half(g),))


def _rotary(x, cos, sin):
    return x * cos + _rot_half(x) * sin


def _make_shifts(seq, t_ctx):
    def dn_raw(x):
        r = _iota(x.shape, 0)
        return jnp.where((r == 0) | (r == t_ctx), 0.0, pltpu.roll(x, 1, 0))

    def up_raw(x):
        r = _iota(x.shape, 0)
        return jnp.where((r == t_ctx - 1) | (r == seq - 1), 0.0, pltpu.roll(x, seq - 1, 0))

    @jax.custom_vjp
    def dn(x):
        return dn_raw(x)

    @jax.custom_vjp
    def up(x):
        return up_raw(x)

    dn.defvjp(lambda x: (dn_raw(x), None), lambda _, g: (up_raw(g),))
    up.defvjp(lambda x: (up_raw(x), None), lambda _, g: (dn_raw(g),))
    return dn, up


def _conv3(t, w0, w1, w2, dn, up):
    return dn(t) * w0 + t * w1 + up(t) * w2


def _acc(ref, val, first, at=()):
    idx = at + (Ellipsis,)

    @pl.when(first)
    def _():
        ref[idx] = val

    @pl.when(jnp.logical_not(first))
    def _():
        ref[idx] += val


def _mod_fwd(cvec8, wmod, bmod):
    depth = wmod.shape[0]

    def body(c_ref, w_ref, b_ref, o_ref):
        sc = _silu(c_ref[...])
        o_ref[0] = jnp.dot(sc.astype(BF16), w_ref[0], preferred_element_type=F32) + b_ref[0]

    return pl.pallas_call(
        body, grid=(depth, 3),
        in_specs=[pl.BlockSpec((8, D), lambda l, j: (0, 0)),
                  pl.BlockSpec((1, D, D), lambda l, j: (l, 0, j)),
                  pl.BlockSpec((1, 1, D), lambda l, j: (l, 0, j))],
        out_specs=pl.BlockSpec((1, 8, D), lambda l, j: (l, 0, j)),
        out_shape=jax.ShapeDtypeStruct((depth, 8, 3 * D), F32),
        compiler_params=_params(("arbitrary", "arbitrary")), name="mod_fwd")(cvec8, wmod, bmod)


def _mod_bwd(dmod, wmod, cvec8):
    depth = wmod.shape[0]

    def body(dm_ref, w_ref, c_ref, dc_ref, db_ref):
        l, j = pl.program_id(0), pl.program_id(1)
        dm = dm_ref[0]
        db_ref[0] = jnp.sum(dm, axis=0, keepdims=True)
        part = _bdot(dm, w_ref[0], 1, 1)
        _acc(dc_ref, part, (l == 0) & (j == 0))

        @pl.when((l == depth - 1) & (j == 2))
        def _():
            c = c_ref[...]
            s = jax.nn.sigmoid(c)
            dc_ref[...] = dc_ref[...] * (s * (1.0 + c * (1.0 - s)))

    return pl.pallas_call(
        body, grid=(depth, 3),
        in_specs=[pl.BlockSpec((1, 8, D), lambda l, j: (l, 0, j)),
                  pl.BlockSpec((1, D, D), lambda l, j: (l, 0, j)),
                  pl.BlockSpec((8, D), lambda l, j: (0, 0))],
        out_specs=[pl.BlockSpec((8, D), lambda l, j: (0, 0)),
                   pl.BlockSpec((1, 1, D), lambda l, j: (l, 0, j))],
        out_shape=[jax.ShapeDtypeStruct((8, D), F32), jax.ShapeDtypeStruct((depth, 1, 3 * D), F32)],
        compiler_params=_params(("arbitrary", "arbitrary")), name="mod_bwd")(dmod, wmod, cvec8)


def _wmod_grad(c_rows, dmod_cols):
    depth, rows, n = dmod_cols.shape

    def body(c_ref, dm_ref, o_ref):
        sc = _silu(c_ref[...])
        o_ref[0] = lax.dot_general(sc, dm_ref[0], (((0,), (0,)), ((), ())), precision=HI,
                                   preferred_element_type=F32)

    return pl.pallas_call(
        body, grid=(depth,),
        in_specs=[pl.BlockSpec((rows, D), lambda l: (0, 0)), pl.BlockSpec((1, rows, n), lambda l: (l, 0, 0))],
        out_specs=pl.BlockSpec((1, D, n), lambda l: (l, 0, 0)),
        out_shape=jax.ShapeDtypeStruct((depth, D, n), F32),
        compiler_params=_params(("arbitrary",)), name="wmod_grad")(c_rows, dmod_cols)


class _Lay:
    def __init__(self, batch, t_ctx, t_lat):
        self.b, self.t_ctx, self.t_lat = batch, t_ctx, t_lat
        self.s = t_ctx + t_lat
        self.tm = min(256, t_ctx)
        self.tpb = self.s // self.tm
        self.nct = t_ctx // self.tm
        self.ntiles = batch * self.tpb
        self.rows = batch * self.s

    def mod_row(self, i):
        return jnp.where(i % self.tpb < self.nct, 2, i // self.tpb)

    def group(self, i):
        return 2 * (i // self.tpb) + jnp.where(i % self.tpb < self.nct, 0, 1)

    def group_first(self, i):
        return (i % self.tpb == 0) | (i % self.tpb == self.nct)


def _norm_mod(x, g, shift, scale):
    r = lax.rsqrt(jnp.mean(x * x, axis=-1, keepdims=True) + EPS)
    return (x * r * g) * (1.0 + scale) + shift


def _inproj_fwd(lay, xc, mod3, gpre, w):
    tm = lay.tm

    def body(x_ref, sh_ref, sc_ref, g_ref, w_ref, p_ref, ht_ref):
        h = _norm_mod(x_ref[...], g_ref[...], sh_ref[0, 0], sc_ref[0, 0])
        ht_ref[...] = h.T.astype(BF16)
        p_ref[...] = jnp.dot(h.astype(BF16), w_ref[...], preferred_element_type=F32)

    return pl.pallas_call(
        body, grid=(lay.ntiles,),
        in_specs=[pl.BlockSpec((tm, D), lambda i: (i, 0)),
                  pl.BlockSpec((1, 1, 1, D), lambda i: (0, lay.mod_row(i), 0, 0)),
                  pl.BlockSpec((1, 1, 1, D), lambda i: (1, lay.mod_row(i), 0, 0)),
                  pl.BlockSpec((1, D), lambda i: (0, 0)),
                  pl.BlockSpec((D, W_PAD), lambda i: (0, 0))],
        out_specs=[pl.BlockSpec((tm, W_PAD), lambda i: (i, 0)), pl.BlockSpec((D, tm), lambda i: (0, i))],
        out_shape=[jax.ShapeDtypeStruct((lay.rows, W_PAD), F32), jax.ShapeDtypeStruct((D, lay.rows), BF16)],
        compiler_params=_params(("arbitrary",)), name="inproj_fwd")(xc, mod3, mod3, gpre, w)


def _inproj_bwd(lay, xc, mod3, gpre, wt, dxc, pieces):
    tm = lay.tm
    npc = len(pieces)
    offs = [off for _, off in pieces]

    def body(*refs):
        x_ref, sh_ref, sc_ref, g_ref, wt_ref, dx_in = refs[:6]
        dps = refs[6:6 + npc]
        dx_ref, dg_ref, dsh_ref, dsc_ref = refs[6 + npc:]
        i = pl.program_id(0)
        dh = None
        for dp_ref, off in zip(dps, offs):
            wd = dp_ref.shape[1]
            part = jnp.dot(dp_ref[...], wt_ref[off:off + wd, :], preferred_element_type=F32)
            dh = part if dh is None else dh + part
        _, vjp = jax.vjp(_norm_mod, x_ref[...], g_ref[...], sh_ref[0, 0], sc_ref[0, 0])
        dx, dg, dsh, dsc = vjp(dh)
        dx_ref[...] = dx_in[...] + dx
        _acc(dg_ref, dg, i == 0)
        first = lay.group_first(i)
        _acc(dsh_ref, dsh, first, at=(0,))
        _acc(dsc_ref, dsc, first, at=(0,))

    return pl.pallas_call(
        body, grid=(lay.ntiles,),
        in_specs=[pl.BlockSpec((tm, D), lambda i: (i, 0)),
                  pl.BlockSpec((1, 1, 1, D), lambda i: (0, lay.mod_row(i), 0, 0)),
                  pl.BlockSpec((1, 1, 1, D), lambda i: (1, lay.mod_row(i), 0, 0)),
                  pl.BlockSpec((1, D), lambda i: (0, 0)),
                  pl.BlockSpec((W_PAD, D), lambda i: (0, 0)),
                  pl.BlockSpec((tm, D), lambda i: (i, 0))]
        + [pl.BlockSpec((tm, dp.shape[1]), lambda i: (i, 0)) for dp, _ in pieces],
        out_specs=[pl.BlockSpec((tm, D), lambda i: (i, 0)),
                   pl.BlockSpec((1, D), lambda i: (0, 0)),
                   pl.BlockSpec((1, 1, D), lambda i: (lay.group(i), 0, 0)),
                   pl.BlockSpec((1, 1, D), lambda i: (lay.group(i), 0, 0))],
        out_shape=[jax.ShapeDtypeStruct((lay.rows, D), F32), jax.ShapeDtypeStruct((1, D), F32),
                   jax.ShapeDtypeStruct((2 * lay.b, 1, D), F32), jax.ShapeDtypeStruct((2 * lay.b, 1, D), F32)],
        compiler_params=_params(("arbitrary",)), name="inproj_bwd",
    )(xc, mod3, mod3, gpre, wt, dxc, *[dp for dp, _ in pieces])


def _weight_grad(lay, ht, dp, name):
    wd = dp.shape[1]
    tn = wd
    tr = lay.rows if wd <= 256 else (lay.rows // 2 if lay.rows % (2 * 256) == 0 else lay.tm)

    def body(ht_ref, dp_ref, o_ref):
        _acc(o_ref, jnp.dot(ht_ref[...], dp_ref[...], preferred_element_type=F32), pl.program_id(1) == 0)

    return pl.pallas_call(
        body, grid=(wd // tn, lay.rows // tr),
        in_specs=[pl.BlockSpec((D, tr), lambda j, i: (0, i)), pl.BlockSpec((tr, tn), lambda j, i: (i, j))],
        out_specs=pl.BlockSpec((D, tn), lambda j, i: (0, j)),
        out_shape=jax.ShapeDtypeStruct((D, wd), F32),
        compiler_params=_params(("arbitrary", "arbitrary")), name=name)(ht, dp)


def _outproj_post(o, x, gpost, gate):
    r = lax.rsqrt(jnp.mean(o * o, axis=-1, keepdims=True) + EPS)
    return x + gate * (o * r * gpost)


def _outproj_matmul(ys, w_ref):
    o = None
    for k, y in enumerate(ys):
        part = jnp.dot(y[...], w_ref[BRW * k:BRW * (k + 1), :], preferred_element_type=F32)
        o = part if o is None else o + part
    return o


def _outproj_specs(lay):
    tm = lay.tm
    return ([pl.BlockSpec((tm, BRW), lambda i: (i, 0))] * 4
            + [pl.BlockSpec((tm, D), lambda i: (i, 0))]
            + [pl.BlockSpec((D, D), lambda i: (0, 0))]
            + [pl.BlockSpec((1, D), lambda i: (0, 0))]
            + [pl.BlockSpec((1, 1, 1, D), lambda i: (2, lay.mod_row(i), 0, 0))])


def _outproj_fwd(lay, ys, xc, wout, gpost, mod3):
    tm = lay.tm

    def body(y0, y1, y2, y3, x_ref, w_ref, g_ref, gt_ref, o_ref, yt_ref):
        ys_ = (y0, y1, y2, y3)
        o_ref[...] = _outproj_post(_outproj_matmul(ys_, w_ref), x_ref[...], g_ref[...], gt_ref[0, 0])
        for k, y in enumerate(ys_):
            yt_ref[BRW * k:BRW * (k + 1), :] = y[...].astype(F32).T.astype(BF16)

    return pl.pallas_call(
        body, grid=(lay.ntiles,), in_specs=_outproj_specs(lay),
        out_specs=[pl.BlockSpec((tm, D), lambda i: (i, 0)), pl.BlockSpec((D, tm), lambda i: (0, i))],
        out_shape=[jax.ShapeDtypeStruct((lay.rows, D), F32), jax.ShapeDtypeStruct((D, lay.rows), BF16)],
        compiler_params=_params(("arbitrary",)), name="outproj_fwd")(*ys, xc, wout, gpost, mod3)


def _outproj_bwd(lay, ys, xc, wout, gpost, mod3, dxc):
    tm = lay.tm

    def body(y0, y1, y2, y3, x_ref, w_ref, g_ref, gt_ref, dx_ref, d0, d1, d2, d3, do_ref, dg_ref, dgt_ref):
        i = pl.program_id(0)
        o = _outproj_matmul((y0, y1, y2, y3), w_ref)
        _, vjp = jax.vjp(_outproj_post, o, x_ref[...], g_ref[...], gt_ref[0, 0])
        do, _, dg, dgt = vjp(dx_ref[...])
        do = do.astype(BF16)
        do_ref[...] = do
        for k, d in enumerate((d0, d1, d2, d3)):
            d[...] = _bdot(do, w_ref[BRW * k:BRW * (k + 1), :], 1, 1)
        _acc(dg_ref, dg, i == 0)
        _acc(dgt_ref, dgt, lay.group_first(i), at=(0,))

    return pl.pallas_call(
        body, grid=(lay.ntiles,),
        in_specs=_outproj_specs(lay) + [pl.BlockSpec((tm, D), lambda i: (i, 0))],
        out_specs=[pl.BlockSpec((tm, BRW), lambda i: (i, 0))] * 4
        + [pl.BlockSpec((tm, D), lambda i: (i, 0)), pl.BlockSpec((1, D), lambda i: (0, 0)),
           pl.BlockSpec((1, 1, D), lambda i: (lay.group(i), 0, 0))],
        out_shape=[jax.ShapeDtypeStruct((lay.rows, BRW), F32)] * 4
        + [jax.ShapeDtypeStruct((lay.rows, D), BF16), jax.ShapeDtypeStruct((1, D), F32),
           jax.ShapeDtypeStruct((2 * lay.b, 1, D), F32)],
        compiler_params=_params(("arbitrary",)), name="outproj_bwd")(*ys, xc, wout, gpost, mod3, dxc)


def _loss_kernel(lay, xc3, target):
    tm, nct = lay.tm, lay.nct

    def body(x_ref, t_ref, loss_ref, dx_ref):
        b, i = pl.program_id(0), pl.program_id(1)
        lat = i >= nct
        err = x_ref[0] - t_ref[0]
        dx_ref[0] = jnp.where(lat, err * (1.0 / D), 0.0)
        part = jnp.sum(jnp.sum(err * err, axis=1, keepdims=True), axis=0, keepdims=True) * (0.5 / D)
        part = jnp.broadcast_to(jnp.where(lat, part, 0.0), (8, LANES))
        _acc(loss_ref, part, (b == 0) & (i == 0))

    return pl.pallas_call(
        body, grid=(lay.b, lay.tpb),
        in_specs=[pl.BlockSpec((1, tm, D), lambda b, i: (b, i, 0)),
                  pl.BlockSpec((1, tm, D), lambda b, i: (b, jnp.maximum(i - nct, 0), 0))],
        out_specs=[pl.BlockSpec((8, LANES), lambda b, i: (0, 0)), pl.BlockSpec((1, tm, D), lambda b, i: (b, i, 0))],
        out_shape=[jax.ShapeDtypeStruct((8, LANES), F32), jax.ShapeDtypeStruct(xc3.shape, F32)],
        compiler_params=_params(("arbitrary", "arbitrary")), name="loss")(xc3, target)


def _chunk_orders(n_ctx, n_all):
    fwd = list(range(n_all))
    rev = list(range(n_ctx - 1, -1, -1)) + list(range(n_all - 1, n_ctx - 1, -1))
    return fwd, rev


def _ret_state_fn(k, v, cos, sin):
    kt = _rotary(k, cos, sin) * (HD ** -0.5)
    lg = _lane_by_head(LOG_GAMMA)
    j = _iota((RC, 1), 0).astype(F32)
    bd = _block_diag(BRW, BRW)
    af = mm_tn(kt * jnp.exp((RC - 1.0 - j) * lg), v) * bd
    ar = mm_tn(kt * jnp.exp(j * lg), v) * bd
    return af, ar


def _ret_out_fn(q, k, v, z, cos, sin, sf, sr, ng):
    qt = _rotary(q, cos, sin)
    kt = _rotary(k, cos, sin) * (HD ** -0.5)
    diff = (_iota((RC, RC), 0) - _iota((RC, RC), 1)).astype(F32)
    o = None
    for h in range(NH):
        m = _head_mask(h)
        sc = mm_nt(qt * m, kt)
        wgt = sc * jnp.exp(jnp.abs(diff) * LOG_GAMMA[h]) * jnp.where(diff == 0, 2.0, 1.0)
        part = mm(wgt, v * m)
        o = part if o is None else o + part
    lg = _lane_by_head(LOG_GAMMA)
    i = _iota((RC, 1), 0).astype(F32)
    o = o + mm(qt, sf) * jnp.exp((i + 1.0) * lg) + mm(qt, sr) * jnp.exp((RC - i) * lg)
    mu = _head_sum(o) * (1.0 / HD)
    cen = o - mu
    var = _head_sum(cen * cen) * (1.0 / HD)
    return cen * lax.rsqrt(var + EPS) * ng * _silu(z)


def _ret_chunks(lay):
    nc = lay.s // RC
    return 6 if nc % 6 == 0 else (2 if nc % 2 == 0 else 1)


def _ret_specs(lay, cols):
    rows = _ret_chunks(lay) * RC
    return [pl.BlockSpec((1, rows, BRW), functools.partial(lambda b, i, c: (b, i, c), c=COL_RET + c)) for c in cols]


def _ret_state(lay, p3, cos, sin):
    nc, per = lay.s // RC, _ret_chunks(lay)

    def body(k_ref, v_ref, c_ref, s_ref, a_ref):
        for n in range(per):
            rows = pl.ds(RC * n, RC)
            af, ar = _ret_state_fn(k_ref[0, rows, :], v_ref[0, rows, :], c_ref[rows, :], s_ref[rows, :])
            a_ref[0, n, 0] = af
            a_ref[0, n, 1] = ar

    tab = pl.BlockSpec((per * RC, BRW), lambda b, i: (i, 0))
    return pl.pallas_call(
        body, grid=(lay.b, nc // per), in_specs=_ret_specs(lay, (1, 2)) + [tab, tab],
        out_specs=pl.BlockSpec((1, per, 2, BRW, BRW), lambda b, i: (b, i, 0, 0, 0)),
        out_shape=jax.ShapeDtypeStruct((lay.b, nc, 2, BRW, BRW), F32),
        compiler_params=_params(("arbitrary", "arbitrary")), name="ret_state")(p3, p3, cos, sin)


def _ret_state_bwd(lay, p3, cos, sin, d_a, dpr):
    nc, per = lay.s // RC, _ret_chunks(lay)

    def body(k_ref, v_ref, c_ref, s_ref, da_ref, dpr_ref, o_ref):
        for n in range(per):
            rows = pl.ds(RC * n, RC)
            cos_, sin_ = c_ref[rows, :], s_ref[rows, :]
            _, vjp = jax.vjp(lambda k, v: _ret_state_fn(k, v, cos_, sin_), k_ref[0, rows, :], v_ref[0, rows, :])
            dk, dv = vjp((da_ref[0, n, 0], da_ref[0, n, 1]))
            o_ref[0, rows, 0:BRW] = dpr_ref[0, rows, 0:BRW].astype(BF16)
            o_ref[0, rows, BRW:2 * BRW] = (dpr_ref[0, rows, BRW:2 * BRW] + dk).astype(BF16)
            o_ref[0, rows, 2 * BRW:3 * BRW] = (dpr_ref[0, rows, 2 * BRW:3 * BRW] + dv).astype(BF16)
            o_ref[0, rows, 3 * BRW:] = dpr_ref[0, rows, 3 * BRW:].astype(BF16)

    tab = pl.BlockSpec((per * RC, BRW), lambda b, i: (i, 0))
    return pl.pallas_call(
        body, grid=(lay.b, nc // per),
        in_specs=_ret_specs(lay, (1, 2)) + [tab, tab,
                                            pl.BlockSpec((1, per, 2, BRW, BRW), lambda b, i: (b, i, 0, 0, 0)),
                                            pl.BlockSpec((1, per * RC, 4 * BRW), lambda b, i: (b, i, 0))],
        out_specs=pl.BlockSpec((1, per * RC, 4 * BRW), lambda b, i: (b, i, 0)),
        out_shape=jax.ShapeDtypeStruct((lay.b, lay.s, 4 * BRW), BF16),
        compiler_params=_params(("arbitrary", "arbitrary")), name="ret_state_bwd")(p3, p3, cos, sin, d_a, dpr)


def _state_scan(lay, a, nc_ctx, transpose, name):
    b, nc = a.shape[0], a.shape[1]
    orders = _chunk_orders(nc_ctx, nc)

    def body(a_ref, o_ref):
        d, jh = pl.program_id(1), pl.program_id(2)
        head = (_iota((1, LANES), 1) + jh * LANES) // HD
        lg = jnp.full((1, LANES), LOG_GAMMA[NH - 1], F32)
        for h in range(NH - 2, -1, -1):
            lg = jnp.where(head == h, LOG_GAMMA[h], lg)
        dec = jnp.exp(RC * lg)
        for dd in (0, 1):
            @pl.when(d == dd)
            def _(order=orders[dd]):
                acc = jnp.zeros((BRW, LANES), F32)
                if not transpose:
                    for c in order:
                        o_ref[0, c, 0] = acc
                        acc = acc * dec + a_ref[0, c, 0]
                else:
                    for c in reversed(order):
                        o_ref[0, c, 0] = acc
                        acc = a_ref[0, c, 0] + acc * dec

    spec = pl.BlockSpec((1, nc, 1, BRW, LANES), lambda bb, d, jh: (bb, 0, d, 0, jh))
    return pl.pallas_call(
        body, grid=(b, 2, BRW // LANES), in_specs=[spec], out_specs=spec,
        out_shape=jax.ShapeDtypeStruct(a.shape, F32),
        compiler_params=_params(("arbitrary",) * 3), name=name)(a)


def _ret_out(lay, p3, cos, sin, states, ng):
    nc, per = lay.s // RC, _ret_chunks(lay)

    def body(q_ref, k_ref, v_ref, z_ref, c_ref, s_ref, st_ref, ng_ref, y_ref):
        for n in range(per):
            rows = pl.ds(RC * n, RC)
            y = _ret_out_fn(q_ref[0, rows, :], k_ref[0, rows, :], v_ref[0, rows, :], z_ref[0, rows, :],
                            c_ref[rows, :], s_ref[rows, :], st_ref[0, n, 0], st_ref[0, n, 1], ng_ref[...])
            y_ref[0, rows, :] = y.astype(BF16)

    tab = pl.BlockSpec((per * RC, BRW), lambda b, i: (i, 0))
    return pl.pallas_call(
        body, grid=(lay.b, nc // per),
        in_specs=_ret_specs(lay, (0, 1, 2, 3)) + [tab, tab,
                                                  pl.BlockSpec((1, per, 2, BRW, BRW), lambda b, i: (b, i, 0, 0, 0)),
                                                  pl.BlockSpec((1, BRW), lambda b, i: (0, 0))],
        out_specs=pl.BlockSpec((1, per * RC, BRW), lambda b, i: (b, i, 0)),
        out_shape=jax.ShapeDtypeStruct((lay.b, lay.s, BRW), BF16),
        compiler_params=_params(("arbitrary", "arbitrary")), name="ret_out")(p3, p3, p3, p3, cos, sin, states, ng)


def _ret_out_bwd(lay, p3, cos, sin, states, ng, dy):
    nc, per = lay.s // RC, _ret_chunks(lay)

    def body(q_ref, k_ref, v_ref, z_ref, c_ref, s_ref, st_ref, ng_ref, dy_ref, dp_ref, dst_ref, dng_ref):
        b, i = pl.program_id(0), pl.program_id(1)
        dng_sum = None
        for n in range(per):
            rows = pl.ds(RC * n, RC)
            cos_, sin_ = c_ref[rows, :], s_ref[rows, :]
            fn = lambda q, k, v, z, sf, sr, ng: _ret_out_fn(q, k, v, z, cos_, sin_, sf, sr, ng)
            _, vjp = jax.vjp(fn, q_ref[0, rows, :], k_ref[0, rows, :], v_ref[0, rows, :], z_ref[0, rows, :],
                             st_ref[0, n, 0], st_ref[0, n, 1], ng_ref[...])
            dq, dk, dv, dz, dsf, dsr, dng = vjp(dy_ref[0, rows, :])
            for m, g in enumerate((dq, dk, dv, dz)):
                dp_ref[0, rows, BRW * m:BRW * (m + 1)] = g
            dst_ref[0, n, 0] = dsf
            dst_ref[0, n, 1] = dsr
            dng_sum = dng if dng_sum is None else dng_sum + dng
        _acc(dng_ref, dng_sum, (b == 0) & (i == 0))

    tab = pl.BlockSpec((per * RC, BRW), lambda b, i: (i, 0))
    st = pl.BlockSpec((1, per, 2, BRW, BRW), lambda b, i: (b, i, 0, 0, 0))
    return pl.pallas_call(
        body, grid=(lay.b, nc // per),
        in_specs=_ret_specs(lay, (0, 1, 2, 3)) + [tab, tab, st, pl.BlockSpec((1, BRW), lambda b, i: (0, 0)),
                                                  pl.BlockSpec((1, per * RC, BRW), lambda b, i: (b, i, 0))],
        out_specs=[pl.BlockSpec((1, per * RC, 4 * BRW), lambda b, i: (b, i, 0)), st,
                   pl.BlockSpec((1, BRW), lambda b, i: (0, 0))],
        out_shape=[jax.ShapeDtypeStruct((lay.b, lay.s, 4 * BRW), F32),
                   jax.ShapeDtypeStruct(states.shape, F32), jax.ShapeDtypeStruct((1, BRW), F32)],
        compiler_params=_params(("arbitrary", "arbitrary")), name="ret_out_bwd",
    )(p3, p3, p3, p3, cos, sin, states, ng, dy)


def _sg_fn(u, v, z, w, b8):
    ug = jax.nn.gelu(u)
    vg = jax.nn.gelu(v)
    mu = jnp.mean(vg, axis=-1, keepdims=True)
    cen = vg - mu
    vn = cen * lax.rsqrt(jnp.mean(cen * cen, axis=-1, keepdims=True) + EPS)
    masks = (_iota((NH, 1, BRW), 2) // HD == _iota((NH, 1, BRW), 0)).astype(F32)
    s = jnp.sum(mm(w, vn[None] * masks), axis=0)
    expand = (_iota((8, BRW), 1) // HD == _iota((8, BRW), 0)).astype(F32)
    bias = lax.dot_general(b8, expand, (((0,), (0,)), ((), ())), precision=HI, preferred_element_type=F32)
    return ug * (s + bias) * _silu(z)


def _sg_chunks(lay):
    nc = lay.s // RC
    return 6 if nc % 6 == 0 else (2 if nc % 2 == 0 else 1)


def _sg_specs(lay):
    rows = _sg_chunks(lay) * RC
    return ([pl.BlockSpec((1, rows, BRW), functools.partial(lambda b, i, c: (b, i, c), c=COL_SG + c)) for c in range(3)]
            + [pl.BlockSpec((NH, RC, RC), lambda b, i: (0, 0, 0)), pl.BlockSpec((8, RC), lambda b, i: (0, 0))])


def _sg_fwd(lay, p3, sgw, sgb8):
    per = _sg_chunks(lay)

    def body(u_ref, v_ref, z_ref, w_ref, b_ref, y_ref):
        for k in range(per):
            rows = pl.ds(RC * k, RC)
            y = _sg_fn(u_ref[0, rows, :], v_ref[0, rows, :], z_ref[0, rows, :], w_ref[...], b_ref[...])
            y_ref[0, rows, :] = y.astype(BF16)

    return pl.pallas_call(
        body, grid=(lay.b, lay.s // (per * RC)), in_specs=_sg_specs(lay),
        out_specs=pl.BlockSpec((1, per * RC, BRW), lambda b, i: (b, i, 0)),
        out_shape=jax.ShapeDtypeStruct((lay.b, lay.s, BRW), BF16),
        compiler_params=_params(("arbitrary", "arbitrary")), name="sg_fwd")(p3, p3, p3, sgw, sgb8)


def _sg_bwd(lay, p3, sgw, sgb8, dy):
    per = _sg_chunks(lay)

    def body(u_ref, v_ref, z_ref, w_ref, b_ref, dy_ref, dp_ref, dw_ref, db_ref):
        first = (pl.program_id(0) == 0) & (pl.program_id(1) == 0)
        dw = db = None
        for k in range(per):
            rows = pl.ds(RC * k, RC)
            _, vjp = jax.vjp(_sg_fn, u_ref[0, rows, :], v_ref[0, rows, :], z_ref[0, rows, :], w_ref[...], b_ref[...])
            g = vjp(dy_ref[0, rows, :])
            for n in range(3):
                dp_ref[0, rows, BRW * n:BRW * (n + 1)] = g[n].astype(BF16)
            dw = g[3] if dw is None else dw + g[3]
            db = g[4] if db is None else db + g[4]
        _acc(dw_ref, dw, first)
        _acc(db_ref, db, first)

    return pl.pallas_call(
        body, grid=(lay.b, lay.s // (per * RC)),
        in_specs=_sg_specs(lay) + [pl.BlockSpec((1, per * RC, BRW), lambda b, i: (b, i, 0))],
        out_specs=[pl.BlockSpec((1, per * RC, 3 * BRW), lambda b, i: (b, i, 0)),
                   pl.BlockSpec((NH, RC, RC), lambda b, i: (0, 0, 0)), pl.BlockSpec((8, RC), lambda b, i: (0, 0))],
        out_shape=[jax.ShapeDtypeStruct((lay.b, lay.s, 3 * BRW), BF16),
                   jax.ShapeDtypeStruct((NH, RC, RC), F32), jax.ShapeDtypeStruct((8, RC), F32)],
        compiler_params=_params(("arbitrary", "arbitrary")), name="sg_bwd")(p3, p3, p3, sgw, sgb8, dy)


def _sc_specs(lay):
    first = COL_SC * BRW // LANES
    blk = [pl.BlockSpec((1, lay.s, LANES), functools.partial(lambda j, b, c: (b, 0, c + j), c=first + 2 * n))
           for n in range(4)]
    return blk + [pl.BlockSpec((8, LANES), lambda j, b: (0, j))]


def _sc_fwd(lay, p3, w8):
    dn, up = _make_shifts(lay.s, lay.t_ctx)

    def fn(b_, c_, h_, z_, w0, w1, w2):
        return b_ * _conv3(c_ * h_, w0, w1, w2, dn, up) * _silu(z_)

    def body(b_ref, c_ref, h_ref, z_ref, w_ref, y_ref):
        y = fn(b_ref[0], c_ref[0], h_ref[0], z_ref[0], w_ref[0:1, :], w_ref[1:2, :], w_ref[2:3, :])
        y_ref[0] = y.astype(BF16)

    return pl.pallas_call(
        body, grid=(BRW // LANES, lay.b), in_specs=_sc_specs(lay),
        out_specs=pl.BlockSpec((1, lay.s, LANES), lambda j, b: (b, 0, j)),
        out_shape=jax.ShapeDtypeStruct((lay.b, lay.s, BRW), BF16),
        compiler_params=_params(("arbitrary", "arbitrary")), name="sc_fwd")(p3, p3, p3, p3, w8)


def _sc_bwd(lay, p3, w8, dy):
    dn, up = _make_shifts(lay.s, lay.t_ctx)

    def fn(b_, c_, h_, z_, w0, w1, w2):
        return b_ * _conv3(c_ * h_, w0, w1, w2, dn, up) * _silu(z_)

    def body(b_ref, c_ref, h_ref, z_ref, w_ref, dy_ref, db_ref, dc_ref, dh_ref, dz_ref, dw_ref):
        _, vjp = jax.vjp(fn, b_ref[0], c_ref[0], h_ref[0], z_ref[0], w_ref[0:1, :], w_ref[1:2, :], w_ref[2:3, :])
        g = vjp(dy_ref[0])
        for ref, val in zip((db_ref, dc_ref, dh_ref, dz_ref), g[:4]):
            ref[0] = val.astype(BF16)
        dw = jnp.concatenate([g[4], g[5], g[6], jnp.zeros((5, LANES), F32)], axis=0)
        _acc(dw_ref, dw, pl.program_id(1) == 0)

    out = pl.BlockSpec((1, lay.s, LANES), lambda j, b: (b, 0, j))
    return pl.pallas_call(
        body, grid=(BRW // LANES, lay.b), in_specs=_sc_specs(lay) + [out],
        out_specs=[out] * 4 + [pl.BlockSpec((8, LANES), lambda j, b: (0, j))],
        out_shape=[jax.ShapeDtypeStruct((lay.b, lay.s, BRW), BF16)] * 4 + [jax.ShapeDtypeStruct((8, BRW), F32)],
        compiler_params=_params(("arbitrary", "arbitrary")), name="sc_bwd")(p3, p3, p3, p3, w8, dy)


def _gdn_conv_fn(x, w0, w1, w2, normed, dn, up):
    a = _silu(_conv3(x, w0, w1, w2, dn, up))
    nrm = a * lax.rsqrt(_head_sum(a * a) + EPS)
    return jnp.where(normed, nrm, a)


def _gdn_conv(lay, p3, w8):
    dn, up = _make_shifts(lay.s, lay.t_ctx)
    first = COL_GDN * BRW // LANES

    def body(x_ref, w_ref, o_ref):
        normed = pl.program_id(0) < 2 * BRW // LANES
        o_ref[0] = _gdn_conv_fn(x_ref[0], w_ref[0:1, :], w_ref[1:2, :], w_ref[2:3, :], normed, dn, up)

    return pl.pallas_call(
        body, grid=(3 * BRW // LANES, lay.b),
        in_specs=[pl.BlockSpec((1, lay.s, LANES), lambda j, b: (b, 0, first + j)),
                  pl.BlockSpec((8, LANES), lambda j, b: (0, j))],
        out_specs=pl.BlockSpec((1, lay.s, LANES), lambda j, b: (b, 0, j)),
        out_shape=jax.ShapeDtypeStruct((lay.b, lay.s, 3 * BRW), F32),
        compiler_params=_params(("arbitrary", "arbitrary")), name="gdn_conv")(p3, w8)


def _gdn_conv_bwd(lay, p3, w8, dqkv):
    dn, up = _make_shifts(lay.s, lay.t_ctx)
    first = COL_GDN * BRW // LANES

    def body(x_ref, w_ref, g_ref, dx_ref, dw_ref):
        normed = pl.program_id(0) < 2 * BRW // LANES
        fn = lambda x, w0, w1, w2: _gdn_conv_fn(x, w0, w1, w2, normed, dn, up)
        _, vjp = jax.vjp(fn, x_ref[0], w_ref[0:1, :], w_ref[1:2, :], w_ref[2:3, :])
        g = vjp(g_ref[0])
        dx_ref[0] = g[0].astype(BF16)
        dw = jnp.concatenate([g[1], g[2], g[3], jnp.zeros((5, LANES), F32)], axis=0)
        _acc(dw_ref, dw, pl.program_id(1) == 0)

    blk = pl.BlockSpec((1, lay.s, LANES), lambda j, b: (b, 0, j))
    return pl.pallas_call(
        body, grid=(3 * BRW // LANES, lay.b),
        in_specs=[pl.BlockSpec((1, lay.s, LANES), lambda j, b: (b, 0, first + j)),
                  pl.BlockSpec((8, LANES), lambda j, b: (0, j)), blk],
        out_specs=[blk, pl.BlockSpec((8, LANES), lambda j, b: (0, j))],
        out_shape=[jax.ShapeDtypeStruct((lay.b, lay.s, 3 * BRW), BF16), jax.ShapeDtypeStruct((8, 3 * BRW), F32)],
        compiler_params=_params(("arbitrary", "arbitrary")), name="gdn_conv_bwd")(p3, w8, dqkv)


def _tri_inverse(low):
    i, j = _iota(low.shape, low.ndim - 2), _iota(low.shape, low.ndim - 1) % GC
    t = (i == j).astype(F32)
    s = 1
    while s < GC:
        pair = (i // (2 * s)) == (j // (2 * s))
        off = pair & (((i // s) % 2) != ((j // s) % 2))
        cb = jnp.where(off, low, 0.0)
        t = t - (cb if s == 1 else _bdot(t, _stack_heads(_bdot(cb, _stack_heads(t), 1, 0)), 1, 0))
        s *= 2
    return t


@jax.custom_vjp
def _tri_solve(t, low, r1, r2):
    del low
    return _bdot(t, _stack_heads(r1), 1, 0), _bdot(t, _stack_heads(r2), 1, 0)


def _tri_solve_fwd(t, low, r1, r2):
    del low
    x1, x2 = _bdot(t, _stack_heads(r1), 1, 0), _bdot(t, _stack_heads(r2), 1, 0)
    return (x1, x2), (t, x1, x2)


def _tri_solve_bwd(res, g):
    t, x1, x2 = res
    bd = _block_diag(BRW, BRW)
    d1 = _unstack_heads(_bdot(t, g[0], 0, 0) * bd)
    d2 = _unstack_heads(_bdot(t, g[1], 0, 0) * bd)
    dlow = -(_bdot(d1, _stack_heads(x1), 1, 1) + _bdot(d2, _stack_heads(x2), 1, 1))
    return jnp.zeros_like(t), dlow, d1, d2


_tri_solve.defvjp(_tri_solve_fwd, _tri_solve_bwd)

N_PACK = 5


def _gdn_prep_fn(qn, kn, vv, a, alog, dtb, t=None):
    n = qn.shape[0]
    col = _iota((1, 1, LANES), 2)
    xx = a + dtb
    softplus = jnp.maximum(xx, 0.0) + jnp.log(1.0 + jnp.exp(-jnp.abs(xx)))
    g_small = jnp.where(col < 8, -jnp.exp(alog) * softplus, 0.0).reshape(n * GC, LANES)
    beta_small = jax.nn.sigmoid(a).reshape(n * GC, LANES)
    sel_col, sel_head = _iota((LANES, BRW), 0), _iota((LANES, BRW), 1) // HD
    g_l, b_l = [], []
    for d in (0, 1):
        g_l.append(_dotf(g_small, (sel_col == 4 * d + sel_head).astype(F32)))
        b_l.append(_dotf(beta_small, (sel_col == 8 + 4 * d + sel_head).astype(F32)))
    g_l = jnp.concatenate(g_l, axis=0).reshape(2 * n, GC, BRW)
    b_l = jnp.concatenate(b_l, axis=0).reshape(2 * n, GC, BRW)
    rev = _iota((2 * n, 1, 1), 0) >= n
    fwd = jnp.logical_not(rev)
    ri, ci = _iota((1, GC, GC), 1), _iota((1, GC, GC), 2)
    tri = ((fwd & (ri >= ci)) | (rev & (ri <= ci))).astype(F32)
    gc_l = lax.dot_general(tri, g_l, (((2,), (1,)), ((0,), (0,))), precision=HI,
                           preferred_element_type=F32)
    gtot_l = jnp.sum(g_l, axis=1, keepdims=True)
    i, j = _iota((1, GC, BRW), 1), _iota((1, GC, BRW), 2) % GC
    gc_t = jnp.sum(jnp.where(i == j, gc_l, 0.0), axis=1, keepdims=True)
    incl = (fwd & (i >= j)) | (rev & (i <= j))
    strict = (fwd & (i > j)) | (rev & (i < j))
    decay = jnp.where(incl, jnp.exp(jnp.where(incl, gc_l - gc_t, 0.0)), 0.0)
    kn2 = jnp.concatenate([kn, kn], axis=0)
    vv2 = jnp.concatenate([vv, vv], axis=0)
    qs = jnp.concatenate([qn, qn], axis=0) * (HD ** -0.5)
    kst = _stack_heads(kn2)
    kb = kn2 * b_l
    low = jnp.where(strict, mm_nt(kb, kst) * decay, 0.0)
    eg = jnp.exp(gc_l)
    t_inv = _tri_inverse(low) if t is None else t
    u, w = _tri_solve(t_inv, low, vv2 * b_l, kb * eg)
    k_tail = kn2 * jnp.exp(gtot_l - gc_l)
    intra = mm_nt(qs, kst) * decay
    return (u, w, k_tail, qs * eg, intra), jnp.exp(gtot_l), t_inv


def _prep_chunks(lay):
    return 4 if (lay.s // GC) % 4 == 0 else 2


def _gdn_prep_specs(lay):
    rows = _prep_chunks(lay) * GC
    return ([pl.BlockSpec((1, rows, BRW), functools.partial(lambda b, i, c: (b, i, c), c=c)) for c in range(3)]
            + [pl.BlockSpec((1, rows, LANES), lambda b, i: (b, i, COL_A128)),
               pl.BlockSpec((8, LANES), lambda b, i: (0, 0))])


def _gdn_prep(lay, qkv, p3, prm):
    nc, per = lay.s // GC, _prep_chunks(lay)

    def body(q_ref, k_ref, v_ref, a_ref, prm_ref, pack_ref, cd_ref, t_ref):
        chunks = lambda ref: ref[0].reshape(per, GC, ref.shape[-1])
        pack, cd, t_inv = _gdn_prep_fn(chunks(q_ref), chunks(k_ref), chunks(v_ref), chunks(a_ref),
                                       prm_ref[0:1, :], prm_ref[1:2, :])
        for d in (0, 1):
            for n in range(N_PACK):
                pack_ref[0, :, d, n] = pack[n][per * d:per * (d + 1)]
            cd_ref[0, :, d] = cd[per * d:per * (d + 1)]
            t_ref[0, :, d] = t_inv[per * d:per * (d + 1)]

    return pl.pallas_call(
        body, grid=(lay.b, nc // per), in_specs=_gdn_prep_specs(lay),
        out_specs=[pl.BlockSpec((1, per, 2, N_PACK, GC, BRW), lambda b, i: (b, i, 0, 0, 0, 0)),
                   pl.BlockSpec((1, per, 2, 1, BRW), lambda b, i: (b, i, 0, 0, 0)),
                   pl.BlockSpec((1, per, 2, GC, BRW), lambda b, i: (b, i, 0, 0, 0))],
        out_shape=[jax.ShapeDtypeStruct((lay.b, nc, 2, N_PACK, GC, BRW), F32),
                   jax.ShapeDtypeStruct((lay.b, nc, 2, 1, BRW), F32),
                   jax.ShapeDtypeStruct((lay.b, nc, 2, GC, BRW), F32)],
        compiler_params=_params(("arbitrary", "arbitrary")), name="gdn_prep")(qkv, qkv, qkv, p3, prm)


def _gdn_prep_bwd(lay, qkv, p3, prm, dpacks, dcds, t_inv):
    nc, per = lay.s // GC, _prep_chunks(lay)

    def body(q_ref, k_ref, v_ref, a_ref, prm_ref, dpf_ref, dpr_ref, dcf_ref, dcr_ref, t_ref, dqkv_ref, da_ref,
             dprm_ref):
        first = (pl.program_id(0) == 0) & (pl.program_id(1) == 0)
        chunks = lambda ref: ref[0].reshape(per, GC, ref.shape[-1])
        t_inv = jnp.concatenate([t_ref[0, :, 0], t_ref[0, :, 1]], axis=0)
        fn = lambda q, k, v, a, alog, dtb: _gdn_prep_fn(q, k, v, a, alog, dtb, t_inv)[:2]
        _, vjp = jax.vjp(fn, chunks(q_ref), chunks(k_ref), chunks(v_ref), chunks(a_ref),
                         prm_ref[0:1, :], prm_ref[1:2, :])
        dpack = tuple(jnp.concatenate([dpf_ref[0, :, n], dpr_ref[0, :, n]], axis=0) for n in range(N_PACK))
        dq, dk, dv, da, dalog, ddtb = vjp((dpack, jnp.concatenate([dcf_ref[0], dcr_ref[0]], axis=0)))
        dqkv_ref[0, :, 0:BRW] = dq.reshape(per * GC, BRW)
        dqkv_ref[0, :, BRW:2 * BRW] = dk.reshape(per * GC, BRW)
        dqkv_ref[0, :, 2 * BRW:] = dv.reshape(per * GC, BRW)
        da_ref[0] = da.reshape(per * GC, LANES).astype(BF16)
        _acc(dprm_ref, jnp.concatenate([dalog, ddtb, jnp.zeros((6, LANES), F32)], axis=0), first)

    rows_blk = per * GC
    return pl.pallas_call(
        body, grid=(lay.b, nc // per),
        in_specs=_gdn_prep_specs(lay)
        + [pl.BlockSpec((1, per, N_PACK, GC, BRW), lambda b, i: (b, i, 0, 0, 0))] * 2
        + [pl.BlockSpec((1, per, 1, BRW), lambda b, i: (b, i, 0, 0))] * 2
        + [pl.BlockSpec((1, per, 2, GC, BRW), lambda b, i: (b, i, 0, 0, 0))],
        out_specs=[pl.BlockSpec((1, rows_blk, 3 * BRW), lambda b, i: (b, i, 0)),
                   pl.BlockSpec((1, rows_blk, LANES), lambda b, i: (b, i, 0)),
                   pl.BlockSpec((8, LANES), lambda b, i: (0, 0))],
        out_shape=[jax.ShapeDtypeStruct((lay.b, lay.s, 3 * BRW), F32),
                   jax.ShapeDtypeStruct((lay.b, lay.s, LANES), BF16), jax.ShapeDtypeStruct((8, LANES), F32)],
        compiler_params=_params(("arbitrary", "arbitrary")), name="gdn_prep_bwd",
    )(qkv, qkv, qkv, p3, prm, *dpacks, *dcds, t_inv)


def _gdn_step_fn(s, u, w, k_tail, qd, intra, cdec):
    v_new = u - mm(w, s)
    o = mm(qd, s) + mm(intra, _stack_heads(v_new))
    return s * cdec + mm_tn(k_tail, v_new) * _block_diag(BRW, BRW), o


def _order_index(nc_ctx, nc, d, step):
    rev = jnp.where(step < nc_ctx, nc_ctx - 1 - step, nc + nc_ctx - 1 - step)
    return jnp.where(d == 0, step, rev)


def _gdn_scan(lay, pack, cd):
    nc, nc_ctx = lay.s // GC, lay.t_ctx // GC
    chunk = functools.partial(_order_index, nc_ctx, nc)

    def body(pf_ref, pr_ref, cf_ref, cr_ref, of_ref, or_ref, sf_ref, sr_ref, s_scr):
        @pl.when(pl.program_id(0) == 0)
        def _():
            s_scr[...] = jnp.zeros_like(s_scr)

        nb = lay.b
        s = s_scr[...]
        st = _unstack_heads(s)
        sf_ref[:, 0] = st[:nb]
        sr_ref[:, 0] = st[nb:]
        args = [jnp.concatenate([pf_ref[:, 0, 0, n], pr_ref[:, 0, 0, n]], axis=0) for n in range(N_PACK)]
        s_new, o = _gdn_step_fn(s, *args, jnp.concatenate([cf_ref[:, 0, 0], cr_ref[:, 0, 0]], axis=0))
        of_ref[:, 0] = o[:nb]
        or_ref[:, 0] = o[nb:]
        s_scr[...] = s_new

    def pk(d):
        return pl.BlockSpec((lay.b, 1, 1, N_PACK, GC, BRW), lambda t: (0, chunk(d, t), d, 0, 0, 0))

    def cdb(d):
        return pl.BlockSpec((lay.b, 1, 1, 1, BRW), lambda t: (0, chunk(d, t), d, 0, 0))

    def out(d):
        return pl.BlockSpec((lay.b, 1, GC, BRW), lambda t: (0, chunk(d, t), 0, 0))

    return pl.pallas_call(
        body, grid=(nc,), in_specs=[pk(0), pk(1), cdb(0), cdb(1)],
        out_specs=[out(0), out(1), out(0), out(1)],
        out_shape=[jax.ShapeDtypeStruct((lay.b, nc, GC, BRW), F32)] * 4,
        scratch_shapes=[pltpu.VMEM((2 * lay.b, BRW, BRW), F32)],
        compiler_params=_params(("arbitrary",)), name="gdn_scan")(pack, pack, cd, cd)


def _gdn_scan_bwd(lay, pack, cd, states, do):
    nc, nc_ctx = lay.s // GC, lay.t_ctx // GC

    def chunk(d, t):
        return _order_index(nc_ctx, nc, d, nc - 1 - t)

    def body(pf_ref, pr_ref, cf_ref, cr_ref, sf_ref, sr_ref, dof_ref, dor_ref, dpf_ref, dpr_ref, dcf_ref, dcr_ref,
             ds_scr):
        @pl.when(pl.program_id(0) == 0)
        def _():
            ds_scr[...] = jnp.zeros_like(ds_scr)

        nb = lay.b
        both = lambda f, r: jnp.concatenate([f, r], axis=0)
        args = ([_stack_heads(both(sf_ref[:, 0], sr_ref[:, 0]))]
                + [both(pf_ref[:, 0, 0, n], pr_ref[:, 0, 0, n]) for n in range(N_PACK)]
                + [both(cf_ref[:, 0, 0], cr_ref[:, 0, 0])])
        _, vjp = jax.vjp(_gdn_step_fn, *args)
        g = vjp((ds_scr[...], both(dof_ref[...], dor_ref[...])))
        ds_scr[...] = g[0]
        for n in range(N_PACK):
            dpf_ref[:, 0, n] = g[1 + n][:nb]
            dpr_ref[:, 0, n] = g[1 + n][nb:]
        dcf_ref[:, 0] = g[1 + N_PACK][:nb]
        dcr_ref[:, 0] = g[1 + N_PACK][nb:]

    def pk(d):
        return pl.BlockSpec((lay.b, 1, 1, N_PACK, GC, BRW), lambda t: (0, chunk(d, t), d, 0, 0, 0))

    def cdb(d):
        return pl.BlockSpec((lay.b, 1, 1, 1, BRW), lambda t: (0, chunk(d, t), d, 0, 0))

    def st(d):
        return pl.BlockSpec((lay.b, 1, GC, BRW), lambda t: (0, chunk(d, t), 0, 0))

    def dob(d):
        return pl.BlockSpec((lay.b, GC, BRW), lambda t: (0, chunk(d, t), 0))

    def dpk(d):
        return pl.BlockSpec((lay.b, 1, N_PACK, GC, BRW), lambda t: (0, chunk(d, t), 0, 0, 0))

    def dcb(d):
        return pl.BlockSpec((lay.b, 1, 1, BRW), lambda t: (0, chunk(d, t), 0, 0))

    return pl.pallas_call(
        body, grid=(nc,),
        in_specs=[pk(0), pk(1), cdb(0), cdb(1), st(0), st(1), dob(0), dob(1)],
        out_specs=[dpk(0), dpk(1), dcb(0), dcb(1)],
        out_shape=[jax.ShapeDtypeStruct((lay.b, nc, N_PACK, GC, BRW), F32)] * 2
        + [jax.ShapeDtypeStruct((lay.b, nc, 1, BRW), F32)] * 2,
        scratch_shapes=[pltpu.VMEM((2 * lay.b, BRW, BRW), F32)],
        compiler_params=_params(("arbitrary",)), name="gdn_scan_bwd")(pack, pack, cd, cd, *states, do, do)


def _gdn_finish_fn(o, z, ng):
    return o * lax.rsqrt(_head_sum(o * o) * (1.0 / HD) + EPS) * ng * _silu(z)


def _finish_chunks(lay):
    nc = lay.s // GC
    return 12 if nc % 12 == 0 else (6 if nc % 6 == 0 else 2)


def _gdn_o(of_ref, or_ref):
    return (of_ref[0] + or_ref[0]).reshape(of_ref.shape[1] * GC, BRW)


def _gdn_finish_specs(lay):
    per = _finish_chunks(lay)
    ob = pl.BlockSpec((1, per, GC, BRW), lambda b, i: (b, i, 0, 0))
    return [ob, ob, pl.BlockSpec((1, per * GC, BRW), lambda b, i: (b, i, COL_GDN + 3)),
            pl.BlockSpec((1, BRW), lambda b, i: (0, 0))]


def _gdn_finish(lay, o_f, o_r, p3, ng):
    rows = _finish_chunks(lay) * GC

    def body(of_ref, or_ref, z_ref, ng_ref, y_ref):
        y_ref[0] = _gdn_finish_fn(_gdn_o(of_ref, or_ref), z_ref[0], ng_ref[...]).astype(BF16)

    return pl.pallas_call(
        body, grid=(lay.b, lay.s // rows), in_specs=_gdn_finish_specs(lay),
        out_specs=pl.BlockSpec((1, rows, BRW), lambda b, i: (b, i, 0)),
        out_shape=jax.ShapeDtypeStruct((lay.b, lay.s, BRW), BF16),
        compiler_params=_params(("arbitrary", "arbitrary")), name="gdn_finish")(o_f, o_r, p3, ng)


def _gdn_finish_bwd(lay, o_f, o_r, p3, ng, dy):
    rows = _finish_chunks(lay) * GC

    def body(of_ref, or_ref, z_ref, ng_ref, dy_ref, do_ref, dz_ref, dng_ref):
        first = (pl.program_id(0) == 0) & (pl.program_id(1) == 0)
        _, vjp = jax.vjp(_gdn_finish_fn, _gdn_o(of_ref, or_ref), z_ref[0], ng_ref[...])
        do, dz, dng = vjp(dy_ref[0])
        do_ref[0] = do
        dz_ref[0] = dz.astype(BF16)
        _acc(dng_ref, dng, first)

    blk = pl.BlockSpec((1, rows, BRW), lambda b, i: (b, i, 0))
    return pl.pallas_call(
        body, grid=(lay.b, lay.s // rows), in_specs=_gdn_finish_specs(lay) + [blk],
        out_specs=[blk, blk, pl.BlockSpec((1, BRW), lambda b, i: (0, 0))],
        out_shape=[jax.ShapeDtypeStruct((lay.b, lay.s, BRW), F32), jax.ShapeDtypeStruct((lay.b, lay.s, BRW), BF16),
                   jax.ShapeDtypeStruct((1, BRW), F32)],
        compiler_params=_params(("arbitrary", "arbitrary")), name="gdn_finish_bwd")(o_f, o_r, p3, ng, dy)


def _rope_tables(lay):
    t = jnp.arange(lay.t_lat)
    lane = np.arange(BRW)
    dim = lane % HD
    inv = jnp.asarray(ROPE_BASE ** (-(dim % 16).astype(np.float32) / 16.0), F32)
    pos = jnp.where((dim // 32 == 0)[None, :], (t // GRID_W)[:, None], (t % GRID_W)[:, None]).astype(F32)
    ang = pos * inv[None, :]
    cos = jnp.concatenate([jnp.ones((lay.t_ctx, BRW), F32), jnp.cos(ang)], axis=0)
    sin = jnp.concatenate([jnp.zeros((lay.t_ctx, BRW), F32), jnp.sin(ang)], axis=0)
    return cos, sin


def _pad_rows(a, rows):
    return jnp.concatenate([a, jnp.zeros((rows - a.shape[0],) + a.shape[1:], a.dtype)], axis=0)


def _layer_fwd(lay, xc, wl, cos, sin):
    p, ht = _inproj_fwd(lay, xc, wl["mod3"], wl["gpre"], wl["win"])
    p3 = p.reshape(lay.b, lay.s, W_PAD)
    nctx = lay.t_ctx // RC
    states = _state_scan(lay, _ret_state(lay, p3, cos, sin), nctx, False, "ret_scan")
    y_ret = _ret_out(lay, p3, cos, sin, states, wl["ret_ng"])
    y_sg = _sg_fwd(lay, p3, wl["sgw"], wl["sgb8"])
    y_sc = _sc_fwd(lay, p3, wl["scw8"])
    qkv = _gdn_conv(lay, p3, wl["gdnw8"])
    pack, cd, t_inv = _gdn_prep(lay, qkv, p3, wl["prm"])
    o_f, o_r, st_f, st_r = _gdn_scan(lay, pack, cd)
    y_gdn = _gdn_finish(lay, o_f, o_r, p3, wl["gdn_ng"])
    ys = [y.reshape(lay.rows, BRW) for y in (y_ret, y_sg, y_sc, y_gdn)]
    xc_new, yt = _outproj_fwd(lay, ys, xc, wl["wout"], wl["gpost"], wl["mod3"])
    saved = dict(xc=xc, p3=p3, ht=ht, yt=yt, states=states, qkv=qkv, pack=pack, cd=cd, t_inv=t_inv, o_f=o_f, o_r=o_r, gstates=(st_f, st_r),
                 ys=ys)
    return xc_new, saved


def _layer_bwd(lay, sv, wl, cos, sin, dxc):
    p3 = sv["p3"]
    as3 = lambda a: a.reshape(lay.b, lay.s, a.shape[-1])
    as2 = lambda a: a.reshape(lay.rows, a.shape[-1])
    dy_ret, dy_sg, dy_sc, dy_gdn, do_, dgpost, dgate = _outproj_bwd(
        lay, sv["ys"], sv["xc"], wl["wout"], wl["gpost"], wl["mod3"], dxc)
    dwout = _weight_grad(lay, sv["yt"], do_, "wout_grad")
    nctx = lay.t_ctx // RC
    dpr, dstates, dret_ng = _ret_out_bwd(lay, p3, cos, sin, sv["states"], wl["ret_ng"], as3(dy_ret))
    d_a = _state_scan(lay, dstates, nctx, True, "ret_scan_bwd")
    dp_ret = _ret_state_bwd(lay, p3, cos, sin, d_a, dpr)
    dp_sg, dsgw, dsgb8 = _sg_bwd(lay, p3, wl["sgw"], wl["sgb8"], as3(dy_sg))
    dsb, dsc_, dsh_, dsz, dscw8 = _sc_bwd(lay, p3, wl["scw8"], as3(dy_sc))
    do, dgz, dgdn_ng = _gdn_finish_bwd(lay, sv["o_f"], sv["o_r"], p3, wl["gdn_ng"], as3(dy_gdn))
    dpf, dpr_, dcf, dcr = _gdn_scan_bwd(lay, sv["pack"], sv["cd"], sv["gstates"], do)
    dqkv, da, dprm = _gdn_prep_bwd(lay, sv["qkv"], p3, wl["prm"], (dpf, dpr_), (dcf, dcr), sv["t_inv"])
    dp_gqkv, dgdnw8 = _gdn_conv_bwd(lay, p3, wl["gdnw8"], dqkv)
    pieces = [(as2(dp_ret), 0), (as2(dp_sg), COL_SG * BRW), (as2(dsb), COL_SC * BRW), (as2(dsc_), (COL_SC + 1) * BRW),
              (as2(dsh_), (COL_SC + 2) * BRW), (as2(dsz), (COL_SC + 3) * BRW), (as2(dp_gqkv), COL_GDN * BRW),
              (as2(dgz), (COL_GDN + 3) * BRW), (as2(da), COL_A128 * LANES)]
    dxc_prev, dgpre, dshift, dscale = _inproj_bwd(lay, sv["xc"], wl["mod3"], wl["gpre"], wl["wint"], dxc, pieces)
    dws = [_weight_grad(lay, sv["ht"], dp, "win_grad_%d" % off) for dp, off in pieces]
    dwin = jnp.concatenate(dws[:-1] + [dws[-1][:, :W_IN - COL_A128 * LANES]], axis=1)

    def rows3(g):
        return jnp.concatenate([g[1], g[3], g[0] + g[2]], axis=0)

    dmod = _pad_rows(jnp.concatenate([rows3(dshift), rows3(dscale), rows3(dgate)], axis=1), 8)
    grads = dict(win=dwin, wout=dwout, gpre=dgpre[0], gpost=dgpost[0], ret_ng=dret_ng[0], sgw=dsgw, sgb=dsgb8[:NH],
                 scw=dscw8[:3], gdnw=dgdnw8[:3], alog=dprm[0, :2 * NH].reshape(2, NH),
                 dtb=dprm[1, :2 * NH].reshape(2, NH), gdn_ng=dgdn_ng.reshape(NH, HD).sum(axis=0), dmod=dmod)
    return dxc_prev, grads


def _local_step(x, c, ctx, c_ctx, first, later, token, bmod, gpre, gpost, ret_ng, sgw, sgb, scw, gdnw, alog, dtb,
                gdn_ng, target):
    depth = bmod.shape[0]
    lay = _Lay(x.shape[0], ctx.shape[1], x.shape[1])
    assert lay.b == 2 and lay.t_ctx % RC == 0 and lay.t_lat % RC == 0
    cos, sin = _rope_tables(lay)
    cvec8 = _pad_rows(jnp.concatenate([c, c_ctx[None]], axis=0), 8) + token[0, 0]
    wmod = first[0]
    mod = _mod_fwd(cvec8, wmod, bmod[:1, None, :])
    xc = jnp.concatenate([ctx, x], axis=1).reshape(lay.rows, D)
    layers, saved = [], []
    for l in range(depth):
        if l == 1:
            rest = later(xc)
            wmod = jnp.concatenate([first[0], rest[0]], axis=0)
            mod = jnp.concatenate([mod, _mod_fwd(cvec8, rest[0], bmod[1:, None, :])], axis=0)
        win, wout = (first[1][0], first[2][0]) if l == 0 else (rest[1][l - 1], rest[2][l - 1])
        wl = dict(mod3=mod[l].reshape(8, 3, D).transpose(1, 0, 2)[:, :, None, :], gpre=gpre[l][None], gpost=gpost[l][None],
                  win=win, wint=jnp.swapaxes(win, 0, 1), wout=wout, ret_ng=ret_ng[l][None], sgw=sgw[l],
                  sgb8=_pad_rows(sgb[l], 8), scw8=_pad_rows(scw[l], 8), gdnw8=_pad_rows(gdnw[l], 8),
                  prm=_pad_rows(jnp.pad(jnp.stack([alog[l].reshape(-1), dtb[l].reshape(-1)]),
                                        ((0, 0), (0, LANES - 2 * NH))), 8),
                  gdn_ng=jnp.tile(gdn_ng[l], NH)[None])
        xc, sv = _layer_fwd(lay, xc, wl, cos, sin)
        layers.append(wl)
        saved.append(sv)
    loss, dxc3 = _loss_kernel(lay, xc.reshape(lay.b, lay.s, D), target)
    dxc = dxc3.reshape(lay.rows, D)
    grads = [None] * depth
    for l in reversed(range(depth)):
        dxc, grads[l] = _layer_bwd(lay, saved[l], layers[l], cos, sin, dxc)
    stacked = {k: jnp.stack([g[k] for g in grads]) for k in grads[0] if k not in ("win", "wout")}
    stacked["win"] = [g["win"] for g in grads]
    stacked["wout"] = [g["wout"] for g in grads]
    dcvec8, dbmod = _mod_bwd(stacked["dmod"], wmod, cvec8)
    stacked["bmod"] = dbmod[:, 0, :]
    stacked["c_ctx"] = dcvec8[2]
    dx = dxc.reshape(lay.b, lay.s, D)[:, lay.t_ctx:, :]
    return loss, dx, stacked, cvec8


MESH = pl.DeviceIdType.MESH
ANY = pl.BlockSpec(memory_space=pl.ANY)


def _me():
    return lax.axis_index("x"), lax.axis_index("y"), lax.axis_index("c")


def _gather_weights(shards, fulls, blocks):
    n = len(shards)

    def body(*refs):
        ins, outs = refs[:n], refs[n:2 * n]
        send_sems, recv_sems, loc_sems = refs[2 * n:]
        x, y, c = _me()
        me, sibling = (x, y, c), (x, y, 1 - c)
        chips = [(1 - x, y), (x, 1 - y), (1 - x, 1 - y)]

        def blk(a, dev):
            return blocks[a](outs[a], 4 * dev[0] + 2 * dev[1] + dev[2])

        def copy(a, k, block, to, src=None):
            return pltpu.make_async_remote_copy(
                src_ref=blk(a, block) if src is None else src, dst_ref=blk(a, block), send_sem=send_sems.at[a, k],
                recv_sem=recv_sems.at[a, k], device_id=to, device_id_type=MESH)

        mine = [pltpu.make_async_copy(ins[a], blk(a, me), loc_sems.at[a]) for a in range(n)]
        for cp in mine:
            cp.start()
        first = []
        for a in range(n):
            first.append(copy(a, 0, me, sibling, src=ins[a]))
            first += [copy(a, 1 + j, me, (*chip, c), src=ins[a]) for j, chip in enumerate(chips)]
        for cp in first:
            cp.start()
        passed = []
        for j, chip in enumerate(chips):
            for a in range(n):
                copy(a, 1 + j, (*chip, c), me).wait_recv()
                fwd = copy(a, 4 + j, (*chip, c), sibling)
                fwd.start()
                passed.append(fwd)
        for a in range(n):
            copy(a, 0, sibling, me).wait_recv()
            for j, chip in enumerate(chips):
                copy(a, 4 + j, (*chip, 1 - c), me).wait_recv()
        for cp in first + passed:
            cp.wait_send()
        for cp in mine:
            cp.wait()

    return pl.pallas_call(
        body, in_specs=[ANY] * n, out_specs=[ANY] * n,
        out_shape=[jax.ShapeDtypeStruct(f, s.dtype) for f, s in zip(fulls, shards)],
        scratch_shapes=[pltpu.SemaphoreType.DMA((n, 7)), pltpu.SemaphoreType.DMA((n, 7)),
                        pltpu.SemaphoreType.DMA((n,))],
        name="gather_weights")(*shards)


HBM = pl.BlockSpec(memory_space=pltpu.HBM)
SEM = pl.BlockSpec(memory_space=pltpu.SEMAPHORE)


def _peer(k, x, y, c):
    return (1 - x if k & 4 else x, 1 - y if k & 2 else y, 1 - c if k & 1 else c)


def _whole(ref, j):
    del j
    return ref


def _gather_start(shards, lands, blocks, name, parts=None):
    n = len(shards)
    parts = parts or [_whole] * n

    def body(*refs):
        ins, land = refs[:n], refs[n:2 * n]
        send_sems, recv_sems = refs[2 * n], refs[2 * n + 1]
        token = refs[-1]
        x, y, c = _me()
        me = 4 * x + 2 * y + c
        for a in range(n):
            for k in range(1, N_DEV):
                px, py, pc = _peer(k, x, y, c)
                pltpu.make_async_remote_copy(
                    src_ref=parts[a](ins[a], 4 * px + 2 * py + pc), dst_ref=blocks[a](land[a], me),
                    send_sem=send_sems.at[7 * a + k - 1], recv_sem=recv_sems.at[7 * a + k - 1],
                    device_id=(px, py, pc), device_id_type=MESH).start()
        token[...] = jnp.zeros_like(token)

    args = [pltpu.with_memory_space_constraint(a, pltpu.HBM) for a in list(shards) + list(lands)]
    out = pl.pallas_call(
        body, name=name,
        out_shape=[pltpu.SemaphoreType.DMA((7 * n,)), pltpu.SemaphoreType.DMA((7 * n,))]
        + [pltpu.HBM(a.shape, a.dtype) for a in args] + [jax.ShapeDtypeStruct((8, LANES), F32)],
        in_specs=[HBM] * (2 * n), out_specs=[SEM, SEM] + [HBM] * (2 * n) + [pl.BlockSpec(memory_space=pltpu.VMEM)],
        input_output_aliases={i: 2 + i for i in range(2 * n)},
        compiler_params=pltpu.CompilerParams(has_side_effects=pltpu.SideEffectType.DATAFLOW_SIDE_EFFECTING),
    )(*args)
    return out[0], out[1], out[2:2 + n], out[2 + n:2 + 2 * n], out[-1]


def _gather_wait(started, after, blocks, name, parts=None):
    send_sems, recv_sems, shards, lands, _ = started
    n = len(shards)
    parts = parts or [_whole] * n

    def body(*refs):
        ins, land = refs[:n], refs[n:2 * n]
        send_sems, recv_sems = refs[2 * n], refs[2 * n + 1]
        x, y, c = _me()
        for a in range(n):
            for k in range(1, N_DEV):
                px, py, pc = _peer(k, x, y, c)
                peer = 4 * px + 2 * py + pc
                cp = pltpu.make_async_remote_copy(
                    src_ref=parts[a](ins[a], peer), dst_ref=blocks[a](land[a], peer),
                    send_sem=send_sems.at[7 * a + k - 1], recv_sem=recv_sems.at[7 * a + k - 1],
                    device_id=(px, py, pc), device_id_type=MESH)
                cp.wait_send()
                cp.wait_recv()

    out = pl.pallas_call(
        body, name=name,
        out_shape=[pltpu.HBM(a.shape, a.dtype) for a in list(shards) + list(lands)],
        in_specs=[HBM] * (2 * n) + [SEM, SEM, ANY], out_specs=[HBM] * (2 * n),
        input_output_aliases={i: i for i in range(2 * n)},
        compiler_params=pltpu.CompilerParams(has_side_effects=pltpu.SideEffectType.DATAFLOW_SIDE_EFFECTING),
    )(*shards, *lands, send_sems, recv_sems, after)
    return out[:n], out[n:]


def _scatter_pair(srcs, slabs, slab_shapes):
    n = len(srcs)

    def body(*refs):
        ins, outs = refs[:n], refs[n:2 * n]
        send_sems, recv_sems = refs[2 * n:]
        x, y, c = _me()
        cps = []
        for a in range(n):
            for q in range(4):
                j = 2 * q + (1 - c)
                cps.append(pltpu.make_async_remote_copy(
                    src_ref=slabs[a](ins[a], j), dst_ref=outs[a].at[q], send_sem=send_sems.at[a, q],
                    recv_sem=recv_sems.at[a, q], device_id=(x, y, 1 - c), device_id_type=MESH))
        for cp in cps:
            cp.start()
        for cp in cps:
            cp.wait_recv()
        for cp in cps:
            cp.wait_send()

    return pl.pallas_call(
        body, in_specs=[ANY] * n, out_specs=[ANY] * n,
        out_shape=[jax.ShapeDtypeStruct((4,) + tuple(shp), s.dtype) for shp, s in zip(slab_shapes, srcs)],
        scratch_shapes=[pltpu.SemaphoreType.DMA((n, 4)), pltpu.SemaphoreType.DMA((n, 4))],
        name="scatter_pair")(*srcs)


def _scatter_chips(parts, small):
    n = len(parts)

    def body(*refs):
        ins, small_ref = refs[:n], refs[n]
        outs, all_ref = refs[n + 1:2 * n + 1], refs[2 * n + 1]
        send_sems, recv_sems, g_send, g_recv, loc_sem = refs[2 * n + 2:]
        x, y, c = _me()
        me = 4 * x + 2 * y + c
        chips = [(1 - x, y), (x, 1 - y), (1 - x, 1 - y)]
        cps = []
        for a in range(n):
            for k, (px, py) in enumerate(chips):
                cps.append(pltpu.make_async_remote_copy(
                    src_ref=ins[a].at[2 * px + py], dst_ref=outs[a].at[k], send_sem=send_sems.at[a, k],
                    recv_sem=recv_sems.at[a, k], device_id=(px, py, c), device_id_type=MESH))

        def gather(k, dst_blk, peer_xyz):
            return pltpu.make_async_remote_copy(
                src_ref=small_ref, dst_ref=all_ref.at[dst_blk], send_sem=g_send.at[k], recv_sem=g_recv.at[k],
                device_id=peer_xyz, device_id_type=MESH)

        local = pltpu.make_async_copy(small_ref, all_ref.at[me], loc_sem)
        local.start()
        peers = []
        for k in range(1, N_DEV):
            px = 1 - x if k & 4 else x
            py = 1 - y if k & 2 else y
            pc = 1 - c if k & 1 else c
            peers.append((4 * px + 2 * py + pc, (px, py, pc)))
        sends = [gather(k, me, xyz) for k, (_, xyz) in enumerate(peers)]
        for cp in sends + cps:
            cp.start()
        for k, (peer, xyz) in enumerate(peers):
            gather(k, peer, xyz).wait_recv()
        for cp in cps:
            cp.wait_recv()
        for cp in sends + cps:
            cp.wait_send()
        local.wait()

    return pl.pallas_call(
        body, in_specs=[ANY] * (n + 1), out_specs=[ANY] * (n + 1),
        out_shape=[jax.ShapeDtypeStruct((3,) + p.shape[1:], p.dtype) for p in parts]
        + [jax.ShapeDtypeStruct((N_DEV,) + small.shape, small.dtype)],
        scratch_shapes=[pltpu.SemaphoreType.DMA((n, 3)), pltpu.SemaphoreType.DMA((n, 3)),
                        pltpu.SemaphoreType.DMA((N_DEV - 1,)), pltpu.SemaphoreType.DMA((N_DEV - 1,)),
                        pltpu.SemaphoreType.DMA(())],
        name="scatter_chips")(*parts, small)


def _add_rows(arrs, out_dtype, name):
    shp = arrs[0].shape
    two = [a.reshape(-1, shp[-1]) for a in arrs]
    rows, cols = two[0].shape
    tr = _row_tile(rows, 1024)

    def body(*refs):
        acc = refs[0][...].astype(F32)
        for r in refs[1:-1]:
            acc = acc + r[...].astype(F32)
        refs[-1][...] = acc.astype(out_dtype)

    blk = pl.BlockSpec((tr, cols), lambda i: (i, 0))
    return pl.pallas_call(
        body, grid=(rows // tr,), in_specs=[blk] * len(two), out_specs=blk,
        out_shape=jax.ShapeDtypeStruct((rows, cols), out_dtype),
        compiler_params=_params(("arbitrary",)), name=name)(*two).reshape(shp)


def _row_tile(rows, cap):
    best = 8
    for t in range(8, min(rows, cap) + 1, 8):
        if rows % t == 0:
            best = t
    return best


def _sum_devices(x):
    _, rows, cols = x.shape
    tr = _row_tile(rows, 2048)

    def body(x_ref, o_ref):
        acc = x_ref[0]
        for j in range(1, N_DEV):
            acc = acc + x_ref[j]
        o_ref[...] = acc

    return pl.pallas_call(
        body, grid=(rows // tr,), in_specs=[pl.BlockSpec((N_DEV, tr, cols), lambda i: (0, i, 0))],
        out_specs=pl.BlockSpec((tr, cols), lambda i: (i, 0)), out_shape=jax.ShapeDtypeStruct((rows, cols), F32),
        compiler_params=_params(("arbitrary",)), name="sum_devices")(x)


def _adamw(w, g, m, v, name):
    rows, cols = w.shape
    tr = _row_tile(rows, 512)
    bc1 = 1.0 - ADAM_B1 ** ADAM_STEP
    bc2 = 1.0 - ADAM_B2 ** ADAM_STEP

    def body(w_ref, g_ref, m_ref, v_ref, d_ref, nm_ref, nv_ref):
        g_ = g_ref[...]
        m_ = ADAM_B1 * m_ref[...] + (1.0 - ADAM_B1) * g_
        v_ = ADAM_B2 * v_ref[...] + (1.0 - ADAM_B2) * (g_ * g_)
        d_ref[...] = -ADAM_LR * ((m_ / bc1) / (jnp.sqrt(v_ / bc2) + ADAM_EPS) + ADAM_WD * w_ref[...])
        nm_ref[...] = m_
        nv_ref[...] = v_

    blk = pl.BlockSpec((tr, cols), lambda i: (i, 0))
    return pl.pallas_call(
        body, grid=(rows // tr,), in_specs=[blk] * 4, out_specs=[blk] * 3,
        out_shape=[jax.ShapeDtypeStruct((rows, cols), F32)] * 3,
        compiler_params=_params(("arbitrary",)), name=name)(w, g, m, v)


def _pack_rows(shape):
    return -(-int(np.prod(shape)) // (16 * LANES)) * 16


def _pack(arrs, dtype=F32):
    blocks = []
    for a in arrs:
        flat = a.reshape(-1).astype(dtype)
        rows = _pack_rows(a.shape)
        blocks.append(jnp.pad(flat, (0, rows * LANES - flat.shape[0])).reshape(rows, LANES))
    return jnp.concatenate(blocks, axis=0)


def _unpack(packed, shapes):
    out, off = [], 0
    for s in shapes:
        rows = _pack_rows(s)
        out.append(packed[off:off + rows].reshape(-1)[:int(np.prod(s))].reshape(s))
        off += rows
    return out


SMALL = ("c_ctx", "b_mod", "g_pre", "g_post", "ret_norm_g", "sg_w", "sg_b", "sc_conv_w", "gdn_conv_w", "gdn_a_log",
         "gdn_dt_bias", "gdn_norm_g")
ORDER = ("c_ctx", "w_mod", "b_mod", "g_pre", "g_post", "w_in", "w_out", "ret_norm_g", "sg_w", "sg_b", "sc_conv_w",
         "gdn_conv_w", "gdn_a_log", "gdn_dt_bias", "gdn_norm_g")


def kernel(x, c, ctx, c_ctx, w_mod, b_mod, g_pre, g_post, w_in, w_out, ret_norm_g, sg_w, sg_b, sc_conv_w, gdn_conv_w, gdn_a_log, gdn_dt_bias, gdn_norm_g, loss_target, m_c_ctx, m_w_mod, m_b_mod, m_g_pre, m_g_post, m_w_in, m_w_out, m_ret_norm_g, m_sg_w, m_sg_b, m_sc_conv_w, m_gdn_conv_w, m_gdn_a_log, m_gdn_dt_bias, m_gdn_norm_g, v_c_ctx, v_w_mod, v_b_mod, v_g_pre, v_g_post, v_w_in, v_w_out, v_ret_norm_g, v_sg_w, v_sg_b, v_sc_conv_w, v_gdn_conv_w, v_gdn_a_log, v_gdn_dt_bias, v_gdn_norm_g):
    wts = dict(c_ctx=c_ctx, w_mod=w_mod, b_mod=b_mod, g_pre=g_pre, g_post=g_post, w_in=w_in, w_out=w_out,
               ret_norm_g=ret_norm_g, sg_w=sg_w, sg_b=sg_b, sc_conv_w=sc_conv_w, gdn_conv_w=gdn_conv_w,
               gdn_a_log=gdn_a_log, gdn_dt_bias=gdn_dt_bias, gdn_norm_g=gdn_norm_g)
    mom = dict(c_ctx=m_c_ctx, w_mod=m_w_mod, b_mod=m_b_mod, g_pre=m_g_pre, g_post=m_g_post, w_in=m_w_in, w_out=m_w_out,
               ret_norm_g=m_ret_norm_g, sg_w=m_sg_w, sg_b=m_sg_b, sc_conv_w=m_sc_conv_w, gdn_conv_w=m_gdn_conv_w,
               gdn_a_log=m_gdn_a_log, gdn_dt_bias=m_gdn_dt_bias, gdn_norm_g=m_gdn_norm_g)
    var = dict(c_ctx=v_c_ctx, w_mod=v_w_mod, b_mod=v_b_mod, g_pre=v_g_pre, g_post=v_g_post, w_in=v_w_in, w_out=v_w_out,
               ret_norm_g=v_ret_norm_g, sg_w=v_sg_w, sg_b=v_sg_b, sc_conv_w=v_sc_conv_w, gdn_conv_w=v_gdn_conv_w,
               gdn_a_log=v_gdn_a_log, gdn_dt_bias=v_gdn_dt_bias, gdn_norm_g=v_gdn_norm_g)
    depth = w_mod.shape[0]
    n_mod, n_in, n_out = w_mod.shape[2], w_in.shape[2], w_out.shape[1]
    n_sc, n_gdn = sc_conv_w.shape[2], gdn_conv_w.shape[2]
    xi, yi, ci = _me()
    me = 4 * xi + 2 * yi + ci

    conv = _pack([sc_conv_w, gdn_conv_w])
    n_conv = depth * 3 * n_sc
    rest = depth - 1
    blocks = [lambda r, j: r.at[:, :, pl.ds(pl.multiple_of(j * n_mod, LANES), n_mod)],
              lambda r, j: r.at[j],
              lambda r, j: r.at[:, pl.ds(pl.multiple_of(j * n_out, 16), n_out), :],
              lambda r, j: r.at[j]]

    def in_place(g):
        return jnp.pad(g.transpose(1, 2, 0, 3).reshape(g.shape[1], D, N_DEV * n_in),
                       ((0, 0), (0, 0), (0, W_PAD - N_DEV * n_in)))

    wmod_0, win_g, wout_0, conv_g = _gather_weights(
        [w_mod[:1].astype(BF16), w_in[:1].astype(BF16), w_out[:1].astype(BF16), conv],
        [(1, D, N_DEV * n_mod), (N_DEV, 1, D, n_in), (1, N_DEV * n_out, D), (N_DEV,) + conv.shape], blocks)
    later_shards = [w_mod[1:].astype(BF16), w_in[1:].astype(BF16), w_out[1:].astype(BF16)]
    zero = jnp.zeros((), jnp.int32)
    lands = [lax.dynamic_update_slice(lax.empty((rest, D, N_DEV * n_mod), BF16), later_shards[0],
                                      (zero, zero, me * n_mod)),
             lax.dynamic_update_slice(lax.empty((N_DEV, rest, D, n_in), BF16), later_shards[1][None],
                                      (me, zero, zero, zero)),
             lax.dynamic_update_slice(lax.empty((rest, N_DEV * n_out, D), BF16), later_shards[2],
                                      (zero, me * n_out, zero))]
    started = _gather_start(later_shards, lands, blocks[:3], "gather_start")

    def later(stream):
        wmod_r, win_r, wout_r = _gather_wait(started, stream, blocks[:3], "gather_wait")[1]
        return wmod_r, in_place(win_r), wout_r

    slabs = [lambda r, j: r.at[j], lambda r, j: r.at[:, pl.ds(pl.multiple_of(j * n_out, 16), n_out), :]]

    r_sc = _pack_rows(sc_conv_w.shape)
    scw_f = conv_g[:, :r_sc].reshape(N_DEV, -1)[:, :n_conv]
    scw_f = scw_f.reshape(N_DEV, depth, 3, n_sc).transpose(1, 2, 0, 3).reshape(depth, 3, -1)
    gdnw_f = conv_g[:, r_sc:].reshape(N_DEV, -1)[:, :depth * 3 * n_gdn]
    gdnw_f = gdnw_f.reshape(N_DEV, depth, 3, n_gdn).transpose(1, 2, 0, 3)
    gdnw_f = gdnw_f.reshape(depth, 3, -1)

    loss8, dx, g, cvec8 = _local_step(x, c, ctx, c_ctx, (wmod_0, in_place(win_g), wout_0), later, started[4], b_mod,
                                      g_pre, g_post, ret_norm_g, sg_w, sg_b, scw_f, gdnw_f, gdn_a_log, gdn_dt_bias,
                                      gdn_norm_g, loss_target)

    gin = jnp.stack(g["win"]).astype(BF16).reshape(depth, D, N_DEV, n_in).transpose(2, 0, 1, 3)
    gout = jnp.stack(g["wout"]).astype(BF16)
    got_in, got_out = _scatter_pair([gin, gout], slabs, [(depth, D, n_in), (depth, n_out, D)])
    mine_in = lax.dynamic_index_in_dim(gin.reshape(4, 2, depth, D, n_in), ci, axis=1, keepdims=False)
    mine_out = lax.dynamic_index_in_dim(gout.reshape(depth, 4, 2, n_out, D), ci, axis=2, keepdims=False)
    mine_out = mine_out.transpose(1, 0, 2, 3)
    local_small = dict(c_ctx=g["c_ctx"], b_mod=g["bmod"], g_pre=g["gpre"], g_post=g["gpost"], ret_norm_g=g["ret_ng"],
                       sg_w=g["sgw"], sg_b=g["sgb"], sc_conv_w=g["scw"], gdn_conv_w=g["gdnw"], gdn_a_log=g["alog"],
                       gdn_dt_bias=g["dtb"], gdn_norm_g=g["gdn_ng"])
    to_sum = _pack([loss8[0, :1]] + [local_small[k] for k in SMALL])
    rows_sum = to_sum.shape[0]
    as_is = _pack([cvec8[:3], g["dmod"][:, :3, :]])
    far_in, far_out, everyone = _scatter_chips([_add_rows([mine_in, got_in], BF16, "pair_sum_in"),
                                                _add_rows([mine_out, got_out], BF16, "pair_sum_out")],
                                               jnp.concatenate([to_sum, as_is], axis=0))
    chip = 2 * xi + yi
    own = lambda a: lax.dynamic_index_in_dim(a, chip, axis=0, keepdims=False)
    grad = dict(w_in=_add_rows([own(mine_in), own(got_in), far_in[0], far_in[1], far_in[2]], F32, "grad_sum_in"),
                w_out=_add_rows([own(mine_out), own(got_out), far_out[0], far_out[1], far_out[2]], F32,
                                "grad_sum_out"))

    small_sum = _unpack(_sum_devices(everyone[:, :rows_sum]), [(1,)] + [local_small[k].shape for k in SMALL])
    loss = small_sum[0][0]
    for k, val in zip(SMALL, small_sum[1:]):
        grad[k] = val
    grad["sc_conv_w"] = lax.dynamic_slice_in_dim(grad["sc_conv_w"], me * n_sc, n_sc, axis=2)
    grad["gdn_conv_w"] = lax.dynamic_slice_in_dim(grad["gdn_conv_w"], me * n_gdn, n_gdn, axis=2)
    r_c = _pack_rows((3, D))
    c_all = everyone[:, rows_sum:rows_sum + r_c].reshape(N_DEV, -1)[:, :3 * D].reshape(N_DEV * 3, D)
    dmod_all = everyone[:, rows_sum + r_c:].reshape(N_DEV, -1)[:, :depth * 9 * D]
    dmod_all = dmod_all.reshape(N_DEV, depth, 3, 3 * D).transpose(1, 0, 2, 3)
    dmod_mine = lax.dynamic_slice_in_dim(dmod_all.reshape(depth, N_DEV * 3, 3 * D), me * n_mod, n_mod, axis=2)
    grad["w_mod"] = _wmod_grad(_pad_rows(c_all, 32), jnp.pad(dmod_mine, ((0, 0), (0, 32 - N_DEV * 3), (0, 0))))

    delta, new_m, new_v = {}, {}, {}
    for k in ("w_mod", "w_in", "w_out"):
        shp = wts[k].shape
        two = lambda a: a.reshape(-1, shp[-1])
        res = _adamw(two(wts[k]), two(grad[k]), two(mom[k]), two(var[k]), "adamw_" + k)
        delta[k], new_m[k], new_v[k] = [r.reshape(shp) for r in res]
    res = _adamw(*[_pack([d[k] for k in SMALL]) for d in (wts, grad, mom, var)], "adamw_small")
    for dst, flat in zip((delta, new_m, new_v), res):
        for k, val in zip(SMALL, _unpack(flat, [wts[k].shape for k in SMALL])):
            dst[k] = val
    return (loss, dx, *[grad[k] for k in ORDER], *[delta[k] for k in ORDER], *[new_m[k] for k in ORDER],
            *[new_v[k] for k in ORDER])
```
